```python
import jax, jax.numpy as jnp
from jax import lax
import numpy as np

D_MODEL = 1024
BATCH = 8
SEQ = 16384
DEPTH = 2

SC_WIDTH = D_MODEL // 4
SC_GROUPS = 4
SC_KERNEL = 3
SB_HEAD_DIM = 64
SB_HEADS = (D_MODEL // 4) // SB_HEAD_DIM
SB_WIDTH = SB_HEADS * SB_HEAD_DIM
SB_BLOCK = 128
SSM_INNER = D_MODEL // 2
SSM_HEAD_DIM = 64
SSM_HEADS = SSM_INNER // SSM_HEAD_DIM
SSM_GROUPS = 2
SSM_STATE = 64
SSM_CONV = 4
SSM_CHUNK = 256
SSM_CONV_DIM = SSM_INNER + 2 * SSM_GROUPS * SSM_STATE
N_BRANCH = 3
FFN_HIDDEN = -(-8 * D_MODEL // (3 * 256)) * 256
NORM_EPS = 1e-6
N_MOD = 6
PROJ_SIZES = (SC_WIDTH, SC_WIDTH, SC_WIDTH,
              SB_WIDTH, SB_WIDTH, SB_WIDTH,
              SSM_INNER, SSM_CONV_DIM, SSM_HEADS,
              D_MODEL, D_MODEL, D_MODEL)
IN_PROJ = sum(PROJ_SIZES)

kernel_name = "hybrid_shortconv_stickbreak_ssd_block"


def rms_norm(x, g):
    x32 = x.astype(jnp.float32)
    y = x32 * lax.rsqrt(jnp.mean(x32 * x32, axis=-1, keepdims=True) + NORM_EPS)
    return (y * g.astype(jnp.float32)).astype(x.dtype)


def causal_depthwise_conv(x, w):
    k = w.shape[0]
    return lax.conv_general_dilated(
        x, w[:, None, :].astype(x.dtype), window_strides=(1,), padding=[(k - 1, 0)],
        dimension_numbers=("NWC", "WIO", "NWC"), feature_group_count=x.shape[-1])


def split_columns(p):
    offsets = [int(o) for o in np.cumsum(PROJ_SIZES)[:-1]]
    return jnp.split(p, offsets, axis=-1)


def short_conv_mixer(b_gate, c_gate, xa, w_conv):
    return b_gate * causal_depthwise_conv(c_gate * xa, w_conv)


def stick_breaking_attention(q, k, v):
    bsz, seq, heads, dh = q.shape
    n_blk = seq // SB_BLOCK
    scale = dh ** -0.5
    qh = q.transpose(0, 2, 1, 3)
    kh = k.transpose(0, 2, 1, 3)
    vh = v.transpose(0, 2, 1, 3)
    strict = jnp.tril(jnp.ones((SB_BLOCK, SB_BLOCK), jnp.float32), -1)
    outs = []
    for i in range(n_blk):
        start, end = i * SB_BLOCK, (i + 1) * SB_BLOCK
        z = jnp.einsum("bhqd,bhkd->bhqk", qh[:, :, start:end], kh[:, :, :end],
                       preferred_element_type=jnp.float32) * scale
        mask = jnp.arange(end)[None, :] < (start + jnp.arange(SB_BLOCK))[:, None]
        log_keep = jnp.where(mask, jax.nn.log_sigmoid(-z), 0.0)
        lk = log_keep.reshape(bsz, heads, SB_BLOCK, i + 1, SB_BLOCK)
        within = jnp.einsum("bhqcj,js->bhqcs", lk, strict)
        blk_tot = jnp.sum(lk, axis=-1)
        later = lax.cumsum(blk_tot, axis=3, reverse=True) - blk_tot
        log_rest = (within + later[..., None]).reshape(bsz, heads, SB_BLOCK, end)
        att = jnp.exp(jnp.where(mask, z + log_keep + log_rest, -jnp.inf))
        outs.append(jnp.einsum("bhqk,bhkd->bhqd", att.astype(vh.dtype), vh[:, :, :end]))
    out = jnp.concatenate(outs, axis=2)
    return out.transpose(0, 2, 1, 3).reshape(bsz, seq, heads * dh)


def segsum_exp(a_cs):
    l = a_cs.shape[-1]
    mask = jnp.tril(jnp.ones((l, l), dtype=bool))
    diff = a_cs[..., :, None] - a_cs[..., None, :]
    return jnp.exp(jnp.where(mask, diff, -jnp.inf))


def ssd_scan(xh, dt, a_neg, bm, cm):
    bsz, seq, heads, hd = xh.shape
    reps = heads // bm.shape[2]
    bh = jnp.repeat(bm, reps, axis=2)
    ch = jnp.repeat(cm, reps, axis=2)
    xdt = xh * dt[..., None]
    a = dt * a_neg
    pad = (-seq) % SSM_CHUNK
    if pad:
        pw = ((0, 0), (0, pad), (0, 0), (0, 0))
        xdt, bh, ch = jnp.pad(xdt, pw), jnp.pad(bh, pw), jnp.pad(ch, pw)
        a = jnp.pad(a, ((0, 0), (0, pad), (0, 0)))
    n_c = (seq + pad) // SSM_CHUNK
    xc = xdt.reshape(bsz, n_c, SSM_CHUNK, heads, hd)
    bc = bh.reshape(bsz, n_c, SSM_CHUNK, heads, -1)
    cc = ch.reshape(bsz, n_c, SSM_CHUNK, heads, -1)
    a_cs = jnp.cumsum(a.reshape(bsz, n_c, SSM_CHUNK, heads), axis=2)
    decay_in = segsum_exp(a_cs.transpose(0, 1, 3, 2))
    scores = jnp.einsum("bclhn,bcshn->bchls", cc, bc) * decay_in
    y_diag = jnp.einsum("bchls,bcshp->bclhp", scores, xc)
    decay_to_end = jnp.exp(a_cs[:, :, -1:, :] - a_cs)
    chunk_states = jnp.einsum("bclhn,bclhp->bchpn", bc * decay_to_end[..., None], xc)
    chunk_decay = jnp.exp(a_cs[:, :, -1, :])

    def step(state, inp):
        s_c, d_c = inp
        return state * d_c[..., None, None] + s_c, state

    init = jnp.zeros((bsz, heads, hd, bc.shape[-1]), jnp.float32)
    _, prev = lax.scan(step, init, (chunk_states.transpose(1, 0, 2, 3, 4),
                                    chunk_decay.transpose(1, 0, 2)))
    prev = prev.transpose(1, 0, 2, 3, 4)
    y_off = jnp.einsum("bclhn,bchpn->bclhp", cc, prev) * jnp.exp(a_cs)[..., None]
    return (y_diag + y_off).reshape(bsz, seq + pad, heads, hd)[:, :seq]


def mamba2_mixer(z, xbc, dt_raw, conv_w, conv_b, dt_bias, a_log, d_skip, norm_w):
    xbc = jax.nn.silu(causal_depthwise_conv(xbc, conv_w) + conv_b)
    xs, bm, cm = jnp.split(xbc, [SSM_INNER, SSM_INNER + SSM_GROUPS * SSM_STATE], axis=-1)
    bsz, seq, _ = xs.shape
    xh = xs.reshape(bsz, seq, SSM_HEADS, SSM_HEAD_DIM).astype(jnp.float32)
    bm = bm.reshape(bsz, seq, SSM_GROUPS, SSM_STATE).astype(jnp.float32)
    cm = cm.reshape(bsz, seq, SSM_GROUPS, SSM_STATE).astype(jnp.float32)
    dt = jax.nn.softplus(dt_raw.astype(jnp.float32) + dt_bias.astype(jnp.float32))
    a_neg = -jnp.exp(a_log.astype(jnp.float32))
    y = ssd_scan(xh, dt, a_neg, bm, cm) + xh * d_skip.astype(jnp.float32)[:, None]
    y = y.reshape(bsz, seq, SSM_INNER) * jax.nn.silu(z.astype(jnp.float32))
    yg = y.reshape(bsz, seq, SSM_GROUPS, SSM_INNER // SSM_GROUPS)
    yg = yg * lax.rsqrt(jnp.mean(yg * yg, axis=-1, keepdims=True) + NORM_EPS)
    return (yg.reshape(bsz, seq, SSM_INNER) * norm_w.astype(jnp.float32)).astype(z.dtype)


def hybrid_layer(x, c, mod_w, mod_b, g_pre_mix, g_post_mix, g_pre_ffn, g_post_ffn,
                 w_in, sc_conv_w, ssm_conv_w, ssm_conv_b, ssm_dt_bias, ssm_a_log, ssm_d,
                 ssm_norm_w, w_sc_out, w_sb_out, w_ssm_out, w_o, w_ffn_in, w_ffn_out):
    bsz, seq, _ = x.shape
    mod = jax.nn.silu(c) @ mod_w + mod_b
    shift1, scale1, gate1, shift2, scale2, gate2 = [m[:, None, :] for m in jnp.split(mod, N_MOD, axis=-1)]

    h = rms_norm(x, g_pre_mix) * (1 + scale1) + shift1
    (sc_b, sc_c, sc_x, q, k, v, z, xbc, dt_raw,
     gl_a, gl_b, gl_c) = split_columns(h @ w_in)
    y_a = short_conv_mixer(sc_b, sc_c, sc_x, sc_conv_w) @ w_sc_out
    y_b = stick_breaking_attention(
        q.reshape(bsz, seq, SB_HEADS, SB_HEAD_DIM),
        k.reshape(bsz, seq, SB_HEADS, SB_HEAD_DIM),
        v.reshape(bsz, seq, SB_HEADS, SB_HEAD_DIM)) @ w_sb_out
    y_c = mamba2_mixer(z, xbc, dt_raw, ssm_conv_w, ssm_conv_b, ssm_dt_bias, ssm_a_log,
                       ssm_d, ssm_norm_w) @ w_ssm_out
    merged = (jax.nn.sigmoid(gl_a) * y_a + jax.nn.sigmoid(gl_b) * y_b
              + jax.nn.sigmoid(gl_c) * y_c)
    mix_out = merged @ w_o
    x = x + (gate1 * rms_norm(mix_out, g_post_mix)).astype(x.dtype)

    h2 = rms_norm(x, g_pre_ffn) * (1 + scale2) + shift2
    gt, up = jnp.split(h2 @ w_ffn_in, 2, axis=-1)
    f = (jax.nn.silu(gt) * up) @ w_ffn_out
    x = x + (gate2 * rms_norm(f, g_post_ffn)).astype(x.dtype)
    return x


def _fwd_setup_inputs(seed: int = 0) -> dict:
    key = jax.random.key(seed)
    ks = jax.random.split(key, 24)
    f32 = jnp.float32

    def nrm(k, shape, fan_in):
        return jax.random.normal(k, shape, f32) * (fan_in ** -0.5)

    def gain(k, shape):
        return 1.0 + 0.05 * jax.random.normal(k, shape, f32)

    dt0 = jnp.exp(jax.random.uniform(ks[14], (DEPTH, SSM_HEADS), f32,
                                     jnp.log(1e-3), jnp.log(1e-1)))
    return {
        "x": jax.random.normal(ks[0], (BATCH, SEQ, D_MODEL), f32),
        "c": jax.random.normal(ks[1], (BATCH, D_MODEL), f32),
        "mod_w": nrm(ks[2], (DEPTH, D_MODEL, N_MOD * D_MODEL), D_MODEL),
        "mod_b": 0.02 * jax.random.normal(ks[3], (DEPTH, N_MOD * D_MODEL), f32),
        "g_pre_mix": gain(ks[4], (DEPTH, D_MODEL)),
        "g_post_mix": gain(ks[5], (DEPTH, D_MODEL)),
        "g_pre_ffn": gain(ks[6], (DEPTH, D_MODEL)),
        "g_post_ffn": gain(ks[7], (DEPTH, D_MODEL)),
        "w_in": nrm(ks[8], (DEPTH, D_MODEL, IN_PROJ), D_MODEL),
        "sc_conv_w": nrm(ks[9], (DEPTH, SC_KERNEL, SC_WIDTH), SC_KERNEL),
        "ssm_conv_w": nrm(ks[10], (DEPTH, SSM_CONV, SSM_CONV_DIM), SSM_CONV),
        "ssm_conv_b": 0.02 * jax.random.normal(ks[11], (DEPTH, SSM_CONV_DIM), f32),
        "ssm_dt_bias": dt0 + jnp.log(-jnp.expm1(-dt0)),
        "ssm_a_log": jnp.log(jax.random.uniform(ks[12], (DEPTH, SSM_HEADS), f32, 1.0, 16.0)),
        "ssm_d": 1.0 + 0.1 * jax.random.normal(ks[13], (DEPTH, SSM_HEADS), f32),
        "ssm_norm_w": gain(ks[15], (DEPTH, SSM_INNER)),
        "w_sc_out": nrm(ks[16], (DEPTH, SC_WIDTH, D_MODEL), SC_WIDTH),
        "w_sb_out": nrm(ks[17], (DEPTH, SB_WIDTH, D_MODEL), SB_WIDTH),
        "w_ssm_out": nrm(ks[18], (DEPTH, SSM_INNER, D_MODEL), SSM_INNER),
        "w_o": nrm(ks[19], (DEPTH, D_MODEL, D_MODEL), D_MODEL),
        "w_ffn_in": nrm(ks[20], (DEPTH, D_MODEL, 2 * FFN_HIDDEN), D_MODEL),
        "w_ffn_out": nrm(ks[21], (DEPTH, FFN_HIDDEN, D_MODEL), FFN_HIDDEN),
    }


def _fwd_reference(x, c, mod_w, mod_b, g_pre_mix, g_post_mix, g_pre_ffn, g_post_ffn, w_in,
              sc_conv_w, ssm_conv_w, ssm_conv_b, ssm_dt_bias, ssm_a_log, ssm_d, ssm_norm_w,
              w_sc_out, w_sb_out, w_ssm_out, w_o, w_ffn_in, w_ffn_out):
    for l in range(DEPTH):
        x = hybrid_layer(x, c, mod_w[l], mod_b[l], g_pre_mix[l], g_post_mix[l],
                         g_pre_ffn[l], g_post_ffn[l], w_in[l], sc_conv_w[l], ssm_conv_w[l],
                         ssm_conv_b[l], ssm_dt_bias[l], ssm_a_log[l], ssm_d[l], ssm_norm_w[l],
                         w_sc_out[l], w_sb_out[l], w_ssm_out[l], w_o[l], w_ffn_in[l],
                         w_ffn_out[l])
    return x


import jax as _jax
import jax.numpy as _jnp

TWIN_FORMAT = 'train_step'
FWD_PARAMS = ['x', 'c', 'mod_w', 'mod_b', 'g_pre_mix', 'g_post_mix', 'g_pre_ffn', 'g_post_ffn', 'w_in', 'sc_conv_w', 'ssm_conv_w', 'ssm_conv_b', 'ssm_dt_bias', 'ssm_a_log', 'ssm_d', 'ssm_norm_w', 'w_sc_out', 'w_sb_out', 'w_ssm_out', 'w_o', 'w_ffn_in', 'w_ffn_out']
TWIN_WEIGHTS = ['mod_w', 'mod_b', 'g_pre_mix', 'g_post_mix', 'g_pre_ffn', 'g_post_ffn', 'w_in', 'sc_conv_w', 'ssm_conv_w', 'ssm_conv_b', 'ssm_dt_bias', 'ssm_a_log', 'ssm_d', 'ssm_norm_w', 'w_sc_out', 'w_sb_out', 'w_ssm_out', 'w_o', 'w_ffn_in', 'w_ffn_out']
TWIN_DIFF_INPUT = 'x'
TWIN_INPUTS = ['x', 'c', 'mod_w', 'mod_b', 'g_pre_mix', 'g_post_mix', 'g_pre_ffn', 'g_post_ffn', 'w_in', 'sc_conv_w', 'ssm_conv_w', 'ssm_conv_b', 'ssm_dt_bias', 'ssm_a_log', 'ssm_d', 'ssm_norm_w', 'w_sc_out', 'w_sb_out', 'w_ssm_out', 'w_o', 'w_ffn_in', 'w_ffn_out', 'loss_target', 'm_mod_w', 'm_mod_b', 'm_g_pre_mix', 'm_g_post_mix', 'm_g_pre_ffn', 'm_g_post_ffn', 'm_w_in', 'm_sc_conv_w', 'm_ssm_conv_w', 'm_ssm_conv_b', 'm_ssm_dt_bias', 'm_ssm_a_log', 'm_ssm_d', 'm_ssm_norm_w', 'm_w_sc_out', 'm_w_sb_out', 'm_w_ssm_out', 'm_w_o', 'm_w_ffn_in', 'm_w_ffn_out', 'v_mod_w', 'v_mod_b', 'v_g_pre_mix', 'v_g_post_mix', 'v_g_pre_ffn', 'v_g_post_ffn', 'v_w_in', 'v_sc_conv_w', 'v_ssm_conv_w', 'v_ssm_conv_b', 'v_ssm_dt_bias', 'v_ssm_a_log', 'v_ssm_d', 'v_ssm_norm_w', 'v_w_sc_out', 'v_w_sb_out', 'v_w_ssm_out', 'v_w_o', 'v_w_ffn_in', 'v_w_ffn_out']
TWIN_OUTPUTS = ['loss', 'grad_x', 'grad_mod_w', 'grad_mod_b', 'grad_g_pre_mix', 'grad_g_post_mix', 'grad_g_pre_ffn', 'grad_g_post_ffn', 'grad_w_in', 'grad_sc_conv_w', 'grad_ssm_conv_w', 'grad_ssm_conv_b', 'grad_ssm_dt_bias', 'grad_ssm_a_log', 'grad_ssm_d', 'grad_ssm_norm_w', 'grad_w_sc_out', 'grad_w_sb_out', 'grad_w_ssm_out', 'grad_w_o', 'grad_w_ffn_in', 'grad_w_ffn_out', 'delta_mod_w', 'delta_mod_b', 'delta_g_pre_mix', 'delta_g_post_mix', 'delta_g_pre_ffn', 'delta_g_post_ffn', 'delta_w_in', 'delta_sc_conv_w', 'delta_ssm_conv_w', 'delta_ssm_conv_b', 'delta_ssm_dt_bias', 'delta_ssm_a_log', 'delta_ssm_d', 'delta_ssm_norm_w', 'delta_w_sc_out', 'delta_w_sb_out', 'delta_w_ssm_out', 'delta_w_o', 'delta_w_ffn_in', 'delta_w_ffn_out', 'new_m_mod_w', 'new_m_mod_b', 'new_m_g_pre_mix', 'new_m_g_post_mix', 'new_m_g_pre_ffn', 'new_m_g_post_ffn', 'new_m_w_in', 'new_m_sc_conv_w', 'new_m_ssm_conv_w', 'new_m_ssm_conv_b', 'new_m_ssm_dt_bias', 'new_m_ssm_a_log', 'new_m_ssm_d', 'new_m_ssm_norm_w', 'new_m_w_sc_out', 'new_m_w_sb_out', 'new_m_w_ssm_out', 'new_m_w_o', 'new_m_w_ffn_in', 'new_m_w_ffn_out', 'new_v_mod_w', 'new_v_mod_b', 'new_v_g_pre_mix', 'new_v_g_post_mix', 'new_v_g_pre_ffn', 'new_v_g_post_ffn', 'new_v_w_in', 'new_v_sc_conv_w', 'new_v_ssm_conv_w', 'new_v_ssm_conv_b', 'new_v_ssm_dt_bias', 'new_v_ssm_a_log', 'new_v_ssm_d', 'new_v_ssm_norm_w', 'new_v_w_sc_out', 'new_v_w_sb_out', 'new_v_w_ssm_out', 'new_v_w_o', 'new_v_w_ffn_in', 'new_v_w_ffn_out']
TWIN_LEAF_KINDS = {'loss': 'loss', 'grad_x': 'grad_x', 'grad_mod_w': 'grad_w', 'grad_mod_b': 'grad_w', 'grad_g_pre_mix': 'grad_w', 'grad_g_post_mix': 'grad_w', 'grad_g_pre_ffn': 'grad_w', 'grad_g_post_ffn': 'grad_w', 'grad_w_in': 'grad_w', 'grad_sc_conv_w': 'grad_w', 'grad_ssm_conv_w': 'grad_w', 'grad_ssm_conv_b': 'grad_w', 'grad_ssm_dt_bias': 'grad_w', 'grad_ssm_a_log': 'grad_w', 'grad_ssm_d': 'grad_w', 'grad_ssm_norm_w': 'grad_w', 'grad_w_sc_out': 'grad_w', 'grad_w_sb_out': 'grad_w', 'grad_w_ssm_out': 'grad_w', 'grad_w_o': 'grad_w', 'grad_w_ffn_in': 'grad_w', 'grad_w_ffn_out': 'grad_w', 'delta_mod_w': 'delta_w', 'delta_mod_b': 'delta_w', 'delta_g_pre_mix': 'delta_w', 'delta_g_post_mix': 'delta_w', 'delta_g_pre_ffn': 'delta_w', 'delta_g_post_ffn': 'delta_w', 'delta_w_in': 'delta_w', 'delta_sc_conv_w': 'delta_w', 'delta_ssm_conv_w': 'delta_w', 'delta_ssm_conv_b': 'delta_w', 'delta_ssm_dt_bias': 'delta_w', 'delta_ssm_a_log': 'delta_w', 'delta_ssm_d': 'delta_w', 'delta_ssm_norm_w': 'delta_w', 'delta_w_sc_out': 'delta_w', 'delta_w_sb_out': 'delta_w', 'delta_w_ssm_out': 'delta_w', 'delta_w_o': 'delta_w', 'delta_w_ffn_in': 'delta_w', 'delta_w_ffn_out': 'delta_w', 'new_m_mod_w': 'new_m', 'new_m_mod_b': 'new_m', 'new_m_g_pre_mix': 'new_m', 'new_m_g_post_mix': 'new_m', 'new_m_g_pre_ffn': 'new_m', 'new_m_g_post_ffn': 'new_m', 'new_m_w_in': 'new_m', 'new_m_sc_conv_w': 'new_m', 'new_m_ssm_conv_w': 'new_m', 'new_m_ssm_conv_b': 'new_m', 'new_m_ssm_dt_bias': 'new_m', 'new_m_ssm_a_log': 'new_m', 'new_m_ssm_d': 'new_m', 'new_m_ssm_norm_w': 'new_m', 'new_m_w_sc_out': 'new_m', 'new_m_w_sb_out': 'new_m', 'new_m_w_ssm_out': 'new_m', 'new_m_w_o': 'new_m', 'new_m_w_ffn_in': 'new_m', 'new_m_w_ffn_out': 'new_m', 'new_v_mod_w': 'new_v', 'new_v_mod_b': 'new_v', 'new_v_g_pre_mix': 'new_v', 'new_v_g_post_mix': 'new_v', 'new_v_g_pre_ffn': 'new_v', 'new_v_g_post_ffn': 'new_v', 'new_v_w_in': 'new_v', 'new_v_sc_conv_w': 'new_v', 'new_v_ssm_conv_w': 'new_v', 'new_v_ssm_conv_b': 'new_v', 'new_v_ssm_dt_bias': 'new_v', 'new_v_ssm_a_log': 'new_v', 'new_v_ssm_d': 'new_v', 'new_v_ssm_norm_w': 'new_v', 'new_v_w_sc_out': 'new_v', 'new_v_w_sb_out': 'new_v', 'new_v_w_ssm_out': 'new_v', 'new_v_w_o': 'new_v', 'new_v_w_ffn_in': 'new_v', 'new_v_w_ffn_out': 'new_v'}


def _forward(args):
    return _fwd_reference(*[args[k] for k in FWD_PARAMS])


def _output_shape():
    def fwd():
        inp = _fwd_setup_inputs(0)
        return _fwd_reference(*[inp[k] for k in FWD_PARAMS])
    out = _jax.eval_shape(fwd)
    return out.shape, out.dtype

N_MICROBATCH = 1
ADAM_LR = 0.001
ADAM_B1 = 0.9
ADAM_B2 = 0.999
ADAM_EPS = 1e-08
ADAM_WD = 0.01
ADAM_STEP = 10
PER_EXAMPLE_BATCH_AXIS = {'x': 0, 'c': 0, 'loss_target': 0}
SHARED_INPUTS = []
_WEIGHT_DTYPES = {'mod_w': _jnp.float32, 'mod_b': _jnp.float32, 'g_pre_mix': _jnp.float32, 'g_post_mix': _jnp.float32, 'g_pre_ffn': _jnp.float32, 'g_post_ffn': _jnp.float32, 'w_in': _jnp.float32, 'sc_conv_w': _jnp.float32, 'ssm_conv_w': _jnp.float32, 'ssm_conv_b': _jnp.float32, 'ssm_dt_bias': _jnp.float32, 'ssm_a_log': _jnp.float32, 'ssm_d': _jnp.float32, 'ssm_norm_w': _jnp.float32, 'w_sc_out': _jnp.float32, 'w_sb_out': _jnp.float32, 'w_ssm_out': _jnp.float32, 'w_o': _jnp.float32, 'w_ffn_in': _jnp.float32, 'w_ffn_out': _jnp.float32}
MOMENT_SCALE = {'mod_w': 9.735831e+00, 'mod_b': 2.125643e+01, 'g_pre_mix': 1.029289e+00, 'g_post_mix': 5.274978e+01, 'g_pre_ffn': 1.016670e+00, 'g_post_ffn': 5.249778e+01, 'w_in': 7.792904e-01, 'sc_conv_w': 1.763214e+00, 'ssm_conv_w': 8.968386e-01, 'ssm_conv_b': 1.364499e+00, 'ssm_dt_bias': 6.016437e-01, 'ssm_a_log': 8.368189e+00, 'ssm_d': 5.946256e+00, 'ssm_norm_w': 1.192520e+00, 'w_sc_out': 9.370142e-01, 'w_sb_out': 1.664712e+00, 'w_ssm_out': 9.569485e-01, 'w_o': 1.967890e+00, 'w_ffn_in': 9.805839e-01, 'w_ffn_out': 1.927665e+00}


def _to_microbatches(a, axis):
    t = _jnp.moveaxis(a, axis, 0)
    t = t.reshape((N_MICROBATCH, t.shape[0] // N_MICROBATCH) + t.shape[1:])
    return _jnp.moveaxis(t, 1, axis + 1)


def setup_inputs(seed: int = 0) -> dict:
    inp = _fwd_setup_inputs(seed)
    key = _jax.random.fold_in(_jax.random.key(seed), 7919)
    shape, _ = _output_shape()
    out = dict(inp)
    out["loss_target"] = _jax.random.normal(_jax.random.fold_in(key, 0), shape, _jnp.float32)
    for i, name in enumerate(TWIN_WEIGHTS):
        w = inp[name].astype(_jnp.float32)
        if MOMENT_SCALE is None:
            s = _jnp.sqrt(_jnp.mean(_jnp.square(w)) + 1e-30)
        else:
            s = MOMENT_SCALE[name]
        km, kv = _jax.random.split(_jax.random.fold_in(key, i + 1))
        out[name] = w
        out["m_" + name] = s * _jax.random.normal(km, w.shape, _jnp.float32)
        out["v_" + name] = (s * s) * _jax.random.uniform(kv, w.shape, _jnp.float32, 0.5, 1.5)
    if N_MICROBATCH > 1:
        for name, axis in PER_EXAMPLE_BATCH_AXIS.items():
            out[name] = _to_microbatches(out[name], axis)
    return {'x': out['x'], 'c': out['c'], 'mod_w': out['mod_w'], 'mod_b': out['mod_b'], 'g_pre_mix': out['g_pre_mix'], 'g_post_mix': out['g_post_mix'], 'g_pre_ffn': out['g_pre_ffn'], 'g_post_ffn': out['g_post_ffn'], 'w_in': out['w_in'], 'sc_conv_w': out['sc_conv_w'], 'ssm_conv_w': out['ssm_conv_w'], 'ssm_conv_b': out['ssm_conv_b'], 'ssm_dt_bias': out['ssm_dt_bias'], 'ssm_a_log': out['ssm_a_log'], 'ssm_d': out['ssm_d'], 'ssm_norm_w': out['ssm_norm_w'], 'w_sc_out': out['w_sc_out'], 'w_sb_out': out['w_sb_out'], 'w_ssm_out': out['w_ssm_out'], 'w_o': out['w_o'], 'w_ffn_in': out['w_ffn_in'], 'w_ffn_out': out['w_ffn_out'], 'loss_target': out['loss_target'], 'm_mod_w': out['m_mod_w'], 'm_mod_b': out['m_mod_b'], 'm_g_pre_mix': out['m_g_pre_mix'], 'm_g_post_mix': out['m_g_post_mix'], 'm_g_pre_ffn': out['m_g_pre_ffn'], 'm_g_post_ffn': out['m_g_post_ffn'], 'm_w_in': out['m_w_in'], 'm_sc_conv_w': out['m_sc_conv_w'], 'm_ssm_conv_w': out['m_ssm_conv_w'], 'm_ssm_conv_b': out['m_ssm_conv_b'], 'm_ssm_dt_bias': out['m_ssm_dt_bias'], 'm_ssm_a_log': out['m_ssm_a_log'], 'm_ssm_d': out['m_ssm_d'], 'm_ssm_norm_w': out['m_ssm_norm_w'], 'm_w_sc_out': out['m_w_sc_out'], 'm_w_sb_out': out['m_w_sb_out'], 'm_w_ssm_out': out['m_w_ssm_out'], 'm_w_o': out['m_w_o'], 'm_w_ffn_in': out['m_w_ffn_in'], 'm_w_ffn_out': out['m_w_ffn_out'], 'v_mod_w': out['v_mod_w'], 'v_mod_b': out['v_mod_b'], 'v_g_pre_mix': out['v_g_pre_mix'], 'v_g_post_mix': out['v_g_post_mix'], 'v_g_pre_ffn': out['v_g_pre_ffn'], 'v_g_post_ffn': out['v_g_post_ffn'], 'v_w_in': out['v_w_in'], 'v_sc_conv_w': out['v_sc_conv_w'], 'v_ssm_conv_w': out['v_ssm_conv_w'], 'v_ssm_conv_b': out['v_ssm_conv_b'], 'v_ssm_dt_bias': out['v_ssm_dt_bias'], 'v_ssm_a_log': out['v_ssm_a_log'], 'v_ssm_d': out['v_ssm_d'], 'v_ssm_norm_w': out['v_ssm_norm_w'], 'v_w_sc_out': out['v_w_sc_out'], 'v_w_sb_out': out['v_w_sb_out'], 'v_w_ssm_out': out['v_w_ssm_out'], 'v_w_o': out['v_w_o'], 'v_w_ffn_in': out['v_w_ffn_in'], 'v_w_ffn_out': out['v_w_ffn_out']}


def _loss(weights, diff, rest, loss_target):
    with _jax.named_scope("forward"):
        args = {**rest, TWIN_DIFF_INPUT: diff, **{k: w.astype(_WEIGHT_DTYPES[k]) for k, w in weights.items()}}
        y = _forward(args)
    with _jax.named_scope("loss_head"):
        err = _jnp.square(y.astype(_jnp.float32) - loss_target)
        return 0.5 * _jnp.sum(_jnp.mean(err, axis=-1)) if err.ndim else 0.5 * err


def _adamw(w, g, m, v):
    m = ADAM_B1 * m + (1.0 - ADAM_B1) * g
    v = ADAM_B2 * v + (1.0 - ADAM_B2) * _jnp.square(g)
    m_hat = m / (1.0 - ADAM_B1 ** ADAM_STEP)
    v_hat = v / (1.0 - ADAM_B2 ** ADAM_STEP)
    delta = -ADAM_LR * (m_hat / (_jnp.sqrt(v_hat) + ADAM_EPS) + ADAM_WD * w)
    return delta, m, v


def reference(x, c, mod_w, mod_b, g_pre_mix, g_post_mix, g_pre_ffn, g_post_ffn, w_in, sc_conv_w, ssm_conv_w, ssm_conv_b, ssm_dt_bias, ssm_a_log, ssm_d, ssm_norm_w, w_sc_out, w_sb_out, w_ssm_out, w_o, w_ffn_in, w_ffn_out, loss_target, m_mod_w, m_mod_b, m_g_pre_mix, m_g_post_mix, m_g_pre_ffn, m_g_post_ffn, m_w_in, m_sc_conv_w, m_ssm_conv_w, m_ssm_conv_b, m_ssm_dt_bias, m_ssm_a_log, m_ssm_d, m_ssm_norm_w, m_w_sc_out, m_w_sb_out, m_w_ssm_out, m_w_o, m_w_ffn_in, m_w_ffn_out, v_mod_w, v_mod_b, v_g_pre_mix, v_g_post_mix, v_g_pre_ffn, v_g_post_ffn, v_w_in, v_sc_conv_w, v_ssm_conv_w, v_ssm_conv_b, v_ssm_dt_bias, v_ssm_a_log, v_ssm_d, v_ssm_norm_w, v_w_sc_out, v_w_sb_out, v_w_ssm_out, v_w_o, v_w_ffn_in, v_w_ffn_out):
    given = dict(x=x, c=c, mod_w=mod_w, mod_b=mod_b, g_pre_mix=g_pre_mix, g_post_mix=g_post_mix, g_pre_ffn=g_pre_ffn, g_post_ffn=g_post_ffn, w_in=w_in, sc_conv_w=sc_conv_w, ssm_conv_w=ssm_conv_w, ssm_conv_b=ssm_conv_b, ssm_dt_bias=ssm_dt_bias, ssm_a_log=ssm_a_log, ssm_d=ssm_d, ssm_norm_w=ssm_norm_w, w_sc_out=w_sc_out, w_sb_out=w_sb_out, w_ssm_out=w_ssm_out, w_o=w_o, w_ffn_in=w_ffn_in, w_ffn_out=w_ffn_out, loss_target=loss_target, m_mod_w=m_mod_w, m_mod_b=m_mod_b, m_g_pre_mix=m_g_pre_mix, m_g_post_mix=m_g_post_mix, m_g_pre_ffn=m_g_pre_ffn, m_g_post_ffn=m_g_post_ffn, m_w_in=m_w_in, m_sc_conv_w=m_sc_conv_w, m_ssm_conv_w=m_ssm_conv_w, m_ssm_conv_b=m_ssm_conv_b, m_ssm_dt_bias=m_ssm_dt_bias, m_ssm_a_log=m_ssm_a_log, m_ssm_d=m_ssm_d, m_ssm_norm_w=m_ssm_norm_w, m_w_sc_out=m_w_sc_out, m_w_sb_out=m_w_sb_out, m_w_ssm_out=m_w_ssm_out, m_w_o=m_w_o, m_w_ffn_in=m_w_ffn_in, m_w_ffn_out=m_w_ffn_out, v_mod_w=v_mod_w, v_mod_b=v_mod_b, v_g_pre_mix=v_g_pre_mix, v_g_post_mix=v_g_post_mix, v_g_pre_ffn=v_g_pre_ffn, v_g_post_ffn=v_g_post_ffn, v_w_in=v_w_in, v_sc_conv_w=v_sc_conv_w, v_ssm_conv_w=v_ssm_conv_w, v_ssm_conv_b=v_ssm_conv_b, v_ssm_dt_bias=v_ssm_dt_bias, v_ssm_a_log=v_ssm_a_log, v_ssm_d=v_ssm_d, v_ssm_norm_w=v_ssm_norm_w, v_w_sc_out=v_w_sc_out, v_w_sb_out=v_w_sb_out, v_w_ssm_out=v_w_ssm_out, v_w_o=v_w_o, v_w_ffn_in=v_w_ffn_in, v_w_ffn_out=v_w_ffn_out)
    weights = {n: given[n] for n in TWIN_WEIGHTS}
    shared = {n: given[n] for n in SHARED_INPUTS}
    per_example = {n: given[n] for n in ['x', 'c']}
    grad_fn = _jax.value_and_grad(_loss, argnums=(0, 1))

    def one_microbatch(ex, loss_target):
        ex = dict(ex)
        diff = ex.pop(TWIN_DIFF_INPUT)
        return grad_fn(weights, diff, {**shared, **ex}, loss_target)

    if N_MICROBATCH == 1:
        loss, (grad_w, grad_x) = one_microbatch(per_example, given["loss_target"])
    else:
        def body(carry, xs):
            loss_sum, grad_sum = carry
            l_k, (gw_k, gx_k) = one_microbatch(xs[0], xs[1])
            with _jax.named_scope("update"):
                return (loss_sum + l_k, _jax.tree.map(_jnp.add, grad_sum, gw_k)), gx_k

        init = (_jnp.zeros((), _jnp.float32), _jax.tree.map(_jnp.zeros_like, weights))
        (loss, grad_w), grad_x = _jax.lax.scan(body, init, (per_example, given["loss_target"]))
    with _jax.named_scope("update"):
        delta_w, new_m, new_v = {}, {}, {}
        for n in TWIN_WEIGHTS:
            delta_w[n], new_m[n], new_v[n] = _adamw(weights[n], grad_w[n], given["m_" + n], given["v_" + n])
    return (loss, grad_x, *[grad_w[n] for n in TWIN_WEIGHTS], *[delta_w[n] for n in TWIN_WEIGHTS],
            *[new_m[n] for n in TWIN_WEIGHTS], *[new_v[n] for n in TWIN_WEIGHTS])
```

```python
import functools

import jax
import jax.numpy as jnp
from jax import lax
from jax.experimental import pallas as pl
from jax.experimental.pallas import tpu as pltpu

F32 = jnp.float32
BF16 = jnp.bfloat16
HIGHEST = lax.Precision.HIGHEST
MESH_ID = pl.DeviceIdType.MESH

D_MODEL = 1024
DEPTH = 2
SC_WIDTH = 256
SC_KERNEL = 3
SB_HEAD_DIM = 64
SSM_INNER = 512
SSM_HEADS = 8
SSM_STATE = 64
SSM_CONV = 4
SSM_CHUNK = 256
SSM_CONV_DIM = 768
FFN_HIDDEN = 2816
NORM_EPS = 1e-6
N_MOD = 6
N_CHIPS = 4
N_DEV = 8

ADAM_LR = 0.001
ADAM_B1 = 0.9
ADAM_B2 = 0.999
ADAM_EPS = 1e-08
ADAM_WD = 0.01
ADAM_STEP = 10

P_WIDTH = 6144
P_A, P_B, P_Z, P_DT, P_XBC, P_G = 0, 768, 1536, 2048, 2304, 3072
DT_PAD = 256

VMEM_LIMIT_BYTES = 56 * 1024 * 1024
LANES = 128

SB_LOG_CUTOFF = -105.0
SB_BLOCK = 256


def _params(sem):
    return pltpu.CompilerParams(dimension_semantics=sem, vmem_limit_bytes=VMEM_LIMIT_BYTES)


def _pick(n, cap):
    if n <= cap:
        return n
    best = None
    for m in range(LANES, cap + 1, LANES):
        if n % m == 0:
            best = m
    assert best is not None, (n, cap)
    return best


def _rowwise(name, fn, rows, vecs, row_outs, acc_outs=(), tl=256):
    L = rows[0][0].shape[0]
    tl = min(tl, L)
    assert L % tl == 0
    n_in = len(rows) + len(vecs)
    n_ro = len(row_outs)

    def body(*refs):
        ins, ro, ao = refs[:n_in], refs[n_in:n_in + n_ro], refs[n_in + n_ro:]
        vals = fn(*[r[...] for r in ins])
        if not isinstance(vals, (tuple, list)):
            vals = (vals,)
        for o, v in zip(ro, vals[:n_ro]):
            o[...] = v.astype(o.dtype)
        if ao:
            @pl.when(pl.program_id(0) == 0)
            def _():
                for o in ao:
                    o[...] = jnp.zeros_like(o)
            for o, v in zip(ao, vals[n_ro:]):
                o[...] += v.astype(F32)

    in_specs = [pl.BlockSpec((tl, w), functools.partial(lambda i, cb: (i, cb), cb=cb)) for _, w, cb in rows]
    in_specs += [pl.BlockSpec(v.shape, lambda i: (0, 0)) for v in vecs]
    out_specs = [pl.BlockSpec((tl, w), lambda i: (i, 0)) for w, _ in row_outs]
    out_specs += [pl.BlockSpec(s, lambda i: (0, 0)) for s in acc_outs]
    out_shape = [jax.ShapeDtypeStruct((L, w), dt) for w, dt in row_outs]
    out_shape += [jax.ShapeDtypeStruct(s, F32) for s in acc_outs]
    return pl.pallas_call(
        body, name=name, grid=(L // tl,), in_specs=in_specs, out_specs=out_specs, out_shape=out_shape,
        compiler_params=_params(("arbitrary",)),
    )(*[a for a, _, _ in rows], *vecs)


def _mm(name, a, b, mode, out_dtype, tm=512, tn_cap=1408, tk_cap=2816):
    if mode == "nn":
        (M, K), (_, N) = a.shape, b.shape
    elif mode == "nt":
        (M, K), (N, _) = a.shape, b.shape
    else:
        (K, M), (_, N) = a.shape, b.shape
        tm, tk_cap = 1408, 512
    tm = _pick(M, tm)
    tn = _pick(N, tn_cap)
    tk = _pick(K, tk_cap)
    nk = K // tk

    def body(a_ref, b_ref, o_ref, *scr):
        if mode == "nn":
            p = jnp.dot(a_ref[...], b_ref[...], preferred_element_type=F32)
        elif mode == "nt":
            p = lax.dot_general(a_ref[...], b_ref[...], (((1,), (1,)), ((), ())), preferred_element_type=F32)
        else:
            p = lax.dot_general(a_ref[...], b_ref[...], (((0,), (0,)), ((), ())), preferred_element_type=F32)
        if nk == 1:
            o_ref[...] = p.astype(o_ref.dtype)
        else:
            acc = scr[0]
            k = pl.program_id(2)

            @pl.when(k == 0)
            def _():
                acc[...] = p

            @pl.when(k > 0)
            def _():
                acc[...] += p

            @pl.when(k == nk - 1)
            def _():
                o_ref[...] = acc[...].astype(o_ref.dtype)

    if mode == "nn":
        a_spec = pl.BlockSpec((tm, tk), lambda i, j, k: (i, k))
        b_spec = pl.BlockSpec((tk, tn), lambda i, j, k: (k, j))
    elif mode == "nt":
        a_spec = pl.BlockSpec((tm, tk), lambda i, j, k: (i, k))
        b_spec = pl.BlockSpec((tn, tk), lambda i, j, k: (j, k))
    else:
        a_spec = pl.BlockSpec((tk, tm), lambda i, j, k: (k, i))
        b_spec = pl.BlockSpec((tk, tn), lambda i, j, k: (k, j))
    return pl.pallas_call(
        body, name=name, grid=(M // tm, N // tn, nk), in_specs=[a_spec, b_spec],
        out_specs=pl.BlockSpec((tm, tn), lambda i, j, k: (i, j)),
        out_shape=jax.ShapeDtypeStruct((M, N), out_dtype),
        scratch_shapes=[pltpu.VMEM((tm, tn), F32)] if nk > 1 else [],
        compiler_params=_params(("arbitrary", "arbitrary", "arbitrary")),
    )(a, b)


def _f(x):
    return x.astype(F32)


def _silu(x):
    return x * jax.nn.sigmoid(x)


def _softplus(x):
    return jnp.maximum(x, 0.0) + jnp.log1p(jnp.exp(-jnp.abs(x)))


def _rms(x, g):
    r = lax.rsqrt(jnp.mean(x * x, axis=-1, keepdims=True) + NORM_EPS)
    return x * r * g


def _adaln(x, g, scale, shift):
    return _rms(x, g) * (1.0 + scale) + shift


def _resid(x, y, gate, g):
    return x + gate * _rms(y, g)


def _mid(x, y, gate, g_post, g_pre, scale, shift):
    x_new = _resid(x, y, gate, g_post)
    return x_new, _adaln(x_new, g_pre, scale, shift)


def _merge(ga, gb, gc, ya, yb, yc):
    return jax.nn.sigmoid(ga) * ya + jax.nn.sigmoid(gb) * yb + jax.nn.sigmoid(gc) * yc


def _swiglu(gt, up):
    return _silu(gt) * up


def _ssm_post(y_ssd, pre_xs, z, d_full, norm_w):
    y = (y_ssd + _silu(pre_xs) * d_full) * _silu(z)
    half = SSM_INNER // 2
    parts = []
    for g in range(2):
        yg = y[:, g * half:(g + 1) * half]
        parts.append(yg * lax.rsqrt(jnp.mean(yg * yg, axis=-1, keepdims=True) + NORM_EPS))
    return jnp.concatenate(parts, axis=1) * norm_w


def _first_fwd(x, vecs):
    return _rowwise("adaln_first", lambda x, g, sc, sh: _adaln(x, g, sc, sh),
                    [(x, D_MODEL, 0)], vecs, [(D_MODEL, BF16)], tl=512)[0]


def _mid_fwd(name, x, y, vecs):
    return _rowwise(name, lambda x, y, *v: _mid(x, _f(y), *v),
                    [(x, D_MODEL, 0), (y, D_MODEL, 0)], vecs, [(D_MODEL, F32), (D_MODEL, BF16)], tl=512)


def _mid_bwd(name, x, y, dx_new, dh, vecs):
    def fn(x, y, dxn, dh, *v):
        _, vjp = jax.vjp(_mid, x, _f(y), *v)
        return vjp((dxn, _f(dh)))

    vec = (1, D_MODEL)
    return _rowwise(name, fn, [(x, D_MODEL, 0), (y, D_MODEL, 0), (dx_new, D_MODEL, 0), (dh, D_MODEL, 0)], vecs,
                    [(D_MODEL, F32), (D_MODEL, BF16)], [vec] * 5)


def _first_bwd(x, dx_in, dh, vecs):
    def fn(x, dxi, dh, *v):
        _, vjp = jax.vjp(_adaln, x, *v)
        dx, dg, dsc, dsh = vjp(_f(dh))
        return dx + dxi, dg, dsc, dsh

    vec = (1, D_MODEL)
    return _rowwise("adaln_first_bwd", fn, [(x, D_MODEL, 0), (dx_in, D_MODEL, 0), (dh, D_MODEL, 0)], vecs,
                    [(D_MODEL, F32)], [vec] * 3)


def _last_bwd(x1, f, target, vecs):
    def fn(x1, f, t, gate, g):
        x2, vjp = jax.vjp(_resid, x1, _f(f), gate, g)
        err = x2 - t
        dx1, df, dgate, dg = vjp(err * (1.0 / D_MODEL))
        loss_cols = jnp.sum(err * err, axis=0, keepdims=True) * (0.5 / D_MODEL)
        return dx1, df, dgate, dg, loss_cols

    vec = (1, D_MODEL)
    return _rowwise("loss_last_bwd", fn, [(x1, D_MODEL, 0), (f, D_MODEL, 0), (target, D_MODEL, 0)], vecs,
                    [(D_MODEL, F32), (D_MODEL, BF16)], [vec] * 3)


HALO = 16


def _shift_down(u, prev, k):
    rows = lax.broadcasted_iota(jnp.int32, u.shape, 0)
    v = pltpu.roll(u, k, 0)
    for t in range(k):
        v = jnp.where(rows == t, prev[HALO - k + t:HALO - k + t + 1, :], v)
    return v


def _shift_up(u, nxt, k):
    n = u.shape[0]
    rows = lax.broadcasted_iota(jnp.int32, u.shape, 0)
    v = pltpu.roll(u, n - k, 0)
    for t in range(k):
        v = jnp.where(rows == n - k + t, nxt[t:t + 1, :], v)
    return v


def _conv_specs(L, tl, width, col_block):
    per = tl // HALO
    last = L // HALO - 1
    main = pl.BlockSpec((tl, width), lambda i: (i, col_block))
    before = pl.BlockSpec((HALO, width), lambda i: (jnp.maximum(i * per - 1, 0), col_block))
    after = pl.BlockSpec((HALO, width), lambda i: (jnp.minimum((i + 1) * per, last), col_block))
    return main, before, after


def _shortconv_fwd(P, w, tl=512):
    L = P.shape[0]
    tl = min(tl, L)
    C = SC_WIDTH
    main, before, _ = _conv_specs(L, tl, 3 * C, 0)

    def body(p_ref, h_ref, w_ref, o_ref):
        first = (pl.program_id(0) == 0)
        p, h = _f(p_ref[...]), _f(h_ref[...])
        b, u = p[:, :C], p[:, C:2 * C] * p[:, 2 * C:]
        uh = jnp.where(first, 0.0, h[:, C:2 * C] * h[:, 2 * C:])
        wv = w_ref[...]
        cv = wv[2:3] * u + wv[1:2] * _shift_down(u, uh, 1) + wv[0:1] * _shift_down(u, uh, 2)
        o_ref[...] = (b * cv).astype(o_ref.dtype)

    return pl.pallas_call(
        body, name="shortconv_fwd", grid=(L // tl,),
        in_specs=[main, before, pl.BlockSpec(w.shape, lambda i: (0, 0))],
        out_specs=pl.BlockSpec((tl, C), lambda i: (i, 0)),
        out_shape=jax.ShapeDtypeStruct((L, C), BF16), compiler_params=_params(("arbitrary",)),
    )(P, P, w)


def _shortconv_bwd(P, dya, w, tl=512):
    L = P.shape[0]
    tl = min(tl, L)
    C = SC_WIDTH
    main, before, after = _conv_specs(L, tl, 3 * C, 0)
    dmain, _, dafter = _conv_specs(L, tl, C, 0)
    n = L // tl

    def body(p_ref, h_ref, n_ref, d_ref, dn_ref, w_ref, o_ref, dw0, dw1, dw2):
        i = pl.program_id(0)
        p, h, nx = _f(p_ref[...]), _f(h_ref[...]), _f(n_ref[...])
        b, c, x = p[:, :C], p[:, C:2 * C], p[:, 2 * C:]
        u = c * x
        uh = jnp.where(i == 0, 0.0, h[:, C:2 * C] * h[:, 2 * C:])
        u1, u2 = _shift_down(u, uh, 1), _shift_down(u, uh, 2)
        wv = w_ref[...]
        cv = wv[2:3] * u + wv[1:2] * u1 + wv[0:1] * u2
        dy = _f(d_ref[...])
        dcv = dy * b
        dcv_n = jnp.where(i == n - 1, 0.0, _f(dn_ref[...]) * nx[:, :C])
        du = wv[2:3] * dcv + wv[1:2] * _shift_up(dcv, dcv_n, 1) + wv[0:1] * _shift_up(dcv, dcv_n, 2)
        o_ref[:, :C] = (dy * cv).astype(o_ref.dtype)
        o_ref[:, C:2 * C] = (du * x).astype(o_ref.dtype)
        o_ref[:, 2 * C:] = (du * c).astype(o_ref.dtype)

        @pl.when(i == 0)
        def _():
            for r in (dw0, dw1, dw2):
                r[...] = jnp.zeros_like(r)

        dw0[...] += jnp.sum(dcv * u2, axis=0, keepdims=True)
        dw1[...] += jnp.sum(dcv * u1, axis=0, keepdims=True)
        dw2[...] += jnp.sum(dcv * u, axis=0, keepdims=True)

    vec = pl.BlockSpec((1, C), lambda i: (0, 0))
    return pl.pallas_call(
        body, name="shortconv_bwd", grid=(n,),
        in_specs=[main, before, after, dmain, dafter, pl.BlockSpec(w.shape, lambda i: (0, 0))],
        out_specs=[pl.BlockSpec((tl, 3 * C), lambda i: (i, 0)), vec, vec, vec],
        out_shape=[jax.ShapeDtypeStruct((L, 3 * C), BF16)] + [jax.ShapeDtypeStruct((1, C), F32)] * 3,
        compiler_params=_params(("arbitrary",)),
    )(P, P, P, dya, dya, w)


def _ssmconv_fwd(P, w, bias, tl=512):
    L = P.shape[0]
    tl = min(tl, L)
    C = SSM_CONV_DIM
    main, before, _ = _conv_specs(L, tl, C, P_XBC // C)

    def body(p_ref, h_ref, w_ref, b_ref, o_ref):
        u = _f(p_ref[...])
        uh = jnp.where(pl.program_id(0) == 0, 0.0, _f(h_ref[...]))
        wv = w_ref[...]
        acc = wv[3:4] * u + b_ref[...]
        for k in range(1, SSM_CONV):
            acc = acc + wv[3 - k:4 - k] * _shift_down(u, uh, k)
        o_ref[...] = acc

    return pl.pallas_call(
        body, name="ssmconv_fwd", grid=(L // tl,),
        in_specs=[main, before, pl.BlockSpec(w.shape, lambda i: (0, 0)), pl.BlockSpec(bias.shape, lambda i: (0, 0))],
        out_specs=pl.BlockSpec((tl, C), lambda i: (i, 0)),
        out_shape=jax.ShapeDtypeStruct((L, C), F32), compiler_params=_params(("arbitrary",)),
    )(P, P, w, bias)


def _ssmconv_bwd(P, dpre, w, tl=512):
    L = P.shape[0]
    tl = min(tl, L)
    C = SSM_CONV_DIM
    main, before, _ = _conv_specs(L, tl, C, P_XBC // C)
    dmain, _, dafter = _conv_specs(L, tl, C, 0)
    n = L // tl

    def body(p_ref, h_ref, d_ref, dn_ref, w_ref, o_ref, dw0, dw1, dw2, dw3, db):
        i = pl.program_id(0)
        u = _f(p_ref[...])
        uh = jnp.where(i == 0, 0.0, _f(h_ref[...]))
        d = d_ref[...]
        dn = jnp.where(i == n - 1, 0.0, dn_ref[...])
        wv = w_ref[...]
        du = wv[3:4] * d
        for k in range(1, SSM_CONV):
            du = du + wv[3 - k:4 - k] * _shift_up(d, dn, k)
        o_ref[...] = du.astype(o_ref.dtype)

        @pl.when(i == 0)
        def _():
            for r in (dw0, dw1, dw2, dw3, db):
                r[...] = jnp.zeros_like(r)

        for k, r in ((3, dw0), (2, dw1), (1, dw2)):
            r[...] += jnp.sum(d * _shift_down(u, uh, k), axis=0, keepdims=True)
        dw3[...] += jnp.sum(d * u, axis=0, keepdims=True)
        db[...] += jnp.sum(d, axis=0, keepdims=True)

    vec = pl.BlockSpec((1, C), lambda i: (0, 0))
    return pl.pallas_call(
        body, name="ssmconv_bwd", grid=(n,),
        in_specs=[main, before, dmain, dafter, pl.BlockSpec(w.shape, lambda i: (0, 0))],
        out_specs=[pl.BlockSpec((tl, C), lambda i: (i, 0))] + [vec] * 5,
        out_shape=[jax.ShapeDtypeStruct((L, C), BF16)] + [jax.ShapeDtypeStruct((1, C), F32)] * 5,
        compiler_params=_params(("arbitrary",)),
    )(P, P, dpre, dpre, w)


def _dot_nt(a, b):
    return lax.dot_general(a, b, (((1,), (1,)), ((), ())), preferred_element_type=F32)


def _dot_tn(a, b):
    return lax.dot_general(a, b, (((0,), (0,)), ((), ())), preferred_element_type=F32)


def _ssd_chunk(pre, dtr, s_prev, dtb, alog):
    T = pre.shape[0]
    act = _silu(pre)
    xs, bm, cm = act[:, :SSM_INNER], act[:, SSM_INNER:SSM_INNER + 128], act[:, SSM_INNER + 128:]
    lane = lax.broadcasted_iota(jnp.int32, (1, LANES), 1)
    dt = jnp.where(lane < SSM_HEADS, _softplus(dtr + dtb), 0.0)
    a = dt * (-jnp.exp(alog))
    ri = lax.broadcasted_iota(jnp.int32, (T, T), 0)
    ci = lax.broadcasted_iota(jnp.int32, (T, T), 1)
    causal = ci <= ri
    a_cs = jnp.dot(causal.astype(F32), a, precision=HIGHEST, preferred_element_type=F32)
    eh = lax.broadcasted_iota(jnp.int32, (LANES, SSM_INNER), 0)
    ej = lax.broadcasted_iota(jnp.int32, (LANES, SSM_INNER), 1)
    expand = (lax.shift_right_logical(ej, 6) == eh).astype(F32)
    dt_full = jnp.dot(dt, expand, precision=HIGHEST, preferred_element_type=F32)
    acs_full = jnp.dot(a_cs, expand, precision=HIGHEST, preferred_element_type=F32)
    alast_full = acs_full[T - 1:T, :]
    xdt = xs * dt_full
    a_cs_t = a_cs.T
    sel_rows = lax.broadcasted_iota(jnp.int32, (LANES, T), 0)
    ys, s_new = [], []
    for g in range(2):
        in_group = lax.shift_right_logical(lane, 6) == g
        cg = jnp.where(in_group, cm, 0.0).astype(BF16)
        bg = jnp.where(in_group, bm, 0.0).astype(BF16)
        scores = _dot_nt(cg, bg)
        for pp in range(2):
            hp = 2 * g + pp
            cols = slice(hp * LANES, (hp + 1) * LANES)
            xp, acsp = xdt[:, cols], acs_full[:, cols]
            per_head = []
            for hh in range(2):
                h = 2 * hp + hh
                col = jnp.dot(a_cs, (sel_rows == h).astype(F32), precision=HIGHEST, preferred_element_type=F32)
                decay = jnp.exp(jnp.where(causal, col - a_cs_t[h:h + 1, :], -jnp.inf))
                per_head.append(jnp.dot((scores * decay).astype(BF16), xp.astype(BF16), preferred_element_type=F32))
            y_diag = jnp.where(lane < SSM_STATE, per_head[0], per_head[1])
            sp = s_prev[hp * LANES:(hp + 1) * LANES, :]
            y_off = jnp.dot(cg, sp.astype(BF16), preferred_element_type=F32) * jnp.exp(acsp)
            ys.append(y_diag + y_off)
            to_end = jnp.exp(alast_full[:, cols] - acsp)
            s_new.append(sp * jnp.exp(alast_full[:, cols]) + _dot_tn(bg, (xp * to_end).astype(BF16)))
    return jnp.concatenate(ys, axis=1), jnp.concatenate(s_new, axis=0)


def _ssd_fwd(pre, P, dtb, alog):
    L = pre.shape[0]
    T = min(SSM_CHUNK, L)
    nc = L // T

    def body(pre_ref, dt_ref, dtb_ref, al_ref, y_ref, st_ref, s_scr):
        @pl.when(pl.program_id(0) == 0)
        def _():
            s_scr[...] = jnp.zeros_like(s_scr)

        st_ref[0] = s_scr[...]
        y, s = _ssd_chunk(pre_ref[...], _f(dt_ref[...]), s_scr[...], dtb_ref[...], al_ref[...])
        y_ref[...] = y
        s_scr[...] = s

    vec = pl.BlockSpec((1, LANES), lambda i: (0, 0))
    return pl.pallas_call(
        body, name="ssd_fwd", grid=(nc,),
        in_specs=[pl.BlockSpec((T, SSM_CONV_DIM), lambda i: (i, 0)), pl.BlockSpec((T, LANES), lambda i: (i, P_DT // LANES)),
                  vec, vec],
        out_specs=[pl.BlockSpec((T, SSM_INNER), lambda i: (i, 0)), pl.BlockSpec((1, 512, LANES), lambda i: (i, 0, 0))],
        out_shape=[jax.ShapeDtypeStruct((L, SSM_INNER), F32), jax.ShapeDtypeStruct((nc, 512, LANES), F32)],
        scratch_shapes=[pltpu.VMEM((512, LANES), F32)], compiler_params=_params(("arbitrary",)),
    )(pre, P, dtb, alog)


def _ssd_bwd(pre, P, states, dy, dxs_extra, dtb, alog):
    L = pre.shape[0]
    T = min(SSM_CHUNK, L)
    nc = L // T

    def body(pre_ref, dt_ref, st_ref, dy_ref, dx_ref, dtb_ref, al_ref, dpre_ref, ddt_ref, ddtb_ref, dal_ref, ds_scr):
        @pl.when(pl.program_id(0) == 0)
        def _():
            ds_scr[...] = jnp.zeros_like(ds_scr)
            ddtb_ref[...] = jnp.zeros_like(ddtb_ref)
            dal_ref[...] = jnp.zeros_like(dal_ref)

        _, vjp = jax.vjp(_ssd_chunk, pre_ref[...], _f(dt_ref[...]), st_ref[0], dtb_ref[...], al_ref[...])
        dpre, ddt, ds, ddtb, dal = vjp((dy_ref[...], ds_scr[...]))
        dpre_ref[:, :SSM_INNER] = dpre[:, :SSM_INNER] + dx_ref[...]
        dpre_ref[:, SSM_INNER:] = dpre[:, SSM_INNER:]
        ddt_ref[:, :LANES] = ddt.astype(ddt_ref.dtype)
        ddt_ref[:, LANES:] = jnp.zeros((T, DT_PAD - LANES), ddt_ref.dtype)
        ds_scr[...] = ds
        ddtb_ref[...] += ddtb
        dal_ref[...] += dal

    vec = pl.BlockSpec((1, LANES), lambda i: (0, 0))
    rev = lambda i: (nc - 1 - i, 0)
    return pl.pallas_call(
        body, name="ssd_bwd", grid=(nc,),
        in_specs=[pl.BlockSpec((T, SSM_CONV_DIM), rev), pl.BlockSpec((T, LANES), lambda i: (nc - 1 - i, P_DT // LANES)),
                  pl.BlockSpec((1, 512, LANES), lambda i: (nc - 1 - i, 0, 0)),
                  pl.BlockSpec((T, SSM_INNER), rev), pl.BlockSpec((T, SSM_INNER), rev), vec, vec],
        out_specs=[pl.BlockSpec((T, SSM_CONV_DIM), rev), pl.BlockSpec((T, DT_PAD), rev), vec, vec],
        out_shape=[jax.ShapeDtypeStruct((L, SSM_CONV_DIM), F32), jax.ShapeDtypeStruct((L, DT_PAD), BF16),
                   jax.ShapeDtypeStruct((1, LANES), F32), jax.ShapeDtypeStruct((1, LANES), F32)],
        scratch_shapes=[pltpu.VMEM((512, LANES), F32)], compiler_params=_params(("arbitrary",)),
    )(pre, P, states, dy, dxs_extra, dtb, alog)


def _sb_tile(qm, kb, later, rel, ri, ci, strict):
    z = _dot_nt(qm, kb) * (SB_HEAD_DIM ** -0.5)
    mask = (ci + rel) < ri
    lk = jnp.where(mask, -_softplus(z), 0.0)
    log_a = jnp.where(mask, z + lk + _dot_split(lk, strict) + later, -jnp.inf)
    return z, mask, lk, log_a


def _dot_split(x, m):
    hi = x.astype(BF16)
    lo = (x - hi.astype(F32)).astype(BF16)
    return jnp.dot(hi, m, preferred_element_type=F32) + jnp.dot(lo, m, preferred_element_type=F32)


def _sb_consts(tq):
    lane = lax.broadcasted_iota(jnp.int32, (1, LANES), 1)
    ri = lax.broadcasted_iota(jnp.int32, (tq, tq), 0)
    ci = lax.broadcasted_iota(jnp.int32, (tq, tq), 1)
    strict = (ri > ci).astype(BF16)
    return lane, ri, ci, strict


def _sb_fwd(P):
    L = P.shape[0]
    tq = min(SB_BLOCK, L)
    nq = L // tq
    qb = P_B // LANES

    def body(q_ref, k_ref, v_ref, o_ref):
        i = pl.program_id(1)
        lane, ri, ci, strict = _sb_consts(tq)
        q = q_ref[...]
        out = jnp.zeros((tq, LANES), F32)
        for hh in range(2):
            head = (lane >= SB_HEAD_DIM) if hh else (lane < SB_HEAD_DIM)
            qm = jnp.where(head, q, jnp.zeros_like(q))

            def cond(c):
                return jnp.logical_and(c[0] >= 0, jnp.max(c[1]) > SB_LOG_CUTOFF)

            def step(c):
                j, later, acc = c
                off = pl.multiple_of(j * tq, tq)
                _, _, lk, log_a = _sb_tile(qm, k_ref[pl.ds(off, tq), :], later, (j - i) * tq, ri, ci, strict)
                acc = acc + jnp.dot(jnp.exp(log_a).astype(BF16), v_ref[pl.ds(off, tq), :], preferred_element_type=F32)
                return j - 1, later + jnp.sum(lk, axis=1, keepdims=True), acc

            _, _, acc = lax.while_loop(cond, step, (i, jnp.zeros((tq, 1), F32), jnp.zeros((tq, LANES), F32)))
            out = jnp.where(head, acc, out)
        o_ref[...] = out.astype(o_ref.dtype)

    return pl.pallas_call(
        body, name="sb_fwd", grid=(2, nq),
        in_specs=[pl.BlockSpec((tq, LANES), lambda p, i: (i, qb + p)),
                  pl.BlockSpec((L, LANES), lambda p, i: (0, qb + 2 + p)),
                  pl.BlockSpec((L, LANES), lambda p, i: (0, qb + 4 + p))],
        out_specs=pl.BlockSpec((tq, LANES), lambda p, i: (i, p)),
        out_shape=jax.ShapeDtypeStruct((L, 2 * LANES), BF16),
        compiler_params=_params(("arbitrary", "arbitrary")),
    )(P, P, P)


def _sb_bwd(P, dyb):
    L = P.shape[0]
    tq = min(SB_BLOCK, L)
    nq = L // tq
    qb = P_B // LANES

    def body(q_ref, k_ref, v_ref, do_ref, dq_ref, dk_ref, dv_ref):
        i = pl.program_id(1)
        lane, ri, ci, strict = _sb_consts(tq)

        @pl.when(i == 0)
        def _():
            dk_ref[...] = jnp.zeros_like(dk_ref)
            dv_ref[...] = jnp.zeros_like(dv_ref)

        q = q_ref[...]
        do = do_ref[...]
        dq = jnp.zeros((tq, LANES), F32)
        for hh in range(2):
            head = (lane >= SB_HEAD_DIM) if hh else (lane < SB_HEAD_DIM)
            qm = jnp.where(head, q, jnp.zeros_like(q))
            dom = jnp.where(head, do, jnp.zeros_like(do))

            def tile(j, later):
                off = pl.multiple_of(j * tq, tq)
                kb, vb = k_ref[pl.ds(off, tq), :], v_ref[pl.ds(off, tq), :]
                z, mask, lk, log_a = _sb_tile(qm, kb, later, (j - i) * tq, ri, ci, strict)
                att = jnp.exp(log_a)
                return off, kb, z, mask, lk, att, att * _dot_nt(dom, vb)

            def cond(c):
                return jnp.logical_and(c[0] >= 0, jnp.max(c[1]) > SB_LOG_CUTOFF)

            def count(c):
                j, later, later_g = c
                _, _, _, _, lk, _, g = tile(j, later)
                return j - 1, later + jnp.sum(lk, axis=1, keepdims=True), later_g + jnp.sum(g, axis=1, keepdims=True)

            def step(c):
                j, later, later_g, acc = c
                off, kb, z, mask, lk, att, g = tile(j, later)
                g_after = _dot_split(g, strict) + later_g
                dz = jnp.where(mask, g - (total - g_after) * jnp.exp(z + lk), 0.0) * (SB_HEAD_DIM ** -0.5)
                dzb = dz.astype(BF16)
                dk_ref[pl.ds(off, tq), :] += _dot_tn(dzb, qm)
                dv_ref[pl.ds(off, tq), :] += _dot_tn(att.astype(BF16), dom)
                acc = acc + jnp.dot(dzb, kb, preferred_element_type=F32)
                return (j - 1, later + jnp.sum(lk, axis=1, keepdims=True),
                        later_g + jnp.sum(g, axis=1, keepdims=True), acc)

            zero = jnp.zeros((tq, 1), F32)
            _, _, total = lax.while_loop(cond, count, (i, zero, zero))
            _, _, _, acc = lax.while_loop(cond, step, (i, zero, zero, jnp.zeros((tq, LANES), F32)))
            dq = jnp.where(head, acc, dq)
        dq_ref[...] = dq

    full = pl.BlockSpec((L, LANES), lambda p, i: (0, p))
    tile_spec = pl.BlockSpec((tq, LANES), lambda p, i: (i, p))
    return pl.pallas_call(
        body, name="sb_bwd", grid=(2, nq),
        in_specs=[pl.BlockSpec((tq, LANES), lambda p, i: (i, qb + p)),
                  pl.BlockSpec((L, LANES), lambda p, i: (0, qb + 2 + p)),
                  pl.BlockSpec((L, LANES), lambda p, i: (0, qb + 4 + p)), tile_spec],
        out_specs=[tile_spec, full, full],
        out_shape=[jax.ShapeDtypeStruct((L, 2 * LANES), F32)] * 3,
        compiler_params=_params(("arbitrary", "arbitrary")),
    )(P, P, P, dyb)


MOD_SHARD = N_MOD * D_MODEL // N_CHIPS


def _mod_fwd(c_all, mod_w, mod_b_sh):
    tn = 512

    def body(c_ref, w_ref, b_ref, o_ref):
        o_ref[0] = jnp.dot(_silu(c_ref[...]), w_ref[0], precision=HIGHEST, preferred_element_type=F32) + b_ref[0]

    return pl.pallas_call(
        body, name="mod_fwd", grid=(DEPTH, MOD_SHARD // tn),
        in_specs=[pl.BlockSpec((N_DEV, D_MODEL), lambda l, j: (0, 0)),
                  pl.BlockSpec((1, D_MODEL, tn), lambda l, j: (l, 0, j)),
                  pl.BlockSpec((1, 1, tn), lambda l, j: (l, 0, j))],
        out_specs=pl.BlockSpec((1, N_DEV, tn), lambda l, j: (l, 0, j)),
        out_shape=jax.ShapeDtypeStruct((DEPTH, N_DEV, MOD_SHARD), F32),
        compiler_params=_params(("arbitrary", "arbitrary")),
    )(c_all, mod_w, mod_b_sh)


def _mod_bwd(c_all, dmod_sh):
    tn = 512

    def body(c_ref, d_ref, o_ref):
        o_ref[0] = lax.dot_general(_silu(c_ref[...]), d_ref[0], (((0,), (0,)), ((), ())), precision=HIGHEST,
                                   preferred_element_type=F32)

    return pl.pallas_call(
        body, name="mod_bwd", grid=(DEPTH, MOD_SHARD // tn),
        in_specs=[pl.BlockSpec((N_DEV, D_MODEL), lambda l, j: (0, 0)),
                  pl.BlockSpec((1, N_DEV, tn), lambda l, j: (l, 0, j))],
        out_specs=pl.BlockSpec((1, D_MODEL, tn), lambda l, j: (l, 0, j)),
        out_shape=jax.ShapeDtypeStruct((DEPTH, D_MODEL, MOD_SHARD), F32),
        compiler_params=_params(("arbitrary", "arbitrary")),
    )(c_all, dmod_sh)


def _row_tile(rows, cap):
    if rows <= cap:
        return rows
    best = None
    for t in range(8, cap + 1, 8):
        if rows % t == 0:
            best = t
    assert best is not None, (rows, cap)
    return best


def _adamw(name, w, gs, m, v, tr=256):
    R, W = w.shape
    tr = _row_tile(R, tr)
    ng = len(gs)

    def body(*refs):
        w_ref, g_refs, (m_ref, v_ref) = refs[0], refs[1:1 + ng], refs[1 + ng:3 + ng]
        g_out, d_out, m_out, v_out = refs[3 + ng:]
        g = g_refs[0][...]
        for r in g_refs[1:]:
            g = g + r[...]
        mm = ADAM_B1 * m_ref[...] + (1.0 - ADAM_B1) * g
        vv = ADAM_B2 * v_ref[...] + (1.0 - ADAM_B2) * (g * g)
        m_hat = mm / (1.0 - ADAM_B1 ** ADAM_STEP)
        v_hat = vv / (1.0 - ADAM_B2 ** ADAM_STEP)
        g_out[...] = g
        d_out[...] = -ADAM_LR * (m_hat / (jnp.sqrt(v_hat) + ADAM_EPS) + ADAM_WD * w_ref[...])
        m_out[...] = mm
        v_out[...] = vv

    spec = pl.BlockSpec((tr, W), lambda i: (i, 0))
    return pl.pallas_call(
        body, name=name, grid=(R // tr,), in_specs=[spec] * (3 + ng), out_specs=[spec] * 4,
        out_shape=[jax.ShapeDtypeStruct((R, W), F32)] * 4, compiler_params=_params(("arbitrary",)),
    )(w, *gs, m, v)


def _sum_slots(name, a, tr=256):
    n, R, W = a.shape
    tr = _row_tile(R, tr)

    def body(a_ref, o_ref):
        acc = _f(a_ref[0])
        for j in range(1, n):
            acc = acc + _f(a_ref[j])
        o_ref[...] = acc

    return pl.pallas_call(
        body, name=name, grid=(R // tr,), in_specs=[pl.BlockSpec((n, tr, W), lambda i: (0, i, 0))],
        out_specs=pl.BlockSpec((tr, W), lambda i: (i, 0)), out_shape=jax.ShapeDtypeStruct((R, W), F32),
        compiler_params=_params(("arbitrary",)),
    )(a)


def _here():
    return lax.axis_index("x"), lax.axis_index("y"), lax.axis_index("c")


def _flip(v, d):
    return 1 - v if d else v


def _allgather_small(name, buf):
    R = buf.shape[0]
    rel = [(dx, dy, dc) for dx in (0, 1) for dy in (0, 1) for dc in (0, 1)][1:]

    def body(x_ref, o_ref, send, recv, lsem):
        x, y, c = _here()
        me = 4 * x + 2 * y + c
        mine = pltpu.make_async_copy(x_ref, o_ref.at[me], lsem)
        mine.start()

        def copy(k, slot):
            dx, dy, dc = rel[k]
            return pltpu.make_async_remote_copy(
                src_ref=x_ref, dst_ref=o_ref.at[slot], send_sem=send.at[k], recv_sem=recv.at[k],
                device_id=(_flip(x, dx), _flip(y, dy), _flip(c, dc)), device_id_type=MESH_ID)

        sent = [copy(k, me) for k in range(len(rel))]
        for cp in sent:
            cp.start()
        for k, (dx, dy, dc) in enumerate(rel):
            copy(k, 4 * _flip(x, dx) + 2 * _flip(y, dy) + _flip(c, dc)).wait_recv()
        for cp in sent:
            cp.wait_send()
        mine.wait()

    return pl.pallas_call(
        body, name=name, out_shape=jax.ShapeDtypeStruct((N_DEV, R, LANES), F32),
        in_specs=[pl.BlockSpec(memory_space=pltpu.VMEM)], out_specs=pl.BlockSpec(memory_space=pltpu.VMEM),
        scratch_shapes=[pltpu.SemaphoreType.DMA((7,)), pltpu.SemaphoreType.DMA((7,)), pltpu.SemaphoreType.DMA],
    )(buf)


CHIP_REL = [(1, 0), (0, 1), (1, 1)]


def _chip_exchange(name, arrays, gather):
    n = len(arrays)

    def body(*refs):
        ins, outs = refs[:n], refs[n:2 * n]
        send, recv, lsem = refs[2 * n:]
        x, y, c = _here()
        s = 2 * x + y

        def copy(w, k, slot):
            dx, dy = CHIP_REL[k]
            px, py = _flip(x, dx), _flip(y, dy)
            src = ins[w] if gather else ins[w].at[2 * px + py]
            return pltpu.make_async_remote_copy(
                src_ref=src, dst_ref=outs[w].at[slot], send_sem=send.at[3 * w + k], recv_sem=recv.at[3 * w + k],
                device_id=(px, py, c), device_id_type=MESH_ID)

        local = [pltpu.make_async_copy(ins[w] if gather else ins[w].at[s], outs[w].at[s], lsem.at[w]) for w in range(n)]
        for cp in local:
            cp.start()
        sent = [copy(w, k, s) for w in range(n) for k in range(3)]
        for cp in sent:
            cp.start()
        for w in range(n):
            for k, (dx, dy) in enumerate(CHIP_REL):
                copy(w, k, 2 * _flip(x, dx) + _flip(y, dy)).wait_recv()
        for cp in sent:
            cp.wait_send()
        for cp in local:
            cp.wait()

    def out_of(a):
        return jax.ShapeDtypeStruct(((N_CHIPS,) + a.shape) if gather else a.shape, a.dtype)

    any_spec = pl.BlockSpec(memory_space=pl.ANY)
    return pl.pallas_call(
        body, name=name, out_shape=[out_of(a) for a in arrays],
        in_specs=[any_spec] * n, out_specs=[any_spec] * n,
        scratch_shapes=[pltpu.SemaphoreType.DMA((3 * n,)), pltpu.SemaphoreType.DMA((3 * n,)), pltpu.SemaphoreType.DMA((n,))],
    )(*arrays)


def _sibling_exchange(name, arrays):
    n = len(arrays)

    def body(*refs):
        ins, outs = refs[:n], refs[n:2 * n]
        send, recv = refs[2 * n:]
        x, y, c = _here()
        cps = [pltpu.make_async_remote_copy(src_ref=ins[w], dst_ref=outs[w], send_sem=send.at[w], recv_sem=recv.at[w],
                                            device_id=(x, y, 1 - c), device_id_type=MESH_ID) for w in range(n)]
        for cp in cps:
            cp.start()
        for cp in cps:
            cp.wait()

    any_spec = pl.BlockSpec(memory_space=pl.ANY)
    return pl.pallas_call(
        body, name=name, out_shape=[jax.ShapeDtypeStruct(a.shape, a.dtype) for a in arrays],
        in_specs=[any_spec] * n, out_specs=[any_spec] * n,
        scratch_shapes=[pltpu.SemaphoreType.DMA((n,)), pltpu.SemaphoreType.DMA((n,))],
    )(*arrays)


def _pack(arrs):
    flat = jnp.concatenate([a.reshape(-1).astype(F32) for a in arrs])
    n = flat.shape[0]
    rows = -(-n // (8 * LANES)) * 8
    return jnp.pad(flat, (0, rows * LANES - n)).reshape(rows, LANES)


def _unpack(buf, shapes):
    lead = buf.shape[:-2]
    flat = buf.reshape(lead + (-1,))
    out, off = [], 0
    for s in shapes:
        n = 1
        for d in s:
            n *= d
        out.append(flat[..., off:off + n].reshape(lead + tuple(s)))
        off += n
    return out


def _pad_w_in(w):
    return jnp.concatenate([w[:, :2048], w[:, 2816:2824], jnp.zeros((w.shape[0], P_XBC - P_DT - 8), w.dtype),
                            w[:, 2048:2816], w[:, 2824:]], axis=1)


def _unpad_w_in(g):
    return jnp.concatenate([g[:, :P_DT], g[:, P_XBC:P_G], g[:, P_DT:P_DT + 8], g[:, P_G:]], axis=1)


def _row(v):
    return v.reshape(1, -1)


BIG = (("w_in", 2), ("w_sc_out", 2), ("w_sb_out", 2), ("w_ssm_out", 2), ("w_o", 1), ("w_ffn_in", 2), ("w_ffn_out", 1))
SMALL = ("mod_b", "g_pre_mix", "g_post_mix", "g_pre_ffn", "g_post_ffn", "sc_conv_w", "ssm_conv_w", "ssm_conv_b",
         "ssm_dt_bias", "ssm_a_log", "ssm_d", "ssm_norm_w")
WEIGHT_ORDER = ("mod_w", "mod_b", "g_pre_mix", "g_post_mix", "g_pre_ffn", "g_post_ffn", "w_in", "sc_conv_w",
                "ssm_conv_w", "ssm_conv_b", "ssm_dt_bias", "ssm_a_log", "ssm_d", "ssm_norm_w", "w_sc_out", "w_sb_out",
                "w_ssm_out", "w_o", "w_ffn_in", "w_ffn_out")


def _layer_fwd(l, x_in, h, W, V):
    S = {"x_in": x_in, "h": h}
    P = _mm(f"in_proj{l}", h, W["w_in"], "nn", BF16, tn_cap=1024)
    S["P"] = P
    S["ya"] = _shortconv_fwd(P, V["sc_w"])
    S["yb"] = _sb_fwd(P)
    S["pre"] = _ssmconv_fwd(P, V["ssm_w"], V["ssm_b"])
    S["y_ssd"], S["states"] = _ssd_fwd(S["pre"], P, V["dtb"], V["alog"])
    S["yc"] = _rowwise(f"ssm_post{l}", lambda y, px, z, d, nw: _ssm_post(y, px, _f(z), d, nw),
                       [(S["y_ssd"], SSM_INNER, 0), (S["pre"], SSM_INNER, 0), (P, SSM_INNER, P_Z // SSM_INNER)],
                       [V["d_full"], V["norm_w"]], [(SSM_INNER, BF16)])[0]
    S["Ya"] = _mm(f"sc_out{l}", S["ya"], W["w_sc_out"], "nn", BF16)
    S["Yb"] = _mm(f"sb_out{l}", S["yb"], W["w_sb_out"], "nn", BF16)
    S["Yc"] = _mm(f"ssm_out{l}", S["yc"], W["w_ssm_out"], "nn", BF16)
    gb = P_G // D_MODEL
    S["merged"] = _rowwise(f"merge{l}", lambda *t: _merge(*[_f(v) for v in t]),
                           [(P, D_MODEL, gb), (P, D_MODEL, gb + 1), (P, D_MODEL, gb + 2),
                            (S["Ya"], D_MODEL, 0), (S["Yb"], D_MODEL, 0), (S["Yc"], D_MODEL, 0)], [], [(D_MODEL, BF16)])[0]
    S["mix"] = _mm(f"w_o{l}", S["merged"], W["w_o"], "nn", BF16)
    S["x1"], S["h2"] = _mid_fwd(f"mid_mix{l}", x_in, S["mix"], V["mid_mix"])
    S["GU"] = _mm(f"ffn_in{l}", S["h2"], W["w_ffn_in"], "nn", BF16)
    S["act"] = _rowwise(f"swiglu{l}", lambda g, u: _swiglu(_f(g), _f(u)),
                        [(S["GU"], FFN_HIDDEN, 0), (S["GU"], FFN_HIDDEN, 1)], [], [(FFN_HIDDEN, BF16)])[0]
    S["f"] = _mm(f"ffn_out{l}", S["act"], W["w_ffn_out"], "nn", BF16)
    return S


def _layer_bwd(l, S, W, V, dx1, df):
    G = {}
    P = S["P"]
    d_act = _mm(f"d_act{l}", df, W["w_ffn_out"], "nt", BF16)
    G["w_ffn_out"] = _mm(f"gw_ffn_out{l}", S["act"], df, "tn", F32)

    def swiglu_bwd(g, u, d):
        _, vjp = jax.vjp(_swiglu, _f(g), _f(u))
        return jnp.concatenate(vjp(_f(d)), axis=1)

    dGU = _rowwise(f"swiglu_bwd{l}", swiglu_bwd, [(S["GU"], FFN_HIDDEN, 0), (S["GU"], FFN_HIDDEN, 1), (d_act, FFN_HIDDEN, 0)],
                   [], [(2 * FFN_HIDDEN, BF16)])[0]
    dh2 = _mm(f"d_h2{l}", dGU, W["w_ffn_in"], "nt", BF16)
    G["w_ffn_in"] = _mm(f"gw_ffn_in{l}", S["h2"], dGU, "tn", F32)
    dx, dmix, G["gate1"], G["g_post_mix"], G["g_pre_ffn"], G["scale2"], G["shift2"] = _mid_bwd(
        f"mid_mix_bwd{l}", S["x_in"], S["mix"], dx1, dh2, V["mid_mix"])
    dmerged = _mm(f"d_merged{l}", dmix, W["w_o"], "nt", BF16)
    G["w_o"] = _mm(f"gw_o{l}", S["merged"], dmix, "tn", F32)

    def merge_bwd(ga, gb, gc, ya, yb, yc, d):
        _, vjp = jax.vjp(_merge, *[_f(v) for v in (ga, gb, gc, ya, yb, yc)])
        dga, dgb, dgc, dya, dyb, dyc = vjp(_f(d))
        return jnp.concatenate([dga, dgb, dgc], axis=1), dya, dyb, dyc

    gb = P_G // D_MODEL
    dG, dYa, dYb, dYc = _rowwise(
        f"merge_bwd{l}", merge_bwd,
        [(P, D_MODEL, gb), (P, D_MODEL, gb + 1), (P, D_MODEL, gb + 2), (S["Ya"], D_MODEL, 0), (S["Yb"], D_MODEL, 0),
         (S["Yc"], D_MODEL, 0), (dmerged, D_MODEL, 0)], [], [(3 * D_MODEL, BF16)] + [(D_MODEL, BF16)] * 3)
    dya = _mm(f"d_ya{l}", dYa, W["w_sc_out"], "nt", BF16)
    dyb = _mm(f"d_yb{l}", dYb, W["w_sb_out"], "nt", BF16)
    dyc = _mm(f"d_yc{l}", dYc, W["w_ssm_out"], "nt", BF16)
    G["w_sc_out"] = _mm(f"gw_sc_out{l}", S["ya"], dYa, "tn", F32)
    G["w_sb_out"] = _mm(f"gw_sb_out{l}", S["yb"], dYb, "tn", F32)
    G["w_ssm_out"] = _mm(f"gw_ssm_out{l}", S["yc"], dYc, "tn", F32)

    def post_bwd(y, px, z, d, dfull, nw):
        _, vjp = jax.vjp(_ssm_post, y, px, _f(z), dfull, nw)
        return vjp(_f(d))

    dy_ssd, dxs, dz, G["d_full"], G["ssm_norm_w"] = _rowwise(
        f"ssm_post_bwd{l}", post_bwd,
        [(S["y_ssd"], SSM_INNER, 0), (S["pre"], SSM_INNER, 0), (P, SSM_INNER, P_Z // SSM_INNER), (dyc, SSM_INNER, 0)],
        [V["d_full"], V["norm_w"]], [(SSM_INNER, F32), (SSM_INNER, F32), (SSM_INNER, BF16)], [(1, SSM_INNER)] * 2)
    dpre, ddt, G["dtb"], G["alog"] = _ssd_bwd(S["pre"], P, S["states"], dy_ssd, dxs, V["dtb"], V["alog"])
    dxbc, w0, w1, w2, w3, G["ssm_conv_b"] = _ssmconv_bwd(P, dpre, V["ssm_w"])
    G["ssm_conv_w"] = jnp.concatenate([w0, w1, w2, w3], axis=0)
    dq, dk, dv = _sb_bwd(P, dyb)
    dA, s0, s1, s2 = _shortconv_bwd(P, dya, V["sc_w"])
    G["sc_conv_w"] = jnp.concatenate([s0, s1, s2], axis=0)
    dP = _rowwise(f"assemble_dp{l}", lambda *t: jnp.concatenate([v.astype(BF16) for v in t], axis=1),
                  [(dA, 3 * SC_WIDTH, 0), (dq, 256, 0), (dk, 256, 0), (dv, 256, 0), (dz, SSM_INNER, 0), (ddt, DT_PAD, 0),
                   (dxbc, SSM_CONV_DIM, 0), (dG, 3 * D_MODEL, 0)], [], [(P_WIDTH, BF16)])[0]
    dh = _mm(f"d_h{l}", dP, W["w_in"], "nt", BF16, tk_cap=2048)
    G["w_in"] = _mm(f"gw_in{l}", S["h"], dP, "tn", F32, tn_cap=1024)
    return dx, dh, G


def kernel(x, c, mod_w, mod_b, g_pre_mix, g_post_mix, g_pre_ffn, g_post_ffn, w_in, sc_conv_w, ssm_conv_w, ssm_conv_b, ssm_dt_bias, ssm_a_log, ssm_d, ssm_norm_w, w_sc_out, w_sb_out, w_ssm_out, w_o, w_ffn_in, w_ffn_out, loss_target, m_mod_w, m_mod_b, m_g_pre_mix, m_g_post_mix, m_g_pre_ffn, m_g_post_ffn, m_w_in, m_sc_conv_w, m_ssm_conv_w, m_ssm_conv_b, m_ssm_dt_bias, m_ssm_a_log, m_ssm_d, m_ssm_norm_w, m_w_sc_out, m_w_sb_out, m_w_ssm_out, m_w_o, m_w_ffn_in, m_w_ffn_out, v_mod_w, v_mod_b, v_g_pre_mix, v_g_post_mix, v_g_pre_ffn, v_g_post_ffn, v_w_in, v_sc_conv_w, v_ssm_conv_w, v_ssm_conv_b, v_ssm_dt_bias, v_ssm_a_log, v_ssm_d, v_ssm_norm_w, v_w_sc_out, v_w_sb_out, v_w_ssm_out, v_w_o, v_w_ffn_in, v_w_ffn_out):
    wts = dict(mod_w=mod_w, mod_b=mod_b, g_pre_mix=g_pre_mix, g_post_mix=g_post_mix, g_pre_ffn=g_pre_ffn,
               g_post_ffn=g_post_ffn, w_in=w_in, sc_conv_w=sc_conv_w, ssm_conv_w=ssm_conv_w, ssm_conv_b=ssm_conv_b,
               ssm_dt_bias=ssm_dt_bias, ssm_a_log=ssm_a_log, ssm_d=ssm_d, ssm_norm_w=ssm_norm_w, w_sc_out=w_sc_out,
               w_sb_out=w_sb_out, w_ssm_out=w_ssm_out, w_o=w_o, w_ffn_in=w_ffn_in, w_ffn_out=w_ffn_out)
    mom = dict(mod_w=m_mod_w, mod_b=m_mod_b, g_pre_mix=m_g_pre_mix, g_post_mix=m_g_post_mix, g_pre_ffn=m_g_pre_ffn,
               g_post_ffn=m_g_post_ffn, w_in=m_w_in, sc_conv_w=m_sc_conv_w, ssm_conv_w=m_ssm_conv_w,
               ssm_conv_b=m_ssm_conv_b, ssm_dt_bias=m_ssm_dt_bias, ssm_a_log=m_ssm_a_log, ssm_d=m_ssm_d,
               ssm_norm_w=m_ssm_norm_w, w_sc_out=m_w_sc_out, w_sb_out=m_w_sb_out, w_ssm_out=m_w_ssm_out, w_o=m_w_o,
               w_ffn_in=m_w_ffn_in, w_ffn_out=m_w_ffn_out)
    var = dict(mod_w=v_mod_w, mod_b=v_mod_b, g_pre_mix=v_g_pre_mix, g_post_mix=v_g_post_mix, g_pre_ffn=v_g_pre_ffn,
               g_post_ffn=v_g_post_ffn, w_in=v_w_in, sc_conv_w=v_sc_conv_w, ssm_conv_w=v_ssm_conv_w,
               ssm_conv_b=v_ssm_conv_b, ssm_dt_bias=v_ssm_dt_bias, ssm_a_log=v_ssm_a_log, ssm_d=v_ssm_d,
               ssm_norm_w=v_ssm_norm_w, w_sc_out=v_w_sc_out, w_sb_out=v_w_sb_out, w_ssm_out=v_w_ssm_out, w_o=v_w_o,
               w_ffn_in=v_w_ffn_in, w_ffn_out=v_w_ffn_out)
    xi, yi, ci = _here()
    chip = 2 * xi + yi
    me = 4 * xi + 2 * yi + ci
    x0, target = x[0], loss_target[0]

    first_shapes = [(D_MODEL,), sc_conv_w.shape, ssm_conv_w.shape]
    g0 = _allgather_small("gather_cond", _pack([c, sc_conv_w, ssm_conv_w]))
    c_rows, sc_sh, ssm_sh = _unpack(g0, first_shapes)
    c_all = c_rows
    sc_w = jnp.concatenate([sc_sh[2 * j] for j in range(N_CHIPS)], axis=-1)
    ssm_w = jnp.concatenate([ssm_sh[2 * j] for j in range(N_CHIPS)], axis=-1)

    mod_b_sh = lax.dynamic_slice_in_dim(mod_b, chip * MOD_SHARD, MOD_SHARD, axis=1).reshape(DEPTH, 1, MOD_SHARD)
    modpart = _mod_fwd(c_all, mod_w, mod_b_sh)
    g1 = _allgather_small("gather_mod", modpart.reshape(-1, LANES)).reshape(N_DEV, DEPTH, N_DEV, MOD_SHARD)
    mod = jnp.concatenate([lax.dynamic_index_in_dim(g1[2 * j], me, axis=1, keepdims=False) for j in range(N_CHIPS)],
                          axis=-1)

    gathered = _chip_exchange("gather_weights", [wts[n].astype(BF16) for n, _ in BIG], gather=True)
    full = {n: jnp.concatenate([g[j] for j in range(N_CHIPS)], axis=ax) for (n, ax), g in zip(BIG, gathered)}

    Ws, Vs = [], []
    for l in range(DEPTH):
        W = {n: full[n][l] for n, _ in BIG}
        W["w_in"] = _pad_w_in(W["w_in"])
        Ws.append(W)
        sh1, sc1, gt1, sh2, sc2, gt2 = [_row(v) for v in jnp.split(mod[l], N_MOD)]
        Vs.append(dict(
            shift1=sh1, scale1=sc1, g_pre_mix=_row(g_pre_mix[l]),
            mid_mix=[gt1, _row(g_post_mix[l]), _row(g_pre_ffn[l]), sc2, sh2],
            gate2=gt2, g_post_ffn=_row(g_post_ffn[l]),
            sc_w=sc_w[l], ssm_w=ssm_w[l], ssm_b=_row(ssm_conv_b[l]),
            dtb=_row(jnp.pad(ssm_dt_bias[l], (0, LANES - SSM_HEADS))), alog=_row(jnp.pad(ssm_a_log[l], (0, LANES - SSM_HEADS))),
            d_full=_row(jnp.repeat(ssm_d[l], SSM_INNER // SSM_HEADS)), norm_w=_row(ssm_norm_w[l])))

    def mid_ffn_vecs(l):
        return [Vs[l]["gate2"], Vs[l]["g_post_ffn"], Vs[l + 1]["g_pre_mix"], Vs[l + 1]["scale1"], Vs[l + 1]["shift1"]]

    saved = []
    x_in = x0
    h = _first_fwd(x0, [Vs[0]["g_pre_mix"], Vs[0]["scale1"], Vs[0]["shift1"]])
    for l in range(DEPTH):
        S = _layer_fwd(l, x_in, h, Ws[l], Vs[l])
        saved.append(S)
        if l + 1 < DEPTH:
            x_in, h = _mid_fwd(f"mid_ffn{l}", S["x1"], S["f"], mid_ffn_vecs(l))

    GL = [None] * DEPTH
    S = saved[-1]
    dx1, df, g_gate2, g_gpf, loss_cols = _last_bwd(S["x1"], S["f"], target, [Vs[-1]["gate2"], Vs[-1]["g_post_ffn"]])
    for l in reversed(range(DEPTH)):
        dx, dh, G = _layer_bwd(l, saved[l], Ws[l], Vs[l], dx1, df)
        G["gate2"], G["g_post_ffn"] = g_gate2, g_gpf
        GL[l] = G
        if l > 0:
            Sp = saved[l - 1]
            dx1, df, g_gate2, g_gpf, G["g_pre_mix"], G["scale1"], G["shift1"] = _mid_bwd(
                f"mid_ffn_bwd{l - 1}", Sp["x1"], Sp["f"], dx, dh, mid_ffn_vecs(l - 1))
        else:
            grad_x, G["g_pre_mix"], G["scale1"], G["shift1"] = _first_bwd(
                x0, dx, dh, [Vs[0]["g_pre_mix"], Vs[0]["scale1"], Vs[0]["shift1"]])
    loss = lax.psum(jnp.sum(loss_cols), ("x", "y", "c"))

    def both(key, shape=None):
        a = jnp.stack([GL[l][key] for l in range(DEPTH)])
        return a if shape is None else a.reshape(shape)

    dmod = jnp.concatenate([both(k, (DEPTH, D_MODEL)) for k in ("shift1", "scale1", "gate1", "shift2", "scale2", "gate2")],
                           axis=1)
    part_small = dict(
        mod_b=dmod, g_pre_mix=both("g_pre_mix", (DEPTH, D_MODEL)), g_post_mix=both("g_post_mix", (DEPTH, D_MODEL)),
        g_pre_ffn=both("g_pre_ffn", (DEPTH, D_MODEL)), g_post_ffn=both("g_post_ffn", (DEPTH, D_MODEL)),
        sc_conv_w=both("sc_conv_w"), ssm_conv_w=both("ssm_conv_w"), ssm_conv_b=both("ssm_conv_b", (DEPTH, SSM_CONV_DIM)),
        ssm_dt_bias=both("dtb", (DEPTH, LANES))[:, :SSM_HEADS], ssm_a_log=both("alog", (DEPTH, LANES))[:, :SSM_HEADS],
        ssm_d=both("d_full", (DEPTH, SSM_HEADS, SSM_INNER // SSM_HEADS)).sum(-1),
        ssm_norm_w=both("ssm_norm_w", (DEPTH, SSM_INNER)))
    small_shapes = [part_small[n].shape for n in SMALL]
    g2 = _allgather_small("gather_small_grads", _pack([part_small[n] for n in SMALL]))
    tot = dict(zip(SMALL, _unpack(_sum_slots("sum_small_grads", g2), small_shapes)))
    dmod_all = _unpack(g2, small_shapes)[0]
    dmod_sh = jnp.swapaxes(lax.dynamic_slice_in_dim(dmod_all, chip * MOD_SHARD, MOD_SHARD, axis=2), 0, 1)
    grads = {"mod_w": _mod_bwd(c_all, dmod_sh)}
    for n in SMALL:
        grads[n] = tot[n]
    grads["sc_conv_w"] = lax.dynamic_slice_in_dim(tot["sc_conv_w"], chip * 64, 64, axis=2)
    grads["ssm_conv_w"] = lax.dynamic_slice_in_dim(tot["ssm_conv_w"], chip * 192, 192, axis=2)

    pieces = []
    for n, ax in BIG:
        g = jnp.stack([_unpad_w_in(GL[l][n]) if n == "w_in" else GL[l][n] for l in range(DEPTH)])
        pieces.append(jnp.stack(jnp.split(g, N_CHIPS, axis=ax)))
    landed = _chip_exchange("scatter_grads", pieces, gather=False)
    mine = [_sum_slots(f"sum_{n}", a.reshape(N_CHIPS, -1, a.shape[-1])) for (n, _), a in zip(BIG, landed)]
    theirs = _sibling_exchange("swap_core_sums", mine)

    out = {}

    def update(name, w2, gs, m2, v2, shape):
        g, d, nm, nv = _adamw(f"adamw_{name}", w2, gs, m2, v2)
        out[name] = tuple(a.reshape(shape) for a in (g, d, nm, nv))

    for (n, _), a, b in zip(BIG, mine, theirs):
        shp = wts[n].shape
        two = (-1, shp[-1])
        update(n, wts[n].reshape(two), [a, b], mom[n].reshape(two), var[n].reshape(two), shp)
    two = (-1, MOD_SHARD)
    update("mod_w", mod_w.reshape(two), [grads["mod_w"].reshape(two)], m_mod_w.reshape(two), v_mod_w.reshape(two), mod_w.shape)
    shapes = [wts[n].shape for n in SMALL]
    res = _adamw("adamw_small", _pack([wts[n] for n in SMALL]), [_pack([grads[n] for n in SMALL])],
                 _pack([mom[n] for n in SMALL]), _pack([var[n] for n in SMALL]))
    for n, g, d, nm, nv in zip(SMALL, *[_unpack(r, shapes) for r in res]):
        out[n] = (g, d, nm, nv)

    result = [loss, grad_x[None]]
    for k in range(4):
        result += [out[n][k] for n in WEIGHT_ORDER]
    return tuple(result)
```

```python
import functools

import jax
import jax.numpy as jnp
from jax import lax
from jax.experimental import pallas as pl
from jax.experimental.pallas import tpu as pltpu

F32 = jnp.float32
BF16 = jnp.bfloat16
HIGHEST = lax.Precision.HIGHEST
MESH_ID = pl.DeviceIdType.MESH

D_MODEL = 1024
DEPTH = 2
SC_WIDTH = 256
SC_KERNEL = 3
SB_HEAD_DIM = 64
SSM_INNER = 512
SSM_HEADS = 8
SSM_STATE = 64
SSM_CONV = 4
SSM_CHUNK = 256
SSM_CONV_DIM = 768
FFN_HIDDEN = 2816
NORM_EPS = 1e-6
N_MOD = 6
N_CHIPS = 4
N_DEV = 8

ADAM_LR = 0.001
ADAM_B1 = 0.9
ADAM_B2 = 0.999
ADAM_EPS = 1e-08
ADAM_WD = 0.01
ADAM_STEP = 10

P_WIDTH = 6144
P_A, P_B, P_Z, P_DT, P_XBC, P_G = 0, 768, 1536, 2048, 2304, 3072
DT_PAD = 256

VMEM_LIMIT_BYTES = 56 * 1024 * 1024
LANES = 128

SB_LOG_CUTOFF = -105.0
SB_BLOCK = 256


def _params(sem):
    return pltpu.CompilerParams(dimension_semantics=sem, vmem_limit_bytes=VMEM_LIMIT_BYTES)


def _pick(n, cap):
    if n <= cap:
        return n
    best = None
    for m in range(LANES, cap + 1, LANES):
        if n % m == 0:
            best = m
    assert best is not None, (n, cap)
    return best


def _rowwise(name, fn, rows, vecs, row_outs, acc_outs=(), tl=256):
    L = rows[0][0].shape[0]
    tl = min(tl, L)
    assert L % tl == 0
    n_in = len(rows) + len(vecs)
    n_ro = len(row_outs)

    def body(*refs):
        ins, ro, ao = refs[:n_in], refs[n_in:n_in + n_ro], refs[n_in + n_ro:]
        vals = fn(*[r[...] for r in ins])
        if not isinstance(vals, (tuple, list)):
            vals = (vals,)
        for o, v in zip(ro, vals[:n_ro]):
            o[...] = v.astype(o.dtype)
        if ao:
            @pl.when(pl.program_id(0) == 0)
            def _():
                for o in ao:
                    o[...] = jnp.zeros_like(o)
            for o, v in zip(ao, vals[n_ro:]):
                o[...] += v.astype(F32)

    in_specs = [pl.BlockSpec((tl, w), functools.partial(lambda i, cb: (i, cb), cb=cb)) for _, w, cb in rows]
    in_specs += [pl.BlockSpec(v.shape, lambda i: (0, 0)) for v in vecs]
    out_specs = [pl.BlockSpec((tl, w), lambda i: (i, 0)) for w, _ in row_outs]
    out_specs += [pl.BlockSpec(s, lambda i: (0, 0)) for s in acc_outs]
    out_shape = [jax.ShapeDtypeStruct((L, w), dt) for w, dt in row_outs]
    out_shape += [jax.ShapeDtypeStruct(s, F32) for s in acc_outs]
    return pl.pallas_call(
        body, name=name, grid=(L // tl,), in_specs=in_specs, out_specs=out_specs, out_shape=out_shape,
        compiler_params=_params(("arbitrary",)),
    )(*[a for a, _, _ in rows], *vecs)


def _mm(name, a, b, mode, out_dtype, tm=1024, tn_cap=1408, tk_cap=2816):
    if mode == "nn":
        (M, K), (_, N) = a.shape, b.shape
    elif mode == "nt":
        (M, K), (N, _) = a.shape, b.shape
    else:
        (K, M), (_, N) = a.shape, b.shape
        tm, tk_cap = 1408, 2048
    tm = _pick(M, tm)
    tn = _pick(N, tn_cap)
    tk = _pick(K, tk_cap)
    nk = K // tk

    def body(a_ref, b_ref, o_ref, *scr):
        if mode == "nn":
            p = jnp.dot(a_ref[...], b_ref[...], preferred_element_type=F32)
        elif mode == "nt":
            p = lax.dot_general(a_ref[...], b_ref[...], (((1,), (1,)), ((), ())), preferred_element_type=F32)
        else:
            p = lax.dot_general(a_ref[...], b_ref[...], (((0,), (0,)), ((), ())), preferred_element_type=F32)
        if nk == 1:
            o_ref[...] = p.astype(o_ref.dtype)
        else:
            acc = scr[0]
            k = pl.program_id(2)

            @pl.when(k == 0)
            def _():
                acc[...] = p

            @pl.when(k > 0)
            def _():
                acc[...] += p

            @pl.when(k == nk - 1)
            def _():
                o_ref[...] = acc[...].astype(o_ref.dtype)

    if mode == "nn":
        a_spec = pl.BlockSpec((tm, tk), lambda i, j, k: (i, k))
        b_spec = pl.BlockSpec((tk, tn), lambda i, j, k: (k, j))
    elif mode == "nt":
        a_spec = pl.BlockSpec((tm, tk), lambda i, j, k: (i, k))
        b_spec = pl.BlockSpec((tn, tk), lambda i, j, k: (j, k))
    else:
        a_spec = pl.BlockSpec((tk, tm), lambda i, j, k: (k, i))
        b_spec = pl.BlockSpec((tk, tn), lambda i, j, k: (k, j))
    return pl.pallas_call(
        body, name=name, grid=(M // tm, N // tn, nk), in_specs=[a_spec, b_spec],
        out_specs=pl.BlockSpec((tm, tn), lambda i, j, k: (i, j)),
        out_shape=jax.ShapeDtypeStruct((M, N), out_dtype),
        scratch_shapes=[pltpu.VMEM((tm, tn), F32)] if nk > 1 else [],
        compiler_params=_params(("arbitrary", "arbitrary", "arbitrary")),
    )(a, b)


def _f(x):
    return x.astype(F32)


def _silu(x):
    return x * jax.nn.sigmoid(x)


def _softplus(x):
    return jnp.maximum(x, 0.0) + jnp.log1p(jnp.exp(-jnp.abs(x)))


def _rms(x, g):
    r = lax.rsqrt(jnp.mean(x * x, axis=-1, keepdims=True) + NORM_EPS)
    return x * r * g


def _adaln(x, g, scale, shift):
    return _rms(x, g) * (1.0 + scale) + shift


def _resid(x, y, gate, g):
    return x + gate * _rms(y, g)


def _mid(x, y, gate, g_post, g_pre, scale, shift):
    x_new = _resid(x, y, gate, g_post)
    return x_new, _adaln(x_new, g_pre, scale, shift)


def _merge(ga, gb, gc, ya, yb, yc):
    return jax.nn.sigmoid(ga) * ya + jax.nn.sigmoid(gb) * yb + jax.nn.sigmoid(gc) * yc


def _swiglu(gt, up):
    return _silu(gt) * up


def _ssm_post(y_ssd, pre_xs, z, d_full, norm_w):
    y = (y_ssd + _silu(pre_xs) * d_full) * _silu(z)
    half = SSM_INNER // 2
    parts = []
    for g in range(2):
        yg = y[:, g * half:(g + 1) * half]
        parts.append(yg * lax.rsqrt(jnp.mean(yg * yg, axis=-1, keepdims=True) + NORM_EPS))
    return jnp.concatenate(parts, axis=1) * norm_w


def _first_fwd(x, vecs):
    return _rowwise("adaln_first", lambda x, g, sc, sh: _adaln(x, g, sc, sh),
                    [(x, D_MODEL, 0)], vecs, [(D_MODEL, BF16)], tl=512)[0]


def _mid_fwd(name, x, y, vecs):
    return _rowwise(name, lambda x, y, *v: _mid(x, _f(y), *v),
                    [(x, D_MODEL, 0), (y, D_MODEL, 0)], vecs, [(D_MODEL, F32), (D_MODEL, BF16)], tl=512)


def _mid_bwd(name, x, y, dx_new, dh, vecs):
    def fn(x, y, dxn, dh, *v):
        _, vjp = jax.vjp(_mid, x, _f(y), *v)
        return vjp((dxn, _f(dh)))

    vec = (1, D_MODEL)
    return _rowwise(name, fn, [(x, D_MODEL, 0), (y, D_MODEL, 0), (dx_new, D_MODEL, 0), (dh, D_MODEL, 0)], vecs,
                    [(D_MODEL, F32), (D_MODEL, BF16)], [vec] * 5)


def _first_bwd(x, dx_in, dh, vecs):
    def fn(x, dxi, dh, *v):
        _, vjp = jax.vjp(_adaln, x, *v)
        dx, dg, dsc, dsh = vjp(_f(dh))
        return dx + dxi, dg, dsc, dsh

    vec = (1, D_MODEL)
    return _rowwise("adaln_first_bwd", fn, [(x, D_MODEL, 0), (dx_in, D_MODEL, 0), (dh, D_MODEL, 0)], vecs,
                    [(D_MODEL, F32)], [vec] * 3)


def _last_bwd(x1, f, target, vecs):
    def fn(x1, f, t, gate, g):
        x2, vjp = jax.vjp(_resid, x1, _f(f), gate, g)
        err = x2 - t
        dx1, df, dgate, dg = vjp(err * (1.0 / D_MODEL))
        loss_cols = jnp.sum(err * err, axis=0, keepdims=True) * (0.5 / D_MODEL)
        return dx1, df, dgate, dg, loss_cols

    vec = (1, D_MODEL)
    return _rowwise("loss_last_bwd", fn, [(x1, D_MODEL, 0), (f, D_MODEL, 0), (target, D_MODEL, 0)], vecs,
                    [(D_MODEL, F32), (D_MODEL, BF16)], [vec] * 3)


HALO = 16


def _shift_down(u, prev, k):
    rows = lax.broadcasted_iota(jnp.int32, u.shape, 0)
    v = pltpu.roll(u, k, 0)
    for t in range(k):
        v = jnp.where(rows == t, prev[HALO - k + t:HALO - k + t + 1, :], v)
    return v


def _shift_up(u, nxt, k):
    n = u.shape[0]
    rows = lax.broadcasted_iota(jnp.int32, u.shape, 0)
    v = pltpu.roll(u, n - k, 0)
    for t in range(k):
        v = jnp.where(rows == n - k + t, nxt[t:t + 1, :], v)
    return v


def _conv_specs(L, tl, width, col_block):
    per = tl // HALO
    last = L // HALO - 1
    main = pl.BlockSpec((tl, width), lambda i: (i, col_block))
    before = pl.BlockSpec((HALO, width), lambda i: (jnp.maximum(i * per - 1, 0), col_block))
    after = pl.BlockSpec((HALO, width), lambda i: (jnp.minimum((i + 1) * per, last), col_block))
    return main, before, after


def _shortconv_fwd(P, w, tl=512):
    L = P.shape[0]
    tl = min(tl, L)
    C = SC_WIDTH
    main, before, _ = _conv_specs(L, tl, 3 * C, 0)

    def body(p_ref, h_ref, w_ref, o_ref):
        first = (pl.program_id(0) == 0)
        p, h = _f(p_ref[...]), _f(h_ref[...])
        b, u = p[:, :C], p[:, C:2 * C] * p[:, 2 * C:]
        uh = jnp.where(first, 0.0, h[:, C:2 * C] * h[:, 2 * C:])
        wv = w_ref[...]
        cv = wv[2:3] * u + wv[1:2] * _shift_down(u, uh, 1) + wv[0:1] * _shift_down(u, uh, 2)
        o_ref[...] = (b * cv).astype(o_ref.dtype)

    return pl.pallas_call(
        body, name="shortconv_fwd", grid=(L // tl,),
        in_specs=[main, before, pl.BlockSpec(w.shape, lambda i: (0, 0))],
        out_specs=pl.BlockSpec((tl, C), lambda i: (i, 0)),
        out_shape=jax.ShapeDtypeStruct((L, C), BF16), compiler_params=_params(("arbitrary",)),
    )(P, P, w)


def _shortconv_bwd(P, dya, w, tl=512):
    L = P.shape[0]
    tl = min(tl, L)
    C = SC_WIDTH
    main, before, after = _conv_specs(L, tl, 3 * C, 0)
    dmain, _, dafter = _conv_specs(L, tl, C, 0)
    n = L // tl

    def body(p_ref, h_ref, n_ref, d_ref, dn_ref, w_ref, o_ref, dw0, dw1, dw2):
        i = pl.program_id(0)
        p, h, nx = _f(p_ref[...]), _f(h_ref[...]), _f(n_ref[...])
        b, c, x = p[:, :C], p[:, C:2 * C], p[:, 2 * C:]
        u = c * x
        uh = jnp.where(i == 0, 0.0, h[:, C:2 * C] * h[:, 2 * C:])
        u1, u2 = _shift_down(u, uh, 1), _shift_down(u, uh, 2)
        wv = w_ref[...]
        cv = wv[2:3] * u + wv[1:2] * u1 + wv[0:1] * u2
        dy = _f(d_ref[...])
        dcv = dy * b
        dcv_n = jnp.where(i == n - 1, 0.0, _f(dn_ref[...]) * nx[:, :C])
        du = wv[2:3] * dcv + wv[1:2] * _shift_up(dcv, dcv_n, 1) + wv[0:1] * _shift_up(dcv, dcv_n, 2)
        o_ref[:, :C] = (dy * cv).astype(o_ref.dtype)
        o_ref[:, C:2 * C] = (du * x).astype(o_ref.dtype)
        o_ref[:, 2 * C:] = (du * c).astype(o_ref.dtype)

        @pl.when(i == 0)
        def _():
            for r in (dw0, dw1, dw2):
                r[...] = jnp.zeros_like(r)

        dw0[...] += jnp.sum(dcv * u2, axis=0, keepdims=True)
        dw1[...] += jnp.sum(dcv * u1, axis=0, keepdims=True)
        dw2[...] += jnp.sum(dcv * u, axis=0, keepdims=True)

    vec = pl.BlockSpec((1, C), lambda i: (0, 0))
    return pl.pallas_call(
        body, name="shortconv_bwd", grid=(n,),
        in_specs=[main, before, after, dmain, dafter, pl.BlockSpec(w.shape, lambda i: (0, 0))],
        out_specs=[pl.BlockSpec((tl, 3 * C), lambda i: (i, 0)), vec, vec, vec],
        out_shape=[jax.ShapeDtypeStruct((L, 3 * C), BF16)] + [jax.ShapeDtypeStruct((1, C), F32)] * 3,
        compiler_params=_params(("arbitrary",)),
    )(P, P, P, dya, dya, w)


def _ssmconv_fwd(P, w, bias, tl=512):
    L = P.shape[0]
    tl = min(tl, L)
    C = SSM_CONV_DIM
    main, before, _ = _conv_specs(L, tl, C, P_XBC // C)

    def body(p_ref, h_ref, w_ref, b_ref, o_ref):
        u = _f(p_ref[...])
        uh = jnp.where(pl.program_id(0) == 0, 0.0, _f(h_ref[...]))
        wv = w_ref[...]
        acc = wv[3:4] * u + b_ref[...]
        for k in range(1, SSM_CONV):
            acc = acc + wv[3 - k:4 - k] * _shift_down(u, uh, k)
        o_ref[...] = acc

    return pl.pallas_call(
        body, name="ssmconv_fwd", grid=(L // tl,),
        in_specs=[main, before, pl.BlockSpec(w.shape, lambda i: (0, 0)), pl.BlockSpec(bias.shape, lambda i: (0, 0))],
        out_specs=pl.BlockSpec((tl, C), lambda i: (i, 0)),
        out_shape=jax.ShapeDtypeStruct((L, C), F32), compiler_params=_params(("arbitrary",)),
    )(P, P, w, bias)


def _ssmconv_bwd(P, dpre, w, tl=512):
    L = P.shape[0]
    tl = min(tl, L)
    C = SSM_CONV_DIM
    main, before, _ = _conv_specs(L, tl, C, P_XBC // C)
    dmain, _, dafter = _conv_specs(L, tl, C, 0)
    n = L // tl

    def body(p_ref, h_ref, d_ref, dn_ref, w_ref, o_ref, dw0, dw1, dw2, dw3, db):
        i = pl.program_id(0)
        u = _f(p_ref[...])
        uh = jnp.where(i == 0, 0.0, _f(h_ref[...]))
        d = d_ref[...]
        dn = jnp.where(i == n - 1, 0.0, dn_ref[...])
        wv = w_ref[...]
        du = wv[3:4] * d
        for k in range(1, SSM_CONV):
            du = du + wv[3 - k:4 - k] * _shift_up(d, dn, k)
        o_ref[...] = du.astype(o_ref.dtype)

        @pl.when(i == 0)
        def _():
            for r in (dw0, dw1, dw2, dw3, db):
                r[...] = jnp.zeros_like(r)

        for k, r in ((3, dw0), (2, dw1), (1, dw2)):
            r[...] += jnp.sum(d * _shift_down(u, uh, k), axis=0, keepdims=True)
        dw3[...] += jnp.sum(d * u, axis=0, keepdims=True)
        db[...] += jnp.sum(d, axis=0, keepdims=True)

    vec = pl.BlockSpec((1, C), lambda i: (0, 0))
    return pl.pallas_call(
        body, name="ssmconv_bwd", grid=(n,),
        in_specs=[main, before, dmain, dafter, pl.BlockSpec(w.shape, lambda i: (0, 0))],
        out_specs=[pl.BlockSpec((tl, C), lambda i: (i, 0))] + [vec] * 5,
        out_shape=[jax.ShapeDtypeStruct((L, C), BF16)] + [jax.ShapeDtypeStruct((1, C), F32)] * 5,
        compiler_params=_params(("arbitrary",)),
    )(P, P, dpre, dpre, w)


def _dot_nt(a, b):
    return lax.dot_general(a, b, (((1,), (1,)), ((), ())), preferred_element_type=F32)


def _dot_tn(a, b):
    return lax.dot_general(a, b, (((0,), (0,)), ((), ())), preferred_element_type=F32)


def _split3(x):
    hi = x.astype(BF16)
    r = x - hi.astype(F32)
    mid = r.astype(BF16)
    return hi, mid, (r - mid.astype(F32)).astype(BF16)


@jax.custom_vjp
def _xm01(x, m):
    return sum(jnp.dot(t, m, preferred_element_type=F32) for t in _split3(x))


def _xm01_fwd(x, m):
    return _xm01(x, m), m


def _xm01_bwd(m, g):
    return sum(_dot_nt(t, m) for t in _split3(g)), jnp.zeros_like(m)


_xm01.defvjp(_xm01_fwd, _xm01_bwd)


@jax.custom_vjp
def _m01x(m, x):
    return sum(jnp.dot(m, t, preferred_element_type=F32) for t in _split3(x))


def _m01x_fwd(m, x):
    return _m01x(m, x), m


def _m01x_bwd(m, g):
    return jnp.zeros_like(m), sum(_dot_tn(m, t) for t in _split3(g))


_m01x.defvjp(_m01x_fwd, _m01x_bwd)


def _ssd_chunk(pre, dtr, s_prev, dtb, alog):
    T = pre.shape[0]
    act = _silu(pre)
    xs, bm, cm = act[:, :SSM_INNER], act[:, SSM_INNER:SSM_INNER + 128], act[:, SSM_INNER + 128:]
    lane = lax.broadcasted_iota(jnp.int32, (1, LANES), 1)
    dt = jnp.where(lane < SSM_HEADS, _softplus(dtr + dtb), 0.0)
    a = dt * (-jnp.exp(alog))
    ri = lax.broadcasted_iota(jnp.int32, (T, T), 0)
    ci = lax.broadcasted_iota(jnp.int32, (T, T), 1)
    causal = ci <= ri
    a_cs = _m01x(causal.astype(BF16), a)
    eh = lax.broadcasted_iota(jnp.int32, (LANES, SSM_INNER), 0)
    ej = lax.broadcasted_iota(jnp.int32, (LANES, SSM_INNER), 1)
    expand = (lax.shift_right_logical(ej, 6) == eh).astype(BF16)
    dt_full = _xm01(dt, expand)
    acs_full = _xm01(a_cs, expand)
    alast_full = acs_full[T - 1:T, :]
    xdt = xs * dt_full
    a_cs_t = a_cs.T
    ys, s_new = [], []
    for g in range(2):
        in_group = lax.shift_right_logical(lane, 6) == g
        cg = jnp.where(in_group, cm, 0.0).astype(BF16)
        bg = jnp.where(in_group, bm, 0.0).astype(BF16)
        scores = _dot_nt(cg, bg)
        for pp in range(2):
            hp = 2 * g + pp
            cols = slice(hp * LANES, (hp + 1) * LANES)
            xp, acsp = xdt[:, cols], acs_full[:, cols]
            per_head = []
            for hh in range(2):
                h = 2 * hp + hh
                decay = jnp.exp(jnp.where(causal, a_cs[:, h:h + 1] - a_cs_t[h:h + 1, :], -jnp.inf))
                per_head.append(jnp.dot((scores * decay).astype(BF16), xp.astype(BF16), preferred_element_type=F32))
            y_diag = jnp.where(lane < SSM_STATE, per_head[0], per_head[1])
            sp = s_prev[hp * LANES:(hp + 1) * LANES, :]
            y_off = jnp.dot(cg, sp.astype(BF16), preferred_element_type=F32) * jnp.exp(acsp)
            ys.append(y_diag + y_off)
            to_end = jnp.exp(alast_full[:, cols] - acsp)
            s_new.append(sp * jnp.exp(alast_full[:, cols]) + _dot_tn(bg, (xp * to_end).astype(BF16)))
    return jnp.concatenate(ys, axis=1), jnp.concatenate(s_new, axis=0)


def _ssd_fwd(pre, P, dtb, alog):
    L = pre.shape[0]
    T = min(SSM_CHUNK, L)
    nc = L // T

    def body(pre_ref, dt_ref, dtb_ref, al_ref, y_ref, st_ref, s_scr):
        @pl.when(pl.program_id(0) == 0)
        def _():
            s_scr[...] = jnp.zeros_like(s_scr)

        st_ref[0] = s_scr[...]
        y, s = _ssd_chunk(pre_ref[...], _f(dt_ref[...]), s_scr[...], dtb_ref[...], al_ref[...])
        y_ref[...] = y
        s_scr[...] = s

    vec = pl.BlockSpec((1, LANES), lambda i: (0, 0))
    return pl.pallas_call(
        body, name="ssd_fwd", grid=(nc,),
        in_specs=[pl.BlockSpec((T, SSM_CONV_DIM), lambda i: (i, 0)), pl.BlockSpec((T, LANES), lambda i: (i, P_DT // LANES)),
                  vec, vec],
        out_specs=[pl.BlockSpec((T, SSM_INNER), lambda i: (i, 0)), pl.BlockSpec((1, 512, LANES), lambda i: (i, 0, 0))],
        out_shape=[jax.ShapeDtypeStruct((L, SSM_INNER), F32), jax.ShapeDtypeStruct((nc, 512, LANES), F32)],
        scratch_shapes=[pltpu.VMEM((512, LANES), F32)], compiler_params=_params(("arbitrary",)),
    )(pre, P, dtb, alog)


def _ssd_bwd(pre, P, states, dy, dxs_extra, dtb, alog):
    L = pre.shape[0]
    T = min(SSM_CHUNK, L)
    nc = L // T

    def body(pre_ref, dt_ref, st_ref, dy_ref, dx_ref, dtb_ref, al_ref, dpre_ref, ddt_ref, ddtb_ref, dal_ref, ds_scr):
        @pl.when(pl.program_id(0) == 0)
        def _():
            ds_scr[...] = jnp.zeros_like(ds_scr)
            ddtb_ref[...] = jnp.zeros_like(ddtb_ref)
            dal_ref[...] = jnp.zeros_like(dal_ref)

        _, vjp = jax.vjp(_ssd_chunk, pre_ref[...], _f(dt_ref[...]), st_ref[0], dtb_ref[...], al_ref[...])
        dpre, ddt, ds, ddtb, dal = vjp((dy_ref[...], ds_scr[...]))
        dpre_ref[:, :SSM_INNER] = dpre[:, :SSM_INNER] + dx_ref[...]
        dpre_ref[:, SSM_INNER:] = dpre[:, SSM_INNER:]
        ddt_ref[:, :LANES] = ddt.astype(ddt_ref.dtype)
        ddt_ref[:, LANES:] = jnp.zeros((T, DT_PAD - LANES), ddt_ref.dtype)
        ds_scr[...] = ds
        ddtb_ref[...] += ddtb
        dal_ref[...] += dal

    vec = pl.BlockSpec((1, LANES), lambda i: (0, 0))
    rev = lambda i: (nc - 1 - i, 0)
    return pl.pallas_call(
        body, name="ssd_bwd", grid=(nc,),
        in_specs=[pl.BlockSpec((T, SSM_CONV_DIM), rev), pl.BlockSpec((T, LANES), lambda i: (nc - 1 - i, P_DT // LANES)),
                  pl.BlockSpec((1, 512, LANES), lambda i: (nc - 1 - i, 0, 0)),
                  pl.BlockSpec((T, SSM_INNER), rev), pl.BlockSpec((T, SSM_INNER), rev), vec, vec],
        out_specs=[pl.BlockSpec((T, SSM_CONV_DIM), rev), pl.BlockSpec((T, DT_PAD), rev), vec, vec],
        out_shape=[jax.ShapeDtypeStruct((L, SSM_CONV_DIM), F32), jax.ShapeDtypeStruct((L, DT_PAD), BF16),
                   jax.ShapeDtypeStruct((1, LANES), F32), jax.ShapeDtypeStruct((1, LANES), F32)],
        scratch_shapes=[pltpu.VMEM((512, LANES), F32)], compiler_params=_params(("arbitrary",)),
    )(pre, P, states, dy, dxs_extra, dtb, alog)


def _sb_tile(qm, kb, later, rel, ri, ci, strict):
    z = _dot_nt(qm, kb) * (SB_HEAD_DIM ** -0.5)
    mask = (ci + rel) < ri
    lk = jnp.where(mask, -_softplus(z), 0.0)
    log_a = jnp.where(mask, z + lk + _dot_split(lk, strict) + later, -jnp.inf)
    return z, mask, lk, log_a


def _dot_split(x, m):
    hi = x.astype(BF16)
    lo = (x - hi.astype(F32)).astype(BF16)
    return jnp.dot(hi, m, preferred_element_type=F32) + jnp.dot(lo, m, preferred_element_type=F32)


def _sb_consts(tq):
    lane = lax.broadcasted_iota(jnp.int32, (1, LANES), 1)
    ri = lax.broadcasted_iota(jnp.int32, (tq, tq), 0)
    ci = lax.broadcasted_iota(jnp.int32, (tq, tq), 1)
    strict = (ri > ci).astype(BF16)
    return lane, ri, ci, strict


def _sb_fwd(P):
    L = P.shape[0]
    tq = min(SB_BLOCK, L)
    nq = L // tq
    qb = P_B // LANES

    def body(q_ref, k_ref, v_ref, o_ref):
        i = pl.program_id(1)
        lane, ri, ci, strict = _sb_consts(tq)
        q = q_ref[...]
        first = lane < SB_HEAD_DIM
        qms = (jnp.where(first, q, jnp.zeros_like(q)), jnp.where(first, jnp.zeros_like(q), q))

        def cond(c):
            return jnp.logical_and(c[0] >= 0, jnp.maximum(jnp.max(c[1][0]), jnp.max(c[1][1])) > SB_LOG_CUTOFF)

        def step(c):
            j, laters, accs = c
            off = pl.multiple_of(j * tq, tq)
            kb, vb = k_ref[pl.ds(off, tq), :], v_ref[pl.ds(off, tq), :]
            new_l, new_a = [], []
            for qm, later, acc in zip(qms, laters, accs):
                _, _, lk, log_a = _sb_tile(qm, kb, later, (j - i) * tq, ri, ci, strict)
                new_a.append(acc + jnp.dot(jnp.exp(log_a).astype(BF16), vb, preferred_element_type=F32))
                new_l.append(later + jnp.sum(lk, axis=1, keepdims=True))
            return j - 1, tuple(new_l), tuple(new_a)

        zero, zacc = jnp.zeros((tq, 1), F32), jnp.zeros((tq, LANES), F32)
        _, _, accs = lax.while_loop(cond, step, (i, (zero, zero), (zacc, zacc)))
        o_ref[...] = jnp.where(first, accs[0], accs[1]).astype(o_ref.dtype)

    return pl.pallas_call(
        body, name="sb_fwd", grid=(2, nq),
        in_specs=[pl.BlockSpec((tq, LANES), lambda p, i: (i, qb + p)),
                  pl.BlockSpec((L, LANES), lambda p, i: (0, qb + 2 + p)),
                  pl.BlockSpec((L, LANES), lambda p, i: (0, qb + 4 + p))],
        out_specs=pl.BlockSpec((tq, LANES), lambda p, i: (i, p)),
        out_shape=jax.ShapeDtypeStruct((L, 2 * LANES), BF16),
        compiler_params=_params(("arbitrary", "arbitrary")),
    )(P, P, P)


def _sb_bwd(P, dyb):
    L = P.shape[0]
    tq = min(SB_BLOCK, L)
    nq = L // tq
    qb = P_B // LANES

    def body(q_ref, k_ref, v_ref, do_ref, dq_ref, dk_ref, dv_ref):
        i = pl.program_id(1)
        lane, ri, ci, strict = _sb_consts(tq)

        @pl.when(i == 0)
        def _():
            dk_ref[...] = jnp.zeros_like(dk_ref)
            dv_ref[...] = jnp.zeros_like(dv_ref)

        q = q_ref[...]
        do = do_ref[...]
        first = lane < SB_HEAD_DIM
        qms = (jnp.where(first, q, jnp.zeros_like(q)), jnp.where(first, jnp.zeros_like(q), q))
        doms = (jnp.where(first, do, jnp.zeros_like(do)), jnp.where(first, jnp.zeros_like(do), do))

        def tile(h, j, later):
            off = pl.multiple_of(j * tq, tq)
            kb, vb = k_ref[pl.ds(off, tq), :], v_ref[pl.ds(off, tq), :]
            z, mask, lk, log_a = _sb_tile(qms[h], kb, later, (j - i) * tq, ri, ci, strict)
            att = jnp.exp(log_a)
            return off, kb, z, mask, lk, att, att * _dot_nt(doms[h], vb)

        def cond(c):
            return jnp.logical_and(c[0] >= 0, jnp.maximum(jnp.max(c[1][0]), jnp.max(c[1][1])) > SB_LOG_CUTOFF)

        def count(c):
            j, laters, later_gs = c
            new_l, new_g = [], []
            for h in range(2):
                _, _, _, _, lk, _, g = tile(h, j, laters[h])
                new_l.append(laters[h] + jnp.sum(lk, axis=1, keepdims=True))
                new_g.append(later_gs[h] + jnp.sum(g, axis=1, keepdims=True))
            return j - 1, tuple(new_l), tuple(new_g)

        def step(c):
            j, laters, later_gs, accs = c
            new_l, new_g, new_a = [], [], []
            dk = dv = None
            for h in range(2):
                off, kb, z, mask, lk, att, g = tile(h, j, laters[h])
                g_after = _dot_split(g, strict) + later_gs[h]
                dz = jnp.where(mask, g - (totals[h] - g_after) * jnp.exp(z + lk), 0.0) * (SB_HEAD_DIM ** -0.5)
                dzb = dz.astype(BF16)
                dk_h, dv_h = _dot_tn(dzb, qms[h]), _dot_tn(att.astype(BF16), doms[h])
                dk, dv = (dk_h, dv_h) if h == 0 else (dk + dk_h, dv + dv_h)
                new_a.append(accs[h] + jnp.dot(dzb, kb, preferred_element_type=F32))
                new_l.append(laters[h] + jnp.sum(lk, axis=1, keepdims=True))
                new_g.append(later_gs[h] + jnp.sum(g, axis=1, keepdims=True))
            dk_ref[pl.ds(off, tq), :] += dk
            dv_ref[pl.ds(off, tq), :] += dv
            return j - 1, tuple(new_l), tuple(new_g), tuple(new_a)

        zero, zacc = jnp.zeros((tq, 1), F32), jnp.zeros((tq, LANES), F32)
        _, _, totals = lax.while_loop(cond, count, (i, (zero, zero), (zero, zero)))
        _, _, _, accs = lax.while_loop(cond, step, (i, (zero, zero), (zero, zero), (zacc, zacc)))
        dq_ref[...] = jnp.where(first, accs[0], accs[1])

    full = pl.BlockSpec((L, LANES), lambda p, i: (0, p))
    tile_spec = pl.BlockSpec((tq, LANES), lambda p, i: (i, p))
    return pl.pallas_call(
        body, name="sb_bwd", grid=(2, nq),
        in_specs=[pl.BlockSpec((tq, LANES), lambda p, i: (i, qb + p)),
                  pl.BlockSpec((L, LANES), lambda p, i: (0, qb + 2 + p)),
                  pl.BlockSpec((L, LANES), lambda p, i: (0, qb + 4 + p)), tile_spec],
        out_specs=[tile_spec, full, full],
        out_shape=[jax.ShapeDtypeStruct((L, 2 * LANES), F32)] * 3,
        compiler_params=_params(("arbitrary", "arbitrary")),
    )(P, P, P, dyb)


MOD_SHARD = N_MOD * D_MODEL // N_CHIPS


def _mod_fwd(c_all, mod_w, mod_b_sh):
    tn = 512

    def body(c_ref, w_ref, b_ref, o_ref):
        o_ref[0] = jnp.dot(_silu(c_ref[...]), w_ref[0], precision=HIGHEST, preferred_element_type=F32) + b_ref[0]

    return pl.pallas_call(
        body, name="mod_fwd", grid=(DEPTH, MOD_SHARD // tn),
        in_specs=[pl.BlockSpec((N_DEV, D_MODEL), lambda l, j: (0, 0)),
                  pl.BlockSpec((1, D_MODEL, tn), lambda l, j: (l, 0, j)),
                  pl.BlockSpec((1, 1, tn), lambda l, j: (l, 0, j))],
        out_specs=pl.BlockSpec((1, N_DEV, tn), lambda l, j: (l, 0, j)),
        out_shape=jax.ShapeDtypeStruct((DEPTH, N_DEV, MOD_SHARD), F32),
        compiler_params=_params(("arbitrary", "arbitrary")),
    )(c_all, mod_w, mod_b_sh)


def _mod_bwd(c_all, dmod_sh):
    tn = 512

    def body(c_ref, d_ref, o_ref):
        o_ref[0] = lax.dot_general(_silu(c_ref[...]), d_ref[0], (((0,), (0,)), ((), ())), precision=HIGHEST,
                                   preferred_element_type=F32)

    return pl.pallas_call(
        body, name="mod_bwd", grid=(DEPTH, MOD_SHARD // tn),
        in_specs=[pl.BlockSpec((N_DEV, D_MODEL), lambda l, j: (0, 0)),
                  pl.BlockSpec((1, N_DEV, tn), lambda l, j: (l, 0, j))],
        out_specs=pl.BlockSpec((1, D_MODEL, tn), lambda l, j: (l, 0, j)),
        out_shape=jax.ShapeDtypeStruct((DEPTH, D_MODEL, MOD_SHARD), F32),
        compiler_params=_params(("arbitrary", "arbitrary")),
    )(c_all, dmod_sh)


def _row_tile(rows, cap):
    if rows <= cap:
        return rows
    best = None
    for t in range(8, cap + 1, 8):
        if rows % t == 0:
            best = t
    assert best is not None, (rows, cap)
    return best


def _adamw(name, w, gs, m, v, tr=256):
    R, W = w.shape
    tr = _row_tile(R, tr)
    ng = len(gs)

    def body(*refs):
        w_ref, g_refs, (m_ref, v_ref) = refs[0], refs[1:1 + ng], refs[1 + ng:3 + ng]
        g_out, d_out, m_out, v_out = refs[3 + ng:]
        g = g_refs[0][...]
        for r in g_refs[1:]:
            g = g + r[...]
        mm = ADAM_B1 * m_ref[...] + (1.0 - ADAM_B1) * g
        vv = ADAM_B2 * v_ref[...] + (1.0 - ADAM_B2) * (g * g)
        m_hat = mm / (1.0 - ADAM_B1 ** ADAM_STEP)
        v_hat = vv / (1.0 - ADAM_B2 ** ADAM_STEP)
        g_out[...] = g
        d_out[...] = -ADAM_LR * (m_hat / (jnp.sqrt(v_hat) + ADAM_EPS) + ADAM_WD * w_ref[...])
        m_out[...] = mm
        v_out[...] = vv

    spec = pl.BlockSpec((tr, W), lambda i: (i, 0))
    return pl.pallas_call(
        body, name=name, grid=(R // tr,), in_specs=[spec] * (3 + ng), out_specs=[spec] * 4,
        out_shape=[jax.ShapeDtypeStruct((R, W), F32)] * 4, compiler_params=_params(("arbitrary",)),
    )(w, *gs, m, v)


def _sum_slots(name, a, tr=256):
    n, R, W = a.shape
    tr = _row_tile(R, tr)

    def body(a_ref, o_ref):
        acc = _f(a_ref[0])
        for j in range(1, n):
            acc = acc + _f(a_ref[j])
        o_ref[...] = acc

    return pl.pallas_call(
        body, name=name, grid=(R // tr,), in_specs=[pl.BlockSpec((n, tr, W), lambda i: (0, i, 0))],
        out_specs=pl.BlockSpec((tr, W), lambda i: (i, 0)), out_shape=jax.ShapeDtypeStruct((R, W), F32),
        compiler_params=_params(("arbitrary",)),
    )(a)


def _here():
    return lax.axis_index("x"), lax.axis_index("y"), lax.axis_index("c")


def _flip(v, d):
    return 1 - v if d else v


def _allgather_small(name, buf):
    R = buf.shape[0]
    rel = [(dx, dy, dc) for dx in (0, 1) for dy in (0, 1) for dc in (0, 1)][1:]

    def body(x_ref, o_ref, send, recv, lsem):
        x, y, c = _here()
        me = 4 * x + 2 * y + c
        mine = pltpu.make_async_copy(x_ref, o_ref.at[me], lsem)
        mine.start()

        def copy(k, slot):
            dx, dy, dc = rel[k]
            return pltpu.make_async_remote_copy(
                src_ref=x_ref, dst_ref=o_ref.at[slot], send_sem=send.at[k], recv_sem=recv.at[k],
                device_id=(_flip(x, dx), _flip(y, dy), _flip(c, dc)), device_id_type=MESH_ID)

        sent = [copy(k, me) for k in range(len(rel))]
        for cp in sent:
            cp.start()
        for k, (dx, dy, dc) in enumerate(rel):
            copy(k, 4 * _flip(x, dx) + 2 * _flip(y, dy) + _flip(c, dc)).wait_recv()
        for cp in sent:
            cp.wait_send()
        mine.wait()

    return pl.pallas_call(
        body, name=name, out_shape=jax.ShapeDtypeStruct((N_DEV, R, LANES), F32),
        in_specs=[pl.BlockSpec(memory_space=pltpu.VMEM)], out_specs=pl.BlockSpec(memory_space=pltpu.VMEM),
        scratch_shapes=[pltpu.SemaphoreType.DMA((7,)), pltpu.SemaphoreType.DMA((7,)), pltpu.SemaphoreType.DMA],
    )(buf)


CHIP_REL = [(1, 0), (0, 1), (1, 1)]


def _chip_scatter(name, arrays):
    n = len(arrays)

    def body(*refs):
        ins, outs = refs[:n], refs[n:2 * n]
        send, recv, lsem = refs[2 * n:]
        x, y, c = _here()
        s = 2 * x + y

        def copy(w, k, slot):
            dx, dy = CHIP_REL[k]
            px, py = _flip(x, dx), _flip(y, dy)
            return pltpu.make_async_remote_copy(
                src_ref=ins[w].at[2 * px + py], dst_ref=outs[w].at[slot], send_sem=send.at[3 * w + k],
                recv_sem=recv.at[3 * w + k], device_id=(px, py, c), device_id_type=MESH_ID)

        local = [pltpu.make_async_copy(ins[w].at[s], outs[w].at[s], lsem.at[w]) for w in range(n)]
        for cp in local:
            cp.start()
        sent = [copy(w, k, s) for w in range(n) for k in range(3)]
        for cp in sent:
            cp.start()
        for w in range(n):
            for k, (dx, dy) in enumerate(CHIP_REL):
                copy(w, k, 2 * _flip(x, dx) + _flip(y, dy)).wait_recv()
        for cp in sent:
            cp.wait_send()
        for cp in local:
            cp.wait()

    any_spec = pl.BlockSpec(memory_space=pl.ANY)
    return pl.pallas_call(
        body, name=name, out_shape=[jax.ShapeDtypeStruct(a.shape, a.dtype) for a in arrays],
        in_specs=[any_spec] * n, out_specs=[any_spec] * n,
        scratch_shapes=[pltpu.SemaphoreType.DMA((3 * n,)), pltpu.SemaphoreType.DMA((3 * n,)), pltpu.SemaphoreType.DMA((n,))],
    )(*arrays)


def _gather_weights(shards):
    n = len(shards)

    def body(*refs):
        ins, outs = refs[:n], refs[n:2 * n]
        send, recv, fsend, frecv, lsem = refs[2 * n:]
        x, y, c = _here()
        s = 2 * x + y

        def chip_of(k):
            dx, dy = CHIP_REL[k]
            return _flip(x, dx), _flip(y, dy)

        def over_ici(w, k, slot):
            px, py = chip_of(k)
            return pltpu.make_async_remote_copy(
                src_ref=ins[w].at[c], dst_ref=outs[w].at[slot].at[c], send_sem=send.at[3 * w + k], recv_sem=recv.at[3 * w + k],
                device_id=(px, py, c), device_id_type=MESH_ID)

        def to_sibling(w, k, layer):
            px, py = chip_of(k)
            part = outs[w].at[2 * px + py].at[layer]
            return pltpu.make_async_remote_copy(
                src_ref=part, dst_ref=part, send_sem=fsend.at[3 * w + k], recv_sem=frecv.at[3 * w + k],
                device_id=(x, y, 1 - c), device_id_type=MESH_ID)

        local = [pltpu.make_async_copy(ins[w], outs[w].at[s], lsem.at[w]) for w in range(n)]
        for cp in local:
            cp.start()
        sent = [over_ici(w, k, s) for w in range(n) for k in range(3)]
        for cp in sent:
            cp.start()
        passed = []
        for w in range(n):
            for k in range(3):
                px, py = chip_of(k)
                over_ici(w, k, 2 * px + py).wait_recv()
                passed.append(to_sibling(w, k, c))
                passed[-1].start()
        for w in range(n):
            for k in range(3):
                to_sibling(w, k, 1 - c).wait_recv()
        for cp in sent + passed:
            cp.wait_send()
        for cp in local:
            cp.wait()

    any_spec = pl.BlockSpec(memory_space=pl.ANY)
    return pl.pallas_call(
        body, name="gather_weights", out_shape=[jax.ShapeDtypeStruct((N_CHIPS,) + a.shape, a.dtype) for a in shards],
        in_specs=[any_spec] * n, out_specs=[any_spec] * n,
        scratch_shapes=[pltpu.SemaphoreType.DMA((3 * n,))] * 4 + [pltpu.SemaphoreType.DMA((n,))],
    )(*shards)


def _sibling_exchange(name, arrays):
    n = len(arrays)

    def body(*refs):
        ins, outs = refs[:n], refs[n:2 * n]
        send, recv = refs[2 * n:]
        x, y, c = _here()
        cps = [pltpu.make_async_remote_copy(src_ref=ins[w], dst_ref=outs[w], send_sem=send.at[w], recv_sem=recv.at[w],
                                            device_id=(x, y, 1 - c), device_id_type=MESH_ID) for w in range(n)]
        for cp in cps:
            cp.start()
        for cp in cps:
            cp.wait()

    any_spec = pl.BlockSpec(memory_space=pl.ANY)
    return pl.pallas_call(
        body, name=name, out_shape=[jax.ShapeDtypeStruct(a.shape, a.dtype) for a in arrays],
        in_specs=[any_spec] * n, out_specs=[any_spec] * n,
        scratch_shapes=[pltpu.SemaphoreType.DMA((n,)), pltpu.SemaphoreType.DMA((n,))],
    )(*arrays)


def _pack(arrs):
    flat = jnp.concatenate([a.reshape(-1).astype(F32) for a in arrs])
    n = flat.shape[0]
    rows = -(-n // (8 * LANES)) * 8
    return jnp.pad(flat, (0, rows * LANES - n)).reshape(rows, LANES)


def _unpack(buf, shapes):
    lead = buf.shape[:-2]
    flat = buf.reshape(lead + (-1,))
    out, off = [], 0
    for s in shapes:
        n = 1
        for d in s:
            n *= d
        out.append(flat[..., off:off + n].reshape(lead + tuple(s)))
        off += n
    return out


def _pad_w_in(w):
    return jnp.concatenate([w[:, :2048], w[:, 2816:2824], jnp.zeros((w.shape[0], P_XBC - P_DT - 8), w.dtype),
                            w[:, 2048:2816], w[:, 2824:]], axis=1)


def _unpad_w_in(g):
    return jnp.concatenate([g[:, :P_DT], g[:, P_XBC:P_G], g[:, P_DT:P_DT + 8], g[:, P_G:]], axis=1)


def _row(v):
    return v.reshape(1, -1)


BIG = (("w_in", 2), ("w_sc_out", 2), ("w_sb_out", 2), ("w_ssm_out", 2), ("w_o", 1), ("w_ffn_in", 2), ("w_ffn_out", 1))
SMALL = ("mod_b", "g_pre_mix", "g_post_mix", "g_pre_ffn", "g_post_ffn", "sc_conv_w", "ssm_conv_w", "ssm_conv_b",
         "ssm_dt_bias", "ssm_a_log", "ssm_d", "ssm_norm_w")
WEIGHT_ORDER = ("mod_w", "mod_b", "g_pre_mix", "g_post_mix", "g_pre_ffn", "g_post_ffn", "w_in", "sc_conv_w",
                "ssm_conv_w", "ssm_conv_b", "ssm_dt_bias", "ssm_a_log", "ssm_d", "ssm_norm_w", "w_sc_out", "w_sb_out",
                "w_ssm_out", "w_o", "w_ffn_in", "w_ffn_out")


def _layer_fwd(l, x_in, h, W, V):
    S = {"x_in": x_in, "h": h}
    P = _mm(f"in_proj{l}", h, W["w_in"], "nn", BF16, tn_cap=1024)
    S["P"] = P
    S["ya"] = _shortconv_fwd(P, V["sc_w"])
    S["yb"] = _sb_fwd(P)
    S["pre"] = _ssmconv_fwd(P, V["ssm_w"], V["ssm_b"])
    S["y_ssd"], S["states"] = _ssd_fwd(S["pre"], P, V["dtb"], V["alog"])
    S["yc"] = _rowwise(f"ssm_post{l}", lambda y, px, z, d, nw: _ssm_post(y, px, _f(z), d, nw),
                       [(S["y_ssd"], SSM_INNER, 0), (S["pre"], SSM_INNER, 0), (P, SSM_INNER, P_Z // SSM_INNER)],
                       [V["d_full"], V["norm_w"]], [(SSM_INNER, BF16)])[0]
    S["Ya"] = _mm(f"sc_out{l}", S["ya"], W["w_sc_out"], "nn", BF16)
    S["Yb"] = _mm(f"sb_out{l}", S["yb"], W["w_sb_out"], "nn", BF16)
    S["Yc"] = _mm(f"ssm_out{l}", S["yc"], W["w_ssm_out"], "nn", BF16)
    gb = P_G // D_MODEL
    S["merged"] = _rowwise(f"merge{l}", lambda *t: _merge(*[_f(v) for v in t]),
                           [(P, D_MODEL, gb), (P, D_MODEL, gb + 1), (P, D_MODEL, gb + 2),
                            (S["Ya"], D_MODEL, 0), (S["Yb"], D_MODEL, 0), (S["Yc"], D_MODEL, 0)], [], [(D_MODEL, BF16)])[0]
    S["mix"] = _mm(f"w_o{l}", S["merged"], W["w_o"], "nn", BF16)
    S["x1"], S["h2"] = _mid_fwd(f"mid_mix{l}", x_in, S["mix"], V["mid_mix"])
    S["GU"] = _mm(f"ffn_in{l}", S["h2"], W["w_ffn_in"], "nn", BF16)
    S["act"] = _rowwise(f"swiglu{l}", lambda g, u: _swiglu(_f(g), _f(u)),
                        [(S["GU"], FFN_HIDDEN, 0), (S["GU"], FFN_HIDDEN, 1)], [], [(FFN_HIDDEN, BF16)])[0]
    S["f"] = _mm(f"ffn_out{l}", S["act"], W["w_ffn_out"], "nn", BF16)
    return S


def _layer_bwd(l, S, W, V, dx1, df):
    G = {}
    P = S["P"]
    d_act = _mm(f"d_act{l}", df, W["w_ffn_out"], "nt", BF16)
    G["w_ffn_out"] = _mm(f"gw_ffn_out{l}", S["act"], df, "tn", F32)

    def swiglu_bwd(g, u, d):
        _, vjp = jax.vjp(_swiglu, _f(g), _f(u))
        return jnp.concatenate(vjp(_f(d)), axis=1)

    dGU = _rowwise(f"swiglu_bwd{l}", swiglu_bwd, [(S["GU"], FFN_HIDDEN, 0), (S["GU"], FFN_HIDDEN, 1), (d_act, FFN_HIDDEN, 0)],
                   [], [(2 * FFN_HIDDEN, BF16)])[0]
    dh2 = _mm(f"d_h2{l}", dGU, W["w_ffn_in"], "nt", BF16)
    G["w_ffn_in"] = _mm(f"gw_ffn_in{l}", S["h2"], dGU, "tn", F32)
    dx, dmix, G["gate1"], G["g_post_mix"], G["g_pre_ffn"], G["scale2"], G["shift2"] = _mid_bwd(
        f"mid_mix_bwd{l}", S["x_in"], S["mix"], dx1, dh2, V["mid_mix"])
    dmerged = _mm(f"d_merged{l}", dmix, W["w_o"], "nt", BF16)
    G["w_o"] = _mm(f"gw_o{l}", S["merged"], dmix, "tn", F32)

    def merge_bwd(ga, gb, gc, ya, yb, yc, d):
        _, vjp = jax.vjp(_merge, *[_f(v) for v in (ga, gb, gc, ya, yb, yc)])
        dga, dgb, dgc, dya, dyb, dyc = vjp(_f(d))
        return jnp.concatenate([dga, dgb, dgc], axis=1), dya, dyb, dyc

    gb = P_G // D_MODEL
    dG, dYa, dYb, dYc = _rowwise(
        f"merge_bwd{l}", merge_bwd,
        [(P, D_MODEL, gb), (P, D_MODEL, gb + 1), (P, D_MODEL, gb + 2), (S["Ya"], D_MODEL, 0), (S["Yb"], D_MODEL, 0),
         (S["Yc"], D_MODEL, 0), (dmerged, D_MODEL, 0)], [], [(3 * D_MODEL, BF16)] + [(D_MODEL, BF16)] * 3)
    dya = _mm(f"d_ya{l}", dYa, W["w_sc_out"], "nt", BF16)
    dyb = _mm(f"d_yb{l}", dYb, W["w_sb_out"], "nt", BF16)
    dyc = _mm(f"d_yc{l}", dYc, W["w_ssm_out"], "nt", BF16)
    G["w_sc_out"] = _mm(f"gw_sc_out{l}", S["ya"], dYa, "tn", F32)
    G["w_sb_out"] = _mm(f"gw_sb_out{l}", S["yb"], dYb, "tn", F32)
    G["w_ssm_out"] = _mm(f"gw_ssm_out{l}", S["yc"], dYc, "tn", F32)

    def post_bwd(y, px, z, d, dfull, nw):
        _, vjp = jax.vjp(_ssm_post, y, px, _f(z), dfull, nw)
        return vjp(_f(d))

    dy_ssd, dxs, dz, G["d_full"], G["ssm_norm_w"] = _rowwise(
        f"ssm_post_bwd{l}", post_bwd,
        [(S["y_ssd"], SSM_INNER, 0), (S["pre"], SSM_INNER, 0), (P, SSM_INNER, P_Z // SSM_INNER), (dyc, SSM_INNER, 0)],
        [V["d_full"], V["norm_w"]], [(SSM_INNER, F32), (SSM_INNER, F32), (SSM_INNER, BF16)], [(1, SSM_INNER)] * 2)
    dpre, ddt, G["dtb"], G["alog"] = _ssd_bwd(S["pre"], P, S["states"], dy_ssd, dxs, V["dtb"], V["alog"])
    dxbc, w0, w1, w2, w3, G["ssm_conv_b"] = _ssmconv_bwd(P, dpre, V["ssm_w"])
    G["ssm_conv_w"] = jnp.concatenate([w0, w1, w2, w3], axis=0)
    dq, dk, dv = _sb_bwd(P, dyb)
    dA, s0, s1, s2 = _shortconv_bwd(P, dya, V["sc_w"])
    G["sc_conv_w"] = jnp.concatenate([s0, s1, s2], axis=0)
    dP = _rowwise(f"assemble_dp{l}", lambda *t: jnp.concatenate([v.astype(BF16) for v in t], axis=1),
                  [(dA, 3 * SC_WIDTH, 0), (dq, 256, 0), (dk, 256, 0), (dv, 256, 0), (dz, SSM_INNER, 0), (ddt, DT_PAD, 0),
                   (dxbc, SSM_CONV_DIM, 0), (dG, 3 * D_MODEL, 0)], [], [(P_WIDTH, BF16)])[0]
    dh = _mm(f"d_h{l}", dP, W["w_in"], "nt", BF16, tk_cap=2048)
    G["w_in"] = _mm(f"gw_in{l}", S["h"], dP, "tn", F32, tn_cap=1024)
    return dx, dh, G


def kernel(x, c, mod_w, mod_b, g_pre_mix, g_post_mix, g_pre_ffn, g_post_ffn, w_in, sc_conv_w, ssm_conv_w, ssm_conv_b, ssm_dt_bias, ssm_a_log, ssm_d, ssm_norm_w, w_sc_out, w_sb_out, w_ssm_out, w_o, w_ffn_in, w_ffn_out, loss_target, m_mod_w, m_mod_b, m_g_pre_mix, m_g_post_mix, m_g_pre_ffn, m_g_post_ffn, m_w_in, m_sc_conv_w, m_ssm_conv_w, m_ssm_conv_b, m_ssm_dt_bias, m_ssm_a_log, m_ssm_d, m_ssm_norm_w, m_w_sc_out, m_w_sb_out, m_w_ssm_out, m_w_o, m_w_ffn_in, m_w_ffn_out, v_mod_w, v_mod_b, v_g_pre_mix, v_g_post_mix, v_g_pre_ffn, v_g_post_ffn, v_w_in, v_sc_conv_w, v_ssm_conv_w, v_ssm_conv_b, v_ssm_dt_bias, v_ssm_a_log, v_ssm_d, v_ssm_norm_w, v_w_sc_out, v_w_sb_out, v_w_ssm_out, v_w_o, v_w_ffn_in, v_w_ffn_out):
    wts = dict(mod_w=mod_w, mod_b=mod_b, g_pre_mix=g_pre_mix, g_post_mix=g_post_mix, g_pre_ffn=g_pre_ffn,
               g_post_ffn=g_post_ffn, w_in=w_in, sc_conv_w=sc_conv_w, ssm_conv_w=ssm_conv_w, ssm_conv_b=ssm_conv_b,
               ssm_dt_bias=ssm_dt_bias, ssm_a_log=ssm_a_log, ssm_d=ssm_d, ssm_norm_w=ssm_norm_w, w_sc_out=w_sc_out,
               w_sb_out=w_sb_out, w_ssm_out=w_ssm_out, w_o=w_o, w_ffn_in=w_ffn_in, w_ffn_out=w_ffn_out)
    mom = dict(mod_w=m_mod_w, mod_b=m_mod_b, g_pre_mix=m_g_pre_mix, g_post_mix=m_g_post_mix, g_pre_ffn=m_g_pre_ffn,
               g_post_ffn=m_g_post_ffn, w_in=m_w_in, sc_conv_w=m_sc_conv_w, ssm_conv_w=m_ssm_conv_w,
               ssm_conv_b=m_ssm_conv_b, ssm_dt_bias=m_ssm_dt_bias, ssm_a_log=m_ssm_a_log, ssm_d=m_ssm_d,
               ssm_norm_w=m_ssm_norm_w, w_sc_out=m_w_sc_out, w_sb_out=m_w_sb_out, w_ssm_out=m_w_ssm_out, w_o=m_w_o,
               w_ffn_in=m_w_ffn_in, w_ffn_out=m_w_ffn_out)
    var = dict(mod_w=v_mod_w, mod_b=v_mod_b, g_pre_mix=v_g_pre_mix, g_post_mix=v_g_post_mix, g_pre_ffn=v_g_pre_ffn,
               g_post_ffn=v_g_post_ffn, w_in=v_w_in, sc_conv_w=v_sc_conv_w, ssm_conv_w=v_ssm_conv_w,
               ssm_conv_b=v_ssm_conv_b, ssm_dt_bias=v_ssm_dt_bias, ssm_a_log=v_ssm_a_log, ssm_d=v_ssm_d,
               ssm_norm_w=v_ssm_norm_w, w_sc_out=v_w_sc_out, w_sb_out=v_w_sb_out, w_ssm_out=v_w_ssm_out, w_o=v_w_o,
               w_ffn_in=v_w_ffn_in, w_ffn_out=v_w_ffn_out)
    xi, yi, ci = _here()
    chip = 2 * xi + yi
    me = 4 * xi + 2 * yi + ci
    x0, target = x[0], loss_target[0]

    first_shapes = [(D_MODEL,), sc_conv_w.shape, ssm_conv_w.shape]
    g0 = _allgather_small("gather_cond", _pack([c, sc_conv_w, ssm_conv_w]))
    c_rows, sc_sh, ssm_sh = _unpack(g0, first_shapes)
    c_all = c_rows
    sc_w = jnp.concatenate([sc_sh[2 * j] for j in range(N_CHIPS)], axis=-1)
    ssm_w = jnp.concatenate([ssm_sh[2 * j] for j in range(N_CHIPS)], axis=-1)

    mod_b_sh = lax.dynamic_slice_in_dim(mod_b, chip * MOD_SHARD, MOD_SHARD, axis=1).reshape(DEPTH, 1, MOD_SHARD)
    modpart = _mod_fwd(c_all, mod_w, mod_b_sh)
    g1 = _allgather_small("gather_mod", modpart.reshape(-1, LANES)).reshape(N_DEV, DEPTH, N_DEV, MOD_SHARD)
    mod = jnp.concatenate([lax.dynamic_index_in_dim(g1[2 * j], me, axis=1, keepdims=False) for j in range(N_CHIPS)],
                          axis=-1)

    gathered = _gather_weights([wts[n].astype(BF16) for n, _ in BIG])
    full = {n: jnp.concatenate([g[j] for j in range(N_CHIPS)], axis=ax) for (n, ax), g in zip(BIG, gathered)}

    Ws, Vs = [], []
    for l in range(DEPTH):
        W = {n: full[n][l] for n, _ in BIG}
        W["w_in"] = _pad_w_in(W["w_in"])
        Ws.append(W)
        sh1, sc1, gt1, sh2, sc2, gt2 = [_row(v) for v in jnp.split(mod[l], N_MOD)]
        Vs.append(dict(
            shift1=sh1, scale1=sc1, g_pre_mix=_row(g_pre_mix[l]),
            mid_mix=[gt1, _row(g_post_mix[l]), _row(g_pre_ffn[l]), sc2, sh2],
            gate2=gt2, g_post_ffn=_row(g_post_ffn[l]),
            sc_w=sc_w[l], ssm_w=ssm_w[l], ssm_b=_row(ssm_conv_b[l]),
            dtb=_row(jnp.pad(ssm_dt_bias[l], (0, LANES - SSM_HEADS))), alog=_row(jnp.pad(ssm_a_log[l], (0, LANES - SSM_HEADS))),
            d_full=_row(jnp.repeat(ssm_d[l], SSM_INNER // SSM_HEADS)), norm_w=_row(ssm_norm_w[l])))

    def mid_ffn_vecs(l):
        return [Vs[l]["gate2"], Vs[l]["g_post_ffn"], Vs[l + 1]["g_pre_mix"], Vs[l + 1]["scale1"], Vs[l + 1]["shift1"]]

    saved = []
    x_in = x0
    h = _first_fwd(x0, [Vs[0]["g_pre_mix"], Vs[0]["scale1"], Vs[0]["shift1"]])
    for l in range(DEPTH):
        S = _layer_fwd(l, x_in, h, Ws[l], Vs[l])
        saved.append(S)
        if l + 1 < DEPTH:
            x_in, h = _mid_fwd(f"mid_ffn{l}", S["x1"], S["f"], mid_ffn_vecs(l))

    GL = [None] * DEPTH
    S = saved[-1]
    dx1, df, g_gate2, g_gpf, loss_cols = _last_bwd(S["x1"], S["f"], target, [Vs[-1]["gate2"], Vs[-1]["g_post_ffn"]])
    for l in reversed(range(DEPTH)):
        dx, dh, G = _layer_bwd(l, saved[l], Ws[l], Vs[l], dx1, df)
        G["gate2"], G["g_post_ffn"] = g_gate2, g_gpf
        GL[l] = G
        if l > 0:
            Sp = saved[l - 1]
            dx1, df, g_gate2, g_gpf, G["g_pre_mix"], G["scale1"], G["shift1"] = _mid_bwd(
                f"mid_ffn_bwd{l - 1}", Sp["x1"], Sp["f"], dx, dh, mid_ffn_vecs(l - 1))
        else:
            grad_x, G["g_pre_mix"], G["scale1"], G["shift1"] = _first_bwd(
                x0, dx, dh, [Vs[0]["g_pre_mix"], Vs[0]["scale1"], Vs[0]["shift1"]])
    loss = lax.psum(jnp.sum(loss_cols), ("x", "y", "c"))

    def both(key, shape=None):
        a = jnp.stack([GL[l][key] for l in range(DEPTH)])
        return a if shape is None else a.reshape(shape)

    dmod = jnp.concatenate([both(k, (DEPTH, D_MODEL)) for k in ("shift1", "scale1", "gate1", "shift2", "scale2", "gate2")],
                           axis=1)
    part_small = dict(
        mod_b=dmod, g_pre_mix=both("g_pre_mix", (DEPTH, D_MODEL)), g_post_mix=both("g_post_mix", (DEPTH, D_MODEL)),
        g_pre_ffn=both("g_pre_ffn", (DEPTH, D_MODEL)), g_post_ffn=both("g_post_ffn", (DEPTH, D_MODEL)),
        sc_conv_w=both("sc_conv_w"), ssm_conv_w=both("ssm_conv_w"), ssm_conv_b=both("ssm_conv_b", (DEPTH, SSM_CONV_DIM)),
        ssm_dt_bias=both("dtb", (DEPTH, LANES))[:, :SSM_HEADS], ssm_a_log=both("alog", (DEPTH, LANES))[:, :SSM_HEADS],
        ssm_d=both("d_full", (DEPTH, SSM_HEADS, SSM_INNER // SSM_HEADS)).sum(-1),
        ssm_norm_w=both("ssm_norm_w", (DEPTH, SSM_INNER)))
    small_shapes = [part_small[n].shape for n in SMALL]
    g2 = _allgather_small("gather_small_grads", _pack([part_small[n] for n in SMALL]))
    tot = dict(zip(SMALL, _unpack(_sum_slots("sum_small_grads", g2), small_shapes)))
    dmod_all = _unpack(g2, small_shapes)[0]
    dmod_sh = jnp.swapaxes(lax.dynamic_slice_in_dim(dmod_all, chip * MOD_SHARD, MOD_SHARD, axis=2), 0, 1)
    grads = {"mod_w": _mod_bwd(c_all, dmod_sh)}
    for n in SMALL:
        grads[n] = tot[n]
    grads["sc_conv_w"] = lax.dynamic_slice_in_dim(tot["sc_conv_w"], chip * 64, 64, axis=2)
    grads["ssm_conv_w"] = lax.dynamic_slice_in_dim(tot["ssm_conv_w"], chip * 192, 192, axis=2)

    pieces = []
    for n, ax in BIG:
        g = jnp.stack([_unpad_w_in(GL[l][n]) if n == "w_in" else GL[l][n] for l in range(DEPTH)])
        pieces.append(jnp.stack(jnp.split(g, N_CHIPS, axis=ax)).astype(BF16))
    landed = _chip_scatter("scatter_grads", pieces)
    mine = [_sum_slots(f"sum_{n}", a.reshape(N_CHIPS, -1, a.shape[-1])) for (n, _), a in zip(BIG, landed)]
    theirs = _sibling_exchange("swap_core_sums", mine)

    out = {}

    def update(name, w2, gs, m2, v2, shape):
        g, d, nm, nv = _adamw(f"adamw_{name}", w2, gs, m2, v2)
        out[name] = tuple(a.reshape(shape) for a in (g, d, nm, nv))

    for (n, _), a, b in zip(BIG, mine, theirs):
        shp = wts[n].shape
        two = (-1, shp[-1])
        update(n, wts[n].reshape(two), [a, b], mom[n].reshape(two), var[n].reshape(two), shp)
    two = (-1, MOD_SHARD)
    update("mod_w", mod_w.reshape(two), [grads["mod_w"].reshape(two)], m_mod_w.reshape(two), v_mod_w.reshape(two), mod_w.shape)
    shapes = [wts[n].shape for n in SMALL]
    res = _adamw("adamw_small", _pack([wts[n] for n in SMALL]), [_pack([grads[n] for n in SMALL])],
                 _pack([mom[n] for n in SMALL]), _pack([var[n] for n in SMALL]))
    for n, g, d, nm, nv in zip(SMALL, *[_unpack(r, shapes) for r in res]):
        out[n] = (g, d, nm, nv)

    result = [loss, grad_x[None]]
    for k in range(4):
        result += [out[n][k] for n in WEIGHT_ORDER]
    return tuple(result)
```

```python
import functools

import jax
import jax.numpy as jnp
from jax import lax
from jax.experimental import pallas as pl
from jax.experimental.pallas import tpu as pltpu

F32 = jnp.float32
BF16 = jnp.bfloat16
HIGHEST = lax.Precision.HIGHEST
MESH_ID = pl.DeviceIdType.MESH

D_MODEL = 1024
DEPTH = 2
SC_WIDTH = 256
SC_KERNEL = 3
SB_HEAD_DIM = 64
SSM_INNER = 512
SSM_HEADS = 8
SSM_STATE = 64
SSM_CONV = 4
SSM_CHUNK = 256
SSM_CONV_DIM = 768
FFN_HIDDEN = 2816
NORM_EPS = 1e-6
N_MOD = 6
N_CHIPS = 4
N_DEV = 8

ADAM_LR = 0.001
ADAM_B1 = 0.9
ADAM_B2 = 0.999
ADAM_EPS = 1e-08
ADAM_WD = 0.01
ADAM_STEP = 10

P_WIDTH = 6144
P_A, P_B, P_Z, P_DT, P_XBC, P_G = 0, 768, 1536, 2048, 2304, 3072
DT_PAD = 256

VMEM_LIMIT_BYTES = 56 * 1024 * 1024
LANES = 128

SB_LOG_CUTOFF = -105.0
SB_TQ = 128
SB_TK = 256


def _params(sem):
    return pltpu.CompilerParams(dimension_semantics=sem, vmem_limit_bytes=VMEM_LIMIT_BYTES)


def _pick(n, cap):
    if n <= cap:
        return n
    best = None
    for m in range(LANES, cap + 1, LANES):
        if n % m == 0:
            best = m
    assert best is not None, (n, cap)
    return best


def _rowwise(name, fn, rows, vecs, row_outs, acc_outs=(), tl=256):
    L = rows[0][0].shape[0]
    tl = min(tl, L)
    assert L % tl == 0
    n_in = len(rows) + len(vecs)
    n_ro = len(row_outs)

    def body(*refs):
        ins, ro, ao = refs[:n_in], refs[n_in:n_in + n_ro], refs[n_in + n_ro:]
        vals = fn(*[r[...] for r in ins])
        if not isinstance(vals, (tuple, list)):
            vals = (vals,)
        for o, v in zip(ro, vals[:n_ro]):
            o[...] = v.astype(o.dtype)
        if ao:
            @pl.when(pl.program_id(0) == 0)
            def _():
                for o in ao:
                    o[...] = jnp.zeros_like(o)
            for o, v in zip(ao, vals[n_ro:]):
                o[...] += v.astype(F32)

    in_specs = [pl.BlockSpec((tl, w), functools.partial(lambda i, cb: (i, cb), cb=cb)) for _, w, cb in rows]
    in_specs += [pl.BlockSpec(v.shape, lambda i: (0, 0)) for v in vecs]
    out_specs = [pl.BlockSpec((tl, w), lambda i: (i, 0)) for w, _ in row_outs]
    out_specs += [pl.BlockSpec(s, lambda i: (0, 0)) for s in acc_outs]
    out_shape = [jax.ShapeDtypeStruct((L, w), dt) for w, dt in row_outs]
    out_shape += [jax.ShapeDtypeStruct(s, F32) for s in acc_outs]
    return pl.pallas_call(
        body, name=name, grid=(L // tl,), in_specs=in_specs, out_specs=out_specs, out_shape=out_shape,
        compiler_params=_params(("arbitrary",)),
    )(*[a for a, _, _ in rows], *vecs)


def _mm(name, a, b, mode, out_dtype, tm=1024, tn_cap=1408, tk_cap=2816):
    if mode == "nn":
        (M, K), (_, N) = a.shape, b.shape
    elif mode == "nt":
        (M, K), (N, _) = a.shape, b.shape
    else:
        (K, M), (_, N) = a.shape, b.shape
        tm, tk_cap = 1408, 2048
    tm = _pick(M, tm)
    tn = _pick(N, tn_cap)
    tk = _pick(K, tk_cap)
    nk = K // tk

    def body(a_ref, b_ref, o_ref, *scr):
        if mode == "nn":
            p = jnp.dot(a_ref[...], b_ref[...], preferred_element_type=F32)
        elif mode == "nt":
            p = lax.dot_general(a_ref[...], b_ref[...], (((1,), (1,)), ((), ())), preferred_element_type=F32)
        else:
            p = lax.dot_general(a_ref[...], b_ref[...], (((0,), (0,)), ((), ())), preferred_element_type=F32)
        if nk == 1:
            o_ref[...] = p.astype(o_ref.dtype)
        else:
            acc = scr[0]
            k = pl.program_id(2)

            @pl.when(k == 0)
            def _():
                acc[...] = p

            @pl.when(k > 0)
            def _():
                acc[...] += p

            @pl.when(k == nk - 1)
            def _():
                o_ref[...] = acc[...].astype(o_ref.dtype)

    if mode == "nn":
        a_spec = pl.BlockSpec((tm, tk), lambda i, j, k: (i, k))
        b_spec = pl.BlockSpec((tk, tn), lambda i, j, k: (k, j))
    elif mode == "nt":
        a_spec = pl.BlockSpec((tm, tk), lambda i, j, k: (i, k))
        b_spec = pl.BlockSpec((tn, tk), lambda i, j, k: (j, k))
    else:
        a_spec = pl.BlockSpec((tk, tm), lambda i, j, k: (k, i))
        b_spec = pl.BlockSpec((tk, tn), lambda i, j, k: (k, j))
    return pl.pallas_call(
        body, name=name, grid=(M // tm, N // tn, nk), in_specs=[a_spec, b_spec],
        out_specs=pl.BlockSpec((tm, tn), lambda i, j, k: (i, j)),
        out_shape=jax.ShapeDtypeStruct((M, N), out_dtype),
        scratch_shapes=[pltpu.VMEM((tm, tn), F32)] if nk > 1 else [],
        compiler_params=_params(("arbitrary", "arbitrary", "arbitrary")),
    )(a, b)


def _mm_epi(name, a, b, mode, tn, extras, epi, outs, tm=512):
    if mode == "nn":
        (M, K), (_, N) = a.shape, b.shape
    else:
        (M, K), (N, _) = a.shape, b.shape
    tm = _pick(M, tm)
    n_ex = len(extras)

    def body(*refs):
        a_ref, b_ref, ex, o_refs = refs[0], refs[1], refs[2:2 + n_ex], refs[2 + n_ex:]
        if mode == "nn":
            p = jnp.dot(a_ref[...], b_ref[...], preferred_element_type=F32)
        else:
            p = lax.dot_general(a_ref[...], b_ref[...], (((1,), (1,)), ((), ())), preferred_element_type=F32)
        for o, v in zip(o_refs, epi(p, *[r[...] for r in ex])):
            o[...] = v.astype(o.dtype)

    a_spec = pl.BlockSpec((tm, K), lambda i, j: (i, 0))
    b_spec = pl.BlockSpec((K, tn), lambda i, j: (0, j)) if mode == "nn" else pl.BlockSpec((tn, K), lambda i, j: (j, 0))
    return pl.pallas_call(
        body, name=name, grid=(M // tm, N // tn),
        in_specs=[a_spec, b_spec] + [pl.BlockSpec((tm, w), lambda i, j: (i, j)) for _, w in extras],
        out_specs=[pl.BlockSpec((tm, w), lambda i, j: (i, j)) for w, _ in outs],
        out_shape=[jax.ShapeDtypeStruct((M, (N // tn) * w), dt) for w, dt in outs],
        compiler_params=_params(("arbitrary", "arbitrary")),
    )(a, b, *[e for e, _ in extras])


def _f(x):
    return x.astype(F32)


def _silu(x):
    return x * jax.nn.sigmoid(x)


def _softplus(x):
    return jnp.maximum(x, 0.0) + jnp.log1p(jnp.exp(-jnp.abs(x)))


def _rms(x, g):
    r = lax.rsqrt(jnp.mean(x * x, axis=-1, keepdims=True) + NORM_EPS)
    return x * r * g


def _adaln(x, g, scale, shift):
    return _rms(x, g) * (1.0 + scale) + shift


def _resid(x, y, gate, g):
    return x + gate * _rms(y, g)


def _mid(x, y, gate, g_post, g_pre, scale, shift):
    x_new = _resid(x, y, gate, g_post)
    return x_new, _adaln(x_new, g_pre, scale, shift)


def _merge(ga, gb, gc, ya, yb, yc):
    return jax.nn.sigmoid(ga) * ya + jax.nn.sigmoid(gb) * yb + jax.nn.sigmoid(gc) * yc


def _swiglu(gt, up):
    return _silu(gt) * up


def _ssm_post(y_ssd, pre_xs, z, d_full, norm_w):
    y = (y_ssd + _silu(pre_xs) * d_full) * _silu(z)
    half = SSM_INNER // 2
    parts = []
    for g in range(2):
        yg = y[:, g * half:(g + 1) * half]
        parts.append(yg * lax.rsqrt(jnp.mean(yg * yg, axis=-1, keepdims=True) + NORM_EPS))
    return jnp.concatenate(parts, axis=1) * norm_w


def _first_fwd(x, vecs):
    return _rowwise("adaln_first", lambda x, g, sc, sh: _adaln(x, g, sc, sh),
                    [(x, D_MODEL, 0)], vecs, [(D_MODEL, BF16)], tl=512)[0]


def _mid_fwd(name, x, y, vecs):
    return _rowwise(name, lambda x, y, *v: _mid(x, _f(y), *v),
                    [(x, D_MODEL, 0), (y, D_MODEL, 0)], vecs, [(D_MODEL, F32), (D_MODEL, BF16)], tl=512)


def _mid_bwd(name, x, y, dx_new, dh, vecs):
    def fn(x, y, dxn, dh, *v):
        _, vjp = jax.vjp(_mid, x, _f(y), *v)
        return vjp((dxn, _f(dh)))

    vec = (1, D_MODEL)
    return _rowwise(name, fn, [(x, D_MODEL, 0), (y, D_MODEL, 0), (dx_new, D_MODEL, 0), (dh, D_MODEL, 0)], vecs,
                    [(D_MODEL, F32), (D_MODEL, BF16)], [vec] * 5)


def _first_bwd(x, dx_in, dh, vecs):
    def fn(x, dxi, dh, *v):
        _, vjp = jax.vjp(_adaln, x, *v)
        dx, dg, dsc, dsh = vjp(_f(dh))
        return dx + dxi, dg, dsc, dsh

    vec = (1, D_MODEL)
    return _rowwise("adaln_first_bwd", fn, [(x, D_MODEL, 0), (dx_in, D_MODEL, 0), (dh, D_MODEL, 0)], vecs,
                    [(D_MODEL, F32)], [vec] * 3)


def _last_bwd(x1, f, target, vecs):
    def fn(x1, f, t, gate, g):
        x2, vjp = jax.vjp(_resid, x1, _f(f), gate, g)
        err = x2 - t
        dx1, df, dgate, dg = vjp(err * (1.0 / D_MODEL))
        loss_cols = jnp.sum(err * err, axis=0, keepdims=True) * (0.5 / D_MODEL)
        return dx1, df, dgate, dg, loss_cols

    vec = (1, D_MODEL)
    return _rowwise("loss_last_bwd", fn, [(x1, D_MODEL, 0), (f, D_MODEL, 0), (target, D_MODEL, 0)], vecs,
                    [(D_MODEL, F32), (D_MODEL, BF16)], [vec] * 3)


HALO = 16


def _shift_down(u, prev, k):
    rows = lax.broadcasted_iota(jnp.int32, u.shape, 0)
    v = pltpu.roll(u, k, 0)
    for t in range(k):
        v = jnp.where(rows == t, prev[HALO - k + t:HALO - k + t + 1, :], v)
    return v


def _shift_up(u, nxt, k):
    n = u.shape[0]
    rows = lax.broadcasted_iota(jnp.int32, u.shape, 0)
    v = pltpu.roll(u, n - k, 0)
    for t in range(k):
        v = jnp.where(rows == n - k + t, nxt[t:t + 1, :], v)
    return v


def _conv_specs(L, tl, width, col_block):
    per = tl // HALO
    last = L // HALO - 1
    main = pl.BlockSpec((tl, width), lambda i: (i, col_block))
    before = pl.BlockSpec((HALO, width), lambda i: (jnp.maximum(i * per - 1, 0), col_block))
    after = pl.BlockSpec((HALO, width), lambda i: (jnp.minimum((i + 1) * per, last), col_block))
    return main, before, after


def _shortconv_fwd(P, w, tl=512):
    L = P.shape[0]
    tl = min(tl, L)
    C = SC_WIDTH
    main, before, _ = _conv_specs(L, tl, 3 * C, 0)

    def body(p_ref, h_ref, w_ref, o_ref):
        first = (pl.program_id(0) == 0)
        p, h = _f(p_ref[...]), _f(h_ref[...])
        b, u = p[:, :C], p[:, C:2 * C] * p[:, 2 * C:]
        uh = jnp.where(first, 0.0, h[:, C:2 * C] * h[:, 2 * C:])
        wv = w_ref[...]
        cv = wv[2:3] * u + wv[1:2] * _shift_down(u, uh, 1) + wv[0:1] * _shift_down(u, uh, 2)
        o_ref[...] = (b * cv).astype(o_ref.dtype)

    return pl.pallas_call(
        body, name="shortconv_fwd", grid=(L // tl,),
        in_specs=[main, before, pl.BlockSpec(w.shape, lambda i: (0, 0))],
        out_specs=pl.BlockSpec((tl, C), lambda i: (i, 0)),
        out_shape=jax.ShapeDtypeStruct((L, C), BF16), compiler_params=_params(("arbitrary",)),
    )(P, P, w)


def _shortconv_bwd(P, dya, w, tl=512):
    L = P.shape[0]
    tl = min(tl, L)
    C = SC_WIDTH
    main, before, after = _conv_specs(L, tl, 3 * C, 0)
    dmain, _, dafter = _conv_specs(L, tl, C, 0)
    n = L // tl

    def body(p_ref, h_ref, n_ref, d_ref, dn_ref, w_ref, o_ref, dw0, dw1, dw2):
        i = pl.program_id(0)
        p, h, nx = _f(p_ref[...]), _f(h_ref[...]), _f(n_ref[...])
        b, c, x = p[:, :C], p[:, C:2 * C], p[:, 2 * C:]
        u = c * x
        uh = jnp.where(i == 0, 0.0, h[:, C:2 * C] * h[:, 2 * C:])
        u1, u2 = _shift_down(u, uh, 1), _shift_down(u, uh, 2)
        wv = w_ref[...]
        cv = wv[2:3] * u + wv[1:2] * u1 + wv[0:1] * u2
        dy = _f(d_ref[...])
        dcv = dy * b
        dcv_n = jnp.where(i == n - 1, 0.0, _f(dn_ref[...]) * nx[:, :C])
        du = wv[2:3] * dcv + wv[1:2] * _shift_up(dcv, dcv_n, 1) + wv[0:1] * _shift_up(dcv, dcv_n, 2)
        o_ref[:, :C] = (dy * cv).astype(o_ref.dtype)
        o_ref[:, C:2 * C] = (du * x).astype(o_ref.dtype)
        o_ref[:, 2 * C:] = (du * c).astype(o_ref.dtype)

        @pl.when(i == 0)
        def _():
            for r in (dw0, dw1, dw2):
                r[...] = jnp.zeros_like(r)

        dw0[...] += jnp.sum(dcv * u2, axis=0, keepdims=True)
        dw1[...] += jnp.sum(dcv * u1, axis=0, keepdims=True)
        dw2[...] += jnp.sum(dcv * u, axis=0, keepdims=True)

    vec = pl.BlockSpec((1, C), lambda i: (0, 0))
    return pl.pallas_call(
        body, name="shortconv_bwd", grid=(n,),
        in_specs=[main, before, after, dmain, dafter, pl.BlockSpec(w.shape, lambda i: (0, 0))],
        out_specs=[pl.BlockSpec((tl, 3 * C), lambda i: (i, 0)), vec, vec, vec],
        out_shape=[jax.ShapeDtypeStruct((L, 3 * C), BF16)] + [jax.ShapeDtypeStruct((1, C), F32)] * 3,
        compiler_params=_params(("arbitrary",)),
    )(P, P, P, dya, dya, w)


def _ssmconv_fwd(P, w, bias, tl=512):
    L = P.shape[0]
    tl = min(tl, L)
    C = SSM_CONV_DIM
    main, before, _ = _conv_specs(L, tl, C, P_XBC // C)

    def body(p_ref, h_ref, w_ref, b_ref, o_ref):
        u = _f(p_ref[...])
        uh = jnp.where(pl.program_id(0) == 0, 0.0, _f(h_ref[...]))
        wv = w_ref[...]
        acc = wv[3:4] * u + b_ref[...]
        for k in range(1, SSM_CONV):
            acc = acc + wv[3 - k:4 - k] * _shift_down(u, uh, k)
        o_ref[...] = acc

    return pl.pallas_call(
        body, name="ssmconv_fwd", grid=(L // tl,),
        in_specs=[main, before, pl.BlockSpec(w.shape, lambda i: (0, 0)), pl.BlockSpec(bias.shape, lambda i: (0, 0))],
        out_specs=pl.BlockSpec((tl, C), lambda i: (i, 0)),
        out_shape=jax.ShapeDtypeStruct((L, C), F32), compiler_params=_params(("arbitrary",)),
    )(P, P, w, bias)


def _ssmconv_bwd(P, dpre, w, tl=512):
    L = P.shape[0]
    tl = min(tl, L)
    C = SSM_CONV_DIM
    main, before, _ = _conv_specs(L, tl, C, P_XBC // C)
    dmain, _, dafter = _conv_specs(L, tl, C, 0)
    n = L // tl

    def body(p_ref, h_ref, d_ref, dn_ref, w_ref, o_ref, dw0, dw1, dw2, dw3, db):
        i = pl.program_id(0)
        u = _f(p_ref[...])
        uh = jnp.where(i == 0, 0.0, _f(h_ref[...]))
        d = d_ref[...]
        dn = jnp.where(i == n - 1, 0.0, dn_ref[...])
        wv = w_ref[...]
        du = wv[3:4] * d
        for k in range(1, SSM_CONV):
            du = du + wv[3 - k:4 - k] * _shift_up(d, dn, k)
        o_ref[...] = du.astype(o_ref.dtype)

        @pl.when(i == 0)
        def _():
            for r in (dw0, dw1, dw2, dw3, db):
                r[...] = jnp.zeros_like(r)

        for k, r in ((3, dw0), (2, dw1), (1, dw2)):
            r[...] += jnp.sum(d * _shift_down(u, uh, k), axis=0, keepdims=True)
        dw3[...] += jnp.sum(d * u, axis=0, keepdims=True)
        db[...] += jnp.sum(d, axis=0, keepdims=True)

    vec = pl.BlockSpec((1, C), lambda i: (0, 0))
    return pl.pallas_call(
        body, name="ssmconv_bwd", grid=(n,),
        in_specs=[main, before, dmain, dafter, pl.BlockSpec(w.shape, lambda i: (0, 0))],
        out_specs=[pl.BlockSpec((tl, C), lambda i: (i, 0))] + [vec] * 5,
        out_shape=[jax.ShapeDtypeStruct((L, C), BF16)] + [jax.ShapeDtypeStruct((1, C), F32)] * 5,
        compiler_params=_params(("arbitrary",)),
    )(P, P, dpre, dpre, w)


def _dot_nt(a, b):
    return lax.dot_general(a, b, (((1,), (1,)), ((), ())), preferred_element_type=F32)


def _dot_tn(a, b):
    return lax.dot_general(a, b, (((0,), (0,)), ((), ())), preferred_element_type=F32)


def _split3(x):
    hi = x.astype(BF16)
    r = x - hi.astype(F32)
    mid = r.astype(BF16)
    return hi, mid, (r - mid.astype(F32)).astype(BF16)


@jax.custom_vjp
def _xm01(x, m):
    return sum(jnp.dot(t, m, preferred_element_type=F32) for t in _split3(x))


def _xm01_fwd(x, m):
    return _xm01(x, m), m


def _xm01_bwd(m, g):
    return sum(_dot_nt(t, m) for t in _split3(g)), jnp.zeros_like(m)


_xm01.defvjp(_xm01_fwd, _xm01_bwd)


@jax.custom_vjp
def _m01x(m, x):
    return sum(jnp.dot(m, t, preferred_element_type=F32) for t in _split3(x))


def _m01x_fwd(m, x):
    return _m01x(m, x), m


def _m01x_bwd(m, g):
    return jnp.zeros_like(m), sum(_dot_tn(m, t) for t in _split3(g))


_m01x.defvjp(_m01x_fwd, _m01x_bwd)


def _ssd_chunk(pre, dtr, s_prev, dtb, alog):
    T = pre.shape[0]
    act = _silu(pre)
    xs, bm, cm = act[:, :SSM_INNER], act[:, SSM_INNER:SSM_INNER + 128], act[:, SSM_INNER + 128:]
    lane = lax.broadcasted_iota(jnp.int32, (1, LANES), 1)
    dt = jnp.where(lane < SSM_HEADS, _softplus(dtr + dtb), 0.0)
    a = dt * (-jnp.exp(alog))
    ri = lax.broadcasted_iota(jnp.int32, (T, T), 0)
    ci = lax.broadcasted_iota(jnp.int32, (T, T), 1)
    causal = ci <= ri
    a_cs = _m01x(causal.astype(BF16), a)
    eh = lax.broadcasted_iota(jnp.int32, (LANES, SSM_INNER), 0)
    ej = lax.broadcasted_iota(jnp.int32, (LANES, SSM_INNER), 1)
    expand = (lax.shift_right_logical(ej, 6) == eh).astype(BF16)
    dt_full = _xm01(dt, expand)
    acs_full = _xm01(a_cs, expand)
    alast_full = acs_full[T - 1:T, :]
    xdt = xs * dt_full
    a_cs_t = a_cs.T
    ys, s_new = [], []
    for g in range(2):
        in_group = lax.shift_right_logical(lane, 6) == g
        cg = jnp.where(in_group, cm, 0.0).astype(BF16)
        bg = jnp.where(in_group, bm, 0.0).astype(BF16)
        scores = _dot_nt(cg, bg)
        for pp in range(2):
            hp = 2 * g + pp
            cols = slice(hp * LANES, (hp + 1) * LANES)
            xp, acsp = xdt[:, cols], acs_full[:, cols]
            per_head = []
            for hh in range(2):
                h = 2 * hp + hh
                decay = jnp.exp(jnp.where(causal, a_cs[:, h:h + 1] - a_cs_t[h:h + 1, :], -jnp.inf))
                per_head.append(jnp.dot((scores * decay).astype(BF16), xp.astype(BF16), preferred_element_type=F32))
            y_diag = jnp.where(lane < SSM_STATE, per_head[0], per_head[1])
            sp = s_prev[hp * LANES:(hp + 1) * LANES, :]
            y_off = jnp.dot(cg, sp.astype(BF16), preferred_element_type=F32) * jnp.exp(acsp)
            ys.append(y_diag + y_off)
            to_end = jnp.exp(alast_full[:, cols] - acsp)
            s_new.append(sp * jnp.exp(alast_full[:, cols]) + _dot_tn(bg, (xp * to_end).astype(BF16)))
    return jnp.concatenate(ys, axis=1), jnp.concatenate(s_new, axis=0)


def _ssd_fwd(pre, P, dtb, alog):
    L = pre.shape[0]
    T = min(SSM_CHUNK, L)
    nc = L // T

    def body(pre_ref, dt_ref, dtb_ref, al_ref, y_ref, st_ref, s_scr):
        @pl.when(pl.program_id(0) == 0)
        def _():
            s_scr[...] = jnp.zeros_like(s_scr)

        st_ref[0] = s_scr[...]
        y, s = _ssd_chunk(pre_ref[...], _f(dt_ref[...]), s_scr[...], dtb_ref[...], al_ref[...])
        y_ref[...] = y
        s_scr[...] = s

    vec = pl.BlockSpec((1, LANES), lambda i: (0, 0))
    return pl.pallas_call(
        body, name="ssd_fwd", grid=(nc,),
        in_specs=[pl.BlockSpec((T, SSM_CONV_DIM), lambda i: (i, 0)), pl.BlockSpec((T, LANES), lambda i: (i, P_DT // LANES)),
                  vec, vec],
        out_specs=[pl.BlockSpec((T, SSM_INNER), lambda i: (i, 0)), pl.BlockSpec((1, 512, LANES), lambda i: (i, 0, 0))],
        out_shape=[jax.ShapeDtypeStruct((L, SSM_INNER), F32), jax.ShapeDtypeStruct((nc, 512, LANES), F32)],
        scratch_shapes=[pltpu.VMEM((512, LANES), F32)], compiler_params=_params(("arbitrary",)),
    )(pre, P, dtb, alog)


def _ssd_bwd(pre, P, states, dy, dxs_extra, dtb, alog):
    L = pre.shape[0]
    T = min(SSM_CHUNK, L)
    nc = L // T

    def body(pre_ref, dt_ref, st_ref, dy_ref, dx_ref, dtb_ref, al_ref, dpre_ref, ddt_ref, ddtb_ref, dal_ref, ds_scr):
        @pl.when(pl.program_id(0) == 0)
        def _():
            ds_scr[...] = jnp.zeros_like(ds_scr)
            ddtb_ref[...] = jnp.zeros_like(ddtb_ref)
            dal_ref[...] = jnp.zeros_like(dal_ref)

        _, vjp = jax.vjp(_ssd_chunk, pre_ref[...], _f(dt_ref[...]), st_ref[0], dtb_ref[...], al_ref[...])
        dpre, ddt, ds, ddtb, dal = vjp((dy_ref[...], ds_scr[...]))
        dpre_ref[:, :SSM_INNER] = dpre[:, :SSM_INNER] + dx_ref[...]
        dpre_ref[:, SSM_INNER:] = dpre[:, SSM_INNER:]
        ddt_ref[:, :LANES] = ddt.astype(ddt_ref.dtype)
        ddt_ref[:, LANES:] = jnp.zeros((T, DT_PAD - LANES), ddt_ref.dtype)
        ds_scr[...] = ds
        ddtb_ref[...] += ddtb
        dal_ref[...] += dal

    vec = pl.BlockSpec((1, LANES), lambda i: (0, 0))
    rev = lambda i: (nc - 1 - i, 0)
    return pl.pallas_call(
        body, name="ssd_bwd", grid=(nc,),
        in_specs=[pl.BlockSpec((T, SSM_CONV_DIM), rev), pl.BlockSpec((T, LANES), lambda i: (nc - 1 - i, P_DT // LANES)),
                  pl.BlockSpec((1, 512, LANES), lambda i: (nc - 1 - i, 0, 0)),
                  pl.BlockSpec((T, SSM_INNER), rev), pl.BlockSpec((T, SSM_INNER), rev), vec, vec],
        out_specs=[pl.BlockSpec((T, SSM_CONV_DIM), rev), pl.BlockSpec((T, DT_PAD), rev), vec, vec],
        out_shape=[jax.ShapeDtypeStruct((L, SSM_CONV_DIM), F32), jax.ShapeDtypeStruct((L, DT_PAD), BF16),
                   jax.ShapeDtypeStruct((1, LANES), F32), jax.ShapeDtypeStruct((1, LANES), F32)],
        scratch_shapes=[pltpu.VMEM((512, LANES), F32)], compiler_params=_params(("arbitrary",)),
    )(pre, P, states, dy, dxs_extra, dtb, alog)


def _sb_scores(qm, kb, later, strict, mask):
    z = _dot_nt(qm, kb)
    lk = jnp.minimum(-z, 0.0) - jnp.log(1.0 + jnp.exp(-jnp.abs(z)))
    if mask is not None:
        lk = jnp.where(mask, lk, 0.0)
    log_a = z + lk + jnp.dot(lk.astype(BF16), strict, preferred_element_type=F32) + later
    if mask is not None:
        log_a = jnp.where(mask, log_a, -jnp.inf)
    return z, lk, log_a


def _dot_split(x, m):
    hi = x.astype(BF16)
    lo = (x - hi.astype(F32)).astype(BF16)
    return jnp.dot(hi, m, preferred_element_type=F32) + jnp.dot(lo, m, preferred_element_type=F32)


def _sb_setup(q_ref, i, tq, tk):
    lane = lax.broadcasted_iota(jnp.int32, (1, LANES), 1)
    first = lane < SB_HEAD_DIM
    q = q_ref[...] * (SB_HEAD_DIM ** -0.5)
    qms = (jnp.where(first, q, jnp.zeros_like(q)), jnp.where(first, jnp.zeros_like(q), q))
    j0 = lax.div(i * tq, tk)
    ri = lax.broadcasted_iota(jnp.int32, (tq, tk), 0)
    ci = lax.broadcasted_iota(jnp.int32, (tq, tk), 1)
    diag_mask = (ci + (j0 * tk - i * tq)) < ri
    kr = lax.broadcasted_iota(jnp.int32, (tk, tk), 0)
    kc = lax.broadcasted_iota(jnp.int32, (tk, tk), 1)
    strict = (kr > kc).astype(BF16)
    return first, qms, j0, diag_mask, strict


def _sb_continue(c):
    return jnp.logical_and(c[0] >= 0, jnp.maximum(jnp.max(c[1][0]), jnp.max(c[1][1])) > SB_LOG_CUTOFF)


def _sb_fwd(P):
    L = P.shape[0]
    tq, tk = min(SB_TQ, L), min(SB_TK, L)
    nq = L // tq
    qb = P_B // LANES

    def body(q_ref, k_ref, v_ref, o_ref, of_ref):
        i = pl.program_id(1)
        first, qms, j0, diag_mask, strict = _sb_setup(q_ref, i, tq, tk)

        def tiles(j, laters, accs, mask=None, valid=None):
            off = pl.multiple_of(j * tk, tk)
            kb, vb = k_ref[pl.ds(off, tk), :], v_ref[pl.ds(off, tk), :]
            new_l, new_a = [], []
            for qm, later, acc in zip(qms, laters, accs):
                gate = later if valid is None else jnp.where(valid, later, -jnp.inf)
                _, lk, log_a = _sb_scores(qm, kb, gate, strict, mask)
                new_a.append(acc + jnp.dot(jnp.exp(log_a).astype(BF16), vb, preferred_element_type=F32))
                rows = jnp.sum(lk, axis=1, keepdims=True)
                new_l.append(later + (rows if valid is None else jnp.where(valid, rows, 0.0)))
            return tuple(new_l), tuple(new_a)

        def step(c):
            laters, accs = tiles(c[0], c[1], c[2])
            return c[0] - 1, laters, accs

        zero, zacc = jnp.zeros((tq, 1), F32), jnp.zeros((tq, LANES), F32)
        laters, accs = tiles(j0, (zero, zero), (zacc, zacc), mask=diag_mask)
        laters, accs = tiles(jnp.maximum(j0 - 1, 0), laters, accs, valid=j0 >= 1)
        _, _, accs = lax.while_loop(_sb_continue, step, (j0 - 2, laters, accs))
        out = jnp.where(first, accs[0], accs[1])
        o_ref[...] = out.astype(o_ref.dtype)
        of_ref[...] = out

    tile_spec = pl.BlockSpec((tq, LANES), lambda p, i: (i, p))
    return pl.pallas_call(
        body, name="sb_fwd", grid=(2, nq),
        in_specs=[pl.BlockSpec((tq, LANES), lambda p, i: (i, qb + p)),
                  pl.BlockSpec((L, LANES), lambda p, i: (0, qb + 2 + p)),
                  pl.BlockSpec((L, LANES), lambda p, i: (0, qb + 4 + p))],
        out_specs=[tile_spec, tile_spec],
        out_shape=[jax.ShapeDtypeStruct((L, 2 * LANES), BF16), jax.ShapeDtypeStruct((L, 2 * LANES), F32)],
        compiler_params=_params(("arbitrary", "arbitrary")),
    )(P, P, P)


def _sb_bwd(P, dyb, yb32):
    L = P.shape[0]
    tq, tk = min(SB_TQ, L), min(SB_TK, L)
    nq = L // tq
    qb = P_B // LANES

    def body(q_ref, k_ref, v_ref, do_ref, of_ref, dq_ref, dk_ref, dv_ref):
        i = pl.program_id(1)
        first, qms, j0, diag_mask, strict = _sb_setup(q_ref, i, tq, tk)

        @pl.when(i == 0)
        def _():
            dk_ref[...] = jnp.zeros_like(dk_ref)
            dv_ref[...] = jnp.zeros_like(dv_ref)

        do = do_ref[...]
        doms = (jnp.where(first, do, jnp.zeros_like(do)), jnp.where(first, jnp.zeros_like(do), do))
        prod = _f(do) * of_ref[...]
        totals = (jnp.sum(jnp.where(first, prod, 0.0), axis=1, keepdims=True),
                  jnp.sum(jnp.where(first, 0.0, prod), axis=1, keepdims=True))

        def tiles(j, laters, later_gs, accs, mask=None, valid=None):
            off = pl.multiple_of(j * tk, tk)
            kb, vb = k_ref[pl.ds(off, tk), :], v_ref[pl.ds(off, tk), :]
            new_l, new_g, new_a = [], [], []
            dk = dv = None
            for h in range(2):
                gate = laters[h] if valid is None else jnp.where(valid, laters[h], -jnp.inf)
                z, lk, log_a = _sb_scores(qms[h], kb, gate, strict, mask)
                att = jnp.exp(log_a).astype(BF16)
                g = _f(att) * _dot_nt(doms[h], vb)
                before = totals[h] - later_gs[h]
                if valid is not None:
                    before = jnp.where(valid, before, 0.0)
                dz = g - (before - _dot_split(g, strict)) * jnp.exp(z + lk)
                if mask is not None:
                    dz = jnp.where(mask, dz, 0.0)
                dzb = dz.astype(BF16)
                dk_h, dv_h = _dot_tn(dzb, qms[h]), _dot_tn(att, doms[h])
                dk, dv = (dk_h, dv_h) if h == 0 else (dk + dk_h, dv + dv_h)
                new_a.append(accs[h] + jnp.dot(dzb, kb, preferred_element_type=F32))
                rows = jnp.sum(lk, axis=1, keepdims=True)
                new_l.append(laters[h] + (rows if valid is None else jnp.where(valid, rows, 0.0)))
                new_g.append(later_gs[h] + jnp.sum(g, axis=1, keepdims=True))
            dk_ref[pl.ds(off, tk), :] += dk
            dv_ref[pl.ds(off, tk), :] += dv
            return tuple(new_l), tuple(new_g), tuple(new_a)

        def step(c):
            return (c[0] - 1,) + tiles(c[0], c[1], c[2], c[3])

        zero, zacc = jnp.zeros((tq, 1), F32), jnp.zeros((tq, LANES), F32)
        carry = tiles(j0, (zero, zero), (zero, zero), (zacc, zacc), mask=diag_mask)
        carry = tiles(jnp.maximum(j0 - 1, 0), *carry, valid=j0 >= 1)
        _, _, _, accs = lax.while_loop(_sb_continue, step, (j0 - 2,) + carry)
        dq_ref[...] = jnp.where(first, accs[0], accs[1]) * (SB_HEAD_DIM ** -0.5)

    full = pl.BlockSpec((L, LANES), lambda p, i: (0, p))
    tile_spec = pl.BlockSpec((tq, LANES), lambda p, i: (i, p))
    return pl.pallas_call(
        body, name="sb_bwd", grid=(2, nq),
        in_specs=[pl.BlockSpec((tq, LANES), lambda p, i: (i, qb + p)),
                  pl.BlockSpec((L, LANES), lambda p, i: (0, qb + 2 + p)),
                  pl.BlockSpec((L, LANES), lambda p, i: (0, qb + 4 + p)), tile_spec, tile_spec],
        out_specs=[tile_spec, full, full],
        out_shape=[jax.ShapeDtypeStruct((L, 2 * LANES), F32)] * 3,
        compiler_params=_params(("arbitrary", "arbitrary")),
    )(P, P, P, dyb, yb32)


MOD_SHARD = N_MOD * D_MODEL // N_CHIPS


def _mod_fwd(c_all, mod_w, mod_b_sh):
    tn = 512

    def body(c_ref, w_ref, b_ref, o_ref):
        o_ref[0] = jnp.dot(_silu(c_ref[...]), w_ref[0], precision=HIGHEST, preferred_element_type=F32) + b_ref[0]

    return pl.pallas_call(
        body, name="mod_fwd", grid=(DEPTH, MOD_SHARD // tn),
        in_specs=[pl.BlockSpec((N_DEV, D_MODEL), lambda l, j: (0, 0)),
                  pl.BlockSpec((1, D_MODEL, tn), lambda l, j: (l, 0, j)),
                  pl.BlockSpec((1, 1, tn), lambda l, j: (l, 0, j))],
        out_specs=pl.BlockSpec((1, N_DEV, tn), lambda l, j: (l, 0, j)),
        out_shape=jax.ShapeDtypeStruct((DEPTH, N_DEV, MOD_SHARD), F32),
        compiler_params=_params(("arbitrary", "arbitrary")),
    )(c_all, mod_w, mod_b_sh)


def _mod_bwd(c_all, dmod_sh):
    tn = 512

    def body(c_ref, d_ref, o_ref):
        o_ref[0] = lax.dot_general(_silu(c_ref[...]), d_ref[0], (((0,), (0,)), ((), ())), precision=HIGHEST,
                                   preferred_element_type=F32)

    return pl.pallas_call(
        body, name="mod_bwd", grid=(DEPTH, MOD_SHARD // tn),
        in_specs=[pl.BlockSpec((N_DEV, D_MODEL), lambda l, j: (0, 0)),
                  pl.BlockSpec((1, N_DEV, tn), lambda l, j: (l, 0, j))],
        out_specs=pl.BlockSpec((1, D_MODEL, tn), lambda l, j: (l, 0, j)),
        out_shape=jax.ShapeDtypeStruct((DEPTH, D_MODEL, MOD_SHARD), F32),
        compiler_params=_params(("arbitrary", "arbitrary")),
    )(c_all, dmod_sh)


def _row_tile(rows, cap):
    if rows <= cap:
        return rows
    best = None
    for t in range(8, cap + 1, 8):
        if rows % t == 0:
            best = t
    assert best is not None, (rows, cap)
    return best


def _adamw(name, w, gs, m, v, tr=256):
    R, W = w.shape
    tr = _row_tile(R, tr)
    ng = len(gs)

    def body(*refs):
        w_ref, g_refs, (m_ref, v_ref) = refs[0], refs[1:1 + ng], refs[1 + ng:3 + ng]
        g_out, d_out, m_out, v_out = refs[3 + ng:]
        g = g_refs[0][...]
        for r in g_refs[1:]:
            g = g + r[...]
        mm = ADAM_B1 * m_ref[...] + (1.0 - ADAM_B1) * g
        vv = ADAM_B2 * v_ref[...] + (1.0 - ADAM_B2) * (g * g)
        m_hat = mm / (1.0 - ADAM_B1 ** ADAM_STEP)
        v_hat = vv / (1.0 - ADAM_B2 ** ADAM_STEP)
        g_out[...] = g
        d_out[...] = -ADAM_LR * (m_hat / (jnp.sqrt(v_hat) + ADAM_EPS) + ADAM_WD * w_ref[...])
        m_out[...] = mm
        v_out[...] = vv

    spec = pl.BlockSpec((tr, W), lambda i: (i, 0))
    return pl.pallas_call(
        body, name=name, grid=(R // tr,), in_specs=[spec] * (3 + ng), out_specs=[spec] * 4,
        out_shape=[jax.ShapeDtypeStruct((R, W), F32)] * 4, compiler_params=_params(("arbitrary",)),
    )(w, *gs, m, v)


def _sum_slots(name, a, tr=256):
    n, R, W = a.shape
    tr = _row_tile(R, tr)

    def body(a_ref, o_ref):
        acc = _f(a_ref[0])
        for j in range(1, n):
            acc = acc + _f(a_ref[j])
        o_ref[...] = acc

    return pl.pallas_call(
        body, name=name, grid=(R // tr,), in_specs=[pl.BlockSpec((n, tr, W), lambda i: (0, i, 0))],
        out_specs=pl.BlockSpec((tr, W), lambda i: (i, 0)), out_shape=jax.ShapeDtypeStruct((R, W), F32),
        compiler_params=_params(("arbitrary",)),
    )(a)


def _here():
    return lax.axis_index("x"), lax.axis_index("y"), lax.axis_index("c")


def _flip(v, d):
    return 1 - v if d else v


def _allgather_small(name, buf):
    R = buf.shape[0]
    rel = [(dx, dy, dc) for dx in (0, 1) for dy in (0, 1) for dc in (0, 1)][1:]

    def body(x_ref, o_ref, send, recv, lsem):
        x, y, c = _here()
        me = 4 * x + 2 * y + c
        mine = pltpu.make_async_copy(x_ref, o_ref.at[me], lsem)
        mine.start()

        def copy(k, slot):
            dx, dy, dc = rel[k]
            return pltpu.make_async_remote_copy(
                src_ref=x_ref, dst_ref=o_ref.at[slot], send_sem=send.at[k], recv_sem=recv.at[k],
                device_id=(_flip(x, dx), _flip(y, dy), _flip(c, dc)), device_id_type=MESH_ID)

        sent = [copy(k, me) for k in range(len(rel))]
        for cp in sent:
            cp.start()
        for k, (dx, dy, dc) in enumerate(rel):
            copy(k, 4 * _flip(x, dx) + 2 * _flip(y, dy) + _flip(c, dc)).wait_recv()
        for cp in sent:
            cp.wait_send()
        mine.wait()

    return pl.pallas_call(
        body, name=name, out_shape=jax.ShapeDtypeStruct((N_DEV, R, LANES), F32),
        in_specs=[pl.BlockSpec(memory_space=pltpu.VMEM)], out_specs=pl.BlockSpec(memory_space=pltpu.VMEM),
        scratch_shapes=[pltpu.SemaphoreType.DMA((7,)), pltpu.SemaphoreType.DMA((7,)), pltpu.SemaphoreType.DMA],
    )(buf)


CHIP_REL = [(1, 0), (0, 1), (1, 1)]


def _chip_scatter(name, arrays):
    n = len(arrays)

    def body(*refs):
        ins, outs = refs[:n], refs[n:2 * n]
        send, recv, lsem = refs[2 * n:]
        x, y, c = _here()
        s = 2 * x + y

        def copy(w, k, slot):
            dx, dy = CHIP_REL[k]
            px, py = _flip(x, dx), _flip(y, dy)
            return pltpu.make_async_remote_copy(
                src_ref=ins[w].at[2 * px + py], dst_ref=outs[w].at[slot], send_sem=send.at[3 * w + k],
                recv_sem=recv.at[3 * w + k], device_id=(px, py, c), device_id_type=MESH_ID)

        local = [pltpu.make_async_copy(ins[w].at[s], outs[w].at[s], lsem.at[w]) for w in range(n)]
        for cp in local:
            cp.start()
        sent = [copy(w, k, s) for w in range(n) for k in range(3)]
        for cp in sent:
            cp.start()
        for w in range(n):
            for k, (dx, dy) in enumerate(CHIP_REL):
                copy(w, k, 2 * _flip(x, dx) + _flip(y, dy)).wait_recv()
        for cp in sent:
            cp.wait_send()
        for cp in local:
            cp.wait()

    any_spec = pl.BlockSpec(memory_space=pl.ANY)
    return pl.pallas_call(
        body, name=name, out_shape=[jax.ShapeDtypeStruct(a.shape, a.dtype) for a in arrays],
        in_specs=[any_spec] * n, out_specs=[any_spec] * n,
        scratch_shapes=[pltpu.SemaphoreType.DMA((3 * n,)), pltpu.SemaphoreType.DMA((3 * n,)), pltpu.SemaphoreType.DMA((n,))],
    )(*arrays)


def _gather_weights(shards):
    n = len(shards)

    def body(*refs):
        ins, outs = refs[:n], refs[n:2 * n]
        send, recv, fsend, frecv, lsem = refs[2 * n:]
        x, y, c = _here()
        s = 2 * x + y

        def chip_of(k):
            dx, dy = CHIP_REL[k]
            return _flip(x, dx), _flip(y, dy)

        def over_ici(w, k, slot):
            px, py = chip_of(k)
            return pltpu.make_async_remote_copy(
                src_ref=ins[w].at[c], dst_ref=outs[w].at[slot].at[c], send_sem=send.at[3 * w + k], recv_sem=recv.at[3 * w + k],
                device_id=(px, py, c), device_id_type=MESH_ID)

        def to_sibling(w, k, layer):
            px, py = chip_of(k)
            part = outs[w].at[2 * px + py].at[layer]
            return pltpu.make_async_remote_copy(
                src_ref=part, dst_ref=part, send_sem=fsend.at[3 * w + k], recv_sem=frecv.at[3 * w + k],
                device_id=(x, y, 1 - c), device_id_type=MESH_ID)

        local = [pltpu.make_async_copy(ins[w], outs[w].at[s], lsem.at[w]) for w in range(n)]
        for cp in local:
            cp.start()
        sent = [over_ici(w, k, s) for w in range(n) for k in range(3)]
        for cp in sent:
            cp.start()
        passed = []
        for w in range(n):
            for k in range(3):
                px, py = chip_of(k)
                over_ici(w, k, 2 * px + py).wait_recv()
                passed.append(to_sibling(w, k, c))
                passed[-1].start()
        for w in range(n):
            for k in range(3):
                to_sibling(w, k, 1 - c).wait_recv()
        for cp in sent + passed:
            cp.wait_send()
        for cp in local:
            cp.wait()

    any_spec = pl.BlockSpec(memory_space=pl.ANY)
    return pl.pallas_call(
        body, name="gather_weights", out_shape=[jax.ShapeDtypeStruct((N_CHIPS,) + a.shape, a.dtype) for a in shards],
        in_specs=[any_spec] * n, out_specs=[any_spec] * n,
        scratch_shapes=[pltpu.SemaphoreType.DMA((3 * n,))] * 4 + [pltpu.SemaphoreType.DMA((n,))],
    )(*shards)


def _sibling_exchange(name, arrays):
    n = len(arrays)

    def body(*refs):
        ins, outs = refs[:n], refs[n:2 * n]
        send, recv = refs[2 * n:]
        x, y, c = _here()
        cps = [pltpu.make_async_remote_copy(src_ref=ins[w], dst_ref=outs[w], send_sem=send.at[w], recv_sem=recv.at[w],
                                            device_id=(x, y, 1 - c), device_id_type=MESH_ID) for w in range(n)]
        for cp in cps:
            cp.start()
        for cp in cps:
            cp.wait()

    any_spec = pl.BlockSpec(memory_space=pl.ANY)
    return pl.pallas_call(
        body, name=name, out_shape=[jax.ShapeDtypeStruct(a.shape, a.dtype) for a in arrays],
        in_specs=[any_spec] * n, out_specs=[any_spec] * n,
        scratch_shapes=[pltpu.SemaphoreType.DMA((n,)), pltpu.SemaphoreType.DMA((n,))],
    )(*arrays)


def _pack(arrs):
    flat = jnp.concatenate([a.reshape(-1).astype(F32) for a in arrs])
    n = flat.shape[0]
    rows = -(-n // (8 * LANES)) * 8
    return jnp.pad(flat, (0, rows * LANES - n)).reshape(rows, LANES)


def _unpack(buf, shapes):
    lead = buf.shape[:-2]
    flat = buf.reshape(lead + (-1,))
    out, off = [], 0
    for s in shapes:
        n = 1
        for d in s:
            n *= d
        out.append(flat[..., off:off + n].reshape(lead + tuple(s)))
        off += n
    return out


def _pad_w_in(w):
    return jnp.concatenate([w[:, :2048], w[:, 2816:2824], jnp.zeros((w.shape[0], P_XBC - P_DT - 8), w.dtype),
                            w[:, 2048:2816], w[:, 2824:]], axis=1)


def _unpad_w_in(g):
    return jnp.concatenate([g[:, :P_DT], g[:, P_XBC:P_G], g[:, P_DT:P_DT + 8], g[:, P_G:]], axis=1)


FFN_HALF = FFN_HIDDEN // 2


def _ffn_in_cols(w):
    h = FFN_HALF
    return jnp.concatenate([w[:, :h], w[:, 2 * h:3 * h], w[:, h:2 * h], w[:, 3 * h:]], axis=1)


def _row(v):
    return v.reshape(1, -1)


BIG = (("w_in", 2), ("w_sc_out", 2), ("w_sb_out", 2), ("w_ssm_out", 2), ("w_o", 1), ("w_ffn_in", 2), ("w_ffn_out", 1))
SMALL = ("mod_b", "g_pre_mix", "g_post_mix", "g_pre_ffn", "g_post_ffn", "sc_conv_w", "ssm_conv_w", "ssm_conv_b",
         "ssm_dt_bias", "ssm_a_log", "ssm_d", "ssm_norm_w")
WEIGHT_ORDER = ("mod_w", "mod_b", "g_pre_mix", "g_post_mix", "g_pre_ffn", "g_post_ffn", "w_in", "sc_conv_w",
                "ssm_conv_w", "ssm_conv_b", "ssm_dt_bias", "ssm_a_log", "ssm_d", "ssm_norm_w", "w_sc_out", "w_sb_out",
                "w_ssm_out", "w_o", "w_ffn_in", "w_ffn_out")


def _layer_fwd(l, x_in, h, W, V):
    S = {"x_in": x_in, "h": h}
    P = _mm(f"in_proj{l}", h, W["w_in"], "nn", BF16, tn_cap=1024)
    S["P"] = P
    S["ya"] = _shortconv_fwd(P, V["sc_w"])
    S["yb"], S["yb32"] = _sb_fwd(P)
    S["pre"] = _ssmconv_fwd(P, V["ssm_w"], V["ssm_b"])
    S["y_ssd"], S["states"] = _ssd_fwd(S["pre"], P, V["dtb"], V["alog"])
    S["yc"] = _rowwise(f"ssm_post{l}", lambda y, px, z, d, nw: _ssm_post(y, px, _f(z), d, nw),
                       [(S["y_ssd"], SSM_INNER, 0), (S["pre"], SSM_INNER, 0), (P, SSM_INNER, P_Z // SSM_INNER)],
                       [V["d_full"], V["norm_w"]], [(SSM_INNER, BF16)])[0]
    S["Ya"] = _mm(f"sc_out{l}", S["ya"], W["w_sc_out"], "nn", BF16)
    S["Yb"] = _mm(f"sb_out{l}", S["yb"], W["w_sb_out"], "nn", BF16)
    S["Yc"] = _mm(f"ssm_out{l}", S["yc"], W["w_ssm_out"], "nn", BF16)
    gb = P_G // D_MODEL
    S["merged"] = _rowwise(f"merge{l}", lambda *t: _merge(*[_f(v) for v in t]),
                           [(P, D_MODEL, gb), (P, D_MODEL, gb + 1), (P, D_MODEL, gb + 2),
                            (S["Ya"], D_MODEL, 0), (S["Yb"], D_MODEL, 0), (S["Yc"], D_MODEL, 0)], [], [(D_MODEL, BF16)])[0]
    S["mix"] = _mm(f"w_o{l}", S["merged"], W["w_o"], "nn", BF16)
    S["x1"], S["h2"] = _mid_fwd(f"mid_mix{l}", x_in, S["mix"], V["mid_mix"])
    S["GU"], S["act"] = _mm_epi(f"ffn_in{l}", S["h2"], W["w_ffn_in"], "nn", 2 * FFN_HALF, [],
                                lambda p: (p, _swiglu(p[:, :FFN_HALF], p[:, FFN_HALF:])),
                                [(2 * FFN_HALF, BF16), (FFN_HALF, BF16)])
    S["f"] = _mm(f"ffn_out{l}", S["act"], W["w_ffn_out"], "nn", BF16)
    return S


def _layer_bwd(l, S, W, V, dx1, df):
    G = {}
    P = S["P"]
    G["w_ffn_out"] = _mm(f"gw_ffn_out{l}", S["act"], df, "tn", F32)

    def swiglu_bwd(d_act, gu):
        _, vjp = jax.vjp(_swiglu, _f(gu[:, :FFN_HALF]), _f(gu[:, FFN_HALF:]))
        return (jnp.concatenate(vjp(d_act), axis=1),)

    dGU = _mm_epi(f"d_gu{l}", df, W["w_ffn_out"], "nt", FFN_HALF, [(S["GU"], 2 * FFN_HALF)], swiglu_bwd,
                  [(2 * FFN_HALF, BF16)])[0]
    dh2 = _mm(f"d_h2{l}", dGU, W["w_ffn_in"], "nt", BF16)
    G["w_ffn_in"] = _ffn_in_cols(_mm(f"gw_ffn_in{l}", S["h2"], dGU, "tn", F32))
    dx, dmix, G["gate1"], G["g_post_mix"], G["g_pre_ffn"], G["scale2"], G["shift2"] = _mid_bwd(
        f"mid_mix_bwd{l}", S["x_in"], S["mix"], dx1, dh2, V["mid_mix"])
    dmerged = _mm(f"d_merged{l}", dmix, W["w_o"], "nt", BF16)
    G["w_o"] = _mm(f"gw_o{l}", S["merged"], dmix, "tn", F32)

    def merge_bwd(ga, gb, gc, ya, yb, yc, d):
        _, vjp = jax.vjp(_merge, *[_f(v) for v in (ga, gb, gc, ya, yb, yc)])
        dga, dgb, dgc, dya, dyb, dyc = vjp(_f(d))
        return jnp.concatenate([dga, dgb, dgc], axis=1), dya, dyb, dyc

    gb = P_G // D_MODEL
    dG, dYa, dYb, dYc = _rowwise(
        f"merge_bwd{l}", merge_bwd,
        [(P, D_MODEL, gb), (P, D_MODEL, gb + 1), (P, D_MODEL, gb + 2), (S["Ya"], D_MODEL, 0), (S["Yb"], D_MODEL, 0),
         (S["Yc"], D_MODEL, 0), (dmerged, D_MODEL, 0)], [], [(3 * D_MODEL, BF16)] + [(D_MODEL, BF16)] * 3)
    dya = _mm(f"d_ya{l}", dYa, W["w_sc_out"], "nt", BF16)
    dyb = _mm(f"d_yb{l}", dYb, W["w_sb_out"], "nt", BF16)
    dyc = _mm(f"d_yc{l}", dYc, W["w_ssm_out"], "nt", BF16)
    G["w_sc_out"] = _mm(f"gw_sc_out{l}", S["ya"], dYa, "tn", F32)
    G["w_sb_out"] = _mm(f"gw_sb_out{l}", S["yb"], dYb, "tn", F32)
    G["w_ssm_out"] = _mm(f"gw_ssm_out{l}", S["yc"], dYc, "tn", F32)

    def post_bwd(y, px, z, d, dfull, nw):
        _, vjp = jax.vjp(_ssm_post, y, px, _f(z), dfull, nw)
        return vjp(_f(d))

    dy_ssd, dxs, dz, G["d_full"], G["ssm_norm_w"] = _rowwise(
        f"ssm_post_bwd{l}", post_bwd,
        [(S["y_ssd"], SSM_INNER, 0), (S["pre"], SSM_INNER, 0), (P, SSM_INNER, P_Z // SSM_INNER), (dyc, SSM_INNER, 0)],
        [V["d_full"], V["norm_w"]], [(SSM_INNER, F32), (SSM_INNER, F32), (SSM_INNER, BF16)], [(1, SSM_INNER)] * 2)
    dpre, ddt, G["dtb"], G["alog"] = _ssd_bwd(S["pre"], P, S["states"], dy_ssd, dxs, V["dtb"], V["alog"])
    dxbc, w0, w1, w2, w3, G["ssm_conv_b"] = _ssmconv_bwd(P, dpre, V["ssm_w"])
    G["ssm_conv_w"] = jnp.concatenate([w0, w1, w2, w3], axis=0)
    dq, dk, dv = _sb_bwd(P, dyb, S["yb32"])
    dA, s0, s1, s2 = _shortconv_bwd(P, dya, V["sc_w"])
    G["sc_conv_w"] = jnp.concatenate([s0, s1, s2], axis=0)
    dP = _rowwise(f"assemble_dp{l}", lambda *t: jnp.concatenate([v.astype(BF16) for v in t], axis=1),
                  [(dA, 3 * SC_WIDTH, 0), (dq, 256, 0), (dk, 256, 0), (dv, 256, 0), (dz, SSM_INNER, 0), (ddt, DT_PAD, 0),
                   (dxbc, SSM_CONV_DIM, 0), (dG, 3 * D_MODEL, 0)], [], [(P_WIDTH, BF16)])[0]
    dh = _mm(f"d_h{l}", dP, W["w_in"], "nt", BF16, tk_cap=2048)
    G["w_in"] = _mm(f"gw_in{l}", S["h"], dP, "tn", F32, tn_cap=1024)
    return dx, dh, G


def kernel(x, c, mod_w, mod_b, g_pre_mix, g_post_mix, g_pre_ffn, g_post_ffn, w_in, sc_conv_w, ssm_conv_w, ssm_conv_b, ssm_dt_bias, ssm_a_log, ssm_d, ssm_norm_w, w_sc_out, w_sb_out, w_ssm_out, w_o, w_ffn_in, w_ffn_out, loss_target, m_mod_w, m_mod_b, m_g_pre_mix, m_g_post_mix, m_g_pre_ffn, m_g_post_ffn, m_w_in, m_sc_conv_w, m_ssm_conv_w, m_ssm_conv_b, m_ssm_dt_bias, m_ssm_a_log, m_ssm_d, m_ssm_norm_w, m_w_sc_out, m_w_sb_out, m_w_ssm_out, m_w_o, m_w_ffn_in, m_w_ffn_out, v_mod_w, v_mod_b, v_g_pre_mix, v_g_post_mix, v_g_pre_ffn, v_g_post_ffn, v_w_in, v_sc_conv_w, v_ssm_conv_w, v_ssm_conv_b, v_ssm_dt_bias, v_ssm_a_log, v_ssm_d, v_ssm_norm_w, v_w_sc_out, v_w_sb_out, v_w_ssm_out, v_w_o, v_w_ffn_in, v_w_ffn_out):
    wts = dict(mod_w=mod_w, mod_b=mod_b, g_pre_mix=g_pre_mix, g_post_mix=g_post_mix, g_pre_ffn=g_pre_ffn,
               g_post_ffn=g_post_ffn, w_in=w_in, sc_conv_w=sc_conv_w, ssm_conv_w=ssm_conv_w, ssm_conv_b=ssm_conv_b,
               ssm_dt_bias=ssm_dt_bias, ssm_a_log=ssm_a_log, ssm_d=ssm_d, ssm_norm_w=ssm_norm_w, w_sc_out=w_sc_out,
               w_sb_out=w_sb_out, w_ssm_out=w_ssm_out, w_o=w_o, w_ffn_in=w_ffn_in, w_ffn_out=w_ffn_out)
    mom = dict(mod_w=m_mod_w, mod_b=m_mod_b, g_pre_mix=m_g_pre_mix, g_post_mix=m_g_post_mix, g_pre_ffn=m_g_pre_ffn,
               g_post_ffn=m_g_post_ffn, w_in=m_w_in, sc_conv_w=m_sc_conv_w, ssm_conv_w=m_ssm_conv_w,
               ssm_conv_b=m_ssm_conv_b, ssm_dt_bias=m_ssm_dt_bias, ssm_a_log=m_ssm_a_log, ssm_d=m_ssm_d,
               ssm_norm_w=m_ssm_norm_w, w_sc_out=m_w_sc_out, w_sb_out=m_w_sb_out, w_ssm_out=m_w_ssm_out, w_o=m_w_o,
               w_ffn_in=m_w_ffn_in, w_ffn_out=m_w_ffn_out)
    var = dict(mod_w=v_mod_w, mod_b=v_mod_b, g_pre_mix=v_g_pre_mix, g_post_mix=v_g_post_mix, g_pre_ffn=v_g_pre_ffn,
               g_post_ffn=v_g_post_ffn, w_in=v_w_in, sc_conv_w=v_sc_conv_w, ssm_conv_w=v_ssm_conv_w,
               ssm_conv_b=v_ssm_conv_b, ssm_dt_bias=v_ssm_dt_bias, ssm_a_log=v_ssm_a_log, ssm_d=v_ssm_d,
               ssm_norm_w=v_ssm_norm_w, w_sc_out=v_w_sc_out, w_sb_out=v_w_sb_out, w_ssm_out=v_w_ssm_out, w_o=v_w_o,
               w_ffn_in=v_w_ffn_in, w_ffn_out=v_w_ffn_out)
    xi, yi, ci = _here()
    chip = 2 * xi + yi
    me = 4 * xi + 2 * yi + ci
    x0, target = x[0], loss_target[0]

    first_shapes = [(D_MODEL,), sc_conv_w.shape, ssm_conv_w.shape]
    g0 = _allgather_small("gather_cond", _pack([c, sc_conv_w, ssm_conv_w]))
    c_rows, sc_sh, ssm_sh = _unpack(g0, first_shapes)
    c_all = c_rows
    sc_w = jnp.concatenate([sc_sh[2 * j] for j in range(N_CHIPS)], axis=-1)
    ssm_w = jnp.concatenate([ssm_sh[2 * j] for j in range(N_CHIPS)], axis=-1)

    mod_b_sh = lax.dynamic_slice_in_dim(mod_b, chip * MOD_SHARD, MOD_SHARD, axis=1).reshape(DEPTH, 1, MOD_SHARD)
    modpart = _mod_fwd(c_all, mod_w, mod_b_sh)
    g1 = _allgather_small("gather_mod", modpart.reshape(-1, LANES)).reshape(N_DEV, DEPTH, N_DEV, MOD_SHARD)
    mod = jnp.concatenate([lax.dynamic_index_in_dim(g1[2 * j], me, axis=1, keepdims=False) for j in range(N_CHIPS)],
                          axis=-1)

    gathered = _gather_weights([wts[n].astype(BF16) for n, _ in BIG])
    full = {n: jnp.concatenate([g[j] for j in range(N_CHIPS)], axis=ax) for (n, ax), g in zip(BIG, gathered)}

    Ws, Vs = [], []
    for l in range(DEPTH):
        W = {n: full[n][l] for n, _ in BIG}
        W["w_in"] = _pad_w_in(W["w_in"])
        W["w_ffn_in"] = _ffn_in_cols(W["w_ffn_in"])
        Ws.append(W)
        sh1, sc1, gt1, sh2, sc2, gt2 = [_row(v) for v in jnp.split(mod[l], N_MOD)]
        Vs.append(dict(
            shift1=sh1, scale1=sc1, g_pre_mix=_row(g_pre_mix[l]),
            mid_mix=[gt1, _row(g_post_mix[l]), _row(g_pre_ffn[l]), sc2, sh2],
            gate2=gt2, g_post_ffn=_row(g_post_ffn[l]),
            sc_w=sc_w[l], ssm_w=ssm_w[l], ssm_b=_row(ssm_conv_b[l]),
            dtb=_row(jnp.pad(ssm_dt_bias[l], (0, LANES - SSM_HEADS))), alog=_row(jnp.pad(ssm_a_log[l], (0, LANES - SSM_HEADS))),
            d_full=_row(jnp.repeat(ssm_d[l], SSM_INNER // SSM_HEADS)), norm_w=_row(ssm_norm_w[l])))

    def mid_ffn_vecs(l):
        return [Vs[l]["gate2"], Vs[l]["g_post_ffn"], Vs[l + 1]["g_pre_mix"], Vs[l + 1]["scale1"], Vs[l + 1]["shift1"]]

    saved = []
    x_in = x0
    h = _first_fwd(x0, [Vs[0]["g_pre_mix"], Vs[0]["scale1"], Vs[0]["shift1"]])
    for l in range(DEPTH):
        S = _layer_fwd(l, x_in, h, Ws[l], Vs[l])
        saved.append(S)
        if l + 1 < DEPTH:
            x_in, h = _mid_fwd(f"mid_ffn{l}", S["x1"], S["f"], mid_ffn_vecs(l))

    GL = [None] * DEPTH
    S = saved[-1]
    dx1, df, g_gate2, g_gpf, loss_cols = _last_bwd(S["x1"], S["f"], target, [Vs[-1]["gate2"], Vs[-1]["g_post_ffn"]])
    for l in reversed(range(DEPTH)):
        dx, dh, G = _layer_bwd(l, saved[l], Ws[l], Vs[l], dx1, df)
        G["gate2"], G["g_post_ffn"] = g_gate2, g_gpf
        GL[l] = G
        if l > 0:
            Sp = saved[l - 1]
            dx1, df, g_gate2, g_gpf, G["g_pre_mix"], G["scale1"], G["shift1"] = _mid_bwd(
                f"mid_ffn_bwd{l - 1}", Sp["x1"], Sp["f"], dx, dh, mid_ffn_vecs(l - 1))
        else:
            grad_x, G["g_pre_mix"], G["scale1"], G["shift1"] = _first_bwd(
                x0, dx, dh, [Vs[0]["g_pre_mix"], Vs[0]["scale1"], Vs[0]["shift1"]])
    loss = lax.psum(jnp.sum(loss_cols), ("x", "y", "c"))

    def both(key, shape=None):
        a = jnp.stack([GL[l][key] for l in range(DEPTH)])
        return a if shape is None else a.reshape(shape)

    dmod = jnp.concatenate([both(k, (DEPTH, D_MODEL)) for k in ("shift1", "scale1", "gate1", "shift2", "scale2", "gate2")],
                           axis=1)
    part_small = dict(
        mod_b=dmod, g_pre_mix=both("g_pre_mix", (DEPTH, D_MODEL)), g_post_mix=both("g_post_mix", (DEPTH, D_MODEL)),
        g_pre_ffn=both("g_pre_ffn", (DEPTH, D_MODEL)), g_post_ffn=both("g_post_ffn", (DEPTH, D_MODEL)),
        sc_conv_w=both("sc_conv_w"), ssm_conv_w=both("ssm_conv_w"), ssm_conv_b=both("ssm_conv_b", (DEPTH, SSM_CONV_DIM)),
        ssm_dt_bias=both("dtb", (DEPTH, LANES))[:, :SSM_HEADS], ssm_a_log=both("alog", (DEPTH, LANES))[:, :SSM_HEADS],
        ssm_d=both("d_full", (DEPTH, SSM_HEADS, SSM_INNER // SSM_HEADS)).sum(-1),
        ssm_norm_w=both("ssm_norm_w", (DEPTH, SSM_INNER)))
    small_shapes = [part_small[n].shape for n in SMALL]
    g2 = _allgather_small("gather_small_grads", _pack([part_small[n] for n in SMALL]))
    tot = dict(zip(SMALL, _unpack(_sum_slots("sum_small_grads", g2), small_shapes)))
    dmod_all = _unpack(g2, small_shapes)[0]
    dmod_sh = jnp.swapaxes(lax.dynamic_slice_in_dim(dmod_all, chip * MOD_SHARD, MOD_SHARD, axis=2), 0, 1)
    grads = {"mod_w": _mod_bwd(c_all, dmod_sh)}
    for n in SMALL:
        grads[n] = tot[n]
    grads["sc_conv_w"] = lax.dynamic_slice_in_dim(tot["sc_conv_w"], chip * 64, 64, axis=2)
    grads["ssm_conv_w"] = lax.dynamic_slice_in_dim(tot["ssm_conv_w"], chip * 192, 192, axis=2)

    pieces = []
    for n, ax in BIG:
        g = jnp.stack([_unpad_w_in(GL[l][n]) if n == "w_in" else GL[l][n] for l in range(DEPTH)])
        pieces.append(jnp.stack(jnp.split(g, N_CHIPS, axis=ax)).astype(BF16))
    landed = _chip_scatter("scatter_grads", pieces)
    mine = [_sum_slots(f"sum_{n}", a.reshape(N_CHIPS, -1, a.shape[-1])) for (n, _), a in zip(BIG, landed)]
    theirs = _sibling_exchange("swap_core_sums", mine)

    out = {}

    def update(name, w2, gs, m2, v2, shape):
        g, d, nm, nv = _adamw(f"adamw_{name}", w2, gs, m2, v2)
        out[name] = tuple(a.reshape(shape) for a in (g, d, nm, nv))

    for (n, _), a, b in zip(BIG, mine, theirs):
        shp = wts[n].shape
        two = (-1, shp[-1])
        update(n, wts[n].reshape(two), [a, b], mom[n].reshape(two), var[n].reshape(two), shp)
    two = (-1, MOD_SHARD)
    update("mod_w", mod_w.reshape(two), [grads["mod_w"].reshape(two)], m_mod_w.reshape(two), v_mod_w.reshape(two), mod_w.shape)
    shapes = [wts[n].shape for n in SMALL]
    res = _adamw("adamw_small", _pack([wts[n] for n in SMALL]), [_pack([grads[n] for n in SMALL])],
                 _pack([mom[n] for n in SMALL]), _pack([var[n] for n in SMALL]))
    for n, g, d, nm, nv in zip(SMALL, *[_unpack(r, shapes) for r in res]):
        out[n] = (g, d, nm, nv)

    result = [loss, grad_x[None]]
    for k in range(4):
        result += [out[n][k] for n in WEIGHT_ORDER]
    return tuple(result)
```

```python
import functools

import jax
import jax.numpy as jnp
from jax import lax
from jax.experimental import pallas as pl
from jax.experimental.pallas import tpu as pltpu

F32 = jnp.float32
BF16 = jnp.bfloat16
HIGHEST = lax.Precision.HIGHEST
MESH_ID = pl.DeviceIdType.MESH

D_MODEL = 1024
DEPTH = 2
SC_WIDTH = 256
SC_KERNEL = 3
SB_HEAD_DIM = 64
SSM_INNER = 512
SSM_HEADS = 8
SSM_STATE = 64
SSM_CONV = 4
SSM_CHUNK = 256
SSM_CONV_DIM = 768
FFN_HIDDEN = 2816
NORM_EPS = 1e-6
N_MOD = 6
N_CHIPS = 4
N_DEV = 8

ADAM_LR = 0.001
ADAM_B1 = 0.9
ADAM_B2 = 0.999
ADAM_EPS = 1e-08
ADAM_WD = 0.01
ADAM_STEP = 10

P_WIDTH = 6144
P_A, P_B, P_Z, P_DT, P_XBC, P_G = 0, 768, 1536, 2048, 2304, 3072
DT_PAD = 256

VMEM_LIMIT_BYTES = 56 * 1024 * 1024
LANES = 128

SB_LOG_CUTOFF = -105.0
SB_TQ = 256
SB_TK = 256


def _params(sem):
    return pltpu.CompilerParams(dimension_semantics=sem, vmem_limit_bytes=VMEM_LIMIT_BYTES)


def _pick(n, cap):
    if n <= cap:
        return n
    best = None
    for m in range(LANES, cap + 1, LANES):
        if n % m == 0:
            best = m
    assert best is not None, (n, cap)
    return best


def _rowwise(name, fn, rows, vecs, row_outs, acc_outs=(), tl=256):
    L = rows[0][0].shape[0]
    tl = min(tl, L)
    assert L % tl == 0
    n_in = len(rows) + len(vecs)
    n_ro = len(row_outs)

    def body(*refs):
        ins, ro, ao = refs[:n_in], refs[n_in:n_in + n_ro], refs[n_in + n_ro:]
        vals = fn(*[r[...] for r in ins])
        if not isinstance(vals, (tuple, list)):
            vals = (vals,)
        for o, v in zip(ro, vals[:n_ro]):
            o[...] = v.astype(o.dtype)
        if ao:
            @pl.when(pl.program_id(0) == 0)
            def _():
                for o in ao:
                    o[...] = jnp.zeros_like(o)
            for o, v in zip(ao, vals[n_ro:]):
                o[...] += v.astype(F32)

    in_specs = [pl.BlockSpec((tl, w), functools.partial(lambda i, cb: (i, cb), cb=cb)) for _, w, cb in rows]
    in_specs += [pl.BlockSpec(v.shape, lambda i: (0, 0)) for v in vecs]
    out_specs = [pl.BlockSpec((tl, w), lambda i: (i, 0)) for w, _ in row_outs]
    out_specs += [pl.BlockSpec(s, lambda i: (0, 0)) for s in acc_outs]
    out_shape = [jax.ShapeDtypeStruct((L, w), dt) for w, dt in row_outs]
    out_shape += [jax.ShapeDtypeStruct(s, F32) for s in acc_outs]
    return pl.pallas_call(
        body, name=name, grid=(L // tl,), in_specs=in_specs, out_specs=out_specs, out_shape=out_shape,
        compiler_params=_params(("arbitrary",)),
    )(*[a for a, _, _ in rows], *vecs)


def _mm(name, a, b, mode, out_dtype, tm=1024, tn_cap=1408, tk_cap=2816):
    if mode == "nn":
        (M, K), (_, N) = a.shape, b.shape
    elif mode == "nt":
        (M, K), (N, _) = a.shape, b.shape
    else:
        (K, M), (_, N) = a.shape, b.shape
        tm, tk_cap = 1408, 2048
    tm = _pick(M, tm)
    tn = _pick(N, tn_cap)
    tk = _pick(K, tk_cap)
    nk = K // tk

    def body(a_ref, b_ref, o_ref, *scr):
        if mode == "nn":
            p = jnp.dot(a_ref[...], b_ref[...], preferred_element_type=F32)
        elif mode == "nt":
            p = lax.dot_general(a_ref[...], b_ref[...], (((1,), (1,)), ((), ())), preferred_element_type=F32)
        else:
            p = lax.dot_general(a_ref[...], b_ref[...], (((0,), (0,)), ((), ())), preferred_element_type=F32)
        if nk == 1:
            o_ref[...] = p.astype(o_ref.dtype)
        else:
            acc = scr[0]
            k = pl.program_id(2)

            @pl.when(k == 0)
            def _():
                acc[...] = p

            @pl.when(k > 0)
            def _():
                acc[...] += p

            @pl.when(k == nk - 1)
            def _():
                o_ref[...] = acc[...].astype(o_ref.dtype)

    if mode == "nn":
        a_spec = pl.BlockSpec((tm, tk), lambda i, j, k: (i, k))
        b_spec = pl.BlockSpec((tk, tn), lambda i, j, k: (k, j))
    elif mode == "nt":
        a_spec = pl.BlockSpec((tm, tk), lambda i, j, k: (i, k))
        b_spec = pl.BlockSpec((tn, tk), lambda i, j, k: (j, k))
    else:
        a_spec = pl.BlockSpec((tk, tm), lambda i, j, k: (k, i))
        b_spec = pl.BlockSpec((tk, tn), lambda i, j, k: (k, j))
    return pl.pallas_call(
        body, name=name, grid=(M // tm, N // tn, nk), in_specs=[a_spec, b_spec],
        out_specs=pl.BlockSpec((tm, tn), lambda i, j, k: (i, j)),
        out_shape=jax.ShapeDtypeStruct((M, N), out_dtype),
        scratch_shapes=[pltpu.VMEM((tm, tn), F32)] if nk > 1 else [],
        compiler_params=_params(("arbitrary", "arbitrary", "arbitrary")),
    )(a, b)


def _mm_epi(name, a, b, mode, tn, extras, epi, outs, tm=512):
    if mode == "nn":
        (M, K), (_, N) = a.shape, b.shape
    else:
        (M, K), (N, _) = a.shape, b.shape
    tm = _pick(M, tm)
    n_ex = len(extras)

    def body(*refs):
        a_ref, b_ref, ex, o_refs = refs[0], refs[1], refs[2:2 + n_ex], refs[2 + n_ex:]
        if mode == "nn":
            p = jnp.dot(a_ref[...], b_ref[...], preferred_element_type=F32)
        else:
            p = lax.dot_general(a_ref[...], b_ref[...], (((1,), (1,)), ((), ())), preferred_element_type=F32)
        for o, v in zip(o_refs, epi(p, *[r[...] for r in ex])):
            o[...] = v.astype(o.dtype)

    a_spec = pl.BlockSpec((tm, K), lambda i, j: (i, 0))
    b_spec = pl.BlockSpec((K, tn), lambda i, j: (0, j)) if mode == "nn" else pl.BlockSpec((tn, K), lambda i, j: (j, 0))
    return pl.pallas_call(
        body, name=name, grid=(M // tm, N // tn),
        in_specs=[a_spec, b_spec] + [pl.BlockSpec((tm, w), lambda i, j: (i, j)) for _, w in extras],
        out_specs=[pl.BlockSpec((tm, w), lambda i, j: (i, j)) for w, _ in outs],
        out_shape=[jax.ShapeDtypeStruct((M, (N // tn) * w), dt) for w, dt in outs],
        compiler_params=_params(("arbitrary", "arbitrary")),
    )(a, b, *[e for e, _ in extras])


def _f(x):
    return x.astype(F32)


def _silu(x):
    return x * jax.nn.sigmoid(x)


def _softplus(x):
    return jnp.maximum(x, 0.0) + jnp.log1p(jnp.exp(-jnp.abs(x)))


def _rms(x, g):
    r = lax.rsqrt(jnp.mean(x * x, axis=-1, keepdims=True) + NORM_EPS)
    return x * r * g


def _adaln(x, g, scale, shift):
    return _rms(x, g) * (1.0 + scale) + shift


def _resid(x, y, gate, g):
    return x + gate * _rms(y, g)


def _mid(x, y, gate, g_post, g_pre, scale, shift):
    x_new = _resid(x, y, gate, g_post)
    return x_new, _adaln(x_new, g_pre, scale, shift)


def _merge(ga, gb, gc, ya, yb, yc):
    return jax.nn.sigmoid(ga) * ya + jax.nn.sigmoid(gb) * yb + jax.nn.sigmoid(gc) * yc


def _swiglu(gt, up):
    return _silu(gt) * up


def _ssm_post(y_ssd, pre_xs, z, d_full, norm_w):
    y = (y_ssd + _silu(pre_xs) * d_full) * _silu(z)
    half = SSM_INNER // 2
    parts = []
    for g in range(2):
        yg = y[:, g * half:(g + 1) * half]
        parts.append(yg * lax.rsqrt(jnp.mean(yg * yg, axis=-1, keepdims=True) + NORM_EPS))
    return jnp.concatenate(parts, axis=1) * norm_w


def _first_fwd(x, vecs):
    return _rowwise("adaln_first", lambda x, g, sc, sh: _adaln(x, g, sc, sh),
                    [(x, D_MODEL, 0)], vecs, [(D_MODEL, BF16)], tl=512)[0]


def _mid_fwd(name, x, y, vecs):
    return _rowwise(name, lambda x, y, *v: _mid(x, _f(y), *v),
                    [(x, D_MODEL, 0), (y, D_MODEL, 0)], vecs, [(D_MODEL, F32), (D_MODEL, BF16)], tl=512)


def _mid_bwd(name, x, y, dx_new, dh, vecs):
    def fn(x, y, dxn, dh, *v):
        _, vjp = jax.vjp(_mid, x, _f(y), *v)
        return vjp((dxn, _f(dh)))

    vec = (1, D_MODEL)
    return _rowwise(name, fn, [(x, D_MODEL, 0), (y, D_MODEL, 0), (dx_new, D_MODEL, 0), (dh, D_MODEL, 0)], vecs,
                    [(D_MODEL, F32), (D_MODEL, BF16)], [vec] * 5)


def _first_bwd(x, dx_in, dh, vecs):
    def fn(x, dxi, dh, *v):
        _, vjp = jax.vjp(_adaln, x, *v)
        dx, dg, dsc, dsh = vjp(_f(dh))
        return dx + dxi, dg, dsc, dsh

    vec = (1, D_MODEL)
    return _rowwise("adaln_first_bwd", fn, [(x, D_MODEL, 0), (dx_in, D_MODEL, 0), (dh, D_MODEL, 0)], vecs,
                    [(D_MODEL, F32)], [vec] * 3)


def _last_bwd(x1, f, target, vecs):
    def fn(x1, f, t, gate, g):
        x2, vjp = jax.vjp(_resid, x1, _f(f), gate, g)
        err = x2 - t
        dx1, df, dgate, dg = vjp(err * (1.0 / D_MODEL))
        loss_cols = jnp.sum(err * err, axis=0, keepdims=True) * (0.5 / D_MODEL)
        return dx1, df, dgate, dg, loss_cols

    vec = (1, D_MODEL)
    return _rowwise("loss_last_bwd", fn, [(x1, D_MODEL, 0), (f, D_MODEL, 0), (target, D_MODEL, 0)], vecs,
                    [(D_MODEL, F32), (D_MODEL, BF16)], [vec] * 3)


HALO = 16


def _shift_down(u, prev, k):
    rows = lax.broadcasted_iota(jnp.int32, u.shape, 0)
    v = pltpu.roll(u, k, 0)
    for t in range(k):
        v = jnp.where(rows == t, prev[HALO - k + t:HALO - k + t + 1, :], v)
    return v


def _shift_up(u, nxt, k):
    n = u.shape[0]
    rows = lax.broadcasted_iota(jnp.int32, u.shape, 0)
    v = pltpu.roll(u, n - k, 0)
    for t in range(k):
        v = jnp.where(rows == n - k + t, nxt[t:t + 1, :], v)
    return v


def _conv_specs(L, tl, width, col_block):
    per = tl // HALO
    last = L // HALO - 1
    main = pl.BlockSpec((tl, width), lambda i: (i, col_block))
    before = pl.BlockSpec((HALO, width), lambda i: (jnp.maximum(i * per - 1, 0), col_block))
    after = pl.BlockSpec((HALO, width), lambda i: (jnp.minimum((i + 1) * per, last), col_block))
    return main, before, after


def _shortconv_fwd(P, w, tl=512):
    L = P.shape[0]
    tl = min(tl, L)
    C = SC_WIDTH
    main, before, _ = _conv_specs(L, tl, 3 * C, 0)

    def body(p_ref, h_ref, w_ref, o_ref):
        first = (pl.program_id(0) == 0)
        p, h = _f(p_ref[...]), _f(h_ref[...])
        b, u = p[:, :C], p[:, C:2 * C] * p[:, 2 * C:]
        uh = jnp.where(first, 0.0, h[:, C:2 * C] * h[:, 2 * C:])
        wv = w_ref[...]
        cv = wv[2:3] * u + wv[1:2] * _shift_down(u, uh, 1) + wv[0:1] * _shift_down(u, uh, 2)
        o_ref[...] = (b * cv).astype(o_ref.dtype)

    return pl.pallas_call(
        body, name="shortconv_fwd", grid=(L // tl,),
        in_specs=[main, before, pl.BlockSpec(w.shape, lambda i: (0, 0))],
        out_specs=pl.BlockSpec((tl, C), lambda i: (i, 0)),
        out_shape=jax.ShapeDtypeStruct((L, C), BF16), compiler_params=_params(("arbitrary",)),
    )(P, P, w)


def _shortconv_bwd(P, dya, w, tl=512):
    L = P.shape[0]
    tl = min(tl, L)
    C = SC_WIDTH
    main, before, after = _conv_specs(L, tl, 3 * C, 0)
    dmain, _, dafter = _conv_specs(L, tl, C, 0)
    n = L // tl

    def body(p_ref, h_ref, n_ref, d_ref, dn_ref, w_ref, o_ref, dw0, dw1, dw2):
        i = pl.program_id(0)
        p, h, nx = _f(p_ref[...]), _f(h_ref[...]), _f(n_ref[...])
        b, c, x = p[:, :C], p[:, C:2 * C], p[:, 2 * C:]
        u = c * x
        uh = jnp.where(i == 0, 0.0, h[:, C:2 * C] * h[:, 2 * C:])
        u1, u2 = _shift_down(u, uh, 1), _shift_down(u, uh, 2)
        wv = w_ref[...]
        cv = wv[2:3] * u + wv[1:2] * u1 + wv[0:1] * u2
        dy = _f(d_ref[...])
        dcv = dy * b
        dcv_n = jnp.where(i == n - 1, 0.0, _f(dn_ref[...]) * nx[:, :C])
        du = wv[2:3] * dcv + wv[1:2] * _shift_up(dcv, dcv_n, 1) + wv[0:1] * _shift_up(dcv, dcv_n, 2)
        o_ref[:, :C] = (dy * cv).astype(o_ref.dtype)
        o_ref[:, C:2 * C] = (du * x).astype(o_ref.dtype)
        o_ref[:, 2 * C:] = (du * c).astype(o_ref.dtype)

        @pl.when(i == 0)
        def _():
            for r in (dw0, dw1, dw2):
                r[...] = jnp.zeros_like(r)

        dw0[...] += jnp.sum(dcv * u2, axis=0, keepdims=True)
        dw1[...] += jnp.sum(dcv * u1, axis=0, keepdims=True)
        dw2[...] += jnp.sum(dcv * u, axis=0, keepdims=True)

    vec = pl.BlockSpec((1, C), lambda i: (0, 0))
    return pl.pallas_call(
        body, name="shortconv_bwd", grid=(n,),
        in_specs=[main, before, after, dmain, dafter, pl.BlockSpec(w.shape, lambda i: (0, 0))],
        out_specs=[pl.BlockSpec((tl, 3 * C), lambda i: (i, 0)), vec, vec, vec],
        out_shape=[jax.ShapeDtypeStruct((L, 3 * C), BF16)] + [jax.ShapeDtypeStruct((1, C), F32)] * 3,
        compiler_params=_params(("arbitrary",)),
    )(P, P, P, dya, dya, w)


def _ssmconv_fwd(P, w, bias, tl=512):
    L = P.shape[0]
    tl = min(tl, L)
    C = SSM_CONV_DIM
    main, before, _ = _conv_specs(L, tl, C, P_XBC // C)

    def body(p_ref, h_ref, w_ref, b_ref, o_ref):
        u = _f(p_ref[...])
        uh = jnp.where(pl.program_id(0) == 0, 0.0, _f(h_ref[...]))
        wv = w_ref[...]
        acc = wv[3:4] * u + b_ref[...]
        for k in range(1, SSM_CONV):
            acc = acc + wv[3 - k:4 - k] * _shift_down(u, uh, k)
        o_ref[...] = acc

    return pl.pallas_call(
        body, name="ssmconv_fwd", grid=(L // tl,),
        in_specs=[main, before, pl.BlockSpec(w.shape, lambda i: (0, 0)), pl.BlockSpec(bias.shape, lambda i: (0, 0))],
        out_specs=pl.BlockSpec((tl, C), lambda i: (i, 0)),
        out_shape=jax.ShapeDtypeStruct((L, C), F32), compiler_params=_params(("arbitrary",)),
    )(P, P, w, bias)


def _ssmconv_bwd(P, dpre, w, tl=512):
    L = P.shape[0]
    tl = min(tl, L)
    C = SSM_CONV_DIM
    main, before, _ = _conv_specs(L, tl, C, P_XBC // C)
    dmain, _, dafter = _conv_specs(L, tl, C, 0)
    n = L // tl

    def body(p_ref, h_ref, d_ref, dn_ref, w_ref, o_ref, dw0, dw1, dw2, dw3, db):
        i = pl.program_id(0)
        u = _f(p_ref[...])
        uh = jnp.where(i == 0, 0.0, _f(h_ref[...]))
        d = d_ref[...]
        dn = jnp.where(i == n - 1, 0.0, dn_ref[...])
        wv = w_ref[...]
        du = wv[3:4] * d
        for k in range(1, SSM_CONV):
            du = du + wv[3 - k:4 - k] * _shift_up(d, dn, k)
        o_ref[...] = du.astype(o_ref.dtype)

        @pl.when(i == 0)
        def _():
            for r in (dw0, dw1, dw2, dw3, db):
                r[...] = jnp.zeros_like(r)

        for k, r in ((3, dw0), (2, dw1), (1, dw2)):
            r[...] += jnp.sum(d * _shift_down(u, uh, k), axis=0, keepdims=True)
        dw3[...] += jnp.sum(d * u, axis=0, keepdims=True)
        db[...] += jnp.sum(d, axis=0, keepdims=True)

    vec = pl.BlockSpec((1, C), lambda i: (0, 0))
    return pl.pallas_call(
        body, name="ssmconv_bwd", grid=(n,),
        in_specs=[main, before, dmain, dafter, pl.BlockSpec(w.shape, lambda i: (0, 0))],
        out_specs=[pl.BlockSpec((tl, C), lambda i: (i, 0))] + [vec] * 5,
        out_shape=[jax.ShapeDtypeStruct((L, C), BF16)] + [jax.ShapeDtypeStruct((1, C), F32)] * 5,
        compiler_params=_params(("arbitrary",)),
    )(P, P, dpre, dpre, w)


def _dot_nt(a, b):
    return lax.dot_general(a, b, (((1,), (1,)), ((), ())), preferred_element_type=F32)


def _dot_tn(a, b):
    return lax.dot_general(a, b, (((0,), (0,)), ((), ())), preferred_element_type=F32)


def _split3(x):
    hi = x.astype(BF16)
    r = x - hi.astype(F32)
    mid = r.astype(BF16)
    return hi, mid, (r - mid.astype(F32)).astype(BF16)


@jax.custom_vjp
def _xm01(x, m):
    return sum(jnp.dot(t, m, preferred_element_type=F32) for t in _split3(x))


def _xm01_fwd(x, m):
    return _xm01(x, m), m


def _xm01_bwd(m, g):
    return sum(_dot_nt(t, m) for t in _split3(g)), jnp.zeros_like(m)


_xm01.defvjp(_xm01_fwd, _xm01_bwd)


@jax.custom_vjp
def _m01x(m, x):
    return sum(jnp.dot(m, t, preferred_element_type=F32) for t in _split3(x))


def _m01x_fwd(m, x):
    return _m01x(m, x), m


def _m01x_bwd(m, g):
    return jnp.zeros_like(m), sum(_dot_tn(m, t) for t in _split3(g))


_m01x.defvjp(_m01x_fwd, _m01x_bwd)


def _ssd_chunk(pre, dtr, s_prev, dtb, alog):
    T = pre.shape[0]
    act = _silu(pre)
    xs, bm, cm = act[:, :SSM_INNER], act[:, SSM_INNER:SSM_INNER + 128], act[:, SSM_INNER + 128:]
    lane = lax.broadcasted_iota(jnp.int32, (1, LANES), 1)
    dt = jnp.where(lane < SSM_HEADS, _softplus(dtr + dtb), 0.0)
    a = dt * (-jnp.exp(alog))
    ri = lax.broadcasted_iota(jnp.int32, (T, T), 0)
    ci = lax.broadcasted_iota(jnp.int32, (T, T), 1)
    causal = ci <= ri
    a_cs = _m01x(causal.astype(BF16), a)
    eh = lax.broadcasted_iota(jnp.int32, (LANES, SSM_INNER), 0)
    ej = lax.broadcasted_iota(jnp.int32, (LANES, SSM_INNER), 1)
    expand = (lax.shift_right_logical(ej, 6) == eh).astype(BF16)
    dt_full = _xm01(dt, expand)
    acs_full = _xm01(a_cs, expand)
    alast_full = acs_full[T - 1:T, :]
    xdt = xs * dt_full
    a_cs_t = a_cs.T
    ys, s_new = [], []
    for g in range(2):
        in_group = lax.shift_right_logical(lane, 6) == g
        cg = jnp.where(in_group, cm, 0.0).astype(BF16)
        bg = jnp.where(in_group, bm, 0.0).astype(BF16)
        scores = _dot_nt(cg, bg)
        for pp in range(2):
            hp = 2 * g + pp
            cols = slice(hp * LANES, (hp + 1) * LANES)
            xp, acsp = xdt[:, cols], acs_full[:, cols]
            per_head = []
            for hh in range(2):
                h = 2 * hp + hh
                decay = jnp.exp(jnp.where(causal, a_cs[:, h:h + 1] - a_cs_t[h:h + 1, :], -jnp.inf))
                per_head.append(jnp.dot((scores * decay).astype(BF16), xp.astype(BF16), preferred_element_type=F32))
            y_diag = jnp.where(lane < SSM_STATE, per_head[0], per_head[1])
            sp = s_prev[hp * LANES:(hp + 1) * LANES, :]
            y_off = jnp.dot(cg, sp.astype(BF16), preferred_element_type=F32) * jnp.exp(acsp)
            ys.append(y_diag + y_off)
            to_end = jnp.exp(alast_full[:, cols] - acsp)
            s_new.append(sp * jnp.exp(alast_full[:, cols]) + _dot_tn(bg, (xp * to_end).astype(BF16)))
    return jnp.concatenate(ys, axis=1), jnp.concatenate(s_new, axis=0)


def _ssd_fwd(pre, P, dtb, alog):
    L = pre.shape[0]
    T = min(SSM_CHUNK, L)
    nc = L // T

    def body(pre_ref, dt_ref, dtb_ref, al_ref, y_ref, st_ref, s_scr):
        @pl.when(pl.program_id(0) == 0)
        def _():
            s_scr[...] = jnp.zeros_like(s_scr)

        st_ref[0] = s_scr[...]
        y, s = _ssd_chunk(pre_ref[...], _f(dt_ref[...]), s_scr[...], dtb_ref[...], al_ref[...])
        y_ref[...] = y
        s_scr[...] = s

    vec = pl.BlockSpec((1, LANES), lambda i: (0, 0))
    return pl.pallas_call(
        body, name="ssd_fwd", grid=(nc,),
        in_specs=[pl.BlockSpec((T, SSM_CONV_DIM), lambda i: (i, 0)), pl.BlockSpec((T, LANES), lambda i: (i, P_DT // LANES)),
                  vec, vec],
        out_specs=[pl.BlockSpec((T, SSM_INNER), lambda i: (i, 0)), pl.BlockSpec((1, 512, LANES), lambda i: (i, 0, 0))],
        out_shape=[jax.ShapeDtypeStruct((L, SSM_INNER), F32), jax.ShapeDtypeStruct((nc, 512, LANES), F32)],
        scratch_shapes=[pltpu.VMEM((512, LANES), F32)], compiler_params=_params(("arbitrary",)),
    )(pre, P, dtb, alog)


def _ssd_bwd(pre, P, states, dy, dxs_extra, dtb, alog):
    L = pre.shape[0]
    T = min(SSM_CHUNK, L)
    nc = L // T

    def body(pre_ref, dt_ref, st_ref, dy_ref, dx_ref, dtb_ref, al_ref, dpre_ref, ddt_ref, ddtb_ref, dal_ref, ds_scr):
        @pl.when(pl.program_id(0) == 0)
        def _():
            ds_scr[...] = jnp.zeros_like(ds_scr)
            ddtb_ref[...] = jnp.zeros_like(ddtb_ref)
            dal_ref[...] = jnp.zeros_like(dal_ref)

        _, vjp = jax.vjp(_ssd_chunk, pre_ref[...], _f(dt_ref[...]), st_ref[0], dtb_ref[...], al_ref[...])
        dpre, ddt, ds, ddtb, dal = vjp((dy_ref[...], ds_scr[...]))
        dpre_ref[:, :SSM_INNER] = dpre[:, :SSM_INNER] + dx_ref[...]
        dpre_ref[:, SSM_INNER:] = dpre[:, SSM_INNER:]
        ddt_ref[:, :LANES] = ddt.astype(ddt_ref.dtype)
        ddt_ref[:, LANES:] = jnp.zeros((T, DT_PAD - LANES), ddt_ref.dtype)
        ds_scr[...] = ds
        ddtb_ref[...] += ddtb
        dal_ref[...] += dal

    vec = pl.BlockSpec((1, LANES), lambda i: (0, 0))
    rev = lambda i: (nc - 1 - i, 0)
    return pl.pallas_call(
        body, name="ssd_bwd", grid=(nc,),
        in_specs=[pl.BlockSpec((T, SSM_CONV_DIM), rev), pl.BlockSpec((T, LANES), lambda i: (nc - 1 - i, P_DT // LANES)),
                  pl.BlockSpec((1, 512, LANES), lambda i: (nc - 1 - i, 0, 0)),
                  pl.BlockSpec((T, SSM_INNER), rev), pl.BlockSpec((T, SSM_INNER), rev), vec, vec],
        out_specs=[pl.BlockSpec((T, SSM_CONV_DIM), rev), pl.BlockSpec((T, DT_PAD), rev), vec, vec],
        out_shape=[jax.ShapeDtypeStruct((L, SSM_CONV_DIM), F32), jax.ShapeDtypeStruct((L, DT_PAD), BF16),
                   jax.ShapeDtypeStruct((1, LANES), F32), jax.ShapeDtypeStruct((1, LANES), F32)],
        scratch_shapes=[pltpu.VMEM((512, LANES), F32)], compiler_params=_params(("arbitrary",)),
    )(pre, P, states, dy, dxs_extra, dtb, alog)


def _sb_scores(qm, kb, later, strict, mask):
    z = _dot_nt(qm, kb)
    lk = jnp.minimum(-z, 0.0) - jnp.log(1.0 + jnp.exp(-jnp.abs(z)))
    if mask is not None:
        lk = jnp.where(mask, lk, 0.0)
    log_a = z + lk + jnp.dot(lk.astype(BF16), strict, preferred_element_type=F32) + later
    if mask is not None:
        log_a = jnp.where(mask, log_a, -jnp.inf)
    return z, lk, log_a


def _dot_split(x, m):
    hi = x.astype(BF16)
    lo = (x - hi.astype(F32)).astype(BF16)
    return jnp.dot(hi, m, preferred_element_type=F32) + jnp.dot(lo, m, preferred_element_type=F32)


def _sb_setup(q_ref, i, tq, tk):
    lane = lax.broadcasted_iota(jnp.int32, (1, LANES), 1)
    first = lane < SB_HEAD_DIM
    q = q_ref[...] * (SB_HEAD_DIM ** -0.5)
    qms = (jnp.where(first, q, jnp.zeros_like(q)), jnp.where(first, jnp.zeros_like(q), q))
    j0 = lax.div(i * tq, tk)
    ri = lax.broadcasted_iota(jnp.int32, (tq, tk), 0)
    ci = lax.broadcasted_iota(jnp.int32, (tq, tk), 1)
    diag_mask = (ci + (j0 * tk - i * tq)) < ri
    kr = lax.broadcasted_iota(jnp.int32, (tk, tk), 0)
    kc = lax.broadcasted_iota(jnp.int32, (tk, tk), 1)
    strict = (kr > kc).astype(BF16)
    return first, qms, j0, diag_mask, strict


def _sb_continue(c):
    return jnp.logical_and(c[0] >= 0, jnp.maximum(jnp.max(c[1][0]), jnp.max(c[1][1])) > SB_LOG_CUTOFF)


def _sb_fwd(P):
    L = P.shape[0]
    tq, tk = min(SB_TQ, L), min(SB_TK, L)
    nq = L // tq
    qb = P_B // LANES

    def body(q_ref, k_ref, v_ref, o_ref, of_ref):
        i = pl.program_id(1)
        first, qms, j0, diag_mask, strict = _sb_setup(q_ref, i, tq, tk)

        def tiles(j, laters, accs, mask=None, valid=None):
            off = pl.multiple_of(j * tk, tk)
            kb, vb = k_ref[pl.ds(off, tk), :], v_ref[pl.ds(off, tk), :]
            new_l, new_a = [], []
            for qm, later, acc in zip(qms, laters, accs):
                gate = later if valid is None else jnp.where(valid, later, -jnp.inf)
                _, lk, log_a = _sb_scores(qm, kb, gate, strict, mask)
                new_a.append(acc + jnp.dot(jnp.exp(log_a).astype(BF16), vb, preferred_element_type=F32))
                rows = jnp.sum(lk, axis=1, keepdims=True)
                new_l.append(later + (rows if valid is None else jnp.where(valid, rows, 0.0)))
            return tuple(new_l), tuple(new_a)

        def step(c):
            laters, accs = tiles(c[0], c[1], c[2])
            return c[0] - 1, laters, accs

        zero, zacc = jnp.zeros((tq, 1), F32), jnp.zeros((tq, LANES), F32)
        laters, accs = tiles(j0, (zero, zero), (zacc, zacc), mask=diag_mask)
        laters, accs = tiles(jnp.maximum(j0 - 1, 0), laters, accs, valid=j0 >= 1)
        _, _, accs = lax.while_loop(_sb_continue, step, (j0 - 2, laters, accs))
        out = jnp.where(first, accs[0], accs[1])
        o_ref[...] = out.astype(o_ref.dtype)
        of_ref[...] = out

    tile_spec = pl.BlockSpec((tq, LANES), lambda p, i: (i, p))
    return pl.pallas_call(
        body, name="sb_fwd", grid=(2, nq),
        in_specs=[pl.BlockSpec((tq, LANES), lambda p, i: (i, qb + p)),
                  pl.BlockSpec((L, LANES), lambda p, i: (0, qb + 2 + p)),
                  pl.BlockSpec((L, LANES), lambda p, i: (0, qb + 4 + p))],
        out_specs=[tile_spec, tile_spec],
        out_shape=[jax.ShapeDtypeStruct((L, 2 * LANES), BF16), jax.ShapeDtypeStruct((L, 2 * LANES), F32)],
        compiler_params=_params(("arbitrary", "arbitrary")),
    )(P, P, P)


def _sb_bwd(P, dyb, yb32):
    L = P.shape[0]
    tq, tk = min(SB_TQ, L), min(SB_TK, L)
    nq = L // tq
    qb = P_B // LANES

    def body(q_ref, k_ref, v_ref, do_ref, of_ref, dq_ref, dk_ref, dv_ref):
        i = pl.program_id(1)
        first, qms, j0, diag_mask, strict = _sb_setup(q_ref, i, tq, tk)

        @pl.when(i == 0)
        def _():
            dk_ref[...] = jnp.zeros_like(dk_ref)
            dv_ref[...] = jnp.zeros_like(dv_ref)

        do = do_ref[...]
        doms = (jnp.where(first, do, jnp.zeros_like(do)), jnp.where(first, jnp.zeros_like(do), do))
        prod = _f(do) * of_ref[...]
        totals = (jnp.sum(jnp.where(first, prod, 0.0), axis=1, keepdims=True),
                  jnp.sum(jnp.where(first, 0.0, prod), axis=1, keepdims=True))

        def tiles(j, laters, later_gs, accs, mask=None, valid=None):
            off = pl.multiple_of(j * tk, tk)
            kb, vb = k_ref[pl.ds(off, tk), :], v_ref[pl.ds(off, tk), :]
            new_l, new_g, new_a = [], [], []
            dk = dv = None
            for h in range(2):
                gate = laters[h] if valid is None else jnp.where(valid, laters[h], -jnp.inf)
                z, lk, log_a = _sb_scores(qms[h], kb, gate, strict, mask)
                att = jnp.exp(log_a).astype(BF16)
                g = _f(att) * _dot_nt(doms[h], vb)
                before = totals[h] - later_gs[h]
                if valid is not None:
                    before = jnp.where(valid, before, 0.0)
                dz = g - (before - _dot_split(g, strict)) * jnp.exp(z + lk)
                if mask is not None:
                    dz = jnp.where(mask, dz, 0.0)
                dzb = dz.astype(BF16)
                dk_h, dv_h = _dot_tn(dzb, qms[h]), _dot_tn(att, doms[h])
                dk, dv = (dk_h, dv_h) if h == 0 else (dk + dk_h, dv + dv_h)
                new_a.append(accs[h] + jnp.dot(dzb, kb, preferred_element_type=F32))
                rows = jnp.sum(lk, axis=1, keepdims=True)
                new_l.append(laters[h] + (rows if valid is None else jnp.where(valid, rows, 0.0)))
                new_g.append(later_gs[h] + jnp.sum(g, axis=1, keepdims=True))
            dk_ref[pl.ds(off, tk), :] += dk
            dv_ref[pl.ds(off, tk), :] += dv
            return tuple(new_l), tuple(new_g), tuple(new_a)

        def step(c):
            return (c[0] - 1,) + tiles(c[0], c[1], c[2], c[3])

        zero, zacc = jnp.zeros((tq, 1), F32), jnp.zeros((tq, LANES), F32)
        carry = tiles(j0, (zero, zero), (zero, zero), (zacc, zacc), mask=diag_mask)
        carry = tiles(jnp.maximum(j0 - 1, 0), *carry, valid=j0 >= 1)
        _, _, _, accs = lax.while_loop(_sb_continue, step, (j0 - 2,) + carry)
        dq_ref[...] = jnp.where(first, accs[0], accs[1]) * (SB_HEAD_DIM ** -0.5)

    full = pl.BlockSpec((L, LANES), lambda p, i: (0, p))
    tile_spec = pl.BlockSpec((tq, LANES), lambda p, i: (i, p))
    return pl.pallas_call(
        body, name="sb_bwd", grid=(2, nq),
        in_specs=[pl.BlockSpec((tq, LANES), lambda p, i: (i, qb + p)),
                  pl.BlockSpec((L, LANES), lambda p, i: (0, qb + 2 + p)),
                  pl.BlockSpec((L, LANES), lambda p, i: (0, qb + 4 + p)), tile_spec, tile_spec],
        out_specs=[tile_spec, full, full],
        out_shape=[jax.ShapeDtypeStruct((L, 2 * LANES), F32)] * 3,
        compiler_params=_params(("arbitrary", "arbitrary")),
    )(P, P, P, dyb, yb32)


MOD_SHARD = N_MOD * D_MODEL // N_CHIPS


def _mod_fwd(c_all, mod_w, mod_b_sh):
    tn = 512

    def body(c_ref, w_ref, b_ref, o_ref):
        o_ref[0] = jnp.dot(_silu(c_ref[...]), w_ref[0], precision=HIGHEST, preferred_element_type=F32) + b_ref[0]

    return pl.pallas_call(
        body, name="mod_fwd", grid=(DEPTH, MOD_SHARD // tn),
        in_specs=[pl.BlockSpec((N_DEV, D_MODEL), lambda l, j: (0, 0)),
                  pl.BlockSpec((1, D_MODEL, tn), lambda l, j: (l, 0, j)),
                  pl.BlockSpec((1, 1, tn), lambda l, j: (l, 0, j))],
        out_specs=pl.BlockSpec((1, N_DEV, tn), lambda l, j: (l, 0, j)),
        out_shape=jax.ShapeDtypeStruct((DEPTH, N_DEV, MOD_SHARD), F32),
        compiler_params=_params(("arbitrary", "arbitrary")),
    )(c_all, mod_w, mod_b_sh)


def _mod_bwd(c_all, dmod_sh):
    tn = 512

    def body(c_ref, d_ref, o_ref):
        o_ref[0] = lax.dot_general(_silu(c_ref[...]), d_ref[0], (((0,), (0,)), ((), ())), precision=HIGHEST,
                                   preferred_element_type=F32)

    return pl.pallas_call(
        body, name="mod_bwd", grid=(DEPTH, MOD_SHARD // tn),
        in_specs=[pl.BlockSpec((N_DEV, D_MODEL), lambda l, j: (0, 0)),
                  pl.BlockSpec((1, N_DEV, tn), lambda l, j: (l, 0, j))],
        out_specs=pl.BlockSpec((1, D_MODEL, tn), lambda l, j: (l, 0, j)),
        out_shape=jax.ShapeDtypeStruct((DEPTH, D_MODEL, MOD_SHARD), F32),
        compiler_params=_params(("arbitrary", "arbitrary")),
    )(c_all, dmod_sh)


def _row_tile(rows, cap):
    if rows <= cap:
        return rows
    best = None
    for t in range(8, cap + 1, 8):
        if rows % t == 0:
            best = t
    assert best is not None, (rows, cap)
    return best


def _adamw(name, w, gs, m, v, tr=256):
    R, W = w.shape
    tr = _row_tile(R, tr)
    ng = len(gs)

    def body(*refs):
        w_ref, g_refs, (m_ref, v_ref) = refs[0], refs[1:1 + ng], refs[1 + ng:3 + ng]
        g_out, d_out, m_out, v_out = refs[3 + ng:]
        g = g_refs[0][...]
        for r in g_refs[1:]:
            g = g + r[...]
        mm = ADAM_B1 * m_ref[...] + (1.0 - ADAM_B1) * g
        vv = ADAM_B2 * v_ref[...] + (1.0 - ADAM_B2) * (g * g)
        m_hat = mm / (1.0 - ADAM_B1 ** ADAM_STEP)
        v_hat = vv / (1.0 - ADAM_B2 ** ADAM_STEP)
        g_out[...] = g
        d_out[...] = -ADAM_LR * (m_hat / (jnp.sqrt(v_hat) + ADAM_EPS) + ADAM_WD * w_ref[...])
        m_out[...] = mm
        v_out[...] = vv

    spec = pl.BlockSpec((tr, W), lambda i: (i, 0))
    return pl.pallas_call(
        body, name=name, grid=(R // tr,), in_specs=[spec] * (3 + ng), out_specs=[spec] * 4,
        out_shape=[jax.ShapeDtypeStruct((R, W), F32)] * 4, compiler_params=_params(("arbitrary",)),
    )(w, *gs, m, v)


def _sum_slots(name, a, tr=256):
    n, R, W = a.shape
    tr = _row_tile(R, tr)

    def body(a_ref, o_ref):
        acc = _f(a_ref[0])
        for j in range(1, n):
            acc = acc + _f(a_ref[j])
        o_ref[...] = acc

    return pl.pallas_call(
        body, name=name, grid=(R // tr,), in_specs=[pl.BlockSpec((n, tr, W), lambda i: (0, i, 0))],
        out_specs=pl.BlockSpec((tr, W), lambda i: (i, 0)), out_shape=jax.ShapeDtypeStruct((R, W), F32),
        compiler_params=_params(("arbitrary",)),
    )(a)


def _here():
    return lax.axis_index("x"), lax.axis_index("y"), lax.axis_index("c")


def _flip(v, d):
    return 1 - v if d else v


def _allgather_small(name, buf):
    R = buf.shape[0]
    rel = [(dx, dy, dc) for dx in (0, 1) for dy in (0, 1) for dc in (0, 1)][1:]

    def body(x_ref, o_ref, send, recv, lsem):
        x, y, c = _here()
        me = 4 * x + 2 * y + c
        mine = pltpu.make_async_copy(x_ref, o_ref.at[me], lsem)
        mine.start()

        def copy(k, slot):
            dx, dy, dc = rel[k]
            return pltpu.make_async_remote_copy(
                src_ref=x_ref, dst_ref=o_ref.at[slot], send_sem=send.at[k], recv_sem=recv.at[k],
                device_id=(_flip(x, dx), _flip(y, dy), _flip(c, dc)), device_id_type=MESH_ID)

        sent = [copy(k, me) for k in range(len(rel))]
        for cp in sent:
            cp.start()
        for k, (dx, dy, dc) in enumerate(rel):
            copy(k, 4 * _flip(x, dx) + 2 * _flip(y, dy) + _flip(c, dc)).wait_recv()
        for cp in sent:
            cp.wait_send()
        mine.wait()

    return pl.pallas_call(
        body, name=name, out_shape=jax.ShapeDtypeStruct((N_DEV, R, LANES), F32),
        in_specs=[pl.BlockSpec(memory_space=pltpu.VMEM)], out_specs=pl.BlockSpec(memory_space=pltpu.VMEM),
        scratch_shapes=[pltpu.SemaphoreType.DMA((7,)), pltpu.SemaphoreType.DMA((7,)), pltpu.SemaphoreType.DMA],
    )(buf)


CHIP_REL = [(1, 0), (0, 1), (1, 1)]


def _chip_scatter(name, arrays):
    n = len(arrays)

    def body(*refs):
        ins, outs = refs[:n], refs[n:2 * n]
        send, recv, lsem = refs[2 * n:]
        x, y, c = _here()
        s = 2 * x + y

        def copy(w, k, slot):
            dx, dy = CHIP_REL[k]
            px, py = _flip(x, dx), _flip(y, dy)
            return pltpu.make_async_remote_copy(
                src_ref=ins[w].at[2 * px + py], dst_ref=outs[w].at[slot], send_sem=send.at[3 * w + k],
                recv_sem=recv.at[3 * w + k], device_id=(px, py, c), device_id_type=MESH_ID)

        local = [pltpu.make_async_copy(ins[w].at[s], outs[w].at[s], lsem.at[w]) for w in range(n)]
        for cp in local:
            cp.start()
        sent = [copy(w, k, s) for w in range(n) for k in range(3)]
        for cp in sent:
            cp.start()
        for w in range(n):
            for k, (dx, dy) in enumerate(CHIP_REL):
                copy(w, k, 2 * _flip(x, dx) + _flip(y, dy)).wait_recv()
        for cp in sent:
            cp.wait_send()
        for cp in local:
            cp.wait()

    any_spec = pl.BlockSpec(memory_space=pl.ANY)
    return pl.pallas_call(
        body, name=name, out_shape=[jax.ShapeDtypeStruct(a.shape, a.dtype) for a in arrays],
        in_specs=[any_spec] * n, out_specs=[any_spec] * n,
        scratch_shapes=[pltpu.SemaphoreType.DMA((3 * n,)), pltpu.SemaphoreType.DMA((3 * n,)), pltpu.SemaphoreType.DMA((n,))],
    )(*arrays)


def _gather_weights(shards):
    n = len(shards)

    def body(*refs):
        ins, outs = refs[:n], refs[n:2 * n]
        send, recv, fsend, frecv, lsem = refs[2 * n:]
        x, y, c = _here()
        s = 2 * x + y

        def chip_of(k):
            dx, dy = CHIP_REL[k]
            return _flip(x, dx), _flip(y, dy)

        def over_ici(w, k, slot):
            px, py = chip_of(k)
            return pltpu.make_async_remote_copy(
                src_ref=ins[w].at[c], dst_ref=outs[w].at[slot].at[c], send_sem=send.at[3 * w + k], recv_sem=recv.at[3 * w + k],
                device_id=(px, py, c), device_id_type=MESH_ID)

        def to_sibling(w, k, layer):
            px, py = chip_of(k)
            part = outs[w].at[2 * px + py].at[layer]
            return pltpu.make_async_remote_copy(
                src_ref=part, dst_ref=part, send_sem=fsend.at[3 * w + k], recv_sem=frecv.at[3 * w + k],
                device_id=(x, y, 1 - c), device_id_type=MESH_ID)

        local = [pltpu.make_async_copy(ins[w], outs[w].at[s], lsem.at[w]) for w in range(n)]
        for cp in local:
            cp.start()
        sent = [over_ici(w, k, s) for w in range(n) for k in range(3)]
        for cp in sent:
            cp.start()
        passed = []
        for w in range(n):
            for k in range(3):
                px, py = chip_of(k)
                over_ici(w, k, 2 * px + py).wait_recv()
                passed.append(to_sibling(w, k, c))
                passed[-1].start()
        for w in range(n):
            for k in range(3):
                to_sibling(w, k, 1 - c).wait_recv()
        for cp in sent + passed:
            cp.wait_send()
        for cp in local:
            cp.wait()

    any_spec = pl.BlockSpec(memory_space=pl.ANY)
    return pl.pallas_call(
        body, name="gather_weights", out_shape=[jax.ShapeDtypeStruct((N_CHIPS,) + a.shape, a.dtype) for a in shards],
        in_specs=[any_spec] * n, out_specs=[any_spec] * n,
        scratch_shapes=[pltpu.SemaphoreType.DMA((3 * n,))] * 4 + [pltpu.SemaphoreType.DMA((n,))],
    )(*shards)


def _sibling_exchange(name, arrays):
    n = len(arrays)

    def body(*refs):
        ins, outs = refs[:n], refs[n:2 * n]
        send, recv = refs[2 * n:]
        x, y, c = _here()
        cps = [pltpu.make_async_remote_copy(src_ref=ins[w], dst_ref=outs[w], send_sem=send.at[w], recv_sem=recv.at[w],
                                            device_id=(x, y, 1 - c), device_id_type=MESH_ID) for w in range(n)]
        for cp in cps:
            cp.start()
        for cp in cps:
            cp.wait()

    any_spec = pl.BlockSpec(memory_space=pl.ANY)
    return pl.pallas_call(
        body, name=name, out_shape=[jax.ShapeDtypeStruct(a.shape, a.dtype) for a in arrays],
        in_specs=[any_spec] * n, out_specs=[any_spec] * n,
        scratch_shapes=[pltpu.SemaphoreType.DMA((n,)), pltpu.SemaphoreType.DMA((n,))],
    )(*arrays)


def _pack(arrs):
    flat = jnp.concatenate([a.reshape(-1).astype(F32) for a in arrs])
    n = flat.shape[0]
    rows = -(-n // (8 * LANES)) * 8
    return jnp.pad(flat, (0, rows * LANES - n)).reshape(rows, LANES)


def _unpack(buf, shapes):
    lead = buf.shape[:-2]
    flat = buf.reshape(lead + (-1,))
    out, off = [], 0
    for s in shapes:
        n = 1
        for d in s:
            n *= d
        out.append(flat[..., off:off + n].reshape(lead + tuple(s)))
        off += n
    return out


def _pad_w_in(w):
    return jnp.concatenate([w[:, :2048], w[:, 2816:2824], jnp.zeros((w.shape[0], P_XBC - P_DT - 8), w.dtype),
                            w[:, 2048:2816], w[:, 2824:]], axis=1)


def _unpad_w_in(g):
    return jnp.concatenate([g[:, :P_DT], g[:, P_XBC:P_G], g[:, P_DT:P_DT + 8], g[:, P_G:]], axis=1)


FFN_HALF = FFN_HIDDEN // 2


def _ffn_in_cols(w):
    h = FFN_HALF
    return jnp.concatenate([w[:, :h], w[:, 2 * h:3 * h], w[:, h:2 * h], w[:, 3 * h:]], axis=1)


def _row(v):
    return v.reshape(1, -1)


BIG = (("w_in", 2), ("w_sc_out", 2), ("w_sb_out", 2), ("w_ssm_out", 2), ("w_o", 1), ("w_ffn_in", 2), ("w_ffn_out", 1))
SMALL = ("mod_b", "g_pre_mix", "g_post_mix", "g_pre_ffn", "g_post_ffn", "sc_conv_w", "ssm_conv_w", "ssm_conv_b",
         "ssm_dt_bias", "ssm_a_log", "ssm_d", "ssm_norm_w")
WEIGHT_ORDER = ("mod_w", "mod_b", "g_pre_mix", "g_post_mix", "g_pre_ffn", "g_post_ffn", "w_in", "sc_conv_w",
                "ssm_conv_w", "ssm_conv_b", "ssm_dt_bias", "ssm_a_log", "ssm_d", "ssm_norm_w", "w_sc_out", "w_sb_out",
                "w_ssm_out", "w_o", "w_ffn_in", "w_ffn_out")


def _layer_fwd(l, x_in, h, W, V):
    S = {"x_in": x_in, "h": h}
    P = _mm(f"in_proj{l}", h, W["w_in"], "nn", BF16, tn_cap=1024)
    S["P"] = P
    S["ya"] = _shortconv_fwd(P, V["sc_w"])
    S["yb"], S["yb32"] = _sb_fwd(P)
    S["pre"] = _ssmconv_fwd(P, V["ssm_w"], V["ssm_b"])
    S["y_ssd"], S["states"] = _ssd_fwd(S["pre"], P, V["dtb"], V["alog"])
    S["yc"] = _rowwise(f"ssm_post{l}", lambda y, px, z, d, nw: _ssm_post(y, px, _f(z), d, nw),
                       [(S["y_ssd"], SSM_INNER, 0), (S["pre"], SSM_INNER, 0), (P, SSM_INNER, P_Z // SSM_INNER)],
                       [V["d_full"], V["norm_w"]], [(SSM_INNER, BF16)])[0]
    S["merged"] = _merge_fwd(f"merge{l}", P, [S["ya"], S["yb"], S["yc"]],
                             [W["w_sc_out"], W["w_sb_out"], W["w_ssm_out"]])
    S["mix"] = _mm(f"w_o{l}", S["merged"], W["w_o"], "nn", BF16)
    S["x1"], S["h2"] = _mid_fwd(f"mid_mix{l}", x_in, S["mix"], V["mid_mix"])
    S["GU"], S["act"] = _mm_epi(f"ffn_in{l}", S["h2"], W["w_ffn_in"], "nn", 2 * FFN_HALF, [],
                                lambda p: (p, _swiglu(p[:, :FFN_HALF], p[:, FFN_HALF:])),
                                [(2 * FFN_HALF, BF16), (FFN_HALF, BF16)])
    S["f"] = _mm(f"ffn_out{l}", S["act"], W["w_ffn_out"], "nn", BF16)
    return S


BRANCH_WIDTHS = (SC_WIDTH, 256, SSM_INNER)


def _branch_specs(tm):
    gb = P_G // D_MODEL
    gates = [pl.BlockSpec((tm, D_MODEL), functools.partial(lambda i, cb: (i, cb), cb=gb + k)) for k in range(3)]
    ys = [pl.BlockSpec((tm, w), lambda i: (i, 0)) for w in BRANCH_WIDTHS]
    ws = [pl.BlockSpec((w, D_MODEL), lambda i: (0, 0)) for w in BRANCH_WIDTHS]
    return gates, ys, ws


def _merge_fwd(name, P, ys, ws, tm=512):
    L = P.shape[0]
    tm = min(tm, L)
    gates, y_specs, w_specs = _branch_specs(tm)

    def body(ga, gb, gc, ya, yb, yc, wa, wb, wc, o_ref):
        acc = None
        for g_ref, y_ref, w_ref in ((ga, ya, wa), (gb, yb, wb), (gc, yc, wc)):
            t = jax.nn.sigmoid(_f(g_ref[...])) * jnp.dot(y_ref[...], w_ref[...], preferred_element_type=F32)
            acc = t if acc is None else acc + t
        o_ref[...] = acc.astype(o_ref.dtype)

    return pl.pallas_call(
        body, name=name, grid=(L // tm,), in_specs=gates + y_specs + w_specs,
        out_specs=pl.BlockSpec((tm, D_MODEL), lambda i: (i, 0)), out_shape=jax.ShapeDtypeStruct((L, D_MODEL), BF16),
        compiler_params=_params(("arbitrary",)),
    )(P, P, P, *ys, *ws)


def _merge_bwd(name, P, ys, ws, dmerged, tm=512):
    L = P.shape[0]
    tm = min(tm, L)
    gates, y_specs, w_specs = _branch_specs(tm)

    def body(ga, gb, gc, ya, yb, yc, wa, wb, wc, dm_ref, dg_ref, dya, dyb, dyc, gwa, gwb, gwc):
        @pl.when(pl.program_id(0) == 0)
        def _():
            for r in (gwa, gwb, gwc):
                r[...] = jnp.zeros_like(r)

        dm = _f(dm_ref[...])
        for k, (g_ref, y_ref, w_ref, dy_ref, gw_ref) in enumerate(
                ((ga, ya, wa, dya, gwa), (gb, yb, wb, dyb, gwb), (gc, yc, wc, dyc, gwc))):
            y, w = y_ref[...], w_ref[...]
            s = jax.nn.sigmoid(_f(g_ref[...]))
            proj = jnp.dot(y, w, preferred_element_type=F32)
            d_proj = (dm * s).astype(BF16)
            dg_ref[:, k * D_MODEL:(k + 1) * D_MODEL] = (dm * proj * s * (1.0 - s)).astype(dg_ref.dtype)
            dy_ref[...] = _dot_nt(d_proj, w).astype(dy_ref.dtype)
            gw_ref[...] += _dot_tn(y, d_proj)

    gate_cols = pl.BlockSpec((tm, P_WIDTH - P_G), lambda i: (i, P_G // (P_WIDTH - P_G)))
    return pl.pallas_call(
        body, name=name, grid=(L // tm,),
        in_specs=gates + y_specs + w_specs + [pl.BlockSpec((tm, D_MODEL), lambda i: (i, 0))],
        out_specs=[gate_cols] + y_specs + w_specs,
        out_shape=[jax.ShapeDtypeStruct((L, P_WIDTH), BF16)] + [jax.ShapeDtypeStruct((L, w), BF16) for w in BRANCH_WIDTHS]
        + [jax.ShapeDtypeStruct((w, D_MODEL), F32) for w in BRANCH_WIDTHS],
        compiler_params=_params(("arbitrary",)),
    )(P, P, P, *ys, *ws, dmerged)


def _assemble_dp(name, dP, parts, tl=256):
    L = dP.shape[0]
    tl = min(tl, L)
    n = len(parts)

    def body(*refs):
        o_ref = refs[n + 1]
        o_ref[...] = jnp.concatenate([r[...].astype(o_ref.dtype) for r in refs[:n]], axis=1)

    return pl.pallas_call(
        body, name=name, grid=(L // tl,),
        in_specs=[pl.BlockSpec((tl, a.shape[1]), lambda i: (i, 0)) for a in parts] + [pl.BlockSpec(memory_space=pl.ANY)],
        out_specs=pl.BlockSpec((tl, P_G), lambda i: (i, 0)), out_shape=jax.ShapeDtypeStruct(dP.shape, dP.dtype),
        input_output_aliases={n: 0}, compiler_params=_params(("arbitrary",)),
    )(*parts, dP)


def _layer_bwd(l, S, W, V, dx1, df):
    G = {}
    P = S["P"]
    G["w_ffn_out"] = _mm(f"gw_ffn_out{l}", S["act"], df, "tn", F32)

    def swiglu_bwd(d_act, gu):
        _, vjp = jax.vjp(_swiglu, _f(gu[:, :FFN_HALF]), _f(gu[:, FFN_HALF:]))
        return (jnp.concatenate(vjp(d_act), axis=1),)

    dGU = _mm_epi(f"d_gu{l}", df, W["w_ffn_out"], "nt", FFN_HALF, [(S["GU"], 2 * FFN_HALF)], swiglu_bwd,
                  [(2 * FFN_HALF, BF16)])[0]
    dh2 = _mm(f"d_h2{l}", dGU, W["w_ffn_in"], "nt", BF16)
    G["w_ffn_in"] = _ffn_in_cols(_mm(f"gw_ffn_in{l}", S["h2"], dGU, "tn", F32))
    dx, dmix, G["gate1"], G["g_post_mix"], G["g_pre_ffn"], G["scale2"], G["shift2"] = _mid_bwd(
        f"mid_mix_bwd{l}", S["x_in"], S["mix"], dx1, dh2, V["mid_mix"])
    dmerged = _mm(f"d_merged{l}", dmix, W["w_o"], "nt", BF16)
    G["w_o"] = _mm(f"gw_o{l}", S["merged"], dmix, "tn", F32)

    dP, dya, dyb, dyc, G["w_sc_out"], G["w_sb_out"], G["w_ssm_out"] = _merge_bwd(
        f"merge_bwd{l}", P, [S["ya"], S["yb"], S["yc"]], [W["w_sc_out"], W["w_sb_out"], W["w_ssm_out"]], dmerged)

    def post_bwd(y, px, z, d, dfull, nw):
        _, vjp = jax.vjp(_ssm_post, y, px, _f(z), dfull, nw)
        return vjp(_f(d))

    dy_ssd, dxs, dz, G["d_full"], G["ssm_norm_w"] = _rowwise(
        f"ssm_post_bwd{l}", post_bwd,
        [(S["y_ssd"], SSM_INNER, 0), (S["pre"], SSM_INNER, 0), (P, SSM_INNER, P_Z // SSM_INNER), (dyc, SSM_INNER, 0)],
        [V["d_full"], V["norm_w"]], [(SSM_INNER, F32), (SSM_INNER, F32), (SSM_INNER, BF16)], [(1, SSM_INNER)] * 2)
    dpre, ddt, G["dtb"], G["alog"] = _ssd_bwd(S["pre"], P, S["states"], dy_ssd, dxs, V["dtb"], V["alog"])
    dxbc, w0, w1, w2, w3, G["ssm_conv_b"] = _ssmconv_bwd(P, dpre, V["ssm_w"])
    G["ssm_conv_w"] = jnp.concatenate([w0, w1, w2, w3], axis=0)
    dq, dk, dv = _sb_bwd(P, dyb, S["yb32"])
    dA, s0, s1, s2 = _shortconv_bwd(P, dya, V["sc_w"])
    G["sc_conv_w"] = jnp.concatenate([s0, s1, s2], axis=0)
    dP = _assemble_dp(f"assemble_dp{l}", dP, [dA, dq, dk, dv, dz, ddt, dxbc])
    dh = _mm(f"d_h{l}", dP, W["w_in"], "nt", BF16, tk_cap=2048)
    G["w_in"] = _mm(f"gw_in{l}", S["h"], dP, "tn", F32, tn_cap=1024)
    return dx, dh, G


def kernel(x, c, mod_w, mod_b, g_pre_mix, g_post_mix, g_pre_ffn, g_post_ffn, w_in, sc_conv_w, ssm_conv_w, ssm_conv_b, ssm_dt_bias, ssm_a_log, ssm_d, ssm_norm_w, w_sc_out, w_sb_out, w_ssm_out, w_o, w_ffn_in, w_ffn_out, loss_target, m_mod_w, m_mod_b, m_g_pre_mix, m_g_post_mix, m_g_pre_ffn, m_g_post_ffn, m_w_in, m_sc_conv_w, m_ssm_conv_w, m_ssm_conv_b, m_ssm_dt_bias, m_ssm_a_log, m_ssm_d, m_ssm_norm_w, m_w_sc_out, m_w_sb_out, m_w_ssm_out, m_w_o, m_w_ffn_in, m_w_ffn_out, v_mod_w, v_mod_b, v_g_pre_mix, v_g_post_mix, v_g_pre_ffn, v_g_post_ffn, v_w_in, v_sc_conv_w, v_ssm_conv_w, v_ssm_conv_b, v_ssm_dt_bias, v_ssm_a_log, v_ssm_d, v_ssm_norm_w, v_w_sc_out, v_w_sb_out, v_w_ssm_out, v_w_o, v_w_ffn_in, v_w_ffn_out):
    wts = dict(mod_w=mod_w, mod_b=mod_b, g_pre_mix=g_pre_mix, g_post_mix=g_post_mix, g_pre_ffn=g_pre_ffn,
               g_post_ffn=g_post_ffn, w_in=w_in, sc_conv_w=sc_conv_w, ssm_conv_w=ssm_conv_w, ssm_conv_b=ssm_conv_b,
               ssm_dt_bias=ssm_dt_bias, ssm_a_log=ssm_a_log, ssm_d=ssm_d, ssm_norm_w=ssm_norm_w, w_sc_out=w_sc_out,
               w_sb_out=w_sb_out, w_ssm_out=w_ssm_out, w_o=w_o, w_ffn_in=w_ffn_in, w_ffn_out=w_ffn_out)
    mom = dict(mod_w=m_mod_w, mod_b=m_mod_b, g_pre_mix=m_g_pre_mix, g_post_mix=m_g_post_mix, g_pre_ffn=m_g_pre_ffn,
               g_post_ffn=m_g_post_ffn, w_in=m_w_in, sc_conv_w=m_sc_conv_w, ssm_conv_w=m_ssm_conv_w,
               ssm_conv_b=m_ssm_conv_b, ssm_dt_bias=m_ssm_dt_bias, ssm_a_log=m_ssm_a_log, ssm_d=m_ssm_d,
               ssm_norm_w=m_ssm_norm_w, w_sc_out=m_w_sc_out, w_sb_out=m_w_sb_out, w_ssm_out=m_w_ssm_out, w_o=m_w_o,
               w_ffn_in=m_w_ffn_in, w_ffn_out=m_w_ffn_out)
    var = dict(mod_w=v_mod_w, mod_b=v_mod_b, g_pre_mix=v_g_pre_mix, g_post_mix=v_g_post_mix, g_pre_ffn=v_g_pre_ffn,
               g_post_ffn=v_g_post_ffn, w_in=v_w_in, sc_conv_w=v_sc_conv_w, ssm_conv_w=v_ssm_conv_w,
               ssm_conv_b=v_ssm_conv_b, ssm_dt_bias=v_ssm_dt_bias, ssm_a_log=v_ssm_a_log, ssm_d=v_ssm_d,
               ssm_norm_w=v_ssm_norm_w, w_sc_out=v_w_sc_out, w_sb_out=v_w_sb_out, w_ssm_out=v_w_ssm_out, w_o=v_w_o,
               w_ffn_in=v_w_ffn_in, w_ffn_out=v_w_ffn_out)
    xi, yi, ci = _here()
    chip = 2 * xi + yi
    me = 4 * xi + 2 * yi + ci
    x0, target = x[0], loss_target[0]

    first_shapes = [(D_MODEL,), sc_conv_w.shape, ssm_conv_w.shape]
    g0 = _allgather_small("gather_cond", _pack([c, sc_conv_w, ssm_conv_w]))
    c_rows, sc_sh, ssm_sh = _unpack(g0, first_shapes)
    c_all = c_rows
    sc_w = jnp.concatenate([sc_sh[2 * j] for j in range(N_CHIPS)], axis=-1)
    ssm_w = jnp.concatenate([ssm_sh[2 * j] for j in range(N_CHIPS)], axis=-1)

    mod_b_sh = lax.dynamic_slice_in_dim(mod_b, chip * MOD_SHARD, MOD_SHARD, axis=1).reshape(DEPTH, 1, MOD_SHARD)
    modpart = _mod_fwd(c_all, mod_w, mod_b_sh)
    g1 = _allgather_small("gather_mod", modpart.reshape(-1, LANES)).reshape(N_DEV, DEPTH, N_DEV, MOD_SHARD)
    mod = jnp.concatenate([lax.dynamic_index_in_dim(g1[2 * j], me, axis=1, keepdims=False) for j in range(N_CHIPS)],
                          axis=-1)

    gathered = _gather_weights([wts[n].astype(BF16) for n, _ in BIG])
    full = {n: jnp.concatenate([g[j] for j in range(N_CHIPS)], axis=ax) for (n, ax), g in zip(BIG, gathered)}

    Ws, Vs = [], []
    for l in range(DEPTH):
        W = {n: full[n][l] for n, _ in BIG}
        W["w_in"] = _pad_w_in(W["w_in"])
        W["w_ffn_in"] = _ffn_in_cols(W["w_ffn_in"])
        Ws.append(W)
        sh1, sc1, gt1, sh2, sc2, gt2 = [_row(v) for v in jnp.split(mod[l], N_MOD)]
        Vs.append(dict(
            shift1=sh1, scale1=sc1, g_pre_mix=_row(g_pre_mix[l]),
            mid_mix=[gt1, _row(g_post_mix[l]), _row(g_pre_ffn[l]), sc2, sh2],
            gate2=gt2, g_post_ffn=_row(g_post_ffn[l]),
            sc_w=sc_w[l], ssm_w=ssm_w[l], ssm_b=_row(ssm_conv_b[l]),
            dtb=_row(jnp.pad(ssm_dt_bias[l], (0, LANES - SSM_HEADS))), alog=_row(jnp.pad(ssm_a_log[l], (0, LANES - SSM_HEADS))),
            d_full=_row(jnp.repeat(ssm_d[l], SSM_INNER // SSM_HEADS)), norm_w=_row(ssm_norm_w[l])))

    def mid_ffn_vecs(l):
        return [Vs[l]["gate2"], Vs[l]["g_post_ffn"], Vs[l + 1]["g_pre_mix"], Vs[l + 1]["scale1"], Vs[l + 1]["shift1"]]

    saved = []
    x_in = x0
    h = _first_fwd(x0, [Vs[0]["g_pre_mix"], Vs[0]["scale1"], Vs[0]["shift1"]])
    for l in range(DEPTH):
        S = _layer_fwd(l, x_in, h, Ws[l], Vs[l])
        saved.append(S)
        if l + 1 < DEPTH:
            x_in, h = _mid_fwd(f"mid_ffn{l}", S["x1"], S["f"], mid_ffn_vecs(l))

    GL = [None] * DEPTH
    S = saved[-1]
    dx1, df, g_gate2, g_gpf, loss_cols = _last_bwd(S["x1"], S["f"], target, [Vs[-1]["gate2"], Vs[-1]["g_post_ffn"]])
    for l in reversed(range(DEPTH)):
        dx, dh, G = _layer_bwd(l, saved[l], Ws[l], Vs[l], dx1, df)
        G["gate2"], G["g_post_ffn"] = g_gate2, g_gpf
        GL[l] = G
        if l > 0:
            Sp = saved[l - 1]
            dx1, df, g_gate2, g_gpf, G["g_pre_mix"], G["scale1"], G["shift1"] = _mid_bwd(
                f"mid_ffn_bwd{l - 1}", Sp["x1"], Sp["f"], dx, dh, mid_ffn_vecs(l - 1))
        else:
            grad_x, G["g_pre_mix"], G["scale1"], G["shift1"] = _first_bwd(
                x0, dx, dh, [Vs[0]["g_pre_mix"], Vs[0]["scale1"], Vs[0]["shift1"]])
    loss = lax.psum(jnp.sum(loss_cols), ("x", "y", "c"))

    def both(key, shape=None):
        a = jnp.stack([GL[l][key] for l in range(DEPTH)])
        return a if shape is None else a.reshape(shape)

    dmod = jnp.concatenate([both(k, (DEPTH, D_MODEL)) for k in ("shift1", "scale1", "gate1", "shift2", "scale2", "gate2")],
                           axis=1)
    part_small = dict(
        mod_b=dmod, g_pre_mix=both("g_pre_mix", (DEPTH, D_MODEL)), g_post_mix=both("g_post_mix", (DEPTH, D_MODEL)),
        g_pre_ffn=both("g_pre_ffn", (DEPTH, D_MODEL)), g_post_ffn=both("g_post_ffn", (DEPTH, D_MODEL)),
        sc_conv_w=both("sc_conv_w"), ssm_conv_w=both("ssm_conv_w"), ssm_conv_b=both("ssm_conv_b", (DEPTH, SSM_CONV_DIM)),
        ssm_dt_bias=both("dtb", (DEPTH, LANES))[:, :SSM_HEADS], ssm_a_log=both("alog", (DEPTH, LANES))[:, :SSM_HEADS],
        ssm_d=both("d_full", (DEPTH, SSM_HEADS, SSM_INNER // SSM_HEADS)).sum(-1),
        ssm_norm_w=both("ssm_norm_w", (DEPTH, SSM_INNER)))
    small_shapes = [part_small[n].shape for n in SMALL]
    g2 = _allgather_small("gather_small_grads", _pack([part_small[n] for n in SMALL]))
    tot = dict(zip(SMALL, _unpack(_sum_slots("sum_small_grads", g2), small_shapes)))
    dmod_all = _unpack(g2, small_shapes)[0]
    dmod_sh = jnp.swapaxes(lax.dynamic_slice_in_dim(dmod_all, chip * MOD_SHARD, MOD_SHARD, axis=2), 0, 1)
    grads = {"mod_w": _mod_bwd(c_all, dmod_sh)}
    for n in SMALL:
        grads[n] = tot[n]
    grads["sc_conv_w"] = lax.dynamic_slice_in_dim(tot["sc_conv_w"], chip * 64, 64, axis=2)
    grads["ssm_conv_w"] = lax.dynamic_slice_in_dim(tot["ssm_conv_w"], chip * 192, 192, axis=2)

    pieces = []
    for n, ax in BIG:
        g = jnp.stack([_unpad_w_in(GL[l][n]) if n == "w_in" else GL[l][n] for l in range(DEPTH)])
        pieces.append(jnp.stack(jnp.split(g, N_CHIPS, axis=ax)).astype(BF16))
    landed = _chip_scatter("scatter_grads", pieces)
    mine = [_sum_slots(f"sum_{n}", a.reshape(N_CHIPS, -1, a.shape[-1])) for (n, _), a in zip(BIG, landed)]
    theirs = _sibling_exchange("swap_core_sums", mine)

    out = {}

    def update(name, w2, gs, m2, v2, shape):
        g, d, nm, nv = _adamw(f"adamw_{name}", w2, gs, m2, v2)
        out[name] = tuple(a.reshape(shape) for a in (g, d, nm, nv))

    for (n, _), a, b in zip(BIG, mine, theirs):
        shp = wts[n].shape
        two = (-1, shp[-1])
        update(n, wts[n].reshape(two), [a, b], mom[n].reshape(two), var[n].reshape(two), shp)
    two = (-1, MOD_SHARD)
    update("mod_w", mod_w.reshape(two), [grads["mod_w"].reshape(two)], m_mod_w.reshape(two), v_mod_w.reshape(two), mod_w.shape)
    shapes = [wts[n].shape for n in SMALL]
    res = _adamw("adamw_small", _pack([wts[n] for n in SMALL]), [_pack([grads[n] for n in SMALL])],
                 _pack([mom[n] for n in SMALL]), _pack([var[n] for n in SMALL]))
    for n, g, d, nm, nv in zip(SMALL, *[_unpack(r, shapes) for r in res]):
        out[n] = (g, d, nm, nv)

    result = [loss, grad_x[None]]
    for k in range(4):
        result += [out[n][k] for n in WEIGHT_ORDER]
    return tuple(result)
```

```python
import functools

import jax
import jax.numpy as jnp
from jax import lax
from jax.experimental import pallas as pl
from jax.experimental.pallas import tpu as pltpu

F32 = jnp.float32
BF16 = jnp.bfloat16
HIGHEST = lax.Precision.HIGHEST
MESH_ID = pl.DeviceIdType.MESH

D_MODEL = 1024
DEPTH = 2
SC_WIDTH = 256
SC_KERNEL = 3
SB_HEAD_DIM = 64
SSM_INNER = 512
SSM_HEADS = 8
SSM_STATE = 64
SSM_CONV = 4
SSM_CHUNK = 256
SSM_CONV_DIM = 768
FFN_HIDDEN = 2816
NORM_EPS = 1e-6
N_MOD = 6
N_CHIPS = 4
N_DEV = 8

ADAM_LR = 0.001
ADAM_B1 = 0.9
ADAM_B2 = 0.999
ADAM_EPS = 1e-08
ADAM_WD = 0.01
ADAM_STEP = 10

P_WIDTH = 6144
P_A, P_B, P_Z, P_DT, P_XBC, P_G = 0, 768, 1536, 2048, 2304, 3072
DT_PAD = 256

VMEM_LIMIT_BYTES = 56 * 1024 * 1024
LANES = 128

SB_LOG_CUTOFF = -105.0
SB_TQ = 256
SB_TK = 256


def _params(sem):
    return pltpu.CompilerParams(dimension_semantics=sem, vmem_limit_bytes=VMEM_LIMIT_BYTES)


def _pick(n, cap):
    if n <= cap:
        return n
    best = None
    for m in range(LANES, cap + 1, LANES):
        if n % m == 0:
            best = m
    assert best is not None, (n, cap)
    return best


def _rowwise(name, fn, rows, vecs, row_outs, acc_outs=(), tl=256):
    L = rows[0][0].shape[0]
    tl = min(tl, L)
    assert L % tl == 0
    n_in = len(rows) + len(vecs)
    n_ro = len(row_outs)

    def body(*refs):
        ins, ro, ao = refs[:n_in], refs[n_in:n_in + n_ro], refs[n_in + n_ro:]
        vals = fn(*[r[...] for r in ins])
        if not isinstance(vals, (tuple, list)):
            vals = (vals,)
        for o, v in zip(ro, vals[:n_ro]):
            o[...] = v.astype(o.dtype)
        if ao:
            @pl.when(pl.program_id(0) == 0)
            def _():
                for o in ao:
                    o[...] = jnp.zeros_like(o)
            for o, v in zip(ao, vals[n_ro:]):
                o[...] += v.astype(F32)

    in_specs = [pl.BlockSpec((tl, w), functools.partial(lambda i, cb: (i, cb), cb=cb)) for _, w, cb in rows]
    in_specs += [pl.BlockSpec(v.shape, lambda i: (0, 0)) for v in vecs]
    out_specs = [pl.BlockSpec((tl, w), lambda i: (i, 0)) for w, _ in row_outs]
    out_specs += [pl.BlockSpec(s, lambda i: (0, 0)) for s in acc_outs]
    out_shape = [jax.ShapeDtypeStruct((L, w), dt) for w, dt in row_outs]
    out_shape += [jax.ShapeDtypeStruct(s, F32) for s in acc_outs]
    return pl.pallas_call(
        body, name=name, grid=(L // tl,), in_specs=in_specs, out_specs=out_specs, out_shape=out_shape,
        compiler_params=_params(("arbitrary",)),
    )(*[a for a, _, _ in rows], *vecs)


def _mm(name, a, b, mode, out_dtype, tm=1024, tn_cap=1408, tk_cap=2816, side=None):
    if mode == "nn":
        (M, K), (_, N) = a.shape, b.shape
    elif mode == "nt":
        (M, K), (N, _) = a.shape, b.shape
    else:
        (K, M), (_, N) = a.shape, b.shape
        tm, tk_cap = 1408, 2048
    tm = _pick(M, tm)
    tn = _pick(N, tn_cap)
    tk = _pick(K, tk_cap)
    nk = K // tk
    grid = (M // tm, N // tn, nk)
    n_si = len(side.arrays) if side else 0
    n_so = len(side.out_shapes) if side else 0
    n_acc = 1 if nk > 1 else 0

    def body(a_ref, b_ref, *rest):
        s_in, o_ref, s_out = rest[:n_si], rest[n_si], rest[n_si + 1:n_si + 1 + n_so]
        scr = rest[n_si + 1 + n_so:]
        if side:
            at = [pl.program_id(d) for d in range(3)]
            is_first = jnp.logical_and(jnp.logical_and(at[0] == 0, at[1] == 0), at[2] == 0)
            is_last = jnp.logical_and(jnp.logical_and(at[0] == grid[0] - 1, at[1] == grid[1] - 1), at[2] == grid[2] - 1)

            @pl.when(is_first)
            def _():
                side.start(s_in, s_out, scr[n_acc:])

        _product(a_ref, b_ref, o_ref, scr)
        if side:
            @pl.when(is_last)
            def _():
                side.finish(s_in, s_out, scr[n_acc:])

    def _product(a_ref, b_ref, o_ref, scr):
        if mode == "nn":
            p = jnp.dot(a_ref[...], b_ref[...], preferred_element_type=F32)
        elif mode == "nt":
            p = lax.dot_general(a_ref[...], b_ref[...], (((1,), (1,)), ((), ())), preferred_element_type=F32)
        else:
            p = lax.dot_general(a_ref[...], b_ref[...], (((0,), (0,)), ((), ())), preferred_element_type=F32)
        if nk == 1:
            o_ref[...] = p.astype(o_ref.dtype)
        else:
            acc = scr[0]
            k = pl.program_id(2)

            @pl.when(k == 0)
            def _():
                acc[...] = p

            @pl.when(k > 0)
            def _():
                acc[...] += p

            @pl.when(k == nk - 1)
            def _():
                o_ref[...] = acc[...].astype(o_ref.dtype)

    if mode == "nn":
        a_spec = pl.BlockSpec((tm, tk), lambda i, j, k: (i, k))
        b_spec = pl.BlockSpec((tk, tn), lambda i, j, k: (k, j))
    elif mode == "nt":
        a_spec = pl.BlockSpec((tm, tk), lambda i, j, k: (i, k))
        b_spec = pl.BlockSpec((tn, tk), lambda i, j, k: (j, k))
    else:
        a_spec = pl.BlockSpec((tk, tm), lambda i, j, k: (k, i))
        b_spec = pl.BlockSpec((tk, tn), lambda i, j, k: (k, j))
    any_spec = pl.BlockSpec(memory_space=pl.ANY)
    res = pl.pallas_call(
        body, name=name, grid=grid, in_specs=[a_spec, b_spec] + [any_spec] * n_si,
        out_specs=[pl.BlockSpec((tm, tn), lambda i, j, k: (i, j))] + [any_spec] * n_so,
        out_shape=[jax.ShapeDtypeStruct((M, N), out_dtype)] + (side.out_shapes if side else []),
        scratch_shapes=([pltpu.VMEM((tm, tn), F32)] if nk > 1 else []) + (side.scratch if side else []),
        compiler_params=_params(("arbitrary", "arbitrary", "arbitrary")),
    )(a, b, *(side.arrays if side else []))
    return res if side else res[0]


def _mm_epi(name, a, b, mode, tn, extras, epi, outs, tm=512):
    if mode == "nn":
        (M, K), (_, N) = a.shape, b.shape
    else:
        (M, K), (N, _) = a.shape, b.shape
    tm = _pick(M, tm)
    n_ex = len(extras)

    def body(*refs):
        a_ref, b_ref, ex, o_refs = refs[0], refs[1], refs[2:2 + n_ex], refs[2 + n_ex:]
        if mode == "nn":
            p = jnp.dot(a_ref[...], b_ref[...], preferred_element_type=F32)
        else:
            p = lax.dot_general(a_ref[...], b_ref[...], (((1,), (1,)), ((), ())), preferred_element_type=F32)
        for o, v in zip(o_refs, epi(p, *[r[...] for r in ex])):
            o[...] = v.astype(o.dtype)

    a_spec = pl.BlockSpec((tm, K), lambda i, j: (i, 0))
    b_spec = pl.BlockSpec((K, tn), lambda i, j: (0, j)) if mode == "nn" else pl.BlockSpec((tn, K), lambda i, j: (j, 0))
    return pl.pallas_call(
        body, name=name, grid=(M // tm, N // tn),
        in_specs=[a_spec, b_spec] + [pl.BlockSpec((tm, w), lambda i, j: (i, j)) for _, w in extras],
        out_specs=[pl.BlockSpec((tm, w), lambda i, j: (i, j)) for w, _ in outs],
        out_shape=[jax.ShapeDtypeStruct((M, (N // tn) * w), dt) for w, dt in outs],
        compiler_params=_params(("arbitrary", "arbitrary")),
    )(a, b, *[e for e, _ in extras])


def _f(x):
    return x.astype(F32)


def _silu(x):
    return x * jax.nn.sigmoid(x)


def _softplus(x):
    return jnp.maximum(x, 0.0) + jnp.log1p(jnp.exp(-jnp.abs(x)))


def _rms(x, g):
    r = lax.rsqrt(jnp.mean(x * x, axis=-1, keepdims=True) + NORM_EPS)
    return x * r * g


def _adaln(x, g, scale, shift):
    return _rms(x, g) * (1.0 + scale) + shift


def _resid(x, y, gate, g):
    return x + gate * _rms(y, g)


def _mid(x, y, gate, g_post, g_pre, scale, shift):
    x_new = _resid(x, y, gate, g_post)
    return x_new, _adaln(x_new, g_pre, scale, shift)


def _swiglu(gt, up):
    return _silu(gt) * up


def _ssm_post(y_ssd, pre_xs, z, d_full, norm_w):
    y = (y_ssd + _silu(pre_xs) * d_full) * _silu(z)
    half = SSM_INNER // 2
    parts = []
    for g in range(2):
        yg = y[:, g * half:(g + 1) * half]
        parts.append(yg * lax.rsqrt(jnp.mean(yg * yg, axis=-1, keepdims=True) + NORM_EPS))
    return jnp.concatenate(parts, axis=1) * norm_w


def _first_fwd(x, vecs):
    return _rowwise("adaln_first", lambda x, g, sc, sh: _adaln(x, g, sc, sh),
                    [(x, D_MODEL, 0)], vecs, [(D_MODEL, BF16)], tl=512)[0]


def _mid_fwd(name, x, y, vecs):
    return _rowwise(name, lambda x, y, *v: _mid(x, _f(y), *v),
                    [(x, D_MODEL, 0), (y, D_MODEL, 0)], vecs, [(D_MODEL, F32), (D_MODEL, BF16)], tl=512)


def _mid_bwd(name, x, y, dx_new, dh, vecs):
    def fn(x, y, dxn, dh, *v):
        _, vjp = jax.vjp(_mid, x, _f(y), *v)
        return vjp((dxn, _f(dh)))

    vec = (1, D_MODEL)
    return _rowwise(name, fn, [(x, D_MODEL, 0), (y, D_MODEL, 0), (dx_new, D_MODEL, 0), (dh, D_MODEL, 0)], vecs,
                    [(D_MODEL, F32), (D_MODEL, BF16)], [vec] * 5)


def _first_bwd(x, dx_in, dh, vecs):
    def fn(x, dxi, dh, *v):
        _, vjp = jax.vjp(_adaln, x, *v)
        dx, dg, dsc, dsh = vjp(_f(dh))
        return dx + dxi, dg, dsc, dsh

    vec = (1, D_MODEL)
    return _rowwise("adaln_first_bwd", fn, [(x, D_MODEL, 0), (dx_in, D_MODEL, 0), (dh, D_MODEL, 0)], vecs,
                    [(D_MODEL, F32)], [vec] * 3)


def _last_bwd(x1, f, target, vecs):
    def fn(x1, f, t, gate, g):
        x2, vjp = jax.vjp(_resid, x1, _f(f), gate, g)
        err = x2 - t
        dx1, df, dgate, dg = vjp(err * (1.0 / D_MODEL))
        loss_cols = jnp.sum(err * err, axis=0, keepdims=True) * (0.5 / D_MODEL)
        return dx1, df, dgate, dg, loss_cols

    vec = (1, D_MODEL)
    return _rowwise("loss_last_bwd", fn, [(x1, D_MODEL, 0), (f, D_MODEL, 0), (target, D_MODEL, 0)], vecs,
                    [(D_MODEL, F32), (D_MODEL, BF16)], [vec] * 3)


HALO = 16


def _shift_down(u, prev, k):
    rows = lax.broadcasted_iota(jnp.int32, u.shape, 0)
    v = pltpu.roll(u, k, 0)
    for t in range(k):
        v = jnp.where(rows == t, prev[HALO - k + t:HALO - k + t + 1, :], v)
    return v


def _shift_up(u, nxt, k):
    n = u.shape[0]
    rows = lax.broadcasted_iota(jnp.int32, u.shape, 0)
    v = pltpu.roll(u, n - k, 0)
    for t in range(k):
        v = jnp.where(rows == n - k + t, nxt[t:t + 1, :], v)
    return v


def _conv_specs(L, tl, width, col_block):
    per = tl // HALO
    last = L // HALO - 1
    main = pl.BlockSpec((tl, width), lambda i: (i, col_block))
    before = pl.BlockSpec((HALO, width), lambda i: (jnp.maximum(i * per - 1, 0), col_block))
    after = pl.BlockSpec((HALO, width), lambda i: (jnp.minimum((i + 1) * per, last), col_block))
    return main, before, after


def _shortconv_fwd(P, w, tl=512):
    L = P.shape[0]
    tl = min(tl, L)
    C = SC_WIDTH
    main, before, _ = _conv_specs(L, tl, 3 * C, 0)

    def body(p_ref, h_ref, w_ref, o_ref):
        first = (pl.program_id(0) == 0)
        p, h = _f(p_ref[...]), _f(h_ref[...])
        b, u = p[:, :C], p[:, C:2 * C] * p[:, 2 * C:]
        uh = jnp.where(first, 0.0, h[:, C:2 * C] * h[:, 2 * C:])
        wv = w_ref[...]
        cv = wv[2:3] * u + wv[1:2] * _shift_down(u, uh, 1) + wv[0:1] * _shift_down(u, uh, 2)
        o_ref[...] = (b * cv).astype(o_ref.dtype)

    return pl.pallas_call(
        body, name="shortconv_fwd", grid=(L // tl,),
        in_specs=[main, before, pl.BlockSpec(w.shape, lambda i: (0, 0))],
        out_specs=pl.BlockSpec((tl, C), lambda i: (i, 0)),
        out_shape=jax.ShapeDtypeStruct((L, C), BF16), compiler_params=_params(("arbitrary",)),
    )(P, P, w)


def _shortconv_bwd(P, dya, w, tl=512):
    L = P.shape[0]
    tl = min(tl, L)
    C = SC_WIDTH
    main, before, after = _conv_specs(L, tl, 3 * C, 0)
    dmain, _, dafter = _conv_specs(L, tl, C, 0)
    n = L // tl

    def body(p_ref, h_ref, n_ref, d_ref, dn_ref, w_ref, o_ref, dw0, dw1, dw2):
        i = pl.program_id(0)
        p, h, nx = _f(p_ref[...]), _f(h_ref[...]), _f(n_ref[...])
        b, c, x = p[:, :C], p[:, C:2 * C], p[:, 2 * C:]
        u = c * x
        uh = jnp.where(i == 0, 0.0, h[:, C:2 * C] * h[:, 2 * C:])
        u1, u2 = _shift_down(u, uh, 1), _shift_down(u, uh, 2)
        wv = w_ref[...]
        cv = wv[2:3] * u + wv[1:2] * u1 + wv[0:1] * u2
        dy = _f(d_ref[...])
        dcv = dy * b
        dcv_n = jnp.where(i == n - 1, 0.0, _f(dn_ref[...]) * nx[:, :C])
        du = wv[2:3] * dcv + wv[1:2] * _shift_up(dcv, dcv_n, 1) + wv[0:1] * _shift_up(dcv, dcv_n, 2)
        o_ref[:, :C] = (dy * cv).astype(o_ref.dtype)
        o_ref[:, C:2 * C] = (du * x).astype(o_ref.dtype)
        o_ref[:, 2 * C:] = (du * c).astype(o_ref.dtype)

        @pl.when(i == 0)
        def _():
            for r in (dw0, dw1, dw2):
                r[...] = jnp.zeros_like(r)

        dw0[...] += jnp.sum(dcv * u2, axis=0, keepdims=True)
        dw1[...] += jnp.sum(dcv * u1, axis=0, keepdims=True)
        dw2[...] += jnp.sum(dcv * u, axis=0, keepdims=True)

    vec = pl.BlockSpec((1, C), lambda i: (0, 0))
    return pl.pallas_call(
        body, name="shortconv_bwd", grid=(n,),
        in_specs=[main, before, after, dmain, dafter, pl.BlockSpec(w.shape, lambda i: (0, 0))],
        out_specs=[pl.BlockSpec((tl, 3 * C), lambda i: (i, 0)), vec, vec, vec],
        out_shape=[jax.ShapeDtypeStruct((L, 3 * C), BF16)] + [jax.ShapeDtypeStruct((1, C), F32)] * 3,
        compiler_params=_params(("arbitrary",)),
    )(P, P, P, dya, dya, w)


def _ssmconv_fwd(P, w, bias, tl=512):
    L = P.shape[0]
    tl = min(tl, L)
    C = SSM_CONV_DIM
    main, before, _ = _conv_specs(L, tl, C, P_XBC // C)

    def body(p_ref, h_ref, w_ref, b_ref, o_ref):
        u = _f(p_ref[...])
        uh = jnp.where(pl.program_id(0) == 0, 0.0, _f(h_ref[...]))
        wv = w_ref[...]
        acc = wv[3:4] * u + b_ref[...]
        for k in range(1, SSM_CONV):
            acc = acc + wv[3 - k:4 - k] * _shift_down(u, uh, k)
        o_ref[...] = acc

    return pl.pallas_call(
        body, name="ssmconv_fwd", grid=(L // tl,),
        in_specs=[main, before, pl.BlockSpec(w.shape, lambda i: (0, 0)), pl.BlockSpec(bias.shape, lambda i: (0, 0))],
        out_specs=pl.BlockSpec((tl, C), lambda i: (i, 0)),
        out_shape=jax.ShapeDtypeStruct((L, C), F32), compiler_params=_params(("arbitrary",)),
    )(P, P, w, bias)


def _ssmconv_bwd(P, dpre, w, tl=512):
    L = P.shape[0]
    tl = min(tl, L)
    C = SSM_CONV_DIM
    main, before, _ = _conv_specs(L, tl, C, P_XBC // C)
    dmain, _, dafter = _conv_specs(L, tl, C, 0)
    n = L // tl

    def body(p_ref, h_ref, d_ref, dn_ref, w_ref, o_ref, dw0, dw1, dw2, dw3, db):
        i = pl.program_id(0)
        u = _f(p_ref[...])
        uh = jnp.where(i == 0, 0.0, _f(h_ref[...]))
        d = d_ref[...]
        dn = jnp.where(i == n - 1, 0.0, dn_ref[...])
        wv = w_ref[...]
        du = wv[3:4] * d
        for k in range(1, SSM_CONV):
            du = du + wv[3 - k:4 - k] * _shift_up(d, dn, k)
        o_ref[...] = du.astype(o_ref.dtype)

        @pl.when(i == 0)
        def _():
            for r in (dw0, dw1, dw2, dw3, db):
                r[...] = jnp.zeros_like(r)

        for k, r in ((3, dw0), (2, dw1), (1, dw2)):
            r[...] += jnp.sum(d * _shift_down(u, uh, k), axis=0, keepdims=True)
        dw3[...] += jnp.sum(d * u, axis=0, keepdims=True)
        db[...] += jnp.sum(d, axis=0, keepdims=True)

    vec = pl.BlockSpec((1, C), lambda i: (0, 0))
    return pl.pallas_call(
        body, name="ssmconv_bwd", grid=(n,),
        in_specs=[main, before, dmain, dafter, pl.BlockSpec(w.shape, lambda i: (0, 0))],
        out_specs=[pl.BlockSpec((tl, C), lambda i: (i, 0))] + [vec] * 5,
        out_shape=[jax.ShapeDtypeStruct((L, C), BF16)] + [jax.ShapeDtypeStruct((1, C), F32)] * 5,
        compiler_params=_params(("arbitrary",)),
    )(P, P, dpre, dpre, w)


def _dot_nt(a, b):
    return lax.dot_general(a, b, (((1,), (1,)), ((), ())), preferred_element_type=F32)


def _dot_tn(a, b):
    return lax.dot_general(a, b, (((0,), (0,)), ((), ())), preferred_element_type=F32)


def _split3(x):
    hi = x.astype(BF16)
    r = x - hi.astype(F32)
    mid = r.astype(BF16)
    return hi, mid, (r - mid.astype(F32)).astype(BF16)


@jax.custom_vjp
def _xm01(x, m):
    return sum(jnp.dot(t, m, preferred_element_type=F32) for t in _split3(x))


def _xm01_fwd(x, m):
    return _xm01(x, m), m


def _xm01_bwd(m, g):
    return sum(_dot_nt(t, m) for t in _split3(g)), jnp.zeros_like(m)


_xm01.defvjp(_xm01_fwd, _xm01_bwd)


@jax.custom_vjp
def _m01x(m, x):
    return sum(jnp.dot(m, t, preferred_element_type=F32) for t in _split3(x))


def _m01x_fwd(m, x):
    return _m01x(m, x), m


def _m01x_bwd(m, g):
    return jnp.zeros_like(m), sum(_dot_tn(m, t) for t in _split3(g))


_m01x.defvjp(_m01x_fwd, _m01x_bwd)


def _ssd_chunk(pre, dtr, s_prev, dtb, alog):
    T = pre.shape[0]
    act = _silu(pre)
    xs, bm, cm = act[:, :SSM_INNER], act[:, SSM_INNER:SSM_INNER + 128], act[:, SSM_INNER + 128:]
    lane = lax.broadcasted_iota(jnp.int32, (1, LANES), 1)
    dt = jnp.where(lane < SSM_HEADS, _softplus(dtr + dtb), 0.0)
    a = dt * (-jnp.exp(alog))
    ri = lax.broadcasted_iota(jnp.int32, (T, T), 0)
    ci = lax.broadcasted_iota(jnp.int32, (T, T), 1)
    causal = ci <= ri
    a_cs = _m01x(causal.astype(BF16), a)
    eh = lax.broadcasted_iota(jnp.int32, (LANES, SSM_INNER), 0)
    ej = lax.broadcasted_iota(jnp.int32, (LANES, SSM_INNER), 1)
    expand = (lax.shift_right_logical(ej, 6) == eh).astype(BF16)
    dt_full = _xm01(dt, expand)
    acs_full = _xm01(a_cs, expand)
    alast_full = acs_full[T - 1:T, :]
    xdt = xs * dt_full
    a_cs_t = a_cs.T
    ys, s_new = [], []
    for g in range(2):
        in_group = lax.shift_right_logical(lane, 6) == g
        cg = jnp.where(in_group, cm, 0.0).astype(BF16)
        bg = jnp.where(in_group, bm, 0.0).astype(BF16)
        scores = _dot_nt(cg, bg)
        for pp in range(2):
            hp = 2 * g + pp
            cols = slice(hp * LANES, (hp + 1) * LANES)
            xp, acsp = xdt[:, cols], acs_full[:, cols]
            per_head = []
            for hh in range(2):
                h = 2 * hp + hh
                decay = jnp.exp(jnp.where(causal, a_cs[:, h:h + 1] - a_cs_t[h:h + 1, :], -jnp.inf))
                per_head.append(jnp.dot((scores * decay).astype(BF16), xp.astype(BF16), preferred_element_type=F32))
            y_diag = jnp.where(lane < SSM_STATE, per_head[0], per_head[1])
            sp = s_prev[hp * LANES:(hp + 1) * LANES, :]
            y_off = jnp.dot(cg, sp.astype(BF16), preferred_element_type=F32) * jnp.exp(acsp)
            ys.append(y_diag + y_off)
            to_end = jnp.exp(alast_full[:, cols] - acsp)
            s_new.append(sp * jnp.exp(alast_full[:, cols]) + _dot_tn(bg, (xp * to_end).astype(BF16)))
    return jnp.concatenate(ys, axis=1), jnp.concatenate(s_new, axis=0)


def _ssd_fwd(pre, P, dtb, alog):
    L = pre.shape[0]
    T = min(SSM_CHUNK, L)
    nc = L // T

    def body(pre_ref, dt_ref, dtb_ref, al_ref, y_ref, st_ref, s_scr):
        @pl.when(pl.program_id(0) == 0)
        def _():
            s_scr[...] = jnp.zeros_like(s_scr)

        st_ref[0] = s_scr[...]
        y, s = _ssd_chunk(pre_ref[...], _f(dt_ref[...]), s_scr[...], dtb_ref[...], al_ref[...])
        y_ref[...] = y
        s_scr[...] = s

    vec = pl.BlockSpec((1, LANES), lambda i: (0, 0))
    return pl.pallas_call(
        body, name="ssd_fwd", grid=(nc,),
        in_specs=[pl.BlockSpec((T, SSM_CONV_DIM), lambda i: (i, 0)), pl.BlockSpec((T, LANES), lambda i: (i, P_DT // LANES)),
                  vec, vec],
        out_specs=[pl.BlockSpec((T, SSM_INNER), lambda i: (i, 0)), pl.BlockSpec((1, 512, LANES), lambda i: (i, 0, 0))],
        out_shape=[jax.ShapeDtypeStruct((L, SSM_INNER), F32), jax.ShapeDtypeStruct((nc, 512, LANES), F32)],
        scratch_shapes=[pltpu.VMEM((512, LANES), F32)], compiler_params=_params(("arbitrary",)),
    )(pre, P, dtb, alog)


def _ssd_bwd(pre, P, states, dy, dxs_extra, dtb, alog):
    L = pre.shape[0]
    T = min(SSM_CHUNK, L)
    nc = L // T

    def body(pre_ref, dt_ref, st_ref, dy_ref, dx_ref, dtb_ref, al_ref, dpre_ref, ddt_ref, ddtb_ref, dal_ref, ds_scr):
        @pl.when(pl.program_id(0) == 0)
        def _():
            ds_scr[...] = jnp.zeros_like(ds_scr)
            ddtb_ref[...] = jnp.zeros_like(ddtb_ref)
            dal_ref[...] = jnp.zeros_like(dal_ref)

        _, vjp = jax.vjp(_ssd_chunk, pre_ref[...], _f(dt_ref[...]), st_ref[0], dtb_ref[...], al_ref[...])
        dpre, ddt, ds, ddtb, dal = vjp((dy_ref[...], ds_scr[...]))
        dpre_ref[:, :SSM_INNER] = dpre[:, :SSM_INNER] + dx_ref[...]
        dpre_ref[:, SSM_INNER:] = dpre[:, SSM_INNER:]
        ddt_ref[:, :LANES] = ddt.astype(ddt_ref.dtype)
        ddt_ref[:, LANES:] = jnp.zeros((T, DT_PAD - LANES), ddt_ref.dtype)
        ds_scr[...] = ds
        ddtb_ref[...] += ddtb
        dal_ref[...] += dal

    vec = pl.BlockSpec((1, LANES), lambda i: (0, 0))
    rev = lambda i: (nc - 1 - i, 0)
    return pl.pallas_call(
        body, name="ssd_bwd", grid=(nc,),
        in_specs=[pl.BlockSpec((T, SSM_CONV_DIM), rev), pl.BlockSpec((T, LANES), lambda i: (nc - 1 - i, P_DT // LANES)),
                  pl.BlockSpec((1, 512, LANES), lambda i: (nc - 1 - i, 0, 0)),
                  pl.BlockSpec((T, SSM_INNER), rev), pl.BlockSpec((T, SSM_INNER), rev), vec, vec],
        out_specs=[pl.BlockSpec((T, SSM_CONV_DIM), rev), pl.BlockSpec((T, DT_PAD), rev), vec, vec],
        out_shape=[jax.ShapeDtypeStruct((L, SSM_CONV_DIM), F32), jax.ShapeDtypeStruct((L, DT_PAD), BF16),
                   jax.ShapeDtypeStruct((1, LANES), F32), jax.ShapeDtypeStruct((1, LANES), F32)],
        scratch_shapes=[pltpu.VMEM((512, LANES), F32)], compiler_params=_params(("arbitrary",)),
    )(pre, P, states, dy, dxs_extra, dtb, alog)


def _sb_scores(qm, kb, later, strict, mask):
    z = _dot_nt(qm, kb)
    lk = jnp.minimum(-z, 0.0) - jnp.log(1.0 + jnp.exp(-jnp.abs(z)))
    if mask is not None:
        lk = jnp.where(mask, lk, 0.0)
    log_a = z + lk + jnp.dot(lk.astype(BF16), strict, preferred_element_type=F32) + later
    if mask is not None:
        log_a = jnp.where(mask, log_a, -jnp.inf)
    return z, lk, log_a


def _dot_split(x, m):
    hi = x.astype(BF16)
    lo = (x - hi.astype(F32)).astype(BF16)
    return jnp.dot(hi, m, preferred_element_type=F32) + jnp.dot(lo, m, preferred_element_type=F32)


def _sb_setup(q_ref, i, tq, tk):
    lane = lax.broadcasted_iota(jnp.int32, (1, LANES), 1)
    first = lane < SB_HEAD_DIM
    q = q_ref[...] * (SB_HEAD_DIM ** -0.5)
    qms = (jnp.where(first, q, jnp.zeros_like(q)), jnp.where(first, jnp.zeros_like(q), q))
    j0 = lax.div(i * tq, tk)
    ri = lax.broadcasted_iota(jnp.int32, (tq, tk), 0)
    ci = lax.broadcasted_iota(jnp.int32, (tq, tk), 1)
    diag_mask = (ci + (j0 * tk - i * tq)) < ri
    kr = lax.broadcasted_iota(jnp.int32, (tk, tk), 0)
    kc = lax.broadcasted_iota(jnp.int32, (tk, tk), 1)
    strict = (kr > kc).astype(BF16)
    return first, qms, j0, diag_mask, strict


def _sb_continue(c):
    return jnp.logical_and(c[0] >= 0, jnp.maximum(jnp.max(c[1][0]), jnp.max(c[1][1])) > SB_LOG_CUTOFF)


def _sb_fwd(P):
    L = P.shape[0]
    tq, tk = min(SB_TQ, L), min(SB_TK, L)
    nq = L // tq
    qb = P_B // LANES

    def body(q_ref, k_ref, v_ref, o_ref, of_ref):
        i = pl.program_id(1)
        first, qms, j0, diag_mask, strict = _sb_setup(q_ref, i, tq, tk)

        def tiles(j, laters, accs, mask=None, valid=None):
            off = pl.multiple_of(j * tk, tk)
            kb, vb = k_ref[pl.ds(off, tk), :], v_ref[pl.ds(off, tk), :]
            new_l, new_a = [], []
            for qm, later, acc in zip(qms, laters, accs):
                gate = later if valid is None else jnp.where(valid, later, -jnp.inf)
                _, lk, log_a = _sb_scores(qm, kb, gate, strict, mask)
                new_a.append(acc + jnp.dot(jnp.exp(log_a).astype(BF16), vb, preferred_element_type=F32))
                rows = jnp.sum(lk, axis=1, keepdims=True)
                new_l.append(later + (rows if valid is None else jnp.where(valid, rows, 0.0)))
            return tuple(new_l), tuple(new_a)

        def step(c):
            laters, accs = tiles(c[0], c[1], c[2])
            return c[0] - 1, laters, accs

        zero, zacc = jnp.zeros((tq, 1), F32), jnp.zeros((tq, LANES), F32)
        laters, accs = tiles(j0, (zero, zero), (zacc, zacc), mask=diag_mask)
        laters, accs = tiles(jnp.maximum(j0 - 1, 0), laters, accs, valid=j0 >= 1)
        _, _, accs = lax.while_loop(_sb_continue, step, (j0 - 2, laters, accs))
        out = jnp.where(first, accs[0], accs[1])
        o_ref[...] = out.astype(o_ref.dtype)
        of_ref[...] = out

    tile_spec = pl.BlockSpec((tq, LANES), lambda p, i: (i, p))
    return pl.pallas_call(
        body, name="sb_fwd", grid=(2, nq),
        in_specs=[pl.BlockSpec((tq, LANES), lambda p, i: (i, qb + p)),
                  pl.BlockSpec((L, LANES), lambda p, i: (0, qb + 2 + p)),
                  pl.BlockSpec((L, LANES), lambda p, i: (0, qb + 4 + p))],
        out_specs=[tile_spec, tile_spec],
        out_shape=[jax.ShapeDtypeStruct((L, 2 * LANES), BF16), jax.ShapeDtypeStruct((L, 2 * LANES), F32)],
        compiler_params=_params(("arbitrary", "arbitrary")),
    )(P, P, P)


def _sb_bwd(P, dyb, yb32):
    L = P.shape[0]
    tq, tk = min(SB_TQ, L), min(SB_TK, L)
    nq = L // tq
    qb = P_B // LANES

    def body(q_ref, k_ref, v_ref, do_ref, of_ref, dq_ref, dk_ref, dv_ref):
        i = pl.program_id(1)
        first, qms, j0, diag_mask, strict = _sb_setup(q_ref, i, tq, tk)

        @pl.when(i == 0)
        def _():
            dk_ref[...] = jnp.zeros_like(dk_ref)
            dv_ref[...] = jnp.zeros_like(dv_ref)

        do = do_ref[...]
        doms = (jnp.where(first, do, jnp.zeros_like(do)), jnp.where(first, jnp.zeros_like(do), do))
        prod = _f(do) * of_ref[...]
        totals = (jnp.sum(jnp.where(first, prod, 0.0), axis=1, keepdims=True),
                  jnp.sum(jnp.where(first, 0.0, prod), axis=1, keepdims=True))

        def tiles(j, laters, later_gs, accs, mask=None, valid=None):
            off = pl.multiple_of(j * tk, tk)
            kb, vb = k_ref[pl.ds(off, tk), :], v_ref[pl.ds(off, tk), :]
            new_l, new_g, new_a = [], [], []
            dk = dv = None
            for h in range(2):
                gate = laters[h] if valid is None else jnp.where(valid, laters[h], -jnp.inf)
                z, lk, log_a = _sb_scores(qms[h], kb, gate, strict, mask)
                att = jnp.exp(log_a).astype(BF16)
                g = _f(att) * _dot_nt(doms[h], vb)
                before = totals[h] - later_gs[h]
                if valid is not None:
                    before = jnp.where(valid, before, 0.0)
                dz = g - (before - _dot_split(g, strict)) * jnp.exp(z + lk)
                if mask is not None:
                    dz = jnp.where(mask, dz, 0.0)
                dzb = dz.astype(BF16)
                dk_h, dv_h = _dot_tn(dzb, qms[h]), _dot_tn(att, doms[h])
                dk, dv = (dk_h, dv_h) if h == 0 else (dk + dk_h, dv + dv_h)
                new_a.append(accs[h] + jnp.dot(dzb, kb, preferred_element_type=F32))
                rows = jnp.sum(lk, axis=1, keepdims=True)
                new_l.append(laters[h] + (rows if valid is None else jnp.where(valid, rows, 0.0)))
                new_g.append(later_gs[h] + jnp.sum(g, axis=1, keepdims=True))
            dk_ref[pl.ds(off, tk), :] += dk
            dv_ref[pl.ds(off, tk), :] += dv
            return tuple(new_l), tuple(new_g), tuple(new_a)

        def step(c):
            return (c[0] - 1,) + tiles(c[0], c[1], c[2], c[3])

        zero, zacc = jnp.zeros((tq, 1), F32), jnp.zeros((tq, LANES), F32)
        carry = tiles(j0, (zero, zero), (zero, zero), (zacc, zacc), mask=diag_mask)
        carry = tiles(jnp.maximum(j0 - 1, 0), *carry, valid=j0 >= 1)
        _, _, _, accs = lax.while_loop(_sb_continue, step, (j0 - 2,) + carry)
        dq_ref[...] = jnp.where(first, accs[0], accs[1]) * (SB_HEAD_DIM ** -0.5)

    full = pl.BlockSpec((L, LANES), lambda p, i: (0, p))
    tile_spec = pl.BlockSpec((tq, LANES), lambda p, i: (i, p))
    return pl.pallas_call(
        body, name="sb_bwd", grid=(2, nq),
        in_specs=[pl.BlockSpec((tq, LANES), lambda p, i: (i, qb + p)),
                  pl.BlockSpec((L, LANES), lambda p, i: (0, qb + 2 + p)),
                  pl.BlockSpec((L, LANES), lambda p, i: (0, qb + 4 + p)), tile_spec, tile_spec],
        out_specs=[tile_spec, full, full],
        out_shape=[jax.ShapeDtypeStruct((L, 2 * LANES), F32)] * 3,
        compiler_params=_params(("arbitrary", "arbitrary")),
    )(P, P, P, dyb, yb32)


MOD_SHARD = N_MOD * D_MODEL // N_CHIPS


def _mod_fwd(c_all, mod_w, mod_b_sh):
    tn = 512

    def body(c_ref, w_ref, b_ref, o_ref):
        o_ref[0] = jnp.dot(_silu(c_ref[...]), w_ref[0], precision=HIGHEST, preferred_element_type=F32) + b_ref[0]

    return pl.pallas_call(
        body, name="mod_fwd", grid=(DEPTH, MOD_SHARD // tn),
        in_specs=[pl.BlockSpec((N_DEV, D_MODEL), lambda l, j: (0, 0)),
                  pl.BlockSpec((1, D_MODEL, tn), lambda l, j: (l, 0, j)),
                  pl.BlockSpec((1, 1, tn), lambda l, j: (l, 0, j))],
        out_specs=pl.BlockSpec((1, N_DEV, tn), lambda l, j: (l, 0, j)),
        out_shape=jax.ShapeDtypeStruct((DEPTH, N_DEV, MOD_SHARD), F32),
        compiler_params=_params(("arbitrary", "arbitrary")),
    )(c_all, mod_w, mod_b_sh)


def _mod_bwd(c_all, dmod_sh):
    tn = 512

    def body(c_ref, d_ref, o_ref):
        o_ref[0] = lax.dot_general(_silu(c_ref[...]), d_ref[0], (((0,), (0,)), ((), ())), precision=HIGHEST,
                                   preferred_element_type=F32)

    return pl.pallas_call(
        body, name="mod_bwd", grid=(DEPTH, MOD_SHARD // tn),
        in_specs=[pl.BlockSpec((N_DEV, D_MODEL), lambda l, j: (0, 0)),
                  pl.BlockSpec((1, N_DEV, tn), lambda l, j: (l, 0, j))],
        out_specs=pl.BlockSpec((1, D_MODEL, tn), lambda l, j: (l, 0, j)),
        out_shape=jax.ShapeDtypeStruct((DEPTH, D_MODEL, MOD_SHARD), F32),
        compiler_params=_params(("arbitrary", "arbitrary")),
    )(c_all, dmod_sh)


def _row_tile(rows, cap):
    if rows <= cap:
        return rows
    best = None
    for t in range(8, cap + 1, 8):
        if rows % t == 0:
            best = t
    assert best is not None, (rows, cap)
    return best


def _adamw(name, w, gs, m, v, tr=256):
    R, W = w.shape
    by_layer = any(isinstance(t, tuple) for t in gs)
    tr = _row_tile(R // 2 if by_layer else R, tr)
    per = (R // 2) // tr

    flat, specs = [], []
    for t in gs:
        if isinstance(t, tuple):
            flat += list(t)
            specs += [pl.BlockSpec((tr, W), lambda i: (jnp.minimum(i, per - 1), 0)),
                      pl.BlockSpec((tr, W), lambda i: (jnp.maximum(i - per, 0), 0))]
        else:
            flat.append(t)
            specs.append(pl.BlockSpec((tr, W), lambda i: (i, 0)))
    ng = len(flat)

    def body(*refs):
        w_ref, g_refs, (m_ref, v_ref) = refs[0], list(refs[1:1 + ng]), refs[1 + ng:3 + ng]
        g_out, d_out, m_out, v_out = refs[3 + ng:]
        g = None
        for t in gs:
            if isinstance(t, tuple):
                lo, hi = g_refs.pop(0), g_refs.pop(0)
                term = jnp.where(pl.program_id(0) < per, lo[...], hi[...])
            else:
                term = g_refs.pop(0)[...]
            g = term if g is None else g + term
        mm = ADAM_B1 * m_ref[...] + (1.0 - ADAM_B1) * g
        vv = ADAM_B2 * v_ref[...] + (1.0 - ADAM_B2) * (g * g)
        m_hat = mm / (1.0 - ADAM_B1 ** ADAM_STEP)
        v_hat = vv / (1.0 - ADAM_B2 ** ADAM_STEP)
        g_out[...] = g
        d_out[...] = -ADAM_LR * (m_hat / (jnp.sqrt(v_hat) + ADAM_EPS) + ADAM_WD * w_ref[...])
        m_out[...] = mm
        v_out[...] = vv

    spec = pl.BlockSpec((tr, W), lambda i: (i, 0))
    return pl.pallas_call(
        body, name=name, grid=(R // tr,), in_specs=[spec] + specs + [spec, spec], out_specs=[spec] * 4,
        out_shape=[jax.ShapeDtypeStruct((R, W), F32)] * 4, compiler_params=_params(("arbitrary",)),
    )(w, *flat, m, v)


def _sum_slots(name, a, tr=256):
    n, R, W = a.shape
    tr = _row_tile(R, tr)

    def body(a_ref, o_ref):
        acc = _f(a_ref[0])
        for j in range(1, n):
            acc = acc + _f(a_ref[j])
        o_ref[...] = acc

    return pl.pallas_call(
        body, name=name, grid=(R // tr,), in_specs=[pl.BlockSpec((n, tr, W), lambda i: (0, i, 0))],
        out_specs=pl.BlockSpec((tr, W), lambda i: (i, 0)), out_shape=jax.ShapeDtypeStruct((R, W), F32),
        compiler_params=_params(("arbitrary",)),
    )(a)


def _here():
    return lax.axis_index("x"), lax.axis_index("y"), lax.axis_index("c")


def _flip(v, d):
    return 1 - v if d else v


def _allgather_small(name, buf):
    R = buf.shape[0]
    rel = [(dx, dy, dc) for dx in (0, 1) for dy in (0, 1) for dc in (0, 1)][1:]

    def body(x_ref, o_ref, send, recv, lsem):
        x, y, c = _here()
        me = 4 * x + 2 * y + c
        mine = pltpu.make_async_copy(x_ref, o_ref.at[me], lsem)
        mine.start()

        def copy(k, slot):
            dx, dy, dc = rel[k]
            return pltpu.make_async_remote_copy(
                src_ref=x_ref, dst_ref=o_ref.at[slot], send_sem=send.at[k], recv_sem=recv.at[k],
                device_id=(_flip(x, dx), _flip(y, dy), _flip(c, dc)), device_id_type=MESH_ID)

        sent = [copy(k, me) for k in range(len(rel))]
        for cp in sent:
            cp.start()
        for k, (dx, dy, dc) in enumerate(rel):
            copy(k, 4 * _flip(x, dx) + 2 * _flip(y, dy) + _flip(c, dc)).wait_recv()
        for cp in sent:
            cp.wait_send()
        mine.wait()

    return pl.pallas_call(
        body, name=name, out_shape=jax.ShapeDtypeStruct((N_DEV, R, LANES), F32),
        in_specs=[pl.BlockSpec(memory_space=pltpu.VMEM)], out_specs=pl.BlockSpec(memory_space=pltpu.VMEM),
        scratch_shapes=[pltpu.SemaphoreType.DMA((7,)), pltpu.SemaphoreType.DMA((7,)), pltpu.SemaphoreType.DMA],
    )(buf)


CHIP_REL = [(1, 0), (0, 1), (1, 1)]


class _Side:
    def __init__(self, arrays, out_shapes, scratch, start, finish):
        self.arrays, self.out_shapes, self.scratch, self.start, self.finish = arrays, out_shapes, scratch, start, finish


def _chip_of(k):
    x, y, _ = _here()
    dx, dy = CHIP_REL[k]
    return _flip(x, dx), _flip(y, dy)


def _scatter_side(arrays):
    n = len(arrays)

    def parts(ins, outs, sems):
        send, recv, lsem = sems
        x, y, c = _here()
        s = 2 * x + y

        def copy(w, k, mine):
            px, py = _chip_of(k)
            return pltpu.make_async_remote_copy(
                src_ref=ins[w].at[2 * px + py], dst_ref=outs[w].at[s if mine else 2 * px + py],
                send_sem=send.at[3 * w + k], recv_sem=recv.at[3 * w + k], device_id=(px, py, c), device_id_type=MESH_ID)

        local = [pltpu.make_async_copy(ins[w].at[s], outs[w].at[s], lsem.at[w]) for w in range(n)]
        return copy, local

    def start(ins, outs, sems):
        copy, local = parts(ins, outs, sems)
        for cp in local:
            cp.start()
        for w in range(n):
            for k in range(3):
                copy(w, k, True).start()

    def finish(ins, outs, sems):
        copy, local = parts(ins, outs, sems)
        for w in range(n):
            for k in range(3):
                copy(w, k, False).wait_recv()
        for w in range(n):
            for k in range(3):
                copy(w, k, True).wait_send()
        for cp in local:
            cp.wait()

    scratch = [pltpu.SemaphoreType.DMA((3 * n,)), pltpu.SemaphoreType.DMA((3 * n,)), pltpu.SemaphoreType.DMA((n,))]
    return _Side(arrays, [jax.ShapeDtypeStruct(a.shape, a.dtype) for a in arrays], scratch, start, finish)


def _gather_side(shards):
    n = len(shards)

    def parts(ins, outs, sems):
        send, recv, fsend, frecv, lsem = sems
        x, y, c = _here()
        s = 2 * x + y

        def half(ref, w, which):
            rows = shards[w].shape[0] // 2
            return ref.at[pl.ds(pl.multiple_of(which * rows, 16), rows)]

        def over_ici(w, k, mine):
            px, py = _chip_of(k)
            return pltpu.make_async_remote_copy(
                src_ref=half(ins[w], w, c), dst_ref=half(outs[w].at[s if mine else 2 * px + py], w, c),
                send_sem=send.at[3 * w + k], recv_sem=recv.at[3 * w + k], device_id=(px, py, c), device_id_type=MESH_ID)

        def to_sibling(w, k, which):
            px, py = _chip_of(k)
            part = half(outs[w].at[2 * px + py], w, which)
            return pltpu.make_async_remote_copy(
                src_ref=part, dst_ref=part, send_sem=fsend.at[3 * w + k], recv_sem=frecv.at[3 * w + k],
                device_id=(x, y, 1 - c), device_id_type=MESH_ID)

        local = [pltpu.make_async_copy(ins[w], outs[w].at[s], lsem.at[w]) for w in range(n)]
        return c, over_ici, to_sibling, local

    def start(ins, outs, sems):
        _, over_ici, _, local = parts(ins, outs, sems)
        for cp in local:
            cp.start()
        for w in range(n):
            for k in range(3):
                over_ici(w, k, True).start()

    def finish(ins, outs, sems):
        c, over_ici, to_sibling, local = parts(ins, outs, sems)
        for w in range(n):
            for k in range(3):
                over_ici(w, k, False).wait_recv()
                to_sibling(w, k, c).start()
        for w in range(n):
            for k in range(3):
                to_sibling(w, k, 1 - c).wait_recv()
        for w in range(n):
            for k in range(3):
                over_ici(w, k, True).wait_send()
                to_sibling(w, k, c).wait_send()
        for cp in local:
            cp.wait()

    scratch = [pltpu.SemaphoreType.DMA((3 * n,))] * 4 + [pltpu.SemaphoreType.DMA((n,))]
    return _Side(shards, [jax.ShapeDtypeStruct((N_CHIPS,) + a.shape, a.dtype) for a in shards], scratch, start, finish)


def _comm_call(name, side):
    n, n_out = len(side.arrays), len(side.out_shapes)

    def body(*refs):
        ins, outs, sems = refs[:n], refs[n:n + n_out], refs[n + n_out:]
        side.start(ins, outs, sems)
        side.finish(ins, outs, sems)

    any_spec = pl.BlockSpec(memory_space=pl.ANY)
    return pl.pallas_call(body, name=name, out_shape=side.out_shapes, in_specs=[any_spec] * n,
                          out_specs=[any_spec] * n_out, scratch_shapes=side.scratch)(*side.arrays)


def _sibling_exchange(name, arrays):
    n = len(arrays)

    def body(*refs):
        ins, outs = refs[:n], refs[n:2 * n]
        send, recv = refs[2 * n:]
        x, y, c = _here()
        cps = [pltpu.make_async_remote_copy(src_ref=ins[w], dst_ref=outs[w], send_sem=send.at[w], recv_sem=recv.at[w],
                                            device_id=(x, y, 1 - c), device_id_type=MESH_ID) for w in range(n)]
        for cp in cps:
            cp.start()
        for cp in cps:
            cp.wait()

    any_spec = pl.BlockSpec(memory_space=pl.ANY)
    return pl.pallas_call(
        body, name=name, out_shape=[jax.ShapeDtypeStruct(a.shape, a.dtype) for a in arrays],
        in_specs=[any_spec] * n, out_specs=[any_spec] * n,
        scratch_shapes=[pltpu.SemaphoreType.DMA((n,)), pltpu.SemaphoreType.DMA((n,))],
    )(*arrays)


def _pack(arrs):
    flat = jnp.concatenate([a.reshape(-1).astype(F32) for a in arrs])
    n = flat.shape[0]
    rows = -(-n // (8 * LANES)) * 8
    return jnp.pad(flat, (0, rows * LANES - n)).reshape(rows, LANES)


def _unpack(buf, shapes):
    lead = buf.shape[:-2]
    flat = buf.reshape(lead + (-1,))
    out, off = [], 0
    for s in shapes:
        n = 1
        for d in s:
            n *= d
        out.append(flat[..., off:off + n].reshape(lead + tuple(s)))
        off += n
    return out


def _pad_w_in(w):
    return jnp.concatenate([w[:, :2048], w[:, 2816:2824], jnp.zeros((w.shape[0], P_XBC - P_DT - 8), w.dtype),
                            w[:, 2048:2816], w[:, 2824:]], axis=1)


def _unpad_w_in(g):
    return jnp.concatenate([g[:, :P_DT], g[:, P_XBC:P_G], g[:, P_DT:P_DT + 8], g[:, P_G:]], axis=1)


FFN_HALF = FFN_HIDDEN // 2


def _ffn_in_cols(w):
    h = FFN_HALF
    return jnp.concatenate([w[:, :h], w[:, 2 * h:3 * h], w[:, h:2 * h], w[:, 3 * h:]], axis=1)


def _row(v):
    return v.reshape(1, -1)


BIG = (("w_in", 2), ("w_sc_out", 2), ("w_sb_out", 2), ("w_ssm_out", 2), ("w_o", 1), ("w_ffn_in", 2), ("w_ffn_out", 1))
SMALL = ("mod_b", "g_pre_mix", "g_post_mix", "g_pre_ffn", "g_post_ffn", "sc_conv_w", "ssm_conv_w", "ssm_conv_b",
         "ssm_dt_bias", "ssm_a_log", "ssm_d", "ssm_norm_w")
WEIGHT_ORDER = ("mod_w", "mod_b", "g_pre_mix", "g_post_mix", "g_pre_ffn", "g_post_ffn", "w_in", "sc_conv_w",
                "ssm_conv_w", "ssm_conv_b", "ssm_dt_bias", "ssm_a_log", "ssm_d", "ssm_norm_w", "w_sc_out", "w_sb_out",
                "w_ssm_out", "w_o", "w_ffn_in", "w_ffn_out")


def _layer_fwd(l, x_in, h, W, V, side=None):
    S = {"x_in": x_in, "h": h}
    arrived = None
    P = _mm(f"in_proj{l}", h, W["w_in"], "nn", BF16, tn_cap=1024, side=side)
    if side:
        P, arrived = P[0], P[1:]
    S["P"] = P
    S["ya"] = _shortconv_fwd(P, V["sc_w"])
    S["yb"], S["yb32"] = _sb_fwd(P)
    S["pre"] = _ssmconv_fwd(P, V["ssm_w"], V["ssm_b"])
    S["y_ssd"], S["states"] = _ssd_fwd(S["pre"], P, V["dtb"], V["alog"])
    S["yc"] = _rowwise(f"ssm_post{l}", lambda y, px, z, d, nw: _ssm_post(y, px, _f(z), d, nw),
                       [(S["y_ssd"], SSM_INNER, 0), (S["pre"], SSM_INNER, 0), (P, SSM_INNER, P_Z // SSM_INNER)],
                       [V["d_full"], V["norm_w"]], [(SSM_INNER, BF16)])[0]
    S["merged"] = _merge_fwd(f"merge{l}", P, [S["ya"], S["yb"], S["yc"]],
                             [W["w_sc_out"], W["w_sb_out"], W["w_ssm_out"]])
    S["mix"] = _mm(f"w_o{l}", S["merged"], W["w_o"], "nn", BF16)
    S["x1"], S["h2"] = _mid_fwd(f"mid_mix{l}", x_in, S["mix"], V["mid_mix"])
    S["GU"], S["act"] = _mm_epi(f"ffn_in{l}", S["h2"], W["w_ffn_in"], "nn", 2 * FFN_HALF, [],
                                lambda p: (p, _swiglu(p[:, :FFN_HALF], p[:, FFN_HALF:])),
                                [(2 * FFN_HALF, BF16), (FFN_HALF, BF16)])
    S["f"] = _mm(f"ffn_out{l}", S["act"], W["w_ffn_out"], "nn", BF16)
    return S, arrived


BRANCH_WIDTHS = (SC_WIDTH, 256, SSM_INNER)


def _branch_specs(tm):
    gb = P_G // D_MODEL
    gates = [pl.BlockSpec((tm, D_MODEL), functools.partial(lambda i, cb: (i, cb), cb=gb + k)) for k in range(3)]
    ys = [pl.BlockSpec((tm, w), lambda i: (i, 0)) for w in BRANCH_WIDTHS]
    ws = [pl.BlockSpec((w, D_MODEL), lambda i: (0, 0)) for w in BRANCH_WIDTHS]
    return gates, ys, ws


def _merge_fwd(name, P, ys, ws, tm=512):
    L = P.shape[0]
    tm = min(tm, L)
    gates, y_specs, w_specs = _branch_specs(tm)

    def body(ga, gb, gc, ya, yb, yc, wa, wb, wc, o_ref):
        acc = None
        for g_ref, y_ref, w_ref in ((ga, ya, wa), (gb, yb, wb), (gc, yc, wc)):
            t = jax.nn.sigmoid(_f(g_ref[...])) * jnp.dot(y_ref[...], w_ref[...], preferred_element_type=F32)
            acc = t if acc is None else acc + t
        o_ref[...] = acc.astype(o_ref.dtype)

    return pl.pallas_call(
        body, name=name, grid=(L // tm,), in_specs=gates + y_specs + w_specs,
        out_specs=pl.BlockSpec((tm, D_MODEL), lambda i: (i, 0)), out_shape=jax.ShapeDtypeStruct((L, D_MODEL), BF16),
        compiler_params=_params(("arbitrary",)),
    )(P, P, P, *ys, *ws)


def _merge_bwd(name, P, ys, ws, dmerged, tm=512):
    L = P.shape[0]
    tm = min(tm, L)
    gates, y_specs, w_specs = _branch_specs(tm)

    def body(ga, gb, gc, ya, yb, yc, wa, wb, wc, dm_ref, dg_ref, dya, dyb, dyc, gwa, gwb, gwc):
        @pl.when(pl.program_id(0) == 0)
        def _():
            for r in (gwa, gwb, gwc):
                r[...] = jnp.zeros_like(r)

        dm = _f(dm_ref[...])
        for k, (g_ref, y_ref, w_ref, dy_ref, gw_ref) in enumerate(
                ((ga, ya, wa, dya, gwa), (gb, yb, wb, dyb, gwb), (gc, yc, wc, dyc, gwc))):
            y, w = y_ref[...], w_ref[...]
            s = jax.nn.sigmoid(_f(g_ref[...]))
            proj = jnp.dot(y, w, preferred_element_type=F32)
            d_proj = (dm * s).astype(BF16)
            dg_ref[:, k * D_MODEL:(k + 1) * D_MODEL] = (dm * proj * s * (1.0 - s)).astype(dg_ref.dtype)
            dy_ref[...] = _dot_nt(d_proj, w).astype(dy_ref.dtype)
            gw_ref[...] += _dot_tn(y, d_proj)

    gate_cols = pl.BlockSpec((tm, P_WIDTH - P_G), lambda i: (i, P_G // (P_WIDTH - P_G)))
    return pl.pallas_call(
        body, name=name, grid=(L // tm,),
        in_specs=gates + y_specs + w_specs + [pl.BlockSpec((tm, D_MODEL), lambda i: (i, 0))],
        out_specs=[gate_cols] + y_specs + w_specs,
        out_shape=[jax.ShapeDtypeStruct((L, P_WIDTH), BF16)] + [jax.ShapeDtypeStruct((L, w), BF16) for w in BRANCH_WIDTHS]
        + [jax.ShapeDtypeStruct((w, D_MODEL), F32) for w in BRANCH_WIDTHS],
        compiler_params=_params(("arbitrary",)),
    )(P, P, P, *ys, *ws, dmerged)


def _assemble_dp(name, dP, parts, tl=256):
    L = dP.shape[0]
    tl = min(tl, L)
    n = len(parts)

    def body(*refs):
        o_ref = refs[n + 1]
        o_ref[...] = jnp.concatenate([r[...].astype(o_ref.dtype) for r in refs[:n]], axis=1)

    return pl.pallas_call(
        body, name=name, grid=(L // tl,),
        in_specs=[pl.BlockSpec((tl, a.shape[1]), lambda i: (i, 0)) for a in parts] + [pl.BlockSpec(memory_space=pl.ANY)],
        out_specs=pl.BlockSpec((tl, P_G), lambda i: (i, 0)), out_shape=jax.ShapeDtypeStruct(dP.shape, dP.dtype),
        input_output_aliases={n: 0}, compiler_params=_params(("arbitrary",)),
    )(*parts, dP)


def _layer_bwd(l, S, W, V, dx1, df, sides, landed):
    G = {}
    P = S["P"]

    def mm(key, *args, **kw):
        if key not in sides:
            return _mm(f"{key}{l}", *args, **kw)
        names, layer, make = sides[key]
        res = _mm(f"{key}{l}", *args, side=make(G), **kw)
        for n, a in zip(names, res[1:]):
            landed[(n, layer)] = a
        return res[0]

    G["w_ffn_out"] = _mm(f"gw_ffn_out{l}", S["act"], df, "tn", F32)

    def swiglu_bwd(d_act, gu):
        _, vjp = jax.vjp(_swiglu, _f(gu[:, :FFN_HALF]), _f(gu[:, FFN_HALF:]))
        return (jnp.concatenate(vjp(d_act), axis=1),)

    dGU = _mm_epi(f"d_gu{l}", df, W["w_ffn_out"], "nt", FFN_HALF, [(S["GU"], 2 * FFN_HALF)], swiglu_bwd,
                  [(2 * FFN_HALF, BF16)])[0]
    dh2 = mm("d_h2", dGU, W["w_ffn_in"], "nt", BF16)
    G["w_ffn_in"] = _ffn_in_cols(mm("gw_ffn_in", S["h2"], dGU, "tn", F32))
    dx, dmix, G["gate1"], G["g_post_mix"], G["g_pre_ffn"], G["scale2"], G["shift2"] = _mid_bwd(
        f"mid_mix_bwd{l}", S["x_in"], S["mix"], dx1, dh2, V["mid_mix"])
    dmerged = _mm(f"d_merged{l}", dmix, W["w_o"], "nt", BF16)
    G["w_o"] = _mm(f"gw_o{l}", S["merged"], dmix, "tn", F32)

    dP, dya, dyb, dyc, G["w_sc_out"], G["w_sb_out"], G["w_ssm_out"] = _merge_bwd(
        f"merge_bwd{l}", P, [S["ya"], S["yb"], S["yc"]], [W["w_sc_out"], W["w_sb_out"], W["w_ssm_out"]], dmerged)

    def post_bwd(y, px, z, d, dfull, nw):
        _, vjp = jax.vjp(_ssm_post, y, px, _f(z), dfull, nw)
        return vjp(_f(d))

    dy_ssd, dxs, dz, G["d_full"], G["ssm_norm_w"] = _rowwise(
        f"ssm_post_bwd{l}", post_bwd,
        [(S["y_ssd"], SSM_INNER, 0), (S["pre"], SSM_INNER, 0), (P, SSM_INNER, P_Z // SSM_INNER), (dyc, SSM_INNER, 0)],
        [V["d_full"], V["norm_w"]], [(SSM_INNER, F32), (SSM_INNER, F32), (SSM_INNER, BF16)], [(1, SSM_INNER)] * 2)
    dpre, ddt, G["dtb"], G["alog"] = _ssd_bwd(S["pre"], P, S["states"], dy_ssd, dxs, V["dtb"], V["alog"])
    dxbc, w0, w1, w2, w3, G["ssm_conv_b"] = _ssmconv_bwd(P, dpre, V["ssm_w"])
    G["ssm_conv_w"] = jnp.concatenate([w0, w1, w2, w3], axis=0)
    dq, dk, dv = _sb_bwd(P, dyb, S["yb32"])
    dA, s0, s1, s2 = _shortconv_bwd(P, dya, V["sc_w"])
    G["sc_conv_w"] = jnp.concatenate([s0, s1, s2], axis=0)
    dP = _assemble_dp(f"assemble_dp{l}", dP, [dA, dq, dk, dv, dz, ddt, dxbc])
    dh = mm("d_h", dP, W["w_in"], "nt", BF16, tk_cap=2048)
    G["w_in"] = _mm(f"gw_in{l}", S["h"], dP, "tn", F32, tn_cap=1024)
    return dx, dh, G


def kernel(x, c, mod_w, mod_b, g_pre_mix, g_post_mix, g_pre_ffn, g_post_ffn, w_in, sc_conv_w, ssm_conv_w, ssm_conv_b, ssm_dt_bias, ssm_a_log, ssm_d, ssm_norm_w, w_sc_out, w_sb_out, w_ssm_out, w_o, w_ffn_in, w_ffn_out, loss_target, m_mod_w, m_mod_b, m_g_pre_mix, m_g_post_mix, m_g_pre_ffn, m_g_post_ffn, m_w_in, m_sc_conv_w, m_ssm_conv_w, m_ssm_conv_b, m_ssm_dt_bias, m_ssm_a_log, m_ssm_d, m_ssm_norm_w, m_w_sc_out, m_w_sb_out, m_w_ssm_out, m_w_o, m_w_ffn_in, m_w_ffn_out, v_mod_w, v_mod_b, v_g_pre_mix, v_g_post_mix, v_g_pre_ffn, v_g_post_ffn, v_w_in, v_sc_conv_w, v_ssm_conv_w, v_ssm_conv_b, v_ssm_dt_bias, v_ssm_a_log, v_ssm_d, v_ssm_norm_w, v_w_sc_out, v_w_sb_out, v_w_ssm_out, v_w_o, v_w_ffn_in, v_w_ffn_out):
    wts = dict(mod_w=mod_w, mod_b=mod_b, g_pre_mix=g_pre_mix, g_post_mix=g_post_mix, g_pre_ffn=g_pre_ffn,
               g_post_ffn=g_post_ffn, w_in=w_in, sc_conv_w=sc_conv_w, ssm_conv_w=ssm_conv_w, ssm_conv_b=ssm_conv_b,
               ssm_dt_bias=ssm_dt_bias, ssm_a_log=ssm_a_log, ssm_d=ssm_d, ssm_norm_w=ssm_norm_w, w_sc_out=w_sc_out,
               w_sb_out=w_sb_out, w_ssm_out=w_ssm_out, w_o=w_o, w_ffn_in=w_ffn_in, w_ffn_out=w_ffn_out)
    mom = dict(mod_w=m_mod_w, mod_b=m_mod_b, g_pre_mix=m_g_pre_mix, g_post_mix=m_g_post_mix, g_pre_ffn=m_g_pre_ffn,
               g_post_ffn=m_g_post_ffn, w_in=m_w_in, sc_conv_w=m_sc_conv_w, ssm_conv_w=m_ssm_conv_w,
               ssm_conv_b=m_ssm_conv_b, ssm_dt_bias=m_ssm_dt_bias, ssm_a_log=m_ssm_a_log, ssm_d=m_ssm_d,
               ssm_norm_w=m_ssm_norm_w, w_sc_out=m_w_sc_out, w_sb_out=m_w_sb_out, w_ssm_out=m_w_ssm_out, w_o=m_w_o,
               w_ffn_in=m_w_ffn_in, w_ffn_out=m_w_ffn_out)
    var = dict(mod_w=v_mod_w, mod_b=v_mod_b, g_pre_mix=v_g_pre_mix, g_post_mix=v_g_post_mix, g_pre_ffn=v_g_pre_ffn,
               g_post_ffn=v_g_post_ffn, w_in=v_w_in, sc_conv_w=v_sc_conv_w, ssm_conv_w=v_ssm_conv_w,
               ssm_conv_b=v_ssm_conv_b, ssm_dt_bias=v_ssm_dt_bias, ssm_a_log=v_ssm_a_log, ssm_d=v_ssm_d,
               ssm_norm_w=v_ssm_norm_w, w_sc_out=v_w_sc_out, w_sb_out=v_w_sb_out, w_ssm_out=v_w_ssm_out, w_o=v_w_o,
               w_ffn_in=v_w_ffn_in, w_ffn_out=v_w_ffn_out)
    xi, yi, ci = _here()
    chip = 2 * xi + yi
    me = 4 * xi + 2 * yi + ci
    x0, target = x[0], loss_target[0]

    first_shapes = [(D_MODEL,), sc_conv_w.shape, ssm_conv_w.shape]
    g0 = _allgather_small("gather_cond", _pack([c, sc_conv_w, ssm_conv_w]))
    c_rows, sc_sh, ssm_sh = _unpack(g0, first_shapes)
    c_all = c_rows
    sc_w = jnp.concatenate([sc_sh[2 * j] for j in range(N_CHIPS)], axis=-1)
    ssm_w = jnp.concatenate([ssm_sh[2 * j] for j in range(N_CHIPS)], axis=-1)

    mod_b_sh = lax.dynamic_slice_in_dim(mod_b, chip * MOD_SHARD, MOD_SHARD, axis=1).reshape(DEPTH, 1, MOD_SHARD)
    modpart = _mod_fwd(c_all, mod_w, mod_b_sh)
    g1 = _allgather_small("gather_mod", modpart.reshape(-1, LANES)).reshape(N_DEV, DEPTH, N_DEV, MOD_SHARD)
    mod = jnp.concatenate([lax.dynamic_index_in_dim(g1[2 * j], me, axis=1, keepdims=False) for j in range(N_CHIPS)],
                          axis=-1)

    def layer_shards(l):
        return [wts[n][l].astype(BF16) for n, _ in BIG]

    def full_weights(gathered):
        W = {n: jnp.concatenate([g[j] for j in range(N_CHIPS)], axis=ax - 1) for (n, ax), g in zip(BIG, gathered)}
        W["w_in"] = _pad_w_in(W["w_in"])
        W["w_ffn_in"] = _ffn_in_cols(W["w_ffn_in"])
        return W

    Ws = [full_weights(_comm_call("gather_weights0", _gather_side(layer_shards(0)))), None]
    Vs = []
    for l in range(DEPTH):
        sh1, sc1, gt1, sh2, sc2, gt2 = [_row(v) for v in jnp.split(mod[l], N_MOD)]
        Vs.append(dict(
            shift1=sh1, scale1=sc1, g_pre_mix=_row(g_pre_mix[l]),
            mid_mix=[gt1, _row(g_post_mix[l]), _row(g_pre_ffn[l]), sc2, sh2],
            gate2=gt2, g_post_ffn=_row(g_post_ffn[l]),
            sc_w=sc_w[l], ssm_w=ssm_w[l], ssm_b=_row(ssm_conv_b[l]),
            dtb=_row(jnp.pad(ssm_dt_bias[l], (0, LANES - SSM_HEADS))), alog=_row(jnp.pad(ssm_a_log[l], (0, LANES - SSM_HEADS))),
            d_full=_row(jnp.repeat(ssm_d[l], SSM_INNER // SSM_HEADS)), norm_w=_row(ssm_norm_w[l])))

    def mid_ffn_vecs(l):
        return [Vs[l]["gate2"], Vs[l]["g_post_ffn"], Vs[l + 1]["g_pre_mix"], Vs[l + 1]["scale1"], Vs[l + 1]["shift1"]]

    saved = []
    x_in = x0
    h = _first_fwd(x0, [Vs[0]["g_pre_mix"], Vs[0]["scale1"], Vs[0]["shift1"]])
    for l in range(DEPTH):
        S, arrived = _layer_fwd(l, x_in, h, Ws[l], Vs[l], _gather_side(layer_shards(l + 1)) if l + 1 < DEPTH else None)
        saved.append(S)
        if l + 1 < DEPTH:
            Ws[l + 1] = full_weights(arrived)
            x_in, h = _mid_fwd(f"mid_ffn{l}", S["x1"], S["f"], mid_ffn_vecs(l))

    def pieces(G, names):
        out = []
        for n, ax in BIG:
            if n in names:
                g = _unpad_w_in(G[n]) if n == "w_in" else G[n]
                out.append(jnp.stack(jnp.split(g, N_CHIPS, axis=ax - 1)).astype(BF16))
        return out

    ffn_names = ("w_ffn_in", "w_ffn_out")
    rest_names = tuple(n for n, _ in BIG if n not in ffn_names)
    late_names = tuple(n for n, _ in BIG if n != "w_in")
    landed = {}

    GL = [None] * DEPTH
    S = saved[-1]
    dx1, df, g_gate2, g_gpf, loss_cols = _last_bwd(S["x1"], S["f"], target, [Vs[-1]["gate2"], Vs[-1]["g_post_ffn"]])
    for l in reversed(range(DEPTH)):
        sides = {}
        if l + 1 < DEPTH:
            sides["d_h2"] = (ffn_names, l + 1, lambda G, up=GL[l + 1]: _scatter_side(pieces(up, ffn_names)))
            sides["gw_ffn_in"] = (rest_names, l + 1, lambda G, up=GL[l + 1]: _scatter_side(pieces(up, rest_names)))
        if l == 0:
            sides["d_h"] = (late_names, l, lambda G: _scatter_side(pieces(G, late_names)))
        dx, dh, G = _layer_bwd(l, saved[l], Ws[l], Vs[l], dx1, df, sides, landed)
        G["gate2"], G["g_post_ffn"] = g_gate2, g_gpf
        GL[l] = G
        if l > 0:
            Sp = saved[l - 1]
            dx1, df, g_gate2, g_gpf, G["g_pre_mix"], G["scale1"], G["shift1"] = _mid_bwd(
                f"mid_ffn_bwd{l - 1}", Sp["x1"], Sp["f"], dx, dh, mid_ffn_vecs(l - 1))
        else:
            grad_x, G["g_pre_mix"], G["scale1"], G["shift1"] = _first_bwd(
                x0, dx, dh, [Vs[0]["g_pre_mix"], Vs[0]["scale1"], Vs[0]["shift1"]])
    loss = lax.psum(jnp.sum(loss_cols), ("x", "y", "c"))

    def both(key, shape=None):
        a = jnp.stack([GL[l][key] for l in range(DEPTH)])
        return a if shape is None else a.reshape(shape)

    dmod = jnp.concatenate([both(k, (DEPTH, D_MODEL)) for k in ("shift1", "scale1", "gate1", "shift2", "scale2", "gate2")],
                           axis=1)
    part_small = dict(
        mod_b=dmod, g_pre_mix=both("g_pre_mix", (DEPTH, D_MODEL)), g_post_mix=both("g_post_mix", (DEPTH, D_MODEL)),
        g_pre_ffn=both("g_pre_ffn", (DEPTH, D_MODEL)), g_post_ffn=both("g_post_ffn", (DEPTH, D_MODEL)),
        sc_conv_w=both("sc_conv_w"), ssm_conv_w=both("ssm_conv_w"), ssm_conv_b=both("ssm_conv_b", (DEPTH, SSM_CONV_DIM)),
        ssm_dt_bias=both("dtb", (DEPTH, LANES))[:, :SSM_HEADS], ssm_a_log=both("alog", (DEPTH, LANES))[:, :SSM_HEADS],
        ssm_d=both("d_full", (DEPTH, SSM_HEADS, SSM_INNER // SSM_HEADS)).sum(-1),
        ssm_norm_w=both("ssm_norm_w", (DEPTH, SSM_INNER)))
    small_shapes = [part_small[n].shape for n in SMALL]
    g2 = _allgather_small("gather_small_grads", _pack([part_small[n] for n in SMALL]))
    tot = dict(zip(SMALL, _unpack(_sum_slots("sum_small_grads", g2), small_shapes)))
    dmod_all = _unpack(g2, small_shapes)[0]
    dmod_sh = jnp.swapaxes(lax.dynamic_slice_in_dim(dmod_all, chip * MOD_SHARD, MOD_SHARD, axis=2), 0, 1)
    grads = {"mod_w": _mod_bwd(c_all, dmod_sh)}
    for n in SMALL:
        grads[n] = tot[n]
    grads["sc_conv_w"] = lax.dynamic_slice_in_dim(tot["sc_conv_w"], chip * 64, 64, axis=2)
    grads["ssm_conv_w"] = lax.dynamic_slice_in_dim(tot["ssm_conv_w"], chip * 192, 192, axis=2)

    landed[("w_in", 0)] = _comm_call("scatter_w_in0", _scatter_side(pieces(GL[0], ("w_in",))))[0]
    keys = [(n, l) for n, _ in BIG for l in range(DEPTH)]
    mine = [_sum_slots(f"sum_{n}{l}", landed[(n, l)]) for n, l in keys]
    theirs = dict(zip(keys, _sibling_exchange("swap_core_sums", mine)))
    mine = dict(zip(keys, mine))

    out = {}

    def update(name, w2, gs, m2, v2, shape):
        g, d, nm, nv = _adamw(f"adamw_{name}", w2, gs, m2, v2)
        out[name] = tuple(a.reshape(shape) for a in (g, d, nm, nv))

    for n, _ in BIG:
        shp = wts[n].shape
        two = (-1, shp[-1])
        by_layer = [tuple(src[(n, l)] for l in range(DEPTH)) for src in (mine, theirs)]
        update(n, wts[n].reshape(two), by_layer, mom[n].reshape(two), var[n].reshape(two), shp)
    two = (-1, MOD_SHARD)
    update("mod_w", mod_w.reshape(two), [grads["mod_w"].reshape(two)], m_mod_w.reshape(two), v_mod_w.reshape(two), mod_w.shape)
    shapes = [wts[n].shape for n in SMALL]
    res = _adamw("adamw_small", _pack([wts[n] for n in SMALL]), [_pack([grads[n] for n in SMALL])],
                 _pack([mom[n] for n in SMALL]), _pack([var[n] for n in SMALL]))
    for n, g, d, nm, nv in zip(SMALL, *[_unpack(r, shapes) for r in res]):
        out[n] = (g, d, nm, nv)

    result = [loss, grad_x[None]]
    for k in range(4):
        result += [out[n][k] for n in WEIGHT_ORDER]
    return tuple(result)
```

```python
import functools

import jax
import jax.numpy as jnp
from jax import lax
from jax.experimental import pallas as pl
from jax.experimental.pallas import tpu as pltpu

F32 = jnp.float32
BF16 = jnp.bfloat16
HIGHEST = lax.Precision.HIGHEST
MESH_ID = pl.DeviceIdType.MESH

D_MODEL = 1024
DEPTH = 2
SC_WIDTH = 256
SC_KERNEL = 3
SB_HEAD_DIM = 64
SSM_INNER = 512
SSM_HEADS = 8
SSM_STATE = 64
SSM_CONV = 4
SSM_CHUNK = 256
SSM_CONV_DIM = 768
FFN_HIDDEN = 2816
NORM_EPS = 1e-6
N_MOD = 6
N_CHIPS = 4
N_DEV = 8

ADAM_LR = 0.001
ADAM_B1 = 0.9
ADAM_B2 = 0.999
ADAM_EPS = 1e-08
ADAM_WD = 0.01
ADAM_STEP = 10

P_WIDTH = 6144
P_A, P_B, P_Z, P_DT, P_XBC, P_G = 0, 768, 1536, 2048, 2304, 3072
DT_PAD = 256

VMEM_LIMIT_BYTES = 56 * 1024 * 1024
LANES = 128

SB_LOG_CUTOFF = -105.0
SB_TQ = 256
SB_TK = 256


def _params(sem):
    return pltpu.CompilerParams(dimension_semantics=sem, vmem_limit_bytes=VMEM_LIMIT_BYTES)


def _pick(n, cap):
    if n <= cap:
        return n
    best = None
    for m in range(LANES, cap + 1, LANES):
        if n % m == 0:
            best = m
    assert best is not None, (n, cap)
    return best


def _rowwise(name, fn, rows, vecs, row_outs, acc_outs=(), tl=256):
    L = rows[0][0].shape[0]
    tl = min(tl, L)
    assert L % tl == 0
    n_in = len(rows) + len(vecs)
    n_ro = len(row_outs)

    def body(*refs):
        ins, ro, ao = refs[:n_in], refs[n_in:n_in + n_ro], refs[n_in + n_ro:]
        vals = fn(*[r[...] for r in ins])
        if not isinstance(vals, (tuple, list)):
            vals = (vals,)
        for o, v in zip(ro, vals[:n_ro]):
            o[...] = v.astype(o.dtype)
        if ao:
            @pl.when(pl.program_id(0) == 0)
            def _():
                for o in ao:
                    o[...] = jnp.zeros_like(o)
            for o, v in zip(ao, vals[n_ro:]):
                o[...] += v.astype(F32)

    in_specs = [pl.BlockSpec((tl, w), functools.partial(lambda i, cb: (i, cb), cb=cb)) for _, w, cb in rows]
    in_specs += [pl.BlockSpec(v.shape, lambda i: (0, 0)) for v in vecs]
    out_specs = [pl.BlockSpec((tl, w), lambda i: (i, 0)) for w, _ in row_outs]
    out_specs += [pl.BlockSpec(s, lambda i: (0, 0)) for s in acc_outs]
    out_shape = [jax.ShapeDtypeStruct((L, w), dt) for w, dt in row_outs]
    out_shape += [jax.ShapeDtypeStruct(s, F32) for s in acc_outs]
    return pl.pallas_call(
        body, name=name, grid=(L // tl,), in_specs=in_specs, out_specs=out_specs, out_shape=out_shape,
        compiler_params=_params(("arbitrary",)),
    )(*[a for a, _, _ in rows], *vecs)


def _mm(name, a, b, mode, out_dtype, tm=1024, tn_cap=1408, tk_cap=2816, side=None):
    if mode == "nn":
        (M, K), (_, N) = a.shape, b.shape
    elif mode == "nt":
        (M, K), (N, _) = a.shape, b.shape
    else:
        (K, M), (_, N) = a.shape, b.shape
        tm, tk_cap = 1408, 2048
    tm = _pick(M, tm)
    tn = _pick(N, tn_cap)
    tk = _pick(K, tk_cap)
    nk = K // tk
    grid = (M // tm, N // tn, nk)
    n_si = len(side.arrays) if side else 0
    n_so = len(side.out_shapes) if side else 0
    n_acc = 1 if nk > 1 else 0

    def body(a_ref, b_ref, *rest):
        s_in, o_ref, s_out = rest[:n_si], rest[n_si], rest[n_si + 1:n_si + 1 + n_so]
        scr = rest[n_si + 1 + n_so:]
        if side:
            at = [pl.program_id(d) for d in range(3)]
            is_first = jnp.logical_and(jnp.logical_and(at[0] == 0, at[1] == 0), at[2] == 0)
            is_last = jnp.logical_and(jnp.logical_and(at[0] == grid[0] - 1, at[1] == grid[1] - 1), at[2] == grid[2] - 1)

            @pl.when(is_first)
            def _():
                side.start(s_in, s_out, scr[n_acc:])

        _product(a_ref, b_ref, o_ref, scr)
        if side:
            @pl.when(is_last)
            def _():
                side.finish(s_in, s_out, scr[n_acc:])

    def _product(a_ref, b_ref, o_ref, scr):
        if mode == "nn":
            p = jnp.dot(a_ref[...], b_ref[...], preferred_element_type=F32)
        elif mode == "nt":
            p = lax.dot_general(a_ref[...], b_ref[...], (((1,), (1,)), ((), ())), preferred_element_type=F32)
        else:
            p = lax.dot_general(a_ref[...], b_ref[...], (((0,), (0,)), ((), ())), preferred_element_type=F32)
        if nk == 1:
            o_ref[...] = p.astype(o_ref.dtype)
        else:
            acc = scr[0]
            k = pl.program_id(2)

            @pl.when(k == 0)
            def _():
                acc[...] = p

            @pl.when(k > 0)
            def _():
                acc[...] += p

            @pl.when(k == nk - 1)
            def _():
                o_ref[...] = acc[...].astype(o_ref.dtype)

    if mode == "nn":
        a_spec = pl.BlockSpec((tm, tk), lambda i, j, k: (i, k))
        b_spec = pl.BlockSpec((tk, tn), lambda i, j, k: (k, j))
    elif mode == "nt":
        a_spec = pl.BlockSpec((tm, tk), lambda i, j, k: (i, k))
        b_spec = pl.BlockSpec((tn, tk), lambda i, j, k: (j, k))
    else:
        a_spec = pl.BlockSpec((tk, tm), lambda i, j, k: (k, i))
        b_spec = pl.BlockSpec((tk, tn), lambda i, j, k: (k, j))
    any_spec = pl.BlockSpec(memory_space=pl.ANY)
    res = pl.pallas_call(
        body, name=name, grid=grid, in_specs=[a_spec, b_spec] + [any_spec] * n_si,
        out_specs=[pl.BlockSpec((tm, tn), lambda i, j, k: (i, j))] + [any_spec] * n_so,
        out_shape=[jax.ShapeDtypeStruct((M, N), out_dtype)] + (side.out_shapes if side else []),
        scratch_shapes=([pltpu.VMEM((tm, tn), F32)] if nk > 1 else []) + (side.scratch if side else []),
        compiler_params=_params(("arbitrary", "arbitrary", "arbitrary")),
    )(a, b, *(side.arrays if side else []))
    return res if side else res[0]


def _mm_epi(name, a, b, mode, tn, extras, epi, outs, tm=512, side=None):
    if mode == "nn":
        (M, K), (_, N) = a.shape, b.shape
    else:
        (M, K), (N, _) = a.shape, b.shape
    tm = _pick(M, tm)
    grid = (M // tm, N // tn)
    n_ex, n_out = len(extras), len(outs)
    n_si = len(side.arrays) if side else 0
    n_so = len(side.out_shapes) if side else 0

    def body(*refs):
        a_ref, b_ref, ex = refs[0], refs[1], refs[2:2 + n_ex]
        s_in = refs[2 + n_ex:2 + n_ex + n_si]
        o_refs = refs[2 + n_ex + n_si:2 + n_ex + n_si + n_out]
        s_out = refs[2 + n_ex + n_si + n_out:2 + n_ex + n_si + n_out + n_so]
        sems = refs[2 + n_ex + n_si + n_out + n_so:]
        if side:
            @pl.when(jnp.logical_and(pl.program_id(0) == 0, pl.program_id(1) == 0))
            def _():
                side.start(s_in, s_out, sems)

        if mode == "nn":
            p = jnp.dot(a_ref[...], b_ref[...], preferred_element_type=F32)
        else:
            p = lax.dot_general(a_ref[...], b_ref[...], (((1,), (1,)), ((), ())), preferred_element_type=F32)
        for o, v in zip(o_refs, epi(p, *[r[...] for r in ex])):
            o[...] = v.astype(o.dtype)
        if side:
            @pl.when(jnp.logical_and(pl.program_id(0) == grid[0] - 1, pl.program_id(1) == grid[1] - 1))
            def _():
                side.finish(s_in, s_out, sems)

    any_spec = pl.BlockSpec(memory_space=pl.ANY)
    a_spec = pl.BlockSpec((tm, K), lambda i, j: (i, 0))
    b_spec = pl.BlockSpec((K, tn), lambda i, j: (0, j)) if mode == "nn" else pl.BlockSpec((tn, K), lambda i, j: (j, 0))
    return pl.pallas_call(
        body, name=name, grid=grid,
        in_specs=[a_spec, b_spec]
        + [pl.BlockSpec(e.shape, lambda i, j: (0, 0)) if w is None else pl.BlockSpec((tm, w), lambda i, j: (i, j))
           for e, w in extras] + [any_spec] * n_si,
        out_specs=[pl.BlockSpec((tm, w), lambda i, j: (i, j)) for w, _ in outs] + [any_spec] * n_so,
        out_shape=[jax.ShapeDtypeStruct((M, (N // tn) * w), dt) for w, dt in outs] + (side.out_shapes if side else []),
        scratch_shapes=side.scratch if side else [],
        compiler_params=_params(("arbitrary", "arbitrary")),
    )(a, b, *[e for e, _ in extras], *(side.arrays if side else []))


def _f(x):
    return x.astype(F32)


def _silu(x):
    return x * jax.nn.sigmoid(x)


def _softplus(x):
    return jnp.maximum(x, 0.0) + jnp.log1p(jnp.exp(-jnp.abs(x)))


def _rms(x, g):
    r = lax.rsqrt(jnp.mean(x * x, axis=-1, keepdims=True) + NORM_EPS)
    return x * r * g


def _adaln(x, g, scale, shift):
    return _rms(x, g) * (1.0 + scale) + shift


def _resid(x, y, gate, g):
    return x + gate * _rms(y, g)


def _mid(x, y, gate, g_post, g_pre, scale, shift):
    x_new = _resid(x, y, gate, g_post)
    return x_new, _adaln(x_new, g_pre, scale, shift)


def _swiglu(gt, up):
    return _silu(gt) * up


def _ssm_post(y_ssd, pre_xs, z, d_full, norm_w):
    y = (y_ssd + _silu(pre_xs) * d_full) * _silu(z)
    half = SSM_INNER // 2
    parts = []
    for g in range(2):
        yg = y[:, g * half:(g + 1) * half]
        parts.append(yg * lax.rsqrt(jnp.mean(yg * yg, axis=-1, keepdims=True) + NORM_EPS))
    return jnp.concatenate(parts, axis=1) * norm_w


def _first_fwd(x, vecs):
    return _rowwise("adaln_first", lambda x, g, sc, sh: _adaln(x, g, sc, sh),
                    [(x, D_MODEL, 0)], vecs, [(D_MODEL, BF16)], tl=512)[0]


def _mid_bwd(name, x, y, dx_new, dh, vecs):
    def fn(x, y, dxn, dh, *v):
        _, vjp = jax.vjp(_mid, x, _f(y), *v)
        return vjp((dxn, _f(dh)))

    vec = (1, D_MODEL)
    return _rowwise(name, fn, [(x, D_MODEL, 0), (y, D_MODEL, 0), (dx_new, D_MODEL, 0), (dh, D_MODEL, 0)], vecs,
                    [(D_MODEL, F32), (D_MODEL, BF16)], [vec] * 5)


def _first_bwd(x, dx_in, dh, vecs):
    def fn(x, dxi, dh, *v):
        _, vjp = jax.vjp(_adaln, x, *v)
        dx, dg, dsc, dsh = vjp(_f(dh))
        return dx + dxi, dg, dsc, dsh

    vec = (1, D_MODEL)
    return _rowwise("adaln_first_bwd", fn, [(x, D_MODEL, 0), (dx_in, D_MODEL, 0), (dh, D_MODEL, 0)], vecs,
                    [(D_MODEL, F32)], [vec] * 3)


def _last_bwd(x1, f, target, vecs):
    def fn(x1, f, t, gate, g):
        x2, vjp = jax.vjp(_resid, x1, _f(f), gate, g)
        err = x2 - t
        dx1, df, dgate, dg = vjp(err * (1.0 / D_MODEL))
        loss_cols = jnp.sum(err * err, axis=0, keepdims=True) * (0.5 / D_MODEL)
        return dx1, df, dgate, dg, loss_cols

    vec = (1, D_MODEL)
    return _rowwise("loss_last_bwd", fn, [(x1, D_MODEL, 0), (f, D_MODEL, 0), (target, D_MODEL, 0)], vecs,
                    [(D_MODEL, F32), (D_MODEL, BF16)], [vec] * 3)


HALO = 16


def _shift_down(u, prev, k):
    rows = lax.broadcasted_iota(jnp.int32, u.shape, 0)
    v = pltpu.roll(u, k, 0)
    for t in range(k):
        v = jnp.where(rows == t, prev[HALO - k + t:HALO - k + t + 1, :], v)
    return v


def _shift_up(u, nxt, k):
    n = u.shape[0]
    rows = lax.broadcasted_iota(jnp.int32, u.shape, 0)
    v = pltpu.roll(u, n - k, 0)
    for t in range(k):
        v = jnp.where(rows == n - k + t, nxt[t:t + 1, :], v)
    return v


def _conv_specs(L, tl, width, col_block):
    per = tl // HALO
    last = L // HALO - 1
    main = pl.BlockSpec((tl, width), lambda i: (i, col_block))
    before = pl.BlockSpec((HALO, width), lambda i: (jnp.maximum(i * per - 1, 0), col_block))
    after = pl.BlockSpec((HALO, width), lambda i: (jnp.minimum((i + 1) * per, last), col_block))
    return main, before, after


def _shortconv_fwd(P, w, tl=512):
    L = P.shape[0]
    tl = min(tl, L)
    C = SC_WIDTH
    main, before, _ = _conv_specs(L, tl, 3 * C, 0)

    def body(p_ref, h_ref, w_ref, o_ref):
        first = (pl.program_id(0) == 0)
        p, h = _f(p_ref[...]), _f(h_ref[...])
        b, u = p[:, :C], p[:, C:2 * C] * p[:, 2 * C:]
        uh = jnp.where(first, 0.0, h[:, C:2 * C] * h[:, 2 * C:])
        wv = w_ref[...]
        cv = wv[2:3] * u + wv[1:2] * _shift_down(u, uh, 1) + wv[0:1] * _shift_down(u, uh, 2)
        o_ref[...] = (b * cv).astype(o_ref.dtype)

    return pl.pallas_call(
        body, name="shortconv_fwd", grid=(L // tl,),
        in_specs=[main, before, pl.BlockSpec(w.shape, lambda i: (0, 0))],
        out_specs=pl.BlockSpec((tl, C), lambda i: (i, 0)),
        out_shape=jax.ShapeDtypeStruct((L, C), BF16), compiler_params=_params(("arbitrary",)),
    )(P, P, w)


def _shortconv_bwd(P, dya, w, tl=512):
    L = P.shape[0]
    tl = min(tl, L)
    C = SC_WIDTH
    main, before, after = _conv_specs(L, tl, 3 * C, 0)
    dmain, _, dafter = _conv_specs(L, tl, C, 0)
    n = L // tl

    def body(p_ref, h_ref, n_ref, d_ref, dn_ref, w_ref, o_ref, dw0, dw1, dw2):
        i = pl.program_id(0)
        p, h, nx = _f(p_ref[...]), _f(h_ref[...]), _f(n_ref[...])
        b, c, x = p[:, :C], p[:, C:2 * C], p[:, 2 * C:]
        u = c * x
        uh = jnp.where(i == 0, 0.0, h[:, C:2 * C] * h[:, 2 * C:])
        u1, u2 = _shift_down(u, uh, 1), _shift_down(u, uh, 2)
        wv = w_ref[...]
        cv = wv[2:3] * u + wv[1:2] * u1 + wv[0:1] * u2
        dy = _f(d_ref[...])
        dcv = dy * b
        dcv_n = jnp.where(i == n - 1, 0.0, _f(dn_ref[...]) * nx[:, :C])
        du = wv[2:3] * dcv + wv[1:2] * _shift_up(dcv, dcv_n, 1) + wv[0:1] * _shift_up(dcv, dcv_n, 2)
        o_ref[:, :C] = (dy * cv).astype(o_ref.dtype)
        o_ref[:, C:2 * C] = (du * x).astype(o_ref.dtype)
        o_ref[:, 2 * C:] = (du * c).astype(o_ref.dtype)

        @pl.when(i == 0)
        def _():
            for r in (dw0, dw1, dw2):
                r[...] = jnp.zeros_like(r)

        dw0[...] += jnp.sum(dcv * u2, axis=0, keepdims=True)
        dw1[...] += jnp.sum(dcv * u1, axis=0, keepdims=True)
        dw2[...] += jnp.sum(dcv * u, axis=0, keepdims=True)

    vec = pl.BlockSpec((1, C), lambda i: (0, 0))
    return pl.pallas_call(
        body, name="shortconv_bwd", grid=(n,),
        in_specs=[main, before, after, dmain, dafter, pl.BlockSpec(w.shape, lambda i: (0, 0))],
        out_specs=[pl.BlockSpec((tl, 3 * C), lambda i: (i, 0)), vec, vec, vec],
        out_shape=[jax.ShapeDtypeStruct((L, 3 * C), BF16)] + [jax.ShapeDtypeStruct((1, C), F32)] * 3,
        compiler_params=_params(("arbitrary",)),
    )(P, P, P, dya, dya, w)


def _ssmconv_fwd(P, w, bias, tl=512):
    L = P.shape[0]
    tl = min(tl, L)
    C = SSM_CONV_DIM
    main, before, _ = _conv_specs(L, tl, C, P_XBC // C)

    def body(p_ref, h_ref, w_ref, b_ref, o_ref):
        u = _f(p_ref[...])
        uh = jnp.where(pl.program_id(0) == 0, 0.0, _f(h_ref[...]))
        wv = w_ref[...]
        acc = wv[3:4] * u + b_ref[...]
        for k in range(1, SSM_CONV):
            acc = acc + wv[3 - k:4 - k] * _shift_down(u, uh, k)
        o_ref[...] = acc

    return pl.pallas_call(
        body, name="ssmconv_fwd", grid=(L // tl,),
        in_specs=[main, before, pl.BlockSpec(w.shape, lambda i: (0, 0)), pl.BlockSpec(bias.shape, lambda i: (0, 0))],
        out_specs=pl.BlockSpec((tl, C), lambda i: (i, 0)),
        out_shape=jax.ShapeDtypeStruct((L, C), F32), compiler_params=_params(("arbitrary",)),
    )(P, P, w, bias)


def _ssmconv_bwd(P, dpre, w, tl=512):
    L = P.shape[0]
    tl = min(tl, L)
    C = SSM_CONV_DIM
    main, before, _ = _conv_specs(L, tl, C, P_XBC // C)
    dmain, _, dafter = _conv_specs(L, tl, C, 0)
    n = L // tl

    def body(p_ref, h_ref, d_ref, dn_ref, w_ref, o_ref, dw0, dw1, dw2, dw3, db):
        i = pl.program_id(0)
        u = _f(p_ref[...])
        uh = jnp.where(i == 0, 0.0, _f(h_ref[...]))
        d = d_ref[...]
        dn = jnp.where(i == n - 1, 0.0, dn_ref[...])
        wv = w_ref[...]
        du = wv[3:4] * d
        for k in range(1, SSM_CONV):
            du = du + wv[3 - k:4 - k] * _shift_up(d, dn, k)
        o_ref[...] = du.astype(o_ref.dtype)

        @pl.when(i == 0)
        def _():
            for r in (dw0, dw1, dw2, dw3, db):
                r[...] = jnp.zeros_like(r)

        for k, r in ((3, dw0), (2, dw1), (1, dw2)):
            r[...] += jnp.sum(d * _shift_down(u, uh, k), axis=0, keepdims=True)
        dw3[...] += jnp.sum(d * u, axis=0, keepdims=True)
        db[...] += jnp.sum(d, axis=0, keepdims=True)

    vec = pl.BlockSpec((1, C), lambda i: (0, 0))
    return pl.pallas_call(
        body, name="ssmconv_bwd", grid=(n,),
        in_specs=[main, before, dmain, dafter, pl.BlockSpec(w.shape, lambda i: (0, 0))],
        out_specs=[pl.BlockSpec((tl, C), lambda i: (i, 0))] + [vec] * 5,
        out_shape=[jax.ShapeDtypeStruct((L, C), BF16)] + [jax.ShapeDtypeStruct((1, C), F32)] * 5,
        compiler_params=_params(("arbitrary",)),
    )(P, P, dpre, dpre, w)


def _dot_nt(a, b):
    return lax.dot_general(a, b, (((1,), (1,)), ((), ())), preferred_element_type=F32)


def _dot_tn(a, b):
    return lax.dot_general(a, b, (((0,), (0,)), ((), ())), preferred_element_type=F32)


def _split3(x):
    hi = x.astype(BF16)
    r = x - hi.astype(F32)
    mid = r.astype(BF16)
    return hi, mid, (r - mid.astype(F32)).astype(BF16)


@jax.custom_vjp
def _xm01(x, m):
    return sum(jnp.dot(t, m, preferred_element_type=F32) for t in _split3(x))


def _xm01_fwd(x, m):
    return _xm01(x, m), m


def _xm01_bwd(m, g):
    return sum(_dot_nt(t, m) for t in _split3(g)), jnp.zeros_like(m)


_xm01.defvjp(_xm01_fwd, _xm01_bwd)


@jax.custom_vjp
def _m01x(m, x):
    return sum(jnp.dot(m, t, preferred_element_type=F32) for t in _split3(x))


def _m01x_fwd(m, x):
    return _m01x(m, x), m


def _m01x_bwd(m, g):
    return jnp.zeros_like(m), sum(_dot_tn(m, t) for t in _split3(g))


_m01x.defvjp(_m01x_fwd, _m01x_bwd)


def _ssd_chunk(pre, dtr, s_prev, dtb, alog):
    T = pre.shape[0]
    act = _silu(pre)
    xs, bm, cm = act[:, :SSM_INNER], act[:, SSM_INNER:SSM_INNER + 128], act[:, SSM_INNER + 128:]
    lane = lax.broadcasted_iota(jnp.int32, (1, LANES), 1)
    dt = jnp.where(lane < SSM_HEADS, _softplus(dtr + dtb), 0.0)
    a = dt * (-jnp.exp(alog))
    ri = lax.broadcasted_iota(jnp.int32, (T, T), 0)
    ci = lax.broadcasted_iota(jnp.int32, (T, T), 1)
    causal = ci <= ri
    a_cs = _m01x(causal.astype(BF16), a)
    eh = lax.broadcasted_iota(jnp.int32, (LANES, SSM_INNER), 0)
    ej = lax.broadcasted_iota(jnp.int32, (LANES, SSM_INNER), 1)
    expand = (lax.shift_right_logical(ej, 6) == eh).astype(BF16)
    dt_full = _xm01(dt, expand)
    acs_full = _xm01(a_cs, expand)
    alast_full = acs_full[T - 1:T, :]
    xdt = xs * dt_full
    a_cs_t = a_cs.T
    ys, s_new = [], []
    for g in range(2):
        in_group = lax.shift_right_logical(lane, 6) == g
        cg = jnp.where(in_group, cm, 0.0).astype(BF16)
        bg = jnp.where(in_group, bm, 0.0).astype(BF16)
        scores = _dot_nt(cg, bg)
        for pp in range(2):
            hp = 2 * g + pp
            cols = slice(hp * LANES, (hp + 1) * LANES)
            xp, acsp = xdt[:, cols], acs_full[:, cols]
            per_head = []
            for hh in range(2):
                h = 2 * hp + hh
                decay = jnp.exp(jnp.where(causal, a_cs[:, h:h + 1] - a_cs_t[h:h + 1, :], -jnp.inf))
                per_head.append(jnp.dot((scores * decay).astype(BF16), xp.astype(BF16), preferred_element_type=F32))
            y_diag = jnp.where(lane < SSM_STATE, per_head[0], per_head[1])
            sp = s_prev[hp * LANES:(hp + 1) * LANES, :]
            y_off = jnp.dot(cg, sp.astype(BF16), preferred_element_type=F32) * jnp.exp(acsp)
            ys.append(y_diag + y_off)
            to_end = jnp.exp(alast_full[:, cols] - acsp)
            s_new.append(sp * jnp.exp(alast_full[:, cols]) + _dot_tn(bg, (xp * to_end).astype(BF16)))
    return jnp.concatenate(ys, axis=1), jnp.concatenate(s_new, axis=0)


def _ssd_fwd(pre, P, dtb, alog):
    L = pre.shape[0]
    T = min(SSM_CHUNK, L)
    nc = L // T

    def body(pre_ref, dt_ref, dtb_ref, al_ref, y_ref, st_ref, s_scr):
        @pl.when(pl.program_id(0) == 0)
        def _():
            s_scr[...] = jnp.zeros_like(s_scr)

        st_ref[0] = s_scr[...]
        y, s = _ssd_chunk(pre_ref[...], _f(dt_ref[...]), s_scr[...], dtb_ref[...], al_ref[...])
        y_ref[...] = y
        s_scr[...] = s

    vec = pl.BlockSpec((1, LANES), lambda i: (0, 0))
    return pl.pallas_call(
        body, name="ssd_fwd", grid=(nc,),
        in_specs=[pl.BlockSpec((T, SSM_CONV_DIM), lambda i: (i, 0)), pl.BlockSpec((T, LANES), lambda i: (i, P_DT // LANES)),
                  vec, vec],
        out_specs=[pl.BlockSpec((T, SSM_INNER), lambda i: (i, 0)), pl.BlockSpec((1, 512, LANES), lambda i: (i, 0, 0))],
        out_shape=[jax.ShapeDtypeStruct((L, SSM_INNER), F32), jax.ShapeDtypeStruct((nc, 512, LANES), F32)],
        scratch_shapes=[pltpu.VMEM((512, LANES), F32)], compiler_params=_params(("arbitrary",)),
    )(pre, P, dtb, alog)


def _ssd_bwd(pre, P, states, dy, dxs_extra, dtb, alog):
    L = pre.shape[0]
    T = min(SSM_CHUNK, L)
    nc = L // T

    def body(pre_ref, dt_ref, st_ref, dy_ref, dx_ref, dtb_ref, al_ref, dpre_ref, ddt_ref, ddtb_ref, dal_ref, ds_scr):
        @pl.when(pl.program_id(0) == 0)
        def _():
            ds_scr[...] = jnp.zeros_like(ds_scr)
            ddtb_ref[...] = jnp.zeros_like(ddtb_ref)
            dal_ref[...] = jnp.zeros_like(dal_ref)

        _, vjp = jax.vjp(_ssd_chunk, pre_ref[...], _f(dt_ref[...]), st_ref[0], dtb_ref[...], al_ref[...])
        dpre, ddt, ds, ddtb, dal = vjp((dy_ref[...], ds_scr[...]))
        dpre_ref[:, :SSM_INNER] = dpre[:, :SSM_INNER] + dx_ref[...]
        dpre_ref[:, SSM_INNER:] = dpre[:, SSM_INNER:]
        ddt_ref[:, :LANES] = ddt.astype(ddt_ref.dtype)
        ddt_ref[:, LANES:] = jnp.zeros((T, DT_PAD - LANES), ddt_ref.dtype)
        ds_scr[...] = ds
        ddtb_ref[...] += ddtb
        dal_ref[...] += dal

    vec = pl.BlockSpec((1, LANES), lambda i: (0, 0))
    rev = lambda i: (nc - 1 - i, 0)
    return pl.pallas_call(
        body, name="ssd_bwd", grid=(nc,),
        in_specs=[pl.BlockSpec((T, SSM_CONV_DIM), rev), pl.BlockSpec((T, LANES), lambda i: (nc - 1 - i, P_DT // LANES)),
                  pl.BlockSpec((1, 512, LANES), lambda i: (nc - 1 - i, 0, 0)),
                  pl.BlockSpec((T, SSM_INNER), rev), pl.BlockSpec((T, SSM_INNER), rev), vec, vec],
        out_specs=[pl.BlockSpec((T, SSM_CONV_DIM), rev), pl.BlockSpec((T, DT_PAD), rev), vec, vec],
        out_shape=[jax.ShapeDtypeStruct((L, SSM_CONV_DIM), F32), jax.ShapeDtypeStruct((L, DT_PAD), BF16),
                   jax.ShapeDtypeStruct((1, LANES), F32), jax.ShapeDtypeStruct((1, LANES), F32)],
        scratch_shapes=[pltpu.VMEM((512, LANES), F32)], compiler_params=_params(("arbitrary",)),
    )(pre, P, states, dy, dxs_extra, dtb, alog)


def _sb_scores(qm, kb, later, strict, mask):
    z = _dot_nt(qm, kb)
    lk = jnp.minimum(-z, 0.0) - jnp.log(1.0 + jnp.exp(-jnp.abs(z)))
    if mask is not None:
        lk = jnp.where(mask, lk, 0.0)
    log_a = z + lk + jnp.dot(lk.astype(BF16), strict, preferred_element_type=F32) + later
    if mask is not None:
        log_a = jnp.where(mask, log_a, -jnp.inf)
    return z, lk, log_a


def _dot_split(x, m):
    hi = x.astype(BF16)
    lo = (x - hi.astype(F32)).astype(BF16)
    return jnp.dot(hi, m, preferred_element_type=F32) + jnp.dot(lo, m, preferred_element_type=F32)


def _sb_setup(q_ref, i, tq, tk):
    lane = lax.broadcasted_iota(jnp.int32, (1, LANES), 1)
    first = lane < SB_HEAD_DIM
    q = q_ref[...] * (SB_HEAD_DIM ** -0.5)
    qms = (jnp.where(first, q, jnp.zeros_like(q)), jnp.where(first, jnp.zeros_like(q), q))
    j0 = lax.div(i * tq, tk)
    ri = lax.broadcasted_iota(jnp.int32, (tq, tk), 0)
    ci = lax.broadcasted_iota(jnp.int32, (tq, tk), 1)
    diag_mask = (ci + (j0 * tk - i * tq)) < ri
    kr = lax.broadcasted_iota(jnp.int32, (tk, tk), 0)
    kc = lax.broadcasted_iota(jnp.int32, (tk, tk), 1)
    strict = (kr > kc).astype(BF16)
    return first, qms, j0, diag_mask, strict


def _sb_continue(c):
    return jnp.logical_and(c[0] >= 0, jnp.maximum(jnp.max(c[1][0]), jnp.max(c[1][1])) > SB_LOG_CUTOFF)


def _sb_fwd(P):
    L = P.shape[0]
    tq, tk = min(SB_TQ, L), min(SB_TK, L)
    nq = L // tq
    qb = P_B // LANES

    def body(q_ref, k_ref, v_ref, o_ref, of_ref):
        i = pl.program_id(1)
        first, qms, j0, diag_mask, strict = _sb_setup(q_ref, i, tq, tk)

        def tile(h, j, later, acc, mask=None, valid=None):
            off = pl.multiple_of(j * tk, tk)
            gate = later if valid is None else jnp.where(valid, later, -jnp.inf)
            _, lk, log_a = _sb_scores(qms[h], k_ref[pl.ds(off, tk), :], gate, strict, mask)
            acc = acc + jnp.dot(jnp.exp(log_a).astype(BF16), v_ref[pl.ds(off, tk), :], preferred_element_type=F32)
            rows = jnp.sum(lk, axis=1, keepdims=True)
            return later + (rows if valid is None else jnp.where(valid, rows, 0.0)), acc

        zero, zacc = jnp.zeros((tq, 1), F32), jnp.zeros((tq, LANES), F32)
        state = []
        for h in range(2):
            later, acc = tile(h, j0, zero, zacc, mask=diag_mask)
            state.append(tile(h, jnp.maximum(j0 - 1, 0), later, acc, valid=j0 >= 1))

        def tail(c):
            res = [tile(h, c[0], c[1][h], c[2][h]) for h in range(2)]
            return c[0] - 1, (res[0][0], res[1][0]), (res[0][1], res[1][1])

        _, _, accs = lax.while_loop(_sb_continue, tail, (j0 - 2, (state[0][0], state[1][0]), (state[0][1], state[1][1])))
        out = jnp.where(first, accs[0], accs[1])
        o_ref[...] = out.astype(o_ref.dtype)
        of_ref[...] = out

    tile_spec = pl.BlockSpec((tq, LANES), lambda p, i: (i, p))
    return pl.pallas_call(
        body, name="sb_fwd", grid=(2, nq),
        in_specs=[pl.BlockSpec((tq, LANES), lambda p, i: (i, qb + p)),
                  pl.BlockSpec((L, LANES), lambda p, i: (0, qb + 2 + p)),
                  pl.BlockSpec((L, LANES), lambda p, i: (0, qb + 4 + p))],
        out_specs=[tile_spec, tile_spec],
        out_shape=[jax.ShapeDtypeStruct((L, 2 * LANES), BF16), jax.ShapeDtypeStruct((L, 2 * LANES), F32)],
        compiler_params=_params(("arbitrary", "arbitrary")),
    )(P, P, P)


def _sb_bwd(P, dyb, yb32):
    L = P.shape[0]
    tq, tk = min(SB_TQ, L), min(SB_TK, L)
    nq = L // tq
    qb = P_B // LANES

    def body(q_ref, k_ref, v_ref, do_ref, of_ref, dq_ref, dk_ref, dv_ref):
        i = pl.program_id(1)
        first, qms, j0, diag_mask, strict = _sb_setup(q_ref, i, tq, tk)

        @pl.when(i == 0)
        def _():
            dk_ref[...] = jnp.zeros_like(dk_ref)
            dv_ref[...] = jnp.zeros_like(dv_ref)

        do = do_ref[...]
        doms = (jnp.where(first, do, jnp.zeros_like(do)), jnp.where(first, jnp.zeros_like(do), do))
        prod = _f(do) * of_ref[...]
        totals = (jnp.sum(jnp.where(first, prod, 0.0), axis=1, keepdims=True),
                  jnp.sum(jnp.where(first, 0.0, prod), axis=1, keepdims=True))

        def tile(h, j, later, later_g, acc, mask=None, valid=None):
            off = pl.multiple_of(j * tk, tk)
            kb, vb = k_ref[pl.ds(off, tk), :], v_ref[pl.ds(off, tk), :]
            gate = later if valid is None else jnp.where(valid, later, -jnp.inf)
            z, lk, log_a = _sb_scores(qms[h], kb, gate, strict, mask)
            att = jnp.exp(log_a).astype(BF16)
            g = _f(att) * _dot_nt(doms[h], vb)
            before = totals[h] - later_g
            if valid is not None:
                before = jnp.where(valid, before, 0.0)
            dz = g - (before - _dot_split(g, strict)) * jnp.exp(z + lk)
            if mask is not None:
                dz = jnp.where(mask, dz, 0.0)
            dzb = dz.astype(BF16)
            rows = jnp.sum(lk, axis=1, keepdims=True)
            carry = (later + (rows if valid is None else jnp.where(valid, rows, 0.0)),
                     later_g + jnp.sum(g, axis=1, keepdims=True), acc + jnp.dot(dzb, kb, preferred_element_type=F32))
            return carry, _dot_tn(dzb, qms[h]), _dot_tn(att, doms[h])

        def tail(c):
            off = pl.multiple_of(c[0] * tk, tk)
            (c0, dk0, dv0), (c1, dk1, dv1) = [tile(h, c[0], c[1][h], c[2][h], c[3][h]) for h in range(2)]
            dk_ref[pl.ds(off, tk), :] += dk0 + dk1
            dv_ref[pl.ds(off, tk), :] += dv0 + dv1
            return (c[0] - 1,) + tuple(zip(c0, c1))

        zero, zacc = jnp.zeros((tq, 1), F32), jnp.zeros((tq, LANES), F32)
        jp = jnp.maximum(j0 - 1, 0)
        carries, dks, dvs = [], [], []
        for h in range(2):
            carry, dk0, dv0 = tile(h, j0, zero, zero, zacc, mask=diag_mask)
            carry, dk1, dv1 = tile(h, jp, *carry, valid=j0 >= 1)
            carries.append(carry)
            dks.append((dk0, dk1))
            dvs.append((dv0, dv1))
        for n, j in enumerate((j0, jp)):
            off = pl.multiple_of(j * tk, tk)
            dk_ref[pl.ds(off, tk), :] += dks[0][n] + dks[1][n]
            dv_ref[pl.ds(off, tk), :] += dvs[0][n] + dvs[1][n]
        accs = lax.while_loop(_sb_continue, tail, (j0 - 2,) + tuple(zip(carries[0], carries[1])))[3]
        dq_ref[...] = jnp.where(first, accs[0], accs[1]) * (SB_HEAD_DIM ** -0.5)

    full = pl.BlockSpec((L, LANES), lambda p, i: (0, p))
    tile_spec = pl.BlockSpec((tq, LANES), lambda p, i: (i, p))
    return pl.pallas_call(
        body, name="sb_bwd", grid=(2, nq),
        in_specs=[pl.BlockSpec((tq, LANES), lambda p, i: (i, qb + p)),
                  pl.BlockSpec((L, LANES), lambda p, i: (0, qb + 2 + p)),
                  pl.BlockSpec((L, LANES), lambda p, i: (0, qb + 4 + p)), tile_spec, tile_spec],
        out_specs=[tile_spec, full, full],
        out_shape=[jax.ShapeDtypeStruct((L, 2 * LANES), F32)] * 3,
        compiler_params=_params(("arbitrary", "arbitrary")),
    )(P, P, P, dyb, yb32)


MOD_SHARD = N_MOD * D_MODEL // N_CHIPS


def _mod_fwd(c_all, mod_w, mod_b_sh):
    tn = 512

    def body(c_ref, w_ref, b_ref, o_ref):
        o_ref[0] = jnp.dot(_silu(c_ref[...]), w_ref[0], precision=HIGHEST, preferred_element_type=F32) + b_ref[0]

    return pl.pallas_call(
        body, name="mod_fwd", grid=(DEPTH, MOD_SHARD // tn),
        in_specs=[pl.BlockSpec((N_DEV, D_MODEL), lambda l, j: (0, 0)),
                  pl.BlockSpec((1, D_MODEL, tn), lambda l, j: (l, 0, j)),
                  pl.BlockSpec((1, 1, tn), lambda l, j: (l, 0, j))],
        out_specs=pl.BlockSpec((1, N_DEV, tn), lambda l, j: (l, 0, j)),
        out_shape=jax.ShapeDtypeStruct((DEPTH, N_DEV, MOD_SHARD), F32),
        compiler_params=_params(("arbitrary", "arbitrary")),
    )(c_all, mod_w, mod_b_sh)


def _mod_bwd(c_all, dmod_sh):
    tn = 512

    def body(c_ref, d_ref, o_ref):
        o_ref[0] = lax.dot_general(_silu(c_ref[...]), d_ref[0], (((0,), (0,)), ((), ())), precision=HIGHEST,
                                   preferred_element_type=F32)

    return pl.pallas_call(
        body, name="mod_bwd", grid=(DEPTH, MOD_SHARD // tn),
        in_specs=[pl.BlockSpec((N_DEV, D_MODEL), lambda l, j: (0, 0)),
                  pl.BlockSpec((1, N_DEV, tn), lambda l, j: (l, 0, j))],
        out_specs=pl.BlockSpec((1, D_MODEL, tn), lambda l, j: (l, 0, j)),
        out_shape=jax.ShapeDtypeStruct((DEPTH, D_MODEL, MOD_SHARD), F32),
        compiler_params=_params(("arbitrary", "arbitrary")),
    )(c_all, dmod_sh)


def _row_tile(rows, cap):
    if rows <= cap:
        return rows
    best = None
    for t in range(8, cap + 1, 8):
        if rows % t == 0:
            best = t
    assert best is not None, (rows, cap)
    return best


def _adamw(name, w, gs, m, v, tr=256):
    R, W = w.shape
    by_layer = any(isinstance(t, tuple) for t in gs)
    tr = _row_tile(R // 2 if by_layer else R, tr)
    per = (R // 2) // tr

    flat, specs = [], []
    for t in gs:
        if isinstance(t, tuple):
            flat += list(t)
            specs += [pl.BlockSpec((tr, W), lambda i: (jnp.minimum(i, per - 1), 0)),
                      pl.BlockSpec((tr, W), lambda i: (jnp.maximum(i - per, 0), 0))]
        else:
            flat.append(t)
            specs.append(pl.BlockSpec((tr, W), lambda i: (i, 0)))
    ng = len(flat)

    def body(*refs):
        w_ref, g_refs, (m_ref, v_ref) = refs[0], list(refs[1:1 + ng]), refs[1 + ng:3 + ng]
        g_out, d_out, m_out, v_out = refs[3 + ng:]
        g = None
        for t in gs:
            if isinstance(t, tuple):
                lo, hi = g_refs.pop(0), g_refs.pop(0)
                term = jnp.where(pl.program_id(0) < per, lo[...], hi[...])
            else:
                term = g_refs.pop(0)[...]
            g = term if g is None else g + term
        mm = ADAM_B1 * m_ref[...] + (1.0 - ADAM_B1) * g
        vv = ADAM_B2 * v_ref[...] + (1.0 - ADAM_B2) * (g * g)
        m_hat = mm / (1.0 - ADAM_B1 ** ADAM_STEP)
        v_hat = vv / (1.0 - ADAM_B2 ** ADAM_STEP)
        g_out[...] = g
        d_out[...] = -ADAM_LR * (m_hat / (jnp.sqrt(v_hat) + ADAM_EPS) + ADAM_WD * w_ref[...])
        m_out[...] = mm
        v_out[...] = vv

    spec = pl.BlockSpec((tr, W), lambda i: (i, 0))
    return pl.pallas_call(
        body, name=name, grid=(R // tr,), in_specs=[spec] + specs + [spec, spec], out_specs=[spec] * 4,
        out_shape=[jax.ShapeDtypeStruct((R, W), F32)] * 4, compiler_params=_params(("arbitrary",)),
    )(w, *flat, m, v)


def _sum_slots(name, a, tr=256):
    n, R, W = a.shape
    tr = _row_tile(R, tr)

    def body(a_ref, o_ref):
        acc = _f(a_ref[0])
        for j in range(1, n):
            acc = acc + _f(a_ref[j])
        o_ref[...] = acc

    return pl.pallas_call(
        body, name=name, grid=(R // tr,), in_specs=[pl.BlockSpec((n, tr, W), lambda i: (0, i, 0))],
        out_specs=pl.BlockSpec((tr, W), lambda i: (i, 0)), out_shape=jax.ShapeDtypeStruct((R, W), F32),
        compiler_params=_params(("arbitrary",)),
    )(a)


def _here():
    return lax.axis_index("x"), lax.axis_index("y"), lax.axis_index("c")


def _flip(v, d):
    return 1 - v if d else v


def _allgather_small(name, buf):
    R = buf.shape[0]
    rel = [(dx, dy, dc) for dx in (0, 1) for dy in (0, 1) for dc in (0, 1)][1:]

    def body(x_ref, o_ref, send, recv, lsem):
        x, y, c = _here()
        me = 4 * x + 2 * y + c
        mine = pltpu.make_async_copy(x_ref, o_ref.at[me], lsem)
        mine.start()

        def copy(k, slot):
            dx, dy, dc = rel[k]
            return pltpu.make_async_remote_copy(
                src_ref=x_ref, dst_ref=o_ref.at[slot], send_sem=send.at[k], recv_sem=recv.at[k],
                device_id=(_flip(x, dx), _flip(y, dy), _flip(c, dc)), device_id_type=MESH_ID)

        sent = [copy(k, me) for k in range(len(rel))]
        for cp in sent:
            cp.start()
        for k, (dx, dy, dc) in enumerate(rel):
            copy(k, 4 * _flip(x, dx) + 2 * _flip(y, dy) + _flip(c, dc)).wait_recv()
        for cp in sent:
            cp.wait_send()
        mine.wait()

    return pl.pallas_call(
        body, name=name, out_shape=jax.ShapeDtypeStruct((N_DEV, R, LANES), F32),
        in_specs=[pl.BlockSpec(memory_space=pltpu.VMEM)], out_specs=pl.BlockSpec(memory_space=pltpu.VMEM),
        scratch_shapes=[pltpu.SemaphoreType.DMA((7,)), pltpu.SemaphoreType.DMA((7,)), pltpu.SemaphoreType.DMA],
    )(buf)


CHIP_REL = [(1, 0), (0, 1), (1, 1)]


class _Side:
    def __init__(self, arrays, out_shapes, scratch, start, finish):
        self.arrays, self.out_shapes, self.scratch, self.start, self.finish = arrays, out_shapes, scratch, start, finish


def _chip_of(k):
    x, y, _ = _here()
    dx, dy = CHIP_REL[k]
    return _flip(x, dx), _flip(y, dy)


def _scatter_side(arrays):
    n = len(arrays)

    def parts(ins, outs, sems):
        send, recv, lsem = sems
        x, y, c = _here()
        s = 2 * x + y

        def copy(w, k, mine):
            px, py = _chip_of(k)
            return pltpu.make_async_remote_copy(
                src_ref=ins[w].at[2 * px + py], dst_ref=outs[w].at[s if mine else 2 * px + py],
                send_sem=send.at[3 * w + k], recv_sem=recv.at[3 * w + k], device_id=(px, py, c), device_id_type=MESH_ID)

        local = [pltpu.make_async_copy(ins[w].at[s], outs[w].at[s], lsem.at[w]) for w in range(n)]
        return copy, local

    def start(ins, outs, sems):
        copy, local = parts(ins, outs, sems)
        for cp in local:
            cp.start()
        for w in range(n):
            for k in range(3):
                copy(w, k, True).start()

    def finish(ins, outs, sems):
        copy, local = parts(ins, outs, sems)
        for w in range(n):
            for k in range(3):
                copy(w, k, False).wait_recv()
        for w in range(n):
            for k in range(3):
                copy(w, k, True).wait_send()
        for cp in local:
            cp.wait()

    scratch = [pltpu.SemaphoreType.DMA((3 * n,)), pltpu.SemaphoreType.DMA((3 * n,)), pltpu.SemaphoreType.DMA((n,))]
    return _Side(arrays, [jax.ShapeDtypeStruct(a.shape, a.dtype) for a in arrays], scratch, start, finish)


def _gather_side(shards):
    n = len(shards)

    def parts(ins, outs, sems):
        send, recv, fsend, frecv, lsem = sems
        x, y, c = _here()
        s = 2 * x + y

        def half(ref, w, which):
            rows = shards[w].shape[0] // 2
            return ref.at[pl.ds(pl.multiple_of(which * rows, 16), rows)]

        def over_ici(w, k, mine):
            px, py = _chip_of(k)
            return pltpu.make_async_remote_copy(
                src_ref=half(ins[w], w, c), dst_ref=half(outs[w].at[s if mine else 2 * px + py], w, c),
                send_sem=send.at[3 * w + k], recv_sem=recv.at[3 * w + k], device_id=(px, py, c), device_id_type=MESH_ID)

        def to_sibling(w, k, which):
            px, py = _chip_of(k)
            part = half(outs[w].at[2 * px + py], w, which)
            return pltpu.make_async_remote_copy(
                src_ref=part, dst_ref=part, send_sem=fsend.at[3 * w + k], recv_sem=frecv.at[3 * w + k],
                device_id=(x, y, 1 - c), device_id_type=MESH_ID)

        local = [pltpu.make_async_copy(ins[w], outs[w].at[s], lsem.at[w]) for w in range(n)]
        return c, over_ici, to_sibling, local

    def start(ins, outs, sems):
        _, over_ici, _, local = parts(ins, outs, sems)
        for cp in local:
            cp.start()
        for w in range(n):
            for k in range(3):
                over_ici(w, k, True).start()

    def finish(ins, outs, sems):
        c, over_ici, to_sibling, local = parts(ins, outs, sems)
        for w in range(n):
            for k in range(3):
                over_ici(w, k, False).wait_recv()
                to_sibling(w, k, c).start()
        for w in range(n):
            for k in range(3):
                to_sibling(w, k, 1 - c).wait_recv()
        for w in range(n):
            for k in range(3):
                over_ici(w, k, True).wait_send()
                to_sibling(w, k, c).wait_send()
        for cp in local:
            cp.wait()

    scratch = [pltpu.SemaphoreType.DMA((3 * n,))] * 4 + [pltpu.SemaphoreType.DMA((n,))]
    return _Side(shards, [jax.ShapeDtypeStruct((N_CHIPS,) + a.shape, a.dtype) for a in shards], scratch, start, finish)


def _comm_call(name, side):
    n, n_out = len(side.arrays), len(side.out_shapes)

    def body(*refs):
        ins, outs, sems = refs[:n], refs[n:n + n_out], refs[n + n_out:]
        side.start(ins, outs, sems)
        side.finish(ins, outs, sems)

    any_spec = pl.BlockSpec(memory_space=pl.ANY)
    return pl.pallas_call(body, name=name, out_shape=side.out_shapes, in_specs=[any_spec] * n,
                          out_specs=[any_spec] * n_out, scratch_shapes=side.scratch)(*side.arrays)


def _sibling_exchange(name, arrays):
    n = len(arrays)

    def body(*refs):
        ins, outs = refs[:n], refs[n:2 * n]
        send, recv = refs[2 * n:]
        x, y, c = _here()
        cps = [pltpu.make_async_remote_copy(src_ref=ins[w], dst_ref=outs[w], send_sem=send.at[w], recv_sem=recv.at[w],
                                            device_id=(x, y, 1 - c), device_id_type=MESH_ID) for w in range(n)]
        for cp in cps:
            cp.start()
        for cp in cps:
            cp.wait()

    any_spec = pl.BlockSpec(memory_space=pl.ANY)
    return pl.pallas_call(
        body, name=name, out_shape=[jax.ShapeDtypeStruct(a.shape, a.dtype) for a in arrays],
        in_specs=[any_spec] * n, out_specs=[any_spec] * n,
        scratch_shapes=[pltpu.SemaphoreType.DMA((n,)), pltpu.SemaphoreType.DMA((n,))],
    )(*arrays)


def _pack(arrs):
    flat = jnp.concatenate([a.reshape(-1).astype(F32) for a in arrs])
    n = flat.shape[0]
    rows = -(-n // (8 * LANES)) * 8
    return jnp.pad(flat, (0, rows * LANES - n)).reshape(rows, LANES)


def _unpack(buf, shapes):
    lead = buf.shape[:-2]
    flat = buf.reshape(lead + (-1,))
    out, off = [], 0
    for s in shapes:
        n = 1
        for d in s:
            n *= d
        out.append(flat[..., off:off + n].reshape(lead + tuple(s)))
        off += n
    return out


def _pad_w_in(w):
    return jnp.concatenate([w[:, :2048], w[:, 2816:2824], jnp.zeros((w.shape[0], P_XBC - P_DT - 8), w.dtype),
                            w[:, 2048:2816], w[:, 2824:]], axis=1)


def _unpad_w_in(g):
    return jnp.concatenate([g[:, :P_DT], g[:, P_XBC:P_G], g[:, P_DT:P_DT + 8], g[:, P_G:]], axis=1)


FFN_HALF = FFN_HIDDEN // 2


def _ffn_in_cols(w):
    h = FFN_HALF
    return jnp.concatenate([w[:, :h], w[:, 2 * h:3 * h], w[:, h:2 * h], w[:, 3 * h:]], axis=1)


def _row(v):
    return v.reshape(1, -1)


BIG = (("w_in", 2), ("w_sc_out", 2), ("w_sb_out", 2), ("w_ssm_out", 2), ("w_o", 1), ("w_ffn_in", 2), ("w_ffn_out", 1))
SMALL = ("mod_b", "g_pre_mix", "g_post_mix", "g_pre_ffn", "g_post_ffn", "sc_conv_w", "ssm_conv_w", "ssm_conv_b",
         "ssm_dt_bias", "ssm_a_log", "ssm_d", "ssm_norm_w")
WEIGHT_ORDER = ("mod_w", "mod_b", "g_pre_mix", "g_post_mix", "g_pre_ffn", "g_post_ffn", "w_in", "sc_conv_w",
                "ssm_conv_w", "ssm_conv_b", "ssm_dt_bias", "ssm_a_log", "ssm_d", "ssm_norm_w", "w_sc_out", "w_sb_out",
                "w_ssm_out", "w_o", "w_ffn_in", "w_ffn_out")


def _mm_mid(name, a, w, x, vecs):
    return _mm_epi(name, a, w, "nn", D_MODEL, [(x, D_MODEL)] + [(v, None) for v in vecs],
                   lambda p, x, *v: (p,) + tuple(_mid(x, p, *v)), [(D_MODEL, BF16), (D_MODEL, F32), (D_MODEL, BF16)])


def _layer_fwd(l, x_in, h, W, V, sides, next_vecs):
    S = {"x_in": x_in, "h": h}
    side, handler = sides.get("in_proj", (None, None))
    P = _mm(f"in_proj{l}", h, W["w_in"], "nn", BF16, tn_cap=1024, side=side)
    if side:
        handler(P[1:])
        P = P[0]
    S["P"] = P
    S["ya"] = _shortconv_fwd(P, V["sc_w"])
    S["yb"], S["yb32"] = _sb_fwd(P)
    S["pre"] = _ssmconv_fwd(P, V["ssm_w"], V["ssm_b"])
    S["y_ssd"], S["states"] = _ssd_fwd(S["pre"], P, V["dtb"], V["alog"])
    S["yc"] = _rowwise(f"ssm_post{l}", lambda y, px, z, d, nw: _ssm_post(y, px, _f(z), d, nw),
                       [(S["y_ssd"], SSM_INNER, 0), (S["pre"], SSM_INNER, 0), (P, SSM_INNER, P_Z // SSM_INNER)],
                       [V["d_full"], V["norm_w"]], [(SSM_INNER, BF16)])[0]
    S["merged"] = _merge_fwd(f"merge{l}", P, [S["ya"], S["yb"], S["yc"]],
                             [W["w_sc_out"], W["w_sb_out"], W["w_ssm_out"]])
    S["mix"], S["x1"], S["h2"] = _mm_mid(f"w_o{l}", S["merged"], W["w_o"], x_in, V["mid_mix"])
    side, handler = sides.get("ffn_in", (None, None))
    res = _mm_epi(f"ffn_in{l}", S["h2"], W["w_ffn_in"], "nn", 2 * FFN_HALF, [],
                  lambda p: (p, _swiglu(p[:, :FFN_HALF], p[:, FFN_HALF:])),
                  [(2 * FFN_HALF, BF16), (FFN_HALF, BF16)], side=side)
    S["GU"], S["act"] = res[0], res[1]
    if side:
        handler(res[2:])
    if next_vecs is None:
        S["f"] = _mm(f"ffn_out{l}", S["act"], W["w_ffn_out"], "nn", BF16)
    else:
        S["f"], S["x_next"], S["h_next"] = _mm_mid(f"ffn_out{l}", S["act"], W["w_ffn_out"], S["x1"], next_vecs)
    return S


BRANCH_WIDTHS = (SC_WIDTH, 256, SSM_INNER)


def _branch_specs(tm):
    gb = P_G // D_MODEL
    gates = [pl.BlockSpec((tm, D_MODEL), functools.partial(lambda i, cb: (i, cb), cb=gb + k)) for k in range(3)]
    ys = [pl.BlockSpec((tm, w), lambda i: (i, 0)) for w in BRANCH_WIDTHS]
    ws = [pl.BlockSpec((w, D_MODEL), lambda i: (0, 0)) for w in BRANCH_WIDTHS]
    return gates, ys, ws


def _merge_fwd(name, P, ys, ws, tm=512):
    L = P.shape[0]
    tm = min(tm, L)
    gates, y_specs, w_specs = _branch_specs(tm)

    def body(ga, gb, gc, ya, yb, yc, wa, wb, wc, o_ref):
        acc = None
        for g_ref, y_ref, w_ref in ((ga, ya, wa), (gb, yb, wb), (gc, yc, wc)):
            t = jax.nn.sigmoid(_f(g_ref[...])) * jnp.dot(y_ref[...], w_ref[...], preferred_element_type=F32)
            acc = t if acc is None else acc + t
        o_ref[...] = acc.astype(o_ref.dtype)

    return pl.pallas_call(
        body, name=name, grid=(L // tm,), in_specs=gates + y_specs + w_specs,
        out_specs=pl.BlockSpec((tm, D_MODEL), lambda i: (i, 0)), out_shape=jax.ShapeDtypeStruct((L, D_MODEL), BF16),
        compiler_params=_params(("arbitrary",)),
    )(P, P, P, *ys, *ws)


def _merge_bwd(name, P, ys, ws, dmerged, tm=512):
    L = P.shape[0]
    tm = min(tm, L)
    gates, y_specs, w_specs = _branch_specs(tm)

    def body(ga, gb, gc, ya, yb, yc, wa, wb, wc, dm_ref, dg_ref, dya, dyb, dyc, gwa, gwb, gwc):
        @pl.when(pl.program_id(0) == 0)
        def _():
            for r in (gwa, gwb, gwc):
                r[...] = jnp.zeros_like(r)

        dm = _f(dm_ref[...])
        for k, (g_ref, y_ref, w_ref, dy_ref, gw_ref) in enumerate(
                ((ga, ya, wa, dya, gwa), (gb, yb, wb, dyb, gwb), (gc, yc, wc, dyc, gwc))):
            y, w = y_ref[...], w_ref[...]
            s = jax.nn.sigmoid(_f(g_ref[...]))
            proj = jnp.dot(y, w, preferred_element_type=F32)
            d_proj = (dm * s).astype(BF16)
            dg_ref[:, k * D_MODEL:(k + 1) * D_MODEL] = (dm * proj * s * (1.0 - s)).astype(dg_ref.dtype)
            dy_ref[...] = _dot_nt(d_proj, w).astype(dy_ref.dtype)
            gw_ref[...] += _dot_tn(y, d_proj)

    gate_cols = pl.BlockSpec((tm, P_WIDTH - P_G), lambda i: (i, P_G // (P_WIDTH - P_G)))
    return pl.pallas_call(
        body, name=name, grid=(L // tm,),
        in_specs=gates + y_specs + w_specs + [pl.BlockSpec((tm, D_MODEL), lambda i: (i, 0))],
        out_specs=[gate_cols] + y_specs + w_specs,
        out_shape=[jax.ShapeDtypeStruct((L, P_WIDTH), BF16)] + [jax.ShapeDtypeStruct((L, w), BF16) for w in BRANCH_WIDTHS]
        + [jax.ShapeDtypeStruct((w, D_MODEL), F32) for w in BRANCH_WIDTHS],
        compiler_params=_params(("arbitrary",)),
    )(P, P, P, *ys, *ws, dmerged)


def _assemble_dp(name, dP, parts, tl=256):
    L = dP.shape[0]
    tl = min(tl, L)
    n = len(parts)

    def body(*refs):
        o_ref = refs[n + 1]
        o_ref[...] = jnp.concatenate([r[...].astype(o_ref.dtype) for r in refs[:n]], axis=1)

    return pl.pallas_call(
        body, name=name, grid=(L // tl,),
        in_specs=[pl.BlockSpec((tl, a.shape[1]), lambda i: (i, 0)) for a in parts] + [pl.BlockSpec(memory_space=pl.ANY)],
        out_specs=pl.BlockSpec((tl, P_G), lambda i: (i, 0)), out_shape=jax.ShapeDtypeStruct(dP.shape, dP.dtype),
        input_output_aliases={n: 0}, compiler_params=_params(("arbitrary",)),
    )(*parts, dP)


def _layer_bwd(l, S, W, V, dx1, df, sides, landed):
    G = {}
    P = S["P"]

    def mm(key, *args, **kw):
        if key not in sides:
            return _mm(f"{key}{l}", *args, **kw)
        names, layer, make = sides[key]
        res = _mm(f"{key}{l}", *args, side=make(G), **kw)
        for n, a in zip(names, res[1:]):
            landed[(n, layer)] = a
        return res[0]

    G["w_ffn_out"] = _mm(f"gw_ffn_out{l}", S["act"], df, "tn", F32)

    def swiglu_bwd(d_act, gu):
        _, vjp = jax.vjp(_swiglu, _f(gu[:, :FFN_HALF]), _f(gu[:, FFN_HALF:]))
        return (jnp.concatenate(vjp(d_act), axis=1),)

    dGU = _mm_epi(f"d_gu{l}", df, W["w_ffn_out"], "nt", FFN_HALF, [(S["GU"], 2 * FFN_HALF)], swiglu_bwd,
                  [(2 * FFN_HALF, BF16)])[0]
    dh2 = mm("d_h2", dGU, W["w_ffn_in"], "nt", BF16)
    G["w_ffn_in"] = _ffn_in_cols(mm("gw_ffn_in", S["h2"], dGU, "tn", F32))
    dx, dmix, G["gate1"], G["g_post_mix"], G["g_pre_ffn"], G["scale2"], G["shift2"] = _mid_bwd(
        f"mid_mix_bwd{l}", S["x_in"], S["mix"], dx1, dh2, V["mid_mix"])
    dmerged = _mm(f"d_merged{l}", dmix, W["w_o"], "nt", BF16)
    G["w_o"] = _mm(f"gw_o{l}", S["merged"], dmix, "tn", F32)

    dP, dya, dyb, dyc, G["w_sc_out"], G["w_sb_out"], G["w_ssm_out"] = _merge_bwd(
        f"merge_bwd{l}", P, [S["ya"], S["yb"], S["yc"]], [W["w_sc_out"], W["w_sb_out"], W["w_ssm_out"]], dmerged)

    def post_bwd(y, px, z, d, dfull, nw):
        _, vjp = jax.vjp(_ssm_post, y, px, _f(z), dfull, nw)
        return vjp(_f(d))

    dy_ssd, dxs, dz, G["d_full"], G["ssm_norm_w"] = _rowwise(
        f"ssm_post_bwd{l}", post_bwd,
        [(S["y_ssd"], SSM_INNER, 0), (S["pre"], SSM_INNER, 0), (P, SSM_INNER, P_Z // SSM_INNER), (dyc, SSM_INNER, 0)],
        [V["d_full"], V["norm_w"]], [(SSM_INNER, F32), (SSM_INNER, F32), (SSM_INNER, BF16)], [(1, SSM_INNER)] * 2)
    dpre, ddt, G["dtb"], G["alog"] = _ssd_bwd(S["pre"], P, S["states"], dy_ssd, dxs, V["dtb"], V["alog"])
    dxbc, w0, w1, w2, w3, G["ssm_conv_b"] = _ssmconv_bwd(P, dpre, V["ssm_w"])
    G["ssm_conv_w"] = jnp.concatenate([w0, w1, w2, w3], axis=0)
    dq, dk, dv = _sb_bwd(P, dyb, S["yb32"])
    dA, s0, s1, s2 = _shortconv_bwd(P, dya, V["sc_w"])
    G["sc_conv_w"] = jnp.concatenate([s0, s1, s2], axis=0)
    dP = _assemble_dp(f"assemble_dp{l}", dP, [dA, dq, dk, dv, dz, ddt, dxbc])
    dh = mm("d_h", dP, W["w_in"], "nt", BF16, tk_cap=2048)
    G["w_in"] = _mm(f"gw_in{l}", S["h"], dP, "tn", F32, tn_cap=1024)
    return dx, dh, G


def kernel(x, c, mod_w, mod_b, g_pre_mix, g_post_mix, g_pre_ffn, g_post_ffn, w_in, sc_conv_w, ssm_conv_w, ssm_conv_b, ssm_dt_bias, ssm_a_log, ssm_d, ssm_norm_w, w_sc_out, w_sb_out, w_ssm_out, w_o, w_ffn_in, w_ffn_out, loss_target, m_mod_w, m_mod_b, m_g_pre_mix, m_g_post_mix, m_g_pre_ffn, m_g_post_ffn, m_w_in, m_sc_conv_w, m_ssm_conv_w, m_ssm_conv_b, m_ssm_dt_bias, m_ssm_a_log, m_ssm_d, m_ssm_norm_w, m_w_sc_out, m_w_sb_out, m_w_ssm_out, m_w_o, m_w_ffn_in, m_w_ffn_out, v_mod_w, v_mod_b, v_g_pre_mix, v_g_post_mix, v_g_pre_ffn, v_g_post_ffn, v_w_in, v_sc_conv_w, v_ssm_conv_w, v_ssm_conv_b, v_ssm_dt_bias, v_ssm_a_log, v_ssm_d, v_ssm_norm_w, v_w_sc_out, v_w_sb_out, v_w_ssm_out, v_w_o, v_w_ffn_in, v_w_ffn_out):
    wts = dict(mod_w=mod_w, mod_b=mod_b, g_pre_mix=g_pre_mix, g_post_mix=g_post_mix, g_pre_ffn=g_pre_ffn,
               g_post_ffn=g_post_ffn, w_in=w_in, sc_conv_w=sc_conv_w, ssm_conv_w=ssm_conv_w, ssm_conv_b=ssm_conv_b,
               ssm_dt_bias=ssm_dt_bias, ssm_a_log=ssm_a_log, ssm_d=ssm_d, ssm_norm_w=ssm_norm_w, w_sc_out=w_sc_out,
               w_sb_out=w_sb_out, w_ssm_out=w_ssm_out, w_o=w_o, w_ffn_in=w_ffn_in, w_ffn_out=w_ffn_out)
    mom = dict(mod_w=m_mod_w, mod_b=m_mod_b, g_pre_mix=m_g_pre_mix, g_post_mix=m_g_post_mix, g_pre_ffn=m_g_pre_ffn,
               g_post_ffn=m_g_post_ffn, w_in=m_w_in, sc_conv_w=m_sc_conv_w, ssm_conv_w=m_ssm_conv_w,
               ssm_conv_b=m_ssm_conv_b, ssm_dt_bias=m_ssm_dt_bias, ssm_a_log=m_ssm_a_log, ssm_d=m_ssm_d,
               ssm_norm_w=m_ssm_norm_w, w_sc_out=m_w_sc_out, w_sb_out=m_w_sb_out, w_ssm_out=m_w_ssm_out, w_o=m_w_o,
               w_ffn_in=m_w_ffn_in, w_ffn_out=m_w_ffn_out)
    var = dict(mod_w=v_mod_w, mod_b=v_mod_b, g_pre_mix=v_g_pre_mix, g_post_mix=v_g_post_mix, g_pre_ffn=v_g_pre_ffn,
               g_post_ffn=v_g_post_ffn, w_in=v_w_in, sc_conv_w=v_sc_conv_w, ssm_conv_w=v_ssm_conv_w,
               ssm_conv_b=v_ssm_conv_b, ssm_dt_bias=v_ssm_dt_bias, ssm_a_log=v_ssm_a_log, ssm_d=v_ssm_d,
               ssm_norm_w=v_ssm_norm_w, w_sc_out=v_w_sc_out, w_sb_out=v_w_sb_out, w_ssm_out=v_w_ssm_out, w_o=v_w_o,
               w_ffn_in=v_w_ffn_in, w_ffn_out=v_w_ffn_out)
    xi, yi, ci = _here()
    chip = 2 * xi + yi
    me = 4 * xi + 2 * yi + ci
    x0, target = x[0], loss_target[0]

    first_shapes = [(D_MODEL,), sc_conv_w.shape, ssm_conv_w.shape]
    g0 = _allgather_small("gather_cond", _pack([c, sc_conv_w, ssm_conv_w]))
    c_rows, sc_sh, ssm_sh = _unpack(g0, first_shapes)
    c_all = c_rows
    sc_w = jnp.concatenate([sc_sh[2 * j] for j in range(N_CHIPS)], axis=-1)
    ssm_w = jnp.concatenate([ssm_sh[2 * j] for j in range(N_CHIPS)], axis=-1)

    mod_b_sh = lax.dynamic_slice_in_dim(mod_b, chip * MOD_SHARD, MOD_SHARD, axis=1).reshape(DEPTH, 1, MOD_SHARD)
    modpart = _mod_fwd(c_all, mod_w, mod_b_sh)
    g1 = _allgather_small("gather_mod", modpart.reshape(-1, LANES)).reshape(N_DEV, DEPTH, N_DEV, MOD_SHARD)
    mod = jnp.concatenate([lax.dynamic_index_in_dim(g1[2 * j], me, axis=1, keepdims=False) for j in range(N_CHIPS)],
                          axis=-1)

    def layer_shards(l):
        return [wts[n][l].astype(BF16) for n, _ in BIG]

    def full_weights(which, gathered):
        W = {n: jnp.concatenate([g[j] for j in range(N_CHIPS)], axis=ax - 1) for (n, ax), g in zip(which, gathered)}
        if "w_in" in W:
            W["w_in"] = _pad_w_in(W["w_in"])
        if "w_ffn_in" in W:
            W["w_ffn_in"] = _ffn_in_cols(W["w_ffn_in"])
        return W

    Ws = [full_weights(BIG[:1], _comm_call("gather_w_in0", _gather_side(layer_shards(0)[:1]))), {}]
    fwd_sides = [{"in_proj": (_gather_side(layer_shards(0)[1:]), lambda got: Ws[0].update(full_weights(BIG[1:], got))),
                  "ffn_in": (_gather_side(layer_shards(1)), lambda got: Ws[1].update(full_weights(BIG, got)))}, {}]
    Vs = []
    for l in range(DEPTH):
        sh1, sc1, gt1, sh2, sc2, gt2 = [_row(v) for v in jnp.split(mod[l], N_MOD)]
        Vs.append(dict(
            shift1=sh1, scale1=sc1, g_pre_mix=_row(g_pre_mix[l]),
            mid_mix=[gt1, _row(g_post_mix[l]), _row(g_pre_ffn[l]), sc2, sh2],
            gate2=gt2, g_post_ffn=_row(g_post_ffn[l]),
            sc_w=sc_w[l], ssm_w=ssm_w[l], ssm_b=_row(ssm_conv_b[l]),
            dtb=_row(jnp.pad(ssm_dt_bias[l], (0, LANES - SSM_HEADS))), alog=_row(jnp.pad(ssm_a_log[l], (0, LANES - SSM_HEADS))),
            d_full=_row(jnp.repeat(ssm_d[l], SSM_INNER // SSM_HEADS)), norm_w=_row(ssm_norm_w[l])))

    def mid_ffn_vecs(l):
        return [Vs[l]["gate2"], Vs[l]["g_post_ffn"], Vs[l + 1]["g_pre_mix"], Vs[l + 1]["scale1"], Vs[l + 1]["shift1"]]

    saved = []
    x_in = x0
    h = _first_fwd(x0, [Vs[0]["g_pre_mix"], Vs[0]["scale1"], Vs[0]["shift1"]])
    for l in range(DEPTH):
        S = _layer_fwd(l, x_in, h, Ws[l], Vs[l], fwd_sides[l], mid_ffn_vecs(l) if l + 1 < DEPTH else None)
        saved.append(S)
        if l + 1 < DEPTH:
            x_in, h = S["x_next"], S["h_next"]

    def pieces(G, names):
        out = []
        for n, ax in BIG:
            if n in names:
                g = _unpad_w_in(G[n]) if n == "w_in" else G[n]
                out.append(jnp.stack(jnp.split(g, N_CHIPS, axis=ax - 1)).astype(BF16))
        return out

    ffn_names = ("w_ffn_in", "w_ffn_out")
    rest_names = tuple(n for n, _ in BIG if n not in ffn_names)
    late_names = tuple(n for n, _ in BIG if n != "w_in")
    landed = {}

    GL = [None] * DEPTH
    S = saved[-1]
    dx1, df, g_gate2, g_gpf, loss_cols = _last_bwd(S["x1"], S["f"], target, [Vs[-1]["gate2"], Vs[-1]["g_post_ffn"]])
    for l in reversed(range(DEPTH)):
        sides = {}
        if l + 1 < DEPTH:
            sides["d_h2"] = (ffn_names, l + 1, lambda G, up=GL[l + 1]: _scatter_side(pieces(up, ffn_names)))
            sides["gw_ffn_in"] = (rest_names, l + 1, lambda G, up=GL[l + 1]: _scatter_side(pieces(up, rest_names)))
        if l == 0:
            sides["d_h"] = (late_names, l, lambda G: _scatter_side(pieces(G, late_names)))
        dx, dh, G = _layer_bwd(l, saved[l], Ws[l], Vs[l], dx1, df, sides, landed)
        G["gate2"], G["g_post_ffn"] = g_gate2, g_gpf
        GL[l] = G
        if l > 0:
            Sp = saved[l - 1]
            dx1, df, g_gate2, g_gpf, G["g_pre_mix"], G["scale1"], G["shift1"] = _mid_bwd(
                f"mid_ffn_bwd{l - 1}", Sp["x1"], Sp["f"], dx, dh, mid_ffn_vecs(l - 1))
        else:
            grad_x, G["g_pre_mix"], G["scale1"], G["shift1"] = _first_bwd(
                x0, dx, dh, [Vs[0]["g_pre_mix"], Vs[0]["scale1"], Vs[0]["shift1"]])
    loss = lax.psum(jnp.sum(loss_cols), ("x", "y", "c"))

    def both(key, shape=None):
        a = jnp.stack([GL[l][key] for l in range(DEPTH)])
        return a if shape is None else a.reshape(shape)

    dmod = jnp.concatenate([both(k, (DEPTH, D_MODEL)) for k in ("shift1", "scale1", "gate1", "shift2", "scale2", "gate2")],
                           axis=1)
    part_small = dict(
        mod_b=dmod, g_pre_mix=both("g_pre_mix", (DEPTH, D_MODEL)), g_post_mix=both("g_post_mix", (DEPTH, D_MODEL)),
        g_pre_ffn=both("g_pre_ffn", (DEPTH, D_MODEL)), g_post_ffn=both("g_post_ffn", (DEPTH, D_MODEL)),
        sc_conv_w=both("sc_conv_w"), ssm_conv_w=both("ssm_conv_w"), ssm_conv_b=both("ssm_conv_b", (DEPTH, SSM_CONV_DIM)),
        ssm_dt_bias=both("dtb", (DEPTH, LANES))[:, :SSM_HEADS], ssm_a_log=both("alog", (DEPTH, LANES))[:, :SSM_HEADS],
        ssm_d=both("d_full", (DEPTH, SSM_HEADS, SSM_INNER // SSM_HEADS)).sum(-1),
        ssm_norm_w=both("ssm_norm_w", (DEPTH, SSM_INNER)))
    small_shapes = [part_small[n].shape for n in SMALL]
    g2 = _allgather_small("gather_small_grads", _pack([part_small[n] for n in SMALL]))
    tot = dict(zip(SMALL, _unpack(_sum_slots("sum_small_grads", g2), small_shapes)))
    dmod_all = _unpack(g2, small_shapes)[0]
    dmod_sh = jnp.swapaxes(lax.dynamic_slice_in_dim(dmod_all, chip * MOD_SHARD, MOD_SHARD, axis=2), 0, 1)
    grads = {"mod_w": _mod_bwd(c_all, dmod_sh)}
    for n in SMALL:
        grads[n] = tot[n]
    grads["sc_conv_w"] = lax.dynamic_slice_in_dim(tot["sc_conv_w"], chip * 64, 64, axis=2)
    grads["ssm_conv_w"] = lax.dynamic_slice_in_dim(tot["ssm_conv_w"], chip * 192, 192, axis=2)

    landed[("w_in", 0)] = _comm_call("scatter_w_in0", _scatter_side(pieces(GL[0], ("w_in",))))[0]
    keys = [(n, l) for n, _ in BIG for l in range(DEPTH)]
    mine = [_sum_slots(f"sum_{n}{l}", landed[(n, l)]) for n, l in keys]
    theirs = dict(zip(keys, _sibling_exchange("swap_core_sums", mine)))
    mine = dict(zip(keys, mine))

    out = {}

    def update(name, w2, gs, m2, v2, shape):
        g, d, nm, nv = _adamw(f"adamw_{name}", w2, gs, m2, v2)
        out[name] = tuple(a.reshape(shape) for a in (g, d, nm, nv))

    for n, _ in BIG:
        shp = wts[n].shape
        two = (-1, shp[-1])
        by_layer = [tuple(src[(n, l)] for l in range(DEPTH)) for src in (mine, theirs)]
        update(n, wts[n].reshape(two), by_layer, mom[n].reshape(two), var[n].reshape(two), shp)
    two = (-1, MOD_SHARD)
    update("mod_w", mod_w.reshape(two), [grads["mod_w"].reshape(two)], m_mod_w.reshape(two), v_mod_w.reshape(two), mod_w.shape)
    shapes = [wts[n].shape for n in SMALL]
    res = _adamw("adamw_small", _pack([wts[n] for n in SMALL]), [_pack([grads[n] for n in SMALL])],
                 _pack([mom[n] for n in SMALL]), _pack([var[n] for n in SMALL]))
    for n, g, d, nm, nv in zip(SMALL, *[_unpack(r, shapes) for r in res]):
        out[n] = (g, d, nm, nv)

    result = [loss, grad_x[None]]
    for k in range(4):
        result += [out[n][k] for n in WEIGHT_ORDER]
    return tuple(result)
```

```python
import functools

import jax
import jax.numpy as jnp
from jax import lax
from jax.experimental import pallas as pl
from jax.experimental.pallas import tpu as pltpu

F32 = jnp.float32
BF16 = jnp.bfloat16
HIGHEST = lax.Precision.HIGHEST
MESH_ID = pl.DeviceIdType.MESH

D_MODEL = 1024
DEPTH = 2
SC_WIDTH = 256
SC_KERNEL = 3
SB_HEAD_DIM = 64
SSM_INNER = 512
SSM_HEADS = 8
SSM_STATE = 64
SSM_CONV = 4
SSM_CHUNK = 256
SSM_CONV_DIM = 768
FFN_HIDDEN = 2816
NORM_EPS = 1e-6
N_MOD = 6
N_CHIPS = 4
N_DEV = 8

ADAM_LR = 0.001
ADAM_B1 = 0.9
ADAM_B2 = 0.999
ADAM_EPS = 1e-08
ADAM_WD = 0.01
ADAM_STEP = 10

P_WIDTH = 6144
P_A, P_B, P_Z, P_DT, P_XBC, P_G = 0, 768, 1536, 2048, 2304, 3072
DT_PAD = 256

VMEM_LIMIT_BYTES = 56 * 1024 * 1024
LANES = 128

SB_LOG_CUTOFF = -105.0
SB_TQ = 256
SB_TK = 256


def _params(sem):
    return pltpu.CompilerParams(dimension_semantics=sem, vmem_limit_bytes=VMEM_LIMIT_BYTES)


def _pick(n, cap):
    if n <= cap:
        return n
    best = None
    for m in range(LANES, cap + 1, LANES):
        if n % m == 0:
            best = m
    assert best is not None, (n, cap)
    return best


def _rowwise(name, fn, rows, vecs, row_outs, acc_outs=(), tl=256, side=None):
    L = rows[0][0].shape[0]
    tl = min(tl, L)
    assert L % tl == 0
    n_in = len(rows) + len(vecs)
    n_ro, n_ao = len(row_outs), len(acc_outs)
    n_si = len(side.arrays) if side else 0
    n_so = len(side.out_shapes) if side else 0

    def body(*refs):
        ins, s_in = refs[:n_in], refs[n_in:n_in + n_si]
        outs = refs[n_in + n_si:]
        ro, ao, s_out, sems = outs[:n_ro], outs[n_ro:n_ro + n_ao], outs[n_ro + n_ao:n_ro + n_ao + n_so], outs[n_ro + n_ao + n_so:]
        if side:
            @pl.when(pl.program_id(0) == 0)
            def _():
                side.start(s_in, s_out, sems)

        _rows(ins, ro, ao)
        if side:
            @pl.when(pl.program_id(0) == L // tl - 1)
            def _():
                side.finish(s_in, s_out, sems)

    def _rows(ins, ro, ao):
        vals = fn(*[r[...] for r in ins])
        if not isinstance(vals, (tuple, list)):
            vals = (vals,)
        for o, v in zip(ro, vals[:n_ro]):
            o[...] = v.astype(o.dtype)
        if ao:
            @pl.when(pl.program_id(0) == 0)
            def _():
                for o in ao:
                    o[...] = jnp.zeros_like(o)
            for o, v in zip(ao, vals[n_ro:]):
                o[...] += v.astype(F32)

    in_specs = [pl.BlockSpec((tl, w), functools.partial(lambda i, cb: (i, cb), cb=cb)) for _, w, cb in rows]
    in_specs += [pl.BlockSpec(v.shape, lambda i: (0, 0)) for v in vecs]
    out_specs = [pl.BlockSpec((tl, w), lambda i: (i, 0)) for w, _ in row_outs]
    out_specs += [pl.BlockSpec(s, lambda i: (0, 0)) for s in acc_outs]
    out_shape = [jax.ShapeDtypeStruct((L, w), dt) for w, dt in row_outs]
    out_shape += [jax.ShapeDtypeStruct(s, F32) for s in acc_outs]
    any_spec = pl.BlockSpec(memory_space=pl.ANY)
    return pl.pallas_call(
        body, name=name, grid=(L // tl,), in_specs=in_specs + [any_spec] * n_si, out_specs=out_specs + [any_spec] * n_so,
        out_shape=out_shape + (side.out_shapes if side else []), scratch_shapes=side.scratch if side else [],
        compiler_params=_params(("arbitrary",)),
    )(*[a for a, _, _ in rows], *vecs, *(side.arrays if side else []))


def _mm(name, a, b, mode, out_dtype, tm=1024, tn_cap=1408, tk_cap=2816, side=None):
    if mode == "nn":
        (M, K), (_, N) = a.shape, b.shape
    elif mode == "nt":
        (M, K), (N, _) = a.shape, b.shape
    else:
        (K, M), (_, N) = a.shape, b.shape
        tm, tk_cap = 1408, 2048
    tm = _pick(M, tm)
    tn = _pick(N, tn_cap)
    tk = _pick(K, tk_cap)
    nk = K // tk
    grid = (M // tm, N // tn, nk)
    n_si = len(side.arrays) if side else 0
    n_so = len(side.out_shapes) if side else 0
    n_acc = 1 if nk > 1 else 0

    def body(a_ref, b_ref, *rest):
        s_in, o_ref, s_out = rest[:n_si], rest[n_si], rest[n_si + 1:n_si + 1 + n_so]
        scr = rest[n_si + 1 + n_so:]
        if side:
            at = [pl.program_id(d) for d in range(3)]
            is_first = jnp.logical_and(jnp.logical_and(at[0] == 0, at[1] == 0), at[2] == 0)
            is_last = jnp.logical_and(jnp.logical_and(at[0] == grid[0] - 1, at[1] == grid[1] - 1), at[2] == grid[2] - 1)

            @pl.when(is_first)
            def _():
                side.start(s_in, s_out, scr[n_acc:])

        _product(a_ref, b_ref, o_ref, scr)
        if side:
            @pl.when(is_last)
            def _():
                side.finish(s_in, s_out, scr[n_acc:])

    def _product(a_ref, b_ref, o_ref, scr):
        if mode == "nn":
            p = jnp.dot(a_ref[...], b_ref[...], preferred_element_type=F32)
        elif mode == "nt":
            p = lax.dot_general(a_ref[...], b_ref[...], (((1,), (1,)), ((), ())), preferred_element_type=F32)
        else:
            p = lax.dot_general(a_ref[...], b_ref[...], (((0,), (0,)), ((), ())), preferred_element_type=F32)
        if nk == 1:
            o_ref[...] = p.astype(o_ref.dtype)
        else:
            acc = scr[0]
            k = pl.program_id(2)

            @pl.when(k == 0)
            def _():
                acc[...] = p

            @pl.when(k > 0)
            def _():
                acc[...] += p

            @pl.when(k == nk - 1)
            def _():
                o_ref[...] = acc[...].astype(o_ref.dtype)

    if mode == "nn":
        a_spec = pl.BlockSpec((tm, tk), lambda i, j, k: (i, k))
        b_spec = pl.BlockSpec((tk, tn), lambda i, j, k: (k, j))
    elif mode == "nt":
        a_spec = pl.BlockSpec((tm, tk), lambda i, j, k: (i, k))
        b_spec = pl.BlockSpec((tn, tk), lambda i, j, k: (j, k))
    else:
        a_spec = pl.BlockSpec((tk, tm), lambda i, j, k: (k, i))
        b_spec = pl.BlockSpec((tk, tn), lambda i, j, k: (k, j))
    any_spec = pl.BlockSpec(memory_space=pl.ANY)
    res = pl.pallas_call(
        body, name=name, grid=grid, in_specs=[a_spec, b_spec] + [any_spec] * n_si,
        out_specs=[pl.BlockSpec((tm, tn), lambda i, j, k: (i, j))] + [any_spec] * n_so,
        out_shape=[jax.ShapeDtypeStruct((M, N), out_dtype)] + (side.out_shapes if side else []),
        scratch_shapes=([pltpu.VMEM((tm, tn), F32)] if nk > 1 else []) + (side.scratch if side else []),
        compiler_params=_params(("arbitrary", "arbitrary", "arbitrary")),
    )(a, b, *(side.arrays if side else []))
    return res if side else res[0]


def _mm_epi(name, a, b, mode, tn, extras, epi, outs, tm=512, side=None):
    if mode == "nn":
        (M, K), (_, N) = a.shape, b.shape
    else:
        (M, K), (N, _) = a.shape, b.shape
    tm = _pick(M, tm)
    grid = (N // tn, M // tm)
    n_ex, n_out = len(extras), len(outs)
    n_si = len(side.arrays) if side else 0
    n_so = len(side.out_shapes) if side else 0

    def body(*refs):
        a_ref, b_ref, ex = refs[0], refs[1], refs[2:2 + n_ex]
        s_in = refs[2 + n_ex:2 + n_ex + n_si]
        o_refs = refs[2 + n_ex + n_si:2 + n_ex + n_si + n_out]
        s_out = refs[2 + n_ex + n_si + n_out:2 + n_ex + n_si + n_out + n_so]
        sems = refs[2 + n_ex + n_si + n_out + n_so:]
        if side:
            @pl.when(jnp.logical_and(pl.program_id(0) == 0, pl.program_id(1) == 0))
            def _():
                side.start(s_in, s_out, sems)

        if mode == "nn":
            p = jnp.dot(a_ref[...], b_ref[...], preferred_element_type=F32)
        else:
            p = lax.dot_general(a_ref[...], b_ref[...], (((1,), (1,)), ((), ())), preferred_element_type=F32)
        for o, v in zip(o_refs, epi(p, *[r[...] for r in ex])):
            o[...] = v.astype(o.dtype)
        if side:
            @pl.when(jnp.logical_and(pl.program_id(0) == grid[0] - 1, pl.program_id(1) == grid[1] - 1))
            def _():
                side.finish(s_in, s_out, sems)

    any_spec = pl.BlockSpec(memory_space=pl.ANY)
    a_spec = pl.BlockSpec((tm, K), lambda j, i: (i, 0))
    b_spec = pl.BlockSpec((K, tn), lambda j, i: (0, j)) if mode == "nn" else pl.BlockSpec((tn, K), lambda j, i: (j, 0))
    return pl.pallas_call(
        body, name=name, grid=grid,
        in_specs=[a_spec, b_spec]
        + [pl.BlockSpec(e.shape, lambda j, i: (0, 0)) if w is None else pl.BlockSpec((tm, w), lambda j, i: (i, j))
           for e, w in extras] + [any_spec] * n_si,
        out_specs=[pl.BlockSpec((tm, w), lambda j, i: (i, j)) for w, _ in outs] + [any_spec] * n_so,
        out_shape=[jax.ShapeDtypeStruct((M, (N // tn) * w), dt) for w, dt in outs] + (side.out_shapes if side else []),
        scratch_shapes=side.scratch if side else [],
        compiler_params=_params(("arbitrary", "arbitrary")),
    )(a, b, *[e for e, _ in extras], *(side.arrays if side else []))


def _f(x):
    return x.astype(F32)


def _silu(x):
    return x * jax.nn.sigmoid(x)


def _softplus(x):
    return jnp.maximum(x, 0.0) + jnp.log1p(jnp.exp(-jnp.abs(x)))


def _rms(x, g):
    r = lax.rsqrt(jnp.mean(x * x, axis=-1, keepdims=True) + NORM_EPS)
    return x * r * g


def _adaln(x, g, scale, shift):
    return _rms(x, g) * (1.0 + scale) + shift


def _resid(x, y, gate, g):
    return x + gate * _rms(y, g)


def _mid(x, y, gate, g_post, g_pre, scale, shift):
    x_new = _resid(x, y, gate, g_post)
    return x_new, _adaln(x_new, g_pre, scale, shift)


def _swiglu(gt, up):
    return _silu(gt) * up


def _ssm_post(y_ssd, pre_xs, z, d_full, norm_w):
    y = (y_ssd + _silu(pre_xs) * d_full) * _silu(z)
    half = SSM_INNER // 2
    parts = []
    for g in range(2):
        yg = y[:, g * half:(g + 1) * half]
        parts.append(yg * lax.rsqrt(jnp.mean(yg * yg, axis=-1, keepdims=True) + NORM_EPS))
    return jnp.concatenate(parts, axis=1) * norm_w


def _first_fwd(x, vecs):
    return _rowwise("adaln_first", lambda x, g, sc, sh: _adaln(x, g, sc, sh),
                    [(x, D_MODEL, 0)], vecs, [(D_MODEL, BF16)], tl=512)[0]


def _mid_bwd(name, x, y, dx_new, dh, vecs):
    def fn(x, y, dxn, dh, *v):
        _, vjp = jax.vjp(_mid, x, _f(y), *v)
        return vjp((dxn, _f(dh)))

    vec = (1, D_MODEL)
    return _rowwise(name, fn, [(x, D_MODEL, 0), (y, D_MODEL, 0), (dx_new, D_MODEL, 0), (dh, D_MODEL, 0)], vecs,
                    [(D_MODEL, F32), (D_MODEL, BF16)], [vec] * 5)


def _first_bwd(x, dx_in, dh, vecs, side=None):
    def fn(x, dxi, dh, *v):
        _, vjp = jax.vjp(_adaln, x, *v)
        dx, dg, dsc, dsh = vjp(_f(dh))
        return dx + dxi, dg, dsc, dsh

    vec = (1, D_MODEL)
    return _rowwise("adaln_first_bwd", fn, [(x, D_MODEL, 0), (dx_in, D_MODEL, 0), (dh, D_MODEL, 0)], vecs,
                    [(D_MODEL, F32)], [vec] * 3, side=side)


def _last_bwd(x1, f, target, vecs):
    def fn(x1, f, t, gate, g):
        x2, vjp = jax.vjp(_resid, x1, _f(f), gate, g)
        err = x2 - t
        dx1, df, dgate, dg = vjp(err * (1.0 / D_MODEL))
        loss_cols = jnp.sum(err * err, axis=0, keepdims=True) * (0.5 / D_MODEL)
        return dx1, df, dgate, dg, loss_cols

    vec = (1, D_MODEL)
    return _rowwise("loss_last_bwd", fn, [(x1, D_MODEL, 0), (f, D_MODEL, 0), (target, D_MODEL, 0)], vecs,
                    [(D_MODEL, F32), (D_MODEL, BF16)], [vec] * 3)


HALO = 16


def _shift_down(u, prev, k):
    rows = lax.broadcasted_iota(jnp.int32, u.shape, 0)
    v = pltpu.roll(u, k, 0)
    for t in range(k):
        v = jnp.where(rows == t, prev[HALO - k + t:HALO - k + t + 1, :], v)
    return v


def _shift_up(u, nxt, k):
    n = u.shape[0]
    rows = lax.broadcasted_iota(jnp.int32, u.shape, 0)
    v = pltpu.roll(u, n - k, 0)
    for t in range(k):
        v = jnp.where(rows == n - k + t, nxt[t:t + 1, :], v)
    return v


def _conv_specs(L, tl, width, col_block):
    per = tl // HALO
    last = L // HALO - 1
    main = pl.BlockSpec((tl, width), lambda i: (i, col_block))
    before = pl.BlockSpec((HALO, width), lambda i: (jnp.maximum(i * per - 1, 0), col_block))
    after = pl.BlockSpec((HALO, width), lambda i: (jnp.minimum((i + 1) * per, last), col_block))
    return main, before, after


def _shortconv_fwd(P, w, tl=512):
    L = P.shape[0]
    tl = min(tl, L)
    C = SC_WIDTH
    main, before, _ = _conv_specs(L, tl, 3 * C, 0)

    def body(p_ref, h_ref, w_ref, o_ref):
        first = (pl.program_id(0) == 0)
        p, h = _f(p_ref[...]), _f(h_ref[...])
        b, u = p[:, :C], p[:, C:2 * C] * p[:, 2 * C:]
        uh = jnp.where(first, 0.0, h[:, C:2 * C] * h[:, 2 * C:])
        wv = w_ref[...]
        cv = wv[2:3] * u + wv[1:2] * _shift_down(u, uh, 1) + wv[0:1] * _shift_down(u, uh, 2)
        o_ref[...] = (b * cv).astype(o_ref.dtype)

    return pl.pallas_call(
        body, name="shortconv_fwd", grid=(L // tl,),
        in_specs=[main, before, pl.BlockSpec(w.shape, lambda i: (0, 0))],
        out_specs=pl.BlockSpec((tl, C), lambda i: (i, 0)),
        out_shape=jax.ShapeDtypeStruct((L, C), BF16), compiler_params=_params(("arbitrary",)),
    )(P, P, w)


def _shortconv_bwd(P, dya, w, tl=512):
    L = P.shape[0]
    tl = min(tl, L)
    C = SC_WIDTH
    main, before, after = _conv_specs(L, tl, 3 * C, 0)
    dmain, _, dafter = _conv_specs(L, tl, C, 0)
    n = L // tl

    def body(p_ref, h_ref, n_ref, d_ref, dn_ref, w_ref, o_ref, dw0, dw1, dw2):
        i = pl.program_id(0)
        p, h, nx = _f(p_ref[...]), _f(h_ref[...]), _f(n_ref[...])
        b, c, x = p[:, :C], p[:, C:2 * C], p[:, 2 * C:]
        u = c * x
        uh = jnp.where(i == 0, 0.0, h[:, C:2 * C] * h[:, 2 * C:])
        u1, u2 = _shift_down(u, uh, 1), _shift_down(u, uh, 2)
        wv = w_ref[...]
        cv = wv[2:3] * u + wv[1:2] * u1 + wv[0:1] * u2
        dy = _f(d_ref[...])
        dcv = dy * b
        dcv_n = jnp.where(i == n - 1, 0.0, _f(dn_ref[...]) * nx[:, :C])
        du = wv[2:3] * dcv + wv[1:2] * _shift_up(dcv, dcv_n, 1) + wv[0:1] * _shift_up(dcv, dcv_n, 2)
        o_ref[:, :C] = (dy * cv).astype(o_ref.dtype)
        o_ref[:, C:2 * C] = (du * x).astype(o_ref.dtype)
        o_ref[:, 2 * C:] = (du * c).astype(o_ref.dtype)

        @pl.when(i == 0)
        def _():
            for r in (dw0, dw1, dw2):
                r[...] = jnp.zeros_like(r)

        dw0[...] += jnp.sum(dcv * u2, axis=0, keepdims=True)
        dw1[...] += jnp.sum(dcv * u1, axis=0, keepdims=True)
        dw2[...] += jnp.sum(dcv * u, axis=0, keepdims=True)

    vec = pl.BlockSpec((1, C), lambda i: (0, 0))
    return pl.pallas_call(
        body, name="shortconv_bwd", grid=(n,),
        in_specs=[main, before, after, dmain, dafter, pl.BlockSpec(w.shape, lambda i: (0, 0))],
        out_specs=[pl.BlockSpec((tl, 3 * C), lambda i: (i, 0)), vec, vec, vec],
        out_shape=[jax.ShapeDtypeStruct((L, 3 * C), BF16)] + [jax.ShapeDtypeStruct((1, C), F32)] * 3,
        compiler_params=_params(("arbitrary",)),
    )(P, P, P, dya, dya, w)


def _ssmconv_fwd(P, w, bias, tl=512):
    L = P.shape[0]
    tl = min(tl, L)
    C = SSM_CONV_DIM
    main, before, _ = _conv_specs(L, tl, C, P_XBC // C)

    def body(p_ref, h_ref, w_ref, b_ref, o_ref):
        u = _f(p_ref[...])
        uh = jnp.where(pl.program_id(0) == 0, 0.0, _f(h_ref[...]))
        wv = w_ref[...]
        acc = wv[3:4] * u + b_ref[...]
        for k in range(1, SSM_CONV):
            acc = acc + wv[3 - k:4 - k] * _shift_down(u, uh, k)
        o_ref[...] = acc

    return pl.pallas_call(
        body, name="ssmconv_fwd", grid=(L // tl,),
        in_specs=[main, before, pl.BlockSpec(w.shape, lambda i: (0, 0)), pl.BlockSpec(bias.shape, lambda i: (0, 0))],
        out_specs=pl.BlockSpec((tl, C), lambda i: (i, 0)),
        out_shape=jax.ShapeDtypeStruct((L, C), F32), compiler_params=_params(("arbitrary",)),
    )(P, P, w, bias)


def _ssmconv_bwd(P, dpre, w, tl=512):
    L = P.shape[0]
    tl = min(tl, L)
    C = SSM_CONV_DIM
    main, before, _ = _conv_specs(L, tl, C, P_XBC // C)
    dmain, _, dafter = _conv_specs(L, tl, C, 0)
    n = L // tl

    def body(p_ref, h_ref, d_ref, dn_ref, w_ref, o_ref, dw0, dw1, dw2, dw3, db):
        i = pl.program_id(0)
        u = _f(p_ref[...])
        uh = jnp.where(i == 0, 0.0, _f(h_ref[...]))
        d = d_ref[...]
        dn = jnp.where(i == n - 1, 0.0, dn_ref[...])
        wv = w_ref[...]
        du = wv[3:4] * d
        for k in range(1, SSM_CONV):
            du = du + wv[3 - k:4 - k] * _shift_up(d, dn, k)
        o_ref[...] = du.astype(o_ref.dtype)

        @pl.when(i == 0)
        def _():
            for r in (dw0, dw1, dw2, dw3, db):
                r[...] = jnp.zeros_like(r)

        for k, r in ((3, dw0), (2, dw1), (1, dw2)):
            r[...] += jnp.sum(d * _shift_down(u, uh, k), axis=0, keepdims=True)
        dw3[...] += jnp.sum(d * u, axis=0, keepdims=True)
        db[...] += jnp.sum(d, axis=0, keepdims=True)

    vec = pl.BlockSpec((1, C), lambda i: (0, 0))
    return pl.pallas_call(
        body, name="ssmconv_bwd", grid=(n,),
        in_specs=[main, before, dmain, dafter, pl.BlockSpec(w.shape, lambda i: (0, 0))],
        out_specs=[pl.BlockSpec((tl, C), lambda i: (i, 0))] + [vec] * 5,
        out_shape=[jax.ShapeDtypeStruct((L, C), BF16)] + [jax.ShapeDtypeStruct((1, C), F32)] * 5,
        compiler_params=_params(("arbitrary",)),
    )(P, P, dpre, dpre, w)


def _dot_nt(a, b):
    return lax.dot_general(a, b, (((1,), (1,)), ((), ())), preferred_element_type=F32)


def _dot_tn(a, b):
    return lax.dot_general(a, b, (((0,), (0,)), ((), ())), preferred_element_type=F32)


def _split3(x):
    hi = x.astype(BF16)
    r = x - hi.astype(F32)
    mid = r.astype(BF16)
    return hi, mid, (r - mid.astype(F32)).astype(BF16)


@jax.custom_vjp
def _xm01(x, m):
    return sum(jnp.dot(t, m, preferred_element_type=F32) for t in _split3(x))


def _xm01_fwd(x, m):
    return _xm01(x, m), m


def _xm01_bwd(m, g):
    return sum(_dot_nt(t, m) for t in _split3(g)), jnp.zeros_like(m)


_xm01.defvjp(_xm01_fwd, _xm01_bwd)


@jax.custom_vjp
def _m01x(m, x):
    return sum(jnp.dot(m, t, preferred_element_type=F32) for t in _split3(x))


def _m01x_fwd(m, x):
    return _m01x(m, x), m


def _m01x_bwd(m, g):
    return jnp.zeros_like(m), sum(_dot_tn(m, t) for t in _split3(g))


_m01x.defvjp(_m01x_fwd, _m01x_bwd)


def _ssd_chunk(pre, dtr, s_prev, dtb, alog):
    T = pre.shape[0]
    act = _silu(pre)
    xs, bm, cm = act[:, :SSM_INNER], act[:, SSM_INNER:SSM_INNER + 128], act[:, SSM_INNER + 128:]
    lane = lax.broadcasted_iota(jnp.int32, (1, LANES), 1)
    dt = jnp.where(lane < SSM_HEADS, _softplus(dtr + dtb), 0.0)
    a = dt * (-jnp.exp(alog))
    ri = lax.broadcasted_iota(jnp.int32, (T, T), 0)
    ci = lax.broadcasted_iota(jnp.int32, (T, T), 1)
    causal = ci <= ri
    a_cs = _m01x(causal.astype(BF16), a)
    eh = lax.broadcasted_iota(jnp.int32, (LANES, SSM_INNER), 0)
    ej = lax.broadcasted_iota(jnp.int32, (LANES, SSM_INNER), 1)
    expand = (lax.shift_right_logical(ej, 6) == eh).astype(BF16)
    dt_full = _xm01(dt, expand)
    acs_full = _xm01(a_cs, expand)
    alast_full = acs_full[T - 1:T, :]
    xdt = xs * dt_full
    a_cs_t = a_cs.T
    ys, s_new = [], []
    for g in range(2):
        in_group = lax.shift_right_logical(lane, 6) == g
        cg = jnp.where(in_group, cm, 0.0).astype(BF16)
        bg = jnp.where(in_group, bm, 0.0).astype(BF16)
        scores = _dot_nt(cg, bg)
        for pp in range(2):
            hp = 2 * g + pp
            cols = slice(hp * LANES, (hp + 1) * LANES)
            xp, acsp = xdt[:, cols], acs_full[:, cols]
            per_head = []
            for hh in range(2):
                h = 2 * hp + hh
                decay = jnp.exp(jnp.where(causal, a_cs[:, h:h + 1] - a_cs_t[h:h + 1, :], -jnp.inf))
                per_head.append(jnp.dot((scores * decay).astype(BF16), xp.astype(BF16), preferred_element_type=F32))
            y_diag = jnp.where(lane < SSM_STATE, per_head[0], per_head[1])
            sp = s_prev[hp * LANES:(hp + 1) * LANES, :]
            y_off = jnp.dot(cg, sp.astype(BF16), preferred_element_type=F32) * jnp.exp(acsp)
            ys.append(y_diag + y_off)
            to_end = jnp.exp(alast_full[:, cols] - acsp)
            s_new.append(sp * jnp.exp(alast_full[:, cols]) + _dot_tn(bg, (xp * to_end).astype(BF16)))
    return jnp.concatenate(ys, axis=1), jnp.concatenate(s_new, axis=0)


def _ssd_fwd(pre, P, dtb, alog):
    L = pre.shape[0]
    T = min(SSM_CHUNK, L)
    nc = L // T

    def body(pre_ref, dt_ref, dtb_ref, al_ref, y_ref, st_ref, s_scr):
        @pl.when(pl.program_id(0) == 0)
        def _():
            s_scr[...] = jnp.zeros_like(s_scr)

        st_ref[0] = s_scr[...]
        y, s = _ssd_chunk(pre_ref[...], _f(dt_ref[...]), s_scr[...], dtb_ref[...], al_ref[...])
        y_ref[...] = y
        s_scr[...] = s

    vec = pl.BlockSpec((1, LANES), lambda i: (0, 0))
    return pl.pallas_call(
        body, name="ssd_fwd", grid=(nc,),
        in_specs=[pl.BlockSpec((T, SSM_CONV_DIM), lambda i: (i, 0)), pl.BlockSpec((T, LANES), lambda i: (i, P_DT // LANES)),
                  vec, vec],
        out_specs=[pl.BlockSpec((T, SSM_INNER), lambda i: (i, 0)), pl.BlockSpec((1, 512, LANES), lambda i: (i, 0, 0))],
        out_shape=[jax.ShapeDtypeStruct((L, SSM_INNER), F32), jax.ShapeDtypeStruct((nc, 512, LANES), F32)],
        scratch_shapes=[pltpu.VMEM((512, LANES), F32)], compiler_params=_params(("arbitrary",)),
    )(pre, P, dtb, alog)


def _ssd_bwd(pre, P, states, dy, dxs_extra, dtb, alog):
    L = pre.shape[0]
    T = min(SSM_CHUNK, L)
    nc = L // T

    def body(pre_ref, dt_ref, st_ref, dy_ref, dx_ref, dtb_ref, al_ref, dpre_ref, ddt_ref, ddtb_ref, dal_ref, ds_scr):
        @pl.when(pl.program_id(0) == 0)
        def _():
            ds_scr[...] = jnp.zeros_like(ds_scr)
            ddtb_ref[...] = jnp.zeros_like(ddtb_ref)
            dal_ref[...] = jnp.zeros_like(dal_ref)

        _, vjp = jax.vjp(_ssd_chunk, pre_ref[...], _f(dt_ref[...]), st_ref[0], dtb_ref[...], al_ref[...])
        dpre, ddt, ds, ddtb, dal = vjp((dy_ref[...], ds_scr[...]))
        dpre_ref[:, :SSM_INNER] = dpre[:, :SSM_INNER] + dx_ref[...]
        dpre_ref[:, SSM_INNER:] = dpre[:, SSM_INNER:]
        ddt_ref[:, :LANES] = ddt.astype(ddt_ref.dtype)
        ddt_ref[:, LANES:] = jnp.zeros((T, DT_PAD - LANES), ddt_ref.dtype)
        ds_scr[...] = ds
        ddtb_ref[...] += ddtb
        dal_ref[...] += dal

    vec = pl.BlockSpec((1, LANES), lambda i: (0, 0))
    rev = lambda i: (nc - 1 - i, 0)
    return pl.pallas_call(
        body, name="ssd_bwd", grid=(nc,),
        in_specs=[pl.BlockSpec((T, SSM_CONV_DIM), rev), pl.BlockSpec((T, LANES), lambda i: (nc - 1 - i, P_DT // LANES)),
                  pl.BlockSpec((1, 512, LANES), lambda i: (nc - 1 - i, 0, 0)),
                  pl.BlockSpec((T, SSM_INNER), rev), pl.BlockSpec((T, SSM_INNER), rev), vec, vec],
        out_specs=[pl.BlockSpec((T, SSM_CONV_DIM), rev), pl.BlockSpec((T, DT_PAD), rev), vec, vec],
        out_shape=[jax.ShapeDtypeStruct((L, SSM_CONV_DIM), F32), jax.ShapeDtypeStruct((L, DT_PAD), BF16),
                   jax.ShapeDtypeStruct((1, LANES), F32), jax.ShapeDtypeStruct((1, LANES), F32)],
        scratch_shapes=[pltpu.VMEM((512, LANES), F32)], compiler_params=_params(("arbitrary",)),
    )(pre, P, states, dy, dxs_extra, dtb, alog)


def _sb_scores(qm, kb, later, strict, mask):
    z = _dot_nt(qm, kb)
    lk = jnp.minimum(-z, 0.0) - jnp.log(1.0 + jnp.exp(-jnp.abs(z)))
    if mask is not None:
        lk = jnp.where(mask, lk, 0.0)
    log_a = z + lk + jnp.dot(lk.astype(BF16), strict, preferred_element_type=F32) + later
    if mask is not None:
        log_a = jnp.where(mask, log_a, -jnp.inf)
    return z, lk, log_a


def _dot_split(x, m):
    hi = x.astype(BF16)
    lo = (x - hi.astype(F32)).astype(BF16)
    return jnp.dot(hi, m, preferred_element_type=F32) + jnp.dot(lo, m, preferred_element_type=F32)


def _sb_setup(q_ref, i, tq, tk):
    lane = lax.broadcasted_iota(jnp.int32, (1, LANES), 1)
    first = lane < SB_HEAD_DIM
    q = q_ref[...] * (SB_HEAD_DIM ** -0.5)
    qms = (jnp.where(first, q, jnp.zeros_like(q)), jnp.where(first, jnp.zeros_like(q), q))
    j0 = lax.div(i * tq, tk)
    ri = lax.broadcasted_iota(jnp.int32, (tq, tk), 0)
    ci = lax.broadcasted_iota(jnp.int32, (tq, tk), 1)
    diag_mask = (ci + (j0 * tk - i * tq)) < ri
    kr = lax.broadcasted_iota(jnp.int32, (tk, tk), 0)
    kc = lax.broadcasted_iota(jnp.int32, (tk, tk), 1)
    strict = (kr > kc).astype(BF16)
    return first, qms, j0, diag_mask, strict


def _sb_continue(c):
    return jnp.logical_and(c[0] >= 0, jnp.maximum(jnp.max(c[1][0]), jnp.max(c[1][1])) > SB_LOG_CUTOFF)


def _sb_fwd(P):
    L = P.shape[0]
    tq, tk = min(SB_TQ, L), min(SB_TK, L)
    nq = L // tq
    qb = P_B // LANES

    def body(q_ref, k_ref, v_ref, o_ref, of_ref):
        i = pl.program_id(1)
        first, qms, j0, diag_mask, strict = _sb_setup(q_ref, i, tq, tk)

        def tile(h, j, later, acc, mask=None, valid=None):
            off = pl.multiple_of(j * tk, tk)
            gate = later if valid is None else jnp.where(valid, later, -jnp.inf)
            _, lk, log_a = _sb_scores(qms[h], k_ref[pl.ds(off, tk), :], gate, strict, mask)
            acc = acc + jnp.dot(jnp.exp(log_a).astype(BF16), v_ref[pl.ds(off, tk), :], preferred_element_type=F32)
            rows = jnp.sum(lk, axis=1, keepdims=True)
            return later + (rows if valid is None else jnp.where(valid, rows, 0.0)), acc

        zero, zacc = jnp.zeros((tq, 1), F32), jnp.zeros((tq, LANES), F32)
        state = []
        for h in range(2):
            later, acc = tile(h, j0, zero, zacc, mask=diag_mask)
            state.append(tile(h, jnp.maximum(j0 - 1, 0), later, acc, valid=j0 >= 1))

        def tail(c):
            res = [tile(h, c[0], c[1][h], c[2][h]) for h in range(2)]
            return c[0] - 1, (res[0][0], res[1][0]), (res[0][1], res[1][1])

        _, _, accs = lax.while_loop(_sb_continue, tail, (j0 - 2, (state[0][0], state[1][0]), (state[0][1], state[1][1])))
        out = jnp.where(first, accs[0], accs[1])
        o_ref[...] = out.astype(o_ref.dtype)
        of_ref[...] = out

    tile_spec = pl.BlockSpec((tq, LANES), lambda p, i: (i, p))
    return pl.pallas_call(
        body, name="sb_fwd", grid=(2, nq),
        in_specs=[pl.BlockSpec((tq, LANES), lambda p, i: (i, qb + p)),
                  pl.BlockSpec((L, LANES), lambda p, i: (0, qb + 2 + p)),
                  pl.BlockSpec((L, LANES), lambda p, i: (0, qb + 4 + p))],
        out_specs=[tile_spec, tile_spec],
        out_shape=[jax.ShapeDtypeStruct((L, 2 * LANES), BF16), jax.ShapeDtypeStruct((L, 2 * LANES), F32)],
        compiler_params=_params(("arbitrary", "arbitrary")),
    )(P, P, P)


def _sb_bwd(P, dyb, yb32):
    L = P.shape[0]
    tq, tk = min(SB_TQ, L), min(SB_TK, L)
    nq = L // tq
    qb = P_B // LANES

    def body(q_ref, k_ref, v_ref, do_ref, of_ref, dq_ref, dk_ref, dv_ref):
        i = pl.program_id(1)
        first, qms, j0, diag_mask, strict = _sb_setup(q_ref, i, tq, tk)

        @pl.when(i == 0)
        def _():
            dk_ref[...] = jnp.zeros_like(dk_ref)
            dv_ref[...] = jnp.zeros_like(dv_ref)

        do = do_ref[...]
        doms = (jnp.where(first, do, jnp.zeros_like(do)), jnp.where(first, jnp.zeros_like(do), do))
        prod = _f(do) * of_ref[...]
        totals = (jnp.sum(jnp.where(first, prod, 0.0), axis=1, keepdims=True),
                  jnp.sum(jnp.where(first, 0.0, prod), axis=1, keepdims=True))

        def tile(h, j, later, later_g, acc, mask=None, valid=None):
            off = pl.multiple_of(j * tk, tk)
            kb, vb = k_ref[pl.ds(off, tk), :], v_ref[pl.ds(off, tk), :]
            gate = later if valid is None else jnp.where(valid, later, -jnp.inf)
            z, lk, log_a = _sb_scores(qms[h], kb, gate, strict, mask)
            att = jnp.exp(log_a).astype(BF16)
            g = _f(att) * _dot_nt(doms[h], vb)
            before = totals[h] - later_g
            if valid is not None:
                before = jnp.where(valid, before, 0.0)
            dz = g - (before - _dot_split(g, strict)) * jnp.exp(z + lk)
            if mask is not None:
                dz = jnp.where(mask, dz, 0.0)
            dzb = dz.astype(BF16)
            rows = jnp.sum(lk, axis=1, keepdims=True)
            carry = (later + (rows if valid is None else jnp.where(valid, rows, 0.0)),
                     later_g + jnp.sum(g, axis=1, keepdims=True), acc + jnp.dot(dzb, kb, preferred_element_type=F32))
            return carry, _dot_tn(dzb, qms[h]), _dot_tn(att, doms[h])

        def tail(c):
            off = pl.multiple_of(c[0] * tk, tk)
            (c0, dk0, dv0), (c1, dk1, dv1) = [tile(h, c[0], c[1][h], c[2][h], c[3][h]) for h in range(2)]
            dk_ref[pl.ds(off, tk), :] += dk0 + dk1
            dv_ref[pl.ds(off, tk), :] += dv0 + dv1
            return (c[0] - 1,) + tuple(zip(c0, c1))

        zero, zacc = jnp.zeros((tq, 1), F32), jnp.zeros((tq, LANES), F32)
        jp = jnp.maximum(j0 - 1, 0)
        carries, dks, dvs = [], [], []
        for h in range(2):
            carry, dk0, dv0 = tile(h, j0, zero, zero, zacc, mask=diag_mask)
            carry, dk1, dv1 = tile(h, jp, *carry, valid=j0 >= 1)
            carries.append(carry)
            dks.append((dk0, dk1))
            dvs.append((dv0, dv1))
        for n, j in enumerate((j0, jp)):
            off = pl.multiple_of(j * tk, tk)
            dk_ref[pl.ds(off, tk), :] += dks[0][n] + dks[1][n]
            dv_ref[pl.ds(off, tk), :] += dvs[0][n] + dvs[1][n]
        accs = lax.while_loop(_sb_continue, tail, (j0 - 2,) + tuple(zip(carries[0], carries[1])))[3]
        dq_ref[...] = jnp.where(first, accs[0], accs[1]) * (SB_HEAD_DIM ** -0.5)

    full = pl.BlockSpec((L, LANES), lambda p, i: (0, p))
    tile_spec = pl.BlockSpec((tq, LANES), lambda p, i: (i, p))
    return pl.pallas_call(
        body, name="sb_bwd", grid=(2, nq),
        in_specs=[pl.BlockSpec((tq, LANES), lambda p, i: (i, qb + p)),
                  pl.BlockSpec((L, LANES), lambda p, i: (0, qb + 2 + p)),
                  pl.BlockSpec((L, LANES), lambda p, i: (0, qb + 4 + p)), tile_spec, tile_spec],
        out_specs=[tile_spec, full, full],
        out_shape=[jax.ShapeDtypeStruct((L, 2 * LANES), F32)] * 3,
        compiler_params=_params(("arbitrary", "arbitrary")),
    )(P, P, P, dyb, yb32)


MOD_SHARD = N_MOD * D_MODEL // N_CHIPS


def _mod_fwd(c_all, mod_w, mod_b_sh):
    tn = 512

    def body(c_ref, w_ref, b_ref, o_ref):
        o_ref[0] = jnp.dot(_silu(c_ref[...]), w_ref[0], precision=HIGHEST, preferred_element_type=F32) + b_ref[0]

    return pl.pallas_call(
        body, name="mod_fwd", grid=(DEPTH, MOD_SHARD // tn),
        in_specs=[pl.BlockSpec((N_DEV, D_MODEL), lambda l, j: (0, 0)),
                  pl.BlockSpec((1, D_MODEL, tn), lambda l, j: (l, 0, j)),
                  pl.BlockSpec((1, 1, tn), lambda l, j: (l, 0, j))],
        out_specs=pl.BlockSpec((1, N_DEV, tn), lambda l, j: (l, 0, j)),
        out_shape=jax.ShapeDtypeStruct((DEPTH, N_DEV, MOD_SHARD), F32),
        compiler_params=_params(("arbitrary", "arbitrary")),
    )(c_all, mod_w, mod_b_sh)


def _mod_bwd(c_all, dmod_sh):
    tn = 512

    def body(c_ref, d_ref, o_ref):
        o_ref[0] = lax.dot_general(_silu(c_ref[...]), d_ref[0], (((0,), (0,)), ((), ())), precision=HIGHEST,
                                   preferred_element_type=F32)

    return pl.pallas_call(
        body, name="mod_bwd", grid=(DEPTH, MOD_SHARD // tn),
        in_specs=[pl.BlockSpec((N_DEV, D_MODEL), lambda l, j: (0, 0)),
                  pl.BlockSpec((1, N_DEV, tn), lambda l, j: (l, 0, j))],
        out_specs=pl.BlockSpec((1, D_MODEL, tn), lambda l, j: (l, 0, j)),
        out_shape=jax.ShapeDtypeStruct((DEPTH, D_MODEL, MOD_SHARD), F32),
        compiler_params=_params(("arbitrary", "arbitrary")),
    )(c_all, dmod_sh)


def _row_tile(rows, cap):
    if rows <= cap:
        return rows
    best = None
    for t in range(8, cap + 1, 8):
        if rows % t == 0:
            best = t
    assert best is not None, (rows, cap)
    return best


def _adamw(name, w, gs, m, v, tr=256):
    R, W = w.shape
    by_layer = any(isinstance(t, tuple) for t in gs)
    tr = _row_tile(R // 2 if by_layer else R, tr)
    per = (R // 2) // tr

    flat, specs = [], []
    for t in gs:
        if isinstance(t, tuple):
            flat += list(t)
            specs += [pl.BlockSpec((tr, W), lambda i: (jnp.minimum(i, per - 1), 0)),
                      pl.BlockSpec((tr, W), lambda i: (jnp.maximum(i - per, 0), 0))]
        else:
            flat.append(t)
            specs.append(pl.BlockSpec((tr, W), lambda i: (i, 0)))
    ng = len(flat)

    def body(*refs):
        w_ref, g_refs, (m_ref, v_ref) = refs[0], list(refs[1:1 + ng]), refs[1 + ng:3 + ng]
        g_out, d_out, m_out, v_out = refs[3 + ng:]
        g = None
        for t in gs:
            if isinstance(t, tuple):
                lo, hi = g_refs.pop(0), g_refs.pop(0)
                term = jnp.where(pl.program_id(0) < per, lo[...], hi[...])
            else:
                term = g_refs.pop(0)[...]
            g = term if g is None else g + term
        mm = ADAM_B1 * m_ref[...] + (1.0 - ADAM_B1) * g
        vv = ADAM_B2 * v_ref[...] + (1.0 - ADAM_B2) * (g * g)
        m_hat = mm / (1.0 - ADAM_B1 ** ADAM_STEP)
        v_hat = vv / (1.0 - ADAM_B2 ** ADAM_STEP)
        g_out[...] = g
        d_out[...] = -ADAM_LR * (m_hat / (jnp.sqrt(v_hat) + ADAM_EPS) + ADAM_WD * w_ref[...])
        m_out[...] = mm
        v_out[...] = vv

    spec = pl.BlockSpec((tr, W), lambda i: (i, 0))
    return pl.pallas_call(
        body, name=name, grid=(R // tr,), in_specs=[spec] + specs + [spec, spec], out_specs=[spec] * 4,
        out_shape=[jax.ShapeDtypeStruct((R, W), F32)] * 4, compiler_params=_params(("arbitrary",)),
    )(w, *flat, m, v)


def _sum_slots(name, a, tr=256):
    n, R, W = a.shape
    tr = _row_tile(R, tr)

    def body(a_ref, o_ref):
        acc = _f(a_ref[0])
        for j in range(1, n):
            acc = acc + _f(a_ref[j])
        o_ref[...] = acc

    return pl.pallas_call(
        body, name=name, grid=(R // tr,), in_specs=[pl.BlockSpec((n, tr, W), lambda i: (0, i, 0))],
        out_specs=pl.BlockSpec((tr, W), lambda i: (i, 0)), out_shape=jax.ShapeDtypeStruct((R, W), F32),
        compiler_params=_params(("arbitrary",)),
    )(a)


def _here():
    return lax.axis_index("x"), lax.axis_index("y"), lax.axis_index("c")


def _flip(v, d):
    return 1 - v if d else v


def _allgather_small(name, buf):
    R = buf.shape[0]
    rel = [(dx, dy, dc) for dx in (0, 1) for dy in (0, 1) for dc in (0, 1)][1:]

    def body(x_ref, o_ref, send, recv, lsem):
        x, y, c = _here()
        me = 4 * x + 2 * y + c
        mine = pltpu.make_async_copy(x_ref, o_ref.at[me], lsem)
        mine.start()

        def copy(k, slot):
            dx, dy, dc = rel[k]
            return pltpu.make_async_remote_copy(
                src_ref=x_ref, dst_ref=o_ref.at[slot], send_sem=send.at[k], recv_sem=recv.at[k],
                device_id=(_flip(x, dx), _flip(y, dy), _flip(c, dc)), device_id_type=MESH_ID)

        sent = [copy(k, me) for k in range(len(rel))]
        for cp in sent:
            cp.start()
        for k, (dx, dy, dc) in enumerate(rel):
            copy(k, 4 * _flip(x, dx) + 2 * _flip(y, dy) + _flip(c, dc)).wait_recv()
        for cp in sent:
            cp.wait_send()
        mine.wait()

    return pl.pallas_call(
        body, name=name, out_shape=jax.ShapeDtypeStruct((N_DEV, R, LANES), F32),
        in_specs=[pl.BlockSpec(memory_space=pltpu.VMEM)], out_specs=pl.BlockSpec(memory_space=pltpu.VMEM),
        scratch_shapes=[pltpu.SemaphoreType.DMA((7,)), pltpu.SemaphoreType.DMA((7,)), pltpu.SemaphoreType.DMA],
    )(buf)


CHIP_REL = [(1, 0), (0, 1), (1, 1)]


class _Side:
    def __init__(self, arrays, out_shapes, scratch, start, finish):
        self.arrays, self.out_shapes, self.scratch, self.start, self.finish = arrays, out_shapes, scratch, start, finish


def _chip_of(k):
    x, y, _ = _here()
    dx, dy = CHIP_REL[k]
    return _flip(x, dx), _flip(y, dy)


def _scatter_side(arrays):
    n = len(arrays)

    def parts(ins, outs, sems):
        send, recv, lsem = sems
        x, y, c = _here()
        s = 2 * x + y

        def copy(w, k, mine):
            px, py = _chip_of(k)
            return pltpu.make_async_remote_copy(
                src_ref=ins[w].at[2 * px + py], dst_ref=outs[w].at[s if mine else 2 * px + py],
                send_sem=send.at[3 * w + k], recv_sem=recv.at[3 * w + k], device_id=(px, py, c), device_id_type=MESH_ID)

        local = [pltpu.make_async_copy(ins[w].at[s], outs[w].at[s], lsem.at[w]) for w in range(n)]
        return copy, local

    def start(ins, outs, sems):
        copy, local = parts(ins, outs, sems)
        for cp in local:
            cp.start()
        for w in range(n):
            for k in range(3):
                copy(w, k, True).start()

    def finish(ins, outs, sems):
        copy, local = parts(ins, outs, sems)
        for w in range(n):
            for k in range(3):
                copy(w, k, False).wait_recv()
        for w in range(n):
            for k in range(3):
                copy(w, k, True).wait_send()
        for cp in local:
            cp.wait()

    scratch = [pltpu.SemaphoreType.DMA((3 * n,)), pltpu.SemaphoreType.DMA((3 * n,)), pltpu.SemaphoreType.DMA((n,))]
    return _Side(arrays, [jax.ShapeDtypeStruct(a.shape, a.dtype) for a in arrays], scratch, start, finish)


def _gather_side(shards):
    n = len(shards)

    def parts(ins, outs, sems):
        send, recv, fsend, frecv, lsem = sems
        x, y, c = _here()
        s = 2 * x + y

        def half(ref, w, which):
            rows = shards[w].shape[0] // 2
            return ref.at[pl.ds(pl.multiple_of(which * rows, 16), rows)]

        def over_ici(w, k, mine):
            px, py = _chip_of(k)
            return pltpu.make_async_remote_copy(
                src_ref=half(ins[w], w, c), dst_ref=half(outs[w].at[s if mine else 2 * px + py], w, c),
                send_sem=send.at[3 * w + k], recv_sem=recv.at[3 * w + k], device_id=(px, py, c), device_id_type=MESH_ID)

        def to_sibling(w, k, which):
            px, py = _chip_of(k)
            part = half(outs[w].at[2 * px + py], w, which)
            return pltpu.make_async_remote_copy(
                src_ref=part, dst_ref=part, send_sem=fsend.at[3 * w + k], recv_sem=frecv.at[3 * w + k],
                device_id=(x, y, 1 - c), device_id_type=MESH_ID)

        local = [pltpu.make_async_copy(ins[w], outs[w].at[s], lsem.at[w]) for w in range(n)]
        return c, over_ici, to_sibling, local

    def start(ins, outs, sems):
        _, over_ici, _, local = parts(ins, outs, sems)
        for cp in local:
            cp.start()
        for w in range(n):
            for k in range(3):
                over_ici(w, k, True).start()

    def finish(ins, outs, sems):
        c, over_ici, to_sibling, local = parts(ins, outs, sems)
        for w in range(n):
            for k in range(3):
                over_ici(w, k, False).wait_recv()
                to_sibling(w, k, c).start()
        for w in range(n):
            for k in range(3):
                to_sibling(w, k, 1 - c).wait_recv()
        for w in range(n):
            for k in range(3):
                over_ici(w, k, True).wait_send()
                to_sibling(w, k, c).wait_send()
        for cp in local:
            cp.wait()

    scratch = [pltpu.SemaphoreType.DMA((3 * n,))] * 4 + [pltpu.SemaphoreType.DMA((n,))]
    return _Side(shards, [jax.ShapeDtypeStruct((N_CHIPS,) + a.shape, a.dtype) for a in shards], scratch, start, finish)


def _comm_call(name, side):
    n, n_out = len(side.arrays), len(side.out_shapes)

    def body(*refs):
        ins, outs, sems = refs[:n], refs[n:n + n_out], refs[n + n_out:]
        side.start(ins, outs, sems)
        side.finish(ins, outs, sems)

    any_spec = pl.BlockSpec(memory_space=pl.ANY)
    return pl.pallas_call(body, name=name, out_shape=side.out_shapes, in_specs=[any_spec] * n,
                          out_specs=[any_spec] * n_out, scratch_shapes=side.scratch)(*side.arrays)


def _sibling_exchange(name, arrays):
    n = len(arrays)

    def body(*refs):
        ins, outs = refs[:n], refs[n:2 * n]
        send, recv = refs[2 * n:]
        x, y, c = _here()
        cps = [pltpu.make_async_remote_copy(src_ref=ins[w], dst_ref=outs[w], send_sem=send.at[w], recv_sem=recv.at[w],
                                            device_id=(x, y, 1 - c), device_id_type=MESH_ID) for w in range(n)]
        for cp in cps:
            cp.start()
        for cp in cps:
            cp.wait()

    any_spec = pl.BlockSpec(memory_space=pl.ANY)
    return pl.pallas_call(
        body, name=name, out_shape=[jax.ShapeDtypeStruct(a.shape, a.dtype) for a in arrays],
        in_specs=[any_spec] * n, out_specs=[any_spec] * n,
        scratch_shapes=[pltpu.SemaphoreType.DMA((n,)), pltpu.SemaphoreType.DMA((n,))],
    )(*arrays)


def _pack(arrs):
    flat = jnp.concatenate([a.reshape(-1).astype(F32) for a in arrs])
    n = flat.shape[0]
    rows = -(-n // (8 * LANES)) * 8
    return jnp.pad(flat, (0, rows * LANES - n)).reshape(rows, LANES)


def _unpack(buf, shapes):
    lead = buf.shape[:-2]
    flat = buf.reshape(lead + (-1,))
    out, off = [], 0
    for s in shapes:
        n = 1
        for d in s:
            n *= d
        out.append(flat[..., off:off + n].reshape(lead + tuple(s)))
        off += n
    return out


def _pad_w_in(w):
    return jnp.concatenate([w[:, :2048], w[:, 2816:2824], jnp.zeros((w.shape[0], P_XBC - P_DT - 8), w.dtype),
                            w[:, 2048:2816], w[:, 2824:]], axis=1)


def _unpad_w_in(g):
    return jnp.concatenate([g[:, :P_DT], g[:, P_XBC:P_G], g[:, P_DT:P_DT + 8], g[:, P_G:]], axis=1)


FFN_HALF = FFN_HIDDEN // 2


def _ffn_in_cols(w):
    h = FFN_HALF
    return jnp.concatenate([w[:, :h], w[:, 2 * h:3 * h], w[:, h:2 * h], w[:, 3 * h:]], axis=1)


def _row(v):
    return v.reshape(1, -1)


BIG = (("w_in", 2), ("w_sc_out", 2), ("w_sb_out", 2), ("w_ssm_out", 2), ("w_o", 1), ("w_ffn_in", 2), ("w_ffn_out", 1))
SMALL = ("mod_b", "g_pre_mix", "g_post_mix", "g_pre_ffn", "g_post_ffn", "sc_conv_w", "ssm_conv_w", "ssm_conv_b",
         "ssm_dt_bias", "ssm_a_log", "ssm_d", "ssm_norm_w")
WEIGHT_ORDER = ("mod_w", "mod_b", "g_pre_mix", "g_post_mix", "g_pre_ffn", "g_post_ffn", "w_in", "sc_conv_w",
                "ssm_conv_w", "ssm_conv_b", "ssm_dt_bias", "ssm_a_log", "ssm_d", "ssm_norm_w", "w_sc_out", "w_sb_out",
                "w_ssm_out", "w_o", "w_ffn_in", "w_ffn_out")


def _mm_mid(name, a, w, x, vecs):
    return _mm_epi(name, a, w, "nn", D_MODEL, [(x, D_MODEL)] + [(v, None) for v in vecs],
                   lambda p, x, *v: (p,) + tuple(_mid(x, p, *v)), [(D_MODEL, BF16), (D_MODEL, F32), (D_MODEL, BF16)])


def _layer_fwd(l, x_in, h, W, V, sides, next_vecs):
    S = {"x_in": x_in, "h": h}
    side, handler = sides.get("in_proj", (None, None))
    P = _mm(f"in_proj{l}", h, W["w_in"], "nn", BF16, tn_cap=1024, side=side)
    if side:
        handler(P[1:])
        P = P[0]
    S["P"] = P
    S["ya"] = _shortconv_fwd(P, V["sc_w"])
    S["yb"], S["yb32"] = _sb_fwd(P)
    S["pre"] = _ssmconv_fwd(P, V["ssm_w"], V["ssm_b"])
    S["y_ssd"], S["states"] = _ssd_fwd(S["pre"], P, V["dtb"], V["alog"])
    S["yc"] = _rowwise(f"ssm_post{l}", lambda y, px, z, d, nw: _ssm_post(y, px, _f(z), d, nw),
                       [(S["y_ssd"], SSM_INNER, 0), (S["pre"], SSM_INNER, 0), (P, SSM_INNER, P_Z // SSM_INNER)],
                       [V["d_full"], V["norm_w"]], [(SSM_INNER, BF16)])[0]
    S["merged"] = _merge_fwd(f"merge{l}", P, [S["ya"], S["yb"], S["yc"]],
                             [W["w_sc_out"], W["w_sb_out"], W["w_ssm_out"]])
    S["mix"], S["x1"], S["h2"] = _mm_mid(f"w_o{l}", S["merged"], W["w_o"], x_in, V["mid_mix"])
    side, handler = sides.get("ffn_in", (None, None))
    res = _mm_epi(f"ffn_in{l}", S["h2"], W["w_ffn_in"], "nn", 2 * FFN_HALF, [],
                  lambda p: (p, _swiglu(p[:, :FFN_HALF], p[:, FFN_HALF:])),
                  [(2 * FFN_HALF, BF16), (FFN_HALF, BF16)], side=side)
    S["GU"], S["act"] = res[0], res[1]
    if side:
        handler(res[2:])
    if next_vecs is None:
        S["f"] = _mm(f"ffn_out{l}", S["act"], W["w_ffn_out"], "nn", BF16)
    else:
        S["f"], S["x_next"], S["h_next"] = _mm_mid(f"ffn_out{l}", S["act"], W["w_ffn_out"], S["x1"], next_vecs)
    return S


BRANCH_WIDTHS = (SC_WIDTH, 256, SSM_INNER)


def _branch_specs(tm):
    gb = P_G // D_MODEL
    gates = [pl.BlockSpec((tm, D_MODEL), functools.partial(lambda i, cb: (i, cb), cb=gb + k)) for k in range(3)]
    ys = [pl.BlockSpec((tm, w), lambda i: (i, 0)) for w in BRANCH_WIDTHS]
    ws = [pl.BlockSpec((w, D_MODEL), lambda i: (0, 0)) for w in BRANCH_WIDTHS]
    return gates, ys, ws


def _merge_fwd(name, P, ys, ws, tm=512):
    L = P.shape[0]
    tm = min(tm, L)
    gates, y_specs, w_specs = _branch_specs(tm)

    def body(ga, gb, gc, ya, yb, yc, wa, wb, wc, o_ref):
        acc = None
        for g_ref, y_ref, w_ref in ((ga, ya, wa), (gb, yb, wb), (gc, yc, wc)):
            t = jax.nn.sigmoid(_f(g_ref[...])) * jnp.dot(y_ref[...], w_ref[...], preferred_element_type=F32)
            acc = t if acc is None else acc + t
        o_ref[...] = acc.astype(o_ref.dtype)

    return pl.pallas_call(
        body, name=name, grid=(L // tm,), in_specs=gates + y_specs + w_specs,
        out_specs=pl.BlockSpec((tm, D_MODEL), lambda i: (i, 0)), out_shape=jax.ShapeDtypeStruct((L, D_MODEL), BF16),
        compiler_params=_params(("arbitrary",)),
    )(P, P, P, *ys, *ws)


def _merge_bwd(name, P, ys, ws, dmerged, tm=512):
    L = P.shape[0]
    tm = min(tm, L)
    gates, y_specs, w_specs = _branch_specs(tm)

    def body(ga, gb, gc, ya, yb, yc, wa, wb, wc, dm_ref, dg_ref, dya, dyb, dyc, gwa, gwb, gwc):
        @pl.when(pl.program_id(0) == 0)
        def _():
            for r in (gwa, gwb, gwc):
                r[...] = jnp.zeros_like(r)

        dm = _f(dm_ref[...])
        for k, (g_ref, y_ref, w_ref, dy_ref, gw_ref) in enumerate(
                ((ga, ya, wa, dya, gwa), (gb, yb, wb, dyb, gwb), (gc, yc, wc, dyc, gwc))):
            y, w = y_ref[...], w_ref[...]
            s = jax.nn.sigmoid(_f(g_ref[...]))
            proj = jnp.dot(y, w, preferred_element_type=F32)
            d_proj = (dm * s).astype(BF16)
            dg_ref[:, k * D_MODEL:(k + 1) * D_MODEL] = (dm * proj * s * (1.0 - s)).astype(dg_ref.dtype)
            dy_ref[...] = _dot_nt(d_proj, w).astype(dy_ref.dtype)
            gw_ref[...] += _dot_tn(y, d_proj)

    gate_cols = pl.BlockSpec((tm, P_WIDTH - P_G), lambda i: (i, P_G // (P_WIDTH - P_G)))
    return pl.pallas_call(
        body, name=name, grid=(L // tm,),
        in_specs=gates + y_specs + w_specs + [pl.BlockSpec((tm, D_MODEL), lambda i: (i, 0))],
        out_specs=[gate_cols] + y_specs + w_specs,
        out_shape=[jax.ShapeDtypeStruct((L, P_WIDTH), BF16)] + [jax.ShapeDtypeStruct((L, w), BF16) for w in BRANCH_WIDTHS]
        + [jax.ShapeDtypeStruct((w, D_MODEL), F32) for w in BRANCH_WIDTHS],
        compiler_params=_params(("arbitrary",)),
    )(P, P, P, *ys, *ws, dmerged)


def _assemble_dp(name, dP, parts, tl=256):
    L = dP.shape[0]
    tl = min(tl, L)
    n = len(parts)

    def body(*refs):
        o_ref = refs[n + 1]
        o_ref[...] = jnp.concatenate([r[...].astype(o_ref.dtype) for r in refs[:n]], axis=1)

    return pl.pallas_call(
        body, name=name, grid=(L // tl,),
        in_specs=[pl.BlockSpec((tl, a.shape[1]), lambda i: (i, 0)) for a in parts] + [pl.BlockSpec(memory_space=pl.ANY)],
        out_specs=pl.BlockSpec((tl, P_G), lambda i: (i, 0)), out_shape=jax.ShapeDtypeStruct(dP.shape, dP.dtype),
        input_output_aliases={n: 0}, compiler_params=_params(("arbitrary",)),
    )(*parts, dP)


def _layer_bwd(l, S, W, V, dx1, df, sides, landed):
    G = {}
    P = S["P"]

    def mm(key, *args, **kw):
        if key not in sides:
            return _mm(f"{key}{l}", *args, **kw)
        names, layer, make = sides[key]
        res = _mm(f"{key}{l}", *args, side=make(G), **kw)
        for n, a in zip(names, res[1:]):
            landed[(n, layer)] = a
        return res[0]

    G["w_ffn_out"] = _mm(f"gw_ffn_out{l}", S["act"], df, "tn", F32)

    def swiglu_bwd(d_act, gu):
        _, vjp = jax.vjp(_swiglu, _f(gu[:, :FFN_HALF]), _f(gu[:, FFN_HALF:]))
        return (jnp.concatenate(vjp(d_act), axis=1),)

    dGU = _mm_epi(f"d_gu{l}", df, W["w_ffn_out"], "nt", FFN_HALF, [(S["GU"], 2 * FFN_HALF)], swiglu_bwd,
                  [(2 * FFN_HALF, BF16)])[0]
    dh2 = mm("d_h2", dGU, W["w_ffn_in"], "nt", BF16)
    G["w_ffn_in"] = _ffn_in_cols(mm("gw_ffn_in", S["h2"], dGU, "tn", F32))
    dx, dmix, G["gate1"], G["g_post_mix"], G["g_pre_ffn"], G["scale2"], G["shift2"] = _mid_bwd(
        f"mid_mix_bwd{l}", S["x_in"], S["mix"], dx1, dh2, V["mid_mix"])
    dmerged = _mm(f"d_merged{l}", dmix, W["w_o"], "nt", BF16)
    G["w_o"] = _mm(f"gw_o{l}", S["merged"], dmix, "tn", F32)

    dP, dya, dyb, dyc, G["w_sc_out"], G["w_sb_out"], G["w_ssm_out"] = _merge_bwd(
        f"merge_bwd{l}", P, [S["ya"], S["yb"], S["yc"]], [W["w_sc_out"], W["w_sb_out"], W["w_ssm_out"]], dmerged)

    def post_bwd(y, px, z, d, dfull, nw):
        _, vjp = jax.vjp(_ssm_post, y, px, _f(z), dfull, nw)
        return vjp(_f(d))

    dy_ssd, dxs, dz, G["d_full"], G["ssm_norm_w"] = _rowwise(
        f"ssm_post_bwd{l}", post_bwd,
        [(S["y_ssd"], SSM_INNER, 0), (S["pre"], SSM_INNER, 0), (P, SSM_INNER, P_Z // SSM_INNER), (dyc, SSM_INNER, 0)],
        [V["d_full"], V["norm_w"]], [(SSM_INNER, F32), (SSM_INNER, F32), (SSM_INNER, BF16)], [(1, SSM_INNER)] * 2)
    dpre, ddt, G["dtb"], G["alog"] = _ssd_bwd(S["pre"], P, S["states"], dy_ssd, dxs, V["dtb"], V["alog"])
    dxbc, w0, w1, w2, w3, G["ssm_conv_b"] = _ssmconv_bwd(P, dpre, V["ssm_w"])
    G["ssm_conv_w"] = jnp.concatenate([w0, w1, w2, w3], axis=0)
    dq, dk, dv = _sb_bwd(P, dyb, S["yb32"])
    dA, s0, s1, s2 = _shortconv_bwd(P, dya, V["sc_w"])
    G["sc_conv_w"] = jnp.concatenate([s0, s1, s2], axis=0)
    dP = _assemble_dp(f"assemble_dp{l}", dP, [dA, dq, dk, dv, dz, ddt, dxbc])
    G["w_in"] = _mm(f"gw_in{l}", S["h"], dP, "tn", F32, tn_cap=1024)
    dh = mm("d_h", dP, W["w_in"], "nt", BF16, tk_cap=2048)
    return dx, dh, G


def kernel(x, c, mod_w, mod_b, g_pre_mix, g_post_mix, g_pre_ffn, g_post_ffn, w_in, sc_conv_w, ssm_conv_w, ssm_conv_b, ssm_dt_bias, ssm_a_log, ssm_d, ssm_norm_w, w_sc_out, w_sb_out, w_ssm_out, w_o, w_ffn_in, w_ffn_out, loss_target, m_mod_w, m_mod_b, m_g_pre_mix, m_g_post_mix, m_g_pre_ffn, m_g_post_ffn, m_w_in, m_sc_conv_w, m_ssm_conv_w, m_ssm_conv_b, m_ssm_dt_bias, m_ssm_a_log, m_ssm_d, m_ssm_norm_w, m_w_sc_out, m_w_sb_out, m_w_ssm_out, m_w_o, m_w_ffn_in, m_w_ffn_out, v_mod_w, v_mod_b, v_g_pre_mix, v_g_post_mix, v_g_pre_ffn, v_g_post_ffn, v_w_in, v_sc_conv_w, v_ssm_conv_w, v_ssm_conv_b, v_ssm_dt_bias, v_ssm_a_log, v_ssm_d, v_ssm_norm_w, v_w_sc_out, v_w_sb_out, v_w_ssm_out, v_w_o, v_w_ffn_in, v_w_ffn_out):
    wts = dict(mod_w=mod_w, mod_b=mod_b, g_pre_mix=g_pre_mix, g_post_mix=g_post_mix, g_pre_ffn=g_pre_ffn,
               g_post_ffn=g_post_ffn, w_in=w_in, sc_conv_w=sc_conv_w, ssm_conv_w=ssm_conv_w, ssm_conv_b=ssm_conv_b,
               ssm_dt_bias=ssm_dt_bias, ssm_a_log=ssm_a_log, ssm_d=ssm_d, ssm_norm_w=ssm_norm_w, w_sc_out=w_sc_out,
               w_sb_out=w_sb_out, w_ssm_out=w_ssm_out, w_o=w_o, w_ffn_in=w_ffn_in, w_ffn_out=w_ffn_out)
    mom = dict(mod_w=m_mod_w, mod_b=m_mod_b, g_pre_mix=m_g_pre_mix, g_post_mix=m_g_post_mix, g_pre_ffn=m_g_pre_ffn,
               g_post_ffn=m_g_post_ffn, w_in=m_w_in, sc_conv_w=m_sc_conv_w, ssm_conv_w=m_ssm_conv_w,
               ssm_conv_b=m_ssm_conv_b, ssm_dt_bias=m_ssm_dt_bias, ssm_a_log=m_ssm_a_log, ssm_d=m_ssm_d,
               ssm_norm_w=m_ssm_norm_w, w_sc_out=m_w_sc_out, w_sb_out=m_w_sb_out, w_ssm_out=m_w_ssm_out, w_o=m_w_o,
               w_ffn_in=m_w_ffn_in, w_ffn_out=m_w_ffn_out)
    var = dict(mod_w=v_mod_w, mod_b=v_mod_b, g_pre_mix=v_g_pre_mix, g_post_mix=v_g_post_mix, g_pre_ffn=v_g_pre_ffn,
               g_post_ffn=v_g_post_ffn, w_in=v_w_in, sc_conv_w=v_sc_conv_w, ssm_conv_w=v_ssm_conv_w,
               ssm_conv_b=v_ssm_conv_b, ssm_dt_bias=v_ssm_dt_bias, ssm_a_log=v_ssm_a_log, ssm_d=v_ssm_d,
               ssm_norm_w=v_ssm_norm_w, w_sc_out=v_w_sc_out, w_sb_out=v_w_sb_out, w_ssm_out=v_w_ssm_out, w_o=v_w_o,
               w_ffn_in=v_w_ffn_in, w_ffn_out=v_w_ffn_out)
    xi, yi, ci = _here()
    chip = 2 * xi + yi
    me = 4 * xi + 2 * yi + ci
    x0, target = x[0], loss_target[0]

    first_shapes = [(D_MODEL,), sc_conv_w.shape, ssm_conv_w.shape]
    g0 = _allgather_small("gather_cond", _pack([c, sc_conv_w, ssm_conv_w]))
    c_rows, sc_sh, ssm_sh = _unpack(g0, first_shapes)
    c_all = c_rows
    sc_w = jnp.concatenate([sc_sh[2 * j] for j in range(N_CHIPS)], axis=-1)
    ssm_w = jnp.concatenate([ssm_sh[2 * j] for j in range(N_CHIPS)], axis=-1)

    mod_b_sh = lax.dynamic_slice_in_dim(mod_b, chip * MOD_SHARD, MOD_SHARD, axis=1).reshape(DEPTH, 1, MOD_SHARD)
    modpart = _mod_fwd(c_all, mod_w, mod_b_sh)
    g1 = _allgather_small("gather_mod", modpart.reshape(-1, LANES)).reshape(N_DEV, DEPTH, N_DEV, MOD_SHARD)
    mod = jnp.concatenate([lax.dynamic_index_in_dim(g1[2 * j], me, axis=1, keepdims=False) for j in range(N_CHIPS)],
                          axis=-1)

    def layer_shards(l):
        return [wts[n][l].astype(BF16) for n, _ in BIG]

    def full_weights(which, gathered):
        W = {n: jnp.concatenate([g[j] for j in range(N_CHIPS)], axis=ax - 1) for (n, ax), g in zip(which, gathered)}
        if "w_in" in W:
            W["w_in"] = _pad_w_in(W["w_in"])
        if "w_ffn_in" in W:
            W["w_ffn_in"] = _ffn_in_cols(W["w_ffn_in"])
        return W

    Ws = [full_weights(BIG[:1], _comm_call("gather_w_in0", _gather_side(layer_shards(0)[:1]))), {}]
    fwd_sides = [{"in_proj": (_gather_side(layer_shards(0)[1:]), lambda got: Ws[0].update(full_weights(BIG[1:], got))),
                  "ffn_in": (_gather_side(layer_shards(1)), lambda got: Ws[1].update(full_weights(BIG, got)))}, {}]
    Vs = []
    for l in range(DEPTH):
        sh1, sc1, gt1, sh2, sc2, gt2 = [_row(v) for v in jnp.split(mod[l], N_MOD)]
        Vs.append(dict(
            shift1=sh1, scale1=sc1, g_pre_mix=_row(g_pre_mix[l]),
            mid_mix=[gt1, _row(g_post_mix[l]), _row(g_pre_ffn[l]), sc2, sh2],
            gate2=gt2, g_post_ffn=_row(g_post_ffn[l]),
            sc_w=sc_w[l], ssm_w=ssm_w[l], ssm_b=_row(ssm_conv_b[l]),
            dtb=_row(jnp.pad(ssm_dt_bias[l], (0, LANES - SSM_HEADS))), alog=_row(jnp.pad(ssm_a_log[l], (0, LANES - SSM_HEADS))),
            d_full=_row(jnp.repeat(ssm_d[l], SSM_INNER // SSM_HEADS)), norm_w=_row(ssm_norm_w[l])))

    def mid_ffn_vecs(l):
        return [Vs[l]["gate2"], Vs[l]["g_post_ffn"], Vs[l + 1]["g_pre_mix"], Vs[l + 1]["scale1"], Vs[l + 1]["shift1"]]

    saved = []
    x_in = x0
    h = _first_fwd(x0, [Vs[0]["g_pre_mix"], Vs[0]["scale1"], Vs[0]["shift1"]])
    for l in range(DEPTH):
        S = _layer_fwd(l, x_in, h, Ws[l], Vs[l], fwd_sides[l], mid_ffn_vecs(l) if l + 1 < DEPTH else None)
        saved.append(S)
        if l + 1 < DEPTH:
            x_in, h = S["x_next"], S["h_next"]

    def pieces(G, names):
        out = []
        for n, ax in BIG:
            if n in names:
                g = _unpad_w_in(G[n]) if n == "w_in" else G[n]
                out.append(jnp.stack(jnp.split(g, N_CHIPS, axis=ax - 1)).astype(BF16))
        return out

    ffn_names = ("w_ffn_in", "w_ffn_out")
    rest_names = tuple(n for n, _ in BIG if n not in ffn_names)
    late_names = tuple(n for n, _ in BIG if n != "w_in")
    landed = {}

    GL = [None] * DEPTH
    S = saved[-1]
    dx1, df, g_gate2, g_gpf, loss_cols = _last_bwd(S["x1"], S["f"], target, [Vs[-1]["gate2"], Vs[-1]["g_post_ffn"]])
    for l in reversed(range(DEPTH)):
        sides = {}
        if l + 1 < DEPTH:
            sides["d_h2"] = (ffn_names, l + 1, lambda G, up=GL[l + 1]: _scatter_side(pieces(up, ffn_names)))
            sides["gw_ffn_in"] = (rest_names, l + 1, lambda G, up=GL[l + 1]: _scatter_side(pieces(up, rest_names)))
        if l == 0:
            sides["d_h"] = (late_names, l, lambda G: _scatter_side(pieces(G, late_names)))
        dx, dh, G = _layer_bwd(l, saved[l], Ws[l], Vs[l], dx1, df, sides, landed)
        G["gate2"], G["g_post_ffn"] = g_gate2, g_gpf
        GL[l] = G
        if l > 0:
            Sp = saved[l - 1]
            dx1, df, g_gate2, g_gpf, G["g_pre_mix"], G["scale1"], G["shift1"] = _mid_bwd(
                f"mid_ffn_bwd{l - 1}", Sp["x1"], Sp["f"], dx, dh, mid_ffn_vecs(l - 1))
        else:
            grad_x, G["g_pre_mix"], G["scale1"], G["shift1"], landed[("w_in", 0)] = _first_bwd(
                x0, dx, dh, [Vs[0]["g_pre_mix"], Vs[0]["scale1"], Vs[0]["shift1"]], _scatter_side(pieces(G, ("w_in",))))
    loss = lax.psum(jnp.sum(loss_cols), ("x", "y", "c"))

    def both(key, shape=None):
        a = jnp.stack([GL[l][key] for l in range(DEPTH)])
        return a if shape is None else a.reshape(shape)

    dmod = jnp.concatenate([both(k, (DEPTH, D_MODEL)) for k in ("shift1", "scale1", "gate1", "shift2", "scale2", "gate2")],
                           axis=1)
    part_small = dict(
        mod_b=dmod, g_pre_mix=both("g_pre_mix", (DEPTH, D_MODEL)), g_post_mix=both("g_post_mix", (DEPTH, D_MODEL)),
        g_pre_ffn=both("g_pre_ffn", (DEPTH, D_MODEL)), g_post_ffn=both("g_post_ffn", (DEPTH, D_MODEL)),
        sc_conv_w=both("sc_conv_w"), ssm_conv_w=both("ssm_conv_w"), ssm_conv_b=both("ssm_conv_b", (DEPTH, SSM_CONV_DIM)),
        ssm_dt_bias=both("dtb", (DEPTH, LANES))[:, :SSM_HEADS], ssm_a_log=both("alog", (DEPTH, LANES))[:, :SSM_HEADS],
        ssm_d=both("d_full", (DEPTH, SSM_HEADS, SSM_INNER // SSM_HEADS)).sum(-1),
        ssm_norm_w=both("ssm_norm_w", (DEPTH, SSM_INNER)))
    small_shapes = [part_small[n].shape for n in SMALL]
    g2 = _allgather_small("gather_small_grads", _pack([part_small[n] for n in SMALL]))
    tot = dict(zip(SMALL, _unpack(_sum_slots("sum_small_grads", g2), small_shapes)))
    dmod_all = _unpack(g2, small_shapes)[0]
    dmod_sh = jnp.swapaxes(lax.dynamic_slice_in_dim(dmod_all, chip * MOD_SHARD, MOD_SHARD, axis=2), 0, 1)
    grads = {"mod_w": _mod_bwd(c_all, dmod_sh)}
    for n in SMALL:
        grads[n] = tot[n]
    grads["sc_conv_w"] = lax.dynamic_slice_in_dim(tot["sc_conv_w"], chip * 64, 64, axis=2)
    grads["ssm_conv_w"] = lax.dynamic_slice_in_dim(tot["ssm_conv_w"], chip * 192, 192, axis=2)

    keys = [(n, l) for n, _ in BIG for l in range(DEPTH)]
    mine = [_sum_slots(f"sum_{n}{l}", landed[(n, l)]) for n, l in keys]
    theirs = dict(zip(keys, _sibling_exchange("swap_core_sums", mine)))
    mine = dict(zip(keys, mine))

    out = {}

    def update(name, w2, gs, m2, v2, shape):
        g, d, nm, nv = _adamw(f"adamw_{name}", w2, gs, m2, v2)
        out[name] = tuple(a.reshape(shape) for a in (g, d, nm, nv))

    for n, _ in BIG:
        shp = wts[n].shape
        two = (-1, shp[-1])
        by_layer = [tuple(src[(n, l)] for l in range(DEPTH)) for src in (mine, theirs)]
        update(n, wts[n].reshape(two), by_layer, mom[n].reshape(two), var[n].reshape(two), shp)
    two = (-1, MOD_SHARD)
    update("mod_w", mod_w.reshape(two), [grads["mod_w"].reshape(two)], m_mod_w.reshape(two), v_mod_w.reshape(two), mod_w.shape)
    shapes = [wts[n].shape for n in SMALL]
    res = _adamw("adamw_small", _pack([wts[n] for n in SMALL]), [_pack([grads[n] for n in SMALL])],
                 _pack([mom[n] for n in SMALL]), _pack([var[n] for n in SMALL]))
    for n, g, d, nm, nv in zip(SMALL, *[_unpack(r, shapes) for r in res]):
        out[n] = (g, d, nm, nv)

    result = [loss, grad_x[None]]
    for k in range(4):
        result += [out[n][k] for n in WEIGHT_ORDER]
    return tuple(result)
```

```python
import functools

import jax
import jax.numpy as jnp
from jax import lax
from jax.experimental import pallas as pl
from jax.experimental.pallas import tpu as pltpu

F32 = jnp.float32
BF16 = jnp.bfloat16
HIGHEST = lax.Precision.HIGHEST
MESH_ID = pl.DeviceIdType.MESH

D_MODEL = 1024
DEPTH = 2
SC_WIDTH = 256
SC_KERNEL = 3
SB_HEAD_DIM = 64
SSM_INNER = 512
SSM_HEADS = 8
SSM_STATE = 64
SSM_CONV = 4
SSM_CHUNK = 256
SSM_CONV_DIM = 768
FFN_HIDDEN = 2816
NORM_EPS = 1e-6
N_MOD = 6
N_CHIPS = 4
N_DEV = 8

ADAM_LR = 0.001
ADAM_B1 = 0.9
ADAM_B2 = 0.999
ADAM_EPS = 1e-08
ADAM_WD = 0.01
ADAM_STEP = 10

P_WIDTH = 6144
P_A, P_B, P_Z, P_DT, P_XBC, P_G = 0, 768, 1536, 2048, 2304, 3072
DT_PAD = 256

VMEM_LIMIT_BYTES = 56 * 1024 * 1024
LANES = 128

SB_LOG_CUTOFF = -105.0
SB_TQ = 256
SB_TK = 256


def _params(sem):
    return pltpu.CompilerParams(dimension_semantics=sem, vmem_limit_bytes=VMEM_LIMIT_BYTES)


def _pick(n, cap):
    if n <= cap:
        return n
    best = None
    for m in range(LANES, cap + 1, LANES):
        if n % m == 0:
            best = m
    assert best is not None, (n, cap)
    return best


def _rowwise(name, fn, rows, vecs, row_outs, acc_outs=(), tl=256, side=None):
    L = rows[0][0].shape[0]
    tl = min(tl, L)
    assert L % tl == 0
    n_in = len(rows) + len(vecs)
    n_ro, n_ao = len(row_outs), len(acc_outs)
    n_si = len(side.arrays) if side else 0
    n_so = len(side.out_shapes) if side else 0

    def body(*refs):
        ins, s_in = refs[:n_in], refs[n_in:n_in + n_si]
        outs = refs[n_in + n_si:]
        ro, ao, s_out, sems = outs[:n_ro], outs[n_ro:n_ro + n_ao], outs[n_ro + n_ao:n_ro + n_ao + n_so], outs[n_ro + n_ao + n_so:]
        if side:
            @pl.when(pl.program_id(0) == 0)
            def _():
                side.start(s_in, s_out, sems)

        _rows(ins, ro, ao)
        if side:
            @pl.when(pl.program_id(0) == L // tl - 1)
            def _():
                side.finish(s_in, s_out, sems)

    def _rows(ins, ro, ao):
        vals = fn(*[r[...] for r in ins])
        if not isinstance(vals, (tuple, list)):
            vals = (vals,)
        for o, v in zip(ro, vals[:n_ro]):
            o[...] = v.astype(o.dtype)
        if ao:
            @pl.when(pl.program_id(0) == 0)
            def _():
                for o in ao:
                    o[...] = jnp.zeros_like(o)
            for o, v in zip(ao, vals[n_ro:]):
                o[...] += v.astype(F32)

    in_specs = [pl.BlockSpec((tl, w), functools.partial(lambda i, cb: (i, cb), cb=cb)) for _, w, cb in rows]
    in_specs += [pl.BlockSpec(v.shape, lambda i: (0, 0)) for v in vecs]
    out_specs = [pl.BlockSpec((tl, w), lambda i: (i, 0)) for w, _ in row_outs]
    out_specs += [pl.BlockSpec(s, lambda i: (0, 0)) for s in acc_outs]
    out_shape = [jax.ShapeDtypeStruct((L, w), dt) for w, dt in row_outs]
    out_shape += [jax.ShapeDtypeStruct(s, F32) for s in acc_outs]
    any_spec = pl.BlockSpec(memory_space=pl.ANY)
    return pl.pallas_call(
        body, name=name, grid=(L // tl,), in_specs=in_specs + [any_spec] * n_si, out_specs=out_specs + [any_spec] * n_so,
        out_shape=out_shape + (side.out_shapes if side else []), scratch_shapes=side.scratch if side else [],
        compiler_params=_params(("arbitrary",)),
    )(*[a for a, _, _ in rows], *vecs, *(side.arrays if side else []))


def _mm(name, a, b, mode, out_dtype, tm=1024, tn_cap=1408, tk_cap=2816, side=None):
    if mode == "nn":
        (M, K), (_, N) = a.shape, b.shape
    elif mode == "nt":
        (M, K), (N, _) = a.shape, b.shape
    else:
        (K, M), (_, N) = a.shape, b.shape
        tm, tk_cap = 1408, 2048
    tm = _pick(M, tm)
    tn = _pick(N, tn_cap)
    tk = _pick(K, tk_cap)
    nk = K // tk
    grid = (M // tm, N // tn, nk)
    n_si = len(side.arrays) if side else 0
    n_so = len(side.out_shapes) if side else 0
    n_acc = 1 if nk > 1 else 0

    def body(a_ref, b_ref, *rest):
        s_in, o_ref, s_out = rest[:n_si], rest[n_si], rest[n_si + 1:n_si + 1 + n_so]
        scr = rest[n_si + 1 + n_so:]
        if side:
            at = [pl.program_id(d) for d in range(3)]
            is_first = jnp.logical_and(jnp.logical_and(at[0] == 0, at[1] == 0), at[2] == 0)
            is_last = jnp.logical_and(jnp.logical_and(at[0] == grid[0] - 1, at[1] == grid[1] - 1), at[2] == grid[2] - 1)

            @pl.when(is_first)
            def _():
                side.start(s_in, s_out, scr[n_acc:])

        _product(a_ref, b_ref, o_ref, scr)
        if side:
            @pl.when(is_last)
            def _():
                side.finish(s_in, s_out, scr[n_acc:])

    def _product(a_ref, b_ref, o_ref, scr):
        if mode == "nn":
            p = jnp.dot(a_ref[...], b_ref[...], preferred_element_type=F32)
        elif mode == "nt":
            p = lax.dot_general(a_ref[...], b_ref[...], (((1,), (1,)), ((), ())), preferred_element_type=F32)
        else:
            p = lax.dot_general(a_ref[...], b_ref[...], (((0,), (0,)), ((), ())), preferred_element_type=F32)
        if nk == 1:
            o_ref[...] = p.astype(o_ref.dtype)
        else:
            acc = scr[0]
            k = pl.program_id(2)

            @pl.when(k == 0)
            def _():
                acc[...] = p

            @pl.when(k > 0)
            def _():
                acc[...] += p

            @pl.when(k == nk - 1)
            def _():
                o_ref[...] = acc[...].astype(o_ref.dtype)

    if mode == "nn":
        a_spec = pl.BlockSpec((tm, tk), lambda i, j, k: (i, k))
        b_spec = pl.BlockSpec((tk, tn), lambda i, j, k: (k, j))
    elif mode == "nt":
        a_spec = pl.BlockSpec((tm, tk), lambda i, j, k: (i, k))
        b_spec = pl.BlockSpec((tn, tk), lambda i, j, k: (j, k))
    else:
        a_spec = pl.BlockSpec((tk, tm), lambda i, j, k: (k, i))
        b_spec = pl.BlockSpec((tk, tn), lambda i, j, k: (k, j))
    any_spec = pl.BlockSpec(memory_space=pl.ANY)
    res = pl.pallas_call(
        body, name=name, grid=grid, in_specs=[a_spec, b_spec] + [any_spec] * n_si,
        out_specs=[pl.BlockSpec((tm, tn), lambda i, j, k: (i, j))] + [any_spec] * n_so,
        out_shape=[jax.ShapeDtypeStruct((M, N), out_dtype)] + (side.out_shapes if side else []),
        scratch_shapes=([pltpu.VMEM((tm, tn), F32)] if nk > 1 else []) + (side.scratch if side else []),
        compiler_params=_params(("arbitrary", "arbitrary", "arbitrary")),
    )(a, b, *(side.arrays if side else []))
    return res if side else res[0]


def _mm_epi(name, a, b, mode, tn, extras, epi, outs, tm=512, side=None):
    if mode == "nn":
        (M, K), (_, N) = a.shape, b.shape
    else:
        (M, K), (N, _) = a.shape, b.shape
    tm = _pick(M, tm)
    grid = (N // tn, M // tm)
    n_ex, n_out = len(extras), len(outs)
    n_si = len(side.arrays) if side else 0
    n_so = len(side.out_shapes) if side else 0

    def body(*refs):
        a_ref, b_ref, ex = refs[0], refs[1], refs[2:2 + n_ex]
        s_in = refs[2 + n_ex:2 + n_ex + n_si]
        o_refs = refs[2 + n_ex + n_si:2 + n_ex + n_si + n_out]
        s_out = refs[2 + n_ex + n_si + n_out:2 + n_ex + n_si + n_out + n_so]
        sems = refs[2 + n_ex + n_si + n_out + n_so:]
        if side:
            @pl.when(jnp.logical_and(pl.program_id(0) == 0, pl.program_id(1) == 0))
            def _():
                side.start(s_in, s_out, sems)

        if mode == "nn":
            p = jnp.dot(a_ref[...], b_ref[...], preferred_element_type=F32)
        else:
            p = lax.dot_general(a_ref[...], b_ref[...], (((1,), (1,)), ((), ())), preferred_element_type=F32)
        for o, v in zip(o_refs, epi(p, *[r[...] for r in ex])):
            o[...] = v.astype(o.dtype)
        if side:
            @pl.when(jnp.logical_and(pl.program_id(0) == grid[0] - 1, pl.program_id(1) == grid[1] - 1))
            def _():
                side.finish(s_in, s_out, sems)

    any_spec = pl.BlockSpec(memory_space=pl.ANY)
    a_spec = pl.BlockSpec((tm, K), lambda j, i: (i, 0))
    b_spec = pl.BlockSpec((K, tn), lambda j, i: (0, j)) if mode == "nn" else pl.BlockSpec((tn, K), lambda j, i: (j, 0))
    return pl.pallas_call(
        body, name=name, grid=grid,
        in_specs=[a_spec, b_spec]
        + [pl.BlockSpec(e.shape, lambda j, i: (0, 0)) if w is None else pl.BlockSpec((tm, w), lambda j, i: (i, j))
           for e, w in extras] + [any_spec] * n_si,
        out_specs=[pl.BlockSpec((tm, w), lambda j, i: (i, j)) for w, _ in outs] + [any_spec] * n_so,
        out_shape=[jax.ShapeDtypeStruct((M, (N // tn) * w), dt) for w, dt in outs] + (side.out_shapes if side else []),
        scratch_shapes=side.scratch if side else [],
        compiler_params=_params(("arbitrary", "arbitrary")),
    )(a, b, *[e for e, _ in extras], *(side.arrays if side else []))


def _f(x):
    return x.astype(F32)


def _silu(x):
    return x * jax.nn.sigmoid(x)


def _softplus(x):
    return jnp.maximum(x, 0.0) + jnp.log1p(jnp.exp(-jnp.abs(x)))


def _rms(x, g):
    r = lax.rsqrt(jnp.mean(x * x, axis=-1, keepdims=True) + NORM_EPS)
    return x * r * g


def _adaln(x, g, scale, shift):
    return _rms(x, g) * (1.0 + scale) + shift


def _resid(x, y, gate, g):
    return x + gate * _rms(y, g)


def _mid(x, y, gate, g_post, g_pre, scale, shift):
    x_new = _resid(x, y, gate, g_post)
    return x_new, _adaln(x_new, g_pre, scale, shift)


def _swiglu(gt, up):
    return _silu(gt) * up


def _ssm_post(y_ssd, pre_xs, z, d_full, norm_w):
    y = (y_ssd + _silu(pre_xs) * d_full) * _silu(z)
    half = SSM_INNER // 2
    parts = []
    for g in range(2):
        yg = y[:, g * half:(g + 1) * half]
        parts.append(yg * lax.rsqrt(jnp.mean(yg * yg, axis=-1, keepdims=True) + NORM_EPS))
    return jnp.concatenate(parts, axis=1) * norm_w


def _first_fwd(x, vecs, side=None):
    return _rowwise("adaln_first", lambda x, g, sc, sh: _adaln(x, g, sc, sh),
                    [(x, D_MODEL, 0)], vecs, [(D_MODEL, BF16)], tl=512, side=side)


def _mid_bwd(name, x, y, dx_new, dh, vecs):
    def fn(x, y, dxn, dh, *v):
        _, vjp = jax.vjp(_mid, x, _f(y), *v)
        return vjp((dxn, _f(dh)))

    vec = (1, D_MODEL)
    return _rowwise(name, fn, [(x, D_MODEL, 0), (y, D_MODEL, 0), (dx_new, D_MODEL, 0), (dh, D_MODEL, 0)], vecs,
                    [(D_MODEL, F32), (D_MODEL, BF16)], [vec] * 5)


def _first_bwd(x, dx_in, dh, vecs, side=None):
    def fn(x, dxi, dh, *v):
        _, vjp = jax.vjp(_adaln, x, *v)
        dx, dg, dsc, dsh = vjp(_f(dh))
        return dx + dxi, dg, dsc, dsh

    vec = (1, D_MODEL)
    return _rowwise("adaln_first_bwd", fn, [(x, D_MODEL, 0), (dx_in, D_MODEL, 0), (dh, D_MODEL, 0)], vecs,
                    [(D_MODEL, F32)], [vec] * 3, side=side)


def _last_bwd(x1, f, target, vecs):
    def fn(x1, f, t, gate, g):
        x2, vjp = jax.vjp(_resid, x1, _f(f), gate, g)
        err = x2 - t
        dx1, df, dgate, dg = vjp(err * (1.0 / D_MODEL))
        loss_cols = jnp.sum(err * err, axis=0, keepdims=True) * (0.5 / D_MODEL)
        return dx1, df, dgate, dg, loss_cols

    vec = (1, D_MODEL)
    return _rowwise("loss_last_bwd", fn, [(x1, D_MODEL, 0), (f, D_MODEL, 0), (target, D_MODEL, 0)], vecs,
                    [(D_MODEL, F32), (D_MODEL, BF16)], [vec] * 3)


HALO = 16


def _shift_down(u, prev, k):
    rows = lax.broadcasted_iota(jnp.int32, u.shape, 0)
    v = pltpu.roll(u, k, 0)
    for t in range(k):
        v = jnp.where(rows == t, prev[HALO - k + t:HALO - k + t + 1, :], v)
    return v


def _shift_up(u, nxt, k):
    n = u.shape[0]
    rows = lax.broadcasted_iota(jnp.int32, u.shape, 0)
    v = pltpu.roll(u, n - k, 0)
    for t in range(k):
        v = jnp.where(rows == n - k + t, nxt[t:t + 1, :], v)
    return v


def _conv_specs(L, tl, width, col_block):
    per = tl // HALO
    last = L // HALO - 1
    main = pl.BlockSpec((tl, width), lambda i: (i, col_block))
    before = pl.BlockSpec((HALO, width), lambda i: (jnp.maximum(i * per - 1, 0), col_block))
    after = pl.BlockSpec((HALO, width), lambda i: (jnp.minimum((i + 1) * per, last), col_block))
    return main, before, after


def _shortconv_fwd(P, w, tl=512):
    L = P.shape[0]
    tl = min(tl, L)
    C = SC_WIDTH
    main, before, _ = _conv_specs(L, tl, 3 * C, 0)

    def body(p_ref, h_ref, w_ref, o_ref):
        first = (pl.program_id(0) == 0)
        p, h = _f(p_ref[...]), _f(h_ref[...])
        b, u = p[:, :C], p[:, C:2 * C] * p[:, 2 * C:]
        uh = jnp.where(first, 0.0, h[:, C:2 * C] * h[:, 2 * C:])
        wv = w_ref[...]
        cv = wv[2:3] * u + wv[1:2] * _shift_down(u, uh, 1) + wv[0:1] * _shift_down(u, uh, 2)
        o_ref[...] = (b * cv).astype(o_ref.dtype)

    return pl.pallas_call(
        body, name="shortconv_fwd", grid=(L // tl,),
        in_specs=[main, before, pl.BlockSpec(w.shape, lambda i: (0, 0))],
        out_specs=pl.BlockSpec((tl, C), lambda i: (i, 0)),
        out_shape=jax.ShapeDtypeStruct((L, C), BF16), compiler_params=_params(("arbitrary",)),
    )(P, P, w)


def _shortconv_bwd(P, dya, w, tl=512):
    L = P.shape[0]
    tl = min(tl, L)
    C = SC_WIDTH
    main, before, after = _conv_specs(L, tl, 3 * C, 0)
    dmain, _, dafter = _conv_specs(L, tl, C, 0)
    n = L // tl

    def body(p_ref, h_ref, n_ref, d_ref, dn_ref, w_ref, o_ref, dw0, dw1, dw2):
        i = pl.program_id(0)
        p, h, nx = _f(p_ref[...]), _f(h_ref[...]), _f(n_ref[...])
        b, c, x = p[:, :C], p[:, C:2 * C], p[:, 2 * C:]
        u = c * x
        uh = jnp.where(i == 0, 0.0, h[:, C:2 * C] * h[:, 2 * C:])
        u1, u2 = _shift_down(u, uh, 1), _shift_down(u, uh, 2)
        wv = w_ref[...]
        cv = wv[2:3] * u + wv[1:2] * u1 + wv[0:1] * u2
        dy = _f(d_ref[...])
        dcv = dy * b
        dcv_n = jnp.where(i == n - 1, 0.0, _f(dn_ref[...]) * nx[:, :C])
        du = wv[2:3] * dcv + wv[1:2] * _shift_up(dcv, dcv_n, 1) + wv[0:1] * _shift_up(dcv, dcv_n, 2)
        o_ref[:, :C] = (dy * cv).astype(o_ref.dtype)
        o_ref[:, C:2 * C] = (du * x).astype(o_ref.dtype)
        o_ref[:, 2 * C:] = (du * c).astype(o_ref.dtype)

        @pl.when(i == 0)
        def _():
            for r in (dw0, dw1, dw2):
                r[...] = jnp.zeros_like(r)

        dw0[...] += jnp.sum(dcv * u2, axis=0, keepdims=True)
        dw1[...] += jnp.sum(dcv * u1, axis=0, keepdims=True)
        dw2[...] += jnp.sum(dcv * u, axis=0, keepdims=True)

    vec = pl.BlockSpec((1, C), lambda i: (0, 0))
    return pl.pallas_call(
        body, name="shortconv_bwd", grid=(n,),
        in_specs=[main, before, after, dmain, dafter, pl.BlockSpec(w.shape, lambda i: (0, 0))],
        out_specs=[pl.BlockSpec((tl, 3 * C), lambda i: (i, 0)), vec, vec, vec],
        out_shape=[jax.ShapeDtypeStruct((L, 3 * C), BF16)] + [jax.ShapeDtypeStruct((1, C), F32)] * 3,
        compiler_params=_params(("arbitrary",)),
    )(P, P, P, dya, dya, w)


def _ssmconv_fwd(P, w, bias, tl=512):
    L = P.shape[0]
    tl = min(tl, L)
    C = SSM_CONV_DIM
    main, before, _ = _conv_specs(L, tl, C, P_XBC // C)

    def body(p_ref, h_ref, w_ref, b_ref, o_ref):
        u = _f(p_ref[...])
        uh = jnp.where(pl.program_id(0) == 0, 0.0, _f(h_ref[...]))
        wv = w_ref[...]
        acc = wv[3:4] * u + b_ref[...]
        for k in range(1, SSM_CONV):
            acc = acc + wv[3 - k:4 - k] * _shift_down(u, uh, k)
        o_ref[...] = acc

    return pl.pallas_call(
        body, name="ssmconv_fwd", grid=(L // tl,),
        in_specs=[main, before, pl.BlockSpec(w.shape, lambda i: (0, 0)), pl.BlockSpec(bias.shape, lambda i: (0, 0))],
        out_specs=pl.BlockSpec((tl, C), lambda i: (i, 0)),
        out_shape=jax.ShapeDtypeStruct((L, C), F32), compiler_params=_params(("arbitrary",)),
    )(P, P, w, bias)


def _ssmconv_bwd(P, dpre, w, tl=512):
    L = P.shape[0]
    tl = min(tl, L)
    C = SSM_CONV_DIM
    main, before, _ = _conv_specs(L, tl, C, P_XBC // C)
    dmain, _, dafter = _conv_specs(L, tl, C, 0)
    n = L // tl

    def body(p_ref, h_ref, d_ref, dn_ref, w_ref, o_ref, dw0, dw1, dw2, dw3, db):
        i = pl.program_id(0)
        u = _f(p_ref[...])
        uh = jnp.where(i == 0, 0.0, _f(h_ref[...]))
        d = d_ref[...]
        dn = jnp.where(i == n - 1, 0.0, dn_ref[...])
        wv = w_ref[...]
        du = wv[3:4] * d
        for k in range(1, SSM_CONV):
            du = du + wv[3 - k:4 - k] * _shift_up(d, dn, k)
        o_ref[...] = du.astype(o_ref.dtype)

        @pl.when(i == 0)
        def _():
            for r in (dw0, dw1, dw2, dw3, db):
                r[...] = jnp.zeros_like(r)

        for k, r in ((3, dw0), (2, dw1), (1, dw2)):
            r[...] += jnp.sum(d * _shift_down(u, uh, k), axis=0, keepdims=True)
        dw3[...] += jnp.sum(d * u, axis=0, keepdims=True)
        db[...] += jnp.sum(d, axis=0, keepdims=True)

    vec = pl.BlockSpec((1, C), lambda i: (0, 0))
    return pl.pallas_call(
        body, name="ssmconv_bwd", grid=(n,),
        in_specs=[main, before, dmain, dafter, pl.BlockSpec(w.shape, lambda i: (0, 0))],
        out_specs=[pl.BlockSpec((tl, C), lambda i: (i, 0))] + [vec] * 5,
        out_shape=[jax.ShapeDtypeStruct((L, C), BF16)] + [jax.ShapeDtypeStruct((1, C), F32)] * 5,
        compiler_params=_params(("arbitrary",)),
    )(P, P, dpre, dpre, w)


def _dot_nt(a, b):
    return lax.dot_general(a, b, (((1,), (1,)), ((), ())), preferred_element_type=F32)


def _dot_tn(a, b):
    return lax.dot_general(a, b, (((0,), (0,)), ((), ())), preferred_element_type=F32)


def _split3(x):
    hi = x.astype(BF16)
    r = x - hi.astype(F32)
    mid = r.astype(BF16)
    return hi, mid, (r - mid.astype(F32)).astype(BF16)


@jax.custom_vjp
def _xm01(x, m):
    return sum(jnp.dot(t, m, preferred_element_type=F32) for t in _split3(x))


def _xm01_fwd(x, m):
    return _xm01(x, m), m


def _xm01_bwd(m, g):
    return sum(_dot_nt(t, m) for t in _split3(g)), jnp.zeros_like(m)


_xm01.defvjp(_xm01_fwd, _xm01_bwd)


@jax.custom_vjp
def _m01x(m, x):
    return sum(jnp.dot(m, t, preferred_element_type=F32) for t in _split3(x))


def _m01x_fwd(m, x):
    return _m01x(m, x), m


def _m01x_bwd(m, g):
    return jnp.zeros_like(m), sum(_dot_tn(m, t) for t in _split3(g))


_m01x.defvjp(_m01x_fwd, _m01x_bwd)


def _ssd_chunk(pre, dtr, s_prev, dtb, alog):
    T = pre.shape[0]
    act = _silu(pre)
    xs, bm, cm = act[:, :SSM_INNER], act[:, SSM_INNER:SSM_INNER + 128], act[:, SSM_INNER + 128:]
    lane = lax.broadcasted_iota(jnp.int32, (1, LANES), 1)
    dt = jnp.where(lane < SSM_HEADS, _softplus(dtr + dtb), 0.0)
    a = dt * (-jnp.exp(alog))
    ri = lax.broadcasted_iota(jnp.int32, (T, T), 0)
    ci = lax.broadcasted_iota(jnp.int32, (T, T), 1)
    causal = ci <= ri
    a_cs = _m01x(causal.astype(BF16), a)
    eh = lax.broadcasted_iota(jnp.int32, (LANES, SSM_INNER), 0)
    ej = lax.broadcasted_iota(jnp.int32, (LANES, SSM_INNER), 1)
    expand = (lax.shift_right_logical(ej, 6) == eh).astype(BF16)
    dt_full = _xm01(dt, expand)
    acs_full = _xm01(a_cs, expand)
    alast_full = acs_full[T - 1:T, :]
    xdt = xs * dt_full
    a_cs_t = a_cs.T
    ys, s_new = [], []
    for g in range(2):
        in_group = lax.shift_right_logical(lane, 6) == g
        cg = jnp.where(in_group, cm, 0.0).astype(BF16)
        bg = jnp.where(in_group, bm, 0.0).astype(BF16)
        scores = _dot_nt(cg, bg)
        for pp in range(2):
            hp = 2 * g + pp
            cols = slice(hp * LANES, (hp + 1) * LANES)
            xp, acsp = xdt[:, cols], acs_full[:, cols]
            per_head = []
            for hh in range(2):
                h = 2 * hp + hh
                decay = jnp.exp(jnp.where(causal, a_cs[:, h:h + 1] - a_cs_t[h:h + 1, :], -jnp.inf))
                per_head.append(jnp.dot((scores * decay).astype(BF16), xp.astype(BF16), preferred_element_type=F32))
            y_diag = jnp.where(lane < SSM_STATE, per_head[0], per_head[1])
            sp = s_prev[hp * LANES:(hp + 1) * LANES, :]
            y_off = jnp.dot(cg, sp.astype(BF16), preferred_element_type=F32) * jnp.exp(acsp)
            ys.append(y_diag + y_off)
            to_end = jnp.exp(alast_full[:, cols] - acsp)
            s_new.append(sp * jnp.exp(alast_full[:, cols]) + _dot_tn(bg, (xp * to_end).astype(BF16)))
    return jnp.concatenate(ys, axis=1), jnp.concatenate(s_new, axis=0)


def _ssd_fwd(pre, P, dtb, alog):
    L = pre.shape[0]
    T = min(SSM_CHUNK, L)
    nc = L // T

    def body(pre_ref, dt_ref, dtb_ref, al_ref, y_ref, st_ref, s_scr):
        @pl.when(pl.program_id(0) == 0)
        def _():
            s_scr[...] = jnp.zeros_like(s_scr)

        st_ref[0] = s_scr[...]
        y, s = _ssd_chunk(pre_ref[...], _f(dt_ref[...]), s_scr[...], dtb_ref[...], al_ref[...])
        y_ref[...] = y
        s_scr[...] = s

    vec = pl.BlockSpec((1, LANES), lambda i: (0, 0))
    return pl.pallas_call(
        body, name="ssd_fwd", grid=(nc,),
        in_specs=[pl.BlockSpec((T, SSM_CONV_DIM), lambda i: (i, 0)), pl.BlockSpec((T, LANES), lambda i: (i, P_DT // LANES)),
                  vec, vec],
        out_specs=[pl.BlockSpec((T, SSM_INNER), lambda i: (i, 0)), pl.BlockSpec((1, 512, LANES), lambda i: (i, 0, 0))],
        out_shape=[jax.ShapeDtypeStruct((L, SSM_INNER), F32), jax.ShapeDtypeStruct((nc, 512, LANES), F32)],
        scratch_shapes=[pltpu.VMEM((512, LANES), F32)], compiler_params=_params(("arbitrary",)),
    )(pre, P, dtb, alog)


def _ssd_bwd(pre, P, states, dy, dxs_extra, dtb, alog):
    L = pre.shape[0]
    T = min(SSM_CHUNK, L)
    nc = L // T

    def body(pre_ref, dt_ref, st_ref, dy_ref, dx_ref, dtb_ref, al_ref, dpre_ref, ddt_ref, ddtb_ref, dal_ref, ds_scr):
        @pl.when(pl.program_id(0) == 0)
        def _():
            ds_scr[...] = jnp.zeros_like(ds_scr)
            ddtb_ref[...] = jnp.zeros_like(ddtb_ref)
            dal_ref[...] = jnp.zeros_like(dal_ref)

        _, vjp = jax.vjp(_ssd_chunk, pre_ref[...], _f(dt_ref[...]), st_ref[0], dtb_ref[...], al_ref[...])
        dpre, ddt, ds, ddtb, dal = vjp((dy_ref[...], ds_scr[...]))
        dpre_ref[:, :SSM_INNER] = dpre[:, :SSM_INNER] + dx_ref[...]
        dpre_ref[:, SSM_INNER:] = dpre[:, SSM_INNER:]
        ddt_ref[:, :LANES] = ddt.astype(ddt_ref.dtype)
        ddt_ref[:, LANES:] = jnp.zeros((T, DT_PAD - LANES), ddt_ref.dtype)
        ds_scr[...] = ds
        ddtb_ref[...] += ddtb
        dal_ref[...] += dal

    vec = pl.BlockSpec((1, LANES), lambda i: (0, 0))
    rev = lambda i: (nc - 1 - i, 0)
    return pl.pallas_call(
        body, name="ssd_bwd", grid=(nc,),
        in_specs=[pl.BlockSpec((T, SSM_CONV_DIM), rev), pl.BlockSpec((T, LANES), lambda i: (nc - 1 - i, P_DT // LANES)),
                  pl.BlockSpec((1, 512, LANES), lambda i: (nc - 1 - i, 0, 0)),
                  pl.BlockSpec((T, SSM_INNER), rev), pl.BlockSpec((T, SSM_INNER), rev), vec, vec],
        out_specs=[pl.BlockSpec((T, SSM_CONV_DIM), rev), pl.BlockSpec((T, DT_PAD), rev), vec, vec],
        out_shape=[jax.ShapeDtypeStruct((L, SSM_CONV_DIM), F32), jax.ShapeDtypeStruct((L, DT_PAD), BF16),
                   jax.ShapeDtypeStruct((1, LANES), F32), jax.ShapeDtypeStruct((1, LANES), F32)],
        scratch_shapes=[pltpu.VMEM((512, LANES), F32)], compiler_params=_params(("arbitrary",)),
    )(pre, P, states, dy, dxs_extra, dtb, alog)


def _sb_scores(qm, kb, later, strict, mask):
    z = _dot_nt(qm, kb)
    lk = jnp.minimum(-z, 0.0) - jnp.log(1.0 + jnp.exp(-jnp.abs(z)))
    if mask is not None:
        lk = jnp.where(mask, lk, 0.0)
    log_a = z + lk + jnp.dot(lk.astype(BF16), strict, preferred_element_type=F32) + later
    if mask is not None:
        log_a = jnp.where(mask, log_a, -jnp.inf)
    return z, lk, log_a


def _dot_split(x, m):
    hi = x.astype(BF16)
    lo = (x - hi.astype(F32)).astype(BF16)
    return jnp.dot(hi, m, preferred_element_type=F32) + jnp.dot(lo, m, preferred_element_type=F32)


def _sb_setup(q_ref, i, tq, tk):
    lane = lax.broadcasted_iota(jnp.int32, (1, LANES), 1)
    first = lane < SB_HEAD_DIM
    q = q_ref[...] * (SB_HEAD_DIM ** -0.5)
    qms = (jnp.where(first, q, jnp.zeros_like(q)), jnp.where(first, jnp.zeros_like(q), q))
    j0 = lax.div(i * tq, tk)
    ri = lax.broadcasted_iota(jnp.int32, (tq, tk), 0)
    ci = lax.broadcasted_iota(jnp.int32, (tq, tk), 1)
    diag_mask = (ci + (j0 * tk - i * tq)) < ri
    kr = lax.broadcasted_iota(jnp.int32, (tk, tk), 0)
    kc = lax.broadcasted_iota(jnp.int32, (tk, tk), 1)
    strict = (kr > kc).astype(BF16)
    return first, qms, j0, diag_mask, strict


def _sb_continue(c):
    return jnp.logical_and(c[0] >= 0, jnp.maximum(jnp.max(c[1][0]), jnp.max(c[1][1])) > SB_LOG_CUTOFF)


def _sb_fwd(P):
    L = P.shape[0]
    tq, tk = min(SB_TQ, L), min(SB_TK, L)
    nq = L // tq
    qb = P_B // LANES

    def body(q_ref, k_ref, v_ref, o_ref, of_ref):
        i = pl.program_id(1)
        first, qms, j0, diag_mask, strict = _sb_setup(q_ref, i, tq, tk)

        def tile(h, j, later, acc, mask=None, valid=None):
            off = pl.multiple_of(j * tk, tk)
            gate = later if valid is None else jnp.where(valid, later, -jnp.inf)
            _, lk, log_a = _sb_scores(qms[h], k_ref[pl.ds(off, tk), :], gate, strict, mask)
            acc = acc + jnp.dot(jnp.exp(log_a).astype(BF16), v_ref[pl.ds(off, tk), :], preferred_element_type=F32)
            rows = jnp.sum(lk, axis=1, keepdims=True)
            return later + (rows if valid is None else jnp.where(valid, rows, 0.0)), acc

        zero, zacc = jnp.zeros((tq, 1), F32), jnp.zeros((tq, LANES), F32)
        state = []
        for h in range(2):
            later, acc = tile(h, j0, zero, zacc, mask=diag_mask)
            state.append(tile(h, jnp.maximum(j0 - 1, 0), later, acc, valid=j0 >= 1))

        def tail(c):
            res = [tile(h, c[0], c[1][h], c[2][h]) for h in range(2)]
            return c[0] - 1, (res[0][0], res[1][0]), (res[0][1], res[1][1])

        _, _, accs = lax.while_loop(_sb_continue, tail, (j0 - 2, (state[0][0], state[1][0]), (state[0][1], state[1][1])))
        out = jnp.where(first, accs[0], accs[1])
        o_ref[...] = out.astype(o_ref.dtype)
        of_ref[...] = out

    tile_spec = pl.BlockSpec((tq, LANES), lambda p, i: (i, p))
    return pl.pallas_call(
        body, name="sb_fwd", grid=(2, nq),
        in_specs=[pl.BlockSpec((tq, LANES), lambda p, i: (i, qb + p)),
                  pl.BlockSpec((L, LANES), lambda p, i: (0, qb + 2 + p)),
                  pl.BlockSpec((L, LANES), lambda p, i: (0, qb + 4 + p))],
        out_specs=[tile_spec, tile_spec],
        out_shape=[jax.ShapeDtypeStruct((L, 2 * LANES), BF16), jax.ShapeDtypeStruct((L, 2 * LANES), F32)],
        compiler_params=_params(("arbitrary", "arbitrary")),
    )(P, P, P)


def _sb_bwd(P, dyb, yb32):
    L = P.shape[0]
    tq, tk = min(SB_TQ, L), min(SB_TK, L)
    nq = L // tq
    qb = P_B // LANES

    def body(q_ref, k_ref, v_ref, do_ref, of_ref, dq_ref, dk_ref, dv_ref):
        i = pl.program_id(1)
        first, qms, j0, diag_mask, strict = _sb_setup(q_ref, i, tq, tk)

        @pl.when(i == 0)
        def _():
            dk_ref[...] = jnp.zeros_like(dk_ref)
            dv_ref[...] = jnp.zeros_like(dv_ref)

        do = do_ref[...]
        doms = (jnp.where(first, do, jnp.zeros_like(do)), jnp.where(first, jnp.zeros_like(do), do))
        prod = _f(do) * of_ref[...]
        totals = (jnp.sum(jnp.where(first, prod, 0.0), axis=1, keepdims=True),
                  jnp.sum(jnp.where(first, 0.0, prod), axis=1, keepdims=True))

        def tile(h, j, later, later_g, acc, mask=None, valid=None):
            off = pl.multiple_of(j * tk, tk)
            kb, vb = k_ref[pl.ds(off, tk), :], v_ref[pl.ds(off, tk), :]
            gate = later if valid is None else jnp.where(valid, later, -jnp.inf)
            z, lk, log_a = _sb_scores(qms[h], kb, gate, strict, mask)
            att = jnp.exp(log_a).astype(BF16)
            g = _f(att) * _dot_nt(doms[h], vb)
            before = totals[h] - later_g
            if valid is not None:
                before = jnp.where(valid, before, 0.0)
            dz = g - (before - _dot_split(g, strict)) * jnp.exp(z + lk)
            if mask is not None:
                dz = jnp.where(mask, dz, 0.0)
            dzb = dz.astype(BF16)
            rows = jnp.sum(lk, axis=1, keepdims=True)
            carry = (later + (rows if valid is None else jnp.where(valid, rows, 0.0)),
                     later_g + jnp.sum(g, axis=1, keepdims=True), acc + jnp.dot(dzb, kb, preferred_element_type=F32))
            return carry, _dot_tn(dzb, qms[h]), _dot_tn(att, doms[h])

        def tail(c):
            off = pl.multiple_of(c[0] * tk, tk)
            (c0, dk0, dv0), (c1, dk1, dv1) = [tile(h, c[0], c[1][h], c[2][h], c[3][h]) for h in range(2)]
            dk_ref[pl.ds(off, tk), :] += dk0 + dk1
            dv_ref[pl.ds(off, tk), :] += dv0 + dv1
            return (c[0] - 1,) + tuple(zip(c0, c1))

        zero, zacc = jnp.zeros((tq, 1), F32), jnp.zeros((tq, LANES), F32)
        jp = jnp.maximum(j0 - 1, 0)
        carries, dks, dvs = [], [], []
        for h in range(2):
            carry, dk0, dv0 = tile(h, j0, zero, zero, zacc, mask=diag_mask)
            carry, dk1, dv1 = tile(h, jp, *carry, valid=j0 >= 1)
            carries.append(carry)
            dks.append((dk0, dk1))
            dvs.append((dv0, dv1))
        for n, j in enumerate((j0, jp)):
            off = pl.multiple_of(j * tk, tk)
            dk_ref[pl.ds(off, tk), :] += dks[0][n] + dks[1][n]
            dv_ref[pl.ds(off, tk), :] += dvs[0][n] + dvs[1][n]
        accs = lax.while_loop(_sb_continue, tail, (j0 - 2,) + tuple(zip(carries[0], carries[1])))[3]
        dq_ref[...] = jnp.where(first, accs[0], accs[1]) * (SB_HEAD_DIM ** -0.5)

    full = pl.BlockSpec((L, LANES), lambda p, i: (0, p))
    tile_spec = pl.BlockSpec((tq, LANES), lambda p, i: (i, p))
    return pl.pallas_call(
        body, name="sb_bwd", grid=(2, nq),
        in_specs=[pl.BlockSpec((tq, LANES), lambda p, i: (i, qb + p)),
                  pl.BlockSpec((L, LANES), lambda p, i: (0, qb + 2 + p)),
                  pl.BlockSpec((L, LANES), lambda p, i: (0, qb + 4 + p)), tile_spec, tile_spec],
        out_specs=[tile_spec, full, full],
        out_shape=[jax.ShapeDtypeStruct((L, 2 * LANES), F32)] * 3,
        compiler_params=_params(("arbitrary", "arbitrary")),
    )(P, P, P, dyb, yb32)


MOD_SHARD = N_MOD * D_MODEL // N_CHIPS


def _mod_fwd(c_all, mod_w, mod_b_sh):
    tn = 512

    def body(c_ref, w_ref, b_ref, o_ref):
        o_ref[0] = jnp.dot(_silu(c_ref[...]), w_ref[0], precision=HIGHEST, preferred_element_type=F32) + b_ref[0]

    return pl.pallas_call(
        body, name="mod_fwd", grid=(DEPTH, MOD_SHARD // tn),
        in_specs=[pl.BlockSpec((N_DEV, D_MODEL), lambda l, j: (0, 0)),
                  pl.BlockSpec((1, D_MODEL, tn), lambda l, j: (l, 0, j)),
                  pl.BlockSpec((1, 1, tn), lambda l, j: (l, 0, j))],
        out_specs=pl.BlockSpec((1, N_DEV, tn), lambda l, j: (l, 0, j)),
        out_shape=jax.ShapeDtypeStruct((DEPTH, N_DEV, MOD_SHARD), F32),
        compiler_params=_params(("arbitrary", "arbitrary")),
    )(c_all, mod_w, mod_b_sh)


def _mod_bwd(c_all, dmod_sh):
    tn = 512

    def body(c_ref, d_ref, o_ref):
        o_ref[0] = lax.dot_general(_silu(c_ref[...]), d_ref[0], (((0,), (0,)), ((), ())), precision=HIGHEST,
                                   preferred_element_type=F32)

    return pl.pallas_call(
        body, name="mod_bwd", grid=(DEPTH, MOD_SHARD // tn),
        in_specs=[pl.BlockSpec((N_DEV, D_MODEL), lambda l, j: (0, 0)),
                  pl.BlockSpec((1, N_DEV, tn), lambda l, j: (l, 0, j))],
        out_specs=pl.BlockSpec((1, D_MODEL, tn), lambda l, j: (l, 0, j)),
        out_shape=jax.ShapeDtypeStruct((DEPTH, D_MODEL, MOD_SHARD), F32),
        compiler_params=_params(("arbitrary", "arbitrary")),
    )(c_all, dmod_sh)


def _row_tile(rows, cap):
    if rows <= cap:
        return rows
    best = None
    for t in range(8, cap + 1, 8):
        if rows % t == 0:
            best = t
    assert best is not None, (rows, cap)
    return best


def _adamw(name, w, gs, m, v, tr=256):
    R, W = w.shape
    by_layer = any(isinstance(t, tuple) for t in gs)
    tr = _row_tile(R // 2 if by_layer else R, tr)
    per = (R // 2) // tr

    flat, specs = [], []
    for t in gs:
        if isinstance(t, tuple):
            flat += list(t)
            specs += [pl.BlockSpec((tr, W), lambda i: (jnp.minimum(i, per - 1), 0)),
                      pl.BlockSpec((tr, W), lambda i: (jnp.maximum(i - per, 0), 0))]
        else:
            flat.append(t)
            specs.append(pl.BlockSpec((tr, W), lambda i: (i, 0)))
    ng = len(flat)

    def body(*refs):
        w_ref, g_refs, (m_ref, v_ref) = refs[0], list(refs[1:1 + ng]), refs[1 + ng:3 + ng]
        g_out, d_out, m_out, v_out = refs[3 + ng:]
        g = None
        for t in gs:
            if isinstance(t, tuple):
                lo, hi = g_refs.pop(0), g_refs.pop(0)
                term = jnp.where(pl.program_id(0) < per, lo[...], hi[...])
            else:
                term = g_refs.pop(0)[...]
            g = term if g is None else g + term
        mm = ADAM_B1 * m_ref[...] + (1.0 - ADAM_B1) * g
        vv = ADAM_B2 * v_ref[...] + (1.0 - ADAM_B2) * (g * g)
        m_hat = mm / (1.0 - ADAM_B1 ** ADAM_STEP)
        v_hat = vv / (1.0 - ADAM_B2 ** ADAM_STEP)
        g_out[...] = g
        d_out[...] = -ADAM_LR * (m_hat / (jnp.sqrt(v_hat) + ADAM_EPS) + ADAM_WD * w_ref[...])
        m_out[...] = mm
        v_out[...] = vv

    spec = pl.BlockSpec((tr, W), lambda i: (i, 0))
    return pl.pallas_call(
        body, name=name, grid=(R // tr,), in_specs=[spec] + specs + [spec, spec], out_specs=[spec] * 4,
        out_shape=[jax.ShapeDtypeStruct((R, W), F32)] * 4, compiler_params=_params(("arbitrary",)),
    )(w, *flat, m, v)


def _sum_slots(name, a, tr=256):
    n, R, W = a.shape
    tr = _row_tile(R, tr)

    def body(a_ref, o_ref):
        acc = _f(a_ref[0])
        for j in range(1, n):
            acc = acc + _f(a_ref[j])
        o_ref[...] = acc

    return pl.pallas_call(
        body, name=name, grid=(R // tr,), in_specs=[pl.BlockSpec((n, tr, W), lambda i: (0, i, 0))],
        out_specs=pl.BlockSpec((tr, W), lambda i: (i, 0)), out_shape=jax.ShapeDtypeStruct((R, W), F32),
        compiler_params=_params(("arbitrary",)),
    )(a)


def _here():
    return lax.axis_index("x"), lax.axis_index("y"), lax.axis_index("c")


def _flip(v, d):
    return 1 - v if d else v


def _allgather_small(name, buf):
    R = buf.shape[0]
    rel = [(dx, dy, dc) for dx in (0, 1) for dy in (0, 1) for dc in (0, 1)][1:]

    def body(x_ref, o_ref, send, recv, lsem):
        x, y, c = _here()
        me = 4 * x + 2 * y + c
        mine = pltpu.make_async_copy(x_ref, o_ref.at[me], lsem)
        mine.start()

        def copy(k, slot):
            dx, dy, dc = rel[k]
            return pltpu.make_async_remote_copy(
                src_ref=x_ref, dst_ref=o_ref.at[slot], send_sem=send.at[k], recv_sem=recv.at[k],
                device_id=(_flip(x, dx), _flip(y, dy), _flip(c, dc)), device_id_type=MESH_ID)

        sent = [copy(k, me) for k in range(len(rel))]
        for cp in sent:
            cp.start()
        for k, (dx, dy, dc) in enumerate(rel):
            copy(k, 4 * _flip(x, dx) + 2 * _flip(y, dy) + _flip(c, dc)).wait_recv()
        for cp in sent:
            cp.wait_send()
        mine.wait()

    return pl.pallas_call(
        body, name=name, out_shape=jax.ShapeDtypeStruct((N_DEV, R, LANES), F32),
        in_specs=[pl.BlockSpec(memory_space=pltpu.VMEM)], out_specs=pl.BlockSpec(memory_space=pltpu.VMEM),
        scratch_shapes=[pltpu.SemaphoreType.DMA((7,)), pltpu.SemaphoreType.DMA((7,)), pltpu.SemaphoreType.DMA],
    )(buf)


CHIP_REL = [(1, 0), (0, 1), (1, 1)]


class _Side:
    def __init__(self, arrays, out_shapes, scratch, start, finish):
        self.arrays, self.out_shapes, self.scratch, self.start, self.finish = arrays, out_shapes, scratch, start, finish


def _chip_of(k):
    x, y, _ = _here()
    dx, dy = CHIP_REL[k]
    return _flip(x, dx), _flip(y, dy)


def _scatter_side(arrays):
    n = len(arrays)

    def parts(ins, outs, sems):
        send, recv, lsem = sems
        x, y, c = _here()
        s = 2 * x + y

        def copy(w, k, mine):
            px, py = _chip_of(k)
            return pltpu.make_async_remote_copy(
                src_ref=ins[w].at[2 * px + py], dst_ref=outs[w].at[s if mine else 2 * px + py],
                send_sem=send.at[3 * w + k], recv_sem=recv.at[3 * w + k], device_id=(px, py, c), device_id_type=MESH_ID)

        local = [pltpu.make_async_copy(ins[w].at[s], outs[w].at[s], lsem.at[w]) for w in range(n)]
        return copy, local

    def start(ins, outs, sems):
        copy, local = parts(ins, outs, sems)
        for cp in local:
            cp.start()
        for w in range(n):
            for k in range(3):
                copy(w, k, True).start()

    def finish(ins, outs, sems):
        copy, local = parts(ins, outs, sems)
        for w in range(n):
            for k in range(3):
                copy(w, k, False).wait_recv()
        for w in range(n):
            for k in range(3):
                copy(w, k, True).wait_send()
        for cp in local:
            cp.wait()

    scratch = [pltpu.SemaphoreType.DMA((3 * n,)), pltpu.SemaphoreType.DMA((3 * n,)), pltpu.SemaphoreType.DMA((n,))]
    return _Side(arrays, [jax.ShapeDtypeStruct(a.shape, a.dtype) for a in arrays], scratch, start, finish)


def _gather_side(shards):
    n = len(shards)

    def parts(ins, outs, sems):
        send, recv, fsend, frecv, lsem = sems
        x, y, c = _here()
        s = 2 * x + y

        def half(ref, w, which):
            rows = shards[w].shape[0] // 2
            return ref.at[pl.ds(pl.multiple_of(which * rows, 16), rows)]

        def over_ici(w, k, mine):
            px, py = _chip_of(k)
            return pltpu.make_async_remote_copy(
                src_ref=half(ins[w], w, c), dst_ref=half(outs[w].at[s if mine else 2 * px + py], w, c),
                send_sem=send.at[3 * w + k], recv_sem=recv.at[3 * w + k], device_id=(px, py, c), device_id_type=MESH_ID)

        def to_sibling(w, k, which):
            px, py = _chip_of(k)
            part = half(outs[w].at[2 * px + py], w, which)
            return pltpu.make_async_remote_copy(
                src_ref=part, dst_ref=part, send_sem=fsend.at[3 * w + k], recv_sem=frecv.at[3 * w + k],
                device_id=(x, y, 1 - c), device_id_type=MESH_ID)

        local = [pltpu.make_async_copy(ins[w], outs[w].at[s], lsem.at[w]) for w in range(n)]
        return c, over_ici, to_sibling, local

    def start(ins, outs, sems):
        _, over_ici, _, local = parts(ins, outs, sems)
        for cp in local:
            cp.start()
        for w in range(n):
            for k in range(3):
                over_ici(w, k, True).start()

    def finish(ins, outs, sems):
        c, over_ici, to_sibling, local = parts(ins, outs, sems)
        for w in range(n):
            for k in range(3):
                over_ici(w, k, False).wait_recv()
                to_sibling(w, k, c).start()
        for w in range(n):
            for k in range(3):
                to_sibling(w, k, 1 - c).wait_recv()
        for w in range(n):
            for k in range(3):
                over_ici(w, k, True).wait_send()
                to_sibling(w, k, c).wait_send()
        for cp in local:
            cp.wait()

    scratch = [pltpu.SemaphoreType.DMA((3 * n,))] * 4 + [pltpu.SemaphoreType.DMA((n,))]
    return _Side(shards, [jax.ShapeDtypeStruct((N_CHIPS,) + a.shape, a.dtype) for a in shards], scratch, start, finish)


def _sibling_exchange(name, arrays):
    n = len(arrays)

    def body(*refs):
        ins, outs = refs[:n], refs[n:2 * n]
        send, recv = refs[2 * n:]
        x, y, c = _here()
        cps = [pltpu.make_async_remote_copy(src_ref=ins[w], dst_ref=outs[w], send_sem=send.at[w], recv_sem=recv.at[w],
                                            device_id=(x, y, 1 - c), device_id_type=MESH_ID) for w in range(n)]
        for cp in cps:
            cp.start()
        for cp in cps:
            cp.wait()

    any_spec = pl.BlockSpec(memory_space=pl.ANY)
    return pl.pallas_call(
        body, name=name, out_shape=[jax.ShapeDtypeStruct(a.shape, a.dtype) for a in arrays],
        in_specs=[any_spec] * n, out_specs=[any_spec] * n,
        scratch_shapes=[pltpu.SemaphoreType.DMA((n,)), pltpu.SemaphoreType.DMA((n,))],
    )(*arrays)


def _pack(arrs):
    flat = jnp.concatenate([a.reshape(-1).astype(F32) for a in arrs])
    n = flat.shape[0]
    rows = -(-n // (8 * LANES)) * 8
    return jnp.pad(flat, (0, rows * LANES - n)).reshape(rows, LANES)


def _unpack(buf, shapes):
    lead = buf.shape[:-2]
    flat = buf.reshape(lead + (-1,))
    out, off = [], 0
    for s in shapes:
        n = 1
        for d in s:
            n *= d
        out.append(flat[..., off:off + n].reshape(lead + tuple(s)))
        off += n
    return out


def _pad_w_in(w):
    return jnp.concatenate([w[:, :2048], w[:, 2816:2824], jnp.zeros((w.shape[0], P_XBC - P_DT - 8), w.dtype),
                            w[:, 2048:2816], w[:, 2824:]], axis=1)


def _unpad_w_in(g):
    return jnp.concatenate([g[:, :P_DT], g[:, P_XBC:P_G], g[:, P_DT:P_DT + 8], g[:, P_G:]], axis=1)


FFN_HALF = FFN_HIDDEN // 2


def _ffn_in_cols(w):
    h = FFN_HALF
    return jnp.concatenate([w[:, :h], w[:, 2 * h:3 * h], w[:, h:2 * h], w[:, 3 * h:]], axis=1)


def _row(v):
    return v.reshape(1, -1)


BIG = (("w_in", 2), ("w_sc_out", 2), ("w_sb_out", 2), ("w_ssm_out", 2), ("w_o", 1), ("w_ffn_in", 2), ("w_ffn_out", 1))
SMALL = ("mod_b", "g_pre_mix", "g_post_mix", "g_pre_ffn", "g_post_ffn", "sc_conv_w", "ssm_conv_w", "ssm_conv_b",
         "ssm_dt_bias", "ssm_a_log", "ssm_d", "ssm_norm_w")
WEIGHT_ORDER = ("mod_w", "mod_b", "g_pre_mix", "g_post_mix", "g_pre_ffn", "g_post_ffn", "w_in", "sc_conv_w",
                "ssm_conv_w", "ssm_conv_b", "ssm_dt_bias", "ssm_a_log", "ssm_d", "ssm_norm_w", "w_sc_out", "w_sb_out",
                "w_ssm_out", "w_o", "w_ffn_in", "w_ffn_out")


def _mm_mid(name, a, w, x, vecs):
    return _mm_epi(name, a, w, "nn", D_MODEL, [(x, D_MODEL)] + [(v, None) for v in vecs],
                   lambda p, x, *v: (p,) + tuple(_mid(x, p, *v)), [(D_MODEL, BF16), (D_MODEL, F32), (D_MODEL, BF16)])


def _layer_fwd(l, x_in, h, W, V, sides, next_vecs):
    S = {"x_in": x_in, "h": h}
    side, handler = sides.get("in_proj", (None, None))
    P = _mm(f"in_proj{l}", h, W["w_in"], "nn", BF16, tm=2048, tn_cap=1024, side=side)
    if side:
        handler(P[1:])
        P = P[0]
    S["P"] = P
    S["ya"] = _shortconv_fwd(P, V["sc_w"])
    S["yb"], S["yb32"] = _sb_fwd(P)
    S["pre"] = _ssmconv_fwd(P, V["ssm_w"], V["ssm_b"])
    S["y_ssd"], S["states"] = _ssd_fwd(S["pre"], P, V["dtb"], V["alog"])
    S["yc"] = _rowwise(f"ssm_post{l}", lambda y, px, z, d, nw: _ssm_post(y, px, _f(z), d, nw),
                       [(S["y_ssd"], SSM_INNER, 0), (S["pre"], SSM_INNER, 0), (P, SSM_INNER, P_Z // SSM_INNER)],
                       [V["d_full"], V["norm_w"]], [(SSM_INNER, BF16)])[0]
    S["merged"] = _merge_fwd(f"merge{l}", P, [S["ya"], S["yb"], S["yc"]],
                             [W["w_sc_out"], W["w_sb_out"], W["w_ssm_out"]])
    S["mix"], S["x1"], S["h2"] = _mm_mid(f"w_o{l}", S["merged"], W["w_o"], x_in, V["mid_mix"])
    side, handler = sides.get("ffn_in", (None, None))
    res = _mm_epi(f"ffn_in{l}", S["h2"], W["w_ffn_in"], "nn", 2 * FFN_HALF, [],
                  lambda p: (p, _swiglu(p[:, :FFN_HALF], p[:, FFN_HALF:])),
                  [(2 * FFN_HALF, BF16), (FFN_HALF, BF16)], side=side)
    S["GU"], S["act"] = res[0], res[1]
    if side:
        handler(res[2:])
    if next_vecs is None:
        S["f"] = _mm(f"ffn_out{l}", S["act"], W["w_ffn_out"], "nn", BF16)
    else:
        S["f"], S["x_next"], S["h_next"] = _mm_mid(f"ffn_out{l}", S["act"], W["w_ffn_out"], S["x1"], next_vecs)
    return S


BRANCH_WIDTHS = (SC_WIDTH, 256, SSM_INNER)


def _branch_specs(tm):
    gb = P_G // D_MODEL
    gates = [pl.BlockSpec((tm, D_MODEL), functools.partial(lambda i, cb: (i, cb), cb=gb + k)) for k in range(3)]
    ys = [pl.BlockSpec((tm, w), lambda i: (i, 0)) for w in BRANCH_WIDTHS]
    ws = [pl.BlockSpec((w, D_MODEL), lambda i: (0, 0)) for w in BRANCH_WIDTHS]
    return gates, ys, ws


def _merge_fwd(name, P, ys, ws, tm=512):
    L = P.shape[0]
    tm = min(tm, L)
    gates, y_specs, w_specs = _branch_specs(tm)

    def body(ga, gb, gc, ya, yb, yc, wa, wb, wc, o_ref):
        acc = None
        for g_ref, y_ref, w_ref in ((ga, ya, wa), (gb, yb, wb), (gc, yc, wc)):
            t = jax.nn.sigmoid(_f(g_ref[...])) * jnp.dot(y_ref[...], w_ref[...], preferred_element_type=F32)
            acc = t if acc is None else acc + t
        o_ref[...] = acc.astype(o_ref.dtype)

    return pl.pallas_call(
        body, name=name, grid=(L // tm,), in_specs=gates + y_specs + w_specs,
        out_specs=pl.BlockSpec((tm, D_MODEL), lambda i: (i, 0)), out_shape=jax.ShapeDtypeStruct((L, D_MODEL), BF16),
        compiler_params=_params(("arbitrary",)),
    )(P, P, P, *ys, *ws)


def _merge_bwd(name, P, ys, ws, dmerged, tm=512):
    L = P.shape[0]
    tm = min(tm, L)
    gates, y_specs, w_specs = _branch_specs(tm)

    def body(ga, gb, gc, ya, yb, yc, wa, wb, wc, dm_ref, dg_ref, dya, dyb, dyc, gwa, gwb, gwc):
        @pl.when(pl.program_id(0) == 0)
        def _():
            for r in (gwa, gwb, gwc):
                r[...] = jnp.zeros_like(r)

        dm = _f(dm_ref[...])
        for k, (g_ref, y_ref, w_ref, dy_ref, gw_ref) in enumerate(
                ((ga, ya, wa, dya, gwa), (gb, yb, wb, dyb, gwb), (gc, yc, wc, dyc, gwc))):
            y, w = y_ref[...], w_ref[...]
            s = jax.nn.sigmoid(_f(g_ref[...]))
            proj = jnp.dot(y, w, preferred_element_type=F32)
            d_proj = (dm * s).astype(BF16)
            dg_ref[:, k * D_MODEL:(k + 1) * D_MODEL] = (dm * proj * s * (1.0 - s)).astype(dg_ref.dtype)
            dy_ref[...] = _dot_nt(d_proj, w).astype(dy_ref.dtype)
            gw_ref[...] += _dot_tn(y, d_proj)

    gate_cols = pl.BlockSpec((tm, P_WIDTH - P_G), lambda i: (i, P_G // (P_WIDTH - P_G)))
    return pl.pallas_call(
        body, name=name, grid=(L // tm,),
        in_specs=gates + y_specs + w_specs + [pl.BlockSpec((tm, D_MODEL), lambda i: (i, 0))],
        out_specs=[gate_cols] + y_specs + w_specs,
        out_shape=[jax.ShapeDtypeStruct((L, P_WIDTH), BF16)] + [jax.ShapeDtypeStruct((L, w), BF16) for w in BRANCH_WIDTHS]
        + [jax.ShapeDtypeStruct((w, D_MODEL), F32) for w in BRANCH_WIDTHS],
        compiler_params=_params(("arbitrary",)),
    )(P, P, P, *ys, *ws, dmerged)


def _assemble_dp(name, dP, parts, tl=256):
    L = dP.shape[0]
    tl = min(tl, L)
    n = len(parts)

    def body(*refs):
        o_ref = refs[n + 1]
        o_ref[...] = jnp.concatenate([r[...].astype(o_ref.dtype) for r in refs[:n]], axis=1)

    return pl.pallas_call(
        body, name=name, grid=(L // tl,),
        in_specs=[pl.BlockSpec((tl, a.shape[1]), lambda i: (i, 0)) for a in parts] + [pl.BlockSpec(memory_space=pl.ANY)],
        out_specs=pl.BlockSpec((tl, P_G), lambda i: (i, 0)), out_shape=jax.ShapeDtypeStruct(dP.shape, dP.dtype),
        input_output_aliases={n: 0}, compiler_params=_params(("arbitrary",)),
    )(*parts, dP)


def _layer_bwd(l, S, W, V, dx1, df, sides, landed):
    G = {}
    P = S["P"]

    def mm(key, *args, **kw):
        if key not in sides:
            return _mm(f"{key}{l}", *args, **kw)
        names, layer, make = sides[key]
        res = _mm(f"{key}{l}", *args, side=make(G), **kw)
        for n, a in zip(names, res[1:]):
            landed[(n, layer)] = a
        return res[0]

    G["w_ffn_out"] = _mm(f"gw_ffn_out{l}", S["act"], df, "tn", F32)

    def swiglu_bwd(d_act, gu):
        _, vjp = jax.vjp(_swiglu, _f(gu[:, :FFN_HALF]), _f(gu[:, FFN_HALF:]))
        return (jnp.concatenate(vjp(d_act), axis=1),)

    names, layer, make = sides.get("d_gu", ((), None, None))
    res = _mm_epi(f"d_gu{l}", df, W["w_ffn_out"], "nt", FFN_HALF, [(S["GU"], 2 * FFN_HALF)], swiglu_bwd,
                  [(2 * FFN_HALF, BF16)], side=make(G) if make else None)
    dGU = res[0]
    for n, a in zip(names, res[1:]):
        landed[(n, layer)] = a
    dh2 = mm("d_h2", dGU, W["w_ffn_in"], "nt", BF16)
    G["w_ffn_in"] = _ffn_in_cols(mm("gw_ffn_in", S["h2"], dGU, "tn", F32))
    dx, dmix, G["gate1"], G["g_post_mix"], G["g_pre_ffn"], G["scale2"], G["shift2"] = _mid_bwd(
        f"mid_mix_bwd{l}", S["x_in"], S["mix"], dx1, dh2, V["mid_mix"])
    dmerged = _mm(f"d_merged{l}", dmix, W["w_o"], "nt", BF16)
    G["w_o"] = _mm(f"gw_o{l}", S["merged"], dmix, "tn", F32)

    dP, dya, dyb, dyc, G["w_sc_out"], G["w_sb_out"], G["w_ssm_out"] = _merge_bwd(
        f"merge_bwd{l}", P, [S["ya"], S["yb"], S["yc"]], [W["w_sc_out"], W["w_sb_out"], W["w_ssm_out"]], dmerged)

    def post_bwd(y, px, z, d, dfull, nw):
        _, vjp = jax.vjp(_ssm_post, y, px, _f(z), dfull, nw)
        return vjp(_f(d))

    dy_ssd, dxs, dz, G["d_full"], G["ssm_norm_w"] = _rowwise(
        f"ssm_post_bwd{l}", post_bwd,
        [(S["y_ssd"], SSM_INNER, 0), (S["pre"], SSM_INNER, 0), (P, SSM_INNER, P_Z // SSM_INNER), (dyc, SSM_INNER, 0)],
        [V["d_full"], V["norm_w"]], [(SSM_INNER, F32), (SSM_INNER, F32), (SSM_INNER, BF16)], [(1, SSM_INNER)] * 2)
    dpre, ddt, G["dtb"], G["alog"] = _ssd_bwd(S["pre"], P, S["states"], dy_ssd, dxs, V["dtb"], V["alog"])
    dxbc, w0, w1, w2, w3, G["ssm_conv_b"] = _ssmconv_bwd(P, dpre, V["ssm_w"])
    G["ssm_conv_w"] = jnp.concatenate([w0, w1, w2, w3], axis=0)
    dq, dk, dv = _sb_bwd(P, dyb, S["yb32"])
    dA, s0, s1, s2 = _shortconv_bwd(P, dya, V["sc_w"])
    G["sc_conv_w"] = jnp.concatenate([s0, s1, s2], axis=0)
    dP = _assemble_dp(f"assemble_dp{l}", dP, [dA, dq, dk, dv, dz, ddt, dxbc])
    G["w_in"] = _mm(f"gw_in{l}", S["h"], dP, "tn", F32, tn_cap=1024)
    dh = mm("d_h", dP, W["w_in"], "nt", BF16, tk_cap=3072)
    return dx, dh, G


def kernel(x, c, mod_w, mod_b, g_pre_mix, g_post_mix, g_pre_ffn, g_post_ffn, w_in, sc_conv_w, ssm_conv_w, ssm_conv_b, ssm_dt_bias, ssm_a_log, ssm_d, ssm_norm_w, w_sc_out, w_sb_out, w_ssm_out, w_o, w_ffn_in, w_ffn_out, loss_target, m_mod_w, m_mod_b, m_g_pre_mix, m_g_post_mix, m_g_pre_ffn, m_g_post_ffn, m_w_in, m_sc_conv_w, m_ssm_conv_w, m_ssm_conv_b, m_ssm_dt_bias, m_ssm_a_log, m_ssm_d, m_ssm_norm_w, m_w_sc_out, m_w_sb_out, m_w_ssm_out, m_w_o, m_w_ffn_in, m_w_ffn_out, v_mod_w, v_mod_b, v_g_pre_mix, v_g_post_mix, v_g_pre_ffn, v_g_post_ffn, v_w_in, v_sc_conv_w, v_ssm_conv_w, v_ssm_conv_b, v_ssm_dt_bias, v_ssm_a_log, v_ssm_d, v_ssm_norm_w, v_w_sc_out, v_w_sb_out, v_w_ssm_out, v_w_o, v_w_ffn_in, v_w_ffn_out):
    wts = dict(mod_w=mod_w, mod_b=mod_b, g_pre_mix=g_pre_mix, g_post_mix=g_post_mix, g_pre_ffn=g_pre_ffn,
               g_post_ffn=g_post_ffn, w_in=w_in, sc_conv_w=sc_conv_w, ssm_conv_w=ssm_conv_w, ssm_conv_b=ssm_conv_b,
               ssm_dt_bias=ssm_dt_bias, ssm_a_log=ssm_a_log, ssm_d=ssm_d, ssm_norm_w=ssm_norm_w, w_sc_out=w_sc_out,
               w_sb_out=w_sb_out, w_ssm_out=w_ssm_out, w_o=w_o, w_ffn_in=w_ffn_in, w_ffn_out=w_ffn_out)
    mom = dict(mod_w=m_mod_w, mod_b=m_mod_b, g_pre_mix=m_g_pre_mix, g_post_mix=m_g_post_mix, g_pre_ffn=m_g_pre_ffn,
               g_post_ffn=m_g_post_ffn, w_in=m_w_in, sc_conv_w=m_sc_conv_w, ssm_conv_w=m_ssm_conv_w,
               ssm_conv_b=m_ssm_conv_b, ssm_dt_bias=m_ssm_dt_bias, ssm_a_log=m_ssm_a_log, ssm_d=m_ssm_d,
               ssm_norm_w=m_ssm_norm_w, w_sc_out=m_w_sc_out, w_sb_out=m_w_sb_out, w_ssm_out=m_w_ssm_out, w_o=m_w_o,
               w_ffn_in=m_w_ffn_in, w_ffn_out=m_w_ffn_out)
    var = dict(mod_w=v_mod_w, mod_b=v_mod_b, g_pre_mix=v_g_pre_mix, g_post_mix=v_g_post_mix, g_pre_ffn=v_g_pre_ffn,
               g_post_ffn=v_g_post_ffn, w_in=v_w_in, sc_conv_w=v_sc_conv_w, ssm_conv_w=v_ssm_conv_w,
               ssm_conv_b=v_ssm_conv_b, ssm_dt_bias=v_ssm_dt_bias, ssm_a_log=v_ssm_a_log, ssm_d=v_ssm_d,
               ssm_norm_w=v_ssm_norm_w, w_sc_out=v_w_sc_out, w_sb_out=v_w_sb_out, w_ssm_out=v_w_ssm_out, w_o=v_w_o,
               w_ffn_in=v_w_ffn_in, w_ffn_out=v_w_ffn_out)
    xi, yi, ci = _here()
    chip = 2 * xi + yi
    me = 4 * xi + 2 * yi + ci
    x0, target = x[0], loss_target[0]

    first_shapes = [(D_MODEL,), sc_conv_w.shape, ssm_conv_w.shape]
    g0 = _allgather_small("gather_cond", _pack([c, sc_conv_w, ssm_conv_w]))
    c_rows, sc_sh, ssm_sh = _unpack(g0, first_shapes)
    c_all = c_rows
    sc_w = jnp.concatenate([sc_sh[2 * j] for j in range(N_CHIPS)], axis=-1)
    ssm_w = jnp.concatenate([ssm_sh[2 * j] for j in range(N_CHIPS)], axis=-1)

    mod_b_sh = lax.dynamic_slice_in_dim(mod_b, chip * MOD_SHARD, MOD_SHARD, axis=1).reshape(DEPTH, 1, MOD_SHARD)
    modpart = _mod_fwd(c_all, mod_w, mod_b_sh)
    g1 = _allgather_small("gather_mod", modpart.reshape(-1, LANES)).reshape(N_DEV, DEPTH, N_DEV, MOD_SHARD)
    mod = jnp.concatenate([lax.dynamic_index_in_dim(g1[2 * j], me, axis=1, keepdims=False) for j in range(N_CHIPS)],
                          axis=-1)

    def layer_shards(l):
        return [wts[n][l].astype(BF16) for n, _ in BIG]

    def full_weights(which, gathered):
        W = {n: jnp.concatenate([g[j] for j in range(N_CHIPS)], axis=ax - 1) for (n, ax), g in zip(which, gathered)}
        if "w_in" in W:
            W["w_in"] = _pad_w_in(W["w_in"])
        if "w_ffn_in" in W:
            W["w_ffn_in"] = _ffn_in_cols(W["w_ffn_in"])
        return W

    Ws = [{}, {}]
    fwd_sides = [{"in_proj": (_gather_side(layer_shards(0)[1:]), lambda got: Ws[0].update(full_weights(BIG[1:], got))),
                  "ffn_in": (_gather_side(layer_shards(1)), lambda got: Ws[1].update(full_weights(BIG, got)))}, {}]
    Vs = []
    for l in range(DEPTH):
        sh1, sc1, gt1, sh2, sc2, gt2 = [_row(v) for v in jnp.split(mod[l], N_MOD)]
        Vs.append(dict(
            shift1=sh1, scale1=sc1, g_pre_mix=_row(g_pre_mix[l]),
            mid_mix=[gt1, _row(g_post_mix[l]), _row(g_pre_ffn[l]), sc2, sh2],
            gate2=gt2, g_post_ffn=_row(g_post_ffn[l]),
            sc_w=sc_w[l], ssm_w=ssm_w[l], ssm_b=_row(ssm_conv_b[l]),
            dtb=_row(jnp.pad(ssm_dt_bias[l], (0, LANES - SSM_HEADS))), alog=_row(jnp.pad(ssm_a_log[l], (0, LANES - SSM_HEADS))),
            d_full=_row(jnp.repeat(ssm_d[l], SSM_INNER // SSM_HEADS)), norm_w=_row(ssm_norm_w[l])))

    def mid_ffn_vecs(l):
        return [Vs[l]["gate2"], Vs[l]["g_post_ffn"], Vs[l + 1]["g_pre_mix"], Vs[l + 1]["scale1"], Vs[l + 1]["shift1"]]

    saved = []
    x_in = x0
    h, *got = _first_fwd(x0, [Vs[0]["g_pre_mix"], Vs[0]["scale1"], Vs[0]["shift1"]], _gather_side(layer_shards(0)[:1]))
    Ws[0].update(full_weights(BIG[:1], got))
    for l in range(DEPTH):
        S = _layer_fwd(l, x_in, h, Ws[l], Vs[l], fwd_sides[l], mid_ffn_vecs(l) if l + 1 < DEPTH else None)
        saved.append(S)
        if l + 1 < DEPTH:
            x_in, h = S["x_next"], S["h_next"]

    def pieces(G, names):
        out = []
        for n, ax in BIG:
            if n in names:
                g = _unpad_w_in(G[n]) if n == "w_in" else G[n]
                out.append(jnp.stack(jnp.split(g, N_CHIPS, axis=ax - 1)).astype(BF16))
        return out

    small_names = tuple(n for n, _ in BIG if n not in ("w_in", "w_ffn_in"))
    late_names = tuple(n for n, _ in BIG if n != "w_in")
    landed = {}

    GL = [None] * DEPTH
    S = saved[-1]
    dx1, df, g_gate2, g_gpf, loss_cols = _last_bwd(S["x1"], S["f"], target, [Vs[-1]["gate2"], Vs[-1]["g_post_ffn"]])
    for l in reversed(range(DEPTH)):
        sides = {}
        if l + 1 < DEPTH:
            for key, names in (("d_gu", small_names), ("d_h2", ("w_ffn_in",)), ("gw_ffn_in", ("w_in",))):
                sides[key] = (names, l + 1, lambda G, up=GL[l + 1], names=names: _scatter_side(pieces(up, names)))
        if l == 0:
            sides["d_h"] = (late_names, l, lambda G: _scatter_side(pieces(G, late_names)))
        dx, dh, G = _layer_bwd(l, saved[l], Ws[l], Vs[l], dx1, df, sides, landed)
        G["gate2"], G["g_post_ffn"] = g_gate2, g_gpf
        GL[l] = G
        if l > 0:
            Sp = saved[l - 1]
            dx1, df, g_gate2, g_gpf, G["g_pre_mix"], G["scale1"], G["shift1"] = _mid_bwd(
                f"mid_ffn_bwd{l - 1}", Sp["x1"], Sp["f"], dx, dh, mid_ffn_vecs(l - 1))
        else:
            grad_x, G["g_pre_mix"], G["scale1"], G["shift1"], landed[("w_in", 0)] = _first_bwd(
                x0, dx, dh, [Vs[0]["g_pre_mix"], Vs[0]["scale1"], Vs[0]["shift1"]], _scatter_side(pieces(G, ("w_in",))))
    loss = lax.psum(jnp.sum(loss_cols), ("x", "y", "c"))

    def both(key, shape=None):
        a = jnp.stack([GL[l][key] for l in range(DEPTH)])
        return a if shape is None else a.reshape(shape)

    dmod = jnp.concatenate([both(k, (DEPTH, D_MODEL)) for k in ("shift1", "scale1", "gate1", "shift2", "scale2", "gate2")],
                           axis=1)
    part_small = dict(
        mod_b=dmod, g_pre_mix=both("g_pre_mix", (DEPTH, D_MODEL)), g_post_mix=both("g_post_mix", (DEPTH, D_MODEL)),
        g_pre_ffn=both("g_pre_ffn", (DEPTH, D_MODEL)), g_post_ffn=both("g_post_ffn", (DEPTH, D_MODEL)),
        sc_conv_w=both("sc_conv_w"), ssm_conv_w=both("ssm_conv_w"), ssm_conv_b=both("ssm_conv_b", (DEPTH, SSM_CONV_DIM)),
        ssm_dt_bias=both("dtb", (DEPTH, LANES))[:, :SSM_HEADS], ssm_a_log=both("alog", (DEPTH, LANES))[:, :SSM_HEADS],
        ssm_d=both("d_full", (DEPTH, SSM_HEADS, SSM_INNER // SSM_HEADS)).sum(-1),
        ssm_norm_w=both("ssm_norm_w", (DEPTH, SSM_INNER)))
    small_shapes = [part_small[n].shape for n in SMALL]
    g2 = _allgather_small("gather_small_grads", _pack([part_small[n] for n in SMALL]))
    tot = dict(zip(SMALL, _unpack(_sum_slots("sum_small_grads", g2), small_shapes)))
    dmod_all = _unpack(g2, small_shapes)[0]
    dmod_sh = jnp.swapaxes(lax.dynamic_slice_in_dim(dmod_all, chip * MOD_SHARD, MOD_SHARD, axis=2), 0, 1)
    grads = {"mod_w": _mod_bwd(c_all, dmod_sh)}
    for n in SMALL:
        grads[n] = tot[n]
    grads["sc_conv_w"] = lax.dynamic_slice_in_dim(tot["sc_conv_w"], chip * 64, 64, axis=2)
    grads["ssm_conv_w"] = lax.dynamic_slice_in_dim(tot["ssm_conv_w"], chip * 192, 192, axis=2)

    keys = [(n, l) for n, _ in BIG for l in range(DEPTH)]
    mine = [_sum_slots(f"sum_{n}{l}", landed[(n, l)]) for n, l in keys]
    theirs = dict(zip(keys, _sibling_exchange("swap_core_sums", mine)))
    mine = dict(zip(keys, mine))

    out = {}

    def update(name, w2, gs, m2, v2, shape):
        g, d, nm, nv = _adamw(f"adamw_{name}", w2, gs, m2, v2)
        out[name] = tuple(a.reshape(shape) for a in (g, d, nm, nv))

    for n, _ in BIG:
        shp = wts[n].shape
        two = (-1, shp[-1])
        by_layer = [tuple(src[(n, l)] for l in range(DEPTH)) for src in (mine, theirs)]
        update(n, wts[n].reshape(two), by_layer, mom[n].reshape(two), var[n].reshape(two), shp)
    two = (-1, MOD_SHARD)
    update("mod_w", mod_w.reshape(two), [grads["mod_w"].reshape(two)], m_mod_w.reshape(two), v_mod_w.reshape(two), mod_w.shape)
    shapes = [wts[n].shape for n in SMALL]
    res = _adamw("adamw_small", _pack([wts[n] for n in SMALL]), [_pack([grads[n] for n in SMALL])],
                 _pack([mom[n] for n in SMALL]), _pack([var[n] for n in SMALL]))
    for n, g, d, nm, nv in zip(SMALL, *[_unpack(r, shapes) for r in res]):
        out[n] = (g, d, nm, nv)

    result = [loss, grad_x[None]]
    for k in range(4):
        result += [out[n][k] for n in WEIGHT_ORDER]
    return tuple(result)
```

```python
import functools

import jax
import jax.numpy as jnp
from jax import lax
from jax.experimental import pallas as pl
from jax.experimental.pallas import tpu as pltpu

F32 = jnp.float32
BF16 = jnp.bfloat16
HIGHEST = lax.Precision.HIGHEST
MESH_ID = pl.DeviceIdType.MESH

D_MODEL = 1024
DEPTH = 2
SC_WIDTH = 256
SB_HEAD_DIM = 64
SSM_INNER = 512
SSM_HEADS = 8
SSM_STATE = 64
SSM_CONV = 4
SSM_CHUNK = 256
SSM_CONV_DIM = 768
FFN_HIDDEN = 2816
NORM_EPS = 1e-6
N_MOD = 6
N_CHIPS = 4
N_DEV = 8

ADAM_LR = 0.001
ADAM_B1 = 0.9
ADAM_B2 = 0.999
ADAM_EPS = 1e-08
ADAM_WD = 0.01
ADAM_STEP = 10

P_WIDTH = 6144
P_A, P_B, P_Z, P_DT, P_XBC, P_G = 0, 768, 1536, 2048, 2304, 3072
DT_PAD = 256

VMEM_LIMIT_BYTES = 56 * 1024 * 1024
LANES = 128

SB_LOG_CUTOFF = -105.0
SB_TQ = 256
SB_TK = 256
SB_SUBS = 2
SB_STRAIGHT = 2


def _params(sem):
    return pltpu.CompilerParams(dimension_semantics=sem, vmem_limit_bytes=VMEM_LIMIT_BYTES)


def _pick(n, cap):
    if n <= cap:
        return n
    best = None
    for m in range(LANES, cap + 1, LANES):
        if n % m == 0:
            best = m
    assert best is not None, (n, cap)
    return best


def _rowwise(name, fn, rows, vecs, row_outs, acc_outs=(), tl=256, side=None):
    L = rows[0][0].shape[0]
    tl = min(tl, L)
    assert L % tl == 0
    n_in = len(rows) + len(vecs)
    n_ro, n_ao = len(row_outs), len(acc_outs)
    n_si = len(side.arrays) if side else 0
    n_so = len(side.out_shapes) if side else 0

    def body(*refs):
        ins, s_in = refs[:n_in], refs[n_in:n_in + n_si]
        outs = refs[n_in + n_si:]
        ro, ao, s_out, sems = outs[:n_ro], outs[n_ro:n_ro + n_ao], outs[n_ro + n_ao:n_ro + n_ao + n_so], outs[n_ro + n_ao + n_so:]
        if side:
            @pl.when(pl.program_id(0) == 0)
            def _():
                side.start(s_in, s_out, sems)

        _rows(ins, ro, ao)
        if side:
            @pl.when(pl.program_id(0) == L // tl - 1)
            def _():
                side.finish(s_in, s_out, sems)

    def _rows(ins, ro, ao):
        vals = fn(*[r[...] for r in ins])
        if not isinstance(vals, (tuple, list)):
            vals = (vals,)
        for o, v in zip(ro, vals[:n_ro]):
            o[...] = v.astype(o.dtype)
        if ao:
            @pl.when(pl.program_id(0) == 0)
            def _():
                for o in ao:
                    o[...] = jnp.zeros_like(o)
            for o, v in zip(ao, vals[n_ro:]):
                o[...] += v.astype(F32)

    in_specs = [pl.BlockSpec((tl, w), functools.partial(lambda i, cb: (i, cb), cb=cb)) for _, w, cb in rows]
    in_specs += [pl.BlockSpec(v.shape, lambda i: (0, 0)) for v in vecs]
    out_specs = [pl.BlockSpec((tl, w), lambda i: (i, 0)) for w, _ in row_outs]
    out_specs += [pl.BlockSpec(s, lambda i: (0, 0)) for s in acc_outs]
    out_shape = [jax.ShapeDtypeStruct((L, w), dt) for w, dt in row_outs]
    out_shape += [jax.ShapeDtypeStruct(s, F32) for s in acc_outs]
    any_spec = pl.BlockSpec(memory_space=pl.ANY)
    return pl.pallas_call(
        body, name=name, grid=(L // tl,), in_specs=in_specs + [any_spec] * n_si, out_specs=out_specs + [any_spec] * n_so,
        out_shape=out_shape + (side.out_shapes if side else []), scratch_shapes=side.scratch if side else [],
        compiler_params=_params(("arbitrary",)),
    )(*[a for a, _, _ in rows], *vecs, *(side.arrays if side else []))


def _mm(name, a, b, mode, out_dtype, tm=1024, tn_cap=1408, tk_cap=2816, side=None):
    if mode == "nn":
        (M, K), (_, N) = a.shape, b.shape
    elif mode == "nt":
        (M, K), (N, _) = a.shape, b.shape
    else:
        (K, M), (_, N) = a.shape, b.shape
        tm, tk_cap = 1408, 2048
    tm = _pick(M, tm)
    tn = _pick(N, tn_cap)
    tk = _pick(K, tk_cap)
    nk = K // tk
    grid = (M // tm, N // tn, nk)
    n_si = len(side.arrays) if side else 0
    n_so = len(side.out_shapes) if side else 0
    n_acc = 1 if nk > 1 else 0

    def body(a_ref, b_ref, *rest):
        s_in, o_ref, s_out = rest[:n_si], rest[n_si], rest[n_si + 1:n_si + 1 + n_so]
        scr = rest[n_si + 1 + n_so:]
        if side:
            at = [pl.program_id(d) for d in range(3)]
            is_first = jnp.logical_and(jnp.logical_and(at[0] == 0, at[1] == 0), at[2] == 0)
            is_last = jnp.logical_and(jnp.logical_and(at[0] == grid[0] - 1, at[1] == grid[1] - 1), at[2] == grid[2] - 1)

            @pl.when(is_first)
            def _():
                side.start(s_in, s_out, scr[n_acc:])

        _product(a_ref, b_ref, o_ref, scr)
        if side:
            @pl.when(is_last)
            def _():
                side.finish(s_in, s_out, scr[n_acc:])

    def _product(a_ref, b_ref, o_ref, scr):
        if mode == "nn":
            p = jnp.dot(a_ref[...], b_ref[...], preferred_element_type=F32)
        elif mode == "nt":
            p = lax.dot_general(a_ref[...], b_ref[...], (((1,), (1,)), ((), ())), preferred_element_type=F32)
        else:
            p = lax.dot_general(a_ref[...], b_ref[...], (((0,), (0,)), ((), ())), preferred_element_type=F32)
        if nk == 1:
            o_ref[...] = p.astype(o_ref.dtype)
        else:
            acc = scr[0]
            k = pl.program_id(2)

            @pl.when(k == 0)
            def _():
                acc[...] = p

            @pl.when(k > 0)
            def _():
                acc[...] += p

            @pl.when(k == nk - 1)
            def _():
                o_ref[...] = acc[...].astype(o_ref.dtype)

    if mode == "nn":
        a_spec = pl.BlockSpec((tm, tk), lambda i, j, k: (i, k))
        b_spec = pl.BlockSpec((tk, tn), lambda i, j, k: (k, j))
    elif mode == "nt":
        a_spec = pl.BlockSpec((tm, tk), lambda i, j, k: (i, k))
        b_spec = pl.BlockSpec((tn, tk), lambda i, j, k: (j, k))
    else:
        a_spec = pl.BlockSpec((tk, tm), lambda i, j, k: (k, i))
        b_spec = pl.BlockSpec((tk, tn), lambda i, j, k: (k, j))
    any_spec = pl.BlockSpec(memory_space=pl.ANY)
    res = pl.pallas_call(
        body, name=name, grid=grid, in_specs=[a_spec, b_spec] + [any_spec] * n_si,
        out_specs=[pl.BlockSpec((tm, tn), lambda i, j, k: (i, j))] + [any_spec] * n_so,
        out_shape=[jax.ShapeDtypeStruct((M, N), out_dtype)] + (side.out_shapes if side else []),
        scratch_shapes=([pltpu.VMEM((tm, tn), F32)] if nk > 1 else []) + (side.scratch if side else []),
        compiler_params=_params(("arbitrary", "arbitrary", "arbitrary")),
    )(a, b, *(side.arrays if side else []))
    return res if side else res[0]


def _mm_epi(name, a, b, mode, tn, extras, epi, outs, tm=512, side=None):
    if mode == "nn":
        (M, K), (_, N) = a.shape, b.shape
    else:
        (M, K), (N, _) = a.shape, b.shape
    tm = _pick(M, tm)
    grid = (N // tn, M // tm)
    n_ex, n_out = len(extras), len(outs)
    n_si = len(side.arrays) if side else 0
    n_so = len(side.out_shapes) if side else 0

    def body(*refs):
        a_ref, b_ref, ex = refs[0], refs[1], refs[2:2 + n_ex]
        s_in = refs[2 + n_ex:2 + n_ex + n_si]
        o_refs = refs[2 + n_ex + n_si:2 + n_ex + n_si + n_out]
        s_out = refs[2 + n_ex + n_si + n_out:2 + n_ex + n_si + n_out + n_so]
        sems = refs[2 + n_ex + n_si + n_out + n_so:]
        if side:
            @pl.when(jnp.logical_and(pl.program_id(0) == 0, pl.program_id(1) == 0))
            def _():
                side.start(s_in, s_out, sems)

        if mode == "nn":
            p = jnp.dot(a_ref[...], b_ref[...], preferred_element_type=F32)
        else:
            p = lax.dot_general(a_ref[...], b_ref[...], (((1,), (1,)), ((), ())), preferred_element_type=F32)
        for o, v in zip(o_refs, epi(p, *[r[...] for r in ex])):
            o[...] = v.astype(o.dtype)
        if side:
            @pl.when(jnp.logical_and(pl.program_id(0) == grid[0] - 1, pl.program_id(1) == grid[1] - 1))
            def _():
                side.finish(s_in, s_out, sems)

    any_spec = pl.BlockSpec(memory_space=pl.ANY)
    a_spec = pl.BlockSpec((tm, K), lambda j, i: (i, 0))
    b_spec = pl.BlockSpec((K, tn), lambda j, i: (0, j)) if mode == "nn" else pl.BlockSpec((tn, K), lambda j, i: (j, 0))
    return pl.pallas_call(
        body, name=name, grid=grid,
        in_specs=[a_spec, b_spec]
        + [pl.BlockSpec(e.shape, lambda j, i: (0, 0)) if w is None else pl.BlockSpec((tm, w), lambda j, i: (i, j))
           for e, w in extras] + [any_spec] * n_si,
        out_specs=[pl.BlockSpec((tm, w), lambda j, i: (i, j)) for w, _ in outs] + [any_spec] * n_so,
        out_shape=[jax.ShapeDtypeStruct((M, (N // tn) * w), dt) for w, dt in outs] + (side.out_shapes if side else []),
        scratch_shapes=side.scratch if side else [],
        compiler_params=_params(("arbitrary", "arbitrary")),
    )(a, b, *[e for e, _ in extras], *(side.arrays if side else []))


def _f(x):
    return x.astype(F32)


def _silu(x):
    return x * jax.nn.sigmoid(x)


def _softplus(x):
    return jnp.maximum(x, 0.0) + jnp.log1p(jnp.exp(-jnp.abs(x)))


def _rms(x, g):
    r = lax.rsqrt(jnp.mean(x * x, axis=-1, keepdims=True) + NORM_EPS)
    return x * r * g


def _adaln(x, g, scale, shift):
    return _rms(x, g) * (1.0 + scale) + shift


def _resid(x, y, gate, g):
    return x + gate * _rms(y, g)


def _mid(x, y, gate, g_post, g_pre, scale, shift):
    x_new = _resid(x, y, gate, g_post)
    return x_new, _adaln(x_new, g_pre, scale, shift)


def _swiglu(gt, up):
    return _silu(gt) * up


def _ssm_post(y_ssd, pre_xs, z, d_full, norm_w):
    y = (y_ssd + _silu(pre_xs) * d_full) * _silu(z)
    half = SSM_INNER // 2
    parts = []
    for g in range(2):
        yg = y[:, g * half:(g + 1) * half]
        parts.append(yg * lax.rsqrt(jnp.mean(yg * yg, axis=-1, keepdims=True) + NORM_EPS))
    return jnp.concatenate(parts, axis=1) * norm_w


def _first_fwd(x, vecs, side=None):
    return _rowwise("adaln_first", lambda x, g, sc, sh: _adaln(x, g, sc, sh),
                    [(x, D_MODEL, 0)], vecs, [(D_MODEL, BF16)], tl=512, side=side)


def _mid_bwd(name, x, y, dx_new, dh, vecs):
    def fn(x, y, dxn, dh, *v):
        _, vjp = jax.vjp(_mid, x, _f(y), *v)
        return vjp((dxn, _f(dh)))

    vec = (1, D_MODEL)
    return _rowwise(name, fn, [(x, D_MODEL, 0), (y, D_MODEL, 0), (dx_new, D_MODEL, 0), (dh, D_MODEL, 0)], vecs,
                    [(D_MODEL, F32), (D_MODEL, BF16)], [vec] * 5)


def _first_bwd(x, dx_in, dh, vecs, side=None):
    def fn(x, dxi, dh, *v):
        _, vjp = jax.vjp(_adaln, x, *v)
        dx, dg, dsc, dsh = vjp(_f(dh))
        return dx + dxi, dg, dsc, dsh

    vec = (1, D_MODEL)
    return _rowwise("adaln_first_bwd", fn, [(x, D_MODEL, 0), (dx_in, D_MODEL, 0), (dh, D_MODEL, 0)], vecs,
                    [(D_MODEL, F32)], [vec] * 3, side=side)


def _last_bwd(x1, f, target, vecs):
    def fn(x1, f, t, gate, g):
        x2, vjp = jax.vjp(_resid, x1, _f(f), gate, g)
        err = x2 - t
        dx1, df, dgate, dg = vjp(err * (1.0 / D_MODEL))
        loss_cols = jnp.sum(err * err, axis=0, keepdims=True) * (0.5 / D_MODEL)
        return dx1, df, dgate, dg, loss_cols

    vec = (1, D_MODEL)
    return _rowwise("loss_last_bwd", fn, [(x1, D_MODEL, 0), (f, D_MODEL, 0), (target, D_MODEL, 0)], vecs,
                    [(D_MODEL, F32), (D_MODEL, BF16)], [vec] * 3)


HALO = 16


def _shift_down(u, prev, k):
    rows = lax.broadcasted_iota(jnp.int32, u.shape, 0)
    v = pltpu.roll(u, k, 0)
    for t in range(k):
        v = jnp.where(rows == t, prev[HALO - k + t:HALO - k + t + 1, :], v)
    return v


def _shift_up(u, nxt, k):
    n = u.shape[0]
    rows = lax.broadcasted_iota(jnp.int32, u.shape, 0)
    v = pltpu.roll(u, n - k, 0)
    for t in range(k):
        v = jnp.where(rows == n - k + t, nxt[t:t + 1, :], v)
    return v


def _conv_specs(L, tl, width, col_block):
    per = tl // HALO
    last = L // HALO - 1
    main = pl.BlockSpec((tl, width), lambda i: (i, col_block))
    before = pl.BlockSpec((HALO, width), lambda i: (jnp.maximum(i * per - 1, 0), col_block))
    after = pl.BlockSpec((HALO, width), lambda i: (jnp.minimum((i + 1) * per, last), col_block))
    return main, before, after


def _shortconv_fwd(P, w, tl=512):
    L = P.shape[0]
    tl = min(tl, L)
    C = SC_WIDTH
    main, before, _ = _conv_specs(L, tl, 3 * C, 0)

    def body(p_ref, h_ref, w_ref, o_ref):
        first = (pl.program_id(0) == 0)
        p, h = _f(p_ref[...]), _f(h_ref[...])
        b, u = p[:, :C], p[:, C:2 * C] * p[:, 2 * C:]
        uh = jnp.where(first, 0.0, h[:, C:2 * C] * h[:, 2 * C:])
        wv = w_ref[...]
        cv = wv[2:3] * u + wv[1:2] * _shift_down(u, uh, 1) + wv[0:1] * _shift_down(u, uh, 2)
        o_ref[...] = (b * cv).astype(o_ref.dtype)

    return pl.pallas_call(
        body, name="shortconv_fwd", grid=(L // tl,),
        in_specs=[main, before, pl.BlockSpec(w.shape, lambda i: (0, 0))],
        out_specs=pl.BlockSpec((tl, C), lambda i: (i, 0)),
        out_shape=jax.ShapeDtypeStruct((L, C), BF16), compiler_params=_params(("arbitrary",)),
    )(P, P, w)


def _shortconv_bwd(P, dya, w, tl=512):
    L = P.shape[0]
    tl = min(tl, L)
    C = SC_WIDTH
    main, before, after = _conv_specs(L, tl, 3 * C, 0)
    dmain, _, dafter = _conv_specs(L, tl, C, 0)
    n = L // tl

    def body(p_ref, h_ref, n_ref, d_ref, dn_ref, w_ref, o_ref, dw0, dw1, dw2):
        i = pl.program_id(0)
        p, h, nx = _f(p_ref[...]), _f(h_ref[...]), _f(n_ref[...])
        b, c, x = p[:, :C], p[:, C:2 * C], p[:, 2 * C:]
        u = c * x
        uh = jnp.where(i == 0, 0.0, h[:, C:2 * C] * h[:, 2 * C:])
        u1, u2 = _shift_down(u, uh, 1), _shift_down(u, uh, 2)
        wv = w_ref[...]
        cv = wv[2:3] * u + wv[1:2] * u1 + wv[0:1] * u2
        dy = _f(d_ref[...])
        dcv = dy * b
        dcv_n = jnp.where(i == n - 1, 0.0, _f(dn_ref[...]) * nx[:, :C])
        du = wv[2:3] * dcv + wv[1:2] * _shift_up(dcv, dcv_n, 1) + wv[0:1] * _shift_up(dcv, dcv_n, 2)
        o_ref[:, :C] = (dy * cv).astype(o_ref.dtype)
        o_ref[:, C:2 * C] = (du * x).astype(o_ref.dtype)
        o_ref[:, 2 * C:] = (du * c).astype(o_ref.dtype)

        @pl.when(i == 0)
        def _():
            for r in (dw0, dw1, dw2):
                r[...] = jnp.zeros_like(r)

        dw0[...] += jnp.sum(dcv * u2, axis=0, keepdims=True)
        dw1[...] += jnp.sum(dcv * u1, axis=0, keepdims=True)
        dw2[...] += jnp.sum(dcv * u, axis=0, keepdims=True)

    vec = pl.BlockSpec((1, C), lambda i: (0, 0))
    return pl.pallas_call(
        body, name="shortconv_bwd", grid=(n,),
        in_specs=[main, before, after, dmain, dafter, pl.BlockSpec(w.shape, lambda i: (0, 0))],
        out_specs=[pl.BlockSpec((tl, 3 * C), lambda i: (i, 0)), vec, vec, vec],
        out_shape=[jax.ShapeDtypeStruct((L, 3 * C), BF16)] + [jax.ShapeDtypeStruct((1, C), F32)] * 3,
        compiler_params=_params(("arbitrary",)),
    )(P, P, P, dya, dya, w)


def _ssmconv_fwd(P, w, bias, tl=512):
    L = P.shape[0]
    tl = min(tl, L)
    C = SSM_CONV_DIM
    main, before, _ = _conv_specs(L, tl, C, P_XBC // C)

    def body(p_ref, h_ref, w_ref, b_ref, o_ref):
        u = _f(p_ref[...])
        uh = jnp.where(pl.program_id(0) == 0, 0.0, _f(h_ref[...]))
        wv = w_ref[...]
        acc = wv[3:4] * u + b_ref[...]
        for k in range(1, SSM_CONV):
            acc = acc + wv[3 - k:4 - k] * _shift_down(u, uh, k)
        o_ref[...] = acc

    return pl.pallas_call(
        body, name="ssmconv_fwd", grid=(L // tl,),
        in_specs=[main, before, pl.BlockSpec(w.shape, lambda i: (0, 0)), pl.BlockSpec(bias.shape, lambda i: (0, 0))],
        out_specs=pl.BlockSpec((tl, C), lambda i: (i, 0)),
        out_shape=jax.ShapeDtypeStruct((L, C), F32), compiler_params=_params(("arbitrary",)),
    )(P, P, w, bias)


def _ssmconv_bwd(P, dpre, w, tl=512):
    L = P.shape[0]
    tl = min(tl, L)
    C = SSM_CONV_DIM
    main, before, _ = _conv_specs(L, tl, C, P_XBC // C)
    dmain, _, dafter = _conv_specs(L, tl, C, 0)
    n = L // tl

    def body(p_ref, h_ref, d_ref, dn_ref, w_ref, o_ref, dw0, dw1, dw2, dw3, db):
        i = pl.program_id(0)
        u = _f(p_ref[...])
        uh = jnp.where(i == 0, 0.0, _f(h_ref[...]))
        d = d_ref[...]
        dn = jnp.where(i == n - 1, 0.0, dn_ref[...])
        wv = w_ref[...]
        du = wv[3:4] * d
        for k in range(1, SSM_CONV):
            du = du + wv[3 - k:4 - k] * _shift_up(d, dn, k)
        o_ref[...] = du.astype(o_ref.dtype)

        @pl.when(i == 0)
        def _():
            for r in (dw0, dw1, dw2, dw3, db):
                r[...] = jnp.zeros_like(r)

        for k, r in ((3, dw0), (2, dw1), (1, dw2)):
            r[...] += jnp.sum(d * _shift_down(u, uh, k), axis=0, keepdims=True)
        dw3[...] += jnp.sum(d * u, axis=0, keepdims=True)
        db[...] += jnp.sum(d, axis=0, keepdims=True)

    vec = pl.BlockSpec((1, C), lambda i: (0, 0))
    return pl.pallas_call(
        body, name="ssmconv_bwd", grid=(n,),
        in_specs=[main, before, dmain, dafter, pl.BlockSpec(w.shape, lambda i: (0, 0))],
        out_specs=[pl.BlockSpec((tl, C), lambda i: (i, 0))] + [vec] * 5,
        out_shape=[jax.ShapeDtypeStruct((L, C), BF16)] + [jax.ShapeDtypeStruct((1, C), F32)] * 5,
        compiler_params=_params(("arbitrary",)),
    )(P, P, dpre, dpre, w)


def _dot_nt(a, b):
    return lax.dot_general(a, b, (((1,), (1,)), ((), ())), preferred_element_type=F32)


def _dot_tn(a, b):
    return lax.dot_general(a, b, (((0,), (0,)), ((), ())), preferred_element_type=F32)


def _split3(x):
    hi = x.astype(BF16)
    r = x - hi.astype(F32)
    mid = r.astype(BF16)
    return hi, mid, (r - mid.astype(F32)).astype(BF16)


@jax.custom_vjp
def _xm01(x, m):
    return sum(jnp.dot(t, m, preferred_element_type=F32) for t in _split3(x))


def _xm01_fwd(x, m):
    return _xm01(x, m), m


def _xm01_bwd(m, g):
    return sum(_dot_nt(t, m) for t in _split3(g)), jnp.zeros_like(m)


_xm01.defvjp(_xm01_fwd, _xm01_bwd)


@jax.custom_vjp
def _m01x(m, x):
    return sum(jnp.dot(m, t, preferred_element_type=F32) for t in _split3(x))


def _m01x_fwd(m, x):
    return _m01x(m, x), m


def _m01x_bwd(m, g):
    return jnp.zeros_like(m), sum(_dot_tn(m, t) for t in _split3(g))


_m01x.defvjp(_m01x_fwd, _m01x_bwd)


def _ssd_chunk(pre, dtr, s_prev, dtb, alog):
    T = pre.shape[0]
    act = _silu(pre)
    xs, bm, cm = act[:, :SSM_INNER], act[:, SSM_INNER:SSM_INNER + 128], act[:, SSM_INNER + 128:]
    lane = lax.broadcasted_iota(jnp.int32, (1, LANES), 1)
    dt = jnp.where(lane < SSM_HEADS, _softplus(dtr + dtb), 0.0)
    a = dt * (-jnp.exp(alog))
    ri = lax.broadcasted_iota(jnp.int32, (T, T), 0)
    ci = lax.broadcasted_iota(jnp.int32, (T, T), 1)
    causal = ci <= ri
    a_cs = _m01x(causal.astype(BF16), a)
    eh = lax.broadcasted_iota(jnp.int32, (LANES, SSM_INNER), 0)
    ej = lax.broadcasted_iota(jnp.int32, (LANES, SSM_INNER), 1)
    expand = (lax.shift_right_logical(ej, 6) == eh).astype(BF16)
    dt_full = _xm01(dt, expand)
    acs_full = _xm01(a_cs, expand)
    alast_full = acs_full[T - 1:T, :]
    xdt = xs * dt_full
    a_cs_t = a_cs.T
    ys, s_new = [], []
    for g in range(2):
        in_group = lax.shift_right_logical(lane, 6) == g
        cg = jnp.where(in_group, cm, 0.0).astype(BF16)
        bg = jnp.where(in_group, bm, 0.0).astype(BF16)
        scores = _dot_nt(cg, bg)
        for pp in range(2):
            hp = 2 * g + pp
            cols = slice(hp * LANES, (hp + 1) * LANES)
            xp, acsp = xdt[:, cols], acs_full[:, cols]
            per_head = []
            for hh in range(2):
                h = 2 * hp + hh
                decay = jnp.exp(jnp.where(causal, a_cs[:, h:h + 1] - a_cs_t[h:h + 1, :], -jnp.inf))
                per_head.append(jnp.dot((scores * decay).astype(BF16), xp.astype(BF16), preferred_element_type=F32))
            y_diag = jnp.where(lane < SSM_STATE, per_head[0], per_head[1])
            sp = s_prev[hp * LANES:(hp + 1) * LANES, :]
            y_off = jnp.dot(cg, sp.astype(BF16), preferred_element_type=F32) * jnp.exp(acsp)
            ys.append(y_diag + y_off)
            to_end = jnp.exp(alast_full[:, cols] - acsp)
            s_new.append(sp * jnp.exp(alast_full[:, cols]) + _dot_tn(bg, (xp * to_end).astype(BF16)))
    return jnp.concatenate(ys, axis=1), jnp.concatenate(s_new, axis=0)


def _ssd_fwd(pre, P, dtb, alog):
    L = pre.shape[0]
    T = min(SSM_CHUNK, L)
    nc = L // T

    def body(pre_ref, dt_ref, dtb_ref, al_ref, y_ref, st_ref, s_scr):
        @pl.when(pl.program_id(0) == 0)
        def _():
            s_scr[...] = jnp.zeros_like(s_scr)

        st_ref[0] = s_scr[...]
        y, s = _ssd_chunk(pre_ref[...], _f(dt_ref[...]), s_scr[...], dtb_ref[...], al_ref[...])
        y_ref[...] = y
        s_scr[...] = s

    vec = pl.BlockSpec((1, LANES), lambda i: (0, 0))
    return pl.pallas_call(
        body, name="ssd_fwd", grid=(nc,),
        in_specs=[pl.BlockSpec((T, SSM_CONV_DIM), lambda i: (i, 0)), pl.BlockSpec((T, LANES), lambda i: (i, P_DT // LANES)),
                  vec, vec],
        out_specs=[pl.BlockSpec((T, SSM_INNER), lambda i: (i, 0)), pl.BlockSpec((1, 512, LANES), lambda i: (i, 0, 0))],
        out_shape=[jax.ShapeDtypeStruct((L, SSM_INNER), F32), jax.ShapeDtypeStruct((nc, 512, LANES), F32)],
        scratch_shapes=[pltpu.VMEM((512, LANES), F32)], compiler_params=_params(("arbitrary",)),
    )(pre, P, dtb, alog)


def _ssd_bwd(pre, P, states, dy, dxs_extra, dtb, alog):
    L = pre.shape[0]
    T = min(SSM_CHUNK, L)
    nc = L // T

    def body(pre_ref, dt_ref, st_ref, dy_ref, dx_ref, dtb_ref, al_ref, dpre_ref, ddt_ref, ddtb_ref, dal_ref, ds_scr):
        @pl.when(pl.program_id(0) == 0)
        def _():
            ds_scr[...] = jnp.zeros_like(ds_scr)
            ddtb_ref[...] = jnp.zeros_like(ddtb_ref)
            dal_ref[...] = jnp.zeros_like(dal_ref)

        _, vjp = jax.vjp(_ssd_chunk, pre_ref[...], _f(dt_ref[...]), st_ref[0], dtb_ref[...], al_ref[...])
        dpre, ddt, ds, ddtb, dal = vjp((dy_ref[...], ds_scr[...]))
        dpre_ref[:, :SSM_INNER] = dpre[:, :SSM_INNER] + dx_ref[...]
        dpre_ref[:, SSM_INNER:] = dpre[:, SSM_INNER:]
        ddt_ref[:, :LANES] = ddt.astype(ddt_ref.dtype)
        ddt_ref[:, LANES:] = jnp.zeros((T, DT_PAD - LANES), ddt_ref.dtype)
        ds_scr[...] = ds
        ddtb_ref[...] += ddtb
        dal_ref[...] += dal

    vec = pl.BlockSpec((1, LANES), lambda i: (0, 0))
    rev = lambda i: (nc - 1 - i, 0)
    return pl.pallas_call(
        body, name="ssd_bwd", grid=(nc,),
        in_specs=[pl.BlockSpec((T, SSM_CONV_DIM), rev), pl.BlockSpec((T, LANES), lambda i: (nc - 1 - i, P_DT // LANES)),
                  pl.BlockSpec((1, 512, LANES), lambda i: (nc - 1 - i, 0, 0)),
                  pl.BlockSpec((T, SSM_INNER), rev), pl.BlockSpec((T, SSM_INNER), rev), vec, vec],
        out_specs=[pl.BlockSpec((T, SSM_CONV_DIM), rev), pl.BlockSpec((T, DT_PAD), rev), vec, vec],
        out_shape=[jax.ShapeDtypeStruct((L, SSM_CONV_DIM), F32), jax.ShapeDtypeStruct((L, DT_PAD), BF16),
                   jax.ShapeDtypeStruct((1, LANES), F32), jax.ShapeDtypeStruct((1, LANES), F32)],
        scratch_shapes=[pltpu.VMEM((512, LANES), F32)], compiler_params=_params(("arbitrary",)),
    )(pre, P, states, dy, dxs_extra, dtb, alog)


def _sb_scores(qm, kb, later, strict, mask):
    z = _dot_nt(qm, kb)
    lk = jnp.minimum(-z, 0.0) - jnp.log(1.0 + jnp.exp(-jnp.abs(z)))
    if mask is not None:
        lk = jnp.where(mask, lk, 0.0)
    log_a = z + lk + jnp.dot(lk.astype(BF16), strict, preferred_element_type=F32) + later
    if mask is not None:
        log_a = jnp.where(mask, log_a, -jnp.inf)
    return z, lk, log_a


def _dot_split(x, m):
    hi = x.astype(BF16)
    lo = (x - hi.astype(F32)).astype(BF16)
    return jnp.dot(hi, m, preferred_element_type=F32) + jnp.dot(lo, m, preferred_element_type=F32)


def _sb_setup(q_ref, i, tq, tk):
    lane = lax.broadcasted_iota(jnp.int32, (1, LANES), 1)
    first = lane < SB_HEAD_DIM
    q = q_ref[...] * (SB_HEAD_DIM ** -0.5)
    qms = (jnp.where(first, q, jnp.zeros_like(q)), jnp.where(first, jnp.zeros_like(q), q))
    j0 = lax.div(i * tq, tk)
    ri = lax.broadcasted_iota(jnp.int32, (tq, tk), 0)
    ci = lax.broadcasted_iota(jnp.int32, (tq, tk), 1)
    diag_mask = (ci + (j0 * tk - i * tq)) < ri
    kr = lax.broadcasted_iota(jnp.int32, (tk, tk), 0)
    kc = lax.broadcasted_iota(jnp.int32, (tk, tk), 1)
    strict = (kr > kc).astype(BF16)
    return first, qms, j0, diag_mask, strict


def _sb_continue(c):
    return jnp.logical_and(c[0] >= 0, jnp.maximum(jnp.max(c[1][0]), jnp.max(c[1][1])) > SB_LOG_CUTOFF)


def _sb_fwd(P):
    L = P.shape[0]
    tq, tk = min(SB_TQ, L), min(SB_TK, L)
    nq = L // tq
    qb = P_B // LANES

    subs = SB_SUBS if L % (SB_SUBS * tq) == 0 else 1

    def body(q_ref, k_ref, v_ref, o_ref, of_ref):
        zero, zacc = jnp.zeros((tq, 1), F32), jnp.zeros((tq, LANES), F32)
        walks = []
        for s in range(subs):
            rows = pl.ds(s * tq, tq)
            first, qms, j0, diag_mask, strict = _sb_setup(q_ref.at[rows, :], pl.program_id(1) * subs + s, tq, tk)

            def tile(h, j, later, acc, mask=None, valid=None, qms=qms, strict=strict):
                off = pl.multiple_of(j * tk, tk)
                gate = later if valid is None else jnp.where(valid, later, -jnp.inf)
                _, lk, log_a = _sb_scores(qms[h], k_ref[pl.ds(off, tk), :], gate, strict, mask)
                acc = acc + jnp.dot(jnp.exp(log_a).astype(BF16), v_ref[pl.ds(off, tk), :], preferred_element_type=F32)
                total = jnp.sum(lk, axis=1, keepdims=True)
                return later + (total if valid is None else jnp.where(valid, total, 0.0)), acc

            state = []
            for h in range(2):
                carry = tile(h, j0, zero, zacc, mask=diag_mask)
                for n in range(1, SB_STRAIGHT):
                    carry = tile(h, jnp.maximum(j0 - n, 0), *carry, valid=j0 >= n)
                state.append(carry)
            walks.append((rows, first, j0, tile, state))

        for rows, first, j0, tile, state in walks:
            def tail(c, tile=tile):
                res = [tile(h, c[0], c[1][h], c[2][h]) for h in range(2)]
                return c[0] - 1, (res[0][0], res[1][0]), (res[0][1], res[1][1])

            _, _, accs = lax.while_loop(
                _sb_continue, tail, (j0 - SB_STRAIGHT, (state[0][0], state[1][0]), (state[0][1], state[1][1])))
            out = jnp.where(first, accs[0], accs[1])
            o_ref[rows, :] = out.astype(o_ref.dtype)
            of_ref[rows, :] = out

    nq = nq // subs
    tile_spec = pl.BlockSpec((subs * tq, LANES), lambda p, i: (i, p))
    return pl.pallas_call(
        body, name="sb_fwd", grid=(2, nq),
        in_specs=[pl.BlockSpec((subs * tq, LANES), lambda p, i: (i, qb + p)),
                  pl.BlockSpec((L, LANES), lambda p, i: (0, qb + 2 + p)),
                  pl.BlockSpec((L, LANES), lambda p, i: (0, qb + 4 + p))],
        out_specs=[tile_spec, tile_spec],
        out_shape=[jax.ShapeDtypeStruct((L, 2 * LANES), BF16), jax.ShapeDtypeStruct((L, 2 * LANES), F32)],
        compiler_params=_params(("arbitrary", "arbitrary")),
    )(P, P, P)


def _sb_bwd(P, dyb, yb32):
    L = P.shape[0]
    tq, tk = min(SB_TQ, L), min(SB_TK, L)
    nq = L // tq
    qb = P_B // LANES

    def body(q_ref, k_ref, v_ref, do_ref, of_ref, dq_ref, dk_ref, dv_ref):
        i = pl.program_id(1)
        first, qms, j0, diag_mask, strict = _sb_setup(q_ref, i, tq, tk)

        @pl.when(i == 0)
        def _():
            dk_ref[...] = jnp.zeros_like(dk_ref)
            dv_ref[...] = jnp.zeros_like(dv_ref)

        do = do_ref[...]
        doms = (jnp.where(first, do, jnp.zeros_like(do)), jnp.where(first, jnp.zeros_like(do), do))
        prod = _f(do) * of_ref[...]
        totals = (jnp.sum(jnp.where(first, prod, 0.0), axis=1, keepdims=True),
                  jnp.sum(jnp.where(first, 0.0, prod), axis=1, keepdims=True))

        def tile(h, j, later, later_g, acc, mask=None, valid=None):
            off = pl.multiple_of(j * tk, tk)
            kb, vb = k_ref[pl.ds(off, tk), :], v_ref[pl.ds(off, tk), :]
            gate = later if valid is None else jnp.where(valid, later, -jnp.inf)
            z, lk, log_a = _sb_scores(qms[h], kb, gate, strict, mask)
            att = jnp.exp(log_a).astype(BF16)
            g = _f(att) * _dot_nt(doms[h], vb)
            before = totals[h] - later_g
            if valid is not None:
                before = jnp.where(valid, before, 0.0)
            dz = g - (before - _dot_split(g, strict)) * jnp.exp(z + lk)
            if mask is not None:
                dz = jnp.where(mask, dz, 0.0)
            dzb = dz.astype(BF16)
            rows = jnp.sum(lk, axis=1, keepdims=True)
            carry = (later + (rows if valid is None else jnp.where(valid, rows, 0.0)),
                     later_g + jnp.sum(g, axis=1, keepdims=True), acc + jnp.dot(dzb, kb, preferred_element_type=F32))
            return carry, _dot_tn(dzb, qms[h]), _dot_tn(att, doms[h])

        def tail(c):
            off = pl.multiple_of(c[0] * tk, tk)
            (c0, dk0, dv0), (c1, dk1, dv1) = [tile(h, c[0], c[1][h], c[2][h], c[3][h]) for h in range(2)]
            dk_ref[pl.ds(off, tk), :] += dk0 + dk1
            dv_ref[pl.ds(off, tk), :] += dv0 + dv1
            return (c[0] - 1,) + tuple(zip(c0, c1))

        zero, zacc = jnp.zeros((tq, 1), F32), jnp.zeros((tq, LANES), F32)
        blocks = [j0] + [jnp.maximum(j0 - n, 0) for n in range(1, SB_STRAIGHT)]
        carries, dks, dvs = [], [], []
        for h in range(2):
            carry, dk, dv = tile(h, j0, zero, zero, zacc, mask=diag_mask)
            dks.append([dk])
            dvs.append([dv])
            for n in range(1, SB_STRAIGHT):
                carry, dk, dv = tile(h, blocks[n], *carry, valid=j0 >= n)
                dks[h].append(dk)
                dvs[h].append(dv)
            carries.append(carry)
        for n, j in enumerate(blocks):
            off = pl.multiple_of(j * tk, tk)
            dk_ref[pl.ds(off, tk), :] += dks[0][n] + dks[1][n]
            dv_ref[pl.ds(off, tk), :] += dvs[0][n] + dvs[1][n]
        accs = lax.while_loop(_sb_continue, tail, (j0 - SB_STRAIGHT,) + tuple(zip(carries[0], carries[1])))[3]
        dq_ref[...] = jnp.where(first, accs[0], accs[1]) * (SB_HEAD_DIM ** -0.5)

    full = pl.BlockSpec((L, LANES), lambda p, i: (0, p))
    tile_spec = pl.BlockSpec((tq, LANES), lambda p, i: (i, p))
    return pl.pallas_call(
        body, name="sb_bwd", grid=(2, nq),
        in_specs=[pl.BlockSpec((tq, LANES), lambda p, i: (i, qb + p)),
                  pl.BlockSpec((L, LANES), lambda p, i: (0, qb + 2 + p)),
                  pl.BlockSpec((L, LANES), lambda p, i: (0, qb + 4 + p)), tile_spec, tile_spec],
        out_specs=[tile_spec, full, full],
        out_shape=[jax.ShapeDtypeStruct((L, 2 * LANES), F32)] * 3,
        compiler_params=_params(("arbitrary", "arbitrary")),
    )(P, P, P, dyb, yb32)


MOD_SHARD = N_MOD * D_MODEL // N_CHIPS


def _mod_fwd(c_all, mod_w, mod_b_sh):
    tn = 512

    def body(c_ref, w_ref, b_ref, o_ref):
        o_ref[0] = jnp.dot(_silu(c_ref[...]), w_ref[0], precision=HIGHEST, preferred_element_type=F32) + b_ref[0]

    return pl.pallas_call(
        body, name="mod_fwd", grid=(DEPTH, MOD_SHARD // tn),
        in_specs=[pl.BlockSpec((N_DEV, D_MODEL), lambda l, j: (0, 0)),
                  pl.BlockSpec((1, D_MODEL, tn), lambda l, j: (l, 0, j)),
                  pl.BlockSpec((1, 1, tn), lambda l, j: (l, 0, j))],
        out_specs=pl.BlockSpec((1, N_DEV, tn), lambda l, j: (l, 0, j)),
        out_shape=jax.ShapeDtypeStruct((DEPTH, N_DEV, MOD_SHARD), F32),
        compiler_params=_params(("arbitrary", "arbitrary")),
    )(c_all, mod_w, mod_b_sh)


def _mod_bwd(c_all, dmod_sh):
    tn = 512

    def body(c_ref, d_ref, o_ref):
        o_ref[0] = lax.dot_general(_silu(c_ref[...]), d_ref[0], (((0,), (0,)), ((), ())), precision=HIGHEST,
                                   preferred_element_type=F32)

    return pl.pallas_call(
        body, name="mod_bwd", grid=(DEPTH, MOD_SHARD // tn),
        in_specs=[pl.BlockSpec((N_DEV, D_MODEL), lambda l, j: (0, 0)),
                  pl.BlockSpec((1, N_DEV, tn), lambda l, j: (l, 0, j))],
        out_specs=pl.BlockSpec((1, D_MODEL, tn), lambda l, j: (l, 0, j)),
        out_shape=jax.ShapeDtypeStruct((DEPTH, D_MODEL, MOD_SHARD), F32),
        compiler_params=_params(("arbitrary", "arbitrary")),
    )(c_all, dmod_sh)


def _row_tile(rows, cap):
    if rows <= cap:
        return rows
    best = None
    for t in range(8, cap + 1, 8):
        if rows % t == 0:
            best = t
    assert best is not None, (rows, cap)
    return best


def _adamw(name, w, gs, m, v, tr=256):
    R, W = w.shape
    by_layer = any(isinstance(t, tuple) for t in gs)
    tr = _row_tile(R // 2 if by_layer else R, tr)
    per = (R // 2) // tr

    flat, specs = [], []
    for t in gs:
        if isinstance(t, tuple):
            flat += list(t)
            specs += [pl.BlockSpec((tr, W), lambda i: (jnp.minimum(i, per - 1), 0)),
                      pl.BlockSpec((tr, W), lambda i: (jnp.maximum(i - per, 0), 0))]
        else:
            flat.append(t)
            specs.append(pl.BlockSpec((tr, W), lambda i: (i, 0)))
    ng = len(flat)

    def body(*refs):
        w_ref, g_refs, (m_ref, v_ref) = refs[0], list(refs[1:1 + ng]), refs[1 + ng:3 + ng]
        g_out, d_out, m_out, v_out = refs[3 + ng:]
        g = None
        for t in gs:
            if isinstance(t, tuple):
                lo, hi = g_refs.pop(0), g_refs.pop(0)
                term = jnp.where(pl.program_id(0) < per, lo[...], hi[...])
            else:
                term = g_refs.pop(0)[...]
            g = term if g is None else g + term
        mm = ADAM_B1 * m_ref[...] + (1.0 - ADAM_B1) * g
        vv = ADAM_B2 * v_ref[...] + (1.0 - ADAM_B2) * (g * g)
        m_hat = mm / (1.0 - ADAM_B1 ** ADAM_STEP)
        v_hat = vv / (1.0 - ADAM_B2 ** ADAM_STEP)
        g_out[...] = g
        d_out[...] = -ADAM_LR * (m_hat / (jnp.sqrt(v_hat) + ADAM_EPS) + ADAM_WD * w_ref[...])
        m_out[...] = mm
        v_out[...] = vv

    spec = pl.BlockSpec((tr, W), lambda i: (i, 0))
    return pl.pallas_call(
        body, name=name, grid=(R // tr,), in_specs=[spec] + specs + [spec, spec], out_specs=[spec] * 4,
        out_shape=[jax.ShapeDtypeStruct((R, W), F32)] * 4, compiler_params=_params(("arbitrary",)),
    )(w, *flat, m, v)


def _sum_slots(name, a, tr=256):
    n, R, W = a.shape
    tr = _row_tile(R, tr)

    def body(a_ref, o_ref):
        acc = _f(a_ref[0])
        for j in range(1, n):
            acc = acc + _f(a_ref[j])
        o_ref[...] = acc

    return pl.pallas_call(
        body, name=name, grid=(R // tr,), in_specs=[pl.BlockSpec((n, tr, W), lambda i: (0, i, 0))],
        out_specs=pl.BlockSpec((tr, W), lambda i: (i, 0)), out_shape=jax.ShapeDtypeStruct((R, W), F32),
        compiler_params=_params(("arbitrary",)),
    )(a)


def _here():
    return lax.axis_index("x"), lax.axis_index("y"), lax.axis_index("c")


def _flip(v, d):
    return 1 - v if d else v


def _allgather_small(name, buf):
    R = buf.shape[0]
    rel = [(dx, dy, dc) for dx in (0, 1) for dy in (0, 1) for dc in (0, 1)][1:]

    def body(x_ref, o_ref, send, recv, lsem):
        x, y, c = _here()
        me = 4 * x + 2 * y + c
        mine = pltpu.make_async_copy(x_ref, o_ref.at[me], lsem)
        mine.start()

        def copy(k, slot):
            dx, dy, dc = rel[k]
            return pltpu.make_async_remote_copy(
                src_ref=x_ref, dst_ref=o_ref.at[slot], send_sem=send.at[k], recv_sem=recv.at[k],
                device_id=(_flip(x, dx), _flip(y, dy), _flip(c, dc)), device_id_type=MESH_ID)

        sent = [copy(k, me) for k in range(len(rel))]
        for cp in sent:
            cp.start()
        for k, (dx, dy, dc) in enumerate(rel):
            copy(k, 4 * _flip(x, dx) + 2 * _flip(y, dy) + _flip(c, dc)).wait_recv()
        for cp in sent:
            cp.wait_send()
        mine.wait()

    return pl.pallas_call(
        body, name=name, out_shape=jax.ShapeDtypeStruct((N_DEV, R, LANES), F32),
        in_specs=[pl.BlockSpec(memory_space=pltpu.VMEM)], out_specs=pl.BlockSpec(memory_space=pltpu.VMEM),
        scratch_shapes=[pltpu.SemaphoreType.DMA((7,)), pltpu.SemaphoreType.DMA((7,)), pltpu.SemaphoreType.DMA],
    )(buf)


CHIP_REL = [(1, 0), (0, 1), (1, 1)]


class _Side:
    def __init__(self, arrays, out_shapes, scratch, start, finish):
        self.arrays, self.out_shapes, self.scratch, self.start, self.finish = arrays, out_shapes, scratch, start, finish


def _chip_of(k):
    x, y, _ = _here()
    dx, dy = CHIP_REL[k]
    return _flip(x, dx), _flip(y, dy)


def _scatter_side(arrays):
    n = len(arrays)

    def parts(ins, outs, sems):
        send, recv, lsem = sems
        x, y, c = _here()
        s = 2 * x + y

        def copy(w, k, mine):
            px, py = _chip_of(k)
            return pltpu.make_async_remote_copy(
                src_ref=ins[w].at[2 * px + py], dst_ref=outs[w].at[s if mine else 2 * px + py],
                send_sem=send.at[3 * w + k], recv_sem=recv.at[3 * w + k], device_id=(px, py, c), device_id_type=MESH_ID)

        local = [pltpu.make_async_copy(ins[w].at[s], outs[w].at[s], lsem.at[w]) for w in range(n)]
        return copy, local

    def start(ins, outs, sems):
        copy, local = parts(ins, outs, sems)
        for cp in local:
            cp.start()
        for w in range(n):
            for k in range(3):
                copy(w, k, True).start()

    def finish(ins, outs, sems):
        copy, local = parts(ins, outs, sems)
        for w in range(n):
            for k in range(3):
                copy(w, k, False).wait_recv()
        for w in range(n):
            for k in range(3):
                copy(w, k, True).wait_send()
        for cp in local:
            cp.wait()

    scratch = [pltpu.SemaphoreType.DMA((3 * n,)), pltpu.SemaphoreType.DMA((3 * n,)), pltpu.SemaphoreType.DMA((n,))]
    return _Side(arrays, [jax.ShapeDtypeStruct(a.shape, a.dtype) for a in arrays], scratch, start, finish)


def _gather_side(shards):
    n = len(shards)

    def parts(ins, outs, sems):
        send, recv, fsend, frecv, lsem = sems
        x, y, c = _here()
        s = 2 * x + y

        def half(ref, w, which):
            rows = shards[w].shape[0] // 2
            return ref.at[pl.ds(pl.multiple_of(which * rows, 16), rows)]

        def over_ici(w, k, mine):
            px, py = _chip_of(k)
            return pltpu.make_async_remote_copy(
                src_ref=half(ins[w], w, c), dst_ref=half(outs[w].at[s if mine else 2 * px + py], w, c),
                send_sem=send.at[3 * w + k], recv_sem=recv.at[3 * w + k], device_id=(px, py, c), device_id_type=MESH_ID)

        def to_sibling(w, k, which):
            px, py = _chip_of(k)
            part = half(outs[w].at[2 * px + py], w, which)
            return pltpu.make_async_remote_copy(
                src_ref=part, dst_ref=part, send_sem=fsend.at[3 * w + k], recv_sem=frecv.at[3 * w + k],
                device_id=(x, y, 1 - c), device_id_type=MESH_ID)

        local = [pltpu.make_async_copy(ins[w], outs[w].at[s], lsem.at[w]) for w in range(n)]
        return c, over_ici, to_sibling, local

    def start(ins, outs, sems):
        _, over_ici, _, local = parts(ins, outs, sems)
        for cp in local:
            cp.start()
        for w in range(n):
            for k in range(3):
                over_ici(w, k, True).start()

    def finish(ins, outs, sems):
        c, over_ici, to_sibling, local = parts(ins, outs, sems)
        for w in range(n):
            for k in range(3):
                over_ici(w, k, False).wait_recv()
                to_sibling(w, k, c).start()
        for w in range(n):
            for k in range(3):
                to_sibling(w, k, 1 - c).wait_recv()
        for w in range(n):
            for k in range(3):
                over_ici(w, k, True).wait_send()
                to_sibling(w, k, c).wait_send()
        for cp in local:
            cp.wait()

    scratch = [pltpu.SemaphoreType.DMA((3 * n,))] * 4 + [pltpu.SemaphoreType.DMA((n,))]
    return _Side(shards, [jax.ShapeDtypeStruct((N_CHIPS,) + a.shape, a.dtype) for a in shards], scratch, start, finish)


def _sibling_exchange(name, arrays):
    n = len(arrays)

    def body(*refs):
        ins, outs = refs[:n], refs[n:2 * n]
        send, recv = refs[2 * n:]
        x, y, c = _here()
        cps = [pltpu.make_async_remote_copy(src_ref=ins[w], dst_ref=outs[w], send_sem=send.at[w], recv_sem=recv.at[w],
                                            device_id=(x, y, 1 - c), device_id_type=MESH_ID) for w in range(n)]
        for cp in cps:
            cp.start()
        for cp in cps:
            cp.wait()

    any_spec = pl.BlockSpec(memory_space=pl.ANY)
    return pl.pallas_call(
        body, name=name, out_shape=[jax.ShapeDtypeStruct(a.shape, a.dtype) for a in arrays],
        in_specs=[any_spec] * n, out_specs=[any_spec] * n,
        scratch_shapes=[pltpu.SemaphoreType.DMA((n,)), pltpu.SemaphoreType.DMA((n,))],
    )(*arrays)


def _pack(arrs):
    flat = jnp.concatenate([a.reshape(-1).astype(F32) for a in arrs])
    n = flat.shape[0]
    rows = -(-n // (8 * LANES)) * 8
    return jnp.pad(flat, (0, rows * LANES - n)).reshape(rows, LANES)


def _unpack(buf, shapes):
    lead = buf.shape[:-2]
    flat = buf.reshape(lead + (-1,))
    out, off = [], 0
    for s in shapes:
        n = 1
        for d in s:
            n *= d
        out.append(flat[..., off:off + n].reshape(lead + tuple(s)))
        off += n
    return out


def _pad_w_in(w):
    return jnp.concatenate([w[:, :2048], w[:, 2816:2824], jnp.zeros((w.shape[0], P_XBC - P_DT - 8), w.dtype),
                            w[:, 2048:2816], w[:, 2824:]], axis=1)


def _unpad_w_in(g):
    return jnp.concatenate([g[:, :P_DT], g[:, P_XBC:P_G], g[:, P_DT:P_DT + 8], g[:, P_G:]], axis=1)


FFN_HALF = FFN_HIDDEN // 2


def _ffn_in_cols(w):
    h = FFN_HALF
    return jnp.concatenate([w[:, :h], w[:, 2 * h:3 * h], w[:, h:2 * h], w[:, 3 * h:]], axis=1)


def _row(v):
    return v.reshape(1, -1)


BIG = (("w_in", 2), ("w_sc_out", 2), ("w_sb_out", 2), ("w_ssm_out", 2), ("w_o", 1), ("w_ffn_in", 2), ("w_ffn_out", 1))
SMALL = ("mod_b", "g_pre_mix", "g_post_mix", "g_pre_ffn", "g_post_ffn", "sc_conv_w", "ssm_conv_w", "ssm_conv_b",
         "ssm_dt_bias", "ssm_a_log", "ssm_d", "ssm_norm_w")
WEIGHT_ORDER = ("mod_w", "mod_b", "g_pre_mix", "g_post_mix", "g_pre_ffn", "g_post_ffn", "w_in", "sc_conv_w",
                "ssm_conv_w", "ssm_conv_b", "ssm_dt_bias", "ssm_a_log", "ssm_d", "ssm_norm_w", "w_sc_out", "w_sb_out",
                "w_ssm_out", "w_o", "w_ffn_in", "w_ffn_out")


def _mm_mid(name, a, w, x, vecs):
    return _mm_epi(name, a, w, "nn", D_MODEL, [(x, D_MODEL)] + [(v, None) for v in vecs],
                   lambda p, x, *v: (p,) + tuple(_mid(x, p, *v)), [(D_MODEL, BF16), (D_MODEL, F32), (D_MODEL, BF16)])


def _layer_fwd(l, x_in, h, W, V, sides, next_vecs):
    S = {"x_in": x_in, "h": h}
    side, handler = sides.get("in_proj", (None, None))
    P = _mm(f"in_proj{l}", h, W["w_in"], "nn", BF16, tm=2048, tn_cap=1024, side=side)
    if side:
        handler(P[1:])
        P = P[0]
    S["P"] = P
    S["ya"] = _shortconv_fwd(P, V["sc_w"])
    S["yb"], S["yb32"] = _sb_fwd(P)
    S["pre"] = _ssmconv_fwd(P, V["ssm_w"], V["ssm_b"])
    S["y_ssd"], S["states"] = _ssd_fwd(S["pre"], P, V["dtb"], V["alog"])
    S["yc"] = _rowwise(f"ssm_post{l}", lambda y, px, z, d, nw: _ssm_post(y, px, _f(z), d, nw),
                       [(S["y_ssd"], SSM_INNER, 0), (S["pre"], SSM_INNER, 0), (P, SSM_INNER, P_Z // SSM_INNER)],
                       [V["d_full"], V["norm_w"]], [(SSM_INNER, BF16)])[0]
    S["merged"] = _merge_fwd(f"merge{l}", P, [S["ya"], S["yb"], S["yc"]],
                             [W["w_sc_out"], W["w_sb_out"], W["w_ssm_out"]])
    S["mix"], S["x1"], S["h2"] = _mm_mid(f"w_o{l}", S["merged"], W["w_o"], x_in, V["mid_mix"])
    side, handler = sides.get("ffn_in", (None, None))
    res = _mm_epi(f"ffn_in{l}", S["h2"], W["w_ffn_in"], "nn", 2 * FFN_HALF, [],
                  lambda p: (p, _swiglu(p[:, :FFN_HALF], p[:, FFN_HALF:])),
                  [(2 * FFN_HALF, BF16), (FFN_HALF, BF16)], side=side)
    S["GU"], S["act"] = res[0], res[1]
    if side:
        handler(res[2:])
    if next_vecs is None:
        S["f"] = _mm(f"ffn_out{l}", S["act"], W["w_ffn_out"], "nn", BF16)
    else:
        S["f"], S["x_next"], S["h_next"] = _mm_mid(f"ffn_out{l}", S["act"], W["w_ffn_out"], S["x1"], next_vecs)
    return S


BRANCH_WIDTHS = (SC_WIDTH, 256, SSM_INNER)


def _branch_specs(tm):
    gb = P_G // D_MODEL
    gates = [pl.BlockSpec((tm, D_MODEL), functools.partial(lambda i, cb: (i, cb), cb=gb + k)) for k in range(3)]
    ys = [pl.BlockSpec((tm, w), lambda i: (i, 0)) for w in BRANCH_WIDTHS]
    ws = [pl.BlockSpec((w, D_MODEL), lambda i: (0, 0)) for w in BRANCH_WIDTHS]
    return gates, ys, ws


def _merge_fwd(name, P, ys, ws, tm=512):
    L = P.shape[0]
    tm = min(tm, L)
    gates, y_specs, w_specs = _branch_specs(tm)

    def body(ga, gb, gc, ya, yb, yc, wa, wb, wc, o_ref):
        acc = None
        for g_ref, y_ref, w_ref in ((ga, ya, wa), (gb, yb, wb), (gc, yc, wc)):
            t = jax.nn.sigmoid(_f(g_ref[...])) * jnp.dot(y_ref[...], w_ref[...], preferred_element_type=F32)
            acc = t if acc is None else acc + t
        o_ref[...] = acc.astype(o_ref.dtype)

    return pl.pallas_call(
        body, name=name, grid=(L // tm,), in_specs=gates + y_specs + w_specs,
        out_specs=pl.BlockSpec((tm, D_MODEL), lambda i: (i, 0)), out_shape=jax.ShapeDtypeStruct((L, D_MODEL), BF16),
        compiler_params=_params(("arbitrary",)),
    )(P, P, P, *ys, *ws)


def _merge_bwd(name, P, ys, ws, dmerged, tm=512):
    L = P.shape[0]
    tm = min(tm, L)
    gates, y_specs, w_specs = _branch_specs(tm)

    def body(ga, gb, gc, ya, yb, yc, wa, wb, wc, dm_ref, dg_ref, dya, dyb, dyc, gwa, gwb, gwc):
        @pl.when(pl.program_id(0) == 0)
        def _():
            for r in (gwa, gwb, gwc):
                r[...] = jnp.zeros_like(r)

        dm = _f(dm_ref[...])
        for k, (g_ref, y_ref, w_ref, dy_ref, gw_ref) in enumerate(
                ((ga, ya, wa, dya, gwa), (gb, yb, wb, dyb, gwb), (gc, yc, wc, dyc, gwc))):
            y, w = y_ref[...], w_ref[...]
            s = jax.nn.sigmoid(_f(g_ref[...]))
            proj = jnp.dot(y, w, preferred_element_type=F32)
            d_proj = (dm * s).astype(BF16)
            dg_ref[:, k * D_MODEL:(k + 1) * D_MODEL] = (dm * proj * s * (1.0 - s)).astype(dg_ref.dtype)
            dy_ref[...] = _dot_nt(d_proj, w).astype(dy_ref.dtype)
            gw_ref[...] += _dot_tn(y, d_proj)

    gate_cols = pl.BlockSpec((tm, P_WIDTH - P_G), lambda i: (i, P_G // (P_WIDTH - P_G)))
    return pl.pallas_call(
        body, name=name, grid=(L // tm,),
        in_specs=gates + y_specs + w_specs + [pl.BlockSpec((tm, D_MODEL), lambda i: (i, 0))],
        out_specs=[gate_cols] + y_specs + w_specs,
        out_shape=[jax.ShapeDtypeStruct((L, P_WIDTH), BF16)] + [jax.ShapeDtypeStruct((L, w), BF16) for w in BRANCH_WIDTHS]
        + [jax.ShapeDtypeStruct((w, D_MODEL), F32) for w in BRANCH_WIDTHS],
        compiler_params=_params(("arbitrary",)),
    )(P, P, P, *ys, *ws, dmerged)


def _assemble_dp(name, dP, parts, tl=256):
    L = dP.shape[0]
    tl = min(tl, L)
    n = len(parts)

    def body(*refs):
        o_ref = refs[n + 1]
        o_ref[...] = jnp.concatenate([r[...].astype(o_ref.dtype) for r in refs[:n]], axis=1)

    return pl.pallas_call(
        body, name=name, grid=(L // tl,),
        in_specs=[pl.BlockSpec((tl, a.shape[1]), lambda i: (i, 0)) for a in parts] + [pl.BlockSpec(memory_space=pl.ANY)],
        out_specs=pl.BlockSpec((tl, P_G), lambda i: (i, 0)), out_shape=jax.ShapeDtypeStruct(dP.shape, dP.dtype),
        input_output_aliases={n: 0}, compiler_params=_params(("arbitrary",)),
    )(*parts, dP)


def _layer_bwd(l, S, W, V, dx1, df, sides, landed):
    G = {}
    P = S["P"]

    def mm(key, *args, **kw):
        if key not in sides:
            return _mm(f"{key}{l}", *args, **kw)
        names, layer, make = sides[key]
        res = _mm(f"{key}{l}", *args, side=make(G), **kw)
        for n, a in zip(names, res[1:]):
            landed[(n, layer)] = a
        return res[0]

    G["w_ffn_out"] = _mm(f"gw_ffn_out{l}", S["act"], df, "tn", F32)

    def swiglu_bwd(d_act, gu):
        _, vjp = jax.vjp(_swiglu, _f(gu[:, :FFN_HALF]), _f(gu[:, FFN_HALF:]))
        return (jnp.concatenate(vjp(d_act), axis=1),)

    names, layer, make = sides.get("d_gu", ((), None, None))
    res = _mm_epi(f"d_gu{l}", df, W["w_ffn_out"], "nt", FFN_HALF, [(S["GU"], 2 * FFN_HALF)], swiglu_bwd,
                  [(2 * FFN_HALF, BF16)], side=make(G) if make else None)
    dGU = res[0]
    for n, a in zip(names, res[1:]):
        landed[(n, layer)] = a
    dh2 = mm("d_h2", dGU, W["w_ffn_in"], "nt", BF16)
    G["w_ffn_in"] = _ffn_in_cols(mm("gw_ffn_in", S["h2"], dGU, "tn", F32))
    dx, dmix, G["gate1"], G["g_post_mix"], G["g_pre_ffn"], G["scale2"], G["shift2"] = _mid_bwd(
        f"mid_mix_bwd{l}", S["x_in"], S["mix"], dx1, dh2, V["mid_mix"])
    dmerged = _mm(f"d_merged{l}", dmix, W["w_o"], "nt", BF16)
    G["w_o"] = _mm(f"gw_o{l}", S["merged"], dmix, "tn", F32)

    dP, dya, dyb, dyc, G["w_sc_out"], G["w_sb_out"], G["w_ssm_out"] = _merge_bwd(
        f"merge_bwd{l}", P, [S["ya"], S["yb"], S["yc"]], [W["w_sc_out"], W["w_sb_out"], W["w_ssm_out"]], dmerged)

    def post_bwd(y, px, z, d, dfull, nw):
        _, vjp = jax.vjp(_ssm_post, y, px, _f(z), dfull, nw)
        return vjp(_f(d))

    dy_ssd, dxs, dz, G["d_full"], G["ssm_norm_w"] = _rowwise(
        f"ssm_post_bwd{l}", post_bwd,
        [(S["y_ssd"], SSM_INNER, 0), (S["pre"], SSM_INNER, 0), (P, SSM_INNER, P_Z // SSM_INNER), (dyc, SSM_INNER, 0)],
        [V["d_full"], V["norm_w"]], [(SSM_INNER, F32), (SSM_INNER, F32), (SSM_INNER, BF16)], [(1, SSM_INNER)] * 2)
    dpre, ddt, G["dtb"], G["alog"] = _ssd_bwd(S["pre"], P, S["states"], dy_ssd, dxs, V["dtb"], V["alog"])
    dxbc, w0, w1, w2, w3, G["ssm_conv_b"] = _ssmconv_bwd(P, dpre, V["ssm_w"])
    G["ssm_conv_w"] = jnp.concatenate([w0, w1, w2, w3], axis=0)
    dq, dk, dv = _sb_bwd(P, dyb, S["yb32"])
    dA, s0, s1, s2 = _shortconv_bwd(P, dya, V["sc_w"])
    G["sc_conv_w"] = jnp.concatenate([s0, s1, s2], axis=0)
    dP = _assemble_dp(f"assemble_dp{l}", dP, [dA, dq, dk, dv, dz, ddt, dxbc])
    G["w_in"] = _mm(f"gw_in{l}", S["h"], dP, "tn", F32, tn_cap=1024)
    dh = mm("d_h", dP, W["w_in"], "nt", BF16, tk_cap=3072)
    return dx, dh, G


def kernel(x, c, mod_w, mod_b, g_pre_mix, g_post_mix, g_pre_ffn, g_post_ffn, w_in, sc_conv_w, ssm_conv_w, ssm_conv_b, ssm_dt_bias, ssm_a_log, ssm_d, ssm_norm_w, w_sc_out, w_sb_out, w_ssm_out, w_o, w_ffn_in, w_ffn_out, loss_target, m_mod_w, m_mod_b, m_g_pre_mix, m_g_post_mix, m_g_pre_ffn, m_g_post_ffn, m_w_in, m_sc_conv_w, m_ssm_conv_w, m_ssm_conv_b, m_ssm_dt_bias, m_ssm_a_log, m_ssm_d, m_ssm_norm_w, m_w_sc_out, m_w_sb_out, m_w_ssm_out, m_w_o, m_w_ffn_in, m_w_ffn_out, v_mod_w, v_mod_b, v_g_pre_mix, v_g_post_mix, v_g_pre_ffn, v_g_post_ffn, v_w_in, v_sc_conv_w, v_ssm_conv_w, v_ssm_conv_b, v_ssm_dt_bias, v_ssm_a_log, v_ssm_d, v_ssm_norm_w, v_w_sc_out, v_w_sb_out, v_w_ssm_out, v_w_o, v_w_ffn_in, v_w_ffn_out):
    wts = dict(mod_w=mod_w, mod_b=mod_b, g_pre_mix=g_pre_mix, g_post_mix=g_post_mix, g_pre_ffn=g_pre_ffn,
               g_post_ffn=g_post_ffn, w_in=w_in, sc_conv_w=sc_conv_w, ssm_conv_w=ssm_conv_w, ssm_conv_b=ssm_conv_b,
               ssm_dt_bias=ssm_dt_bias, ssm_a_log=ssm_a_log, ssm_d=ssm_d, ssm_norm_w=ssm_norm_w, w_sc_out=w_sc_out,
               w_sb_out=w_sb_out, w_ssm_out=w_ssm_out, w_o=w_o, w_ffn_in=w_ffn_in, w_ffn_out=w_ffn_out)
    mom = dict(mod_w=m_mod_w, mod_b=m_mod_b, g_pre_mix=m_g_pre_mix, g_post_mix=m_g_post_mix, g_pre_ffn=m_g_pre_ffn,
               g_post_ffn=m_g_post_ffn, w_in=m_w_in, sc_conv_w=m_sc_conv_w, ssm_conv_w=m_ssm_conv_w,
               ssm_conv_b=m_ssm_conv_b, ssm_dt_bias=m_ssm_dt_bias, ssm_a_log=m_ssm_a_log, ssm_d=m_ssm_d,
               ssm_norm_w=m_ssm_norm_w, w_sc_out=m_w_sc_out, w_sb_out=m_w_sb_out, w_ssm_out=m_w_ssm_out, w_o=m_w_o,
               w_ffn_in=m_w_ffn_in, w_ffn_out=m_w_ffn_out)
    var = dict(mod_w=v_mod_w, mod_b=v_mod_b, g_pre_mix=v_g_pre_mix, g_post_mix=v_g_post_mix, g_pre_ffn=v_g_pre_ffn,
               g_post_ffn=v_g_post_ffn, w_in=v_w_in, sc_conv_w=v_sc_conv_w, ssm_conv_w=v_ssm_conv_w,
               ssm_conv_b=v_ssm_conv_b, ssm_dt_bias=v_ssm_dt_bias, ssm_a_log=v_ssm_a_log, ssm_d=v_ssm_d,
               ssm_norm_w=v_ssm_norm_w, w_sc_out=v_w_sc_out, w_sb_out=v_w_sb_out, w_ssm_out=v_w_ssm_out, w_o=v_w_o,
               w_ffn_in=v_w_ffn_in, w_ffn_out=v_w_ffn_out)
    xi, yi, ci = _here()
    chip = 2 * xi + yi
    me = 4 * xi + 2 * yi + ci
    x0, target = x[0], loss_target[0]

    first_shapes = [(D_MODEL,), sc_conv_w.shape, ssm_conv_w.shape]
    g0 = _allgather_small("gather_cond", _pack([c, sc_conv_w, ssm_conv_w]))
    c_rows, sc_sh, ssm_sh = _unpack(g0, first_shapes)
    c_all = c_rows
    sc_w = jnp.concatenate([sc_sh[2 * j] for j in range(N_CHIPS)], axis=-1)
    ssm_w = jnp.concatenate([ssm_sh[2 * j] for j in range(N_CHIPS)], axis=-1)

    mod_b_sh = lax.dynamic_slice_in_dim(mod_b, chip * MOD_SHARD, MOD_SHARD, axis=1).reshape(DEPTH, 1, MOD_SHARD)
    modpart = _mod_fwd(c_all, mod_w, mod_b_sh)
    g1 = _allgather_small("gather_mod", modpart.reshape(-1, LANES)).reshape(N_DEV, DEPTH, N_DEV, MOD_SHARD)
    mod = jnp.concatenate([lax.dynamic_index_in_dim(g1[2 * j], me, axis=1, keepdims=False) for j in range(N_CHIPS)],
                          axis=-1)

    def layer_shards(l):
        return [wts[n][l].astype(BF16) for n, _ in BIG]

    def full_weights(which, gathered):
        W = {n: jnp.concatenate([g[j] for j in range(N_CHIPS)], axis=ax - 1) for (n, ax), g in zip(which, gathered)}
        if "w_in" in W:
            W["w_in"] = _pad_w_in(W["w_in"])
        if "w_ffn_in" in W:
            W["w_ffn_in"] = _ffn_in_cols(W["w_ffn_in"])
        return W

    Ws = [{}, {}]
    fwd_sides = [{"in_proj": (_gather_side(layer_shards(0)[1:]), lambda got: Ws[0].update(full_weights(BIG[1:], got))),
                  "ffn_in": (_gather_side(layer_shards(1)), lambda got: Ws[1].update(full_weights(BIG, got)))}, {}]
    Vs = []
    for l in range(DEPTH):
        sh1, sc1, gt1, sh2, sc2, gt2 = [_row(v) for v in jnp.split(mod[l], N_MOD)]
        Vs.append(dict(
            shift1=sh1, scale1=sc1, g_pre_mix=_row(g_pre_mix[l]),
            mid_mix=[gt1, _row(g_post_mix[l]), _row(g_pre_ffn[l]), sc2, sh2],
            gate2=gt2, g_post_ffn=_row(g_post_ffn[l]),
            sc_w=sc_w[l], ssm_w=ssm_w[l], ssm_b=_row(ssm_conv_b[l]),
            dtb=_row(jnp.pad(ssm_dt_bias[l], (0, LANES - SSM_HEADS))), alog=_row(jnp.pad(ssm_a_log[l], (0, LANES - SSM_HEADS))),
            d_full=_row(jnp.repeat(ssm_d[l], SSM_INNER // SSM_HEADS)), norm_w=_row(ssm_norm_w[l])))

    def mid_ffn_vecs(l):
        return [Vs[l]["gate2"], Vs[l]["g_post_ffn"], Vs[l + 1]["g_pre_mix"], Vs[l + 1]["scale1"], Vs[l + 1]["shift1"]]

    saved = []
    x_in = x0
    h, *got = _first_fwd(x0, [Vs[0]["g_pre_mix"], Vs[0]["scale1"], Vs[0]["shift1"]], _gather_side(layer_shards(0)[:1]))
    Ws[0].update(full_weights(BIG[:1], got))
    for l in range(DEPTH):
        S = _layer_fwd(l, x_in, h, Ws[l], Vs[l], fwd_sides[l], mid_ffn_vecs(l) if l + 1 < DEPTH else None)
        saved.append(S)
        if l + 1 < DEPTH:
            x_in, h = S["x_next"], S["h_next"]

    def pieces(G, names):
        out = []
        for n, ax in BIG:
            if n in names:
                g = _unpad_w_in(G[n]) if n == "w_in" else G[n]
                out.append(jnp.stack(jnp.split(g, N_CHIPS, axis=ax - 1)).astype(BF16))
        return out

    small_names = tuple(n for n, _ in BIG if n not in ("w_in", "w_ffn_in"))
    late_names = tuple(n for n, _ in BIG if n != "w_in")
    landed = {}

    GL = [None] * DEPTH
    S = saved[-1]
    dx1, df, g_gate2, g_gpf, loss_cols = _last_bwd(S["x1"], S["f"], target, [Vs[-1]["gate2"], Vs[-1]["g_post_ffn"]])
    for l in reversed(range(DEPTH)):
        sides = {}
        if l + 1 < DEPTH:
            for key, names in (("d_gu", small_names), ("d_h2", ("w_ffn_in",)), ("gw_ffn_in", ("w_in",))):
                sides[key] = (names, l + 1, lambda G, up=GL[l + 1], names=names: _scatter_side(pieces(up, names)))
        if l == 0:
            sides["d_h"] = (late_names, l, lambda G: _scatter_side(pieces(G, late_names)))
        dx, dh, G = _layer_bwd(l, saved[l], Ws[l], Vs[l], dx1, df, sides, landed)
        G["gate2"], G["g_post_ffn"] = g_gate2, g_gpf
        GL[l] = G
        if l > 0:
            Sp = saved[l - 1]
            dx1, df, g_gate2, g_gpf, G["g_pre_mix"], G["scale1"], G["shift1"] = _mid_bwd(
                f"mid_ffn_bwd{l - 1}", Sp["x1"], Sp["f"], dx, dh, mid_ffn_vecs(l - 1))
        else:
            grad_x, G["g_pre_mix"], G["scale1"], G["shift1"], landed[("w_in", 0)] = _first_bwd(
                x0, dx, dh, [Vs[0]["g_pre_mix"], Vs[0]["scale1"], Vs[0]["shift1"]], _scatter_side(pieces(G, ("w_in",))))
    loss = lax.psum(jnp.sum(loss_cols), ("x", "y", "c"))

    def both(key, shape=None):
        a = jnp.stack([GL[l][key] for l in range(DEPTH)])
        return a if shape is None else a.reshape(shape)

    dmod = jnp.concatenate([both(k, (DEPTH, D_MODEL)) for k in ("shift1", "scale1", "gate1", "shift2", "scale2", "gate2")],
                           axis=1)
    part_small = dict(
        mod_b=dmod, g_pre_mix=both("g_pre_mix", (DEPTH, D_MODEL)), g_post_mix=both("g_post_mix", (DEPTH, D_MODEL)),
        g_pre_ffn=both("g_pre_ffn", (DEPTH, D_MODEL)), g_post_ffn=both("g_post_ffn", (DEPTH, D_MODEL)),
        sc_conv_w=both("sc_conv_w"), ssm_conv_w=both("ssm_conv_w"), ssm_conv_b=both("ssm_conv_b", (DEPTH, SSM_CONV_DIM)),
        ssm_dt_bias=both("dtb", (DEPTH, LANES))[:, :SSM_HEADS], ssm_a_log=both("alog", (DEPTH, LANES))[:, :SSM_HEADS],
        ssm_d=both("d_full", (DEPTH, SSM_HEADS, SSM_INNER // SSM_HEADS)).sum(-1),
        ssm_norm_w=both("ssm_norm_w", (DEPTH, SSM_INNER)))
    small_shapes = [part_small[n].shape for n in SMALL]
    g2 = _allgather_small("gather_small_grads", _pack([part_small[n] for n in SMALL]))
    tot = dict(zip(SMALL, _unpack(_sum_slots("sum_small_grads", g2), small_shapes)))
    dmod_all = _unpack(g2, small_shapes)[0]
    dmod_sh = jnp.swapaxes(lax.dynamic_slice_in_dim(dmod_all, chip * MOD_SHARD, MOD_SHARD, axis=2), 0, 1)
    grads = {"mod_w": _mod_bwd(c_all, dmod_sh)}
    for n in SMALL:
        grads[n] = tot[n]
    grads["sc_conv_w"] = lax.dynamic_slice_in_dim(tot["sc_conv_w"], chip * 64, 64, axis=2)
    grads["ssm_conv_w"] = lax.dynamic_slice_in_dim(tot["ssm_conv_w"], chip * 192, 192, axis=2)

    keys = [(n, l) for n, _ in BIG for l in range(DEPTH)]
    mine = [_sum_slots(f"sum_{n}{l}", landed[(n, l)]) for n, l in keys]
    theirs = dict(zip(keys, _sibling_exchange("swap_core_sums", mine)))
    mine = dict(zip(keys, mine))

    out = {}

    def update(name, w2, gs, m2, v2, shape):
        g, d, nm, nv = _adamw(f"adamw_{name}", w2, gs, m2, v2)
        out[name] = tuple(a.reshape(shape) for a in (g, d, nm, nv))

    for n, _ in BIG:
        shp = wts[n].shape
        two = (-1, shp[-1])
        by_layer = [tuple(src[(n, l)] for l in range(DEPTH)) for src in (mine, theirs)]
        update(n, wts[n].reshape(two), by_layer, mom[n].reshape(two), var[n].reshape(two), shp)
    two = (-1, MOD_SHARD)
    update("mod_w", mod_w.reshape(two), [grads["mod_w"].reshape(two)], m_mod_w.reshape(two), v_mod_w.reshape(two), mod_w.shape)
    shapes = [wts[n].shape for n in SMALL]
    res = _adamw("adamw_small", _pack([wts[n] for n in SMALL]), [_pack([grads[n] for n in SMALL])],
                 _pack([mom[n] for n in SMALL]), _pack([var[n] for n in SMALL]))
    for n, g, d, nm, nv in zip(SMALL, *[_unpack(r, shapes) for r in res]):
        out[n] = (g, d, nm, nv)

    result = [loss, grad_x[None]]
    for k in range(4):
        result += [out[n][k] for n in WEIGHT_ORDER]
    return tuple(result)
```

```python
import functools

import jax
import jax.numpy as jnp
from jax import lax
from jax.experimental import pallas as pl
from jax.experimental.pallas import tpu as pltpu

F32 = jnp.float32
BF16 = jnp.bfloat16
HIGHEST = lax.Precision.HIGHEST
MESH_ID = pl.DeviceIdType.MESH

D_MODEL = 1024
DEPTH = 2
SC_WIDTH = 256
SB_HEAD_DIM = 64
SSM_INNER = 512
SSM_HEADS = 8
SSM_STATE = 64
SSM_CONV = 4
SSM_CHUNK = 256
SSM_CONV_DIM = 768
FFN_HIDDEN = 2816
NORM_EPS = 1e-6
N_MOD = 6
N_CHIPS = 4
N_DEV = 8

ADAM_LR = 0.001
ADAM_B1 = 0.9
ADAM_B2 = 0.999
ADAM_EPS = 1e-08
ADAM_WD = 0.01
ADAM_STEP = 10

P_WIDTH = 6144
P_A, P_B, P_Z, P_DT, P_XBC, P_G = 0, 768, 1536, 2048, 2304, 3072
DT_PAD = 256

VMEM_LIMIT_BYTES = 56 * 1024 * 1024
LANES = 128

SB_LOG_CUTOFF = -105.0
SB_TQ = 256
SB_TK = 256
SB_SUBS = 2
SB_STRAIGHT = 2


def _params(sem):
    return pltpu.CompilerParams(dimension_semantics=sem, vmem_limit_bytes=VMEM_LIMIT_BYTES)


def _pick(n, cap):
    if n <= cap:
        return n
    best = None
    for m in range(LANES, cap + 1, LANES):
        if n % m == 0:
            best = m
    assert best is not None, (n, cap)
    return best


def _rowwise(name, fn, rows, vecs, row_outs, acc_outs=(), tl=256, side=None):
    L = rows[0][0].shape[0]
    tl = min(tl, L)
    assert L % tl == 0
    n_in = len(rows) + len(vecs)
    n_ro, n_ao = len(row_outs), len(acc_outs)
    n_si = len(side.arrays) if side else 0
    n_so = len(side.out_shapes) if side else 0

    def body(*refs):
        ins, s_in = refs[:n_in], refs[n_in:n_in + n_si]
        outs = refs[n_in + n_si:]
        ro, ao, s_out, sems = outs[:n_ro], outs[n_ro:n_ro + n_ao], outs[n_ro + n_ao:n_ro + n_ao + n_so], outs[n_ro + n_ao + n_so:]
        if side:
            @pl.when(pl.program_id(0) == 0)
            def _():
                side.start(s_in, s_out, sems)

        _rows(ins, ro, ao)
        if side:
            @pl.when(pl.program_id(0) == L // tl - 1)
            def _():
                side.finish(s_in, s_out, sems)

    def _rows(ins, ro, ao):
        vals = fn(*[r[...] for r in ins])
        if not isinstance(vals, (tuple, list)):
            vals = (vals,)
        for o, v in zip(ro, vals[:n_ro]):
            o[...] = v.astype(o.dtype)
        if ao:
            @pl.when(pl.program_id(0) == 0)
            def _():
                for o in ao:
                    o[...] = jnp.zeros_like(o)
            for o, v in zip(ao, vals[n_ro:]):
                o[...] += v.astype(F32)

    in_specs = [pl.BlockSpec((tl, w), functools.partial(lambda i, cb: (i, cb), cb=cb)) for _, w, cb in rows]
    in_specs += [pl.BlockSpec(v.shape, lambda i: (0, 0)) for v in vecs]
    out_specs = [pl.BlockSpec((tl, w), lambda i: (i, 0)) for w, _ in row_outs]
    out_specs += [pl.BlockSpec(s, lambda i: (0, 0)) for s in acc_outs]
    out_shape = [jax.ShapeDtypeStruct((L, w), dt) for w, dt in row_outs]
    out_shape += [jax.ShapeDtypeStruct(s, F32) for s in acc_outs]
    any_spec = pl.BlockSpec(memory_space=pl.ANY)
    return pl.pallas_call(
        body, name=name, grid=(L // tl,), in_specs=in_specs + [any_spec] * n_si, out_specs=out_specs + [any_spec] * n_so,
        out_shape=out_shape + (side.out_shapes if side else []), scratch_shapes=side.scratch if side else [],
        compiler_params=_params(("arbitrary",)),
    )(*[a for a, _, _ in rows], *vecs, *(side.arrays if side else []))


def _mm(name, a, b, mode, out_dtype, tm=1024, tn_cap=1408, tk_cap=2816, side=None):
    if mode == "nn":
        (M, K), (_, N) = a.shape, b.shape
    elif mode == "nt":
        (M, K), (N, _) = a.shape, b.shape
    else:
        (K, M), (_, N) = a.shape, b.shape
        tm, tk_cap = 1408, 2048
    tm = _pick(M, tm)
    tn = _pick(N, tn_cap)
    tk = _pick(K, tk_cap)
    nk = K // tk
    grid = (M // tm, N // tn, nk)
    n_si = len(side.arrays) if side else 0
    n_so = len(side.out_shapes) if side else 0
    n_acc = 1 if nk > 1 else 0

    def body(a_ref, b_ref, *rest):
        s_in, o_ref, s_out = rest[:n_si], rest[n_si], rest[n_si + 1:n_si + 1 + n_so]
        scr = rest[n_si + 1 + n_so:]
        if side:
            at = [pl.program_id(d) for d in range(3)]
            is_first = jnp.logical_and(jnp.logical_and(at[0] == 0, at[1] == 0), at[2] == 0)
            is_last = jnp.logical_and(jnp.logical_and(at[0] == grid[0] - 1, at[1] == grid[1] - 1), at[2] == grid[2] - 1)

            @pl.when(is_first)
            def _():
                side.start(s_in, s_out, scr[n_acc:])

        _product(a_ref, b_ref, o_ref, scr)
        if side:
            @pl.when(is_last)
            def _():
                side.finish(s_in, s_out, scr[n_acc:])

    def _product(a_ref, b_ref, o_ref, scr):
        if mode == "nn":
            p = jnp.dot(a_ref[...], b_ref[...], preferred_element_type=F32)
        elif mode == "nt":
            p = lax.dot_general(a_ref[...], b_ref[...], (((1,), (1,)), ((), ())), preferred_element_type=F32)
        else:
            p = lax.dot_general(a_ref[...], b_ref[...], (((0,), (0,)), ((), ())), preferred_element_type=F32)
        if nk == 1:
            o_ref[...] = p.astype(o_ref.dtype)
        else:
            acc = scr[0]
            k = pl.program_id(2)

            @pl.when(k == 0)
            def _():
                acc[...] = p

            @pl.when(k > 0)
            def _():
                acc[...] += p

            @pl.when(k == nk - 1)
            def _():
                o_ref[...] = acc[...].astype(o_ref.dtype)

    if mode == "nn":
        a_spec = pl.BlockSpec((tm, tk), lambda i, j, k: (i, k))
        b_spec = pl.BlockSpec((tk, tn), lambda i, j, k: (k, j))
    elif mode == "nt":
        a_spec = pl.BlockSpec((tm, tk), lambda i, j, k: (i, k))
        b_spec = pl.BlockSpec((tn, tk), lambda i, j, k: (j, k))
    else:
        a_spec = pl.BlockSpec((tk, tm), lambda i, j, k: (k, i))
        b_spec = pl.BlockSpec((tk, tn), lambda i, j, k: (k, j))
    any_spec = pl.BlockSpec(memory_space=pl.ANY)
    res = pl.pallas_call(
        body, name=name, grid=grid, in_specs=[a_spec, b_spec] + [any_spec] * n_si,
        out_specs=[pl.BlockSpec((tm, tn), lambda i, j, k: (i, j))] + [any_spec] * n_so,
        out_shape=[jax.ShapeDtypeStruct((M, N), out_dtype)] + (side.out_shapes if side else []),
        scratch_shapes=([pltpu.VMEM((tm, tn), F32)] if nk > 1 else []) + (side.scratch if side else []),
        compiler_params=_params(("arbitrary", "arbitrary", "arbitrary")),
    )(a, b, *(side.arrays if side else []))
    return res if side else res[0]


def _mm_epi(name, a, b, mode, tn, extras, epi, outs, tm=512, side=None):
    if mode == "nn":
        (M, K), (_, N) = a.shape, b.shape
    else:
        (M, K), (N, _) = a.shape, b.shape
    tm = _pick(M, tm)
    grid = (N // tn, M // tm)
    n_ex, n_out = len(extras), len(outs)
    n_si = len(side.arrays) if side else 0
    n_so = len(side.out_shapes) if side else 0

    def body(*refs):
        a_ref, b_ref, ex = refs[0], refs[1], refs[2:2 + n_ex]
        s_in = refs[2 + n_ex:2 + n_ex + n_si]
        o_refs = refs[2 + n_ex + n_si:2 + n_ex + n_si + n_out]
        s_out = refs[2 + n_ex + n_si + n_out:2 + n_ex + n_si + n_out + n_so]
        sems = refs[2 + n_ex + n_si + n_out + n_so:]
        if side:
            @pl.when(jnp.logical_and(pl.program_id(0) == 0, pl.program_id(1) == 0))
            def _():
                side.start(s_in, s_out, sems)

        if mode == "nn":
            p = jnp.dot(a_ref[...], b_ref[...], preferred_element_type=F32)
        else:
            p = lax.dot_general(a_ref[...], b_ref[...], (((1,), (1,)), ((), ())), preferred_element_type=F32)
        for o, v in zip(o_refs, epi(p, *[r[...] for r in ex])):
            o[...] = v.astype(o.dtype)
        if side:
            @pl.when(jnp.logical_and(pl.program_id(0) == grid[0] - 1, pl.program_id(1) == grid[1] - 1))
            def _():
                side.finish(s_in, s_out, sems)

    any_spec = pl.BlockSpec(memory_space=pl.ANY)
    a_spec = pl.BlockSpec((tm, K), lambda j, i: (i, 0))
    b_spec = pl.BlockSpec((K, tn), lambda j, i: (0, j)) if mode == "nn" else pl.BlockSpec((tn, K), lambda j, i: (j, 0))
    return pl.pallas_call(
        body, name=name, grid=grid,
        in_specs=[a_spec, b_spec]
        + [pl.BlockSpec(e.shape, lambda j, i: (0, 0)) if w is None else pl.BlockSpec((tm, w), lambda j, i: (i, j))
           for e, w in extras] + [any_spec] * n_si,
        out_specs=[pl.BlockSpec((tm, w), lambda j, i: (i, j)) for w, _ in outs] + [any_spec] * n_so,
        out_shape=[jax.ShapeDtypeStruct((M, (N // tn) * w), dt) for w, dt in outs] + (side.out_shapes if side else []),
        scratch_shapes=side.scratch if side else [],
        compiler_params=_params(("arbitrary", "arbitrary")),
    )(a, b, *[e for e, _ in extras], *(side.arrays if side else []))


def _f(x):
    return x.astype(F32)


def _silu(x):
    return x * jax.nn.sigmoid(x)


def _softplus(x):
    return jnp.maximum(x, 0.0) + jnp.log1p(jnp.exp(-jnp.abs(x)))


def _rms(x, g):
    r = lax.rsqrt(jnp.mean(x * x, axis=-1, keepdims=True) + NORM_EPS)
    return x * r * g


def _adaln(x, g, scale, shift):
    return _rms(x, g) * (1.0 + scale) + shift


def _resid(x, y, gate, g):
    return x + gate * _rms(y, g)


def _mid(x, y, gate, g_post, g_pre, scale, shift):
    x_new = _resid(x, y, gate, g_post)
    return x_new, _adaln(x_new, g_pre, scale, shift)


def _swiglu(gt, up):
    return _silu(gt) * up


def _ssm_post(y_ssd, pre_xs, z, d_full, norm_w):
    y = (y_ssd + _silu(pre_xs) * d_full) * _silu(z)
    half = SSM_INNER // 2
    parts = []
    for g in range(2):
        yg = y[:, g * half:(g + 1) * half]
        parts.append(yg * lax.rsqrt(jnp.mean(yg * yg, axis=-1, keepdims=True) + NORM_EPS))
    return jnp.concatenate(parts, axis=1) * norm_w


def _first_fwd(x, vecs, side=None):
    return _rowwise("adaln_first", lambda x, g, sc, sh: _adaln(x, g, sc, sh),
                    [(x, D_MODEL, 0)], vecs, [(D_MODEL, BF16)], tl=512, side=side)


def _mid_bwd(name, x, y, dx_new, dh, vecs):
    def fn(x, y, dxn, dh, gate, g_post, g_pre, scale, shift):
        y, dh = _f(y), _f(dh)
        r_y = lax.rsqrt(jnp.mean(y * y, axis=-1, keepdims=True) + NORM_EPS)
        n_y = y * r_y
        both = gate * g_post
        x_new = x + n_y * both
        r_x = lax.rsqrt(jnp.mean(x_new * x_new, axis=-1, keepdims=True) + NORM_EPS)
        u = x_new * r_x
        col_p = jnp.sum(dh * u, axis=0, keepdims=True)
        du = dh * (g_pre * (1.0 + scale))
        dxt = dxn + r_x * (du - u * jnp.mean(du * u, axis=-1, keepdims=True))
        col_q = jnp.sum(dxt * n_y, axis=0, keepdims=True)
        dn = dxt * both
        dy = r_y * (dn - n_y * jnp.mean(dn * n_y, axis=-1, keepdims=True))
        return (dxt, dy, col_q * g_post, col_q * gate, col_p * (1.0 + scale), col_p * g_pre,
                jnp.sum(dh, axis=0, keepdims=True))

    vec = (1, D_MODEL)
    return _rowwise(name, fn, [(x, D_MODEL, 0), (y, D_MODEL, 0), (dx_new, D_MODEL, 0), (dh, D_MODEL, 0)], vecs,
                    [(D_MODEL, F32), (D_MODEL, BF16)], [vec] * 5)


def _first_bwd(x, dx_in, dh, vecs, side=None):
    def fn(x, dxi, dh, g, scale, shift):
        dh = _f(dh)
        r = lax.rsqrt(jnp.mean(x * x, axis=-1, keepdims=True) + NORM_EPS)
        u = x * r
        col_p = jnp.sum(dh * u, axis=0, keepdims=True)
        du = dh * (g * (1.0 + scale))
        dx = dxi + r * (du - u * jnp.mean(du * u, axis=-1, keepdims=True))
        return dx, col_p * (1.0 + scale), col_p * g, jnp.sum(dh, axis=0, keepdims=True)

    vec = (1, D_MODEL)
    return _rowwise("adaln_first_bwd", fn, [(x, D_MODEL, 0), (dx_in, D_MODEL, 0), (dh, D_MODEL, 0)], vecs,
                    [(D_MODEL, F32)], [vec] * 3, side=side)


def _last_bwd(x1, f, target, vecs):
    def fn(x1, f, t, gate, g):
        f = _f(f)
        r = lax.rsqrt(jnp.mean(f * f, axis=-1, keepdims=True) + NORM_EPS)
        n = f * r
        both = gate * g
        err = x1 + n * both - t
        d = err * (1.0 / D_MODEL)
        col_q = jnp.sum(d * n, axis=0, keepdims=True)
        dn = d * both
        df = r * (dn - n * jnp.mean(dn * n, axis=-1, keepdims=True))
        loss_cols = jnp.sum(err * err, axis=0, keepdims=True) * (0.5 / D_MODEL)
        return d, df, col_q * g, col_q * gate, loss_cols

    vec = (1, D_MODEL)
    return _rowwise("loss_last_bwd", fn, [(x1, D_MODEL, 0), (f, D_MODEL, 0), (target, D_MODEL, 0)], vecs,
                    [(D_MODEL, F32), (D_MODEL, BF16)], [vec] * 3)


HALO = 16


def _shift_down(u, prev, k):
    rows = lax.broadcasted_iota(jnp.int32, u.shape, 0)
    v = pltpu.roll(u, k, 0)
    for t in range(k):
        v = jnp.where(rows == t, prev[HALO - k + t:HALO - k + t + 1, :], v)
    return v


def _shift_up(u, nxt, k):
    n = u.shape[0]
    rows = lax.broadcasted_iota(jnp.int32, u.shape, 0)
    v = pltpu.roll(u, n - k, 0)
    for t in range(k):
        v = jnp.where(rows == n - k + t, nxt[t:t + 1, :], v)
    return v


def _conv_specs(L, tl, width, col_block):
    per = tl // HALO
    last = L // HALO - 1
    main = pl.BlockSpec((tl, width), lambda i: (i, col_block))
    before = pl.BlockSpec((HALO, width), lambda i: (jnp.maximum(i * per - 1, 0), col_block))
    after = pl.BlockSpec((HALO, width), lambda i: (jnp.minimum((i + 1) * per, last), col_block))
    return main, before, after


def _shortconv_fwd(P, w, tl=512):
    L = P.shape[0]
    tl = min(tl, L)
    C = SC_WIDTH
    main, before, _ = _conv_specs(L, tl, 3 * C, 0)

    def body(p_ref, h_ref, w_ref, o_ref):
        first = (pl.program_id(0) == 0)
        p, h = _f(p_ref[...]), _f(h_ref[...])
        b, u = p[:, :C], p[:, C:2 * C] * p[:, 2 * C:]
        uh = jnp.where(first, 0.0, h[:, C:2 * C] * h[:, 2 * C:])
        wv = w_ref[...]
        cv = wv[2:3] * u + wv[1:2] * _shift_down(u, uh, 1) + wv[0:1] * _shift_down(u, uh, 2)
        o_ref[...] = (b * cv).astype(o_ref.dtype)

    return pl.pallas_call(
        body, name="shortconv_fwd", grid=(L // tl,),
        in_specs=[main, before, pl.BlockSpec(w.shape, lambda i: (0, 0))],
        out_specs=pl.BlockSpec((tl, C), lambda i: (i, 0)),
        out_shape=jax.ShapeDtypeStruct((L, C), BF16), compiler_params=_params(("arbitrary",)),
    )(P, P, w)


def _shortconv_bwd(P, dya, w, tl=512):
    L = P.shape[0]
    tl = min(tl, L)
    C = SC_WIDTH
    main, before, after = _conv_specs(L, tl, 3 * C, 0)
    dmain, _, dafter = _conv_specs(L, tl, C, 0)
    n = L // tl

    def body(p_ref, h_ref, n_ref, d_ref, dn_ref, w_ref, o_ref, dw0, dw1, dw2):
        i = pl.program_id(0)
        p, h, nx = _f(p_ref[...]), _f(h_ref[...]), _f(n_ref[...])
        b, c, x = p[:, :C], p[:, C:2 * C], p[:, 2 * C:]
        u = c * x
        uh = jnp.where(i == 0, 0.0, h[:, C:2 * C] * h[:, 2 * C:])
        u1, u2 = _shift_down(u, uh, 1), _shift_down(u, uh, 2)
        wv = w_ref[...]
        cv = wv[2:3] * u + wv[1:2] * u1 + wv[0:1] * u2
        dy = _f(d_ref[...])
        dcv = dy * b
        dcv_n = jnp.where(i == n - 1, 0.0, _f(dn_ref[...]) * nx[:, :C])
        du = wv[2:3] * dcv + wv[1:2] * _shift_up(dcv, dcv_n, 1) + wv[0:1] * _shift_up(dcv, dcv_n, 2)
        o_ref[:, :C] = (dy * cv).astype(o_ref.dtype)
        o_ref[:, C:2 * C] = (du * x).astype(o_ref.dtype)
        o_ref[:, 2 * C:] = (du * c).astype(o_ref.dtype)

        @pl.when(i == 0)
        def _():
            for r in (dw0, dw1, dw2):
                r[...] = jnp.zeros_like(r)

        dw0[...] += jnp.sum(dcv * u2, axis=0, keepdims=True)
        dw1[...] += jnp.sum(dcv * u1, axis=0, keepdims=True)
        dw2[...] += jnp.sum(dcv * u, axis=0, keepdims=True)

    vec = pl.BlockSpec((1, C), lambda i: (0, 0))
    return pl.pallas_call(
        body, name="shortconv_bwd", grid=(n,),
        in_specs=[main, before, after, dmain, dafter, pl.BlockSpec(w.shape, lambda i: (0, 0))],
        out_specs=[pl.BlockSpec((tl, 3 * C), lambda i: (i, 0)), vec, vec, vec],
        out_shape=[jax.ShapeDtypeStruct((L, 3 * C), BF16)] + [jax.ShapeDtypeStruct((1, C), F32)] * 3,
        compiler_params=_params(("arbitrary",)),
    )(P, P, P, dya, dya, w)


def _ssmconv_fwd(P, w, bias, tl=512):
    L = P.shape[0]
    tl = min(tl, L)
    C = SSM_CONV_DIM
    main, before, _ = _conv_specs(L, tl, C, P_XBC // C)

    def body(p_ref, h_ref, w_ref, b_ref, o_ref):
        u = _f(p_ref[...])
        uh = jnp.where(pl.program_id(0) == 0, 0.0, _f(h_ref[...]))
        wv = w_ref[...]
        acc = wv[3:4] * u + b_ref[...]
        for k in range(1, SSM_CONV):
            acc = acc + wv[3 - k:4 - k] * _shift_down(u, uh, k)
        o_ref[...] = acc

    return pl.pallas_call(
        body, name="ssmconv_fwd", grid=(L // tl,),
        in_specs=[main, before, pl.BlockSpec(w.shape, lambda i: (0, 0)), pl.BlockSpec(bias.shape, lambda i: (0, 0))],
        out_specs=pl.BlockSpec((tl, C), lambda i: (i, 0)),
        out_shape=jax.ShapeDtypeStruct((L, C), F32), compiler_params=_params(("arbitrary",)),
    )(P, P, w, bias)


def _ssmconv_bwd(P, dpre, w, tl=512):
    L = P.shape[0]
    tl = min(tl, L)
    C = SSM_CONV_DIM
    main, before, _ = _conv_specs(L, tl, C, P_XBC // C)
    dmain, _, dafter = _conv_specs(L, tl, C, 0)
    n = L // tl

    def body(p_ref, h_ref, d_ref, dn_ref, w_ref, o_ref, dw0, dw1, dw2, dw3, db):
        i = pl.program_id(0)
        u = _f(p_ref[...])
        uh = jnp.where(i == 0, 0.0, _f(h_ref[...]))
        d = d_ref[...]
        dn = jnp.where(i == n - 1, 0.0, dn_ref[...])
        wv = w_ref[...]
        du = wv[3:4] * d
        for k in range(1, SSM_CONV):
            du = du + wv[3 - k:4 - k] * _shift_up(d, dn, k)
        o_ref[...] = du.astype(o_ref.dtype)

        @pl.when(i == 0)
        def _():
            for r in (dw0, dw1, dw2, dw3, db):
                r[...] = jnp.zeros_like(r)

        for k, r in ((3, dw0), (2, dw1), (1, dw2)):
            r[...] += jnp.sum(d * _shift_down(u, uh, k), axis=0, keepdims=True)
        dw3[...] += jnp.sum(d * u, axis=0, keepdims=True)
        db[...] += jnp.sum(d, axis=0, keepdims=True)

    vec = pl.BlockSpec((1, C), lambda i: (0, 0))
    return pl.pallas_call(
        body, name="ssmconv_bwd", grid=(n,),
        in_specs=[main, before, dmain, dafter, pl.BlockSpec(w.shape, lambda i: (0, 0))],
        out_specs=[pl.BlockSpec((tl, C), lambda i: (i, 0))] + [vec] * 5,
        out_shape=[jax.ShapeDtypeStruct((L, C), BF16)] + [jax.ShapeDtypeStruct((1, C), F32)] * 5,
        compiler_params=_params(("arbitrary",)),
    )(P, P, dpre, dpre, w)


def _dot_nt(a, b):
    return lax.dot_general(a, b, (((1,), (1,)), ((), ())), preferred_element_type=F32)


def _dot_tn(a, b):
    return lax.dot_general(a, b, (((0,), (0,)), ((), ())), preferred_element_type=F32)


def _split3(x):
    hi = x.astype(BF16)
    r = x - hi.astype(F32)
    mid = r.astype(BF16)
    return hi, mid, (r - mid.astype(F32)).astype(BF16)


@jax.custom_vjp
def _xm01(x, m):
    return sum(jnp.dot(t, m, preferred_element_type=F32) for t in _split3(x))


def _xm01_fwd(x, m):
    return _xm01(x, m), m


def _xm01_bwd(m, g):
    return sum(_dot_nt(t, m) for t in _split3(g)), jnp.zeros_like(m)


_xm01.defvjp(_xm01_fwd, _xm01_bwd)


@jax.custom_vjp
def _m01x(m, x):
    return sum(jnp.dot(m, t, preferred_element_type=F32) for t in _split3(x))


def _m01x_fwd(m, x):
    return _m01x(m, x), m


def _m01x_bwd(m, g):
    return jnp.zeros_like(m), sum(_dot_tn(m, t) for t in _split3(g))


_m01x.defvjp(_m01x_fwd, _m01x_bwd)


def _ssd_chunk(pre, dtr, s_prev, dtb, alog):
    T = pre.shape[0]
    act = _silu(pre)
    xs, bm, cm = act[:, :SSM_INNER], act[:, SSM_INNER:SSM_INNER + 128], act[:, SSM_INNER + 128:]
    lane = lax.broadcasted_iota(jnp.int32, (1, LANES), 1)
    dt = jnp.where(lane < SSM_HEADS, _softplus(dtr + dtb), 0.0)
    a = dt * (-jnp.exp(alog))
    ri = lax.broadcasted_iota(jnp.int32, (T, T), 0)
    ci = lax.broadcasted_iota(jnp.int32, (T, T), 1)
    causal = ci <= ri
    a_cs = _m01x(causal.astype(BF16), a)
    eh = lax.broadcasted_iota(jnp.int32, (LANES, SSM_INNER), 0)
    ej = lax.broadcasted_iota(jnp.int32, (LANES, SSM_INNER), 1)
    expand = (lax.shift_right_logical(ej, 6) == eh).astype(BF16)
    dt_full = _xm01(dt, expand)
    acs_full = _xm01(a_cs, expand)
    alast_full = acs_full[T - 1:T, :]
    xdt = xs * dt_full
    a_cs_t = a_cs.T
    ys, s_new = [], []
    for g in range(2):
        in_group = lax.shift_right_logical(lane, 6) == g
        cg = jnp.where(in_group, cm, 0.0).astype(BF16)
        bg = jnp.where(in_group, bm, 0.0).astype(BF16)
        scores = _dot_nt(cg, bg)
        for pp in range(2):
            hp = 2 * g + pp
            cols = slice(hp * LANES, (hp + 1) * LANES)
            xp, acsp = xdt[:, cols], acs_full[:, cols]
            per_head = []
            for hh in range(2):
                h = 2 * hp + hh
                decay = jnp.exp(jnp.where(causal, a_cs[:, h:h + 1] - a_cs_t[h:h + 1, :], -jnp.inf))
                per_head.append(jnp.dot((scores * decay).astype(BF16), xp.astype(BF16), preferred_element_type=F32))
            y_diag = jnp.where(lane < SSM_STATE, per_head[0], per_head[1])
            sp = s_prev[hp * LANES:(hp + 1) * LANES, :]
            y_off = jnp.dot(cg, sp.astype(BF16), preferred_element_type=F32) * jnp.exp(acsp)
            ys.append(y_diag + y_off)
            to_end = jnp.exp(alast_full[:, cols] - acsp)
            s_new.append(sp * jnp.exp(alast_full[:, cols]) + _dot_tn(bg, (xp * to_end).astype(BF16)))
    return jnp.concatenate(ys, axis=1), jnp.concatenate(s_new, axis=0)


def _ssd_fwd(pre, P, dtb, alog):
    L = pre.shape[0]
    T = min(SSM_CHUNK, L)
    nc = L // T

    def body(pre_ref, dt_ref, dtb_ref, al_ref, y_ref, st_ref, s_scr):
        @pl.when(pl.program_id(0) == 0)
        def _():
            s_scr[...] = jnp.zeros_like(s_scr)

        st_ref[0] = s_scr[...]
        y, s = _ssd_chunk(pre_ref[...], _f(dt_ref[...]), s_scr[...], dtb_ref[...], al_ref[...])
        y_ref[...] = y
        s_scr[...] = s

    vec = pl.BlockSpec((1, LANES), lambda i: (0, 0))
    return pl.pallas_call(
        body, name="ssd_fwd", grid=(nc,),
        in_specs=[pl.BlockSpec((T, SSM_CONV_DIM), lambda i: (i, 0)), pl.BlockSpec((T, LANES), lambda i: (i, P_DT // LANES)),
                  vec, vec],
        out_specs=[pl.BlockSpec((T, SSM_INNER), lambda i: (i, 0)), pl.BlockSpec((1, 512, LANES), lambda i: (i, 0, 0))],
        out_shape=[jax.ShapeDtypeStruct((L, SSM_INNER), F32), jax.ShapeDtypeStruct((nc, 512, LANES), F32)],
        scratch_shapes=[pltpu.VMEM((512, LANES), F32)], compiler_params=_params(("arbitrary",)),
    )(pre, P, dtb, alog)


def _ssd_bwd(pre, P, states, dy, dxs_extra, dtb, alog):
    L = pre.shape[0]
    T = min(SSM_CHUNK, L)
    nc = L // T

    def body(pre_ref, dt_ref, st_ref, dy_ref, dx_ref, dtb_ref, al_ref, dpre_ref, ddt_ref, ddtb_ref, dal_ref, ds_scr):
        @pl.when(pl.program_id(0) == 0)
        def _():
            ds_scr[...] = jnp.zeros_like(ds_scr)
            ddtb_ref[...] = jnp.zeros_like(ddtb_ref)
            dal_ref[...] = jnp.zeros_like(dal_ref)

        _, vjp = jax.vjp(_ssd_chunk, pre_ref[...], _f(dt_ref[...]), st_ref[0], dtb_ref[...], al_ref[...])
        dpre, ddt, ds, ddtb, dal = vjp((dy_ref[...], ds_scr[...]))
        dpre_ref[:, :SSM_INNER] = dpre[:, :SSM_INNER] + dx_ref[...]
        dpre_ref[:, SSM_INNER:] = dpre[:, SSM_INNER:]
        ddt_ref[:, :LANES] = ddt.astype(ddt_ref.dtype)
        ddt_ref[:, LANES:] = jnp.zeros((T, DT_PAD - LANES), ddt_ref.dtype)
        ds_scr[...] = ds
        ddtb_ref[...] += ddtb
        dal_ref[...] += dal

    vec = pl.BlockSpec((1, LANES), lambda i: (0, 0))
    rev = lambda i: (nc - 1 - i, 0)
    return pl.pallas_call(
        body, name="ssd_bwd", grid=(nc,),
        in_specs=[pl.BlockSpec((T, SSM_CONV_DIM), rev), pl.BlockSpec((T, LANES), lambda i: (nc - 1 - i, P_DT // LANES)),
                  pl.BlockSpec((1, 512, LANES), lambda i: (nc - 1 - i, 0, 0)),
                  pl.BlockSpec((T, SSM_INNER), rev), pl.BlockSpec((T, SSM_INNER), rev), vec, vec],
        out_specs=[pl.BlockSpec((T, SSM_CONV_DIM), rev), pl.BlockSpec((T, DT_PAD), rev), vec, vec],
        out_shape=[jax.ShapeDtypeStruct((L, SSM_CONV_DIM), F32), jax.ShapeDtypeStruct((L, DT_PAD), BF16),
                   jax.ShapeDtypeStruct((1, LANES), F32), jax.ShapeDtypeStruct((1, LANES), F32)],
        scratch_shapes=[pltpu.VMEM((512, LANES), F32)], compiler_params=_params(("arbitrary",)),
    )(pre, P, states, dy, dxs_extra, dtb, alog)


def _sb_scores(qm, kb, later, strict, mask):
    z = _dot_nt(qm, kb)
    lk = jnp.minimum(-z, 0.0) - jnp.log(1.0 + jnp.exp(-jnp.abs(z)))
    if mask is not None:
        lk = jnp.where(mask, lk, 0.0)
    log_a = z + lk + jnp.dot(lk.astype(BF16), strict, preferred_element_type=F32) + later
    if mask is not None:
        log_a = jnp.where(mask, log_a, -jnp.inf)
    return z, lk, log_a


def _dot_split(x, m):
    hi = x.astype(BF16)
    lo = (x - hi.astype(F32)).astype(BF16)
    return jnp.dot(hi, m, preferred_element_type=F32) + jnp.dot(lo, m, preferred_element_type=F32)


def _sb_setup(q_ref, i, tq, tk):
    lane = lax.broadcasted_iota(jnp.int32, (1, LANES), 1)
    first = lane < SB_HEAD_DIM
    q = q_ref[...] * (SB_HEAD_DIM ** -0.5)
    qms = (jnp.where(first, q, jnp.zeros_like(q)), jnp.where(first, jnp.zeros_like(q), q))
    j0 = lax.div(i * tq, tk)
    ri = lax.broadcasted_iota(jnp.int32, (tq, tk), 0)
    ci = lax.broadcasted_iota(jnp.int32, (tq, tk), 1)
    diag_mask = (ci + (j0 * tk - i * tq)) < ri
    kr = lax.broadcasted_iota(jnp.int32, (tk, tk), 0)
    kc = lax.broadcasted_iota(jnp.int32, (tk, tk), 1)
    strict = (kr > kc).astype(BF16)
    return first, qms, j0, diag_mask, strict


def _sb_continue(c):
    return jnp.logical_and(c[0] >= 0, jnp.maximum(jnp.max(c[1][0]), jnp.max(c[1][1])) > SB_LOG_CUTOFF)


def _sb_fwd(P):
    L = P.shape[0]
    tq, tk = min(SB_TQ, L), min(SB_TK, L)
    nq = L // tq
    qb = P_B // LANES

    subs = SB_SUBS if L % (SB_SUBS * tq) == 0 else 1

    def body(q_ref, k_ref, v_ref, o_ref, of_ref):
        zero, zacc = jnp.zeros((tq, 1), F32), jnp.zeros((tq, LANES), F32)
        walks = []
        for s in range(subs):
            rows = pl.ds(s * tq, tq)
            first, qms, j0, diag_mask, strict = _sb_setup(q_ref.at[rows, :], pl.program_id(1) * subs + s, tq, tk)

            def tile(h, j, later, acc, mask=None, valid=None, qms=qms, strict=strict):
                off = pl.multiple_of(j * tk, tk)
                gate = later if valid is None else jnp.where(valid, later, -jnp.inf)
                _, lk, log_a = _sb_scores(qms[h], k_ref[pl.ds(off, tk), :], gate, strict, mask)
                acc = acc + jnp.dot(jnp.exp(log_a).astype(BF16), v_ref[pl.ds(off, tk), :], preferred_element_type=F32)
                total = jnp.sum(lk, axis=1, keepdims=True)
                return later + (total if valid is None else jnp.where(valid, total, 0.0)), acc

            state = []
            for h in range(2):
                carry = tile(h, j0, zero, zacc, mask=diag_mask)
                for n in range(1, SB_STRAIGHT):
                    carry = tile(h, jnp.maximum(j0 - n, 0), *carry, valid=j0 >= n)
                state.append(carry)
            walks.append((rows, first, j0, tile, state))

        for rows, first, j0, tile, state in walks:
            def tail(c, tile=tile):
                res = [tile(h, c[0], c[1][h], c[2][h]) for h in range(2)]
                return c[0] - 1, (res[0][0], res[1][0]), (res[0][1], res[1][1])

            _, _, accs = lax.while_loop(
                _sb_continue, tail, (j0 - SB_STRAIGHT, (state[0][0], state[1][0]), (state[0][1], state[1][1])))
            out = jnp.where(first, accs[0], accs[1])
            o_ref[rows, :] = out.astype(o_ref.dtype)
            of_ref[rows, :] = out

    nq = nq // subs
    tile_spec = pl.BlockSpec((subs * tq, LANES), lambda p, i: (i, p))
    return pl.pallas_call(
        body, name="sb_fwd", grid=(2, nq),
        in_specs=[pl.BlockSpec((subs * tq, LANES), lambda p, i: (i, qb + p)),
                  pl.BlockSpec((L, LANES), lambda p, i: (0, qb + 2 + p)),
                  pl.BlockSpec((L, LANES), lambda p, i: (0, qb + 4 + p))],
        out_specs=[tile_spec, tile_spec],
        out_shape=[jax.ShapeDtypeStruct((L, 2 * LANES), BF16), jax.ShapeDtypeStruct((L, 2 * LANES), F32)],
        compiler_params=_params(("arbitrary", "arbitrary")),
    )(P, P, P)


def _sb_bwd(P, dyb, yb32):
    L = P.shape[0]
    tq, tk = min(SB_TQ, L), min(SB_TK, L)
    nq = L // tq
    qb = P_B // LANES

    def body(q_ref, k_ref, v_ref, do_ref, of_ref, dq_ref, dk_ref, dv_ref):
        i = pl.program_id(1)
        first, qms, j0, diag_mask, strict = _sb_setup(q_ref, i, tq, tk)

        @pl.when(i == 0)
        def _():
            dk_ref[...] = jnp.zeros_like(dk_ref)
            dv_ref[...] = jnp.zeros_like(dv_ref)

        do = do_ref[...]
        doms = (jnp.where(first, do, jnp.zeros_like(do)), jnp.where(first, jnp.zeros_like(do), do))
        prod = _f(do) * of_ref[...]
        totals = (jnp.sum(jnp.where(first, prod, 0.0), axis=1, keepdims=True),
                  jnp.sum(jnp.where(first, 0.0, prod), axis=1, keepdims=True))

        def tile(h, j, later, later_g, acc, mask=None, valid=None):
            off = pl.multiple_of(j * tk, tk)
            kb, vb = k_ref[pl.ds(off, tk), :], v_ref[pl.ds(off, tk), :]
            gate = later if valid is None else jnp.where(valid, later, -jnp.inf)
            z, lk, log_a = _sb_scores(qms[h], kb, gate, strict, mask)
            att = jnp.exp(log_a).astype(BF16)
            g = _f(att) * _dot_nt(doms[h], vb)
            before = totals[h] - later_g
            if valid is not None:
                before = jnp.where(valid, before, 0.0)
            dz = g - (before - _dot_split(g, strict)) * jnp.exp(z + lk)
            if mask is not None:
                dz = jnp.where(mask, dz, 0.0)
            dzb = dz.astype(BF16)
            rows = jnp.sum(lk, axis=1, keepdims=True)
            carry = (later + (rows if valid is None else jnp.where(valid, rows, 0.0)),
                     later_g + jnp.sum(g, axis=1, keepdims=True), acc + jnp.dot(dzb, kb, preferred_element_type=F32))
            return carry, _dot_tn(dzb, qms[h]), _dot_tn(att, doms[h])

        def tail(c):
            off = pl.multiple_of(c[0] * tk, tk)
            (c0, dk0, dv0), (c1, dk1, dv1) = [tile(h, c[0], c[1][h], c[2][h], c[3][h]) for h in range(2)]
            dk_ref[pl.ds(off, tk), :] += dk0 + dk1
            dv_ref[pl.ds(off, tk), :] += dv0 + dv1
            return (c[0] - 1,) + tuple(zip(c0, c1))

        zero, zacc = jnp.zeros((tq, 1), F32), jnp.zeros((tq, LANES), F32)
        blocks = [j0] + [jnp.maximum(j0 - n, 0) for n in range(1, SB_STRAIGHT)]
        carries, dks, dvs = [], [], []
        for h in range(2):
            carry, dk, dv = tile(h, j0, zero, zero, zacc, mask=diag_mask)
            dks.append([dk])
            dvs.append([dv])
            for n in range(1, SB_STRAIGHT):
                carry, dk, dv = tile(h, blocks[n], *carry, valid=j0 >= n)
                dks[h].append(dk)
                dvs[h].append(dv)
            carries.append(carry)
        for n, j in enumerate(blocks):
            off = pl.multiple_of(j * tk, tk)
            dk_ref[pl.ds(off, tk), :] += dks[0][n] + dks[1][n]
            dv_ref[pl.ds(off, tk), :] += dvs[0][n] + dvs[1][n]
        accs = lax.while_loop(_sb_continue, tail, (j0 - SB_STRAIGHT,) + tuple(zip(carries[0], carries[1])))[3]
        dq_ref[...] = jnp.where(first, accs[0], accs[1]) * (SB_HEAD_DIM ** -0.5)

    full = pl.BlockSpec((L, LANES), lambda p, i: (0, p))
    tile_spec = pl.BlockSpec((tq, LANES), lambda p, i: (i, p))
    return pl.pallas_call(
        body, name="sb_bwd", grid=(2, nq),
        in_specs=[pl.BlockSpec((tq, LANES), lambda p, i: (i, qb + p)),
                  pl.BlockSpec((L, LANES), lambda p, i: (0, qb + 2 + p)),
                  pl.BlockSpec((L, LANES), lambda p, i: (0, qb + 4 + p)), tile_spec, tile_spec],
        out_specs=[tile_spec, full, full],
        out_shape=[jax.ShapeDtypeStruct((L, 2 * LANES), F32)] * 3,
        compiler_params=_params(("arbitrary", "arbitrary")),
    )(P, P, P, dyb, yb32)


MOD_SHARD = N_MOD * D_MODEL // N_CHIPS


def _mod_fwd(c_all, mod_w, mod_b_sh):
    tn = 512

    def body(c_ref, w_ref, b_ref, o_ref):
        o_ref[0] = jnp.dot(_silu(c_ref[...]), w_ref[0], precision=HIGHEST, preferred_element_type=F32) + b_ref[0]

    return pl.pallas_call(
        body, name="mod_fwd", grid=(DEPTH, MOD_SHARD // tn),
        in_specs=[pl.BlockSpec((N_DEV, D_MODEL), lambda l, j: (0, 0)),
                  pl.BlockSpec((1, D_MODEL, tn), lambda l, j: (l, 0, j)),
                  pl.BlockSpec((1, 1, tn), lambda l, j: (l, 0, j))],
        out_specs=pl.BlockSpec((1, N_DEV, tn), lambda l, j: (l, 0, j)),
        out_shape=jax.ShapeDtypeStruct((DEPTH, N_DEV, MOD_SHARD), F32),
        compiler_params=_params(("arbitrary", "arbitrary")),
    )(c_all, mod_w, mod_b_sh)


def _mod_bwd(c_all, dmod_sh):
    tn = 512

    def body(c_ref, d_ref, o_ref):
        o_ref[0] = lax.dot_general(_silu(c_ref[...]), d_ref[0], (((0,), (0,)), ((), ())), precision=HIGHEST,
                                   preferred_element_type=F32)

    return pl.pallas_call(
        body, name="mod_bwd", grid=(DEPTH, MOD_SHARD // tn),
        in_specs=[pl.BlockSpec((N_DEV, D_MODEL), lambda l, j: (0, 0)),
                  pl.BlockSpec((1, N_DEV, tn), lambda l, j: (l, 0, j))],
        out_specs=pl.BlockSpec((1, D_MODEL, tn), lambda l, j: (l, 0, j)),
        out_shape=jax.ShapeDtypeStruct((DEPTH, D_MODEL, MOD_SHARD), F32),
        compiler_params=_params(("arbitrary", "arbitrary")),
    )(c_all, dmod_sh)


def _row_tile(rows, cap):
    if rows <= cap:
        return rows
    best = None
    for t in range(8, cap + 1, 8):
        if rows % t == 0:
            best = t
    assert best is not None, (rows, cap)
    return best


def _adamw(name, w, gs, m, v, tr=256):
    R, W = w.shape
    by_layer = any(isinstance(t, tuple) for t in gs)
    tr = _row_tile(R // 2 if by_layer else R, tr)
    per = (R // 2) // tr

    flat, specs = [], []
    for t in gs:
        if isinstance(t, tuple):
            flat += list(t)
            specs += [pl.BlockSpec((tr, W), lambda i: (jnp.minimum(i, per - 1), 0)),
                      pl.BlockSpec((tr, W), lambda i: (jnp.maximum(i - per, 0), 0))]
        else:
            flat.append(t)
            specs.append(pl.BlockSpec((tr, W), lambda i: (i, 0)))
    ng = len(flat)

    def body(*refs):
        w_ref, g_refs, (m_ref, v_ref) = refs[0], list(refs[1:1 + ng]), refs[1 + ng:3 + ng]
        g_out, d_out, m_out, v_out = refs[3 + ng:]
        g = None
        for t in gs:
            if isinstance(t, tuple):
                lo, hi = g_refs.pop(0), g_refs.pop(0)
                term = jnp.where(pl.program_id(0) < per, lo[...], hi[...])
            else:
                term = g_refs.pop(0)[...]
            g = term if g is None else g + term
        mm = ADAM_B1 * m_ref[...] + (1.0 - ADAM_B1) * g
        vv = ADAM_B2 * v_ref[...] + (1.0 - ADAM_B2) * (g * g)
        m_hat = mm / (1.0 - ADAM_B1 ** ADAM_STEP)
        v_hat = vv / (1.0 - ADAM_B2 ** ADAM_STEP)
        g_out[...] = g
        d_out[...] = -ADAM_LR * (m_hat / (jnp.sqrt(v_hat) + ADAM_EPS) + ADAM_WD * w_ref[...])
        m_out[...] = mm
        v_out[...] = vv

    spec = pl.BlockSpec((tr, W), lambda i: (i, 0))
    return pl.pallas_call(
        body, name=name, grid=(R // tr,), in_specs=[spec] + specs + [spec, spec], out_specs=[spec] * 4,
        out_shape=[jax.ShapeDtypeStruct((R, W), F32)] * 4, compiler_params=_params(("arbitrary",)),
    )(w, *flat, m, v)


def _sum_slots(name, a, tr=256):
    n, R, W = a.shape
    tr = _row_tile(R, tr)

    def body(a_ref, o_ref):
        acc = _f(a_ref[0])
        for j in range(1, n):
            acc = acc + _f(a_ref[j])
        o_ref[...] = acc

    return pl.pallas_call(
        body, name=name, grid=(R // tr,), in_specs=[pl.BlockSpec((n, tr, W), lambda i: (0, i, 0))],
        out_specs=pl.BlockSpec((tr, W), lambda i: (i, 0)), out_shape=jax.ShapeDtypeStruct((R, W), F32),
        compiler_params=_params(("arbitrary",)),
    )(a)


def _here():
    return lax.axis_index("x"), lax.axis_index("y"), lax.axis_index("c")


def _flip(v, d):
    return 1 - v if d else v


def _allgather_small(name, buf):
    R = buf.shape[0]
    rel = [(dx, dy, dc) for dx in (0, 1) for dy in (0, 1) for dc in (0, 1)][1:]

    def body(x_ref, o_ref, send, recv, lsem):
        x, y, c = _here()
        me = 4 * x + 2 * y + c
        mine = pltpu.make_async_copy(x_ref, o_ref.at[me], lsem)
        mine.start()

        def copy(k, slot):
            dx, dy, dc = rel[k]
            return pltpu.make_async_remote_copy(
                src_ref=x_ref, dst_ref=o_ref.at[slot], send_sem=send.at[k], recv_sem=recv.at[k],
                device_id=(_flip(x, dx), _flip(y, dy), _flip(c, dc)), device_id_type=MESH_ID)

        sent = [copy(k, me) for k in range(len(rel))]
        for cp in sent:
            cp.start()
        for k, (dx, dy, dc) in enumerate(rel):
            copy(k, 4 * _flip(x, dx) + 2 * _flip(y, dy) + _flip(c, dc)).wait_recv()
        for cp in sent:
            cp.wait_send()
        mine.wait()

    return pl.pallas_call(
        body, name=name, out_shape=jax.ShapeDtypeStruct((N_DEV, R, LANES), F32),
        in_specs=[pl.BlockSpec(memory_space=pltpu.VMEM)], out_specs=pl.BlockSpec(memory_space=pltpu.VMEM),
        scratch_shapes=[pltpu.SemaphoreType.DMA((7,)), pltpu.SemaphoreType.DMA((7,)), pltpu.SemaphoreType.DMA],
    )(buf)


CHIP_REL = [(1, 0), (0, 1), (1, 1)]


class _Side:
    def __init__(self, arrays, out_shapes, scratch, start, finish):
        self.arrays, self.out_shapes, self.scratch, self.start, self.finish = arrays, out_shapes, scratch, start, finish


def _chip_of(k):
    x, y, _ = _here()
    dx, dy = CHIP_REL[k]
    return _flip(x, dx), _flip(y, dy)


def _scatter_side(arrays):
    n = len(arrays)

    def parts(ins, outs, sems):
        send, recv, lsem = sems
        x, y, c = _here()
        s = 2 * x + y

        def copy(w, k, mine):
            px, py = _chip_of(k)
            return pltpu.make_async_remote_copy(
                src_ref=ins[w].at[2 * px + py], dst_ref=outs[w].at[s if mine else 2 * px + py],
                send_sem=send.at[3 * w + k], recv_sem=recv.at[3 * w + k], device_id=(px, py, c), device_id_type=MESH_ID)

        local = [pltpu.make_async_copy(ins[w].at[s], outs[w].at[s], lsem.at[w]) for w in range(n)]
        return copy, local

    def start(ins, outs, sems):
        copy, local = parts(ins, outs, sems)
        for cp in local:
            cp.start()
        for w in range(n):
            for k in range(3):
                copy(w, k, True).start()

    def finish(ins, outs, sems):
        copy, local = parts(ins, outs, sems)
        for w in range(n):
            for k in range(3):
                copy(w, k, False).wait_recv()
        for w in range(n):
            for k in range(3):
                copy(w, k, True).wait_send()
        for cp in local:
            cp.wait()

    scratch = [pltpu.SemaphoreType.DMA((3 * n,)), pltpu.SemaphoreType.DMA((3 * n,)), pltpu.SemaphoreType.DMA((n,))]
    return _Side(arrays, [jax.ShapeDtypeStruct(a.shape, a.dtype) for a in arrays], scratch, start, finish)


def _gather_side(shards):
    n = len(shards)

    def parts(ins, outs, sems):
        send, recv, fsend, frecv, lsem = sems
        x, y, c = _here()
        s = 2 * x + y

        def half(ref, w, which):
            rows = shards[w].shape[0] // 2
            return ref.at[pl.ds(pl.multiple_of(which * rows, 16), rows)]

        def over_ici(w, k, mine):
            px, py = _chip_of(k)
            return pltpu.make_async_remote_copy(
                src_ref=half(ins[w], w, c), dst_ref=half(outs[w].at[s if mine else 2 * px + py], w, c),
                send_sem=send.at[3 * w + k], recv_sem=recv.at[3 * w + k], device_id=(px, py, c), device_id_type=MESH_ID)

        def to_sibling(w, k, which):
            px, py = _chip_of(k)
            part = half(outs[w].at[2 * px + py], w, which)
            return pltpu.make_async_remote_copy(
                src_ref=part, dst_ref=part, send_sem=fsend.at[3 * w + k], recv_sem=frecv.at[3 * w + k],
                device_id=(x, y, 1 - c), device_id_type=MESH_ID)

        local = [pltpu.make_async_copy(ins[w], outs[w].at[s], lsem.at[w]) for w in range(n)]
        return c, over_ici, to_sibling, local

    def start(ins, outs, sems):
        _, over_ici, _, local = parts(ins, outs, sems)
        for cp in local:
            cp.start()
        for w in range(n):
            for k in range(3):
                over_ici(w, k, True).start()

    def finish(ins, outs, sems):
        c, over_ici, to_sibling, local = parts(ins, outs, sems)
        for w in range(n):
            for k in range(3):
                over_ici(w, k, False).wait_recv()
                to_sibling(w, k, c).start()
        for w in range(n):
            for k in range(3):
                to_sibling(w, k, 1 - c).wait_recv()
        for w in range(n):
            for k in range(3):
                over_ici(w, k, True).wait_send()
                to_sibling(w, k, c).wait_send()
        for cp in local:
            cp.wait()

    scratch = [pltpu.SemaphoreType.DMA((3 * n,))] * 4 + [pltpu.SemaphoreType.DMA((n,))]
    return _Side(shards, [jax.ShapeDtypeStruct((N_CHIPS,) + a.shape, a.dtype) for a in shards], scratch, start, finish)


def _sibling_exchange(name, arrays):
    n = len(arrays)

    def body(*refs):
        ins, outs = refs[:n], refs[n:2 * n]
        send, recv = refs[2 * n:]
        x, y, c = _here()
        cps = [pltpu.make_async_remote_copy(src_ref=ins[w], dst_ref=outs[w], send_sem=send.at[w], recv_sem=recv.at[w],
                                            device_id=(x, y, 1 - c), device_id_type=MESH_ID) for w in range(n)]
        for cp in cps:
            cp.start()
        for cp in cps:
            cp.wait()

    any_spec = pl.BlockSpec(memory_space=pl.ANY)
    return pl.pallas_call(
        body, name=name, out_shape=[jax.ShapeDtypeStruct(a.shape, a.dtype) for a in arrays],
        in_specs=[any_spec] * n, out_specs=[any_spec] * n,
        scratch_shapes=[pltpu.SemaphoreType.DMA((n,)), pltpu.SemaphoreType.DMA((n,))],
    )(*arrays)


def _pack(arrs):
    flat = jnp.concatenate([a.reshape(-1).astype(F32) for a in arrs])
    n = flat.shape[0]
    rows = -(-n // (8 * LANES)) * 8
    return jnp.pad(flat, (0, rows * LANES - n)).reshape(rows, LANES)


def _unpack(buf, shapes):
    lead = buf.shape[:-2]
    flat = buf.reshape(lead + (-1,))
    out, off = [], 0
    for s in shapes:
        n = 1
        for d in s:
            n *= d
        out.append(flat[..., off:off + n].reshape(lead + tuple(s)))
        off += n
    return out


def _pad_w_in(w):
    return jnp.concatenate([w[:, :2048], w[:, 2816:2824], jnp.zeros((w.shape[0], P_XBC - P_DT - 8), w.dtype),
                            w[:, 2048:2816], w[:, 2824:]], axis=1)


def _unpad_w_in(g):
    return jnp.concatenate([g[:, :P_DT], g[:, P_XBC:P_G], g[:, P_DT:P_DT + 8], g[:, P_G:]], axis=1)


FFN_HALF = FFN_HIDDEN // 2


def _ffn_in_cols(w):
    h = FFN_HALF
    return jnp.concatenate([w[:, :h], w[:, 2 * h:3 * h], w[:, h:2 * h], w[:, 3 * h:]], axis=1)


def _row(v):
    return v.reshape(1, -1)


BIG = (("w_in", 2), ("w_sc_out", 2), ("w_sb_out", 2), ("w_ssm_out", 2), ("w_o", 1), ("w_ffn_in", 2), ("w_ffn_out", 1))
SMALL = ("mod_b", "g_pre_mix", "g_post_mix", "g_pre_ffn", "g_post_ffn", "sc_conv_w", "ssm_conv_w", "ssm_conv_b",
         "ssm_dt_bias", "ssm_a_log", "ssm_d", "ssm_norm_w")
WEIGHT_ORDER = ("mod_w", "mod_b", "g_pre_mix", "g_post_mix", "g_pre_ffn", "g_post_ffn", "w_in", "sc_conv_w",
                "ssm_conv_w", "ssm_conv_b", "ssm_dt_bias", "ssm_a_log", "ssm_d", "ssm_norm_w", "w_sc_out", "w_sb_out",
                "w_ssm_out", "w_o", "w_ffn_in", "w_ffn_out")


def _mm_mid(name, a, w, x, vecs):
    return _mm_epi(name, a, w, "nn", D_MODEL, [(x, D_MODEL)] + [(v, None) for v in vecs],
                   lambda p, x, *v: (p,) + tuple(_mid(x, p, *v)), [(D_MODEL, BF16), (D_MODEL, F32), (D_MODEL, BF16)])


def _layer_fwd(l, x_in, h, W, V, sides, next_vecs):
    S = {"x_in": x_in, "h": h}
    side, handler = sides.get("in_proj", (None, None))
    P = _mm(f"in_proj{l}", h, W["w_in"], "nn", BF16, tm=2048, tn_cap=1024, side=side)
    if side:
        handler(P[1:])
        P = P[0]
    S["P"] = P
    S["ya"] = _shortconv_fwd(P, V["sc_w"])
    S["yb"], S["yb32"] = _sb_fwd(P)
    S["pre"] = _ssmconv_fwd(P, V["ssm_w"], V["ssm_b"])
    S["y_ssd"], S["states"] = _ssd_fwd(S["pre"], P, V["dtb"], V["alog"])
    S["yc"] = _rowwise(f"ssm_post{l}", lambda y, px, z, d, nw: _ssm_post(y, px, _f(z), d, nw),
                       [(S["y_ssd"], SSM_INNER, 0), (S["pre"], SSM_INNER, 0), (P, SSM_INNER, P_Z // SSM_INNER)],
                       [V["d_full"], V["norm_w"]], [(SSM_INNER, BF16)])[0]
    S["merged"] = _merge_fwd(f"merge{l}", P, [S["ya"], S["yb"], S["yc"]],
                             [W["w_sc_out"], W["w_sb_out"], W["w_ssm_out"]])
    S["mix"], S["x1"], S["h2"] = _mm_mid(f"w_o{l}", S["merged"], W["w_o"], x_in, V["mid_mix"])
    side, handler = sides.get("ffn_in", (None, None))
    res = _mm_epi(f"ffn_in{l}", S["h2"], W["w_ffn_in"], "nn", 2 * FFN_HALF, [],
                  lambda p: (p, _swiglu(p[:, :FFN_HALF], p[:, FFN_HALF:])),
                  [(2 * FFN_HALF, BF16), (FFN_HALF, BF16)], side=side)
    S["GU"], S["act"] = res[0], res[1]
    if side:
        handler(res[2:])
    if next_vecs is None:
        S["f"] = _mm(f"ffn_out{l}", S["act"], W["w_ffn_out"], "nn", BF16)
    else:
        S["f"], S["x_next"], S["h_next"] = _mm_mid(f"ffn_out{l}", S["act"], W["w_ffn_out"], S["x1"], next_vecs)
    return S


BRANCH_WIDTHS = (SC_WIDTH, 256, SSM_INNER)


def _branch_specs(tm):
    gb = P_G // D_MODEL
    gates = [pl.BlockSpec((tm, D_MODEL), functools.partial(lambda i, cb: (i, cb), cb=gb + k)) for k in range(3)]
    ys = [pl.BlockSpec((tm, w), lambda i: (i, 0)) for w in BRANCH_WIDTHS]
    ws = [pl.BlockSpec((w, D_MODEL), lambda i: (0, 0)) for w in BRANCH_WIDTHS]
    return gates, ys, ws


def _merge_fwd(name, P, ys, ws, tm=512):
    L = P.shape[0]
    tm = min(tm, L)
    gates, y_specs, w_specs = _branch_specs(tm)

    def body(ga, gb, gc, ya, yb, yc, wa, wb, wc, o_ref):
        acc = None
        for g_ref, y_ref, w_ref in ((ga, ya, wa), (gb, yb, wb), (gc, yc, wc)):
            t = jax.nn.sigmoid(_f(g_ref[...])) * jnp.dot(y_ref[...], w_ref[...], preferred_element_type=F32)
            acc = t if acc is None else acc + t
        o_ref[...] = acc.astype(o_ref.dtype)

    return pl.pallas_call(
        body, name=name, grid=(L // tm,), in_specs=gates + y_specs + w_specs,
        out_specs=pl.BlockSpec((tm, D_MODEL), lambda i: (i, 0)), out_shape=jax.ShapeDtypeStruct((L, D_MODEL), BF16),
        compiler_params=_params(("arbitrary",)),
    )(P, P, P, *ys, *ws)


def _merge_bwd(name, P, ys, ws, dmerged, tm=512):
    L = P.shape[0]
    tm = min(tm, L)
    gates, y_specs, w_specs = _branch_specs(tm)

    def body(ga, gb, gc, ya, yb, yc, wa, wb, wc, dm_ref, dg_ref, dya, dyb, dyc, gwa, gwb, gwc):
        @pl.when(pl.program_id(0) == 0)
        def _():
            for r in (gwa, gwb, gwc):
                r[...] = jnp.zeros_like(r)

        dm = _f(dm_ref[...])
        for k, (g_ref, y_ref, w_ref, dy_ref, gw_ref) in enumerate(
                ((ga, ya, wa, dya, gwa), (gb, yb, wb, dyb, gwb), (gc, yc, wc, dyc, gwc))):
            y, w = y_ref[...], w_ref[...]
            s = jax.nn.sigmoid(_f(g_ref[...]))
            proj = jnp.dot(y, w, preferred_element_type=F32)
            d_proj = (dm * s).astype(BF16)
            dg_ref[:, k * D_MODEL:(k + 1) * D_MODEL] = (dm * proj * s * (1.0 - s)).astype(dg_ref.dtype)
            dy_ref[...] = _dot_nt(d_proj, w).astype(dy_ref.dtype)
            gw_ref[...] += _dot_tn(y, d_proj)

    gate_cols = pl.BlockSpec((tm, P_WIDTH - P_G), lambda i: (i, P_G // (P_WIDTH - P_G)))
    return pl.pallas_call(
        body, name=name, grid=(L // tm,),
        in_specs=gates + y_specs + w_specs + [pl.BlockSpec((tm, D_MODEL), lambda i: (i, 0))],
        out_specs=[gate_cols] + y_specs + w_specs,
        out_shape=[jax.ShapeDtypeStruct((L, P_WIDTH), BF16)] + [jax.ShapeDtypeStruct((L, w), BF16) for w in BRANCH_WIDTHS]
        + [jax.ShapeDtypeStruct((w, D_MODEL), F32) for w in BRANCH_WIDTHS],
        compiler_params=_params(("arbitrary",)),
    )(P, P, P, *ys, *ws, dmerged)


def _assemble_dp(name, dP, parts, tl=256):
    L = dP.shape[0]
    tl = min(tl, L)
    n = len(parts)

    def body(*refs):
        o_ref = refs[n + 1]
        o_ref[...] = jnp.concatenate([r[...].astype(o_ref.dtype) for r in refs[:n]], axis=1)

    return pl.pallas_call(
        body, name=name, grid=(L // tl,),
        in_specs=[pl.BlockSpec((tl, a.shape[1]), lambda i: (i, 0)) for a in parts] + [pl.BlockSpec(memory_space=pl.ANY)],
        out_specs=pl.BlockSpec((tl, P_G), lambda i: (i, 0)), out_shape=jax.ShapeDtypeStruct(dP.shape, dP.dtype),
        input_output_aliases={n: 0}, compiler_params=_params(("arbitrary",)),
    )(*parts, dP)


def _layer_bwd(l, S, W, V, dx1, df, sides, landed):
    G = {}
    P = S["P"]

    def mm(key, *args, **kw):
        if key not in sides:
            return _mm(f"{key}{l}", *args, **kw)
        names, layer, make = sides[key]
        res = _mm(f"{key}{l}", *args, side=make(G), **kw)
        for n, a in zip(names, res[1:]):
            landed[(n, layer)] = a
        return res[0]

    G["w_ffn_out"] = _mm(f"gw_ffn_out{l}", S["act"], df, "tn", F32)

    def swiglu_bwd(d_act, gu):
        gt, up = _f(gu[:, :FFN_HALF]), _f(gu[:, FFN_HALF:])
        s = jax.nn.sigmoid(gt)
        gs = gt * s
        return (jnp.concatenate([d_act * up * (s + gs * (1.0 - s)), d_act * gs], axis=1),)

    names, layer, make = sides.get("d_gu", ((), None, None))
    res = _mm_epi(f"d_gu{l}", df, W["w_ffn_out"], "nt", FFN_HALF, [(S["GU"], 2 * FFN_HALF)], swiglu_bwd,
                  [(2 * FFN_HALF, BF16)], side=make(G) if make else None)
    dGU = res[0]
    for n, a in zip(names, res[1:]):
        landed[(n, layer)] = a
    dh2 = mm("d_h2", dGU, W["w_ffn_in"], "nt", BF16)
    G["w_ffn_in"] = _ffn_in_cols(mm("gw_ffn_in", S["h2"], dGU, "tn", F32))
    dx, dmix, G["gate1"], G["g_post_mix"], G["g_pre_ffn"], G["scale2"], G["shift2"] = _mid_bwd(
        f"mid_mix_bwd{l}", S["x_in"], S["mix"], dx1, dh2, V["mid_mix"])
    dmerged = _mm(f"d_merged{l}", dmix, W["w_o"], "nt", BF16)
    G["w_o"] = _mm(f"gw_o{l}", S["merged"], dmix, "tn", F32)

    dP, dya, dyb, dyc, G["w_sc_out"], G["w_sb_out"], G["w_ssm_out"] = _merge_bwd(
        f"merge_bwd{l}", P, [S["ya"], S["yb"], S["yc"]], [W["w_sc_out"], W["w_sb_out"], W["w_ssm_out"]], dmerged)

    def post_bwd(y, px, z, d, dfull, nw):
        _, vjp = jax.vjp(_ssm_post, y, px, _f(z), dfull, nw)
        return vjp(_f(d))

    dy_ssd, dxs, dz, G["d_full"], G["ssm_norm_w"] = _rowwise(
        f"ssm_post_bwd{l}", post_bwd,
        [(S["y_ssd"], SSM_INNER, 0), (S["pre"], SSM_INNER, 0), (P, SSM_INNER, P_Z // SSM_INNER), (dyc, SSM_INNER, 0)],
        [V["d_full"], V["norm_w"]], [(SSM_INNER, F32), (SSM_INNER, F32), (SSM_INNER, BF16)], [(1, SSM_INNER)] * 2)
    dpre, ddt, G["dtb"], G["alog"] = _ssd_bwd(S["pre"], P, S["states"], dy_ssd, dxs, V["dtb"], V["alog"])
    dxbc, w0, w1, w2, w3, G["ssm_conv_b"] = _ssmconv_bwd(P, dpre, V["ssm_w"])
    G["ssm_conv_w"] = jnp.concatenate([w0, w1, w2, w3], axis=0)
    dq, dk, dv = _sb_bwd(P, dyb, S["yb32"])
    dA, s0, s1, s2 = _shortconv_bwd(P, dya, V["sc_w"])
    G["sc_conv_w"] = jnp.concatenate([s0, s1, s2], axis=0)
    dP = _assemble_dp(f"assemble_dp{l}", dP, [dA, dq, dk, dv, dz, ddt, dxbc])
    G["w_in"] = _mm(f"gw_in{l}", S["h"], dP, "tn", F32, tn_cap=1024)
    dh = mm("d_h", dP, W["w_in"], "nt", BF16, tk_cap=3072)
    return dx, dh, G


def kernel(x, c, mod_w, mod_b, g_pre_mix, g_post_mix, g_pre_ffn, g_post_ffn, w_in, sc_conv_w, ssm_conv_w, ssm_conv_b, ssm_dt_bias, ssm_a_log, ssm_d, ssm_norm_w, w_sc_out, w_sb_out, w_ssm_out, w_o, w_ffn_in, w_ffn_out, loss_target, m_mod_w, m_mod_b, m_g_pre_mix, m_g_post_mix, m_g_pre_ffn, m_g_post_ffn, m_w_in, m_sc_conv_w, m_ssm_conv_w, m_ssm_conv_b, m_ssm_dt_bias, m_ssm_a_log, m_ssm_d, m_ssm_norm_w, m_w_sc_out, m_w_sb_out, m_w_ssm_out, m_w_o, m_w_ffn_in, m_w_ffn_out, v_mod_w, v_mod_b, v_g_pre_mix, v_g_post_mix, v_g_pre_ffn, v_g_post_ffn, v_w_in, v_sc_conv_w, v_ssm_conv_w, v_ssm_conv_b, v_ssm_dt_bias, v_ssm_a_log, v_ssm_d, v_ssm_norm_w, v_w_sc_out, v_w_sb_out, v_w_ssm_out, v_w_o, v_w_ffn_in, v_w_ffn_out):
    wts = dict(mod_w=mod_w, mod_b=mod_b, g_pre_mix=g_pre_mix, g_post_mix=g_post_mix, g_pre_ffn=g_pre_ffn,
               g_post_ffn=g_post_ffn, w_in=w_in, sc_conv_w=sc_conv_w, ssm_conv_w=ssm_conv_w, ssm_conv_b=ssm_conv_b,
               ssm_dt_bias=ssm_dt_bias, ssm_a_log=ssm_a_log, ssm_d=ssm_d, ssm_norm_w=ssm_norm_w, w_sc_out=w_sc_out,
               w_sb_out=w_sb_out, w_ssm_out=w_ssm_out, w_o=w_o, w_ffn_in=w_ffn_in, w_ffn_out=w_ffn_out)
    mom = dict(mod_w=m_mod_w, mod_b=m_mod_b, g_pre_mix=m_g_pre_mix, g_post_mix=m_g_post_mix, g_pre_ffn=m_g_pre_ffn,
               g_post_ffn=m_g_post_ffn, w_in=m_w_in, sc_conv_w=m_sc_conv_w, ssm_conv_w=m_ssm_conv_w,
               ssm_conv_b=m_ssm_conv_b, ssm_dt_bias=m_ssm_dt_bias, ssm_a_log=m_ssm_a_log, ssm_d=m_ssm_d,
               ssm_norm_w=m_ssm_norm_w, w_sc_out=m_w_sc_out, w_sb_out=m_w_sb_out, w_ssm_out=m_w_ssm_out, w_o=m_w_o,
               w_ffn_in=m_w_ffn_in, w_ffn_out=m_w_ffn_out)
    var = dict(mod_w=v_mod_w, mod_b=v_mod_b, g_pre_mix=v_g_pre_mix, g_post_mix=v_g_post_mix, g_pre_ffn=v_g_pre_ffn,
               g_post_ffn=v_g_post_ffn, w_in=v_w_in, sc_conv_w=v_sc_conv_w, ssm_conv_w=v_ssm_conv_w,
               ssm_conv_b=v_ssm_conv_b, ssm_dt_bias=v_ssm_dt_bias, ssm_a_log=v_ssm_a_log, ssm_d=v_ssm_d,
               ssm_norm_w=v_ssm_norm_w, w_sc_out=v_w_sc_out, w_sb_out=v_w_sb_out, w_ssm_out=v_w_ssm_out, w_o=v_w_o,
               w_ffn_in=v_w_ffn_in, w_ffn_out=v_w_ffn_out)
    xi, yi, ci = _here()
    chip = 2 * xi + yi
    me = 4 * xi + 2 * yi + ci
    x0, target = x[0], loss_target[0]

    first_shapes = [(D_MODEL,), sc_conv_w.shape, ssm_conv_w.shape]
    g0 = _allgather_small("gather_cond", _pack([c, sc_conv_w, ssm_conv_w]))
    c_rows, sc_sh, ssm_sh = _unpack(g0, first_shapes)
    c_all = c_rows
    sc_w = jnp.concatenate([sc_sh[2 * j] for j in range(N_CHIPS)], axis=-1)
    ssm_w = jnp.concatenate([ssm_sh[2 * j] for j in range(N_CHIPS)], axis=-1)

    mod_b_sh = lax.dynamic_slice_in_dim(mod_b, chip * MOD_SHARD, MOD_SHARD, axis=1).reshape(DEPTH, 1, MOD_SHARD)
    modpart = _mod_fwd(c_all, mod_w, mod_b_sh)
    g1 = _allgather_small("gather_mod", modpart.reshape(-1, LANES)).reshape(N_DEV, DEPTH, N_DEV, MOD_SHARD)
    mod = jnp.concatenate([lax.dynamic_index_in_dim(g1[2 * j], me, axis=1, keepdims=False) for j in range(N_CHIPS)],
                          axis=-1)

    def layer_shards(l):
        return [wts[n][l].astype(BF16) for n, _ in BIG]

    def full_weights(which, gathered):
        W = {n: jnp.concatenate([g[j] for j in range(N_CHIPS)], axis=ax - 1) for (n, ax), g in zip(which, gathered)}
        if "w_in" in W:
            W["w_in"] = _pad_w_in(W["w_in"])
        if "w_ffn_in" in W:
            W["w_ffn_in"] = _ffn_in_cols(W["w_ffn_in"])
        return W

    Ws = [{}, {}]
    fwd_sides = [{"in_proj": (_gather_side(layer_shards(0)[1:]), lambda got: Ws[0].update(full_weights(BIG[1:], got))),
                  "ffn_in": (_gather_side(layer_shards(1)), lambda got: Ws[1].update(full_weights(BIG, got)))}, {}]
    Vs = []
    for l in range(DEPTH):
        sh1, sc1, gt1, sh2, sc2, gt2 = [_row(v) for v in jnp.split(mod[l], N_MOD)]
        Vs.append(dict(
            shift1=sh1, scale1=sc1, g_pre_mix=_row(g_pre_mix[l]),
            mid_mix=[gt1, _row(g_post_mix[l]), _row(g_pre_ffn[l]), sc2, sh2],
            gate2=gt2, g_post_ffn=_row(g_post_ffn[l]),
            sc_w=sc_w[l], ssm_w=ssm_w[l], ssm_b=_row(ssm_conv_b[l]),
            dtb=_row(jnp.pad(ssm_dt_bias[l], (0, LANES - SSM_HEADS))), alog=_row(jnp.pad(ssm_a_log[l], (0, LANES - SSM_HEADS))),
            d_full=_row(jnp.repeat(ssm_d[l], SSM_INNER // SSM_HEADS)), norm_w=_row(ssm_norm_w[l])))

    def mid_ffn_vecs(l):
        return [Vs[l]["gate2"], Vs[l]["g_post_ffn"], Vs[l + 1]["g_pre_mix"], Vs[l + 1]["scale1"], Vs[l + 1]["shift1"]]

    saved = []
    x_in = x0
    h, *got = _first_fwd(x0, [Vs[0]["g_pre_mix"], Vs[0]["scale1"], Vs[0]["shift1"]], _gather_side(layer_shards(0)[:1]))
    Ws[0].update(full_weights(BIG[:1], got))
    for l in range(DEPTH):
        S = _layer_fwd(l, x_in, h, Ws[l], Vs[l], fwd_sides[l], mid_ffn_vecs(l) if l + 1 < DEPTH else None)
        saved.append(S)
        if l + 1 < DEPTH:
            x_in, h = S["x_next"], S["h_next"]

    def pieces(G, names):
        out = []
        for n, ax in BIG:
            if n in names:
                g = _unpad_w_in(G[n]) if n == "w_in" else G[n]
                out.append(jnp.stack(jnp.split(g, N_CHIPS, axis=ax - 1)).astype(BF16))
        return out

    small_names = tuple(n for n, _ in BIG if n not in ("w_in", "w_ffn_in"))
    late_names = tuple(n for n, _ in BIG if n != "w_in")
    landed = {}

    GL = [None] * DEPTH
    S = saved[-1]
    dx1, df, g_gate2, g_gpf, loss_cols = _last_bwd(S["x1"], S["f"], target, [Vs[-1]["gate2"], Vs[-1]["g_post_ffn"]])
    for l in reversed(range(DEPTH)):
        sides = {}
        if l + 1 < DEPTH:
            for key, names in (("d_gu", small_names), ("d_h2", ("w_ffn_in",)), ("gw_ffn_in", ("w_in",))):
                sides[key] = (names, l + 1, lambda G, up=GL[l + 1], names=names: _scatter_side(pieces(up, names)))
        if l == 0:
            sides["d_h"] = (late_names, l, lambda G: _scatter_side(pieces(G, late_names)))
        dx, dh, G = _layer_bwd(l, saved[l], Ws[l], Vs[l], dx1, df, sides, landed)
        G["gate2"], G["g_post_ffn"] = g_gate2, g_gpf
        GL[l] = G
        if l > 0:
            Sp = saved[l - 1]
            dx1, df, g_gate2, g_gpf, G["g_pre_mix"], G["scale1"], G["shift1"] = _mid_bwd(
                f"mid_ffn_bwd{l - 1}", Sp["x1"], Sp["f"], dx, dh, mid_ffn_vecs(l - 1))
        else:
            grad_x, G["g_pre_mix"], G["scale1"], G["shift1"], landed[("w_in", 0)] = _first_bwd(
                x0, dx, dh, [Vs[0]["g_pre_mix"], Vs[0]["scale1"], Vs[0]["shift1"]], _scatter_side(pieces(G, ("w_in",))))
    loss = lax.psum(jnp.sum(loss_cols), ("x", "y", "c"))

    def both(key, shape=None):
        a = jnp.stack([GL[l][key] for l in range(DEPTH)])
        return a if shape is None else a.reshape(shape)

    dmod = jnp.concatenate([both(k, (DEPTH, D_MODEL)) for k in ("shift1", "scale1", "gate1", "shift2", "scale2", "gate2")],
                           axis=1)
    part_small = dict(
        mod_b=dmod, g_pre_mix=both("g_pre_mix", (DEPTH, D_MODEL)), g_post_mix=both("g_post_mix", (DEPTH, D_MODEL)),
        g_pre_ffn=both("g_pre_ffn", (DEPTH, D_MODEL)), g_post_ffn=both("g_post_ffn", (DEPTH, D_MODEL)),
        sc_conv_w=both("sc_conv_w"), ssm_conv_w=both("ssm_conv_w"), ssm_conv_b=both("ssm_conv_b", (DEPTH, SSM_CONV_DIM)),
        ssm_dt_bias=both("dtb", (DEPTH, LANES))[:, :SSM_HEADS], ssm_a_log=both("alog", (DEPTH, LANES))[:, :SSM_HEADS],
        ssm_d=both("d_full", (DEPTH, SSM_HEADS, SSM_INNER // SSM_HEADS)).sum(-1),
        ssm_norm_w=both("ssm_norm_w", (DEPTH, SSM_INNER)))
    small_shapes = [part_small[n].shape for n in SMALL]
    g2 = _allgather_small("gather_small_grads", _pack([part_small[n] for n in SMALL]))
    tot = dict(zip(SMALL, _unpack(_sum_slots("sum_small_grads", g2), small_shapes)))
    dmod_all = _unpack(g2, small_shapes)[0]
    dmod_sh = jnp.swapaxes(lax.dynamic_slice_in_dim(dmod_all, chip * MOD_SHARD, MOD_SHARD, axis=2), 0, 1)
    grads = {"mod_w": _mod_bwd(c_all, dmod_sh)}
    for n in SMALL:
        grads[n] = tot[n]
    grads["sc_conv_w"] = lax.dynamic_slice_in_dim(tot["sc_conv_w"], chip * 64, 64, axis=2)
    grads["ssm_conv_w"] = lax.dynamic_slice_in_dim(tot["ssm_conv_w"], chip * 192, 192, axis=2)

    keys = [(n, l) for n, _ in BIG for l in range(DEPTH)]
    mine = [_sum_slots(f"sum_{n}{l}", landed[(n, l)]) for n, l in keys]
    theirs = dict(zip(keys, _sibling_exchange("swap_core_sums", mine)))
    mine = dict(zip(keys, mine))

    out = {}

    def update(name, w2, gs, m2, v2, shape):
        g, d, nm, nv = _adamw(f"adamw_{name}", w2, gs, m2, v2)
        out[name] = tuple(a.reshape(shape) for a in (g, d, nm, nv))

    for n, _ in BIG:
        shp = wts[n].shape
        two = (-1, shp[-1])
        by_layer = [tuple(src[(n, l)] for l in range(DEPTH)) for src in (mine, theirs)]
        update(n, wts[n].reshape(two), by_layer, mom[n].reshape(two), var[n].reshape(two), shp)
    two = (-1, MOD_SHARD)
    update("mod_w", mod_w.reshape(two), [grads["mod_w"].reshape(two)], m_mod_w.reshape(two), v_mod_w.reshape(two), mod_w.shape)
    shapes = [wts[n].shape for n in SMALL]
    res = _adamw("adamw_small", _pack([wts[n] for n in SMALL]), [_pack([grads[n] for n in SMALL])],
                 _pack([mom[n] for n in SMALL]), _pack([var[n] for n in SMALL]))
    for n, g, d, nm, nv in zip(SMALL, *[_unpack(r, shapes) for r in res]):
        out[n] = (g, d, nm, nv)

    result = [loss, grad_x[None]]
    for k in range(4):
        result += [out[n][k] for n in WEIGHT_ORDER]
    return tuple(result)
```

```python
import functools

import jax
import jax.numpy as jnp
from jax import lax
from jax.experimental import pallas as pl
from jax.experimental.pallas import tpu as pltpu

F32 = jnp.float32
BF16 = jnp.bfloat16
HIGHEST = lax.Precision.HIGHEST
MESH_ID = pl.DeviceIdType.MESH

D_MODEL = 1024
DEPTH = 2
SC_WIDTH = 256
SB_HEAD_DIM = 64
SSM_INNER = 512
SSM_HEADS = 8
SSM_STATE = 64
SSM_CONV = 4
SSM_CHUNK = 256
SSM_CONV_DIM = 768
FFN_HIDDEN = 2816
NORM_EPS = 1e-6
N_MOD = 6
N_CHIPS = 4
N_DEV = 8

ADAM_LR = 0.001
ADAM_B1 = 0.9
ADAM_B2 = 0.999
ADAM_EPS = 1e-08
ADAM_WD = 0.01
ADAM_STEP = 10

P_WIDTH = 6144
P_A, P_B, P_Z, P_DT, P_XBC, P_G = 0, 768, 1536, 2048, 2304, 3072
DT_PAD = 256

VMEM_LIMIT_BYTES = 56 * 1024 * 1024
LANES = 128

SB_LOG_CUTOFF = -105.0
SB_TQ = 256
SB_TK = 256
SB_SUBS = 2
SB_STRAIGHT = 2


def _params(sem):
    return pltpu.CompilerParams(dimension_semantics=sem, vmem_limit_bytes=VMEM_LIMIT_BYTES)


def _pick(n, cap):
    if n <= cap:
        return n
    best = None
    for m in range(LANES, cap + 1, LANES):
        if n % m == 0:
            best = m
    assert best is not None, (n, cap)
    return best


def _rowwise(name, fn, rows, vecs, row_outs, acc_outs=(), tl=256, side=None):
    L = rows[0][0].shape[0]
    tl = min(tl, L)
    assert L % tl == 0
    n_in = len(rows) + len(vecs)
    n_ro, n_ao = len(row_outs), len(acc_outs)
    n_si = len(side.arrays) if side else 0
    n_so = len(side.out_shapes) if side else 0

    def body(*refs):
        ins, s_in = refs[:n_in], refs[n_in:n_in + n_si]
        outs = refs[n_in + n_si:]
        ro, ao, s_out, sems = outs[:n_ro], outs[n_ro:n_ro + n_ao], outs[n_ro + n_ao:n_ro + n_ao + n_so], outs[n_ro + n_ao + n_so:]
        if side:
            @pl.when(pl.program_id(0) == 0)
            def _():
                side.start(s_in, s_out, sems)

        _rows(ins, ro, ao)
        if side:
            @pl.when(pl.program_id(0) == L // tl - 1)
            def _():
                side.finish(s_in, s_out, sems)

    def _rows(ins, ro, ao):
        vals = fn(*[r[...] for r in ins])
        if not isinstance(vals, (tuple, list)):
            vals = (vals,)
        for o, v in zip(ro, vals[:n_ro]):
            o[...] = v.astype(o.dtype)
        if ao:
            @pl.when(pl.program_id(0) == 0)
            def _():
                for o in ao:
                    o[...] = jnp.zeros_like(o)
            for o, v in zip(ao, vals[n_ro:]):
                o[...] += v.astype(F32)

    in_specs = [pl.BlockSpec((tl, w), functools.partial(lambda i, cb: (i, cb), cb=cb)) for _, w, cb in rows]
    in_specs += [pl.BlockSpec(v.shape, lambda i: (0, 0)) for v in vecs]
    out_specs = [pl.BlockSpec((tl, w), lambda i: (i, 0)) for w, _ in row_outs]
    out_specs += [pl.BlockSpec(s, lambda i: (0, 0)) for s in acc_outs]
    out_shape = [jax.ShapeDtypeStruct((L, w), dt) for w, dt in row_outs]
    out_shape += [jax.ShapeDtypeStruct(s, F32) for s in acc_outs]
    any_spec = pl.BlockSpec(memory_space=pl.ANY)
    return pl.pallas_call(
        body, name=name, grid=(L // tl,), in_specs=in_specs + [any_spec] * n_si, out_specs=out_specs + [any_spec] * n_so,
        out_shape=out_shape + (side.out_shapes if side else []), scratch_shapes=side.scratch if side else [],
        compiler_params=_params(("arbitrary",)),
    )(*[a for a, _, _ in rows], *vecs, *(side.arrays if side else []))


def _mm(name, a, b, mode, out_dtype, tm=1024, tn_cap=1408, tk_cap=2816, side=None):
    if mode == "nn":
        (M, K), (_, N) = a.shape, b.shape
    elif mode == "nt":
        (M, K), (N, _) = a.shape, b.shape
    else:
        (K, M), (_, N) = a.shape, b.shape
        tm, tk_cap = 1408, 2048
    tm = _pick(M, tm)
    tn = _pick(N, tn_cap)
    tk = _pick(K, tk_cap)
    nk = K // tk
    grid = (M // tm, N // tn, nk)
    n_si = len(side.arrays) if side else 0
    n_so = len(side.out_shapes) if side else 0
    n_acc = 1 if nk > 1 else 0

    def body(a_ref, b_ref, *rest):
        s_in, o_ref, s_out = rest[:n_si], rest[n_si], rest[n_si + 1:n_si + 1 + n_so]
        scr = rest[n_si + 1 + n_so:]
        if side:
            at = [pl.program_id(d) for d in range(3)]
            is_first = jnp.logical_and(jnp.logical_and(at[0] == 0, at[1] == 0), at[2] == 0)
            is_last = jnp.logical_and(jnp.logical_and(at[0] == grid[0] - 1, at[1] == grid[1] - 1), at[2] == grid[2] - 1)

            @pl.when(is_first)
            def _():
                side.start(s_in, s_out, scr[n_acc:])

        _product(a_ref, b_ref, o_ref, scr)
        if side:
            @pl.when(is_last)
            def _():
                side.finish(s_in, s_out, scr[n_acc:])

    def _product(a_ref, b_ref, o_ref, scr):
        if mode == "nn":
            p = jnp.dot(a_ref[...], b_ref[...], preferred_element_type=F32)
        elif mode == "nt":
            p = lax.dot_general(a_ref[...], b_ref[...], (((1,), (1,)), ((), ())), preferred_element_type=F32)
        else:
            p = lax.dot_general(a_ref[...], b_ref[...], (((0,), (0,)), ((), ())), preferred_element_type=F32)
        if nk == 1:
            o_ref[...] = p.astype(o_ref.dtype)
        else:
            acc = scr[0]
            k = pl.program_id(2)

            @pl.when(k == 0)
            def _():
                acc[...] = p

            @pl.when(k > 0)
            def _():
                acc[...] += p

            @pl.when(k == nk - 1)
            def _():
                o_ref[...] = acc[...].astype(o_ref.dtype)

    if mode == "nn":
        a_spec = pl.BlockSpec((tm, tk), lambda i, j, k: (i, k))
        b_spec = pl.BlockSpec((tk, tn), lambda i, j, k: (k, j))
    elif mode == "nt":
        a_spec = pl.BlockSpec((tm, tk), lambda i, j, k: (i, k))
        b_spec = pl.BlockSpec((tn, tk), lambda i, j, k: (j, k))
    else:
        a_spec = pl.BlockSpec((tk, tm), lambda i, j, k: (k, i))
        b_spec = pl.BlockSpec((tk, tn), lambda i, j, k: (k, j))
    any_spec = pl.BlockSpec(memory_space=pl.ANY)
    res = pl.pallas_call(
        body, name=name, grid=grid, in_specs=[a_spec, b_spec] + [any_spec] * n_si,
        out_specs=[pl.BlockSpec((tm, tn), lambda i, j, k: (i, j))] + [any_spec] * n_so,
        out_shape=[jax.ShapeDtypeStruct((M, N), out_dtype)] + (side.out_shapes if side else []),
        scratch_shapes=([pltpu.VMEM((tm, tn), F32)] if nk > 1 else []) + (side.scratch if side else []),
        compiler_params=_params(("arbitrary", "arbitrary", "arbitrary")),
    )(a, b, *(side.arrays if side else []))
    return res if side else res[0]


def _mm_epi(name, a, b, mode, tn, extras, epi, outs, tm=512, side=None):
    if mode == "nn":
        (M, K), (_, N) = a.shape, b.shape
    else:
        (M, K), (N, _) = a.shape, b.shape
    tm = _pick(M, tm)
    grid = (N // tn, M // tm)
    n_ex, n_out = len(extras), len(outs)
    n_si = len(side.arrays) if side else 0
    n_so = len(side.out_shapes) if side else 0

    def body(*refs):
        a_ref, b_ref, ex = refs[0], refs[1], refs[2:2 + n_ex]
        s_in = refs[2 + n_ex:2 + n_ex + n_si]
        o_refs = refs[2 + n_ex + n_si:2 + n_ex + n_si + n_out]
        s_out = refs[2 + n_ex + n_si + n_out:2 + n_ex + n_si + n_out + n_so]
        sems = refs[2 + n_ex + n_si + n_out + n_so:]
        if side:
            @pl.when(jnp.logical_and(pl.program_id(0) == 0, pl.program_id(1) == 0))
            def _():
                side.start(s_in, s_out, sems)

        if mode == "nn":
            p = jnp.dot(a_ref[...], b_ref[...], preferred_element_type=F32)
        else:
            p = lax.dot_general(a_ref[...], b_ref[...], (((1,), (1,)), ((), ())), preferred_element_type=F32)
        for o, v in zip(o_refs, epi(p, *[r[...] for r in ex])):
            o[...] = v.astype(o.dtype)
        if side:
            @pl.when(jnp.logical_and(pl.program_id(0) == grid[0] - 1, pl.program_id(1) == grid[1] - 1))
            def _():
                side.finish(s_in, s_out, sems)

    any_spec = pl.BlockSpec(memory_space=pl.ANY)
    a_spec = pl.BlockSpec((tm, K), lambda j, i: (i, 0))
    b_spec = pl.BlockSpec((K, tn), lambda j, i: (0, j)) if mode == "nn" else pl.BlockSpec((tn, K), lambda j, i: (j, 0))
    return pl.pallas_call(
        body, name=name, grid=grid,
        in_specs=[a_spec, b_spec]
        + [pl.BlockSpec(e.shape, lambda j, i: (0, 0)) if w is None else pl.BlockSpec((tm, w), lambda j, i: (i, j))
           for e, w in extras] + [any_spec] * n_si,
        out_specs=[pl.BlockSpec((tm, w), lambda j, i: (i, j)) for w, _ in outs] + [any_spec] * n_so,
        out_shape=[jax.ShapeDtypeStruct((M, (N // tn) * w), dt) for w, dt in outs] + (side.out_shapes if side else []),
        scratch_shapes=side.scratch if side else [],
        compiler_params=_params(("arbitrary", "arbitrary")),
    )(a, b, *[e for e, _ in extras], *(side.arrays if side else []))


def _f(x):
    return x.astype(F32)


def _silu(x):
    return x * jax.nn.sigmoid(x)


def _softplus(x):
    return jnp.maximum(x, 0.0) + jnp.log1p(jnp.exp(-jnp.abs(x)))


def _rms(x, g):
    r = lax.rsqrt(jnp.mean(x * x, axis=-1, keepdims=True) + NORM_EPS)
    return x * r * g


def _adaln(x, g, scale, shift):
    return _rms(x, g) * (1.0 + scale) + shift


def _resid(x, y, gate, g):
    return x + gate * _rms(y, g)


def _mid(x, y, gate, g_post, g_pre, scale, shift):
    x_new = _resid(x, y, gate, g_post)
    return x_new, _adaln(x_new, g_pre, scale, shift)


def _swiglu(gt, up):
    return _silu(gt) * up


def _ssm_post(y_ssd, pre_xs, z, d_full, norm_w):
    y = (y_ssd + _silu(pre_xs) * d_full) * _silu(z)
    half = SSM_INNER // 2
    parts = []
    for g in range(2):
        yg = y[:, g * half:(g + 1) * half]
        parts.append(yg * lax.rsqrt(jnp.mean(yg * yg, axis=-1, keepdims=True) + NORM_EPS))
    return jnp.concatenate(parts, axis=1) * norm_w


def _first_fwd(x, vecs, side=None):
    return _rowwise("adaln_first", lambda x, g, sc, sh: _adaln(x, g, sc, sh),
                    [(x, D_MODEL, 0)], vecs, [(D_MODEL, BF16)], tl=512, side=side)


def _mid_bwd(name, x, y, dx_new, dh, vecs):
    def fn(x, y, dxn, dh, gate, g_post, g_pre, scale, shift):
        y, dh = _f(y), _f(dh)
        r_y = lax.rsqrt(jnp.mean(y * y, axis=-1, keepdims=True) + NORM_EPS)
        n_y = y * r_y
        both = gate * g_post
        x_new = x + n_y * both
        r_x = lax.rsqrt(jnp.mean(x_new * x_new, axis=-1, keepdims=True) + NORM_EPS)
        u = x_new * r_x
        col_p = jnp.sum(dh * u, axis=0, keepdims=True)
        du = dh * (g_pre * (1.0 + scale))
        dxt = dxn + r_x * (du - u * jnp.mean(du * u, axis=-1, keepdims=True))
        col_q = jnp.sum(dxt * n_y, axis=0, keepdims=True)
        dn = dxt * both
        dy = r_y * (dn - n_y * jnp.mean(dn * n_y, axis=-1, keepdims=True))
        return (dxt, dy, col_q * g_post, col_q * gate, col_p * (1.0 + scale), col_p * g_pre,
                jnp.sum(dh, axis=0, keepdims=True))

    vec = (1, D_MODEL)
    return _rowwise(name, fn, [(x, D_MODEL, 0), (y, D_MODEL, 0), (dx_new, D_MODEL, 0), (dh, D_MODEL, 0)], vecs,
                    [(D_MODEL, F32), (D_MODEL, BF16)], [vec] * 5, tl=512)


def _first_bwd(x, dx_in, dh, vecs, side=None):
    def fn(x, dxi, dh, g, scale, shift):
        dh = _f(dh)
        r = lax.rsqrt(jnp.mean(x * x, axis=-1, keepdims=True) + NORM_EPS)
        u = x * r
        col_p = jnp.sum(dh * u, axis=0, keepdims=True)
        du = dh * (g * (1.0 + scale))
        dx = dxi + r * (du - u * jnp.mean(du * u, axis=-1, keepdims=True))
        return dx, col_p * (1.0 + scale), col_p * g, jnp.sum(dh, axis=0, keepdims=True)

    vec = (1, D_MODEL)
    return _rowwise("adaln_first_bwd", fn, [(x, D_MODEL, 0), (dx_in, D_MODEL, 0), (dh, D_MODEL, 0)], vecs,
                    [(D_MODEL, F32)], [vec] * 3, tl=512, side=side)


def _last_bwd(x1, f, target, vecs):
    def fn(x1, f, t, gate, g):
        f = _f(f)
        r = lax.rsqrt(jnp.mean(f * f, axis=-1, keepdims=True) + NORM_EPS)
        n = f * r
        both = gate * g
        err = x1 + n * both - t
        d = err * (1.0 / D_MODEL)
        col_q = jnp.sum(d * n, axis=0, keepdims=True)
        dn = d * both
        df = r * (dn - n * jnp.mean(dn * n, axis=-1, keepdims=True))
        loss_cols = jnp.sum(err * err, axis=0, keepdims=True) * (0.5 / D_MODEL)
        return d, df, col_q * g, col_q * gate, loss_cols

    vec = (1, D_MODEL)
    return _rowwise("loss_last_bwd", fn, [(x1, D_MODEL, 0), (f, D_MODEL, 0), (target, D_MODEL, 0)], vecs,
                    [(D_MODEL, F32), (D_MODEL, BF16)], [vec] * 3, tl=512)


HALO = 16


def _shift_down(u, prev, k):
    rows = lax.broadcasted_iota(jnp.int32, u.shape, 0)
    v = pltpu.roll(u, k, 0)
    for t in range(k):
        v = jnp.where(rows == t, prev[HALO - k + t:HALO - k + t + 1, :], v)
    return v


def _shift_up(u, nxt, k):
    n = u.shape[0]
    rows = lax.broadcasted_iota(jnp.int32, u.shape, 0)
    v = pltpu.roll(u, n - k, 0)
    for t in range(k):
        v = jnp.where(rows == n - k + t, nxt[t:t + 1, :], v)
    return v


def _conv_specs(L, tl, width, col_block):
    per = tl // HALO
    last = L // HALO - 1
    main = pl.BlockSpec((tl, width), lambda i: (i, col_block))
    before = pl.BlockSpec((HALO, width), lambda i: (jnp.maximum(i * per - 1, 0), col_block))
    after = pl.BlockSpec((HALO, width), lambda i: (jnp.minimum((i + 1) * per, last), col_block))
    return main, before, after


def _shortconv_fwd(P, w, tl=512):
    L = P.shape[0]
    tl = min(tl, L)
    C = SC_WIDTH
    main, before, _ = _conv_specs(L, tl, 3 * C, 0)

    def body(p_ref, h_ref, w_ref, o_ref):
        first = (pl.program_id(0) == 0)
        p, h = _f(p_ref[...]), _f(h_ref[...])
        b, u = p[:, :C], p[:, C:2 * C] * p[:, 2 * C:]
        uh = jnp.where(first, 0.0, h[:, C:2 * C] * h[:, 2 * C:])
        wv = w_ref[...]
        cv = wv[2:3] * u + wv[1:2] * _shift_down(u, uh, 1) + wv[0:1] * _shift_down(u, uh, 2)
        o_ref[...] = (b * cv).astype(o_ref.dtype)

    return pl.pallas_call(
        body, name="shortconv_fwd", grid=(L // tl,),
        in_specs=[main, before, pl.BlockSpec(w.shape, lambda i: (0, 0))],
        out_specs=pl.BlockSpec((tl, C), lambda i: (i, 0)),
        out_shape=jax.ShapeDtypeStruct((L, C), BF16), compiler_params=_params(("arbitrary",)),
    )(P, P, w)


def _shortconv_bwd(P, dya, w, tl=512):
    L = P.shape[0]
    tl = min(tl, L)
    C = SC_WIDTH
    main, before, after = _conv_specs(L, tl, 3 * C, 0)
    dmain, _, dafter = _conv_specs(L, tl, C, 0)
    n = L // tl

    def body(p_ref, h_ref, n_ref, d_ref, dn_ref, w_ref, o_ref, dw0, dw1, dw2):
        i = pl.program_id(0)
        p, h, nx = _f(p_ref[...]), _f(h_ref[...]), _f(n_ref[...])
        b, c, x = p[:, :C], p[:, C:2 * C], p[:, 2 * C:]
        u = c * x
        uh = jnp.where(i == 0, 0.0, h[:, C:2 * C] * h[:, 2 * C:])
        u1, u2 = _shift_down(u, uh, 1), _shift_down(u, uh, 2)
        wv = w_ref[...]
        cv = wv[2:3] * u + wv[1:2] * u1 + wv[0:1] * u2
        dy = _f(d_ref[...])
        dcv = dy * b
        dcv_n = jnp.where(i == n - 1, 0.0, _f(dn_ref[...]) * nx[:, :C])
        du = wv[2:3] * dcv + wv[1:2] * _shift_up(dcv, dcv_n, 1) + wv[0:1] * _shift_up(dcv, dcv_n, 2)
        o_ref[:, :C] = (dy * cv).astype(o_ref.dtype)
        o_ref[:, C:2 * C] = (du * x).astype(o_ref.dtype)
        o_ref[:, 2 * C:] = (du * c).astype(o_ref.dtype)

        @pl.when(i == 0)
        def _():
            for r in (dw0, dw1, dw2):
                r[...] = jnp.zeros_like(r)

        dw0[...] += jnp.sum(dcv * u2, axis=0, keepdims=True)
        dw1[...] += jnp.sum(dcv * u1, axis=0, keepdims=True)
        dw2[...] += jnp.sum(dcv * u, axis=0, keepdims=True)

    vec = pl.BlockSpec((1, C), lambda i: (0, 0))
    return pl.pallas_call(
        body, name="shortconv_bwd", grid=(n,),
        in_specs=[main, before, after, dmain, dafter, pl.BlockSpec(w.shape, lambda i: (0, 0))],
        out_specs=[pl.BlockSpec((tl, 3 * C), lambda i: (i, 0)), vec, vec, vec],
        out_shape=[jax.ShapeDtypeStruct((L, 3 * C), BF16)] + [jax.ShapeDtypeStruct((1, C), F32)] * 3,
        compiler_params=_params(("arbitrary",)),
    )(P, P, P, dya, dya, w)


def _ssmconv_fwd(P, w, bias, tl=512):
    L = P.shape[0]
    tl = min(tl, L)
    C = SSM_CONV_DIM
    main, before, _ = _conv_specs(L, tl, C, P_XBC // C)

    def body(p_ref, h_ref, w_ref, b_ref, o_ref):
        u = _f(p_ref[...])
        uh = jnp.where(pl.program_id(0) == 0, 0.0, _f(h_ref[...]))
        wv = w_ref[...]
        acc = wv[3:4] * u + b_ref[...]
        for k in range(1, SSM_CONV):
            acc = acc + wv[3 - k:4 - k] * _shift_down(u, uh, k)
        o_ref[...] = acc

    return pl.pallas_call(
        body, name="ssmconv_fwd", grid=(L // tl,),
        in_specs=[main, before, pl.BlockSpec(w.shape, lambda i: (0, 0)), pl.BlockSpec(bias.shape, lambda i: (0, 0))],
        out_specs=pl.BlockSpec((tl, C), lambda i: (i, 0)),
        out_shape=jax.ShapeDtypeStruct((L, C), F32), compiler_params=_params(("arbitrary",)),
    )(P, P, w, bias)


def _ssmconv_bwd(P, dpre, w, tl=512):
    L = P.shape[0]
    tl = min(tl, L)
    C = SSM_CONV_DIM
    main, before, _ = _conv_specs(L, tl, C, P_XBC // C)
    dmain, _, dafter = _conv_specs(L, tl, C, 0)
    n = L // tl

    def body(p_ref, h_ref, d_ref, dn_ref, w_ref, o_ref, dw0, dw1, dw2, dw3, db):
        i = pl.program_id(0)
        u = _f(p_ref[...])
        uh = jnp.where(i == 0, 0.0, _f(h_ref[...]))
        d = d_ref[...]
        dn = jnp.where(i == n - 1, 0.0, dn_ref[...])
        wv = w_ref[...]
        du = wv[3:4] * d
        for k in range(1, SSM_CONV):
            du = du + wv[3 - k:4 - k] * _shift_up(d, dn, k)
        o_ref[...] = du.astype(o_ref.dtype)

        @pl.when(i == 0)
        def _():
            for r in (dw0, dw1, dw2, dw3, db):
                r[...] = jnp.zeros_like(r)

        for k, r in ((3, dw0), (2, dw1), (1, dw2)):
            r[...] += jnp.sum(d * _shift_down(u, uh, k), axis=0, keepdims=True)
        dw3[...] += jnp.sum(d * u, axis=0, keepdims=True)
        db[...] += jnp.sum(d, axis=0, keepdims=True)

    vec = pl.BlockSpec((1, C), lambda i: (0, 0))
    return pl.pallas_call(
        body, name="ssmconv_bwd", grid=(n,),
        in_specs=[main, before, dmain, dafter, pl.BlockSpec(w.shape, lambda i: (0, 0))],
        out_specs=[pl.BlockSpec((tl, C), lambda i: (i, 0))] + [vec] * 5,
        out_shape=[jax.ShapeDtypeStruct((L, C), BF16)] + [jax.ShapeDtypeStruct((1, C), F32)] * 5,
        compiler_params=_params(("arbitrary",)),
    )(P, P, dpre, dpre, w)


def _dot_nt(a, b):
    return lax.dot_general(a, b, (((1,), (1,)), ((), ())), preferred_element_type=F32)


def _dot_tn(a, b):
    return lax.dot_general(a, b, (((0,), (0,)), ((), ())), preferred_element_type=F32)


def _split3(x):
    hi = x.astype(BF16)
    r = x - hi.astype(F32)
    mid = r.astype(BF16)
    return hi, mid, (r - mid.astype(F32)).astype(BF16)


@jax.custom_vjp
def _xm01(x, m):
    return sum(jnp.dot(t, m, preferred_element_type=F32) for t in _split3(x))


def _xm01_fwd(x, m):
    return _xm01(x, m), m


def _xm01_bwd(m, g):
    return sum(_dot_nt(t, m) for t in _split3(g)), jnp.zeros_like(m)


_xm01.defvjp(_xm01_fwd, _xm01_bwd)


@jax.custom_vjp
def _m01x(m, x):
    return sum(jnp.dot(m, t, preferred_element_type=F32) for t in _split3(x))


def _m01x_fwd(m, x):
    return _m01x(m, x), m


def _m01x_bwd(m, g):
    return jnp.zeros_like(m), sum(_dot_tn(m, t) for t in _split3(g))


_m01x.defvjp(_m01x_fwd, _m01x_bwd)


def _ssd_chunk(pre, dtr, s_prev, dtb, alog):
    T = pre.shape[0]
    act = _silu(pre)
    xs, bm, cm = act[:, :SSM_INNER], act[:, SSM_INNER:SSM_INNER + 128], act[:, SSM_INNER + 128:]
    lane = lax.broadcasted_iota(jnp.int32, (1, LANES), 1)
    dt = jnp.where(lane < SSM_HEADS, _softplus(dtr + dtb), 0.0)
    a = dt * (-jnp.exp(alog))
    ri = lax.broadcasted_iota(jnp.int32, (T, T), 0)
    ci = lax.broadcasted_iota(jnp.int32, (T, T), 1)
    causal = ci <= ri
    a_cs = _m01x(causal.astype(BF16), a)
    eh = lax.broadcasted_iota(jnp.int32, (LANES, SSM_INNER), 0)
    ej = lax.broadcasted_iota(jnp.int32, (LANES, SSM_INNER), 1)
    expand = (lax.shift_right_logical(ej, 6) == eh).astype(BF16)
    dt_full = _xm01(dt, expand)
    acs_full = _xm01(a_cs, expand)
    alast_full = acs_full[T - 1:T, :]
    xdt = xs * dt_full
    a_cs_t = a_cs.T
    ys, s_new = [], []
    for g in range(2):
        in_group = lax.shift_right_logical(lane, 6) == g
        cg = jnp.where(in_group, cm, 0.0).astype(BF16)
        bg = jnp.where(in_group, bm, 0.0).astype(BF16)
        scores = _dot_nt(cg, bg)
        for pp in range(2):
            hp = 2 * g + pp
            cols = slice(hp * LANES, (hp + 1) * LANES)
            xp, acsp = xdt[:, cols], acs_full[:, cols]
            per_head = []
            for hh in range(2):
                h = 2 * hp + hh
                decay = jnp.exp(jnp.where(causal, a_cs[:, h:h + 1] - a_cs_t[h:h + 1, :], -jnp.inf))
                per_head.append(jnp.dot((scores * decay).astype(BF16), xp.astype(BF16), preferred_element_type=F32))
            y_diag = jnp.where(lane < SSM_STATE, per_head[0], per_head[1])
            sp = s_prev[hp * LANES:(hp + 1) * LANES, :]
            y_off = jnp.dot(cg, sp.astype(BF16), preferred_element_type=F32) * jnp.exp(acsp)
            ys.append(y_diag + y_off)
            to_end = jnp.exp(alast_full[:, cols] - acsp)
            s_new.append(sp * jnp.exp(alast_full[:, cols]) + _dot_tn(bg, (xp * to_end).astype(BF16)))
    return jnp.concatenate(ys, axis=1), jnp.concatenate(s_new, axis=0)


def _ssd_fwd(pre, P, dtb, alog):
    L = pre.shape[0]
    T = min(SSM_CHUNK, L)
    nc = L // T

    def body(pre_ref, dt_ref, dtb_ref, al_ref, y_ref, st_ref, s_scr):
        @pl.when(pl.program_id(0) == 0)
        def _():
            s_scr[...] = jnp.zeros_like(s_scr)

        st_ref[0] = s_scr[...]
        y, s = _ssd_chunk(pre_ref[...], _f(dt_ref[...]), s_scr[...], dtb_ref[...], al_ref[...])
        y_ref[...] = y
        s_scr[...] = s

    vec = pl.BlockSpec((1, LANES), lambda i: (0, 0))
    return pl.pallas_call(
        body, name="ssd_fwd", grid=(nc,),
        in_specs=[pl.BlockSpec((T, SSM_CONV_DIM), lambda i: (i, 0)), pl.BlockSpec((T, LANES), lambda i: (i, P_DT // LANES)),
                  vec, vec],
        out_specs=[pl.BlockSpec((T, SSM_INNER), lambda i: (i, 0)), pl.BlockSpec((1, 512, LANES), lambda i: (i, 0, 0))],
        out_shape=[jax.ShapeDtypeStruct((L, SSM_INNER), F32), jax.ShapeDtypeStruct((nc, 512, LANES), F32)],
        scratch_shapes=[pltpu.VMEM((512, LANES), F32)], compiler_params=_params(("arbitrary",)),
    )(pre, P, dtb, alog)


def _ssd_bwd(pre, P, states, dy, dxs_extra, dtb, alog):
    L = pre.shape[0]
    T = min(SSM_CHUNK, L)
    nc = L // T

    def body(pre_ref, dt_ref, st_ref, dy_ref, dx_ref, dtb_ref, al_ref, dpre_ref, ddt_ref, ddtb_ref, dal_ref, ds_scr):
        @pl.when(pl.program_id(0) == 0)
        def _():
            ds_scr[...] = jnp.zeros_like(ds_scr)
            ddtb_ref[...] = jnp.zeros_like(ddtb_ref)
            dal_ref[...] = jnp.zeros_like(dal_ref)

        _, vjp = jax.vjp(_ssd_chunk, pre_ref[...], _f(dt_ref[...]), st_ref[0], dtb_ref[...], al_ref[...])
        dpre, ddt, ds, ddtb, dal = vjp((dy_ref[...], ds_scr[...]))
        dpre_ref[:, :SSM_INNER] = dpre[:, :SSM_INNER] + dx_ref[...]
        dpre_ref[:, SSM_INNER:] = dpre[:, SSM_INNER:]
        ddt_ref[:, :LANES] = ddt.astype(ddt_ref.dtype)
        ddt_ref[:, LANES:] = jnp.zeros((T, DT_PAD - LANES), ddt_ref.dtype)
        ds_scr[...] = ds
        ddtb_ref[...] += ddtb
        dal_ref[...] += dal

    vec = pl.BlockSpec((1, LANES), lambda i: (0, 0))
    rev = lambda i: (nc - 1 - i, 0)
    return pl.pallas_call(
        body, name="ssd_bwd", grid=(nc,),
        in_specs=[pl.BlockSpec((T, SSM_CONV_DIM), rev), pl.BlockSpec((T, LANES), lambda i: (nc - 1 - i, P_DT // LANES)),
                  pl.BlockSpec((1, 512, LANES), lambda i: (nc - 1 - i, 0, 0)),
                  pl.BlockSpec((T, SSM_INNER), rev), pl.BlockSpec((T, SSM_INNER), rev), vec, vec],
        out_specs=[pl.BlockSpec((T, SSM_CONV_DIM), rev), pl.BlockSpec((T, DT_PAD), rev), vec, vec],
        out_shape=[jax.ShapeDtypeStruct((L, SSM_CONV_DIM), F32), jax.ShapeDtypeStruct((L, DT_PAD), BF16),
                   jax.ShapeDtypeStruct((1, LANES), F32), jax.ShapeDtypeStruct((1, LANES), F32)],
        scratch_shapes=[pltpu.VMEM((512, LANES), F32)], compiler_params=_params(("arbitrary",)),
    )(pre, P, states, dy, dxs_extra, dtb, alog)


def _sb_scores(qm, kb, later, strict, mask):
    z = _dot_nt(qm, kb)
    lk = jnp.minimum(-z, 0.0) - jnp.log(1.0 + jnp.exp(-jnp.abs(z)))
    if mask is not None:
        lk = jnp.where(mask, lk, 0.0)
    log_a = z + lk + jnp.dot(lk.astype(BF16), strict, preferred_element_type=F32) + later
    if mask is not None:
        log_a = jnp.where(mask, log_a, -jnp.inf)
    return z, lk, log_a


def _dot_split(x, m):
    hi = x.astype(BF16)
    lo = (x - hi.astype(F32)).astype(BF16)
    return jnp.dot(hi, m, preferred_element_type=F32) + jnp.dot(lo, m, preferred_element_type=F32)


def _sb_setup(q_ref, i, tq, tk):
    lane = lax.broadcasted_iota(jnp.int32, (1, LANES), 1)
    first = lane < SB_HEAD_DIM
    q = q_ref[...] * (SB_HEAD_DIM ** -0.5)
    qms = (jnp.where(first, q, jnp.zeros_like(q)), jnp.where(first, jnp.zeros_like(q), q))
    j0 = lax.div(i * tq, tk)
    ri = lax.broadcasted_iota(jnp.int32, (tq, tk), 0)
    ci = lax.broadcasted_iota(jnp.int32, (tq, tk), 1)
    diag_mask = (ci + (j0 * tk - i * tq)) < ri
    kr = lax.broadcasted_iota(jnp.int32, (tk, tk), 0)
    kc = lax.broadcasted_iota(jnp.int32, (tk, tk), 1)
    strict = (kr > kc).astype(BF16)
    return first, qms, j0, diag_mask, strict


def _sb_continue(c):
    return jnp.logical_and(c[0] >= 0, jnp.maximum(jnp.max(c[1][0]), jnp.max(c[1][1])) > SB_LOG_CUTOFF)


def _sb_fwd(P):
    L = P.shape[0]
    tq, tk = min(SB_TQ, L), min(SB_TK, L)
    nq = L // tq
    qb = P_B // LANES

    subs = SB_SUBS if L % (SB_SUBS * tq) == 0 else 1

    def body(q_ref, k_ref, v_ref, o_ref, of_ref):
        zero, zacc = jnp.zeros((tq, 1), F32), jnp.zeros((tq, LANES), F32)
        walks = []
        for s in range(subs):
            rows = pl.ds(s * tq, tq)
            first, qms, j0, diag_mask, strict = _sb_setup(q_ref.at[rows, :], pl.program_id(1) * subs + s, tq, tk)

            def tile(h, j, later, acc, mask=None, valid=None, qms=qms, strict=strict):
                off = pl.multiple_of(j * tk, tk)
                gate = later if valid is None else jnp.where(valid, later, -jnp.inf)
                _, lk, log_a = _sb_scores(qms[h], k_ref[pl.ds(off, tk), :], gate, strict, mask)
                acc = acc + jnp.dot(jnp.exp(log_a).astype(BF16), v_ref[pl.ds(off, tk), :], preferred_element_type=F32)
                total = jnp.sum(lk, axis=1, keepdims=True)
                return later + (total if valid is None else jnp.where(valid, total, 0.0)), acc

            state = []
            for h in range(2):
                carry = tile(h, j0, zero, zacc, mask=diag_mask)
                for n in range(1, SB_STRAIGHT):
                    carry = tile(h, jnp.maximum(j0 - n, 0), *carry, valid=j0 >= n)
                state.append(carry)
            walks.append((rows, first, j0, tile, state))

        for rows, first, j0, tile, state in walks:
            def tail(c, tile=tile):
                res = [tile(h, c[0], c[1][h], c[2][h]) for h in range(2)]
                return c[0] - 1, (res[0][0], res[1][0]), (res[0][1], res[1][1])

            _, _, accs = lax.while_loop(
                _sb_continue, tail, (j0 - SB_STRAIGHT, (state[0][0], state[1][0]), (state[0][1], state[1][1])))
            out = jnp.where(first, accs[0], accs[1])
            o_ref[rows, :] = out.astype(o_ref.dtype)
            of_ref[rows, :] = out

    nq = nq // subs
    tile_spec = pl.BlockSpec((subs * tq, LANES), lambda p, i: (i, p))
    return pl.pallas_call(
        body, name="sb_fwd", grid=(2, nq),
        in_specs=[pl.BlockSpec((subs * tq, LANES), lambda p, i: (i, qb + p)),
                  pl.BlockSpec((L, LANES), lambda p, i: (0, qb + 2 + p)),
                  pl.BlockSpec((L, LANES), lambda p, i: (0, qb + 4 + p))],
        out_specs=[tile_spec, tile_spec],
        out_shape=[jax.ShapeDtypeStruct((L, 2 * LANES), BF16), jax.ShapeDtypeStruct((L, 2 * LANES), F32)],
        compiler_params=_params(("arbitrary", "arbitrary")),
    )(P, P, P)


def _sb_bwd(P, dyb, yb32):
    L = P.shape[0]
    tq, tk = min(SB_TQ, L), min(SB_TK, L)
    nq = L // tq
    qb = P_B // LANES

    def body(q_ref, k_ref, v_ref, do_ref, of_ref, dq_ref, dk_ref, dv_ref):
        i = pl.program_id(1)
        first, qms, j0, diag_mask, strict = _sb_setup(q_ref, i, tq, tk)

        @pl.when(i == 0)
        def _():
            dk_ref[...] = jnp.zeros_like(dk_ref)
            dv_ref[...] = jnp.zeros_like(dv_ref)

        do = do_ref[...]
        doms = (jnp.where(first, do, jnp.zeros_like(do)), jnp.where(first, jnp.zeros_like(do), do))
        prod = _f(do) * of_ref[...]
        totals = (jnp.sum(jnp.where(first, prod, 0.0), axis=1, keepdims=True),
                  jnp.sum(jnp.where(first, 0.0, prod), axis=1, keepdims=True))

        def tile(h, j, later, later_g, acc, mask=None, valid=None):
            off = pl.multiple_of(j * tk, tk)
            kb, vb = k_ref[pl.ds(off, tk), :], v_ref[pl.ds(off, tk), :]
            gate = later if valid is None else jnp.where(valid, later, -jnp.inf)
            z, lk, log_a = _sb_scores(qms[h], kb, gate, strict, mask)
            att = jnp.exp(log_a).astype(BF16)
            g = _f(att) * _dot_nt(doms[h], vb)
            before = totals[h] - later_g
            if valid is not None:
                before = jnp.where(valid, before, 0.0)
            dz = g - (before - _dot_split(g, strict)) * jnp.exp(z + lk)
            if mask is not None:
                dz = jnp.where(mask, dz, 0.0)
            dzb = dz.astype(BF16)
            rows = jnp.sum(lk, axis=1, keepdims=True)
            carry = (later + (rows if valid is None else jnp.where(valid, rows, 0.0)),
                     later_g + jnp.sum(g, axis=1, keepdims=True), acc + jnp.dot(dzb, kb, preferred_element_type=F32))
            return carry, _dot_tn(dzb, qms[h]), _dot_tn(att, doms[h])

        def tail(c):
            off = pl.multiple_of(c[0] * tk, tk)
            (c0, dk0, dv0), (c1, dk1, dv1) = [tile(h, c[0], c[1][h], c[2][h], c[3][h]) for h in range(2)]
            dk_ref[pl.ds(off, tk), :] += dk0 + dk1
            dv_ref[pl.ds(off, tk), :] += dv0 + dv1
            return (c[0] - 1,) + tuple(zip(c0, c1))

        zero, zacc = jnp.zeros((tq, 1), F32), jnp.zeros((tq, LANES), F32)
        blocks = [j0] + [jnp.maximum(j0 - n, 0) for n in range(1, SB_STRAIGHT)]
        carries, dks, dvs = [], [], []
        for h in range(2):
            carry, dk, dv = tile(h, j0, zero, zero, zacc, mask=diag_mask)
            dks.append([dk])
            dvs.append([dv])
            for n in range(1, SB_STRAIGHT):
                carry, dk, dv = tile(h, blocks[n], *carry, valid=j0 >= n)
                dks[h].append(dk)
                dvs[h].append(dv)
            carries.append(carry)
        for n, j in enumerate(blocks):
            off = pl.multiple_of(j * tk, tk)
            dk_ref[pl.ds(off, tk), :] += dks[0][n] + dks[1][n]
            dv_ref[pl.ds(off, tk), :] += dvs[0][n] + dvs[1][n]
        accs = lax.while_loop(_sb_continue, tail, (j0 - SB_STRAIGHT,) + tuple(zip(carries[0], carries[1])))[3]
        dq_ref[...] = jnp.where(first, accs[0], accs[1]) * (SB_HEAD_DIM ** -0.5)

    full = pl.BlockSpec((L, LANES), lambda p, i: (0, p))
    tile_spec = pl.BlockSpec((tq, LANES), lambda p, i: (i, p))
    return pl.pallas_call(
        body, name="sb_bwd", grid=(2, nq),
        in_specs=[pl.BlockSpec((tq, LANES), lambda p, i: (i, qb + p)),
                  pl.BlockSpec((L, LANES), lambda p, i: (0, qb + 2 + p)),
                  pl.BlockSpec((L, LANES), lambda p, i: (0, qb + 4 + p)), tile_spec, tile_spec],
        out_specs=[tile_spec, full, full],
        out_shape=[jax.ShapeDtypeStruct((L, 2 * LANES), F32)] * 3,
        compiler_params=_params(("arbitrary", "arbitrary")),
    )(P, P, P, dyb, yb32)


MOD_SHARD = N_MOD * D_MODEL // N_CHIPS


def _mod_fwd(c_all, mod_w, mod_b_sh):
    tn = 512

    def body(c_ref, w_ref, b_ref, o_ref):
        o_ref[0] = jnp.dot(_silu(c_ref[...]), w_ref[0], precision=HIGHEST, preferred_element_type=F32) + b_ref[0]

    return pl.pallas_call(
        body, name="mod_fwd", grid=(DEPTH, MOD_SHARD // tn),
        in_specs=[pl.BlockSpec((N_DEV, D_MODEL), lambda l, j: (0, 0)),
                  pl.BlockSpec((1, D_MODEL, tn), lambda l, j: (l, 0, j)),
                  pl.BlockSpec((1, 1, tn), lambda l, j: (l, 0, j))],
        out_specs=pl.BlockSpec((1, N_DEV, tn), lambda l, j: (l, 0, j)),
        out_shape=jax.ShapeDtypeStruct((DEPTH, N_DEV, MOD_SHARD), F32),
        compiler_params=_params(("arbitrary", "arbitrary")),
    )(c_all, mod_w, mod_b_sh)


def _mod_bwd(c_all, dmod_sh):
    tn = 512

    def body(c_ref, d_ref, o_ref):
        o_ref[0] = lax.dot_general(_silu(c_ref[...]), d_ref[0], (((0,), (0,)), ((), ())), precision=HIGHEST,
                                   preferred_element_type=F32)

    return pl.pallas_call(
        body, name="mod_bwd", grid=(DEPTH, MOD_SHARD // tn),
        in_specs=[pl.BlockSpec((N_DEV, D_MODEL), lambda l, j: (0, 0)),
                  pl.BlockSpec((1, N_DEV, tn), lambda l, j: (l, 0, j))],
        out_specs=pl.BlockSpec((1, D_MODEL, tn), lambda l, j: (l, 0, j)),
        out_shape=jax.ShapeDtypeStruct((DEPTH, D_MODEL, MOD_SHARD), F32),
        compiler_params=_params(("arbitrary", "arbitrary")),
    )(c_all, dmod_sh)


def _row_tile(rows, cap):
    if rows <= cap:
        return rows
    best = None
    for t in range(8, cap + 1, 8):
        if rows % t == 0:
            best = t
    assert best is not None, (rows, cap)
    return best


def _adamw(name, w, gs, m, v, tr=256):
    R, W = w.shape
    by_layer = any(isinstance(t, tuple) for t in gs)
    tr = _row_tile(R // 2 if by_layer else R, tr)
    per = (R // 2) // tr

    flat, specs = [], []
    for t in gs:
        if isinstance(t, tuple):
            flat += list(t)
            specs += [pl.BlockSpec((tr, W), lambda i: (jnp.minimum(i, per - 1), 0)),
                      pl.BlockSpec((tr, W), lambda i: (jnp.maximum(i - per, 0), 0))]
        else:
            flat.append(t)
            specs.append(pl.BlockSpec((tr, W), lambda i: (i, 0)))
    ng = len(flat)

    def body(*refs):
        w_ref, g_refs, (m_ref, v_ref) = refs[0], list(refs[1:1 + ng]), refs[1 + ng:3 + ng]
        g_out, d_out, m_out, v_out = refs[3 + ng:]
        g = None
        for t in gs:
            if isinstance(t, tuple):
                lo, hi = g_refs.pop(0), g_refs.pop(0)
                term = jnp.where(pl.program_id(0) < per, lo[...], hi[...])
            else:
                term = g_refs.pop(0)[...]
            g = term if g is None else g + term
        mm = ADAM_B1 * m_ref[...] + (1.0 - ADAM_B1) * g
        vv = ADAM_B2 * v_ref[...] + (1.0 - ADAM_B2) * (g * g)
        m_hat = mm / (1.0 - ADAM_B1 ** ADAM_STEP)
        v_hat = vv / (1.0 - ADAM_B2 ** ADAM_STEP)
        g_out[...] = g
        d_out[...] = -ADAM_LR * (m_hat / (jnp.sqrt(v_hat) + ADAM_EPS) + ADAM_WD * w_ref[...])
        m_out[...] = mm
        v_out[...] = vv

    spec = pl.BlockSpec((tr, W), lambda i: (i, 0))
    return pl.pallas_call(
        body, name=name, grid=(R // tr,), in_specs=[spec] + specs + [spec, spec], out_specs=[spec] * 4,
        out_shape=[jax.ShapeDtypeStruct((R, W), F32)] * 4, compiler_params=_params(("arbitrary",)),
    )(w, *flat, m, v)


def _sum_slots(name, a, tr=256):
    n, R, W = a.shape
    tr = _row_tile(R, tr)

    def body(a_ref, o_ref):
        acc = _f(a_ref[0])
        for j in range(1, n):
            acc = acc + _f(a_ref[j])
        o_ref[...] = acc

    return pl.pallas_call(
        body, name=name, grid=(R // tr,), in_specs=[pl.BlockSpec((n, tr, W), lambda i: (0, i, 0))],
        out_specs=pl.BlockSpec((tr, W), lambda i: (i, 0)), out_shape=jax.ShapeDtypeStruct((R, W), F32),
        compiler_params=_params(("arbitrary",)),
    )(a)


def _here():
    return lax.axis_index("x"), lax.axis_index("y"), lax.axis_index("c")


def _flip(v, d):
    return 1 - v if d else v


def _allgather_small(name, buf):
    R = buf.shape[0]
    rel = [(dx, dy, dc) for dx in (0, 1) for dy in (0, 1) for dc in (0, 1)][1:]

    def body(x_ref, o_ref, send, recv, lsem):
        x, y, c = _here()
        me = 4 * x + 2 * y + c
        mine = pltpu.make_async_copy(x_ref, o_ref.at[me], lsem)
        mine.start()

        def copy(k, slot):
            dx, dy, dc = rel[k]
            return pltpu.make_async_remote_copy(
                src_ref=x_ref, dst_ref=o_ref.at[slot], send_sem=send.at[k], recv_sem=recv.at[k],
                device_id=(_flip(x, dx), _flip(y, dy), _flip(c, dc)), device_id_type=MESH_ID)

        sent = [copy(k, me) for k in range(len(rel))]
        for cp in sent:
            cp.start()
        for k, (dx, dy, dc) in enumerate(rel):
            copy(k, 4 * _flip(x, dx) + 2 * _flip(y, dy) + _flip(c, dc)).wait_recv()
        for cp in sent:
            cp.wait_send()
        mine.wait()

    return pl.pallas_call(
        body, name=name, out_shape=jax.ShapeDtypeStruct((N_DEV, R, LANES), F32),
        in_specs=[pl.BlockSpec(memory_space=pltpu.VMEM)], out_specs=pl.BlockSpec(memory_space=pltpu.VMEM),
        scratch_shapes=[pltpu.SemaphoreType.DMA((7,)), pltpu.SemaphoreType.DMA((7,)), pltpu.SemaphoreType.DMA],
    )(buf)


CHIP_REL = [(1, 0), (0, 1), (1, 1)]


class _Side:
    def __init__(self, arrays, out_shapes, scratch, start, finish):
        self.arrays, self.out_shapes, self.scratch, self.start, self.finish = arrays, out_shapes, scratch, start, finish


def _chip_of(k):
    x, y, _ = _here()
    dx, dy = CHIP_REL[k]
    return _flip(x, dx), _flip(y, dy)


def _scatter_side(arrays):
    n = len(arrays)

    def parts(ins, outs, sems):
        send, recv, lsem = sems
        x, y, c = _here()
        s = 2 * x + y

        def copy(w, k, mine):
            px, py = _chip_of(k)
            return pltpu.make_async_remote_copy(
                src_ref=ins[w].at[2 * px + py], dst_ref=outs[w].at[s if mine else 2 * px + py],
                send_sem=send.at[3 * w + k], recv_sem=recv.at[3 * w + k], device_id=(px, py, c), device_id_type=MESH_ID)

        local = [pltpu.make_async_copy(ins[w].at[s], outs[w].at[s], lsem.at[w]) for w in range(n)]
        return copy, local

    def start(ins, outs, sems):
        copy, local = parts(ins, outs, sems)
        for cp in local:
            cp.start()
        for w in range(n):
            for k in range(3):
                copy(w, k, True).start()

    def finish(ins, outs, sems):
        copy, local = parts(ins, outs, sems)
        for w in range(n):
            for k in range(3):
                copy(w, k, False).wait_recv()
        for w in range(n):
            for k in range(3):
                copy(w, k, True).wait_send()
        for cp in local:
            cp.wait()

    scratch = [pltpu.SemaphoreType.DMA((3 * n,)), pltpu.SemaphoreType.DMA((3 * n,)), pltpu.SemaphoreType.DMA((n,))]
    return _Side(arrays, [jax.ShapeDtypeStruct(a.shape, a.dtype) for a in arrays], scratch, start, finish)


def _gather_side(shards):
    n = len(shards)

    def parts(ins, outs, sems):
        send, recv, fsend, frecv, lsem = sems
        x, y, c = _here()
        s = 2 * x + y

        def half(ref, w, which):
            rows = shards[w].shape[0] // 2
            return ref.at[pl.ds(pl.multiple_of(which * rows, 16), rows)]

        def over_ici(w, k, mine):
            px, py = _chip_of(k)
            return pltpu.make_async_remote_copy(
                src_ref=half(ins[w], w, c), dst_ref=half(outs[w].at[s if mine else 2 * px + py], w, c),
                send_sem=send.at[3 * w + k], recv_sem=recv.at[3 * w + k], device_id=(px, py, c), device_id_type=MESH_ID)

        def to_sibling(w, k, which):
            px, py = _chip_of(k)
            part = half(outs[w].at[2 * px + py], w, which)
            return pltpu.make_async_remote_copy(
                src_ref=part, dst_ref=part, send_sem=fsend.at[3 * w + k], recv_sem=frecv.at[3 * w + k],
                device_id=(x, y, 1 - c), device_id_type=MESH_ID)

        local = [pltpu.make_async_copy(ins[w], outs[w].at[s], lsem.at[w]) for w in range(n)]
        return c, over_ici, to_sibling, local

    def start(ins, outs, sems):
        _, over_ici, _, local = parts(ins, outs, sems)
        for cp in local:
            cp.start()
        for w in range(n):
            for k in range(3):
                over_ici(w, k, True).start()

    def finish(ins, outs, sems):
        c, over_ici, to_sibling, local = parts(ins, outs, sems)
        for w in range(n):
            for k in range(3):
                over_ici(w, k, False).wait_recv()
                to_sibling(w, k, c).start()
        for w in range(n):
            for k in range(3):
                to_sibling(w, k, 1 - c).wait_recv()
        for w in range(n):
            for k in range(3):
                over_ici(w, k, True).wait_send()
                to_sibling(w, k, c).wait_send()
        for cp in local:
            cp.wait()

    scratch = [pltpu.SemaphoreType.DMA((3 * n,))] * 4 + [pltpu.SemaphoreType.DMA((n,))]
    return _Side(shards, [jax.ShapeDtypeStruct((N_CHIPS,) + a.shape, a.dtype) for a in shards], scratch, start, finish)


def _sibling_exchange(name, arrays):
    n = len(arrays)

    def body(*refs):
        ins, outs = refs[:n], refs[n:2 * n]
        send, recv = refs[2 * n:]
        x, y, c = _here()
        cps = [pltpu.make_async_remote_copy(src_ref=ins[w], dst_ref=outs[w], send_sem=send.at[w], recv_sem=recv.at[w],
                                            device_id=(x, y, 1 - c), device_id_type=MESH_ID) for w in range(n)]
        for cp in cps:
            cp.start()
        for cp in cps:
            cp.wait()

    any_spec = pl.BlockSpec(memory_space=pl.ANY)
    return pl.pallas_call(
        body, name=name, out_shape=[jax.ShapeDtypeStruct(a.shape, a.dtype) for a in arrays],
        in_specs=[any_spec] * n, out_specs=[any_spec] * n,
        scratch_shapes=[pltpu.SemaphoreType.DMA((n,)), pltpu.SemaphoreType.DMA((n,))],
    )(*arrays)


def _pack(arrs):
    flat = jnp.concatenate([a.reshape(-1).astype(F32) for a in arrs])
    n = flat.shape[0]
    rows = -(-n // (8 * LANES)) * 8
    return jnp.pad(flat, (0, rows * LANES - n)).reshape(rows, LANES)


def _unpack(buf, shapes):
    lead = buf.shape[:-2]
    flat = buf.reshape(lead + (-1,))
    out, off = [], 0
    for s in shapes:
        n = 1
        for d in s:
            n *= d
        out.append(flat[..., off:off + n].reshape(lead + tuple(s)))
        off += n
    return out


def _pad_w_in(w):
    return jnp.concatenate([w[:, :2048], w[:, 2816:2824], jnp.zeros((w.shape[0], P_XBC - P_DT - 8), w.dtype),
                            w[:, 2048:2816], w[:, 2824:]], axis=1)


def _unpad_w_in(g):
    return jnp.concatenate([g[:, :P_DT], g[:, P_XBC:P_G], g[:, P_DT:P_DT + 8], g[:, P_G:]], axis=1)


FFN_HALF = FFN_HIDDEN // 2


def _ffn_in_cols(w):
    h = FFN_HALF
    return jnp.concatenate([w[:, :h], w[:, 2 * h:3 * h], w[:, h:2 * h], w[:, 3 * h:]], axis=1)


def _row(v):
    return v.reshape(1, -1)


BIG = (("w_in", 2), ("w_sc_out", 2), ("w_sb_out", 2), ("w_ssm_out", 2), ("w_o", 1), ("w_ffn_in", 2), ("w_ffn_out", 1))
SMALL = ("mod_b", "g_pre_mix", "g_post_mix", "g_pre_ffn", "g_post_ffn", "sc_conv_w", "ssm_conv_w", "ssm_conv_b",
         "ssm_dt_bias", "ssm_a_log", "ssm_d", "ssm_norm_w")
WEIGHT_ORDER = ("mod_w", "mod_b", "g_pre_mix", "g_post_mix", "g_pre_ffn", "g_post_ffn", "w_in", "sc_conv_w",
                "ssm_conv_w", "ssm_conv_b", "ssm_dt_bias", "ssm_a_log", "ssm_d", "ssm_norm_w", "w_sc_out", "w_sb_out",
                "w_ssm_out", "w_o", "w_ffn_in", "w_ffn_out")


def _mm_mid(name, a, w, x, vecs):
    return _mm_epi(name, a, w, "nn", D_MODEL, [(x, D_MODEL)] + [(v, None) for v in vecs],
                   lambda p, x, *v: (p,) + tuple(_mid(x, p, *v)), [(D_MODEL, BF16), (D_MODEL, F32), (D_MODEL, BF16)])


def _layer_fwd(l, x_in, h, W, V, sides, next_vecs):
    S = {"x_in": x_in, "h": h}
    side, handler = sides.get("in_proj", (None, None))
    P = _mm(f"in_proj{l}", h, W["w_in"], "nn", BF16, tm=2048, tn_cap=1024, side=side)
    if side:
        handler(P[1:])
        P = P[0]
    S["P"] = P
    S["ya"] = _shortconv_fwd(P, V["sc_w"])
    S["yb"], S["yb32"] = _sb_fwd(P)
    S["pre"] = _ssmconv_fwd(P, V["ssm_w"], V["ssm_b"])
    S["y_ssd"], S["states"] = _ssd_fwd(S["pre"], P, V["dtb"], V["alog"])
    S["yc"] = _rowwise(f"ssm_post{l}", lambda y, px, z, d, nw: _ssm_post(y, px, _f(z), d, nw),
                       [(S["y_ssd"], SSM_INNER, 0), (S["pre"], SSM_INNER, 0), (P, SSM_INNER, P_Z // SSM_INNER)],
                       [V["d_full"], V["norm_w"]], [(SSM_INNER, BF16)])[0]
    S["merged"] = _merge_fwd(f"merge{l}", P, [S["ya"], S["yb"], S["yc"]],
                             [W["w_sc_out"], W["w_sb_out"], W["w_ssm_out"]])
    S["mix"], S["x1"], S["h2"] = _mm_mid(f"w_o{l}", S["merged"], W["w_o"], x_in, V["mid_mix"])
    side, handler = sides.get("ffn_in", (None, None))
    res = _mm_epi(f"ffn_in{l}", S["h2"], W["w_ffn_in"], "nn", 2 * FFN_HALF, [],
                  lambda p: (p, _swiglu(p[:, :FFN_HALF], p[:, FFN_HALF:])),
                  [(2 * FFN_HALF, BF16), (FFN_HALF, BF16)], side=side)
    S["GU"], S["act"] = res[0], res[1]
    if side:
        handler(res[2:])
    if next_vecs is None:
        S["f"] = _mm(f"ffn_out{l}", S["act"], W["w_ffn_out"], "nn", BF16)
    else:
        S["f"], S["x_next"], S["h_next"] = _mm_mid(f"ffn_out{l}", S["act"], W["w_ffn_out"], S["x1"], next_vecs)
    return S


BRANCH_WIDTHS = (SC_WIDTH, 256, SSM_INNER)


def _branch_specs(tm):
    gb = P_G // D_MODEL
    gates = [pl.BlockSpec((tm, D_MODEL), functools.partial(lambda i, cb: (i, cb), cb=gb + k)) for k in range(3)]
    ys = [pl.BlockSpec((tm, w), lambda i: (i, 0)) for w in BRANCH_WIDTHS]
    ws = [pl.BlockSpec((w, D_MODEL), lambda i: (0, 0)) for w in BRANCH_WIDTHS]
    return gates, ys, ws


def _merge_fwd(name, P, ys, ws, tm=512):
    L = P.shape[0]
    tm = min(tm, L)
    gates, y_specs, w_specs = _branch_specs(tm)

    def body(ga, gb, gc, ya, yb, yc, wa, wb, wc, o_ref):
        acc = None
        for g_ref, y_ref, w_ref in ((ga, ya, wa), (gb, yb, wb), (gc, yc, wc)):
            t = jax.nn.sigmoid(_f(g_ref[...])) * jnp.dot(y_ref[...], w_ref[...], preferred_element_type=F32)
            acc = t if acc is None else acc + t
        o_ref[...] = acc.astype(o_ref.dtype)

    return pl.pallas_call(
        body, name=name, grid=(L // tm,), in_specs=gates + y_specs + w_specs,
        out_specs=pl.BlockSpec((tm, D_MODEL), lambda i: (i, 0)), out_shape=jax.ShapeDtypeStruct((L, D_MODEL), BF16),
        compiler_params=_params(("arbitrary",)),
    )(P, P, P, *ys, *ws)


def _merge_bwd(name, P, ys, ws, dmerged, tm=512):
    L = P.shape[0]
    tm = min(tm, L)
    gates, y_specs, w_specs = _branch_specs(tm)

    def body(ga, gb, gc, ya, yb, yc, wa, wb, wc, dm_ref, dg_ref, dya, dyb, dyc, gwa, gwb, gwc):
        @pl.when(pl.program_id(0) == 0)
        def _():
            for r in (gwa, gwb, gwc):
                r[...] = jnp.zeros_like(r)

        dm = _f(dm_ref[...])
        for k, (g_ref, y_ref, w_ref, dy_ref, gw_ref) in enumerate(
                ((ga, ya, wa, dya, gwa), (gb, yb, wb, dyb, gwb), (gc, yc, wc, dyc, gwc))):
            y, w = y_ref[...], w_ref[...]
            s = jax.nn.sigmoid(_f(g_ref[...]))
            proj = jnp.dot(y, w, preferred_element_type=F32)
            d_proj = (dm * s).astype(BF16)
            dg_ref[:, k * D_MODEL:(k + 1) * D_MODEL] = (dm * proj * s * (1.0 - s)).astype(dg_ref.dtype)
            dy_ref[...] = _dot_nt(d_proj, w).astype(dy_ref.dtype)
            gw_ref[...] += _dot_tn(y, d_proj)

    gate_cols = pl.BlockSpec((tm, P_WIDTH - P_G), lambda i: (i, P_G // (P_WIDTH - P_G)))
    return pl.pallas_call(
        body, name=name, grid=(L // tm,),
        in_specs=gates + y_specs + w_specs + [pl.BlockSpec((tm, D_MODEL), lambda i: (i, 0))],
        out_specs=[gate_cols] + y_specs + w_specs,
        out_shape=[jax.ShapeDtypeStruct((L, P_WIDTH), BF16)] + [jax.ShapeDtypeStruct((L, w), BF16) for w in BRANCH_WIDTHS]
        + [jax.ShapeDtypeStruct((w, D_MODEL), F32) for w in BRANCH_WIDTHS],
        compiler_params=_params(("arbitrary",)),
    )(P, P, P, *ys, *ws, dmerged)


def _assemble_dp(name, dP, parts, tl=256):
    L = dP.shape[0]
    tl = min(tl, L)
    n = len(parts)

    def body(*refs):
        o_ref = refs[n + 1]
        o_ref[...] = jnp.concatenate([r[...].astype(o_ref.dtype) for r in refs[:n]], axis=1)

    return pl.pallas_call(
        body, name=name, grid=(L // tl,),
        in_specs=[pl.BlockSpec((tl, a.shape[1]), lambda i: (i, 0)) for a in parts] + [pl.BlockSpec(memory_space=pl.ANY)],
        out_specs=pl.BlockSpec((tl, P_G), lambda i: (i, 0)), out_shape=jax.ShapeDtypeStruct(dP.shape, dP.dtype),
        input_output_aliases={n: 0}, compiler_params=_params(("arbitrary",)),
    )(*parts, dP)


def _layer_bwd(l, S, W, V, dx1, df, sides, landed):
    G = {}
    P = S["P"]

    def mm(key, *args, **kw):
        if key not in sides:
            return _mm(f"{key}{l}", *args, **kw)
        names, layer, make = sides[key]
        res = _mm(f"{key}{l}", *args, side=make(G), **kw)
        for n, a in zip(names, res[1:]):
            landed[(n, layer)] = a
        return res[0]

    G["w_ffn_out"] = _mm(f"gw_ffn_out{l}", S["act"], df, "tn", F32)

    def swiglu_bwd(d_act, gu):
        gt, up = _f(gu[:, :FFN_HALF]), _f(gu[:, FFN_HALF:])
        s = jax.nn.sigmoid(gt)
        gs = gt * s
        return (jnp.concatenate([d_act * up * (s + gs * (1.0 - s)), d_act * gs], axis=1),)

    names, layer, make = sides.get("d_gu", ((), None, None))
    res = _mm_epi(f"d_gu{l}", df, W["w_ffn_out"], "nt", FFN_HALF, [(S["GU"], 2 * FFN_HALF)], swiglu_bwd,
                  [(2 * FFN_HALF, BF16)], side=make(G) if make else None)
    dGU = res[0]
    for n, a in zip(names, res[1:]):
        landed[(n, layer)] = a
    dh2 = mm("d_h2", dGU, W["w_ffn_in"], "nt", BF16)
    G["w_ffn_in"] = _ffn_in_cols(mm("gw_ffn_in", S["h2"], dGU, "tn", F32))
    dx, dmix, G["gate1"], G["g_post_mix"], G["g_pre_ffn"], G["scale2"], G["shift2"] = _mid_bwd(
        f"mid_mix_bwd{l}", S["x_in"], S["mix"], dx1, dh2, V["mid_mix"])
    dmerged = _mm(f"d_merged{l}", dmix, W["w_o"], "nt", BF16)
    G["w_o"] = _mm(f"gw_o{l}", S["merged"], dmix, "tn", F32)

    dP, dya, dyb, dyc, G["w_sc_out"], G["w_sb_out"], G["w_ssm_out"] = _merge_bwd(
        f"merge_bwd{l}", P, [S["ya"], S["yb"], S["yc"]], [W["w_sc_out"], W["w_sb_out"], W["w_ssm_out"]], dmerged)

    def post_bwd(y_ssd, px, z, d, dfull, nw):
        z, d = _f(z), _f(d)
        sx, sz = jax.nn.sigmoid(px), jax.nn.sigmoid(z)
        xs, gz = px * sx, z * sz
        t = y_ssd + xs * dfull
        y = t * gz
        dn = d * nw
        half = SSM_INNER // 2
        ns, dys = [], []
        for g in range(2):
            yg, dng = y[:, g * half:(g + 1) * half], dn[:, g * half:(g + 1) * half]
            r = lax.rsqrt(jnp.mean(yg * yg, axis=-1, keepdims=True) + NORM_EPS)
            ns.append(yg * r)
            dys.append(r * (dng - ns[g] * jnp.mean(dng * ns[g], axis=-1, keepdims=True)))
        n, dy = jnp.concatenate(ns, axis=1), jnp.concatenate(dys, axis=1)
        dt = dy * gz
        return (dt, dt * dfull * (sx + xs * (1.0 - sx)), dy * t * (sz + gz * (1.0 - sz)),
                jnp.sum(dt * xs, axis=0, keepdims=True), jnp.sum(d * n, axis=0, keepdims=True))

    dy_ssd, dxs, dz, G["d_full"], G["ssm_norm_w"] = _rowwise(
        f"ssm_post_bwd{l}", post_bwd,
        [(S["y_ssd"], SSM_INNER, 0), (S["pre"], SSM_INNER, 0), (P, SSM_INNER, P_Z // SSM_INNER), (dyc, SSM_INNER, 0)],
        [V["d_full"], V["norm_w"]], [(SSM_INNER, F32), (SSM_INNER, F32), (SSM_INNER, BF16)], [(1, SSM_INNER)] * 2)
    dpre, ddt, G["dtb"], G["alog"] = _ssd_bwd(S["pre"], P, S["states"], dy_ssd, dxs, V["dtb"], V["alog"])
    dxbc, w0, w1, w2, w3, G["ssm_conv_b"] = _ssmconv_bwd(P, dpre, V["ssm_w"])
    G["ssm_conv_w"] = jnp.concatenate([w0, w1, w2, w3], axis=0)
    dq, dk, dv = _sb_bwd(P, dyb, S["yb32"])
    dA, s0, s1, s2 = _shortconv_bwd(P, dya, V["sc_w"])
    G["sc_conv_w"] = jnp.concatenate([s0, s1, s2], axis=0)
    dP = _assemble_dp(f"assemble_dp{l}", dP, [dA, dq, dk, dv, dz, ddt, dxbc])
    G["w_in"] = _mm(f"gw_in{l}", S["h"], dP, "tn", F32, tn_cap=1024)
    dh = mm("d_h", dP, W["w_in"], "nt", BF16, tk_cap=3072)
    return dx, dh, G


def kernel(x, c, mod_w, mod_b, g_pre_mix, g_post_mix, g_pre_ffn, g_post_ffn, w_in, sc_conv_w, ssm_conv_w, ssm_conv_b, ssm_dt_bias, ssm_a_log, ssm_d, ssm_norm_w, w_sc_out, w_sb_out, w_ssm_out, w_o, w_ffn_in, w_ffn_out, loss_target, m_mod_w, m_mod_b, m_g_pre_mix, m_g_post_mix, m_g_pre_ffn, m_g_post_ffn, m_w_in, m_sc_conv_w, m_ssm_conv_w, m_ssm_conv_b, m_ssm_dt_bias, m_ssm_a_log, m_ssm_d, m_ssm_norm_w, m_w_sc_out, m_w_sb_out, m_w_ssm_out, m_w_o, m_w_ffn_in, m_w_ffn_out, v_mod_w, v_mod_b, v_g_pre_mix, v_g_post_mix, v_g_pre_ffn, v_g_post_ffn, v_w_in, v_sc_conv_w, v_ssm_conv_w, v_ssm_conv_b, v_ssm_dt_bias, v_ssm_a_log, v_ssm_d, v_ssm_norm_w, v_w_sc_out, v_w_sb_out, v_w_ssm_out, v_w_o, v_w_ffn_in, v_w_ffn_out):
    wts = dict(mod_w=mod_w, mod_b=mod_b, g_pre_mix=g_pre_mix, g_post_mix=g_post_mix, g_pre_ffn=g_pre_ffn,
               g_post_ffn=g_post_ffn, w_in=w_in, sc_conv_w=sc_conv_w, ssm_conv_w=ssm_conv_w, ssm_conv_b=ssm_conv_b,
               ssm_dt_bias=ssm_dt_bias, ssm_a_log=ssm_a_log, ssm_d=ssm_d, ssm_norm_w=ssm_norm_w, w_sc_out=w_sc_out,
               w_sb_out=w_sb_out, w_ssm_out=w_ssm_out, w_o=w_o, w_ffn_in=w_ffn_in, w_ffn_out=w_ffn_out)
    mom = dict(mod_w=m_mod_w, mod_b=m_mod_b, g_pre_mix=m_g_pre_mix, g_post_mix=m_g_post_mix, g_pre_ffn=m_g_pre_ffn,
               g_post_ffn=m_g_post_ffn, w_in=m_w_in, sc_conv_w=m_sc_conv_w, ssm_conv_w=m_ssm_conv_w,
               ssm_conv_b=m_ssm_conv_b, ssm_dt_bias=m_ssm_dt_bias, ssm_a_log=m_ssm_a_log, ssm_d=m_ssm_d,
               ssm_norm_w=m_ssm_norm_w, w_sc_out=m_w_sc_out, w_sb_out=m_w_sb_out, w_ssm_out=m_w_ssm_out, w_o=m_w_o,
               w_ffn_in=m_w_ffn_in, w_ffn_out=m_w_ffn_out)
    var = dict(mod_w=v_mod_w, mod_b=v_mod_b, g_pre_mix=v_g_pre_mix, g_post_mix=v_g_post_mix, g_pre_ffn=v_g_pre_ffn,
               g_post_ffn=v_g_post_ffn, w_in=v_w_in, sc_conv_w=v_sc_conv_w, ssm_conv_w=v_ssm_conv_w,
               ssm_conv_b=v_ssm_conv_b, ssm_dt_bias=v_ssm_dt_bias, ssm_a_log=v_ssm_a_log, ssm_d=v_ssm_d,
               ssm_norm_w=v_ssm_norm_w, w_sc_out=v_w_sc_out, w_sb_out=v_w_sb_out, w_ssm_out=v_w_ssm_out, w_o=v_w_o,
               w_ffn_in=v_w_ffn_in, w_ffn_out=v_w_ffn_out)
    xi, yi, ci = _here()
    chip = 2 * xi + yi
    me = 4 * xi + 2 * yi + ci
    x0, target = x[0], loss_target[0]

    first_shapes = [(D_MODEL,), sc_conv_w.shape, ssm_conv_w.shape]
    g0 = _allgather_small("gather_cond", _pack([c, sc_conv_w, ssm_conv_w]))
    c_rows, sc_sh, ssm_sh = _unpack(g0, first_shapes)
    c_all = c_rows
    sc_w = jnp.concatenate([sc_sh[2 * j] for j in range(N_CHIPS)], axis=-1)
    ssm_w = jnp.concatenate([ssm_sh[2 * j] for j in range(N_CHIPS)], axis=-1)

    mod_b_sh = lax.dynamic_slice_in_dim(mod_b, chip * MOD_SHARD, MOD_SHARD, axis=1).reshape(DEPTH, 1, MOD_SHARD)
    modpart = _mod_fwd(c_all, mod_w, mod_b_sh)
    g1 = _allgather_small("gather_mod", modpart.reshape(-1, LANES)).reshape(N_DEV, DEPTH, N_DEV, MOD_SHARD)
    mod = jnp.concatenate([lax.dynamic_index_in_dim(g1[2 * j], me, axis=1, keepdims=False) for j in range(N_CHIPS)],
                          axis=-1)

    def layer_shards(l):
        return [wts[n][l].astype(BF16) for n, _ in BIG]

    def full_weights(which, gathered):
        W = {n: jnp.concatenate([g[j] for j in range(N_CHIPS)], axis=ax - 1) for (n, ax), g in zip(which, gathered)}
        if "w_in" in W:
            W["w_in"] = _pad_w_in(W["w_in"])
        if "w_ffn_in" in W:
            W["w_ffn_in"] = _ffn_in_cols(W["w_ffn_in"])
        return W

    Ws = [{}, {}]
    fwd_sides = [{"in_proj": (_gather_side(layer_shards(0)[1:]), lambda got: Ws[0].update(full_weights(BIG[1:], got))),
                  "ffn_in": (_gather_side(layer_shards(1)), lambda got: Ws[1].update(full_weights(BIG, got)))}, {}]
    Vs = []
    for l in range(DEPTH):
        sh1, sc1, gt1, sh2, sc2, gt2 = [_row(v) for v in jnp.split(mod[l], N_MOD)]
        Vs.append(dict(
            shift1=sh1, scale1=sc1, g_pre_mix=_row(g_pre_mix[l]),
            mid_mix=[gt1, _row(g_post_mix[l]), _row(g_pre_ffn[l]), sc2, sh2],
            gate2=gt2, g_post_ffn=_row(g_post_ffn[l]),
            sc_w=sc_w[l], ssm_w=ssm_w[l], ssm_b=_row(ssm_conv_b[l]),
            dtb=_row(jnp.pad(ssm_dt_bias[l], (0, LANES - SSM_HEADS))), alog=_row(jnp.pad(ssm_a_log[l], (0, LANES - SSM_HEADS))),
            d_full=_row(jnp.repeat(ssm_d[l], SSM_INNER // SSM_HEADS)), norm_w=_row(ssm_norm_w[l])))

    def mid_ffn_vecs(l):
        return [Vs[l]["gate2"], Vs[l]["g_post_ffn"], Vs[l + 1]["g_pre_mix"], Vs[l + 1]["scale1"], Vs[l + 1]["shift1"]]

    saved = []
    x_in = x0
    h, *got = _first_fwd(x0, [Vs[0]["g_pre_mix"], Vs[0]["scale1"], Vs[0]["shift1"]], _gather_side(layer_shards(0)[:1]))
    Ws[0].update(full_weights(BIG[:1], got))
    for l in range(DEPTH):
        S = _layer_fwd(l, x_in, h, Ws[l], Vs[l], fwd_sides[l], mid_ffn_vecs(l) if l + 1 < DEPTH else None)
        saved.append(S)
        if l + 1 < DEPTH:
            x_in, h = S["x_next"], S["h_next"]

    def pieces(G, names):
        out = []
        for n, ax in BIG:
            if n in names:
                g = _unpad_w_in(G[n]) if n == "w_in" else G[n]
                out.append(jnp.stack(jnp.split(g, N_CHIPS, axis=ax - 1)).astype(BF16))
        return out

    small_names = tuple(n for n, _ in BIG if n not in ("w_in", "w_ffn_in"))
    late_names = tuple(n for n, _ in BIG if n != "w_in")
    landed = {}

    GL = [None] * DEPTH
    S = saved[-1]
    dx1, df, g_gate2, g_gpf, loss_cols = _last_bwd(S["x1"], S["f"], target, [Vs[-1]["gate2"], Vs[-1]["g_post_ffn"]])
    for l in reversed(range(DEPTH)):
        sides = {}
        if l + 1 < DEPTH:
            for key, names in (("d_gu", small_names), ("d_h2", ("w_ffn_in",)), ("gw_ffn_in", ("w_in",))):
                sides[key] = (names, l + 1, lambda G, up=GL[l + 1], names=names: _scatter_side(pieces(up, names)))
        if l == 0:
            sides["d_h"] = (late_names, l, lambda G: _scatter_side(pieces(G, late_names)))
        dx, dh, G = _layer_bwd(l, saved[l], Ws[l], Vs[l], dx1, df, sides, landed)
        G["gate2"], G["g_post_ffn"] = g_gate2, g_gpf
        GL[l] = G
        if l > 0:
            Sp = saved[l - 1]
            dx1, df, g_gate2, g_gpf, G["g_pre_mix"], G["scale1"], G["shift1"] = _mid_bwd(
                f"mid_ffn_bwd{l - 1}", Sp["x1"], Sp["f"], dx, dh, mid_ffn_vecs(l - 1))
        else:
            grad_x, G["g_pre_mix"], G["scale1"], G["shift1"], landed[("w_in", 0)] = _first_bwd(
                x0, dx, dh, [Vs[0]["g_pre_mix"], Vs[0]["scale1"], Vs[0]["shift1"]], _scatter_side(pieces(G, ("w_in",))))
    loss = lax.psum(jnp.sum(loss_cols), ("x", "y", "c"))

    def both(key, shape=None):
        a = jnp.stack([GL[l][key] for l in range(DEPTH)])
        return a if shape is None else a.reshape(shape)

    dmod = jnp.concatenate([both(k, (DEPTH, D_MODEL)) for k in ("shift1", "scale1", "gate1", "shift2", "scale2", "gate2")],
                           axis=1)
    part_small = dict(
        mod_b=dmod, g_pre_mix=both("g_pre_mix", (DEPTH, D_MODEL)), g_post_mix=both("g_post_mix", (DEPTH, D_MODEL)),
        g_pre_ffn=both("g_pre_ffn", (DEPTH, D_MODEL)), g_post_ffn=both("g_post_ffn", (DEPTH, D_MODEL)),
        sc_conv_w=both("sc_conv_w"), ssm_conv_w=both("ssm_conv_w"), ssm_conv_b=both("ssm_conv_b", (DEPTH, SSM_CONV_DIM)),
        ssm_dt_bias=both("dtb", (DEPTH, LANES))[:, :SSM_HEADS], ssm_a_log=both("alog", (DEPTH, LANES))[:, :SSM_HEADS],
        ssm_d=both("d_full", (DEPTH, SSM_HEADS, SSM_INNER // SSM_HEADS)).sum(-1),
        ssm_norm_w=both("ssm_norm_w", (DEPTH, SSM_INNER)))
    small_shapes = [part_small[n].shape for n in SMALL]
    g2 = _allgather_small("gather_small_grads", _pack([part_small[n] for n in SMALL]))
    tot = dict(zip(SMALL, _unpack(_sum_slots("sum_small_grads", g2), small_shapes)))
    dmod_all = _unpack(g2, small_shapes)[0]
    dmod_sh = jnp.swapaxes(lax.dynamic_slice_in_dim(dmod_all, chip * MOD_SHARD, MOD_SHARD, axis=2), 0, 1)
    grads = {"mod_w": _mod_bwd(c_all, dmod_sh)}
    for n in SMALL:
        grads[n] = tot[n]
    grads["sc_conv_w"] = lax.dynamic_slice_in_dim(tot["sc_conv_w"], chip * 64, 64, axis=2)
    grads["ssm_conv_w"] = lax.dynamic_slice_in_dim(tot["ssm_conv_w"], chip * 192, 192, axis=2)

    keys = [(n, l) for n, _ in BIG for l in range(DEPTH)]
    mine = [_sum_slots(f"sum_{n}{l}", landed[(n, l)]) for n, l in keys]
    theirs = dict(zip(keys, _sibling_exchange("swap_core_sums", mine)))
    mine = dict(zip(keys, mine))

    out = {}

    def update(name, w2, gs, m2, v2, shape):
        g, d, nm, nv = _adamw(f"adamw_{name}", w2, gs, m2, v2)
        out[name] = tuple(a.reshape(shape) for a in (g, d, nm, nv))

    for n, _ in BIG:
        shp = wts[n].shape
        two = (-1, shp[-1])
        by_layer = [tuple(src[(n, l)] for l in range(DEPTH)) for src in (mine, theirs)]
        update(n, wts[n].reshape(two), by_layer, mom[n].reshape(two), var[n].reshape(two), shp)
    two = (-1, MOD_SHARD)
    update("mod_w", mod_w.reshape(two), [grads["mod_w"].reshape(two)], m_mod_w.reshape(two), v_mod_w.reshape(two), mod_w.shape)
    shapes = [wts[n].shape for n in SMALL]
    res = _adamw("adamw_small", _pack([wts[n] for n in SMALL]), [_pack([grads[n] for n in SMALL])],
                 _pack([mom[n] for n in SMALL]), _pack([var[n] for n in SMALL]))
    for n, g, d, nm, nv in zip(SMALL, *[_unpack(r, shapes) for r in res]):
        out[n] = (g, d, nm, nv)

    result = [loss, grad_x[None]]
    for k in range(4):
        result += [out[n][k] for n in WEIGHT_ORDER]
    return tuple(result)
```

```python
import functools

import jax
import jax.numpy as jnp
from jax import lax
from jax.experimental import pallas as pl
from jax.experimental.pallas import tpu as pltpu

F32 = jnp.float32
BF16 = jnp.bfloat16
HIGHEST = lax.Precision.HIGHEST
MESH_ID = pl.DeviceIdType.MESH

D_MODEL = 1024
DEPTH = 2
SC_WIDTH = 256
SB_HEAD_DIM = 64
SSM_INNER = 512
SSM_HEADS = 8
SSM_STATE = 64
SSM_CONV = 4
SSM_CHUNK = 256
SSM_CONV_DIM = 768
FFN_HIDDEN = 2816
NORM_EPS = 1e-6
N_MOD = 6
N_CHIPS = 4
N_DEV = 8

ADAM_LR = 0.001
ADAM_B1 = 0.9
ADAM_B2 = 0.999
ADAM_EPS = 1e-08
ADAM_WD = 0.01
ADAM_STEP = 10

P_WIDTH = 6144
P_A, P_B, P_Z, P_DT, P_XBC, P_G = 0, 768, 1536, 2048, 2304, 3072
DT_PAD = 256

VMEM_LIMIT_BYTES = 56 * 1024 * 1024
LANES = 128

SB_LOG_CUTOFF = -105.0
SB_TQ = 256
SB_TK = 256
SB_SUBS = 2
SB_STRAIGHT = 2


def _params(sem):
    return pltpu.CompilerParams(dimension_semantics=sem, vmem_limit_bytes=VMEM_LIMIT_BYTES)


def _pick(n, cap):
    if n <= cap:
        return n
    best = None
    for m in range(LANES, cap + 1, LANES):
        if n % m == 0:
            best = m
    assert best is not None, (n, cap)
    return best


def _rowwise(name, fn, rows, vecs, row_outs, acc_outs=(), tl=512, side=None):
    L = rows[0][0].shape[0]
    tl = min(tl, L)
    assert L % tl == 0
    n_in = len(rows) + len(vecs)
    n_ro, n_ao = len(row_outs), len(acc_outs)
    n_si = len(side.arrays) if side else 0
    n_so = len(side.out_shapes) if side else 0

    def body(*refs):
        ins, s_in = refs[:n_in], refs[n_in:n_in + n_si]
        outs = refs[n_in + n_si:]
        ro, ao, s_out, sems = outs[:n_ro], outs[n_ro:n_ro + n_ao], outs[n_ro + n_ao:n_ro + n_ao + n_so], outs[n_ro + n_ao + n_so:]
        if side:
            @pl.when(pl.program_id(0) == 0)
            def _():
                side.start(s_in, s_out, sems)

        _rows(ins, ro, ao)
        if side:
            @pl.when(pl.program_id(0) == L // tl - 1)
            def _():
                side.finish(s_in, s_out, sems)

    def _rows(ins, ro, ao):
        vals = fn(*[r[...] for r in ins])
        if not isinstance(vals, (tuple, list)):
            vals = (vals,)
        for o, v in zip(ro, vals[:n_ro]):
            o[...] = v.astype(o.dtype)
        if ao:
            @pl.when(pl.program_id(0) == 0)
            def _():
                for o in ao:
                    o[...] = jnp.zeros_like(o)
            for o, v in zip(ao, vals[n_ro:]):
                o[...] += v.astype(F32)

    in_specs = [pl.BlockSpec((tl, w), functools.partial(lambda i, cb: (i, cb), cb=cb)) for _, w, cb in rows]
    in_specs += [pl.BlockSpec(v.shape, lambda i: (0, 0)) for v in vecs]
    out_specs = [pl.BlockSpec((tl, w), lambda i: (i, 0)) for w, _ in row_outs]
    out_specs += [pl.BlockSpec(s, lambda i: (0, 0)) for s in acc_outs]
    out_shape = [jax.ShapeDtypeStruct((L, w), dt) for w, dt in row_outs]
    out_shape += [jax.ShapeDtypeStruct(s, F32) for s in acc_outs]
    any_spec = pl.BlockSpec(memory_space=pl.ANY)
    return pl.pallas_call(
        body, name=name, grid=(L // tl,), in_specs=in_specs + [any_spec] * n_si, out_specs=out_specs + [any_spec] * n_so,
        out_shape=out_shape + (side.out_shapes if side else []), scratch_shapes=side.scratch if side else [],
        compiler_params=_params(("arbitrary",)),
    )(*[a for a, _, _ in rows], *vecs, *(side.arrays if side else []))


def _mm(name, a, b, mode, out_dtype, tm=1024, tn_cap=1408, tk_cap=2816, side=None):
    if mode == "nn":
        (M, K), (_, N) = a.shape, b.shape
    elif mode == "nt":
        (M, K), (N, _) = a.shape, b.shape
    else:
        (K, M), (_, N) = a.shape, b.shape
        tm, tk_cap = 1408, 2048
    tm = _pick(M, tm)
    tn = _pick(N, tn_cap)
    tk = _pick(K, tk_cap)
    nk = K // tk
    grid = (M // tm, N // tn, nk)
    n_si = len(side.arrays) if side else 0
    n_so = len(side.out_shapes) if side else 0
    n_acc = 1 if nk > 1 else 0

    def body(a_ref, b_ref, *rest):
        s_in, o_ref, s_out = rest[:n_si], rest[n_si], rest[n_si + 1:n_si + 1 + n_so]
        scr = rest[n_si + 1 + n_so:]
        if side:
            at = [pl.program_id(d) for d in range(3)]
            is_first = jnp.logical_and(jnp.logical_and(at[0] == 0, at[1] == 0), at[2] == 0)
            is_last = jnp.logical_and(jnp.logical_and(at[0] == grid[0] - 1, at[1] == grid[1] - 1), at[2] == grid[2] - 1)

            @pl.when(is_first)
            def _():
                side.start(s_in, s_out, scr[n_acc:])

        _product(a_ref, b_ref, o_ref, scr)
        if side:
            @pl.when(is_last)
            def _():
                side.finish(s_in, s_out, scr[n_acc:])

    def _product(a_ref, b_ref, o_ref, scr):
        if mode == "nn":
            p = jnp.dot(a_ref[...], b_ref[...], preferred_element_type=F32)
        elif mode == "nt":
            p = lax.dot_general(a_ref[...], b_ref[...], (((1,), (1,)), ((), ())), preferred_element_type=F32)
        else:
            p = lax.dot_general(a_ref[...], b_ref[...], (((0,), (0,)), ((), ())), preferred_element_type=F32)
        if nk == 1:
            o_ref[...] = p.astype(o_ref.dtype)
        else:
            acc = scr[0]
            k = pl.program_id(2)

            @pl.when(k == 0)
            def _():
                acc[...] = p

            @pl.when(k > 0)
            def _():
                acc[...] += p

            @pl.when(k == nk - 1)
            def _():
                o_ref[...] = acc[...].astype(o_ref.dtype)

    if mode == "nn":
        a_spec = pl.BlockSpec((tm, tk), lambda i, j, k: (i, k))
        b_spec = pl.BlockSpec((tk, tn), lambda i, j, k: (k, j))
    elif mode == "nt":
        a_spec = pl.BlockSpec((tm, tk), lambda i, j, k: (i, k))
        b_spec = pl.BlockSpec((tn, tk), lambda i, j, k: (j, k))
    else:
        a_spec = pl.BlockSpec((tk, tm), lambda i, j, k: (k, i))
        b_spec = pl.BlockSpec((tk, tn), lambda i, j, k: (k, j))
    any_spec = pl.BlockSpec(memory_space=pl.ANY)
    res = pl.pallas_call(
        body, name=name, grid=grid, in_specs=[a_spec, b_spec] + [any_spec] * n_si,
        out_specs=[pl.BlockSpec((tm, tn), lambda i, j, k: (i, j))] + [any_spec] * n_so,
        out_shape=[jax.ShapeDtypeStruct((M, N), out_dtype)] + (side.out_shapes if side else []),
        scratch_shapes=([pltpu.VMEM((tm, tn), F32)] if nk > 1 else []) + (side.scratch if side else []),
        compiler_params=_params(("arbitrary", "arbitrary", "arbitrary")),
    )(a, b, *(side.arrays if side else []))
    return res if side else res[0]


def _mm_epi(name, a, b, mode, tn, extras, epi, outs, tm=512, side=None):
    if mode == "nn":
        (M, K), (_, N) = a.shape, b.shape
    else:
        (M, K), (N, _) = a.shape, b.shape
    tm = _pick(M, tm)
    grid = (N // tn, M // tm)
    n_ex, n_out = len(extras), len(outs)
    n_si = len(side.arrays) if side else 0
    n_so = len(side.out_shapes) if side else 0

    def body(*refs):
        a_ref, b_ref, ex = refs[0], refs[1], refs[2:2 + n_ex]
        s_in = refs[2 + n_ex:2 + n_ex + n_si]
        o_refs = refs[2 + n_ex + n_si:2 + n_ex + n_si + n_out]
        s_out = refs[2 + n_ex + n_si + n_out:2 + n_ex + n_si + n_out + n_so]
        sems = refs[2 + n_ex + n_si + n_out + n_so:]
        if side:
            @pl.when(jnp.logical_and(pl.program_id(0) == 0, pl.program_id(1) == 0))
            def _():
                side.start(s_in, s_out, sems)

        if mode == "nn":
            p = jnp.dot(a_ref[...], b_ref[...], preferred_element_type=F32)
        else:
            p = lax.dot_general(a_ref[...], b_ref[...], (((1,), (1,)), ((), ())), preferred_element_type=F32)
        for o, v in zip(o_refs, epi(p, *[r[...] for r in ex])):
            o[...] = v.astype(o.dtype)
        if side:
            @pl.when(jnp.logical_and(pl.program_id(0) == grid[0] - 1, pl.program_id(1) == grid[1] - 1))
            def _():
                side.finish(s_in, s_out, sems)

    any_spec = pl.BlockSpec(memory_space=pl.ANY)
    a_spec = pl.BlockSpec((tm, K), lambda j, i: (i, 0))
    b_spec = pl.BlockSpec((K, tn), lambda j, i: (0, j)) if mode == "nn" else pl.BlockSpec((tn, K), lambda j, i: (j, 0))
    return pl.pallas_call(
        body, name=name, grid=grid,
        in_specs=[a_spec, b_spec]
        + [pl.BlockSpec(e.shape, lambda j, i: (0, 0)) if w is None else pl.BlockSpec((tm, w), lambda j, i: (i, j))
           for e, w in extras] + [any_spec] * n_si,
        out_specs=[pl.BlockSpec((tm, w), lambda j, i: (i, j)) for w, _ in outs] + [any_spec] * n_so,
        out_shape=[jax.ShapeDtypeStruct((M, (N // tn) * w), dt) for w, dt in outs] + (side.out_shapes if side else []),
        scratch_shapes=side.scratch if side else [],
        compiler_params=_params(("arbitrary", "arbitrary")),
    )(a, b, *[e for e, _ in extras], *(side.arrays if side else []))


def _f(x):
    return x.astype(F32)


def _silu(x):
    return x * jax.nn.sigmoid(x)


def _softplus(x):
    return jnp.maximum(x, 0.0) + jnp.log1p(jnp.exp(-jnp.abs(x)))


def _rms(x, g):
    r = lax.rsqrt(jnp.mean(x * x, axis=-1, keepdims=True) + NORM_EPS)
    return x * r * g


def _adaln(x, g, scale, shift):
    return _rms(x, g) * (1.0 + scale) + shift


def _resid(x, y, gate, g):
    return x + gate * _rms(y, g)


def _mid(x, y, gate, g_post, g_pre, scale, shift):
    x_new = _resid(x, y, gate, g_post)
    return x_new, _adaln(x_new, g_pre, scale, shift)


def _swiglu(gt, up):
    return _silu(gt) * up


def _ssm_post(y_ssd, pre_xs, z, d_full, norm_w):
    y = (y_ssd + _silu(pre_xs) * d_full) * _silu(z)
    half = SSM_INNER // 2
    parts = []
    for g in range(2):
        yg = y[:, g * half:(g + 1) * half]
        parts.append(yg * lax.rsqrt(jnp.mean(yg * yg, axis=-1, keepdims=True) + NORM_EPS))
    return jnp.concatenate(parts, axis=1) * norm_w


def _first_fwd(x, vecs, side=None):
    return _rowwise("adaln_first", lambda x, g, sc, sh: _adaln(x, g, sc, sh),
                    [(x, D_MODEL, 0)], vecs, [(D_MODEL, BF16)], side=side)


def _mid_bwd(name, x, y, dx_new, dh, vecs):
    def fn(x, y, dxn, dh, gate, g_post, g_pre, scale, shift):
        y, dh = _f(y), _f(dh)
        r_y = lax.rsqrt(jnp.mean(y * y, axis=-1, keepdims=True) + NORM_EPS)
        n_y = y * r_y
        both = gate * g_post
        x_new = x + n_y * both
        r_x = lax.rsqrt(jnp.mean(x_new * x_new, axis=-1, keepdims=True) + NORM_EPS)
        u = x_new * r_x
        col_p = jnp.sum(dh * u, axis=0, keepdims=True)
        du = dh * (g_pre * (1.0 + scale))
        dxt = dxn + r_x * (du - u * jnp.mean(du * u, axis=-1, keepdims=True))
        col_q = jnp.sum(dxt * n_y, axis=0, keepdims=True)
        dn = dxt * both
        dy = r_y * (dn - n_y * jnp.mean(dn * n_y, axis=-1, keepdims=True))
        return (dxt, dy, col_q * g_post, col_q * gate, col_p * (1.0 + scale), col_p * g_pre,
                jnp.sum(dh, axis=0, keepdims=True))

    vec = (1, D_MODEL)
    return _rowwise(name, fn, [(x, D_MODEL, 0), (y, D_MODEL, 0), (dx_new, D_MODEL, 0), (dh, D_MODEL, 0)], vecs,
                    [(D_MODEL, F32), (D_MODEL, BF16)], [vec] * 5)


def _first_bwd(x, dx_in, dh, vecs, side=None):
    def fn(x, dxi, dh, g, scale, shift):
        dh = _f(dh)
        r = lax.rsqrt(jnp.mean(x * x, axis=-1, keepdims=True) + NORM_EPS)
        u = x * r
        col_p = jnp.sum(dh * u, axis=0, keepdims=True)
        du = dh * (g * (1.0 + scale))
        dx = dxi + r * (du - u * jnp.mean(du * u, axis=-1, keepdims=True))
        return dx, col_p * (1.0 + scale), col_p * g, jnp.sum(dh, axis=0, keepdims=True)

    vec = (1, D_MODEL)
    return _rowwise("adaln_first_bwd", fn, [(x, D_MODEL, 0), (dx_in, D_MODEL, 0), (dh, D_MODEL, 0)], vecs,
                    [(D_MODEL, F32)], [vec] * 3, side=side)


def _last_bwd(x1, f, target, vecs):
    def fn(x1, f, t, gate, g):
        f = _f(f)
        r = lax.rsqrt(jnp.mean(f * f, axis=-1, keepdims=True) + NORM_EPS)
        n = f * r
        both = gate * g
        err = x1 + n * both - t
        d = err * (1.0 / D_MODEL)
        col_q = jnp.sum(d * n, axis=0, keepdims=True)
        dn = d * both
        df = r * (dn - n * jnp.mean(dn * n, axis=-1, keepdims=True))
        loss_cols = jnp.sum(err * err, axis=0, keepdims=True) * (0.5 / D_MODEL)
        return d, df, col_q * g, col_q * gate, loss_cols

    vec = (1, D_MODEL)
    return _rowwise("loss_last_bwd", fn, [(x1, D_MODEL, 0), (f, D_MODEL, 0), (target, D_MODEL, 0)], vecs,
                    [(D_MODEL, F32), (D_MODEL, BF16)], [vec] * 3)


HALO = 16


def _shift_down(u, prev, k):
    rows = lax.broadcasted_iota(jnp.int32, u.shape, 0)
    v = pltpu.roll(u, k, 0)
    for t in range(k):
        v = jnp.where(rows == t, prev[HALO - k + t:HALO - k + t + 1, :], v)
    return v


def _shift_up(u, nxt, k):
    n = u.shape[0]
    rows = lax.broadcasted_iota(jnp.int32, u.shape, 0)
    v = pltpu.roll(u, n - k, 0)
    for t in range(k):
        v = jnp.where(rows == n - k + t, nxt[t:t + 1, :], v)
    return v


def _conv_specs(L, tl, width, col_block):
    per = tl // HALO
    last = L // HALO - 1
    main = pl.BlockSpec((tl, width), lambda i: (i, col_block))
    before = pl.BlockSpec((HALO, width), lambda i: (jnp.maximum(i * per - 1, 0), col_block))
    after = pl.BlockSpec((HALO, width), lambda i: (jnp.minimum((i + 1) * per, last), col_block))
    return main, before, after


def _shortconv_fwd(P, w, tl=512):
    L = P.shape[0]
    tl = min(tl, L)
    C = SC_WIDTH
    main, before, _ = _conv_specs(L, tl, 3 * C, 0)

    def body(p_ref, h_ref, w_ref, o_ref):
        first = (pl.program_id(0) == 0)
        p, h = _f(p_ref[...]), _f(h_ref[...])
        b, u = p[:, :C], p[:, C:2 * C] * p[:, 2 * C:]
        uh = jnp.where(first, 0.0, h[:, C:2 * C] * h[:, 2 * C:])
        wv = w_ref[...]
        cv = wv[2:3] * u + wv[1:2] * _shift_down(u, uh, 1) + wv[0:1] * _shift_down(u, uh, 2)
        o_ref[...] = (b * cv).astype(o_ref.dtype)

    return pl.pallas_call(
        body, name="shortconv_fwd", grid=(L // tl,),
        in_specs=[main, before, pl.BlockSpec(w.shape, lambda i: (0, 0))],
        out_specs=pl.BlockSpec((tl, C), lambda i: (i, 0)),
        out_shape=jax.ShapeDtypeStruct((L, C), BF16), compiler_params=_params(("arbitrary",)),
    )(P, P, w)


def _shortconv_bwd(P, dya, w, tl=512):
    L = P.shape[0]
    tl = min(tl, L)
    C = SC_WIDTH
    main, before, after = _conv_specs(L, tl, 3 * C, 0)
    dmain, _, dafter = _conv_specs(L, tl, C, 0)
    n = L // tl

    def body(p_ref, h_ref, n_ref, d_ref, dn_ref, w_ref, o_ref, dw0, dw1, dw2):
        i = pl.program_id(0)
        p, h, nx = _f(p_ref[...]), _f(h_ref[...]), _f(n_ref[...])
        b, c, x = p[:, :C], p[:, C:2 * C], p[:, 2 * C:]
        u = c * x
        uh = jnp.where(i == 0, 0.0, h[:, C:2 * C] * h[:, 2 * C:])
        u1, u2 = _shift_down(u, uh, 1), _shift_down(u, uh, 2)
        wv = w_ref[...]
        cv = wv[2:3] * u + wv[1:2] * u1 + wv[0:1] * u2
        dy = _f(d_ref[...])
        dcv = dy * b
        dcv_n = jnp.where(i == n - 1, 0.0, _f(dn_ref[...]) * nx[:, :C])
        du = wv[2:3] * dcv + wv[1:2] * _shift_up(dcv, dcv_n, 1) + wv[0:1] * _shift_up(dcv, dcv_n, 2)
        o_ref[:, :C] = (dy * cv).astype(o_ref.dtype)
        o_ref[:, C:2 * C] = (du * x).astype(o_ref.dtype)
        o_ref[:, 2 * C:] = (du * c).astype(o_ref.dtype)

        @pl.when(i == 0)
        def _():
            for r in (dw0, dw1, dw2):
                r[...] = jnp.zeros_like(r)

        dw0[...] += jnp.sum(dcv * u2, axis=0, keepdims=True)
        dw1[...] += jnp.sum(dcv * u1, axis=0, keepdims=True)
        dw2[...] += jnp.sum(dcv * u, axis=0, keepdims=True)

    vec = pl.BlockSpec((1, C), lambda i: (0, 0))
    return pl.pallas_call(
        body, name="shortconv_bwd", grid=(n,),
        in_specs=[main, before, after, dmain, dafter, pl.BlockSpec(w.shape, lambda i: (0, 0))],
        out_specs=[pl.BlockSpec((tl, 3 * C), lambda i: (i, 0)), vec, vec, vec],
        out_shape=[jax.ShapeDtypeStruct((L, 3 * C), BF16)] + [jax.ShapeDtypeStruct((1, C), F32)] * 3,
        compiler_params=_params(("arbitrary",)),
    )(P, P, P, dya, dya, w)


def _ssmconv_fwd(P, w, bias, tl=512):
    L = P.shape[0]
    tl = min(tl, L)
    C = SSM_CONV_DIM
    main, before, _ = _conv_specs(L, tl, C, P_XBC // C)

    def body(p_ref, h_ref, w_ref, b_ref, o_ref):
        u = _f(p_ref[...])
        uh = jnp.where(pl.program_id(0) == 0, 0.0, _f(h_ref[...]))
        wv = w_ref[...]
        acc = wv[3:4] * u + b_ref[...]
        for k in range(1, SSM_CONV):
            acc = acc + wv[3 - k:4 - k] * _shift_down(u, uh, k)
        o_ref[...] = acc

    return pl.pallas_call(
        body, name="ssmconv_fwd", grid=(L // tl,),
        in_specs=[main, before, pl.BlockSpec(w.shape, lambda i: (0, 0)), pl.BlockSpec(bias.shape, lambda i: (0, 0))],
        out_specs=pl.BlockSpec((tl, C), lambda i: (i, 0)),
        out_shape=jax.ShapeDtypeStruct((L, C), F32), compiler_params=_params(("arbitrary",)),
    )(P, P, w, bias)


def _ssmconv_bwd(P, dpre, w, tl=512):
    L = P.shape[0]
    tl = min(tl, L)
    C = SSM_CONV_DIM
    main, before, _ = _conv_specs(L, tl, C, P_XBC // C)
    dmain, _, dafter = _conv_specs(L, tl, C, 0)
    n = L // tl

    def body(p_ref, h_ref, d_ref, dn_ref, w_ref, o_ref, dw0, dw1, dw2, dw3, db):
        i = pl.program_id(0)
        u = _f(p_ref[...])
        uh = jnp.where(i == 0, 0.0, _f(h_ref[...]))
        d = d_ref[...]
        dn = jnp.where(i == n - 1, 0.0, dn_ref[...])
        wv = w_ref[...]
        du = wv[3:4] * d
        for k in range(1, SSM_CONV):
            du = du + wv[3 - k:4 - k] * _shift_up(d, dn, k)
        o_ref[...] = du.astype(o_ref.dtype)

        @pl.when(i == 0)
        def _():
            for r in (dw0, dw1, dw2, dw3, db):
                r[...] = jnp.zeros_like(r)

        for k, r in ((3, dw0), (2, dw1), (1, dw2)):
            r[...] += jnp.sum(d * _shift_down(u, uh, k), axis=0, keepdims=True)
        dw3[...] += jnp.sum(d * u, axis=0, keepdims=True)
        db[...] += jnp.sum(d, axis=0, keepdims=True)

    vec = pl.BlockSpec((1, C), lambda i: (0, 0))
    return pl.pallas_call(
        body, name="ssmconv_bwd", grid=(n,),
        in_specs=[main, before, dmain, dafter, pl.BlockSpec(w.shape, lambda i: (0, 0))],
        out_specs=[pl.BlockSpec((tl, C), lambda i: (i, 0))] + [vec] * 5,
        out_shape=[jax.ShapeDtypeStruct((L, C), BF16)] + [jax.ShapeDtypeStruct((1, C), F32)] * 5,
        compiler_params=_params(("arbitrary",)),
    )(P, P, dpre, dpre, w)


def _dot_nt(a, b):
    return lax.dot_general(a, b, (((1,), (1,)), ((), ())), preferred_element_type=F32)


def _dot_tn(a, b):
    return lax.dot_general(a, b, (((0,), (0,)), ((), ())), preferred_element_type=F32)


def _split3(x):
    hi = x.astype(BF16)
    r = x - hi.astype(F32)
    mid = r.astype(BF16)
    return hi, mid, (r - mid.astype(F32)).astype(BF16)


@jax.custom_vjp
def _xm01(x, m):
    return sum(jnp.dot(t, m, preferred_element_type=F32) for t in _split3(x))


def _xm01_fwd(x, m):
    return _xm01(x, m), m


def _xm01_bwd(m, g):
    return sum(_dot_nt(t, m) for t in _split3(g)), jnp.zeros_like(m)


_xm01.defvjp(_xm01_fwd, _xm01_bwd)


@jax.custom_vjp
def _m01x(m, x):
    return sum(jnp.dot(m, t, preferred_element_type=F32) for t in _split3(x))


def _m01x_fwd(m, x):
    return _m01x(m, x), m


def _m01x_bwd(m, g):
    return jnp.zeros_like(m), sum(_dot_tn(m, t) for t in _split3(g))


_m01x.defvjp(_m01x_fwd, _m01x_bwd)


def _ssd_chunk(pre, dtr, s_prev, dtb, alog):
    T = pre.shape[0]
    act = _silu(pre)
    xs, bm, cm = act[:, :SSM_INNER], act[:, SSM_INNER:SSM_INNER + 128], act[:, SSM_INNER + 128:]
    lane = lax.broadcasted_iota(jnp.int32, (1, LANES), 1)
    dt = jnp.where(lane < SSM_HEADS, _softplus(dtr + dtb), 0.0)
    a = dt * (-jnp.exp(alog))
    ri = lax.broadcasted_iota(jnp.int32, (T, T), 0)
    ci = lax.broadcasted_iota(jnp.int32, (T, T), 1)
    causal = ci <= ri
    a_cs = _m01x(causal.astype(BF16), a)
    eh = lax.broadcasted_iota(jnp.int32, (LANES, SSM_INNER), 0)
    ej = lax.broadcasted_iota(jnp.int32, (LANES, SSM_INNER), 1)
    expand = (lax.shift_right_logical(ej, 6) == eh).astype(BF16)
    dt_full = _xm01(dt, expand)
    acs_full = _xm01(a_cs, expand)
    alast_full = acs_full[T - 1:T, :]
    xdt = xs * dt_full
    a_cs_t = a_cs.T
    ys, s_new = [], []
    for g in range(2):
        in_group = lax.shift_right_logical(lane, 6) == g
        cg = jnp.where(in_group, cm, 0.0).astype(BF16)
        bg = jnp.where(in_group, bm, 0.0).astype(BF16)
        scores = _dot_nt(cg, bg)
        for pp in range(2):
            hp = 2 * g + pp
            cols = slice(hp * LANES, (hp + 1) * LANES)
            xp, acsp = xdt[:, cols], acs_full[:, cols]
            per_head = []
            for hh in range(2):
                h = 2 * hp + hh
                decay = jnp.exp(jnp.where(causal, a_cs[:, h:h + 1] - a_cs_t[h:h + 1, :], -jnp.inf))
                per_head.append(jnp.dot((scores * decay).astype(BF16), xp.astype(BF16), preferred_element_type=F32))
            y_diag = jnp.where(lane < SSM_STATE, per_head[0], per_head[1])
            sp = s_prev[hp * LANES:(hp + 1) * LANES, :]
            y_off = jnp.dot(cg, sp.astype(BF16), preferred_element_type=F32) * jnp.exp(acsp)
            ys.append(y_diag + y_off)
            to_end = jnp.exp(alast_full[:, cols] - acsp)
            s_new.append(sp * jnp.exp(alast_full[:, cols]) + _dot_tn(bg, (xp * to_end).astype(BF16)))
    return jnp.concatenate(ys, axis=1), jnp.concatenate(s_new, axis=0)


def _ssd_fwd(pre, P, dtb, alog):
    L = pre.shape[0]
    T = min(SSM_CHUNK, L)
    nc = L // T

    def body(pre_ref, dt_ref, dtb_ref, al_ref, y_ref, st_ref, s_scr):
        @pl.when(pl.program_id(0) == 0)
        def _():
            s_scr[...] = jnp.zeros_like(s_scr)

        st_ref[0] = s_scr[...]
        y, s = _ssd_chunk(pre_ref[...], _f(dt_ref[...]), s_scr[...], dtb_ref[...], al_ref[...])
        y_ref[...] = y
        s_scr[...] = s

    vec = pl.BlockSpec((1, LANES), lambda i: (0, 0))
    return pl.pallas_call(
        body, name="ssd_fwd", grid=(nc,),
        in_specs=[pl.BlockSpec((T, SSM_CONV_DIM), lambda i: (i, 0)), pl.BlockSpec((T, LANES), lambda i: (i, P_DT // LANES)),
                  vec, vec],
        out_specs=[pl.BlockSpec((T, SSM_INNER), lambda i: (i, 0)), pl.BlockSpec((1, 512, LANES), lambda i: (i, 0, 0))],
        out_shape=[jax.ShapeDtypeStruct((L, SSM_INNER), F32), jax.ShapeDtypeStruct((nc, 512, LANES), F32)],
        scratch_shapes=[pltpu.VMEM((512, LANES), F32)], compiler_params=_params(("arbitrary",)),
    )(pre, P, dtb, alog)


def _ssd_bwd(pre, P, states, dy, dxs_extra, dtb, alog):
    L = pre.shape[0]
    T = min(SSM_CHUNK, L)
    nc = L // T

    def body(pre_ref, dt_ref, st_ref, dy_ref, dx_ref, dtb_ref, al_ref, dpre_ref, ddt_ref, ddtb_ref, dal_ref, ds_scr):
        @pl.when(pl.program_id(0) == 0)
        def _():
            ds_scr[...] = jnp.zeros_like(ds_scr)
            ddtb_ref[...] = jnp.zeros_like(ddtb_ref)
            dal_ref[...] = jnp.zeros_like(dal_ref)

        _, vjp = jax.vjp(_ssd_chunk, pre_ref[...], _f(dt_ref[...]), st_ref[0], dtb_ref[...], al_ref[...])
        dpre, ddt, ds, ddtb, dal = vjp((dy_ref[...], ds_scr[...]))
        dpre_ref[:, :SSM_INNER] = dpre[:, :SSM_INNER] + dx_ref[...]
        dpre_ref[:, SSM_INNER:] = dpre[:, SSM_INNER:]
        ddt_ref[:, :LANES] = ddt.astype(ddt_ref.dtype)
        ddt_ref[:, LANES:] = jnp.zeros((T, DT_PAD - LANES), ddt_ref.dtype)
        ds_scr[...] = ds
        ddtb_ref[...] += ddtb
        dal_ref[...] += dal

    vec = pl.BlockSpec((1, LANES), lambda i: (0, 0))
    rev = lambda i: (nc - 1 - i, 0)
    return pl.pallas_call(
        body, name="ssd_bwd", grid=(nc,),
        in_specs=[pl.BlockSpec((T, SSM_CONV_DIM), rev), pl.BlockSpec((T, LANES), lambda i: (nc - 1 - i, P_DT // LANES)),
                  pl.BlockSpec((1, 512, LANES), lambda i: (nc - 1 - i, 0, 0)),
                  pl.BlockSpec((T, SSM_INNER), rev), pl.BlockSpec((T, SSM_INNER), rev), vec, vec],
        out_specs=[pl.BlockSpec((T, SSM_CONV_DIM), rev), pl.BlockSpec((T, DT_PAD), rev), vec, vec],
        out_shape=[jax.ShapeDtypeStruct((L, SSM_CONV_DIM), F32), jax.ShapeDtypeStruct((L, DT_PAD), BF16),
                   jax.ShapeDtypeStruct((1, LANES), F32), jax.ShapeDtypeStruct((1, LANES), F32)],
        scratch_shapes=[pltpu.VMEM((512, LANES), F32)], compiler_params=_params(("arbitrary",)),
    )(pre, P, states, dy, dxs_extra, dtb, alog)


def _sb_scores(qm, kb, later, strict, mask):
    z = _dot_nt(qm, kb)
    lk = jnp.minimum(-z, 0.0) - jnp.log(1.0 + jnp.exp(-jnp.abs(z)))
    if mask is not None:
        lk = jnp.where(mask, lk, 0.0)
    log_a = z + lk + jnp.dot(lk.astype(BF16), strict, preferred_element_type=F32) + later
    if mask is not None:
        log_a = jnp.where(mask, log_a, -jnp.inf)
    return z, lk, log_a


def _dot_split(x, m):
    hi = x.astype(BF16)
    lo = (x - hi.astype(F32)).astype(BF16)
    return jnp.dot(hi, m, preferred_element_type=F32) + jnp.dot(lo, m, preferred_element_type=F32)


def _sb_setup(q_ref, i, tq, tk):
    lane = lax.broadcasted_iota(jnp.int32, (1, LANES), 1)
    first = lane < SB_HEAD_DIM
    q = q_ref[...] * (SB_HEAD_DIM ** -0.5)
    qms = (jnp.where(first, q, jnp.zeros_like(q)), jnp.where(first, jnp.zeros_like(q), q))
    j0 = lax.div(i * tq, tk)
    ri = lax.broadcasted_iota(jnp.int32, (tq, tk), 0)
    ci = lax.broadcasted_iota(jnp.int32, (tq, tk), 1)
    diag_mask = (ci + (j0 * tk - i * tq)) < ri
    kr = lax.broadcasted_iota(jnp.int32, (tk, tk), 0)
    kc = lax.broadcasted_iota(jnp.int32, (tk, tk), 1)
    strict = (kr > kc).astype(BF16)
    return first, qms, j0, diag_mask, strict


def _sb_continue(c):
    return jnp.logical_and(c[0] >= 0, jnp.maximum(jnp.max(c[1][0]), jnp.max(c[1][1])) > SB_LOG_CUTOFF)


def _sb_fwd(P):
    L = P.shape[0]
    tq, tk = min(SB_TQ, L), min(SB_TK, L)
    nq = L // tq
    qb = P_B // LANES

    subs = SB_SUBS if L % (SB_SUBS * tq) == 0 else 1

    def body(q_ref, k_ref, v_ref, o_ref, of_ref):
        zero, zacc = jnp.zeros((tq, 1), F32), jnp.zeros((tq, LANES), F32)
        walks = []
        for s in range(subs):
            rows = pl.ds(s * tq, tq)
            first, qms, j0, diag_mask, strict = _sb_setup(q_ref.at[rows, :], pl.program_id(1) * subs + s, tq, tk)

            def tile(h, j, later, acc, mask=None, valid=None, qms=qms, strict=strict):
                off = pl.multiple_of(j * tk, tk)
                gate = later if valid is None else jnp.where(valid, later, -jnp.inf)
                _, lk, log_a = _sb_scores(qms[h], k_ref[pl.ds(off, tk), :], gate, strict, mask)
                acc = acc + jnp.dot(jnp.exp(log_a).astype(BF16), v_ref[pl.ds(off, tk), :], preferred_element_type=F32)
                total = jnp.sum(lk, axis=1, keepdims=True)
                return later + (total if valid is None else jnp.where(valid, total, 0.0)), acc

            state = []
            for h in range(2):
                carry = tile(h, j0, zero, zacc, mask=diag_mask)
                for n in range(1, SB_STRAIGHT):
                    carry = tile(h, jnp.maximum(j0 - n, 0), *carry, valid=j0 >= n)
                state.append(carry)
            walks.append((rows, first, j0, tile, state))

        for rows, first, j0, tile, state in walks:
            def tail(c, tile=tile):
                res = [tile(h, c[0], c[1][h], c[2][h]) for h in range(2)]
                return c[0] - 1, (res[0][0], res[1][0]), (res[0][1], res[1][1])

            _, _, accs = lax.while_loop(
                _sb_continue, tail, (j0 - SB_STRAIGHT, (state[0][0], state[1][0]), (state[0][1], state[1][1])))
            out = jnp.where(first, accs[0], accs[1])
            o_ref[rows, :] = out.astype(o_ref.dtype)
            of_ref[rows, :] = out

    nq = nq // subs
    tile_spec = pl.BlockSpec((subs * tq, LANES), lambda p, i: (i, p))
    return pl.pallas_call(
        body, name="sb_fwd", grid=(2, nq),
        in_specs=[pl.BlockSpec((subs * tq, LANES), lambda p, i: (i, qb + p)),
                  pl.BlockSpec((L, LANES), lambda p, i: (0, qb + 2 + p)),
                  pl.BlockSpec((L, LANES), lambda p, i: (0, qb + 4 + p))],
        out_specs=[tile_spec, tile_spec],
        out_shape=[jax.ShapeDtypeStruct((L, 2 * LANES), BF16), jax.ShapeDtypeStruct((L, 2 * LANES), F32)],
        compiler_params=_params(("arbitrary", "arbitrary")),
    )(P, P, P)


def _sb_bwd(P, dyb, yb32):
    L = P.shape[0]
    tq, tk = min(SB_TQ, L), min(SB_TK, L)
    nq = L // tq
    qb = P_B // LANES

    def body(q_ref, k_ref, v_ref, do_ref, of_ref, dq_ref, dk_ref, dv_ref):
        i = pl.program_id(1)
        first, qms, j0, diag_mask, strict = _sb_setup(q_ref, i, tq, tk)

        @pl.when(i == 0)
        def _():
            dk_ref[...] = jnp.zeros_like(dk_ref)
            dv_ref[...] = jnp.zeros_like(dv_ref)

        do = do_ref[...]
        doms = (jnp.where(first, do, jnp.zeros_like(do)), jnp.where(first, jnp.zeros_like(do), do))
        prod = _f(do) * of_ref[...]
        totals = (jnp.sum(jnp.where(first, prod, 0.0), axis=1, keepdims=True),
                  jnp.sum(jnp.where(first, 0.0, prod), axis=1, keepdims=True))

        def tile(h, j, later, later_g, acc, mask=None, valid=None):
            off = pl.multiple_of(j * tk, tk)
            kb, vb = k_ref[pl.ds(off, tk), :], v_ref[pl.ds(off, tk), :]
            gate = later if valid is None else jnp.where(valid, later, -jnp.inf)
            z, lk, log_a = _sb_scores(qms[h], kb, gate, strict, mask)
            att = jnp.exp(log_a).astype(BF16)
            g = _f(att) * _dot_nt(doms[h], vb)
            before = totals[h] - later_g
            if valid is not None:
                before = jnp.where(valid, before, 0.0)
            dz = g - (before - _dot_split(g, strict)) * jnp.exp(z + lk)
            if mask is not None:
                dz = jnp.where(mask, dz, 0.0)
            dzb = dz.astype(BF16)
            rows = jnp.sum(lk, axis=1, keepdims=True)
            carry = (later + (rows if valid is None else jnp.where(valid, rows, 0.0)),
                     later_g + jnp.sum(g, axis=1, keepdims=True), acc + jnp.dot(dzb, kb, preferred_element_type=F32))
            return carry, _dot_tn(dzb, qms[h]), _dot_tn(att, doms[h])

        def tail(c):
            off = pl.multiple_of(c[0] * tk, tk)
            (c0, dk0, dv0), (c1, dk1, dv1) = [tile(h, c[0], c[1][h], c[2][h], c[3][h]) for h in range(2)]
            dk_ref[pl.ds(off, tk), :] += dk0 + dk1
            dv_ref[pl.ds(off, tk), :] += dv0 + dv1
            return (c[0] - 1,) + tuple(zip(c0, c1))

        zero, zacc = jnp.zeros((tq, 1), F32), jnp.zeros((tq, LANES), F32)
        blocks = [j0] + [jnp.maximum(j0 - n, 0) for n in range(1, SB_STRAIGHT)]
        carries, dks, dvs = [], [], []
        for h in range(2):
            carry, dk, dv = tile(h, j0, zero, zero, zacc, mask=diag_mask)
            dks.append([dk])
            dvs.append([dv])
            for n in range(1, SB_STRAIGHT):
                carry, dk, dv = tile(h, blocks[n], *carry, valid=j0 >= n)
                dks[h].append(dk)
                dvs[h].append(dv)
            carries.append(carry)
        for n, j in enumerate(blocks):
            off = pl.multiple_of(j * tk, tk)
            dk_ref[pl.ds(off, tk), :] += dks[0][n] + dks[1][n]
            dv_ref[pl.ds(off, tk), :] += dvs[0][n] + dvs[1][n]
        accs = lax.while_loop(_sb_continue, tail, (j0 - SB_STRAIGHT,) + tuple(zip(carries[0], carries[1])))[3]
        dq_ref[...] = jnp.where(first, accs[0], accs[1]) * (SB_HEAD_DIM ** -0.5)

    full = pl.BlockSpec((L, LANES), lambda p, i: (0, p))
    tile_spec = pl.BlockSpec((tq, LANES), lambda p, i: (i, p))
    return pl.pallas_call(
        body, name="sb_bwd", grid=(2, nq),
        in_specs=[pl.BlockSpec((tq, LANES), lambda p, i: (i, qb + p)),
                  pl.BlockSpec((L, LANES), lambda p, i: (0, qb + 2 + p)),
                  pl.BlockSpec((L, LANES), lambda p, i: (0, qb + 4 + p)), tile_spec, tile_spec],
        out_specs=[tile_spec, full, full],
        out_shape=[jax.ShapeDtypeStruct((L, 2 * LANES), F32)] * 3,
        compiler_params=_params(("arbitrary", "arbitrary")),
    )(P, P, P, dyb, yb32)


MOD_SHARD = N_MOD * D_MODEL // N_CHIPS


def _mod_fwd(c_all, mod_w, mod_b_sh):
    tn = 512

    def body(c_ref, w_ref, b_ref, o_ref):
        o_ref[0] = jnp.dot(_silu(c_ref[...]), w_ref[0], precision=HIGHEST, preferred_element_type=F32) + b_ref[0]

    return pl.pallas_call(
        body, name="mod_fwd", grid=(DEPTH, MOD_SHARD // tn),
        in_specs=[pl.BlockSpec((N_DEV, D_MODEL), lambda l, j: (0, 0)),
                  pl.BlockSpec((1, D_MODEL, tn), lambda l, j: (l, 0, j)),
                  pl.BlockSpec((1, 1, tn), lambda l, j: (l, 0, j))],
        out_specs=pl.BlockSpec((1, N_DEV, tn), lambda l, j: (l, 0, j)),
        out_shape=jax.ShapeDtypeStruct((DEPTH, N_DEV, MOD_SHARD), F32),
        compiler_params=_params(("arbitrary", "arbitrary")),
    )(c_all, mod_w, mod_b_sh)


def _mod_bwd(c_all, dmod_sh):
    tn = 512

    def body(c_ref, d_ref, o_ref):
        o_ref[0] = lax.dot_general(_silu(c_ref[...]), d_ref[0], (((0,), (0,)), ((), ())), precision=HIGHEST,
                                   preferred_element_type=F32)

    return pl.pallas_call(
        body, name="mod_bwd", grid=(DEPTH, MOD_SHARD // tn),
        in_specs=[pl.BlockSpec((N_DEV, D_MODEL), lambda l, j: (0, 0)),
                  pl.BlockSpec((1, N_DEV, tn), lambda l, j: (l, 0, j))],
        out_specs=pl.BlockSpec((1, D_MODEL, tn), lambda l, j: (l, 0, j)),
        out_shape=jax.ShapeDtypeStruct((DEPTH, D_MODEL, MOD_SHARD), F32),
        compiler_params=_params(("arbitrary", "arbitrary")),
    )(c_all, dmod_sh)


def _row_tile(rows, cap):
    if rows <= cap:
        return rows
    best = None
    for t in range(8, cap + 1, 8):
        if rows % t == 0:
            best = t
    assert best is not None, (rows, cap)
    return best


def _adamw(name, w, gs, m, v, tr=256):
    R, W = w.shape
    by_layer = any(isinstance(t, tuple) for t in gs)
    tr = _row_tile(R // 2 if by_layer else R, tr)
    per = (R // 2) // tr

    flat, specs = [], []
    for t in gs:
        if isinstance(t, tuple):
            flat += list(t)
            specs += [pl.BlockSpec((tr, W), lambda i: (jnp.minimum(i, per - 1), 0)),
                      pl.BlockSpec((tr, W), lambda i: (jnp.maximum(i - per, 0), 0))]
        else:
            flat.append(t)
            specs.append(pl.BlockSpec((tr, W), lambda i: (i, 0)))
    ng = len(flat)

    def body(*refs):
        w_ref, g_refs, (m_ref, v_ref) = refs[0], list(refs[1:1 + ng]), refs[1 + ng:3 + ng]
        g_out, d_out, m_out, v_out = refs[3 + ng:]
        g = None
        for t in gs:
            if isinstance(t, tuple):
                lo, hi = g_refs.pop(0), g_refs.pop(0)
                term = jnp.where(pl.program_id(0) < per, lo[...], hi[...])
            else:
                term = g_refs.pop(0)[...]
            g = term if g is None else g + term
        mm = ADAM_B1 * m_ref[...] + (1.0 - ADAM_B1) * g
        vv = ADAM_B2 * v_ref[...] + (1.0 - ADAM_B2) * (g * g)
        m_hat = mm / (1.0 - ADAM_B1 ** ADAM_STEP)
        v_hat = vv / (1.0 - ADAM_B2 ** ADAM_STEP)
        g_out[...] = g
        d_out[...] = -ADAM_LR * (m_hat / (jnp.sqrt(v_hat) + ADAM_EPS) + ADAM_WD * w_ref[...])
        m_out[...] = mm
        v_out[...] = vv

    spec = pl.BlockSpec((tr, W), lambda i: (i, 0))
    return pl.pallas_call(
        body, name=name, grid=(R // tr,), in_specs=[spec] + specs + [spec, spec], out_specs=[spec] * 4,
        out_shape=[jax.ShapeDtypeStruct((R, W), F32)] * 4, compiler_params=_params(("arbitrary",)),
    )(w, *flat, m, v)


def _sum_slots(name, a, tr=256):
    n, R, W = a.shape
    tr = _row_tile(R, tr)

    def body(a_ref, o_ref):
        acc = _f(a_ref[0])
        for j in range(1, n):
            acc = acc + _f(a_ref[j])
        o_ref[...] = acc

    return pl.pallas_call(
        body, name=name, grid=(R // tr,), in_specs=[pl.BlockSpec((n, tr, W), lambda i: (0, i, 0))],
        out_specs=pl.BlockSpec((tr, W), lambda i: (i, 0)), out_shape=jax.ShapeDtypeStruct((R, W), F32),
        compiler_params=_params(("arbitrary",)),
    )(a)


def _here():
    return lax.axis_index("x"), lax.axis_index("y"), lax.axis_index("c")


def _flip(v, d):
    return 1 - v if d else v


def _allgather_small(name, buf):
    R = buf.shape[0]
    rel = [(dx, dy, dc) for dx in (0, 1) for dy in (0, 1) for dc in (0, 1)][1:]

    def body(x_ref, o_ref, send, recv, lsem):
        x, y, c = _here()
        me = 4 * x + 2 * y + c
        mine = pltpu.make_async_copy(x_ref, o_ref.at[me], lsem)
        mine.start()

        def copy(k, slot):
            dx, dy, dc = rel[k]
            return pltpu.make_async_remote_copy(
                src_ref=x_ref, dst_ref=o_ref.at[slot], send_sem=send.at[k], recv_sem=recv.at[k],
                device_id=(_flip(x, dx), _flip(y, dy), _flip(c, dc)), device_id_type=MESH_ID)

        sent = [copy(k, me) for k in range(len(rel))]
        for cp in sent:
            cp.start()
        for k, (dx, dy, dc) in enumerate(rel):
            copy(k, 4 * _flip(x, dx) + 2 * _flip(y, dy) + _flip(c, dc)).wait_recv()
        for cp in sent:
            cp.wait_send()
        mine.wait()

    return pl.pallas_call(
        body, name=name, out_shape=jax.ShapeDtypeStruct((N_DEV, R, LANES), F32),
        in_specs=[pl.BlockSpec(memory_space=pltpu.VMEM)], out_specs=pl.BlockSpec(memory_space=pltpu.VMEM),
        scratch_shapes=[pltpu.SemaphoreType.DMA((7,)), pltpu.SemaphoreType.DMA((7,)), pltpu.SemaphoreType.DMA],
    )(buf)


CHIP_REL = [(1, 0), (0, 1), (1, 1)]


class _Side:
    def __init__(self, arrays, out_shapes, scratch, start, finish):
        self.arrays, self.out_shapes, self.scratch, self.start, self.finish = arrays, out_shapes, scratch, start, finish


def _chip_of(k):
    x, y, _ = _here()
    dx, dy = CHIP_REL[k]
    return _flip(x, dx), _flip(y, dy)


def _scatter_side(arrays):
    n = len(arrays)

    def parts(ins, outs, sems):
        send, recv, lsem = sems
        x, y, c = _here()
        s = 2 * x + y

        def copy(w, k, mine):
            px, py = _chip_of(k)
            return pltpu.make_async_remote_copy(
                src_ref=ins[w].at[2 * px + py], dst_ref=outs[w].at[s if mine else 2 * px + py],
                send_sem=send.at[3 * w + k], recv_sem=recv.at[3 * w + k], device_id=(px, py, c), device_id_type=MESH_ID)

        local = [pltpu.make_async_copy(ins[w].at[s], outs[w].at[s], lsem.at[w]) for w in range(n)]
        return copy, local

    def start(ins, outs, sems):
        copy, local = parts(ins, outs, sems)
        for cp in local:
            cp.start()
        for w in range(n):
            for k in range(3):
                copy(w, k, True).start()

    def finish(ins, outs, sems):
        copy, local = parts(ins, outs, sems)
        for w in range(n):
            for k in range(3):
                copy(w, k, False).wait_recv()
        for w in range(n):
            for k in range(3):
                copy(w, k, True).wait_send()
        for cp in local:
            cp.wait()

    scratch = [pltpu.SemaphoreType.DMA((3 * n,)), pltpu.SemaphoreType.DMA((3 * n,)), pltpu.SemaphoreType.DMA((n,))]
    return _Side(arrays, [jax.ShapeDtypeStruct(a.shape, a.dtype) for a in arrays], scratch, start, finish)


def _gather_side(shards):
    n = len(shards)

    def parts(ins, outs, sems):
        send, recv, fsend, frecv, lsem = sems
        x, y, c = _here()
        s = 2 * x + y

        def half(ref, w, which):
            rows = shards[w].shape[0] // 2
            return ref.at[pl.ds(pl.multiple_of(which * rows, 16), rows)]

        def over_ici(w, k, mine):
            px, py = _chip_of(k)
            return pltpu.make_async_remote_copy(
                src_ref=half(ins[w], w, c), dst_ref=half(outs[w].at[s if mine else 2 * px + py], w, c),
                send_sem=send.at[3 * w + k], recv_sem=recv.at[3 * w + k], device_id=(px, py, c), device_id_type=MESH_ID)

        def to_sibling(w, k, which):
            px, py = _chip_of(k)
            part = half(outs[w].at[2 * px + py], w, which)
            return pltpu.make_async_remote_copy(
                src_ref=part, dst_ref=part, send_sem=fsend.at[3 * w + k], recv_sem=frecv.at[3 * w + k],
                device_id=(x, y, 1 - c), device_id_type=MESH_ID)

        local = [pltpu.make_async_copy(ins[w], outs[w].at[s], lsem.at[w]) for w in range(n)]
        return c, over_ici, to_sibling, local

    def start(ins, outs, sems):
        _, over_ici, _, local = parts(ins, outs, sems)
        for cp in local:
            cp.start()
        for w in range(n):
            for k in range(3):
                over_ici(w, k, True).start()

    def finish(ins, outs, sems):
        c, over_ici, to_sibling, local = parts(ins, outs, sems)
        for w in range(n):
            for k in range(3):
                over_ici(w, k, False).wait_recv()
                to_sibling(w, k, c).start()
        for w in range(n):
            for k in range(3):
                to_sibling(w, k, 1 - c).wait_recv()
        for w in range(n):
            for k in range(3):
                over_ici(w, k, True).wait_send()
                to_sibling(w, k, c).wait_send()
        for cp in local:
            cp.wait()

    scratch = [pltpu.SemaphoreType.DMA((3 * n,))] * 4 + [pltpu.SemaphoreType.DMA((n,))]
    return _Side(shards, [jax.ShapeDtypeStruct((N_CHIPS,) + a.shape, a.dtype) for a in shards], scratch, start, finish)


def _sibling_exchange(name, arrays):
    n = len(arrays)

    def body(*refs):
        ins, outs = refs[:n], refs[n:2 * n]
        send, recv = refs[2 * n:]
        x, y, c = _here()
        cps = [pltpu.make_async_remote_copy(src_ref=ins[w], dst_ref=outs[w], send_sem=send.at[w], recv_sem=recv.at[w],
                                            device_id=(x, y, 1 - c), device_id_type=MESH_ID) for w in range(n)]
        for cp in cps:
            cp.start()
        for cp in cps:
            cp.wait()

    any_spec = pl.BlockSpec(memory_space=pl.ANY)
    return pl.pallas_call(
        body, name=name, out_shape=[jax.ShapeDtypeStruct(a.shape, a.dtype) for a in arrays],
        in_specs=[any_spec] * n, out_specs=[any_spec] * n,
        scratch_shapes=[pltpu.SemaphoreType.DMA((n,)), pltpu.SemaphoreType.DMA((n,))],
    )(*arrays)


def _pack(arrs):
    flat = jnp.concatenate([a.reshape(-1).astype(F32) for a in arrs])
    n = flat.shape[0]
    rows = -(-n // (8 * LANES)) * 8
    return jnp.pad(flat, (0, rows * LANES - n)).reshape(rows, LANES)


def _unpack(buf, shapes):
    lead = buf.shape[:-2]
    flat = buf.reshape(lead + (-1,))
    out, off = [], 0
    for s in shapes:
        n = 1
        for d in s:
            n *= d
        out.append(flat[..., off:off + n].reshape(lead + tuple(s)))
        off += n
    return out


def _pad_w_in(w):
    return jnp.concatenate([w[:, :2048], w[:, 2816:2824], jnp.zeros((w.shape[0], P_XBC - P_DT - 8), w.dtype),
                            w[:, 2048:2816], w[:, 2824:]], axis=1)


def _unpad_w_in(g):
    return jnp.concatenate([g[:, :P_DT], g[:, P_XBC:P_G], g[:, P_DT:P_DT + 8], g[:, P_G:]], axis=1)


FFN_HALF = FFN_HIDDEN // 2


def _ffn_in_cols(w):
    h = FFN_HALF
    return jnp.concatenate([w[:, :h], w[:, 2 * h:3 * h], w[:, h:2 * h], w[:, 3 * h:]], axis=1)


def _row(v):
    return v.reshape(1, -1)


BIG = (("w_in", 2), ("w_sc_out", 2), ("w_sb_out", 2), ("w_ssm_out", 2), ("w_o", 1), ("w_ffn_in", 2), ("w_ffn_out", 1))
SMALL = ("mod_b", "g_pre_mix", "g_post_mix", "g_pre_ffn", "g_post_ffn", "sc_conv_w", "ssm_conv_w", "ssm_conv_b",
         "ssm_dt_bias", "ssm_a_log", "ssm_d", "ssm_norm_w")
WEIGHT_ORDER = ("mod_w", "mod_b", "g_pre_mix", "g_post_mix", "g_pre_ffn", "g_post_ffn", "w_in", "sc_conv_w",
                "ssm_conv_w", "ssm_conv_b", "ssm_dt_bias", "ssm_a_log", "ssm_d", "ssm_norm_w", "w_sc_out", "w_sb_out",
                "w_ssm_out", "w_o", "w_ffn_in", "w_ffn_out")


def _mm_mid(name, a, w, x, vecs):
    return _mm_epi(name, a, w, "nn", D_MODEL, [(x, D_MODEL)] + [(v, None) for v in vecs],
                   lambda p, x, *v: (p,) + tuple(_mid(x, p, *v)), [(D_MODEL, BF16), (D_MODEL, F32), (D_MODEL, BF16)])


def _layer_fwd(l, x_in, h, W, V, sides, next_vecs):
    S = {"x_in": x_in, "h": h}
    side, handler = sides.get("in_proj", (None, None))
    P = _mm(f"in_proj{l}", h, W["w_in"], "nn", BF16, tm=2048, tn_cap=1024, side=side)
    if side:
        handler(P[1:])
        P = P[0]
    S["P"] = P
    S["ya"] = _shortconv_fwd(P, V["sc_w"])
    S["yb"], S["yb32"] = _sb_fwd(P)
    S["pre"] = _ssmconv_fwd(P, V["ssm_w"], V["ssm_b"])
    S["y_ssd"], S["states"] = _ssd_fwd(S["pre"], P, V["dtb"], V["alog"])
    S["yc"] = _rowwise(f"ssm_post{l}", lambda y, px, z, d, nw: _ssm_post(y, px, _f(z), d, nw),
                       [(S["y_ssd"], SSM_INNER, 0), (S["pre"], SSM_INNER, 0), (P, SSM_INNER, P_Z // SSM_INNER)],
                       [V["d_full"], V["norm_w"]], [(SSM_INNER, BF16)])[0]
    S["merged"] = _merge_fwd(f"merge{l}", P, [S["ya"], S["yb"], S["yc"]],
                             [W["w_sc_out"], W["w_sb_out"], W["w_ssm_out"]])
    S["mix"], S["x1"], S["h2"] = _mm_mid(f"w_o{l}", S["merged"], W["w_o"], x_in, V["mid_mix"])
    side, handler = sides.get("ffn_in", (None, None))
    res = _mm_epi(f"ffn_in{l}", S["h2"], W["w_ffn_in"], "nn", 2 * FFN_HALF, [],
                  lambda p: (p, _swiglu(p[:, :FFN_HALF], p[:, FFN_HALF:])),
                  [(2 * FFN_HALF, BF16), (FFN_HALF, BF16)], side=side)
    S["GU"], S["act"] = res[0], res[1]
    if side:
        handler(res[2:])
    if next_vecs is None:
        S["f"] = _mm(f"ffn_out{l}", S["act"], W["w_ffn_out"], "nn", BF16)
    else:
        S["f"], S["x_next"], S["h_next"] = _mm_mid(f"ffn_out{l}", S["act"], W["w_ffn_out"], S["x1"], next_vecs)
    return S


BRANCH_WIDTHS = (SC_WIDTH, 256, SSM_INNER)


def _branch_specs(tm):
    gb = P_G // D_MODEL
    gates = [pl.BlockSpec((tm, D_MODEL), functools.partial(lambda i, cb: (i, cb), cb=gb + k)) for k in range(3)]
    ys = [pl.BlockSpec((tm, w), lambda i: (i, 0)) for w in BRANCH_WIDTHS]
    ws = [pl.BlockSpec((w, D_MODEL), lambda i: (0, 0)) for w in BRANCH_WIDTHS]
    return gates, ys, ws


def _merge_fwd(name, P, ys, ws, tm=512):
    L = P.shape[0]
    tm = min(tm, L)
    gates, y_specs, w_specs = _branch_specs(tm)

    def body(ga, gb, gc, ya, yb, yc, wa, wb, wc, o_ref):
        acc = None
        for g_ref, y_ref, w_ref in ((ga, ya, wa), (gb, yb, wb), (gc, yc, wc)):
            t = jax.nn.sigmoid(_f(g_ref[...])) * jnp.dot(y_ref[...], w_ref[...], preferred_element_type=F32)
            acc = t if acc is None else acc + t
        o_ref[...] = acc.astype(o_ref.dtype)

    return pl.pallas_call(
        body, name=name, grid=(L // tm,), in_specs=gates + y_specs + w_specs,
        out_specs=pl.BlockSpec((tm, D_MODEL), lambda i: (i, 0)), out_shape=jax.ShapeDtypeStruct((L, D_MODEL), BF16),
        compiler_params=_params(("arbitrary",)),
    )(P, P, P, *ys, *ws)


def _merge_bwd(name, P, ys, ws, dmerged, tm=512):
    L = P.shape[0]
    tm = min(tm, L)
    gates, y_specs, w_specs = _branch_specs(tm)

    def body(ga, gb, gc, ya, yb, yc, wa, wb, wc, dm_ref, dg_ref, dya, dyb, dyc, gwa, gwb, gwc):
        @pl.when(pl.program_id(0) == 0)
        def _():
            for r in (gwa, gwb, gwc):
                r[...] = jnp.zeros_like(r)

        dm = _f(dm_ref[...])
        for k, (g_ref, y_ref, w_ref, dy_ref, gw_ref) in enumerate(
                ((ga, ya, wa, dya, gwa), (gb, yb, wb, dyb, gwb), (gc, yc, wc, dyc, gwc))):
            y, w = y_ref[...], w_ref[...]
            s = jax.nn.sigmoid(_f(g_ref[...]))
            proj = jnp.dot(y, w, preferred_element_type=F32)
            d_proj = (dm * s).astype(BF16)
            dg_ref[:, k * D_MODEL:(k + 1) * D_MODEL] = (dm * proj * s * (1.0 - s)).astype(dg_ref.dtype)
            dy_ref[...] = _dot_nt(d_proj, w).astype(dy_ref.dtype)
            gw_ref[...] += _dot_tn(y, d_proj)

    gate_cols = pl.BlockSpec((tm, P_WIDTH - P_G), lambda i: (i, P_G // (P_WIDTH - P_G)))
    return pl.pallas_call(
        body, name=name, grid=(L // tm,),
        in_specs=gates + y_specs + w_specs + [pl.BlockSpec((tm, D_MODEL), lambda i: (i, 0))],
        out_specs=[gate_cols] + y_specs + w_specs,
        out_shape=[jax.ShapeDtypeStruct((L, P_WIDTH), BF16)] + [jax.ShapeDtypeStruct((L, w), BF16) for w in BRANCH_WIDTHS]
        + [jax.ShapeDtypeStruct((w, D_MODEL), F32) for w in BRANCH_WIDTHS],
        compiler_params=_params(("arbitrary",)),
    )(P, P, P, *ys, *ws, dmerged)


def _assemble_dp(name, dP, parts, tl=512):
    L = dP.shape[0]
    tl = min(tl, L)
    n = len(parts)

    def body(*refs):
        o_ref = refs[n + 1]
        o_ref[...] = jnp.concatenate([r[...].astype(o_ref.dtype) for r in refs[:n]], axis=1)

    return pl.pallas_call(
        body, name=name, grid=(L // tl,),
        in_specs=[pl.BlockSpec((tl, a.shape[1]), lambda i: (i, 0)) for a in parts] + [pl.BlockSpec(memory_space=pl.ANY)],
        out_specs=pl.BlockSpec((tl, P_G), lambda i: (i, 0)), out_shape=jax.ShapeDtypeStruct(dP.shape, dP.dtype),
        input_output_aliases={n: 0}, compiler_params=_params(("arbitrary",)),
    )(*parts, dP)


def _layer_bwd(l, S, W, V, dx1, df, sides, landed):
    G = {}
    P = S["P"]

    def mm(key, *args, **kw):
        if key not in sides:
            return _mm(f"{key}{l}", *args, **kw)
        names, layer, make = sides[key]
        res = _mm(f"{key}{l}", *args, side=make(G), **kw)
        for n, a in zip(names, res[1:]):
            landed[(n, layer)] = a
        return res[0]

    G["w_ffn_out"] = _mm(f"gw_ffn_out{l}", S["act"], df, "tn", F32)

    def swiglu_bwd(d_act, gu):
        gt, up = _f(gu[:, :FFN_HALF]), _f(gu[:, FFN_HALF:])
        s = jax.nn.sigmoid(gt)
        gs = gt * s
        return (jnp.concatenate([d_act * up * (s + gs * (1.0 - s)), d_act * gs], axis=1),)

    names, layer, make = sides.get("d_gu", ((), None, None))
    res = _mm_epi(f"d_gu{l}", df, W["w_ffn_out"], "nt", FFN_HALF, [(S["GU"], 2 * FFN_HALF)], swiglu_bwd,
                  [(2 * FFN_HALF, BF16)], side=make(G) if make else None)
    dGU = res[0]
    for n, a in zip(names, res[1:]):
        landed[(n, layer)] = a
    dh2 = mm("d_h2", dGU, W["w_ffn_in"], "nt", BF16)
    G["w_ffn_in"] = _ffn_in_cols(mm("gw_ffn_in", S["h2"], dGU, "tn", F32))
    dx, dmix, G["gate1"], G["g_post_mix"], G["g_pre_ffn"], G["scale2"], G["shift2"] = _mid_bwd(
        f"mid_mix_bwd{l}", S["x_in"], S["mix"], dx1, dh2, V["mid_mix"])
    dmerged = _mm(f"d_merged{l}", dmix, W["w_o"], "nt", BF16)
    G["w_o"] = _mm(f"gw_o{l}", S["merged"], dmix, "tn", F32)

    dP, dya, dyb, dyc, G["w_sc_out"], G["w_sb_out"], G["w_ssm_out"] = _merge_bwd(
        f"merge_bwd{l}", P, [S["ya"], S["yb"], S["yc"]], [W["w_sc_out"], W["w_sb_out"], W["w_ssm_out"]], dmerged)

    def post_bwd(y_ssd, px, z, d, dfull, nw):
        z, d = _f(z), _f(d)
        sx, sz = jax.nn.sigmoid(px), jax.nn.sigmoid(z)
        xs, gz = px * sx, z * sz
        t = y_ssd + xs * dfull
        y = t * gz
        dn = d * nw
        half = SSM_INNER // 2
        ns, dys = [], []
        for g in range(2):
            yg, dng = y[:, g * half:(g + 1) * half], dn[:, g * half:(g + 1) * half]
            r = lax.rsqrt(jnp.mean(yg * yg, axis=-1, keepdims=True) + NORM_EPS)
            ns.append(yg * r)
            dys.append(r * (dng - ns[g] * jnp.mean(dng * ns[g], axis=-1, keepdims=True)))
        n, dy = jnp.concatenate(ns, axis=1), jnp.concatenate(dys, axis=1)
        dt = dy * gz
        return (dt, dt * dfull * (sx + xs * (1.0 - sx)), dy * t * (sz + gz * (1.0 - sz)),
                jnp.sum(dt * xs, axis=0, keepdims=True), jnp.sum(d * n, axis=0, keepdims=True))

    dy_ssd, dxs, dz, G["d_full"], G["ssm_norm_w"] = _rowwise(
        f"ssm_post_bwd{l}", post_bwd,
        [(S["y_ssd"], SSM_INNER, 0), (S["pre"], SSM_INNER, 0), (P, SSM_INNER, P_Z // SSM_INNER), (dyc, SSM_INNER, 0)],
        [V["d_full"], V["norm_w"]], [(SSM_INNER, F32), (SSM_INNER, F32), (SSM_INNER, BF16)], [(1, SSM_INNER)] * 2)
    dpre, ddt, G["dtb"], G["alog"] = _ssd_bwd(S["pre"], P, S["states"], dy_ssd, dxs, V["dtb"], V["alog"])
    dxbc, w0, w1, w2, w3, G["ssm_conv_b"] = _ssmconv_bwd(P, dpre, V["ssm_w"])
    G["ssm_conv_w"] = jnp.concatenate([w0, w1, w2, w3], axis=0)
    dq, dk, dv = _sb_bwd(P, dyb, S["yb32"])
    dA, s0, s1, s2 = _shortconv_bwd(P, dya, V["sc_w"])
    G["sc_conv_w"] = jnp.concatenate([s0, s1, s2], axis=0)
    dP = _assemble_dp(f"assemble_dp{l}", dP, [dA, dq, dk, dv, dz, ddt, dxbc])
    G["w_in"] = _mm(f"gw_in{l}", S["h"], dP, "tn", F32, tn_cap=1024)
    dh = mm("d_h", dP, W["w_in"], "nt", BF16, tk_cap=3072)
    return dx, dh, G


def kernel(x, c, mod_w, mod_b, g_pre_mix, g_post_mix, g_pre_ffn, g_post_ffn, w_in, sc_conv_w, ssm_conv_w, ssm_conv_b, ssm_dt_bias, ssm_a_log, ssm_d, ssm_norm_w, w_sc_out, w_sb_out, w_ssm_out, w_o, w_ffn_in, w_ffn_out, loss_target, m_mod_w, m_mod_b, m_g_pre_mix, m_g_post_mix, m_g_pre_ffn, m_g_post_ffn, m_w_in, m_sc_conv_w, m_ssm_conv_w, m_ssm_conv_b, m_ssm_dt_bias, m_ssm_a_log, m_ssm_d, m_ssm_norm_w, m_w_sc_out, m_w_sb_out, m_w_ssm_out, m_w_o, m_w_ffn_in, m_w_ffn_out, v_mod_w, v_mod_b, v_g_pre_mix, v_g_post_mix, v_g_pre_ffn, v_g_post_ffn, v_w_in, v_sc_conv_w, v_ssm_conv_w, v_ssm_conv_b, v_ssm_dt_bias, v_ssm_a_log, v_ssm_d, v_ssm_norm_w, v_w_sc_out, v_w_sb_out, v_w_ssm_out, v_w_o, v_w_ffn_in, v_w_ffn_out):
    wts = dict(mod_w=mod_w, mod_b=mod_b, g_pre_mix=g_pre_mix, g_post_mix=g_post_mix, g_pre_ffn=g_pre_ffn,
               g_post_ffn=g_post_ffn, w_in=w_in, sc_conv_w=sc_conv_w, ssm_conv_w=ssm_conv_w, ssm_conv_b=ssm_conv_b,
               ssm_dt_bias=ssm_dt_bias, ssm_a_log=ssm_a_log, ssm_d=ssm_d, ssm_norm_w=ssm_norm_w, w_sc_out=w_sc_out,
               w_sb_out=w_sb_out, w_ssm_out=w_ssm_out, w_o=w_o, w_ffn_in=w_ffn_in, w_ffn_out=w_ffn_out)
    mom = dict(mod_w=m_mod_w, mod_b=m_mod_b, g_pre_mix=m_g_pre_mix, g_post_mix=m_g_post_mix, g_pre_ffn=m_g_pre_ffn,
               g_post_ffn=m_g_post_ffn, w_in=m_w_in, sc_conv_w=m_sc_conv_w, ssm_conv_w=m_ssm_conv_w,
               ssm_conv_b=m_ssm_conv_b, ssm_dt_bias=m_ssm_dt_bias, ssm_a_log=m_ssm_a_log, ssm_d=m_ssm_d,
               ssm_norm_w=m_ssm_norm_w, w_sc_out=m_w_sc_out, w_sb_out=m_w_sb_out, w_ssm_out=m_w_ssm_out, w_o=m_w_o,
               w_ffn_in=m_w_ffn_in, w_ffn_out=m_w_ffn_out)
    var = dict(mod_w=v_mod_w, mod_b=v_mod_b, g_pre_mix=v_g_pre_mix, g_post_mix=v_g_post_mix, g_pre_ffn=v_g_pre_ffn,
               g_post_ffn=v_g_post_ffn, w_in=v_w_in, sc_conv_w=v_sc_conv_w, ssm_conv_w=v_ssm_conv_w,
               ssm_conv_b=v_ssm_conv_b, ssm_dt_bias=v_ssm_dt_bias, ssm_a_log=v_ssm_a_log, ssm_d=v_ssm_d,
               ssm_norm_w=v_ssm_norm_w, w_sc_out=v_w_sc_out, w_sb_out=v_w_sb_out, w_ssm_out=v_w_ssm_out, w_o=v_w_o,
               w_ffn_in=v_w_ffn_in, w_ffn_out=v_w_ffn_out)
    xi, yi, ci = _here()
    chip = 2 * xi + yi
    me = 4 * xi + 2 * yi + ci
    x0, target = x[0], loss_target[0]

    first_shapes = [(D_MODEL,), sc_conv_w.shape, ssm_conv_w.shape]
    g0 = _allgather_small("gather_cond", _pack([c, sc_conv_w, ssm_conv_w]))
    c_rows, sc_sh, ssm_sh = _unpack(g0, first_shapes)
    c_all = c_rows
    sc_w = jnp.concatenate([sc_sh[2 * j] for j in range(N_CHIPS)], axis=-1)
    ssm_w = jnp.concatenate([ssm_sh[2 * j] for j in range(N_CHIPS)], axis=-1)

    mod_b_sh = lax.dynamic_slice_in_dim(mod_b, chip * MOD_SHARD, MOD_SHARD, axis=1).reshape(DEPTH, 1, MOD_SHARD)
    modpart = _mod_fwd(c_all, mod_w, mod_b_sh)
    g1 = _allgather_small("gather_mod", modpart.reshape(-1, LANES)).reshape(N_DEV, DEPTH, N_DEV, MOD_SHARD)
    mod = jnp.concatenate([lax.dynamic_index_in_dim(g1[2 * j], me, axis=1, keepdims=False) for j in range(N_CHIPS)],
                          axis=-1)

    def layer_shards(l):
        return [wts[n][l].astype(BF16) for n, _ in BIG]

    def full_weights(which, gathered):
        W = {n: jnp.concatenate([g[j] for j in range(N_CHIPS)], axis=ax - 1) for (n, ax), g in zip(which, gathered)}
        if "w_in" in W:
            W["w_in"] = _pad_w_in(W["w_in"])
        if "w_ffn_in" in W:
            W["w_ffn_in"] = _ffn_in_cols(W["w_ffn_in"])
        return W

    Ws = [{}, {}]
    fwd_sides = [{"in_proj": (_gather_side(layer_shards(0)[1:]), lambda got: Ws[0].update(full_weights(BIG[1:], got))),
                  "ffn_in": (_gather_side(layer_shards(1)), lambda got: Ws[1].update(full_weights(BIG, got)))}, {}]
    Vs = []
    for l in range(DEPTH):
        sh1, sc1, gt1, sh2, sc2, gt2 = [_row(v) for v in jnp.split(mod[l], N_MOD)]
        Vs.append(dict(
            shift1=sh1, scale1=sc1, g_pre_mix=_row(g_pre_mix[l]),
            mid_mix=[gt1, _row(g_post_mix[l]), _row(g_pre_ffn[l]), sc2, sh2],
            gate2=gt2, g_post_ffn=_row(g_post_ffn[l]),
            sc_w=sc_w[l], ssm_w=ssm_w[l], ssm_b=_row(ssm_conv_b[l]),
            dtb=_row(jnp.pad(ssm_dt_bias[l], (0, LANES - SSM_HEADS))), alog=_row(jnp.pad(ssm_a_log[l], (0, LANES - SSM_HEADS))),
            d_full=_row(jnp.repeat(ssm_d[l], SSM_INNER // SSM_HEADS)), norm_w=_row(ssm_norm_w[l])))

    def mid_ffn_vecs(l):
        return [Vs[l]["gate2"], Vs[l]["g_post_ffn"], Vs[l + 1]["g_pre_mix"], Vs[l + 1]["scale1"], Vs[l + 1]["shift1"]]

    saved = []
    x_in = x0
    h, *got = _first_fwd(x0, [Vs[0]["g_pre_mix"], Vs[0]["scale1"], Vs[0]["shift1"]], _gather_side(layer_shards(0)[:1]))
    Ws[0].update(full_weights(BIG[:1], got))
    for l in range(DEPTH):
        S = _layer_fwd(l, x_in, h, Ws[l], Vs[l], fwd_sides[l], mid_ffn_vecs(l) if l + 1 < DEPTH else None)
        saved.append(S)
        if l + 1 < DEPTH:
            x_in, h = S["x_next"], S["h_next"]

    def pieces(G, names):
        out = []
        for n, ax in BIG:
            if n in names:
                g = _unpad_w_in(G[n]) if n == "w_in" else G[n]
                out.append(jnp.stack(jnp.split(g, N_CHIPS, axis=ax - 1)).astype(BF16))
        return out

    small_names = tuple(n for n, _ in BIG if n not in ("w_in", "w_ffn_in"))
    late_names = tuple(n for n, _ in BIG if n != "w_in")
    landed = {}

    GL = [None] * DEPTH
    S = saved[-1]
    dx1, df, g_gate2, g_gpf, loss_cols = _last_bwd(S["x1"], S["f"], target, [Vs[-1]["gate2"], Vs[-1]["g_post_ffn"]])
    for l in reversed(range(DEPTH)):
        sides = {}
        if l + 1 < DEPTH:
            for key, names in (("d_gu", small_names), ("d_h2", ("w_ffn_in",)), ("gw_ffn_in", ("w_in",))):
                sides[key] = (names, l + 1, lambda G, up=GL[l + 1], names=names: _scatter_side(pieces(up, names)))
        if l == 0:
            sides["d_h"] = (late_names, l, lambda G: _scatter_side(pieces(G, late_names)))
        dx, dh, G = _layer_bwd(l, saved[l], Ws[l], Vs[l], dx1, df, sides, landed)
        G["gate2"], G["g_post_ffn"] = g_gate2, g_gpf
        GL[l] = G
        if l > 0:
            Sp = saved[l - 1]
            dx1, df, g_gate2, g_gpf, G["g_pre_mix"], G["scale1"], G["shift1"] = _mid_bwd(
                f"mid_ffn_bwd{l - 1}", Sp["x1"], Sp["f"], dx, dh, mid_ffn_vecs(l - 1))
        else:
            grad_x, G["g_pre_mix"], G["scale1"], G["shift1"], landed[("w_in", 0)] = _first_bwd(
                x0, dx, dh, [Vs[0]["g_pre_mix"], Vs[0]["scale1"], Vs[0]["shift1"]], _scatter_side(pieces(G, ("w_in",))))
    loss = lax.psum(jnp.sum(loss_cols), ("x", "y", "c"))

    def both(key, shape=None):
        a = jnp.stack([GL[l][key] for l in range(DEPTH)])
        return a if shape is None else a.reshape(shape)

    dmod = jnp.concatenate([both(k, (DEPTH, D_MODEL)) for k in ("shift1", "scale1", "gate1", "shift2", "scale2", "gate2")],
                           axis=1)
    part_small = dict(
        mod_b=dmod, g_pre_mix=both("g_pre_mix", (DEPTH, D_MODEL)), g_post_mix=both("g_post_mix", (DEPTH, D_MODEL)),
        g_pre_ffn=both("g_pre_ffn", (DEPTH, D_MODEL)), g_post_ffn=both("g_post_ffn", (DEPTH, D_MODEL)),
        sc_conv_w=both("sc_conv_w"), ssm_conv_w=both("ssm_conv_w"), ssm_conv_b=both("ssm_conv_b", (DEPTH, SSM_CONV_DIM)),
        ssm_dt_bias=both("dtb", (DEPTH, LANES))[:, :SSM_HEADS], ssm_a_log=both("alog", (DEPTH, LANES))[:, :SSM_HEADS],
        ssm_d=both("d_full", (DEPTH, SSM_HEADS, SSM_INNER // SSM_HEADS)).sum(-1),
        ssm_norm_w=both("ssm_norm_w", (DEPTH, SSM_INNER)))
    small_shapes = [part_small[n].shape for n in SMALL]
    g2 = _allgather_small("gather_small_grads", _pack([part_small[n] for n in SMALL]))
    tot = dict(zip(SMALL, _unpack(_sum_slots("sum_small_grads", g2), small_shapes)))
    dmod_all = _unpack(g2, small_shapes)[0]
    dmod_sh = jnp.swapaxes(lax.dynamic_slice_in_dim(dmod_all, chip * MOD_SHARD, MOD_SHARD, axis=2), 0, 1)
    grads = {"mod_w": _mod_bwd(c_all, dmod_sh)}
    for n in SMALL:
        grads[n] = tot[n]
    grads["sc_conv_w"] = lax.dynamic_slice_in_dim(tot["sc_conv_w"], chip * 64, 64, axis=2)
    grads["ssm_conv_w"] = lax.dynamic_slice_in_dim(tot["ssm_conv_w"], chip * 192, 192, axis=2)

    keys = [(n, l) for n, _ in BIG for l in range(DEPTH)]
    mine = [_sum_slots(f"sum_{n}{l}", landed[(n, l)]) for n, l in keys]
    theirs = dict(zip(keys, _sibling_exchange("swap_core_sums", mine)))
    mine = dict(zip(keys, mine))

    out = {}

    def update(name, w2, gs, m2, v2, shape):
        g, d, nm, nv = _adamw(f"adamw_{name}", w2, gs, m2, v2)
        out[name] = tuple(a.reshape(shape) for a in (g, d, nm, nv))

    for n, _ in BIG:
        shp = wts[n].shape
        two = (-1, shp[-1])
        by_layer = [tuple(src[(n, l)] for l in range(DEPTH)) for src in (mine, theirs)]
        update(n, wts[n].reshape(two), by_layer, mom[n].reshape(two), var[n].reshape(two), shp)
    two = (-1, MOD_SHARD)
    update("mod_w", mod_w.reshape(two), [grads["mod_w"].reshape(two)], m_mod_w.reshape(two), v_mod_w.reshape(two), mod_w.shape)
    shapes = [wts[n].shape for n in SMALL]
    res = _adamw("adamw_small", _pack([wts[n] for n in SMALL]), [_pack([grads[n] for n in SMALL])],
                 _pack([mom[n] for n in SMALL]), _pack([var[n] for n in SMALL]))
    for n, g, d, nm, nv in zip(SMALL, *[_unpack(r, shapes) for r in res]):
        out[n] = (g, d, nm, nv)

    result = [loss, grad_x[None]]
    for k in range(4):
        result += [out[n][k] for n in WEIGHT_ORDER]
    return tuple(result)
```

```python
import functools

import jax
import jax.numpy as jnp
from jax import lax
from jax.experimental import pallas as pl
from jax.experimental.pallas import tpu as pltpu

F32 = jnp.float32
BF16 = jnp.bfloat16
HIGHEST = lax.Precision.HIGHEST
MESH_ID = pl.DeviceIdType.MESH

D_MODEL = 1024
DEPTH = 2
SC_WIDTH = 256
SB_HEAD_DIM = 64
SSM_INNER = 512
SSM_HEADS = 8
SSM_STATE = 64
SSM_CONV = 4
SSM_CHUNK = 256
SSM_CONV_DIM = 768
FFN_HIDDEN = 2816
NORM_EPS = 1e-6
N_MOD = 6
N_CHIPS = 4
N_DEV = 8

ADAM_LR = 0.001
ADAM_B1 = 0.9
ADAM_B2 = 0.999
ADAM_EPS = 1e-08
ADAM_WD = 0.01
ADAM_STEP = 10

P_WIDTH = 6144
P_A, P_B, P_Z, P_DT, P_XBC, P_G = 0, 768, 1536, 2048, 2304, 3072
DT_PAD = 256

VMEM_LIMIT_BYTES = 56 * 1024 * 1024
LANES = 128

SB_LOG_CUTOFF = -105.0
SB_TQ = 256
SB_TK = 256
SB_SUBS = 2
SB_STRAIGHT = 2


def _params(sem):
    return pltpu.CompilerParams(dimension_semantics=sem, vmem_limit_bytes=VMEM_LIMIT_BYTES)


def _pick(n, cap):
    if n <= cap:
        return n
    best = None
    for m in range(LANES, cap + 1, LANES):
        if n % m == 0:
            best = m
    assert best is not None, (n, cap)
    return best


def _rowwise(name, fn, rows, vecs, row_outs, acc_outs=(), tl=512, side=None):
    L = rows[0][0].shape[0]
    tl = min(tl, L)
    assert L % tl == 0
    n_in = len(rows) + len(vecs)
    n_ro, n_ao = len(row_outs), len(acc_outs)
    n_si = len(side.arrays) if side else 0
    n_so = len(side.out_shapes) if side else 0

    def body(*refs):
        ins, s_in = refs[:n_in], refs[n_in:n_in + n_si]
        outs = refs[n_in + n_si:]
        ro, ao, s_out, sems = outs[:n_ro], outs[n_ro:n_ro + n_ao], outs[n_ro + n_ao:n_ro + n_ao + n_so], outs[n_ro + n_ao + n_so:]
        if side:
            @pl.when(pl.program_id(0) == 0)
            def _():
                side.start(s_in, s_out, sems)

        _rows(ins, ro, ao)
        if side:
            @pl.when(pl.program_id(0) == L // tl - 1)
            def _():
                side.finish(s_in, s_out, sems)

    def _rows(ins, ro, ao):
        vals = fn(*[r[...] for r in ins])
        if not isinstance(vals, (tuple, list)):
            vals = (vals,)
        for o, v in zip(ro, vals[:n_ro]):
            o[...] = v.astype(o.dtype)
        if ao:
            @pl.when(pl.program_id(0) == 0)
            def _():
                for o in ao:
                    o[...] = jnp.zeros_like(o)
            for o, v in zip(ao, vals[n_ro:]):
                o[...] += v.astype(F32)

    in_specs = [pl.BlockSpec((tl, w), functools.partial(lambda i, cb: (i, cb), cb=cb)) for _, w, cb in rows]
    in_specs += [pl.BlockSpec(v.shape, lambda i: (0, 0)) for v in vecs]
    out_specs = [pl.BlockSpec((tl, w), lambda i: (i, 0)) for w, _ in row_outs]
    out_specs += [pl.BlockSpec(s, lambda i: (0, 0)) for s in acc_outs]
    out_shape = [jax.ShapeDtypeStruct((L, w), dt) for w, dt in row_outs]
    out_shape += [jax.ShapeDtypeStruct(s, F32) for s in acc_outs]
    any_spec = pl.BlockSpec(memory_space=pl.ANY)
    return pl.pallas_call(
        body, name=name, grid=(L // tl,), in_specs=in_specs + [any_spec] * n_si, out_specs=out_specs + [any_spec] * n_so,
        out_shape=out_shape + (side.out_shapes if side else []), scratch_shapes=side.scratch if side else [],
        compiler_params=_params(("arbitrary",)),
    )(*[a for a, _, _ in rows], *vecs, *(side.arrays if side else []))


def _mm(name, a, b, mode, out_dtype, tm=1024, tn_cap=1408, tk_cap=2816, side=None):
    if mode == "nn":
        (M, K), (_, N) = a.shape, b.shape
    elif mode == "nt":
        (M, K), (N, _) = a.shape, b.shape
    else:
        (K, M), (_, N) = a.shape, b.shape
        tm, tk_cap = 1408, 2048
    tm = _pick(M, tm)
    tn = _pick(N, tn_cap)
    tk = _pick(K, tk_cap)
    nk = K // tk
    grid = (M // tm, N // tn, nk)
    n_si = len(side.arrays) if side else 0
    n_so = len(side.out_shapes) if side else 0
    n_acc = 1 if nk > 1 else 0

    def body(a_ref, b_ref, *rest):
        s_in, o_ref, s_out = rest[:n_si], rest[n_si], rest[n_si + 1:n_si + 1 + n_so]
        scr = rest[n_si + 1 + n_so:]
        if side:
            at = [pl.program_id(d) for d in range(3)]
            is_first = jnp.logical_and(jnp.logical_and(at[0] == 0, at[1] == 0), at[2] == 0)
            is_last = jnp.logical_and(jnp.logical_and(at[0] == grid[0] - 1, at[1] == grid[1] - 1), at[2] == grid[2] - 1)

            @pl.when(is_first)
            def _():
                side.start(s_in, s_out, scr[n_acc:])

        _product(a_ref, b_ref, o_ref, scr)
        if side:
            @pl.when(is_last)
            def _():
                side.finish(s_in, s_out, scr[n_acc:])

    def _product(a_ref, b_ref, o_ref, scr):
        if mode == "nn":
            p = jnp.dot(a_ref[...], b_ref[...], preferred_element_type=F32)
        elif mode == "nt":
            p = lax.dot_general(a_ref[...], b_ref[...], (((1,), (1,)), ((), ())), preferred_element_type=F32)
        else:
            p = lax.dot_general(a_ref[...], b_ref[...], (((0,), (0,)), ((), ())), preferred_element_type=F32)
        if nk == 1:
            o_ref[...] = p.astype(o_ref.dtype)
        else:
            acc = scr[0]
            k = pl.program_id(2)

            @pl.when(k == 0)
            def _():
                acc[...] = p

            @pl.when(k > 0)
            def _():
                acc[...] += p

            @pl.when(k == nk - 1)
            def _():
                o_ref[...] = acc[...].astype(o_ref.dtype)

    if mode == "nn":
        a_spec = pl.BlockSpec((tm, tk), lambda i, j, k: (i, k))
        b_spec = pl.BlockSpec((tk, tn), lambda i, j, k: (k, j))
    elif mode == "nt":
        a_spec = pl.BlockSpec((tm, tk), lambda i, j, k: (i, k))
        b_spec = pl.BlockSpec((tn, tk), lambda i, j, k: (j, k))
    else:
        a_spec = pl.BlockSpec((tk, tm), lambda i, j, k: (k, i))
        b_spec = pl.BlockSpec((tk, tn), lambda i, j, k: (k, j))
    any_spec = pl.BlockSpec(memory_space=pl.ANY)
    res = pl.pallas_call(
        body, name=name, grid=grid, in_specs=[a_spec, b_spec] + [any_spec] * n_si,
        out_specs=[pl.BlockSpec((tm, tn), lambda i, j, k: (i, j))] + [any_spec] * n_so,
        out_shape=[jax.ShapeDtypeStruct((M, N), out_dtype)] + (side.out_shapes if side else []),
        scratch_shapes=([pltpu.VMEM((tm, tn), F32)] if nk > 1 else []) + (side.scratch if side else []),
        compiler_params=_params(("arbitrary", "arbitrary", "arbitrary")),
    )(a, b, *(side.arrays if side else []))
    return res if side else res[0]


def _mm_epi(name, a, b, mode, tn, extras, epi, outs, tm=512, side=None):
    if mode == "nn":
        (M, K), (_, N) = a.shape, b.shape
    else:
        (M, K), (N, _) = a.shape, b.shape
    tm = _pick(M, tm)
    grid = (N // tn, M // tm)
    n_ex, n_out = len(extras), len(outs)
    n_si = len(side.arrays) if side else 0
    n_so = len(side.out_shapes) if side else 0

    def body(*refs):
        a_ref, b_ref, ex = refs[0], refs[1], refs[2:2 + n_ex]
        s_in = refs[2 + n_ex:2 + n_ex + n_si]
        o_refs = refs[2 + n_ex + n_si:2 + n_ex + n_si + n_out]
        s_out = refs[2 + n_ex + n_si + n_out:2 + n_ex + n_si + n_out + n_so]
        sems = refs[2 + n_ex + n_si + n_out + n_so:]
        if side:
            @pl.when(jnp.logical_and(pl.program_id(0) == 0, pl.program_id(1) == 0))
            def _():
                side.start(s_in, s_out, sems)

        if mode == "nn":
            p = jnp.dot(a_ref[...], b_ref[...], preferred_element_type=F32)
        else:
            p = lax.dot_general(a_ref[...], b_ref[...], (((1,), (1,)), ((), ())), preferred_element_type=F32)
        for o, v in zip(o_refs, epi(p, *[r[...] for r in ex])):
            o[...] = v.astype(o.dtype)
        if side:
            @pl.when(jnp.logical_and(pl.program_id(0) == grid[0] - 1, pl.program_id(1) == grid[1] - 1))
            def _():
                side.finish(s_in, s_out, sems)

    any_spec = pl.BlockSpec(memory_space=pl.ANY)
    a_spec = pl.BlockSpec((tm, K), lambda j, i: (i, 0))
    b_spec = pl.BlockSpec((K, tn), lambda j, i: (0, j)) if mode == "nn" else pl.BlockSpec((tn, K), lambda j, i: (j, 0))
    return pl.pallas_call(
        body, name=name, grid=grid,
        in_specs=[a_spec, b_spec]
        + [pl.BlockSpec(e.shape, lambda j, i: (0, 0)) if w is None else pl.BlockSpec((tm, w), lambda j, i: (i, j))
           for e, w in extras] + [any_spec] * n_si,
        out_specs=[pl.BlockSpec((tm, w), lambda j, i: (i, j)) for w, _ in outs] + [any_spec] * n_so,
        out_shape=[jax.ShapeDtypeStruct((M, (N // tn) * w), dt) for w, dt in outs] + (side.out_shapes if side else []),
        scratch_shapes=side.scratch if side else [],
        compiler_params=_params(("arbitrary", "arbitrary")),
    )(a, b, *[e for e, _ in extras], *(side.arrays if side else []))


def _f(x):
    return x.astype(F32)


def _silu(x):
    return x * jax.nn.sigmoid(x)


def _softplus(x):
    return jnp.maximum(x, 0.0) + jnp.log1p(jnp.exp(-jnp.abs(x)))


def _rms(x, g):
    r = lax.rsqrt(jnp.mean(x * x, axis=-1, keepdims=True) + NORM_EPS)
    return x * r * g


def _adaln(x, g, scale, shift):
    return _rms(x, g) * (1.0 + scale) + shift


def _resid(x, y, gate, g):
    return x + gate * _rms(y, g)


def _mid(x, y, gate, g_post, g_pre, scale, shift):
    x_new = _resid(x, y, gate, g_post)
    return x_new, _adaln(x_new, g_pre, scale, shift)


def _swiglu(gt, up):
    return _silu(gt) * up


def _ssm_post(y_ssd, pre_xs, z, d_full, norm_w):
    y = (y_ssd + _silu(pre_xs) * d_full) * _silu(z)
    half = SSM_INNER // 2
    parts = []
    for g in range(2):
        yg = y[:, g * half:(g + 1) * half]
        parts.append(yg * lax.rsqrt(jnp.mean(yg * yg, axis=-1, keepdims=True) + NORM_EPS))
    return jnp.concatenate(parts, axis=1) * norm_w


def _first_fwd(x, vecs, side=None):
    return _rowwise("adaln_first", lambda x, g, sc, sh: _adaln(x, g, sc, sh),
                    [(x, D_MODEL, 0)], vecs, [(D_MODEL, BF16)], side=side)


def _mid_bwd(name, x, y, dx_new, dh, vecs):
    def fn(x, y, dxn, dh, gate, g_post, g_pre, scale, shift):
        y, dh = _f(y), _f(dh)
        r_y = lax.rsqrt(jnp.mean(y * y, axis=-1, keepdims=True) + NORM_EPS)
        n_y = y * r_y
        both = gate * g_post
        x_new = x + n_y * both
        r_x = lax.rsqrt(jnp.mean(x_new * x_new, axis=-1, keepdims=True) + NORM_EPS)
        u = x_new * r_x
        col_p = jnp.sum(dh * u, axis=0, keepdims=True)
        du = dh * (g_pre * (1.0 + scale))
        dxt = dxn + r_x * (du - u * jnp.mean(du * u, axis=-1, keepdims=True))
        col_q = jnp.sum(dxt * n_y, axis=0, keepdims=True)
        dn = dxt * both
        dy = r_y * (dn - n_y * jnp.mean(dn * n_y, axis=-1, keepdims=True))
        return (dxt, dy, col_q * g_post, col_q * gate, col_p * (1.0 + scale), col_p * g_pre,
                jnp.sum(dh, axis=0, keepdims=True))

    vec = (1, D_MODEL)
    return _rowwise(name, fn, [(x, D_MODEL, 0), (y, D_MODEL, 0), (dx_new, D_MODEL, 0), (dh, D_MODEL, 0)], vecs,
                    [(D_MODEL, F32), (D_MODEL, BF16)], [vec] * 5)


def _first_bwd(x, dx_in, dh, vecs, side=None):
    def fn(x, dxi, dh, g, scale, shift):
        dh = _f(dh)
        r = lax.rsqrt(jnp.mean(x * x, axis=-1, keepdims=True) + NORM_EPS)
        u = x * r
        col_p = jnp.sum(dh * u, axis=0, keepdims=True)
        du = dh * (g * (1.0 + scale))
        dx = dxi + r * (du - u * jnp.mean(du * u, axis=-1, keepdims=True))
        return dx, col_p * (1.0 + scale), col_p * g, jnp.sum(dh, axis=0, keepdims=True)

    vec = (1, D_MODEL)
    return _rowwise("adaln_first_bwd", fn, [(x, D_MODEL, 0), (dx_in, D_MODEL, 0), (dh, D_MODEL, 0)], vecs,
                    [(D_MODEL, F32)], [vec] * 3, side=side)


def _last_bwd(x1, f, target, vecs):
    def fn(x1, f, t, gate, g):
        f = _f(f)
        r = lax.rsqrt(jnp.mean(f * f, axis=-1, keepdims=True) + NORM_EPS)
        n = f * r
        both = gate * g
        err = x1 + n * both - t
        d = err * (1.0 / D_MODEL)
        col_q = jnp.sum(d * n, axis=0, keepdims=True)
        dn = d * both
        df = r * (dn - n * jnp.mean(dn * n, axis=-1, keepdims=True))
        loss_cols = jnp.sum(err * err, axis=0, keepdims=True) * (0.5 / D_MODEL)
        return d, df, col_q * g, col_q * gate, loss_cols

    vec = (1, D_MODEL)
    return _rowwise("loss_last_bwd", fn, [(x1, D_MODEL, 0), (f, D_MODEL, 0), (target, D_MODEL, 0)], vecs,
                    [(D_MODEL, F32), (D_MODEL, BF16)], [vec] * 3)


HALO = 16


def _shift_down(u, prev, k):
    rows = lax.broadcasted_iota(jnp.int32, u.shape, 0)
    v = pltpu.roll(u, k, 0)
    for t in range(k):
        v = jnp.where(rows == t, prev[HALO - k + t:HALO - k + t + 1, :], v)
    return v


def _shift_up(u, nxt, k):
    n = u.shape[0]
    rows = lax.broadcasted_iota(jnp.int32, u.shape, 0)
    v = pltpu.roll(u, n - k, 0)
    for t in range(k):
        v = jnp.where(rows == n - k + t, nxt[t:t + 1, :], v)
    return v


def _conv_specs(L, tl, width, col_block):
    per = tl // HALO
    last = L // HALO - 1
    main = pl.BlockSpec((tl, width), lambda i: (i, col_block))
    before = pl.BlockSpec((HALO, width), lambda i: (jnp.maximum(i * per - 1, 0), col_block))
    after = pl.BlockSpec((HALO, width), lambda i: (jnp.minimum((i + 1) * per, last), col_block))
    return main, before, after


def _shortconv_fwd(P, w, tl=512):
    L = P.shape[0]
    tl = min(tl, L)
    C = SC_WIDTH
    main, before, _ = _conv_specs(L, tl, 3 * C, 0)

    def body(p_ref, h_ref, w_ref, o_ref):
        first = (pl.program_id(0) == 0)
        p, h = _f(p_ref[...]), _f(h_ref[...])
        b, u = p[:, :C], p[:, C:2 * C] * p[:, 2 * C:]
        uh = jnp.where(first, 0.0, h[:, C:2 * C] * h[:, 2 * C:])
        wv = w_ref[...]
        cv = wv[2:3] * u + wv[1:2] * _shift_down(u, uh, 1) + wv[0:1] * _shift_down(u, uh, 2)
        o_ref[...] = (b * cv).astype(o_ref.dtype)

    return pl.pallas_call(
        body, name="shortconv_fwd", grid=(L // tl,),
        in_specs=[main, before, pl.BlockSpec(w.shape, lambda i: (0, 0))],
        out_specs=pl.BlockSpec((tl, C), lambda i: (i, 0)),
        out_shape=jax.ShapeDtypeStruct((L, C), BF16), compiler_params=_params(("arbitrary",)),
    )(P, P, w)


def _shortconv_bwd(P, dya, w, tl=512):
    L = P.shape[0]
    tl = min(tl, L)
    C = SC_WIDTH
    main, before, after = _conv_specs(L, tl, 3 * C, 0)
    dmain, _, dafter = _conv_specs(L, tl, C, 0)
    n = L // tl

    def body(p_ref, h_ref, n_ref, d_ref, dn_ref, w_ref, o_ref, dw0, dw1, dw2):
        i = pl.program_id(0)
        p, h, nx = _f(p_ref[...]), _f(h_ref[...]), _f(n_ref[...])
        b, c, x = p[:, :C], p[:, C:2 * C], p[:, 2 * C:]
        u = c * x
        uh = jnp.where(i == 0, 0.0, h[:, C:2 * C] * h[:, 2 * C:])
        u1, u2 = _shift_down(u, uh, 1), _shift_down(u, uh, 2)
        wv = w_ref[...]
        cv = wv[2:3] * u + wv[1:2] * u1 + wv[0:1] * u2
        dy = _f(d_ref[...])
        dcv = dy * b
        dcv_n = jnp.where(i == n - 1, 0.0, _f(dn_ref[...]) * nx[:, :C])
        du = wv[2:3] * dcv + wv[1:2] * _shift_up(dcv, dcv_n, 1) + wv[0:1] * _shift_up(dcv, dcv_n, 2)
        o_ref[:, :C] = (dy * cv).astype(o_ref.dtype)
        o_ref[:, C:2 * C] = (du * x).astype(o_ref.dtype)
        o_ref[:, 2 * C:] = (du * c).astype(o_ref.dtype)

        @pl.when(i == 0)
        def _():
            for r in (dw0, dw1, dw2):
                r[...] = jnp.zeros_like(r)

        dw0[...] += jnp.sum(dcv * u2, axis=0, keepdims=True)
        dw1[...] += jnp.sum(dcv * u1, axis=0, keepdims=True)
        dw2[...] += jnp.sum(dcv * u, axis=0, keepdims=True)

    vec = pl.BlockSpec((1, C), lambda i: (0, 0))
    return pl.pallas_call(
        body, name="shortconv_bwd", grid=(n,),
        in_specs=[main, before, after, dmain, dafter, pl.BlockSpec(w.shape, lambda i: (0, 0))],
        out_specs=[pl.BlockSpec((tl, 3 * C), lambda i: (i, 0)), vec, vec, vec],
        out_shape=[jax.ShapeDtypeStruct((L, 3 * C), BF16)] + [jax.ShapeDtypeStruct((1, C), F32)] * 3,
        compiler_params=_params(("arbitrary",)),
    )(P, P, P, dya, dya, w)


def _ssmconv_fwd(P, w, bias, tl=512):
    L = P.shape[0]
    tl = min(tl, L)
    C = SSM_CONV_DIM
    main, before, _ = _conv_specs(L, tl, C, P_XBC // C)

    def body(p_ref, h_ref, w_ref, b_ref, o_ref):
        u = _f(p_ref[...])
        uh = jnp.where(pl.program_id(0) == 0, 0.0, _f(h_ref[...]))
        wv = w_ref[...]
        acc = wv[3:4] * u + b_ref[...]
        for k in range(1, SSM_CONV):
            acc = acc + wv[3 - k:4 - k] * _shift_down(u, uh, k)
        o_ref[...] = acc

    return pl.pallas_call(
        body, name="ssmconv_fwd", grid=(L // tl,),
        in_specs=[main, before, pl.BlockSpec(w.shape, lambda i: (0, 0)), pl.BlockSpec(bias.shape, lambda i: (0, 0))],
        out_specs=pl.BlockSpec((tl, C), lambda i: (i, 0)),
        out_shape=jax.ShapeDtypeStruct((L, C), F32), compiler_params=_params(("arbitrary",)),
    )(P, P, w, bias)


def _ssmconv_bwd(P, dpre, w, tl=512):
    L = P.shape[0]
    tl = min(tl, L)
    C = SSM_CONV_DIM
    main, before, _ = _conv_specs(L, tl, C, P_XBC // C)
    dmain, _, dafter = _conv_specs(L, tl, C, 0)
    n = L // tl

    def body(p_ref, h_ref, d_ref, dn_ref, w_ref, o_ref, dw0, dw1, dw2, dw3, db):
        i = pl.program_id(0)
        u = _f(p_ref[...])
        uh = jnp.where(i == 0, 0.0, _f(h_ref[...]))
        d = d_ref[...]
        dn = jnp.where(i == n - 1, 0.0, dn_ref[...])
        wv = w_ref[...]
        du = wv[3:4] * d
        for k in range(1, SSM_CONV):
            du = du + wv[3 - k:4 - k] * _shift_up(d, dn, k)
        o_ref[...] = du.astype(o_ref.dtype)

        @pl.when(i == 0)
        def _():
            for r in (dw0, dw1, dw2, dw3, db):
                r[...] = jnp.zeros_like(r)

        for k, r in ((3, dw0), (2, dw1), (1, dw2)):
            r[...] += jnp.sum(d * _shift_down(u, uh, k), axis=0, keepdims=True)
        dw3[...] += jnp.sum(d * u, axis=0, keepdims=True)
        db[...] += jnp.sum(d, axis=0, keepdims=True)

    vec = pl.BlockSpec((1, C), lambda i: (0, 0))
    return pl.pallas_call(
        body, name="ssmconv_bwd", grid=(n,),
        in_specs=[main, before, dmain, dafter, pl.BlockSpec(w.shape, lambda i: (0, 0))],
        out_specs=[pl.BlockSpec((tl, C), lambda i: (i, 0))] + [vec] * 5,
        out_shape=[jax.ShapeDtypeStruct((L, C), BF16)] + [jax.ShapeDtypeStruct((1, C), F32)] * 5,
        compiler_params=_params(("arbitrary",)),
    )(P, P, dpre, dpre, w)


def _dot_nt(a, b):
    return lax.dot_general(a, b, (((1,), (1,)), ((), ())), preferred_element_type=F32)


def _dot_tn(a, b):
    return lax.dot_general(a, b, (((0,), (0,)), ((), ())), preferred_element_type=F32)


def _split3(x):
    hi = x.astype(BF16)
    r = x - hi.astype(F32)
    mid = r.astype(BF16)
    return hi, mid, (r - mid.astype(F32)).astype(BF16)


@jax.custom_vjp
def _xm01(x, m):
    return sum(jnp.dot(t, m, preferred_element_type=F32) for t in _split3(x))


def _xm01_fwd(x, m):
    return _xm01(x, m), m


def _xm01_bwd(m, g):
    return sum(_dot_nt(t, m) for t in _split3(g)), jnp.zeros_like(m)


_xm01.defvjp(_xm01_fwd, _xm01_bwd)


@jax.custom_vjp
def _m01x(m, x):
    return sum(jnp.dot(m, t, preferred_element_type=F32) for t in _split3(x))


def _m01x_fwd(m, x):
    return _m01x(m, x), m


def _m01x_bwd(m, g):
    return jnp.zeros_like(m), sum(_dot_tn(m, t) for t in _split3(g))


_m01x.defvjp(_m01x_fwd, _m01x_bwd)


def _ssd_chunk(pre, dtr, s_prev, dtb, alog):
    T = pre.shape[0]
    act = _silu(pre)
    xs, bm, cm = act[:, :SSM_INNER], act[:, SSM_INNER:SSM_INNER + 128], act[:, SSM_INNER + 128:]
    lane = lax.broadcasted_iota(jnp.int32, (1, LANES), 1)
    dt = jnp.where(lane < SSM_HEADS, _softplus(dtr + dtb), 0.0)
    a = dt * (-jnp.exp(alog))
    ri = lax.broadcasted_iota(jnp.int32, (T, T), 0)
    ci = lax.broadcasted_iota(jnp.int32, (T, T), 1)
    causal = ci <= ri
    a_cs = _m01x(causal.astype(BF16), a)
    eh = lax.broadcasted_iota(jnp.int32, (LANES, SSM_INNER), 0)
    ej = lax.broadcasted_iota(jnp.int32, (LANES, SSM_INNER), 1)
    expand = (lax.shift_right_logical(ej, 6) == eh).astype(BF16)
    dt_full = _xm01(dt, expand)
    acs_full = _xm01(a_cs, expand)
    alast_full = acs_full[T - 1:T, :]
    xdt = xs * dt_full
    a_cs_t = a_cs.T
    ys, s_new = [], []
    for g in range(2):
        in_group = lax.shift_right_logical(lane, 6) == g
        cg = jnp.where(in_group, cm, 0.0).astype(BF16)
        bg = jnp.where(in_group, bm, 0.0).astype(BF16)
        scores = _dot_nt(cg, bg)
        for pp in range(2):
            hp = 2 * g + pp
            cols = slice(hp * LANES, (hp + 1) * LANES)
            xp, acsp = xdt[:, cols], acs_full[:, cols]
            per_head = []
            for hh in range(2):
                h = 2 * hp + hh
                decay = jnp.exp(jnp.where(causal, a_cs[:, h:h + 1] - a_cs_t[h:h + 1, :], -jnp.inf))
                per_head.append(jnp.dot((scores * decay).astype(BF16), xp.astype(BF16), preferred_element_type=F32))
            y_diag = jnp.where(lane < SSM_STATE, per_head[0], per_head[1])
            sp = s_prev[hp * LANES:(hp + 1) * LANES, :]
            y_off = jnp.dot(cg, sp.astype(BF16), preferred_element_type=F32) * jnp.exp(acsp)
            ys.append(y_diag + y_off)
            to_end = jnp.exp(alast_full[:, cols] - acsp)
            s_new.append(sp * jnp.exp(alast_full[:, cols]) + _dot_tn(bg, (xp * to_end).astype(BF16)))
    return jnp.concatenate(ys, axis=1), jnp.concatenate(s_new, axis=0)


def _ssd_fwd(pre, P, dtb, alog):
    L = pre.shape[0]
    T = min(SSM_CHUNK, L)
    nc = L // T

    def body(pre_ref, dt_ref, dtb_ref, al_ref, y_ref, st_ref, s_scr):
        @pl.when(pl.program_id(0) == 0)
        def _():
            s_scr[...] = jnp.zeros_like(s_scr)

        st_ref[0] = s_scr[...]
        y, s = _ssd_chunk(pre_ref[...], _f(dt_ref[...]), s_scr[...], dtb_ref[...], al_ref[...])
        y_ref[...] = y
        s_scr[...] = s

    vec = pl.BlockSpec((1, LANES), lambda i: (0, 0))
    return pl.pallas_call(
        body, name="ssd_fwd", grid=(nc,),
        in_specs=[pl.BlockSpec((T, SSM_CONV_DIM), lambda i: (i, 0)), pl.BlockSpec((T, LANES), lambda i: (i, P_DT // LANES)),
                  vec, vec],
        out_specs=[pl.BlockSpec((T, SSM_INNER), lambda i: (i, 0)), pl.BlockSpec((1, 512, LANES), lambda i: (i, 0, 0))],
        out_shape=[jax.ShapeDtypeStruct((L, SSM_INNER), F32), jax.ShapeDtypeStruct((nc, 512, LANES), F32)],
        scratch_shapes=[pltpu.VMEM((512, LANES), F32)], compiler_params=_params(("arbitrary",)),
    )(pre, P, dtb, alog)


def _ssd_bwd(pre, P, states, dy, dxs_extra, dtb, alog):
    L = pre.shape[0]
    T = min(SSM_CHUNK, L)
    nc = L // T

    def body(pre_ref, dt_ref, st_ref, dy_ref, dx_ref, dtb_ref, al_ref, dpre_ref, ddt_ref, ddtb_ref, dal_ref, ds_scr):
        @pl.when(pl.program_id(0) == 0)
        def _():
            ds_scr[...] = jnp.zeros_like(ds_scr)
            ddtb_ref[...] = jnp.zeros_like(ddtb_ref)
            dal_ref[...] = jnp.zeros_like(dal_ref)

        _, vjp = jax.vjp(_ssd_chunk, pre_ref[...], _f(dt_ref[...]), st_ref[0], dtb_ref[...], al_ref[...])
        dpre, ddt, ds, ddtb, dal = vjp((dy_ref[...], ds_scr[...]))
        dpre_ref[:, :SSM_INNER] = dpre[:, :SSM_INNER] + dx_ref[...]
        dpre_ref[:, SSM_INNER:] = dpre[:, SSM_INNER:]
        ddt_ref[:, :LANES] = ddt.astype(ddt_ref.dtype)
        ddt_ref[:, LANES:] = jnp.zeros((T, DT_PAD - LANES), ddt_ref.dtype)
        ds_scr[...] = ds
        ddtb_ref[...] += ddtb
        dal_ref[...] += dal

    vec = pl.BlockSpec((1, LANES), lambda i: (0, 0))
    rev = lambda i: (nc - 1 - i, 0)
    return pl.pallas_call(
        body, name="ssd_bwd", grid=(nc,),
        in_specs=[pl.BlockSpec((T, SSM_CONV_DIM), rev), pl.BlockSpec((T, LANES), lambda i: (nc - 1 - i, P_DT // LANES)),
                  pl.BlockSpec((1, 512, LANES), lambda i: (nc - 1 - i, 0, 0)),
                  pl.BlockSpec((T, SSM_INNER), rev), pl.BlockSpec((T, SSM_INNER), rev), vec, vec],
        out_specs=[pl.BlockSpec((T, SSM_CONV_DIM), rev), pl.BlockSpec((T, DT_PAD), rev), vec, vec],
        out_shape=[jax.ShapeDtypeStruct((L, SSM_CONV_DIM), F32), jax.ShapeDtypeStruct((L, DT_PAD), BF16),
                   jax.ShapeDtypeStruct((1, LANES), F32), jax.ShapeDtypeStruct((1, LANES), F32)],
        scratch_shapes=[pltpu.VMEM((512, LANES), F32)], compiler_params=_params(("arbitrary",)),
    )(pre, P, states, dy, dxs_extra, dtb, alog)


def _sb_scores(qm, kb, later, strict, mask):
    z = _dot_nt(qm, kb)
    lk = jnp.minimum(-z, 0.0) - jnp.log(1.0 + jnp.exp(-jnp.abs(z)))
    if mask is not None:
        lk = jnp.where(mask, lk, 0.0)
    log_a = z + lk + jnp.dot(lk.astype(BF16), strict, preferred_element_type=F32) + later
    if mask is not None:
        log_a = jnp.where(mask, log_a, -jnp.inf)
    return z, lk, log_a


def _dot_split(x, m):
    hi = x.astype(BF16)
    lo = (x - hi.astype(F32)).astype(BF16)
    return jnp.dot(hi, m, preferred_element_type=F32) + jnp.dot(lo, m, preferred_element_type=F32)


def _sb_setup(q_ref, i, tq, tk):
    lane = lax.broadcasted_iota(jnp.int32, (1, LANES), 1)
    first = lane < SB_HEAD_DIM
    q = q_ref[...] * (SB_HEAD_DIM ** -0.5)
    qms = (jnp.where(first, q, jnp.zeros_like(q)), jnp.where(first, jnp.zeros_like(q), q))
    j0 = lax.div(i * tq, tk)
    ri = lax.broadcasted_iota(jnp.int32, (tq, tk), 0)
    ci = lax.broadcasted_iota(jnp.int32, (tq, tk), 1)
    diag_mask = (ci + (j0 * tk - i * tq)) < ri
    kr = lax.broadcasted_iota(jnp.int32, (tk, tk), 0)
    kc = lax.broadcasted_iota(jnp.int32, (tk, tk), 1)
    strict = (kr > kc).astype(BF16)
    return first, qms, j0, diag_mask, strict


def _sb_continue(c):
    return jnp.logical_and(c[0] >= 0, jnp.maximum(jnp.max(c[1][0]), jnp.max(c[1][1])) > SB_LOG_CUTOFF)


def _sb_fwd(P):
    L = P.shape[0]
    tq, tk = min(SB_TQ, L), min(SB_TK, L)
    nq = L // tq
    qb = P_B // LANES

    subs = SB_SUBS if L % (SB_SUBS * tq) == 0 else 1

    def body(q_ref, k_ref, v_ref, o_ref, of_ref):
        zero, zacc = jnp.zeros((tq, 1), F32), jnp.zeros((tq, LANES), F32)
        walks = []
        for s in range(subs):
            rows = pl.ds(s * tq, tq)
            first, qms, j0, diag_mask, strict = _sb_setup(q_ref.at[rows, :], pl.program_id(1) * subs + s, tq, tk)

            def tile(h, j, later, acc, mask=None, valid=None, qms=qms, strict=strict):
                off = pl.multiple_of(j * tk, tk)
                gate = later if valid is None else jnp.where(valid, later, -jnp.inf)
                _, lk, log_a = _sb_scores(qms[h], k_ref[pl.ds(off, tk), :], gate, strict, mask)
                acc = acc + jnp.dot(jnp.exp(log_a).astype(BF16), v_ref[pl.ds(off, tk), :], preferred_element_type=F32)
                total = jnp.sum(lk, axis=1, keepdims=True)
                return later + (total if valid is None else jnp.where(valid, total, 0.0)), acc

            state = []
            for h in range(2):
                carry = tile(h, j0, zero, zacc, mask=diag_mask)
                for n in range(1, SB_STRAIGHT):
                    carry = tile(h, jnp.maximum(j0 - n, 0), *carry, valid=j0 >= n)
                state.append(carry)
            walks.append((rows, first, j0, tile, state))

        for rows, first, j0, tile, state in walks:
            def tail(c, tile=tile):
                res = [tile(h, c[0], c[1][h], c[2][h]) for h in range(2)]
                return c[0] - 1, (res[0][0], res[1][0]), (res[0][1], res[1][1])

            _, _, accs = lax.while_loop(
                _sb_continue, tail, (j0 - SB_STRAIGHT, (state[0][0], state[1][0]), (state[0][1], state[1][1])))
            out = jnp.where(first, accs[0], accs[1])
            o_ref[rows, :] = out.astype(o_ref.dtype)
            of_ref[rows, :] = out

    nq = nq // subs
    tile_spec = pl.BlockSpec((subs * tq, LANES), lambda p, i: (i, p))
    return pl.pallas_call(
        body, name="sb_fwd", grid=(2, nq),
        in_specs=[pl.BlockSpec((subs * tq, LANES), lambda p, i: (i, qb + p)),
                  pl.BlockSpec((L, LANES), lambda p, i: (0, qb + 2 + p)),
                  pl.BlockSpec((L, LANES), lambda p, i: (0, qb + 4 + p))],
        out_specs=[tile_spec, tile_spec],
        out_shape=[jax.ShapeDtypeStruct((L, 2 * LANES), BF16), jax.ShapeDtypeStruct((L, 2 * LANES), F32)],
        compiler_params=_params(("arbitrary", "arbitrary")),
    )(P, P, P)


def _sb_bwd(P, dyb, yb32):
    L = P.shape[0]
    tq, tk = min(SB_TQ, L), min(SB_TK, L)
    nq = L // tq
    qb = P_B // LANES

    subs = SB_SUBS if L % (SB_SUBS * tq) == 0 else 1

    def body(q_ref, k_ref, v_ref, do_ref, of_ref, dq_ref, dk_ref, dv_ref):
        @pl.when(pl.program_id(1) == 0)
        def _():
            dk_ref[...] = jnp.zeros_like(dk_ref)
            dv_ref[...] = jnp.zeros_like(dv_ref)

        tails = [walk(pl.ds(s * tq, tq), pl.program_id(1) * subs + s, q_ref, k_ref, v_ref, do_ref, of_ref, dq_ref,
                      dk_ref, dv_ref) for s in range(subs)]
        for finish in tails:
            finish()

    def walk(rows, i, q_ref, k_ref, v_ref, do_ref, of_ref, dq_ref, dk_ref, dv_ref):
        first, qms, j0, diag_mask, strict = _sb_setup(q_ref.at[rows, :], i, tq, tk)
        do = do_ref[rows, :]
        doms = (jnp.where(first, do, jnp.zeros_like(do)), jnp.where(first, jnp.zeros_like(do), do))
        prod = _f(do) * of_ref[rows, :]
        totals = (jnp.sum(jnp.where(first, prod, 0.0), axis=1, keepdims=True),
                  jnp.sum(jnp.where(first, 0.0, prod), axis=1, keepdims=True))

        def tile(h, j, later, later_g, acc, mask=None, valid=None):
            off = pl.multiple_of(j * tk, tk)
            kb, vb = k_ref[pl.ds(off, tk), :], v_ref[pl.ds(off, tk), :]
            gate = later if valid is None else jnp.where(valid, later, -jnp.inf)
            z, lk, log_a = _sb_scores(qms[h], kb, gate, strict, mask)
            att = jnp.exp(log_a).astype(BF16)
            g = _f(att) * _dot_nt(doms[h], vb)
            before = totals[h] - later_g
            if valid is not None:
                before = jnp.where(valid, before, 0.0)
            dz = g - (before - _dot_split(g, strict)) * jnp.exp(z + lk)
            if mask is not None:
                dz = jnp.where(mask, dz, 0.0)
            dzb = dz.astype(BF16)
            rows = jnp.sum(lk, axis=1, keepdims=True)
            carry = (later + (rows if valid is None else jnp.where(valid, rows, 0.0)),
                     later_g + jnp.sum(g, axis=1, keepdims=True), acc + jnp.dot(dzb, kb, preferred_element_type=F32))
            return carry, _dot_tn(dzb, qms[h]), _dot_tn(att, doms[h])

        def tail(c):
            off = pl.multiple_of(c[0] * tk, tk)
            (c0, dk0, dv0), (c1, dk1, dv1) = [tile(h, c[0], c[1][h], c[2][h], c[3][h]) for h in range(2)]
            dk_ref[pl.ds(off, tk), :] += dk0 + dk1
            dv_ref[pl.ds(off, tk), :] += dv0 + dv1
            return (c[0] - 1,) + tuple(zip(c0, c1))

        zero, zacc = jnp.zeros((tq, 1), F32), jnp.zeros((tq, LANES), F32)
        blocks = [j0] + [jnp.maximum(j0 - n, 0) for n in range(1, SB_STRAIGHT)]
        carries, dks, dvs = [], [], []
        for h in range(2):
            carry, dk, dv = tile(h, j0, zero, zero, zacc, mask=diag_mask)
            dks.append([dk])
            dvs.append([dv])
            for n in range(1, SB_STRAIGHT):
                carry, dk, dv = tile(h, blocks[n], *carry, valid=j0 >= n)
                dks[h].append(dk)
                dvs[h].append(dv)
            carries.append(carry)
        for n, j in enumerate(blocks):
            off = pl.multiple_of(j * tk, tk)
            dk_ref[pl.ds(off, tk), :] += dks[0][n] + dks[1][n]
            dv_ref[pl.ds(off, tk), :] += dvs[0][n] + dvs[1][n]
        def finish():
            accs = lax.while_loop(_sb_continue, tail, (j0 - SB_STRAIGHT,) + tuple(zip(carries[0], carries[1])))[3]
            dq_ref[rows, :] = jnp.where(first, accs[0], accs[1]) * (SB_HEAD_DIM ** -0.5)

        return finish

    nq = nq // subs
    full = pl.BlockSpec((L, LANES), lambda p, i: (0, p))
    tile_spec = pl.BlockSpec((subs * tq, LANES), lambda p, i: (i, p))
    return pl.pallas_call(
        body, name="sb_bwd", grid=(2, nq),
        in_specs=[pl.BlockSpec((subs * tq, LANES), lambda p, i: (i, qb + p)),
                  pl.BlockSpec((L, LANES), lambda p, i: (0, qb + 2 + p), pipeline_mode=pl.Buffered(1)),
                  pl.BlockSpec((L, LANES), lambda p, i: (0, qb + 4 + p), pipeline_mode=pl.Buffered(1)), tile_spec, tile_spec],
        out_specs=[tile_spec, full, full],
        out_shape=[jax.ShapeDtypeStruct((L, 2 * LANES), F32)] * 3,
        compiler_params=_params(("arbitrary", "arbitrary")),
    )(P, P, P, dyb, yb32)


MOD_SHARD = N_MOD * D_MODEL // N_CHIPS


def _mod_fwd(c_all, mod_w, mod_b_sh):
    tn = 512

    def body(c_ref, w_ref, b_ref, o_ref):
        o_ref[0] = jnp.dot(_silu(c_ref[...]), w_ref[0], precision=HIGHEST, preferred_element_type=F32) + b_ref[0]

    return pl.pallas_call(
        body, name="mod_fwd", grid=(DEPTH, MOD_SHARD // tn),
        in_specs=[pl.BlockSpec((N_DEV, D_MODEL), lambda l, j: (0, 0)),
                  pl.BlockSpec((1, D_MODEL, tn), lambda l, j: (l, 0, j)),
                  pl.BlockSpec((1, 1, tn), lambda l, j: (l, 0, j))],
        out_specs=pl.BlockSpec((1, N_DEV, tn), lambda l, j: (l, 0, j)),
        out_shape=jax.ShapeDtypeStruct((DEPTH, N_DEV, MOD_SHARD), F32),
        compiler_params=_params(("arbitrary", "arbitrary")),
    )(c_all, mod_w, mod_b_sh)


def _mod_bwd(c_all, dmod_sh):
    tn = 512

    def body(c_ref, d_ref, o_ref):
        o_ref[0] = lax.dot_general(_silu(c_ref[...]), d_ref[0], (((0,), (0,)), ((), ())), precision=HIGHEST,
                                   preferred_element_type=F32)

    return pl.pallas_call(
        body, name="mod_bwd", grid=(DEPTH, MOD_SHARD // tn),
        in_specs=[pl.BlockSpec((N_DEV, D_MODEL), lambda l, j: (0, 0)),
                  pl.BlockSpec((1, N_DEV, tn), lambda l, j: (l, 0, j))],
        out_specs=pl.BlockSpec((1, D_MODEL, tn), lambda l, j: (l, 0, j)),
        out_shape=jax.ShapeDtypeStruct((DEPTH, D_MODEL, MOD_SHARD), F32),
        compiler_params=_params(("arbitrary", "arbitrary")),
    )(c_all, dmod_sh)


def _row_tile(rows, cap):
    if rows <= cap:
        return rows
    best = None
    for t in range(8, cap + 1, 8):
        if rows % t == 0:
            best = t
    assert best is not None, (rows, cap)
    return best


def _adamw(name, w, gs, m, v, tr=256):
    R, W = w.shape
    by_layer = any(isinstance(t, tuple) for t in gs)
    tr = _row_tile(R // 2 if by_layer else R, tr)
    per = (R // 2) // tr

    flat, specs = [], []
    for t in gs:
        if isinstance(t, tuple):
            flat += list(t)
            specs += [pl.BlockSpec((tr, W), lambda i: (jnp.minimum(i, per - 1), 0)),
                      pl.BlockSpec((tr, W), lambda i: (jnp.maximum(i - per, 0), 0))]
        else:
            flat.append(t)
            specs.append(pl.BlockSpec((tr, W), lambda i: (i, 0)))
    ng = len(flat)

    def body(*refs):
        w_ref, g_refs, (m_ref, v_ref) = refs[0], list(refs[1:1 + ng]), refs[1 + ng:3 + ng]
        g_out, d_out, m_out, v_out = refs[3 + ng:]
        g = None
        for t in gs:
            if isinstance(t, tuple):
                lo, hi = g_refs.pop(0), g_refs.pop(0)
                term = jnp.where(pl.program_id(0) < per, lo[...], hi[...])
            else:
                term = g_refs.pop(0)[...]
            g = term if g is None else g + term
        mm = ADAM_B1 * m_ref[...] + (1.0 - ADAM_B1) * g
        vv = ADAM_B2 * v_ref[...] + (1.0 - ADAM_B2) * (g * g)
        m_hat = mm / (1.0 - ADAM_B1 ** ADAM_STEP)
        v_hat = vv / (1.0 - ADAM_B2 ** ADAM_STEP)
        g_out[...] = g
        d_out[...] = -ADAM_LR * (m_hat / (jnp.sqrt(v_hat) + ADAM_EPS) + ADAM_WD * w_ref[...])
        m_out[...] = mm
        v_out[...] = vv

    spec = pl.BlockSpec((tr, W), lambda i: (i, 0))
    return pl.pallas_call(
        body, name=name, grid=(R // tr,), in_specs=[spec] + specs + [spec, spec], out_specs=[spec] * 4,
        out_shape=[jax.ShapeDtypeStruct((R, W), F32)] * 4, compiler_params=_params(("arbitrary",)),
    )(w, *flat, m, v)


def _sum_slots(name, a, tr=256):
    n, R, W = a.shape
    tr = _row_tile(R, tr)

    def body(a_ref, o_ref):
        acc = _f(a_ref[0])
        for j in range(1, n):
            acc = acc + _f(a_ref[j])
        o_ref[...] = acc

    return pl.pallas_call(
        body, name=name, grid=(R // tr,), in_specs=[pl.BlockSpec((n, tr, W), lambda i: (0, i, 0))],
        out_specs=pl.BlockSpec((tr, W), lambda i: (i, 0)), out_shape=jax.ShapeDtypeStruct((R, W), F32),
        compiler_params=_params(("arbitrary",)),
    )(a)


def _here():
    return lax.axis_index("x"), lax.axis_index("y"), lax.axis_index("c")


def _flip(v, d):
    return 1 - v if d else v


def _allgather_small(name, buf):
    R = buf.shape[0]
    rel = [(dx, dy, dc) for dx in (0, 1) for dy in (0, 1) for dc in (0, 1)][1:]

    def body(x_ref, o_ref, send, recv, lsem):
        x, y, c = _here()
        me = 4 * x + 2 * y + c
        mine = pltpu.make_async_copy(x_ref, o_ref.at[me], lsem)
        mine.start()

        def copy(k, slot):
            dx, dy, dc = rel[k]
            return pltpu.make_async_remote_copy(
                src_ref=x_ref, dst_ref=o_ref.at[slot], send_sem=send.at[k], recv_sem=recv.at[k],
                device_id=(_flip(x, dx), _flip(y, dy), _flip(c, dc)), device_id_type=MESH_ID)

        sent = [copy(k, me) for k in range(len(rel))]
        for cp in sent:
            cp.start()
        for k, (dx, dy, dc) in enumerate(rel):
            copy(k, 4 * _flip(x, dx) + 2 * _flip(y, dy) + _flip(c, dc)).wait_recv()
        for cp in sent:
            cp.wait_send()
        mine.wait()

    return pl.pallas_call(
        body, name=name, out_shape=jax.ShapeDtypeStruct((N_DEV, R, LANES), F32),
        in_specs=[pl.BlockSpec(memory_space=pltpu.VMEM)], out_specs=pl.BlockSpec(memory_space=pltpu.VMEM),
        scratch_shapes=[pltpu.SemaphoreType.DMA((7,)), pltpu.SemaphoreType.DMA((7,)), pltpu.SemaphoreType.DMA],
    )(buf)


CHIP_REL = [(1, 0), (0, 1), (1, 1)]


class _Side:
    def __init__(self, arrays, out_shapes, scratch, start, finish):
        self.arrays, self.out_shapes, self.scratch, self.start, self.finish = arrays, out_shapes, scratch, start, finish


def _chip_of(k):
    x, y, _ = _here()
    dx, dy = CHIP_REL[k]
    return _flip(x, dx), _flip(y, dy)


def _scatter_side(arrays):
    n = len(arrays)

    def parts(ins, outs, sems):
        send, recv, lsem = sems
        x, y, c = _here()
        s = 2 * x + y

        def copy(w, k, mine):
            px, py = _chip_of(k)
            return pltpu.make_async_remote_copy(
                src_ref=ins[w].at[2 * px + py], dst_ref=outs[w].at[s if mine else 2 * px + py],
                send_sem=send.at[3 * w + k], recv_sem=recv.at[3 * w + k], device_id=(px, py, c), device_id_type=MESH_ID)

        local = [pltpu.make_async_copy(ins[w].at[s], outs[w].at[s], lsem.at[w]) for w in range(n)]
        return copy, local

    def start(ins, outs, sems):
        copy, local = parts(ins, outs, sems)
        for cp in local:
            cp.start()
        for w in range(n):
            for k in range(3):
                copy(w, k, True).start()

    def finish(ins, outs, sems):
        copy, local = parts(ins, outs, sems)
        for w in range(n):
            for k in range(3):
                copy(w, k, False).wait_recv()
        for w in range(n):
            for k in range(3):
                copy(w, k, True).wait_send()
        for cp in local:
            cp.wait()

    scratch = [pltpu.SemaphoreType.DMA((3 * n,)), pltpu.SemaphoreType.DMA((3 * n,)), pltpu.SemaphoreType.DMA((n,))]
    return _Side(arrays, [jax.ShapeDtypeStruct(a.shape, a.dtype) for a in arrays], scratch, start, finish)


def _gather_side(shards):
    n = len(shards)

    def parts(ins, outs, sems):
        send, recv, fsend, frecv, lsem = sems
        x, y, c = _here()
        s = 2 * x + y

        def half(ref, w, which):
            rows = shards[w].shape[0] // 2
            return ref.at[pl.ds(pl.multiple_of(which * rows, 16), rows)]

        def over_ici(w, k, mine):
            px, py = _chip_of(k)
            return pltpu.make_async_remote_copy(
                src_ref=half(ins[w], w, c), dst_ref=half(outs[w].at[s if mine else 2 * px + py], w, c),
                send_sem=send.at[3 * w + k], recv_sem=recv.at[3 * w + k], device_id=(px, py, c), device_id_type=MESH_ID)

        def to_sibling(w, k, which):
            px, py = _chip_of(k)
            part = half(outs[w].at[2 * px + py], w, which)
            return pltpu.make_async_remote_copy(
                src_ref=part, dst_ref=part, send_sem=fsend.at[3 * w + k], recv_sem=frecv.at[3 * w + k],
                device_id=(x, y, 1 - c), device_id_type=MESH_ID)

        local = [pltpu.make_async_copy(ins[w], outs[w].at[s], lsem.at[w]) for w in range(n)]
        return c, over_ici, to_sibling, local

    def start(ins, outs, sems):
        _, over_ici, _, local = parts(ins, outs, sems)
        for cp in local:
            cp.start()
        for w in range(n):
            for k in range(3):
                over_ici(w, k, True).start()

    def finish(ins, outs, sems):
        c, over_ici, to_sibling, local = parts(ins, outs, sems)
        for w in range(n):
            for k in range(3):
                over_ici(w, k, False).wait_recv()
                to_sibling(w, k, c).start()
        for w in range(n):
            for k in range(3):
                to_sibling(w, k, 1 - c).wait_recv()
        for w in range(n):
            for k in range(3):
                over_ici(w, k, True).wait_send()
                to_sibling(w, k, c).wait_send()
        for cp in local:
            cp.wait()

    scratch = [pltpu.SemaphoreType.DMA((3 * n,))] * 4 + [pltpu.SemaphoreType.DMA((n,))]
    return _Side(shards, [jax.ShapeDtypeStruct((N_CHIPS,) + a.shape, a.dtype) for a in shards], scratch, start, finish)


def _sibling_exchange(name, arrays):
    n = len(arrays)

    def body(*refs):
        ins, outs = refs[:n], refs[n:2 * n]
        send, recv = refs[2 * n:]
        x, y, c = _here()
        cps = [pltpu.make_async_remote_copy(src_ref=ins[w], dst_ref=outs[w], send_sem=send.at[w], recv_sem=recv.at[w],
                                            device_id=(x, y, 1 - c), device_id_type=MESH_ID) for w in range(n)]
        for cp in cps:
            cp.start()
        for cp in cps:
            cp.wait()

    any_spec = pl.BlockSpec(memory_space=pl.ANY)
    return pl.pallas_call(
        body, name=name, out_shape=[jax.ShapeDtypeStruct(a.shape, a.dtype) for a in arrays],
        in_specs=[any_spec] * n, out_specs=[any_spec] * n,
        scratch_shapes=[pltpu.SemaphoreType.DMA((n,)), pltpu.SemaphoreType.DMA((n,))],
    )(*arrays)


def _pack(arrs):
    flat = jnp.concatenate([a.reshape(-1).astype(F32) for a in arrs])
    n = flat.shape[0]
    rows = -(-n // (8 * LANES)) * 8
    return jnp.pad(flat, (0, rows * LANES - n)).reshape(rows, LANES)


def _unpack(buf, shapes):
    lead = buf.shape[:-2]
    flat = buf.reshape(lead + (-1,))
    out, off = [], 0
    for s in shapes:
        n = 1
        for d in s:
            n *= d
        out.append(flat[..., off:off + n].reshape(lead + tuple(s)))
        off += n
    return out


def _pad_w_in(w):
    return jnp.concatenate([w[:, :2048], w[:, 2816:2824], jnp.zeros((w.shape[0], P_XBC - P_DT - 8), w.dtype),
                            w[:, 2048:2816], w[:, 2824:]], axis=1)


def _unpad_w_in(g):
    return jnp.concatenate([g[:, :P_DT], g[:, P_XBC:P_G], g[:, P_DT:P_DT + 8], g[:, P_G:]], axis=1)


FFN_HALF = FFN_HIDDEN // 2


def _ffn_in_cols(w):
    h = FFN_HALF
    return jnp.concatenate([w[:, :h], w[:, 2 * h:3 * h], w[:, h:2 * h], w[:, 3 * h:]], axis=1)


def _row(v):
    return v.reshape(1, -1)


BIG = (("w_in", 2), ("w_sc_out", 2), ("w_sb_out", 2), ("w_ssm_out", 2), ("w_o", 1), ("w_ffn_in", 2), ("w_ffn_out", 1))
SMALL = ("mod_b", "g_pre_mix", "g_post_mix", "g_pre_ffn", "g_post_ffn", "sc_conv_w", "ssm_conv_w", "ssm_conv_b",
         "ssm_dt_bias", "ssm_a_log", "ssm_d", "ssm_norm_w")
WEIGHT_ORDER = ("mod_w", "mod_b", "g_pre_mix", "g_post_mix", "g_pre_ffn", "g_post_ffn", "w_in", "sc_conv_w",
                "ssm_conv_w", "ssm_conv_b", "ssm_dt_bias", "ssm_a_log", "ssm_d", "ssm_norm_w", "w_sc_out", "w_sb_out",
                "w_ssm_out", "w_o", "w_ffn_in", "w_ffn_out")


def _mm_mid(name, a, w, x, vecs):
    return _mm_epi(name, a, w, "nn", D_MODEL, [(x, D_MODEL)] + [(v, None) for v in vecs],
                   lambda p, x, *v: (p,) + tuple(_mid(x, p, *v)), [(D_MODEL, BF16), (D_MODEL, F32), (D_MODEL, BF16)])


def _layer_fwd(l, x_in, h, W, V, sides, next_vecs):
    S = {"x_in": x_in, "h": h}
    side, handler = sides.get("in_proj", (None, None))
    P = _mm(f"in_proj{l}", h, W["w_in"], "nn", BF16, tm=2048, tn_cap=1024, side=side)
    if side:
        handler(P[1:])
        P = P[0]
    S["P"] = P
    S["ya"] = _shortconv_fwd(P, V["sc_w"])
    S["yb"], S["yb32"] = _sb_fwd(P)
    S["pre"] = _ssmconv_fwd(P, V["ssm_w"], V["ssm_b"])
    S["y_ssd"], S["states"] = _ssd_fwd(S["pre"], P, V["dtb"], V["alog"])
    S["yc"] = _rowwise(f"ssm_post{l}", lambda y, px, z, d, nw: _ssm_post(y, px, _f(z), d, nw),
                       [(S["y_ssd"], SSM_INNER, 0), (S["pre"], SSM_INNER, 0), (P, SSM_INNER, P_Z // SSM_INNER)],
                       [V["d_full"], V["norm_w"]], [(SSM_INNER, BF16)])[0]
    S["merged"] = _merge_fwd(f"merge{l}", P, [S["ya"], S["yb"], S["yc"]],
                             [W["w_sc_out"], W["w_sb_out"], W["w_ssm_out"]])
    S["mix"], S["x1"], S["h2"] = _mm_mid(f"w_o{l}", S["merged"], W["w_o"], x_in, V["mid_mix"])
    side, handler = sides.get("ffn_in", (None, None))
    res = _mm_epi(f"ffn_in{l}", S["h2"], W["w_ffn_in"], "nn", 2 * FFN_HALF, [],
                  lambda p: (p, _swiglu(p[:, :FFN_HALF], p[:, FFN_HALF:])),
                  [(2 * FFN_HALF, BF16), (FFN_HALF, BF16)], side=side)
    S["GU"], S["act"] = res[0], res[1]
    if side:
        handler(res[2:])
    if next_vecs is None:
        S["f"] = _mm(f"ffn_out{l}", S["act"], W["w_ffn_out"], "nn", BF16)
    else:
        S["f"], S["x_next"], S["h_next"] = _mm_mid(f"ffn_out{l}", S["act"], W["w_ffn_out"], S["x1"], next_vecs)
    return S


BRANCH_WIDTHS = (SC_WIDTH, 256, SSM_INNER)


def _branch_specs(tm):
    gb = P_G // D_MODEL
    gates = [pl.BlockSpec((tm, D_MODEL), functools.partial(lambda i, cb: (i, cb), cb=gb + k)) for k in range(3)]
    ys = [pl.BlockSpec((tm, w), lambda i: (i, 0)) for w in BRANCH_WIDTHS]
    ws = [pl.BlockSpec((w, D_MODEL), lambda i: (0, 0)) for w in BRANCH_WIDTHS]
    return gates, ys, ws


def _merge_fwd(name, P, ys, ws, tm=512):
    L = P.shape[0]
    tm = min(tm, L)
    gates, y_specs, w_specs = _branch_specs(tm)

    def body(ga, gb, gc, ya, yb, yc, wa, wb, wc, o_ref):
        acc = None
        for g_ref, y_ref, w_ref in ((ga, ya, wa), (gb, yb, wb), (gc, yc, wc)):
            t = jax.nn.sigmoid(_f(g_ref[...])) * jnp.dot(y_ref[...], w_ref[...], preferred_element_type=F32)
            acc = t if acc is None else acc + t
        o_ref[...] = acc.astype(o_ref.dtype)

    return pl.pallas_call(
        body, name=name, grid=(L // tm,), in_specs=gates + y_specs + w_specs,
        out_specs=pl.BlockSpec((tm, D_MODEL), lambda i: (i, 0)), out_shape=jax.ShapeDtypeStruct((L, D_MODEL), BF16),
        compiler_params=_params(("arbitrary",)),
    )(P, P, P, *ys, *ws)


def _merge_bwd(name, P, ys, ws, dmerged, tm=512):
    L = P.shape[0]
    tm = min(tm, L)
    gates, y_specs, w_specs = _branch_specs(tm)

    def body(ga, gb, gc, ya, yb, yc, wa, wb, wc, dm_ref, dg_ref, dya, dyb, dyc, gwa, gwb, gwc):
        @pl.when(pl.program_id(0) == 0)
        def _():
            for r in (gwa, gwb, gwc):
                r[...] = jnp.zeros_like(r)

        dm = _f(dm_ref[...])
        for k, (g_ref, y_ref, w_ref, dy_ref, gw_ref) in enumerate(
                ((ga, ya, wa, dya, gwa), (gb, yb, wb, dyb, gwb), (gc, yc, wc, dyc, gwc))):
            y, w = y_ref[...], w_ref[...]
            s = jax.nn.sigmoid(_f(g_ref[...]))
            proj = jnp.dot(y, w, preferred_element_type=F32)
            d_proj = (dm * s).astype(BF16)
            dg_ref[:, k * D_MODEL:(k + 1) * D_MODEL] = (dm * proj * s * (1.0 - s)).astype(dg_ref.dtype)
            dy_ref[...] = _dot_nt(d_proj, w).astype(dy_ref.dtype)
            gw_ref[...] += _dot_tn(y, d_proj)

    gate_cols = pl.BlockSpec((tm, P_WIDTH - P_G), lambda i: (i, P_G // (P_WIDTH - P_G)))
    return pl.pallas_call(
        body, name=name, grid=(L // tm,),
        in_specs=gates + y_specs + w_specs + [pl.BlockSpec((tm, D_MODEL), lambda i: (i, 0))],
        out_specs=[gate_cols] + y_specs + w_specs,
        out_shape=[jax.ShapeDtypeStruct((L, P_WIDTH), BF16)] + [jax.ShapeDtypeStruct((L, w), BF16) for w in BRANCH_WIDTHS]
        + [jax.ShapeDtypeStruct((w, D_MODEL), F32) for w in BRANCH_WIDTHS],
        compiler_params=_params(("arbitrary",)),
    )(P, P, P, *ys, *ws, dmerged)


def _assemble_dp(name, dP, parts, tl=512):
    L = dP.shape[0]
    tl = min(tl, L)
    n = len(parts)

    def body(*refs):
        o_ref = refs[n + 1]
        o_ref[...] = jnp.concatenate([r[...].astype(o_ref.dtype) for r in refs[:n]], axis=1)

    return pl.pallas_call(
        body, name=name, grid=(L // tl,),
        in_specs=[pl.BlockSpec((tl, a.shape[1]), lambda i: (i, 0)) for a in parts] + [pl.BlockSpec(memory_space=pl.ANY)],
        out_specs=pl.BlockSpec((tl, P_G), lambda i: (i, 0)), out_shape=jax.ShapeDtypeStruct(dP.shape, dP.dtype),
        input_output_aliases={n: 0}, compiler_params=_params(("arbitrary",)),
    )(*parts, dP)


def _layer_bwd(l, S, W, V, dx1, df, sides, landed):
    G = {}
    P = S["P"]

    def mm(key, *args, **kw):
        if key not in sides:
            return _mm(f"{key}{l}", *args, **kw)
        names, layer, make = sides[key]
        res = _mm(f"{key}{l}", *args, side=make(G), **kw)
        for n, a in zip(names, res[1:]):
            landed[(n, layer)] = a
        return res[0]

    G["w_ffn_out"] = _mm(f"gw_ffn_out{l}", S["act"], df, "tn", F32)

    def swiglu_bwd(d_act, gu):
        gt, up = _f(gu[:, :FFN_HALF]), _f(gu[:, FFN_HALF:])
        s = jax.nn.sigmoid(gt)
        gs = gt * s
        return (jnp.concatenate([d_act * up * (s + gs * (1.0 - s)), d_act * gs], axis=1),)

    names, layer, make = sides.get("d_gu", ((), None, None))
    res = _mm_epi(f"d_gu{l}", df, W["w_ffn_out"], "nt", FFN_HALF, [(S["GU"], 2 * FFN_HALF)], swiglu_bwd,
                  [(2 * FFN_HALF, BF16)], side=make(G) if make else None)
    dGU = res[0]
    for n, a in zip(names, res[1:]):
        landed[(n, layer)] = a
    dh2 = mm("d_h2", dGU, W["w_ffn_in"], "nt", BF16)
    G["w_ffn_in"] = _ffn_in_cols(mm("gw_ffn_in", S["h2"], dGU, "tn", F32))
    dx, dmix, G["gate1"], G["g_post_mix"], G["g_pre_ffn"], G["scale2"], G["shift2"] = _mid_bwd(
        f"mid_mix_bwd{l}", S["x_in"], S["mix"], dx1, dh2, V["mid_mix"])
    dmerged = _mm(f"d_merged{l}", dmix, W["w_o"], "nt", BF16)
    G["w_o"] = _mm(f"gw_o{l}", S["merged"], dmix, "tn", F32)

    dP, dya, dyb, dyc, G["w_sc_out"], G["w_sb_out"], G["w_ssm_out"] = _merge_bwd(
        f"merge_bwd{l}", P, [S["ya"], S["yb"], S["yc"]], [W["w_sc_out"], W["w_sb_out"], W["w_ssm_out"]], dmerged)

    def post_bwd(y_ssd, px, z, d, dfull, nw):
        z, d = _f(z), _f(d)
        sx, sz = jax.nn.sigmoid(px), jax.nn.sigmoid(z)
        xs, gz = px * sx, z * sz
        t = y_ssd + xs * dfull
        y = t * gz
        dn = d * nw
        half = SSM_INNER // 2
        ns, dys = [], []
        for g in range(2):
            yg, dng = y[:, g * half:(g + 1) * half], dn[:, g * half:(g + 1) * half]
            r = lax.rsqrt(jnp.mean(yg * yg, axis=-1, keepdims=True) + NORM_EPS)
            ns.append(yg * r)
            dys.append(r * (dng - ns[g] * jnp.mean(dng * ns[g], axis=-1, keepdims=True)))
        n, dy = jnp.concatenate(ns, axis=1), jnp.concatenate(dys, axis=1)
        dt = dy * gz
        return (dt, dt * dfull * (sx + xs * (1.0 - sx)), dy * t * (sz + gz * (1.0 - sz)),
                jnp.sum(dt * xs, axis=0, keepdims=True), jnp.sum(d * n, axis=0, keepdims=True))

    dy_ssd, dxs, dz, G["d_full"], G["ssm_norm_w"] = _rowwise(
        f"ssm_post_bwd{l}", post_bwd,
        [(S["y_ssd"], SSM_INNER, 0), (S["pre"], SSM_INNER, 0), (P, SSM_INNER, P_Z // SSM_INNER), (dyc, SSM_INNER, 0)],
        [V["d_full"], V["norm_w"]], [(SSM_INNER, F32), (SSM_INNER, F32), (SSM_INNER, BF16)], [(1, SSM_INNER)] * 2)
    dpre, ddt, G["dtb"], G["alog"] = _ssd_bwd(S["pre"], P, S["states"], dy_ssd, dxs, V["dtb"], V["alog"])
    dxbc, w0, w1, w2, w3, G["ssm_conv_b"] = _ssmconv_bwd(P, dpre, V["ssm_w"])
    G["ssm_conv_w"] = jnp.concatenate([w0, w1, w2, w3], axis=0)
    dq, dk, dv = _sb_bwd(P, dyb, S["yb32"])
    dA, s0, s1, s2 = _shortconv_bwd(P, dya, V["sc_w"])
    G["sc_conv_w"] = jnp.concatenate([s0, s1, s2], axis=0)
    dP = _assemble_dp(f"assemble_dp{l}", dP, [dA, dq, dk, dv, dz, ddt, dxbc])
    G["w_in"] = _mm(f"gw_in{l}", S["h"], dP, "tn", F32, tn_cap=1024)
    dh = mm("d_h", dP, W["w_in"], "nt", BF16, tk_cap=3072)
    return dx, dh, G


def kernel(x, c, mod_w, mod_b, g_pre_mix, g_post_mix, g_pre_ffn, g_post_ffn, w_in, sc_conv_w, ssm_conv_w, ssm_conv_b, ssm_dt_bias, ssm_a_log, ssm_d, ssm_norm_w, w_sc_out, w_sb_out, w_ssm_out, w_o, w_ffn_in, w_ffn_out, loss_target, m_mod_w, m_mod_b, m_g_pre_mix, m_g_post_mix, m_g_pre_ffn, m_g_post_ffn, m_w_in, m_sc_conv_w, m_ssm_conv_w, m_ssm_conv_b, m_ssm_dt_bias, m_ssm_a_log, m_ssm_d, m_ssm_norm_w, m_w_sc_out, m_w_sb_out, m_w_ssm_out, m_w_o, m_w_ffn_in, m_w_ffn_out, v_mod_w, v_mod_b, v_g_pre_mix, v_g_post_mix, v_g_pre_ffn, v_g_post_ffn, v_w_in, v_sc_conv_w, v_ssm_conv_w, v_ssm_conv_b, v_ssm_dt_bias, v_ssm_a_log, v_ssm_d, v_ssm_norm_w, v_w_sc_out, v_w_sb_out, v_w_ssm_out, v_w_o, v_w_ffn_in, v_w_ffn_out):
    wts = dict(mod_w=mod_w, mod_b=mod_b, g_pre_mix=g_pre_mix, g_post_mix=g_post_mix, g_pre_ffn=g_pre_ffn,
               g_post_ffn=g_post_ffn, w_in=w_in, sc_conv_w=sc_conv_w, ssm_conv_w=ssm_conv_w, ssm_conv_b=ssm_conv_b,
               ssm_dt_bias=ssm_dt_bias, ssm_a_log=ssm_a_log, ssm_d=ssm_d, ssm_norm_w=ssm_norm_w, w_sc_out=w_sc_out,
               w_sb_out=w_sb_out, w_ssm_out=w_ssm_out, w_o=w_o, w_ffn_in=w_ffn_in, w_ffn_out=w_ffn_out)
    mom = dict(mod_w=m_mod_w, mod_b=m_mod_b, g_pre_mix=m_g_pre_mix, g_post_mix=m_g_post_mix, g_pre_ffn=m_g_pre_ffn,
               g_post_ffn=m_g_post_ffn, w_in=m_w_in, sc_conv_w=m_sc_conv_w, ssm_conv_w=m_ssm_conv_w,
               ssm_conv_b=m_ssm_conv_b, ssm_dt_bias=m_ssm_dt_bias, ssm_a_log=m_ssm_a_log, ssm_d=m_ssm_d,
               ssm_norm_w=m_ssm_norm_w, w_sc_out=m_w_sc_out, w_sb_out=m_w_sb_out, w_ssm_out=m_w_ssm_out, w_o=m_w_o,
               w_ffn_in=m_w_ffn_in, w_ffn_out=m_w_ffn_out)
    var = dict(mod_w=v_mod_w, mod_b=v_mod_b, g_pre_mix=v_g_pre_mix, g_post_mix=v_g_post_mix, g_pre_ffn=v_g_pre_ffn,
               g_post_ffn=v_g_post_ffn, w_in=v_w_in, sc_conv_w=v_sc_conv_w, ssm_conv_w=v_ssm_conv_w,
               ssm_conv_b=v_ssm_conv_b, ssm_dt_bias=v_ssm_dt_bias, ssm_a_log=v_ssm_a_log, ssm_d=v_ssm_d,
               ssm_norm_w=v_ssm_norm_w, w_sc_out=v_w_sc_out, w_sb_out=v_w_sb_out, w_ssm_out=v_w_ssm_out, w_o=v_w_o,
               w_ffn_in=v_w_ffn_in, w_ffn_out=v_w_ffn_out)
    xi, yi, ci = _here()
    chip = 2 * xi + yi
    me = 4 * xi + 2 * yi + ci
    x0, target = x[0], loss_target[0]

    first_shapes = [(D_MODEL,), sc_conv_w.shape, ssm_conv_w.shape]
    g0 = _allgather_small("gather_cond", _pack([c, sc_conv_w, ssm_conv_w]))
    c_rows, sc_sh, ssm_sh = _unpack(g0, first_shapes)
    c_all = c_rows
    sc_w = jnp.concatenate([sc_sh[2 * j] for j in range(N_CHIPS)], axis=-1)
    ssm_w = jnp.concatenate([ssm_sh[2 * j] for j in range(N_CHIPS)], axis=-1)

    mod_b_sh = lax.dynamic_slice_in_dim(mod_b, chip * MOD_SHARD, MOD_SHARD, axis=1).reshape(DEPTH, 1, MOD_SHARD)
    modpart = _mod_fwd(c_all, mod_w, mod_b_sh)
    g1 = _allgather_small("gather_mod", modpart.reshape(-1, LANES)).reshape(N_DEV, DEPTH, N_DEV, MOD_SHARD)
    mod = jnp.concatenate([lax.dynamic_index_in_dim(g1[2 * j], me, axis=1, keepdims=False) for j in range(N_CHIPS)],
                          axis=-1)

    def layer_shards(l):
        return [wts[n][l].astype(BF16) for n, _ in BIG]

    def full_weights(which, gathered):
        W = {n: jnp.concatenate([g[j] for j in range(N_CHIPS)], axis=ax - 1) for (n, ax), g in zip(which, gathered)}
        if "w_in" in W:
            W["w_in"] = _pad_w_in(W["w_in"])
        if "w_ffn_in" in W:
            W["w_ffn_in"] = _ffn_in_cols(W["w_ffn_in"])
        return W

    Ws = [{}, {}]
    fwd_sides = [{"in_proj": (_gather_side(layer_shards(0)[1:]), lambda got: Ws[0].update(full_weights(BIG[1:], got))),
                  "ffn_in": (_gather_side(layer_shards(1)), lambda got: Ws[1].update(full_weights(BIG, got)))}, {}]
    Vs = []
    for l in range(DEPTH):
        sh1, sc1, gt1, sh2, sc2, gt2 = [_row(v) for v in jnp.split(mod[l], N_MOD)]
        Vs.append(dict(
            shift1=sh1, scale1=sc1, g_pre_mix=_row(g_pre_mix[l]),
            mid_mix=[gt1, _row(g_post_mix[l]), _row(g_pre_ffn[l]), sc2, sh2],
            gate2=gt2, g_post_ffn=_row(g_post_ffn[l]),
            sc_w=sc_w[l], ssm_w=ssm_w[l], ssm_b=_row(ssm_conv_b[l]),
            dtb=_row(jnp.pad(ssm_dt_bias[l], (0, LANES - SSM_HEADS))), alog=_row(jnp.pad(ssm_a_log[l], (0, LANES - SSM_HEADS))),
            d_full=_row(jnp.repeat(ssm_d[l], SSM_INNER // SSM_HEADS)), norm_w=_row(ssm_norm_w[l])))

    def mid_ffn_vecs(l):
        return [Vs[l]["gate2"], Vs[l]["g_post_ffn"], Vs[l + 1]["g_pre_mix"], Vs[l + 1]["scale1"], Vs[l + 1]["shift1"]]

    saved = []
    x_in = x0
    h, *got = _first_fwd(x0, [Vs[0]["g_pre_mix"], Vs[0]["scale1"], Vs[0]["shift1"]], _gather_side(layer_shards(0)[:1]))
    Ws[0].update(full_weights(BIG[:1], got))
    for l in range(DEPTH):
        S = _layer_fwd(l, x_in, h, Ws[l], Vs[l], fwd_sides[l], mid_ffn_vecs(l) if l + 1 < DEPTH else None)
        saved.append(S)
        if l + 1 < DEPTH:
            x_in, h = S["x_next"], S["h_next"]

    def pieces(G, names):
        out = []
        for n, ax in BIG:
            if n in names:
                g = _unpad_w_in(G[n]) if n == "w_in" else G[n]
                out.append(jnp.stack(jnp.split(g, N_CHIPS, axis=ax - 1)).astype(BF16))
        return out

    small_names = tuple(n for n, _ in BIG if n not in ("w_in", "w_ffn_in"))
    late_names = tuple(n for n, _ in BIG if n != "w_in")
    landed = {}

    GL = [None] * DEPTH
    S = saved[-1]
    dx1, df, g_gate2, g_gpf, loss_cols = _last_bwd(S["x1"], S["f"], target, [Vs[-1]["gate2"], Vs[-1]["g_post_ffn"]])
    for l in reversed(range(DEPTH)):
        sides = {}
        if l + 1 < DEPTH:
            for key, names in (("d_gu", small_names), ("d_h2", ("w_ffn_in",)), ("gw_ffn_in", ("w_in",))):
                sides[key] = (names, l + 1, lambda G, up=GL[l + 1], names=names: _scatter_side(pieces(up, names)))
        if l == 0:
            sides["d_h"] = (late_names, l, lambda G: _scatter_side(pieces(G, late_names)))
        dx, dh, G = _layer_bwd(l, saved[l], Ws[l], Vs[l], dx1, df, sides, landed)
        G["gate2"], G["g_post_ffn"] = g_gate2, g_gpf
        GL[l] = G
        if l > 0:
            Sp = saved[l - 1]
            dx1, df, g_gate2, g_gpf, G["g_pre_mix"], G["scale1"], G["shift1"] = _mid_bwd(
                f"mid_ffn_bwd{l - 1}", Sp["x1"], Sp["f"], dx, dh, mid_ffn_vecs(l - 1))
        else:
            grad_x, G["g_pre_mix"], G["scale1"], G["shift1"], landed[("w_in", 0)] = _first_bwd(
                x0, dx, dh, [Vs[0]["g_pre_mix"], Vs[0]["scale1"], Vs[0]["shift1"]], _scatter_side(pieces(G, ("w_in",))))
    loss = lax.psum(jnp.sum(loss_cols), ("x", "y", "c"))

    def both(key, shape=None):
        a = jnp.stack([GL[l][key] for l in range(DEPTH)])
        return a if shape is None else a.reshape(shape)

    dmod = jnp.concatenate([both(k, (DEPTH, D_MODEL)) for k in ("shift1", "scale1", "gate1", "shift2", "scale2", "gate2")],
                           axis=1)
    part_small = dict(
        mod_b=dmod, g_pre_mix=both("g_pre_mix", (DEPTH, D_MODEL)), g_post_mix=both("g_post_mix", (DEPTH, D_MODEL)),
        g_pre_ffn=both("g_pre_ffn", (DEPTH, D_MODEL)), g_post_ffn=both("g_post_ffn", (DEPTH, D_MODEL)),
        sc_conv_w=both("sc_conv_w"), ssm_conv_w=both("ssm_conv_w"), ssm_conv_b=both("ssm_conv_b", (DEPTH, SSM_CONV_DIM)),
        ssm_dt_bias=both("dtb", (DEPTH, LANES))[:, :SSM_HEADS], ssm_a_log=both("alog", (DEPTH, LANES))[:, :SSM_HEADS],
        ssm_d=both("d_full", (DEPTH, SSM_HEADS, SSM_INNER // SSM_HEADS)).sum(-1),
        ssm_norm_w=both("ssm_norm_w", (DEPTH, SSM_INNER)))
    small_shapes = [part_small[n].shape for n in SMALL]
    g2 = _allgather_small("gather_small_grads", _pack([part_small[n] for n in SMALL]))
    tot = dict(zip(SMALL, _unpack(_sum_slots("sum_small_grads", g2), small_shapes)))
    dmod_all = _unpack(g2, small_shapes)[0]
    dmod_sh = jnp.swapaxes(lax.dynamic_slice_in_dim(dmod_all, chip * MOD_SHARD, MOD_SHARD, axis=2), 0, 1)
    grads = {"mod_w": _mod_bwd(c_all, dmod_sh)}
    for n in SMALL:
        grads[n] = tot[n]
    grads["sc_conv_w"] = lax.dynamic_slice_in_dim(tot["sc_conv_w"], chip * 64, 64, axis=2)
    grads["ssm_conv_w"] = lax.dynamic_slice_in_dim(tot["ssm_conv_w"], chip * 192, 192, axis=2)

    keys = [(n, l) for n, _ in BIG for l in range(DEPTH)]
    mine = [_sum_slots(f"sum_{n}{l}", landed[(n, l)]) for n, l in keys]
    theirs = dict(zip(keys, _sibling_exchange("swap_core_sums", mine)))
    mine = dict(zip(keys, mine))

    out = {}

    def update(name, w2, gs, m2, v2, shape):
        g, d, nm, nv = _adamw(f"adamw_{name}", w2, gs, m2, v2)
        out[name] = tuple(a.reshape(shape) for a in (g, d, nm, nv))

    for n, _ in BIG:
        shp = wts[n].shape
        two = (-1, shp[-1])
        by_layer = [tuple(src[(n, l)] for l in range(DEPTH)) for src in (mine, theirs)]
        update(n, wts[n].reshape(two), by_layer, mom[n].reshape(two), var[n].reshape(two), shp)
    two = (-1, MOD_SHARD)
    update("mod_w", mod_w.reshape(two), [grads["mod_w"].reshape(two)], m_mod_w.reshape(two), v_mod_w.reshape(two), mod_w.shape)
    shapes = [wts[n].shape for n in SMALL]
    res = _adamw("adamw_small", _pack([wts[n] for n in SMALL]), [_pack([grads[n] for n in SMALL])],
                 _pack([mom[n] for n in SMALL]), _pack([var[n] for n in SMALL]))
    for n, g, d, nm, nv in zip(SMALL, *[_unpack(r, shapes) for r in res]):
        out[n] = (g, d, nm, nv)

    result = [loss, grad_x[None]]
    for k in range(4):
        result += [out[n][k] for n in WEIGHT_ORDER]
    return tuple(result)
```

```python
import functools

import jax
import jax.numpy as jnp
from jax import lax
from jax.experimental import pallas as pl
from jax.experimental.pallas import tpu as pltpu

F32 = jnp.float32
BF16 = jnp.bfloat16
HIGHEST = lax.Precision.HIGHEST
MESH_ID = pl.DeviceIdType.MESH

D_MODEL = 1024
DEPTH = 2
SC_WIDTH = 256
SB_HEAD_DIM = 64
SSM_INNER = 512
SSM_HEADS = 8
SSM_STATE = 64
SSM_CONV = 4
SSM_CHUNK = 256
SSM_CONV_DIM = 768
FFN_HIDDEN = 2816
NORM_EPS = 1e-6
N_MOD = 6
N_CHIPS = 4
N_DEV = 8

ADAM_LR = 0.001
ADAM_B1 = 0.9
ADAM_B2 = 0.999
ADAM_EPS = 1e-08
ADAM_WD = 0.01
ADAM_STEP = 10

P_WIDTH = 6144
P_A, P_B, P_Z, P_DT, P_XBC, P_G = 0, 768, 1536, 2048, 2304, 3072
DT_PAD = 256

VMEM_LIMIT_BYTES = 56 * 1024 * 1024
LANES = 128

SB_LOG_CUTOFF = -105.0
SB_TQ = 256
SB_TK = 256
SB_SUBS = 2
SB_STRAIGHT = 2


def _params(sem):
    return pltpu.CompilerParams(dimension_semantics=sem, vmem_limit_bytes=VMEM_LIMIT_BYTES)


def _pick(n, cap):
    if n <= cap:
        return n
    best = None
    for m in range(LANES, cap + 1, LANES):
        if n % m == 0:
            best = m
    assert best is not None, (n, cap)
    return best


def _rowwise(name, fn, rows, vecs, row_outs, acc_outs=(), tl=512, side=None):
    L = rows[0][0].shape[0]
    tl = min(tl, L)
    assert L % tl == 0
    n_in = len(rows) + len(vecs)
    n_ro, n_ao = len(row_outs), len(acc_outs)
    n_si = len(side.arrays) if side else 0
    n_so = len(side.out_shapes) if side else 0

    def body(*refs):
        ins, s_in = refs[:n_in], refs[n_in:n_in + n_si]
        outs = refs[n_in + n_si:]
        ro, ao, s_out, sems = outs[:n_ro], outs[n_ro:n_ro + n_ao], outs[n_ro + n_ao:n_ro + n_ao + n_so], outs[n_ro + n_ao + n_so:]
        if side:
            @pl.when(pl.program_id(0) == 0)
            def _():
                side.start(s_in, s_out, sems)

        _rows(ins, ro, ao)
        if side:
            @pl.when(pl.program_id(0) == L // tl - 1)
            def _():
                side.finish(s_in, s_out, sems)

    def _rows(ins, ro, ao):
        vals = fn(*[r[...] for r in ins])
        if not isinstance(vals, (tuple, list)):
            vals = (vals,)
        for o, v in zip(ro, vals[:n_ro]):
            o[...] = v.astype(o.dtype)
        if ao:
            @pl.when(pl.program_id(0) == 0)
            def _():
                for o in ao:
                    o[...] = jnp.zeros_like(o)
            for o, v in zip(ao, vals[n_ro:]):
                o[...] += v.astype(F32)

    in_specs = [pl.BlockSpec((tl, w), functools.partial(lambda i, cb: (i, cb), cb=cb)) for _, w, cb in rows]
    in_specs += [pl.BlockSpec(v.shape, lambda i: (0, 0)) for v in vecs]
    out_specs = [pl.BlockSpec((tl, w), lambda i: (i, 0)) for w, _ in row_outs]
    out_specs += [pl.BlockSpec(s, lambda i: (0, 0)) for s in acc_outs]
    out_shape = [jax.ShapeDtypeStruct((L, w), dt) for w, dt in row_outs]
    out_shape += [jax.ShapeDtypeStruct(s, F32) for s in acc_outs]
    any_spec = pl.BlockSpec(memory_space=pl.ANY)
    return pl.pallas_call(
        body, name=name, grid=(L // tl,), in_specs=in_specs + [any_spec] * n_si, out_specs=out_specs + [any_spec] * n_so,
        out_shape=out_shape + (side.out_shapes if side else []), scratch_shapes=side.scratch if side else [],
        compiler_params=_params(("arbitrary",)),
    )(*[a for a, _, _ in rows], *vecs, *(side.arrays if side else []))


def _mm(name, a, b, mode, out_dtype, tm=1024, tn_cap=1408, tk_cap=2816, side=None):
    if mode == "nn":
        (M, K), (_, N) = a.shape, b.shape
    elif mode == "nt":
        (M, K), (N, _) = a.shape, b.shape
    else:
        (K, M), (_, N) = a.shape, b.shape
        tm, tk_cap = 1408, 2048
    tm = _pick(M, tm)
    tn = _pick(N, tn_cap)
    tk = _pick(K, tk_cap)
    nk = K // tk
    grid = (M // tm, N // tn, nk)
    n_si = len(side.arrays) if side else 0
    n_so = len(side.out_shapes) if side else 0
    n_acc = 1 if nk > 1 else 0

    def body(a_ref, b_ref, *rest):
        s_in, o_ref, s_out = rest[:n_si], rest[n_si], rest[n_si + 1:n_si + 1 + n_so]
        scr = rest[n_si + 1 + n_so:]
        if side:
            at = [pl.program_id(d) for d in range(3)]
            is_first = jnp.logical_and(jnp.logical_and(at[0] == 0, at[1] == 0), at[2] == 0)
            is_last = jnp.logical_and(jnp.logical_and(at[0] == grid[0] - 1, at[1] == grid[1] - 1), at[2] == grid[2] - 1)

            @pl.when(is_first)
            def _():
                side.start(s_in, s_out, scr[n_acc:])

        _product(a_ref, b_ref, o_ref, scr)
        if side:
            @pl.when(is_last)
            def _():
                side.finish(s_in, s_out, scr[n_acc:])

    def _product(a_ref, b_ref, o_ref, scr):
        if mode == "nn":
            p = jnp.dot(a_ref[...], b_ref[...], preferred_element_type=F32)
        elif mode == "nt":
            p = lax.dot_general(a_ref[...], b_ref[...], (((1,), (1,)), ((), ())), preferred_element_type=F32)
        else:
            p = lax.dot_general(a_ref[...], b_ref[...], (((0,), (0,)), ((), ())), preferred_element_type=F32)
        if nk == 1:
            o_ref[...] = p.astype(o_ref.dtype)
        else:
            acc = scr[0]
            k = pl.program_id(2)

            @pl.when(k == 0)
            def _():
                acc[...] = p

            @pl.when(k > 0)
            def _():
                acc[...] += p

            @pl.when(k == nk - 1)
            def _():
                o_ref[...] = acc[...].astype(o_ref.dtype)

    if mode == "nn":
        a_spec = pl.BlockSpec((tm, tk), lambda i, j, k: (i, k))
        b_spec = pl.BlockSpec((tk, tn), lambda i, j, k: (k, j))
    elif mode == "nt":
        a_spec = pl.BlockSpec((tm, tk), lambda i, j, k: (i, k))
        b_spec = pl.BlockSpec((tn, tk), lambda i, j, k: (j, k))
    else:
        a_spec = pl.BlockSpec((tk, tm), lambda i, j, k: (k, i))
        b_spec = pl.BlockSpec((tk, tn), lambda i, j, k: (k, j))
    any_spec = pl.BlockSpec(memory_space=pl.ANY)
    res = pl.pallas_call(
        body, name=name, grid=grid, in_specs=[a_spec, b_spec] + [any_spec] * n_si,
        out_specs=[pl.BlockSpec((tm, tn), lambda i, j, k: (i, j))] + [any_spec] * n_so,
        out_shape=[jax.ShapeDtypeStruct((M, N), out_dtype)] + (side.out_shapes if side else []),
        scratch_shapes=([pltpu.VMEM((tm, tn), F32)] if nk > 1 else []) + (side.scratch if side else []),
        compiler_params=_params(("arbitrary", "arbitrary", "arbitrary")),
    )(a, b, *(side.arrays if side else []))
    return res if side else res[0]


def _mm_epi(name, a, b, mode, tn, extras, epi, outs, tm=512, side=None):
    if mode == "nn":
        (M, K), (_, N) = a.shape, b.shape
    else:
        (M, K), (N, _) = a.shape, b.shape
    tm = _pick(M, tm)
    grid = (N // tn, M // tm)
    n_ex, n_out = len(extras), len(outs)
    n_si = len(side.arrays) if side else 0
    n_so = len(side.out_shapes) if side else 0

    def body(*refs):
        a_ref, b_ref, ex = refs[0], refs[1], refs[2:2 + n_ex]
        s_in = refs[2 + n_ex:2 + n_ex + n_si]
        o_refs = refs[2 + n_ex + n_si:2 + n_ex + n_si + n_out]
        s_out = refs[2 + n_ex + n_si + n_out:2 + n_ex + n_si + n_out + n_so]
        sems = refs[2 + n_ex + n_si + n_out + n_so:]
        if side:
            @pl.when(jnp.logical_and(pl.program_id(0) == 0, pl.program_id(1) == 0))
            def _():
                side.start(s_in, s_out, sems)

        if mode == "nn":
            p = jnp.dot(a_ref[...], b_ref[...], preferred_element_type=F32)
        else:
            p = lax.dot_general(a_ref[...], b_ref[...], (((1,), (1,)), ((), ())), preferred_element_type=F32)
        for o, v in zip(o_refs, epi(p, *[r[...] for r in ex])):
            o[...] = v.astype(o.dtype)
        if side:
            @pl.when(jnp.logical_and(pl.program_id(0) == grid[0] - 1, pl.program_id(1) == grid[1] - 1))
            def _():
                side.finish(s_in, s_out, sems)

    any_spec = pl.BlockSpec(memory_space=pl.ANY)
    a_spec = pl.BlockSpec((tm, K), lambda j, i: (i, 0))
    b_spec = pl.BlockSpec((K, tn), lambda j, i: (0, j)) if mode == "nn" else pl.BlockSpec((tn, K), lambda j, i: (j, 0))
    return pl.pallas_call(
        body, name=name, grid=grid,
        in_specs=[a_spec, b_spec]
        + [pl.BlockSpec(e.shape, lambda j, i: (0, 0)) if w is None else pl.BlockSpec((tm, w), lambda j, i: (i, j))
           for e, w in extras] + [any_spec] * n_si,
        out_specs=[pl.BlockSpec((tm, w), lambda j, i: (i, j)) for w, _ in outs] + [any_spec] * n_so,
        out_shape=[jax.ShapeDtypeStruct((M, (N // tn) * w), dt) for w, dt in outs] + (side.out_shapes if side else []),
        scratch_shapes=side.scratch if side else [],
        compiler_params=_params(("arbitrary", "arbitrary")),
    )(a, b, *[e for e, _ in extras], *(side.arrays if side else []))


def _f(x):
    return x.astype(F32)


def _silu(x):
    return x * jax.nn.sigmoid(x)


def _softplus(x):
    return jnp.maximum(x, 0.0) + jnp.log1p(jnp.exp(-jnp.abs(x)))


def _rms(x, g):
    r = lax.rsqrt(jnp.mean(x * x, axis=-1, keepdims=True) + NORM_EPS)
    return x * r * g


def _adaln(x, g, scale, shift):
    return _rms(x, g) * (1.0 + scale) + shift


def _resid(x, y, gate, g):
    return x + gate * _rms(y, g)


def _mid(x, y, gate, g_post, g_pre, scale, shift):
    x_new = _resid(x, y, gate, g_post)
    return x_new, _adaln(x_new, g_pre, scale, shift)


def _swiglu(gt, up):
    return _silu(gt) * up


def _ssm_post(y_ssd, pre_xs, z, d_full, norm_w):
    y = (y_ssd + _silu(pre_xs) * d_full) * _silu(z)
    half = SSM_INNER // 2
    parts = []
    for g in range(2):
        yg = y[:, g * half:(g + 1) * half]
        parts.append(yg * lax.rsqrt(jnp.mean(yg * yg, axis=-1, keepdims=True) + NORM_EPS))
    return jnp.concatenate(parts, axis=1) * norm_w


def _first_fwd(x, vecs, side=None):
    return _rowwise("adaln_first", lambda x, g, sc, sh: _adaln(x, g, sc, sh),
                    [(x, D_MODEL, 0)], vecs, [(D_MODEL, BF16)], side=side)


def _mid_bwd(name, x, y, dx_new, dh, vecs):
    def fn(x, y, dxn, dh, gate, g_post, g_pre, scale, shift):
        y, dh = _f(y), _f(dh)
        r_y = lax.rsqrt(jnp.mean(y * y, axis=-1, keepdims=True) + NORM_EPS)
        n_y = y * r_y
        both = gate * g_post
        x_new = x + n_y * both
        r_x = lax.rsqrt(jnp.mean(x_new * x_new, axis=-1, keepdims=True) + NORM_EPS)
        u = x_new * r_x
        col_p = jnp.sum(dh * u, axis=0, keepdims=True)
        du = dh * (g_pre * (1.0 + scale))
        dxt = dxn + r_x * (du - u * jnp.mean(du * u, axis=-1, keepdims=True))
        col_q = jnp.sum(dxt * n_y, axis=0, keepdims=True)
        dn = dxt * both
        dy = r_y * (dn - n_y * jnp.mean(dn * n_y, axis=-1, keepdims=True))
        return (dxt, dy, col_q * g_post, col_q * gate, col_p * (1.0 + scale), col_p * g_pre,
                jnp.sum(dh, axis=0, keepdims=True))

    L = x.shape[0]
    tl = min(512, L)
    n = L // tl
    streams = (x, y, dx_new, dh)

    def body(*refs):
        hbm, vec_refs = refs[:4], refs[4:9]
        dx_o, dy_o, accs = refs[9], refs[10], refs[11:16]
        bufs, sems = refs[16:20], refs[20:24]
        s = pl.program_id(0)

        def copy(k, step):
            rows = pl.ds(pl.multiple_of(step * tl, tl), tl)
            slot = lax.rem(step, RING_SLOTS)
            return pltpu.make_async_copy(hbm[k].at[rows, :], bufs[k].at[slot], sems[k].at[slot])

        @pl.when(s == 0)
        def _():
            for a in accs:
                a[...] = jnp.zeros_like(a)
            for ahead in range(min(RING_SLOTS - 1, n)):
                for k in range(4):
                    copy(k, ahead).start()

        @pl.when(s + RING_SLOTS - 1 < n)
        def _():
            for k in range(4):
                copy(k, s + RING_SLOTS - 1).start()

        for k in range(4):
            copy(k, s).wait()
        slot = lax.rem(s, RING_SLOTS)
        vals = fn(*[b[slot] for b in bufs], *[v[...] for v in vec_refs])
        dx_o[...] = vals[0]
        dy_o[...] = vals[1].astype(dy_o.dtype)
        for a, v in zip(accs, vals[2:]):
            a[...] += v

    any_spec = pl.BlockSpec(memory_space=pl.ANY)
    vec_spec = pl.BlockSpec((1, D_MODEL), lambda i: (0, 0))
    row_spec = pl.BlockSpec((tl, D_MODEL), lambda i: (i, 0))
    return pl.pallas_call(
        body, name=name, grid=(n,), in_specs=[any_spec] * 4 + [vec_spec] * 5, out_specs=[row_spec, row_spec] + [vec_spec] * 5,
        out_shape=[jax.ShapeDtypeStruct((L, D_MODEL), F32), jax.ShapeDtypeStruct((L, D_MODEL), BF16)]
        + [jax.ShapeDtypeStruct((1, D_MODEL), F32)] * 5,
        scratch_shapes=[pltpu.VMEM((RING_SLOTS, tl, D_MODEL), a.dtype) for a in streams]
        + [pltpu.SemaphoreType.DMA((RING_SLOTS,))] * 4,
        compiler_params=_params(("arbitrary",)),
    )(*streams, *vecs)


RING_SLOTS = 3


def _first_bwd(x, dx_in, dh, vecs, side=None):
    def fn(x, dxi, dh, g, scale, shift):
        dh = _f(dh)
        r = lax.rsqrt(jnp.mean(x * x, axis=-1, keepdims=True) + NORM_EPS)
        u = x * r
        col_p = jnp.sum(dh * u, axis=0, keepdims=True)
        du = dh * (g * (1.0 + scale))
        dx = dxi + r * (du - u * jnp.mean(du * u, axis=-1, keepdims=True))
        return dx, col_p * (1.0 + scale), col_p * g, jnp.sum(dh, axis=0, keepdims=True)

    vec = (1, D_MODEL)
    return _rowwise("adaln_first_bwd", fn, [(x, D_MODEL, 0), (dx_in, D_MODEL, 0), (dh, D_MODEL, 0)], vecs,
                    [(D_MODEL, F32)], [vec] * 3, side=side)


def _last_bwd(x1, f, target, vecs):
    def fn(x1, f, t, gate, g):
        f = _f(f)
        r = lax.rsqrt(jnp.mean(f * f, axis=-1, keepdims=True) + NORM_EPS)
        n = f * r
        both = gate * g
        err = x1 + n * both - t
        d = err * (1.0 / D_MODEL)
        col_q = jnp.sum(d * n, axis=0, keepdims=True)
        dn = d * both
        df = r * (dn - n * jnp.mean(dn * n, axis=-1, keepdims=True))
        loss_cols = jnp.sum(err * err, axis=0, keepdims=True) * (0.5 / D_MODEL)
        return d, df, col_q * g, col_q * gate, loss_cols

    vec = (1, D_MODEL)
    return _rowwise("loss_last_bwd", fn, [(x1, D_MODEL, 0), (f, D_MODEL, 0), (target, D_MODEL, 0)], vecs,
                    [(D_MODEL, F32), (D_MODEL, BF16)], [vec] * 3)


HALO = 16


def _shift_down(u, prev, k):
    rows = lax.broadcasted_iota(jnp.int32, u.shape, 0)
    v = pltpu.roll(u, k, 0)
    for t in range(k):
        v = jnp.where(rows == t, prev[HALO - k + t:HALO - k + t + 1, :], v)
    return v


def _shift_up(u, nxt, k):
    n = u.shape[0]
    rows = lax.broadcasted_iota(jnp.int32, u.shape, 0)
    v = pltpu.roll(u, n - k, 0)
    for t in range(k):
        v = jnp.where(rows == n - k + t, nxt[t:t + 1, :], v)
    return v


def _conv_specs(L, tl, width, col_block):
    per = tl // HALO
    last = L // HALO - 1
    main = pl.BlockSpec((tl, width), lambda i: (i, col_block))
    before = pl.BlockSpec((HALO, width), lambda i: (jnp.maximum(i * per - 1, 0), col_block))
    after = pl.BlockSpec((HALO, width), lambda i: (jnp.minimum((i + 1) * per, last), col_block))
    return main, before, after


def _shortconv_fwd(P, w, tl=512):
    L = P.shape[0]
    tl = min(tl, L)
    C = SC_WIDTH
    main, before, _ = _conv_specs(L, tl, 3 * C, 0)

    def body(p_ref, h_ref, w_ref, o_ref):
        first = (pl.program_id(0) == 0)
        p, h = _f(p_ref[...]), _f(h_ref[...])
        b, u = p[:, :C], p[:, C:2 * C] * p[:, 2 * C:]
        uh = jnp.where(first, 0.0, h[:, C:2 * C] * h[:, 2 * C:])
        wv = w_ref[...]
        cv = wv[2:3] * u + wv[1:2] * _shift_down(u, uh, 1) + wv[0:1] * _shift_down(u, uh, 2)
        o_ref[...] = (b * cv).astype(o_ref.dtype)

    return pl.pallas_call(
        body, name="shortconv_fwd", grid=(L // tl,),
        in_specs=[main, before, pl.BlockSpec(w.shape, lambda i: (0, 0))],
        out_specs=pl.BlockSpec((tl, C), lambda i: (i, 0)),
        out_shape=jax.ShapeDtypeStruct((L, C), BF16), compiler_params=_params(("arbitrary",)),
    )(P, P, w)


def _shortconv_bwd(P, dya, w, tl=512):
    L = P.shape[0]
    tl = min(tl, L)
    C = SC_WIDTH
    main, before, after = _conv_specs(L, tl, 3 * C, 0)
    dmain, _, dafter = _conv_specs(L, tl, C, 0)
    n = L // tl

    def body(p_ref, h_ref, n_ref, d_ref, dn_ref, w_ref, o_ref, dw0, dw1, dw2):
        i = pl.program_id(0)
        p, h, nx = _f(p_ref[...]), _f(h_ref[...]), _f(n_ref[...])
        b, c, x = p[:, :C], p[:, C:2 * C], p[:, 2 * C:]
        u = c * x
        uh = jnp.where(i == 0, 0.0, h[:, C:2 * C] * h[:, 2 * C:])
        u1, u2 = _shift_down(u, uh, 1), _shift_down(u, uh, 2)
        wv = w_ref[...]
        cv = wv[2:3] * u + wv[1:2] * u1 + wv[0:1] * u2
        dy = _f(d_ref[...])
        dcv = dy * b
        dcv_n = jnp.where(i == n - 1, 0.0, _f(dn_ref[...]) * nx[:, :C])
        du = wv[2:3] * dcv + wv[1:2] * _shift_up(dcv, dcv_n, 1) + wv[0:1] * _shift_up(dcv, dcv_n, 2)
        o_ref[:, :C] = (dy * cv).astype(o_ref.dtype)
        o_ref[:, C:2 * C] = (du * x).astype(o_ref.dtype)
        o_ref[:, 2 * C:] = (du * c).astype(o_ref.dtype)

        @pl.when(i == 0)
        def _():
            for r in (dw0, dw1, dw2):
                r[...] = jnp.zeros_like(r)

        dw0[...] += jnp.sum(dcv * u2, axis=0, keepdims=True)
        dw1[...] += jnp.sum(dcv * u1, axis=0, keepdims=True)
        dw2[...] += jnp.sum(dcv * u, axis=0, keepdims=True)

    vec = pl.BlockSpec((1, C), lambda i: (0, 0))
    return pl.pallas_call(
        body, name="shortconv_bwd", grid=(n,),
        in_specs=[main, before, after, dmain, dafter, pl.BlockSpec(w.shape, lambda i: (0, 0))],
        out_specs=[pl.BlockSpec((tl, 3 * C), lambda i: (i, 0)), vec, vec, vec],
        out_shape=[jax.ShapeDtypeStruct((L, 3 * C), BF16)] + [jax.ShapeDtypeStruct((1, C), F32)] * 3,
        compiler_params=_params(("arbitrary",)),
    )(P, P, P, dya, dya, w)


def _ssmconv_fwd(P, w, bias, tl=512):
    L = P.shape[0]
    tl = min(tl, L)
    C = SSM_CONV_DIM
    main, before, _ = _conv_specs(L, tl, C, P_XBC // C)

    def body(p_ref, h_ref, w_ref, b_ref, o_ref):
        u = _f(p_ref[...])
        uh = jnp.where(pl.program_id(0) == 0, 0.0, _f(h_ref[...]))
        wv = w_ref[...]
        acc = wv[3:4] * u + b_ref[...]
        for k in range(1, SSM_CONV):
            acc = acc + wv[3 - k:4 - k] * _shift_down(u, uh, k)
        o_ref[...] = acc

    return pl.pallas_call(
        body, name="ssmconv_fwd", grid=(L // tl,),
        in_specs=[main, before, pl.BlockSpec(w.shape, lambda i: (0, 0)), pl.BlockSpec(bias.shape, lambda i: (0, 0))],
        out_specs=pl.BlockSpec((tl, C), lambda i: (i, 0)),
        out_shape=jax.ShapeDtypeStruct((L, C), F32), compiler_params=_params(("arbitrary",)),
    )(P, P, w, bias)


def _ssmconv_bwd(P, dpre, w, tl=512):
    L = P.shape[0]
    tl = min(tl, L)
    C = SSM_CONV_DIM
    main, before, _ = _conv_specs(L, tl, C, P_XBC // C)
    dmain, _, dafter = _conv_specs(L, tl, C, 0)
    n = L // tl

    def body(p_ref, h_ref, d_ref, dn_ref, w_ref, o_ref, dw0, dw1, dw2, dw3, db):
        i = pl.program_id(0)
        u = _f(p_ref[...])
        uh = jnp.where(i == 0, 0.0, _f(h_ref[...]))
        d = d_ref[...]
        dn = jnp.where(i == n - 1, 0.0, dn_ref[...])
        wv = w_ref[...]
        du = wv[3:4] * d
        for k in range(1, SSM_CONV):
            du = du + wv[3 - k:4 - k] * _shift_up(d, dn, k)
        o_ref[...] = du.astype(o_ref.dtype)

        @pl.when(i == 0)
        def _():
            for r in (dw0, dw1, dw2, dw3, db):
                r[...] = jnp.zeros_like(r)

        for k, r in ((3, dw0), (2, dw1), (1, dw2)):
            r[...] += jnp.sum(d * _shift_down(u, uh, k), axis=0, keepdims=True)
        dw3[...] += jnp.sum(d * u, axis=0, keepdims=True)
        db[...] += jnp.sum(d, axis=0, keepdims=True)

    vec = pl.BlockSpec((1, C), lambda i: (0, 0))
    return pl.pallas_call(
        body, name="ssmconv_bwd", grid=(n,),
        in_specs=[main, before, dmain, dafter, pl.BlockSpec(w.shape, lambda i: (0, 0))],
        out_specs=[pl.BlockSpec((tl, C), lambda i: (i, 0))] + [vec] * 5,
        out_shape=[jax.ShapeDtypeStruct((L, C), BF16)] + [jax.ShapeDtypeStruct((1, C), F32)] * 5,
        compiler_params=_params(("arbitrary",)),
    )(P, P, dpre, dpre, w)


def _dot_nt(a, b):
    return lax.dot_general(a, b, (((1,), (1,)), ((), ())), preferred_element_type=F32)


def _dot_tn(a, b):
    return lax.dot_general(a, b, (((0,), (0,)), ((), ())), preferred_element_type=F32)


def _split3(x):
    hi = x.astype(BF16)
    r = x - hi.astype(F32)
    mid = r.astype(BF16)
    return hi, mid, (r - mid.astype(F32)).astype(BF16)


@jax.custom_vjp
def _xm01(x, m):
    return sum(jnp.dot(t, m, preferred_element_type=F32) for t in _split3(x))


def _xm01_fwd(x, m):
    return _xm01(x, m), m


def _xm01_bwd(m, g):
    return sum(_dot_nt(t, m) for t in _split3(g)), jnp.zeros_like(m)


_xm01.defvjp(_xm01_fwd, _xm01_bwd)


@jax.custom_vjp
def _m01x(m, x):
    return sum(jnp.dot(m, t, preferred_element_type=F32) for t in _split3(x))


def _m01x_fwd(m, x):
    return _m01x(m, x), m


def _m01x_bwd(m, g):
    return jnp.zeros_like(m), sum(_dot_tn(m, t) for t in _split3(g))


_m01x.defvjp(_m01x_fwd, _m01x_bwd)


def _ssd_chunk(pre, dtr, s_prev, dtb, alog):
    T = pre.shape[0]
    act = _silu(pre)
    xs, bm, cm = act[:, :SSM_INNER], act[:, SSM_INNER:SSM_INNER + 128], act[:, SSM_INNER + 128:]
    lane = lax.broadcasted_iota(jnp.int32, (1, LANES), 1)
    dt = jnp.where(lane < SSM_HEADS, _softplus(dtr + dtb), 0.0)
    a = dt * (-jnp.exp(alog))
    ri = lax.broadcasted_iota(jnp.int32, (T, T), 0)
    ci = lax.broadcasted_iota(jnp.int32, (T, T), 1)
    causal = ci <= ri
    a_cs = _m01x(causal.astype(BF16), a)
    eh = lax.broadcasted_iota(jnp.int32, (LANES, SSM_INNER), 0)
    ej = lax.broadcasted_iota(jnp.int32, (LANES, SSM_INNER), 1)
    expand = (lax.shift_right_logical(ej, 6) == eh).astype(BF16)
    dt_full = _xm01(dt, expand)
    acs_full = _xm01(a_cs, expand)
    alast_full = acs_full[T - 1:T, :]
    xdt = xs * dt_full
    a_cs_t = a_cs.T
    ys, s_new = [], []
    for g in range(2):
        in_group = lax.shift_right_logical(lane, 6) == g
        cg = jnp.where(in_group, cm, 0.0).astype(BF16)
        bg = jnp.where(in_group, bm, 0.0).astype(BF16)
        scores = _dot_nt(cg, bg)
        for pp in range(2):
            hp = 2 * g + pp
            cols = slice(hp * LANES, (hp + 1) * LANES)
            xp, acsp = xdt[:, cols], acs_full[:, cols]
            per_head = []
            for hh in range(2):
                h = 2 * hp + hh
                decay = jnp.exp(jnp.where(causal, a_cs[:, h:h + 1] - a_cs_t[h:h + 1, :], -jnp.inf))
                per_head.append(jnp.dot((scores * decay).astype(BF16), xp.astype(BF16), preferred_element_type=F32))
            y_diag = jnp.where(lane < SSM_STATE, per_head[0], per_head[1])
            sp = s_prev[hp * LANES:(hp + 1) * LANES, :]
            y_off = jnp.dot(cg, sp.astype(BF16), preferred_element_type=F32) * jnp.exp(acsp)
            ys.append(y_diag + y_off)
            to_end = jnp.exp(alast_full[:, cols] - acsp)
            s_new.append(sp * jnp.exp(alast_full[:, cols]) + _dot_tn(bg, (xp * to_end).astype(BF16)))
    return jnp.concatenate(ys, axis=1), jnp.concatenate(s_new, axis=0)


def _ssd_fwd(pre, P, dtb, alog):
    L = pre.shape[0]
    T = min(SSM_CHUNK, L)
    nc = L // T

    def body(pre_ref, dt_ref, dtb_ref, al_ref, y_ref, st_ref, s_scr):
        @pl.when(pl.program_id(0) == 0)
        def _():
            s_scr[...] = jnp.zeros_like(s_scr)

        st_ref[0] = s_scr[...]
        y, s = _ssd_chunk(pre_ref[...], _f(dt_ref[...]), s_scr[...], dtb_ref[...], al_ref[...])
        y_ref[...] = y
        s_scr[...] = s

    vec = pl.BlockSpec((1, LANES), lambda i: (0, 0))
    return pl.pallas_call(
        body, name="ssd_fwd", grid=(nc,),
        in_specs=[pl.BlockSpec((T, SSM_CONV_DIM), lambda i: (i, 0)), pl.BlockSpec((T, LANES), lambda i: (i, P_DT // LANES)),
                  vec, vec],
        out_specs=[pl.BlockSpec((T, SSM_INNER), lambda i: (i, 0)), pl.BlockSpec((1, 512, LANES), lambda i: (i, 0, 0))],
        out_shape=[jax.ShapeDtypeStruct((L, SSM_INNER), F32), jax.ShapeDtypeStruct((nc, 512, LANES), F32)],
        scratch_shapes=[pltpu.VMEM((512, LANES), F32)], compiler_params=_params(("arbitrary",)),
    )(pre, P, dtb, alog)


def _ssd_bwd(pre, P, states, dy, dxs_extra, dtb, alog):
    L = pre.shape[0]
    T = min(SSM_CHUNK, L)
    nc = L // T

    def body(pre_ref, dt_ref, st_ref, dy_ref, dx_ref, dtb_ref, al_ref, dpre_ref, ddt_ref, ddtb_ref, dal_ref, ds_scr):
        @pl.when(pl.program_id(0) == 0)
        def _():
            ds_scr[...] = jnp.zeros_like(ds_scr)
            ddtb_ref[...] = jnp.zeros_like(ddtb_ref)
            dal_ref[...] = jnp.zeros_like(dal_ref)

        _, vjp = jax.vjp(_ssd_chunk, pre_ref[...], _f(dt_ref[...]), st_ref[0], dtb_ref[...], al_ref[...])
        dpre, ddt, ds, ddtb, dal = vjp((dy_ref[...], ds_scr[...]))
        dpre_ref[:, :SSM_INNER] = dpre[:, :SSM_INNER] + dx_ref[...]
        dpre_ref[:, SSM_INNER:] = dpre[:, SSM_INNER:]
        ddt_ref[:, :LANES] = ddt.astype(ddt_ref.dtype)
        ddt_ref[:, LANES:] = jnp.zeros((T, DT_PAD - LANES), ddt_ref.dtype)
        ds_scr[...] = ds
        ddtb_ref[...] += ddtb
        dal_ref[...] += dal

    vec = pl.BlockSpec((1, LANES), lambda i: (0, 0))
    rev = lambda i: (nc - 1 - i, 0)
    return pl.pallas_call(
        body, name="ssd_bwd", grid=(nc,),
        in_specs=[pl.BlockSpec((T, SSM_CONV_DIM), rev), pl.BlockSpec((T, LANES), lambda i: (nc - 1 - i, P_DT // LANES)),
                  pl.BlockSpec((1, 512, LANES), lambda i: (nc - 1 - i, 0, 0)),
                  pl.BlockSpec((T, SSM_INNER), rev), pl.BlockSpec((T, SSM_INNER), rev), vec, vec],
        out_specs=[pl.BlockSpec((T, SSM_CONV_DIM), rev), pl.BlockSpec((T, DT_PAD), rev), vec, vec],
        out_shape=[jax.ShapeDtypeStruct((L, SSM_CONV_DIM), F32), jax.ShapeDtypeStruct((L, DT_PAD), BF16),
                   jax.ShapeDtypeStruct((1, LANES), F32), jax.ShapeDtypeStruct((1, LANES), F32)],
        scratch_shapes=[pltpu.VMEM((512, LANES), F32)], compiler_params=_params(("arbitrary",)),
    )(pre, P, states, dy, dxs_extra, dtb, alog)


def _sb_scores(qm, kb, later, strict, mask):
    z = _dot_nt(qm, kb)
    lk = jnp.minimum(-z, 0.0) - jnp.log(1.0 + jnp.exp(-jnp.abs(z)))
    if mask is not None:
        lk = jnp.where(mask, lk, 0.0)
    log_a = z + lk + jnp.dot(lk.astype(BF16), strict, preferred_element_type=F32) + later
    if mask is not None:
        log_a = jnp.where(mask, log_a, -jnp.inf)
    return z, lk, log_a


def _dot_split(x, m):
    hi = x.astype(BF16)
    lo = (x - hi.astype(F32)).astype(BF16)
    return jnp.dot(hi, m, preferred_element_type=F32) + jnp.dot(lo, m, preferred_element_type=F32)


def _sb_setup(q_ref, i, tq, tk):
    lane = lax.broadcasted_iota(jnp.int32, (1, LANES), 1)
    first = lane < SB_HEAD_DIM
    q = q_ref[...] * (SB_HEAD_DIM ** -0.5)
    qms = (jnp.where(first, q, jnp.zeros_like(q)), jnp.where(first, jnp.zeros_like(q), q))
    j0 = lax.div(i * tq, tk)
    ri = lax.broadcasted_iota(jnp.int32, (tq, tk), 0)
    ci = lax.broadcasted_iota(jnp.int32, (tq, tk), 1)
    diag_mask = (ci + (j0 * tk - i * tq)) < ri
    kr = lax.broadcasted_iota(jnp.int32, (tk, tk), 0)
    kc = lax.broadcasted_iota(jnp.int32, (tk, tk), 1)
    strict = (kr > kc).astype(BF16)
    return first, qms, j0, diag_mask, strict


def _sb_continue(c):
    return jnp.logical_and(c[0] >= 0, jnp.maximum(jnp.max(c[1][0]), jnp.max(c[1][1])) > SB_LOG_CUTOFF)


def _sb_fwd(P):
    L = P.shape[0]
    tq, tk = min(SB_TQ, L), min(SB_TK, L)
    nq = L // tq
    qb = P_B // LANES

    subs = SB_SUBS if L % (SB_SUBS * tq) == 0 else 1

    def body(q_ref, k_ref, v_ref, o_ref, of_ref):
        zero, zacc = jnp.zeros((tq, 1), F32), jnp.zeros((tq, LANES), F32)
        walks = []
        for s in range(subs):
            rows = pl.ds(s * tq, tq)
            first, qms, j0, diag_mask, strict = _sb_setup(q_ref.at[rows, :], pl.program_id(1) * subs + s, tq, tk)

            def tile(h, j, later, acc, mask=None, valid=None, qms=qms, strict=strict):
                off = pl.multiple_of(j * tk, tk)
                gate = later if valid is None else jnp.where(valid, later, -jnp.inf)
                _, lk, log_a = _sb_scores(qms[h], k_ref[pl.ds(off, tk), :], gate, strict, mask)
                acc = acc + jnp.dot(jnp.exp(log_a).astype(BF16), v_ref[pl.ds(off, tk), :], preferred_element_type=F32)
                total = jnp.sum(lk, axis=1, keepdims=True)
                return later + (total if valid is None else jnp.where(valid, total, 0.0)), acc

            state = []
            for h in range(2):
                carry = tile(h, j0, zero, zacc, mask=diag_mask)
                for n in range(1, SB_STRAIGHT):
                    carry = tile(h, jnp.maximum(j0 - n, 0), *carry, valid=j0 >= n)
                state.append(carry)
            walks.append((rows, first, j0, tile, state))

        for rows, first, j0, tile, state in walks:
            def tail(c, tile=tile):
                res = [tile(h, c[0], c[1][h], c[2][h]) for h in range(2)]
                return c[0] - 1, (res[0][0], res[1][0]), (res[0][1], res[1][1])

            _, _, accs = lax.while_loop(
                _sb_continue, tail, (j0 - SB_STRAIGHT, (state[0][0], state[1][0]), (state[0][1], state[1][1])))
            out = jnp.where(first, accs[0], accs[1])
            o_ref[rows, :] = out.astype(o_ref.dtype)
            of_ref[rows, :] = out

    nq = nq // subs
    tile_spec = pl.BlockSpec((subs * tq, LANES), lambda p, i: (i, p))
    return pl.pallas_call(
        body, name="sb_fwd", grid=(2, nq),
        in_specs=[pl.BlockSpec((subs * tq, LANES), lambda p, i: (i, qb + p)),
                  pl.BlockSpec((L, LANES), lambda p, i: (0, qb + 2 + p)),
                  pl.BlockSpec((L, LANES), lambda p, i: (0, qb + 4 + p))],
        out_specs=[tile_spec, tile_spec],
        out_shape=[jax.ShapeDtypeStruct((L, 2 * LANES), BF16), jax.ShapeDtypeStruct((L, 2 * LANES), F32)],
        compiler_params=_params(("arbitrary", "arbitrary")),
    )(P, P, P)


def _sb_bwd(P, dyb, yb32):
    L = P.shape[0]
    tq, tk = min(SB_TQ, L), min(SB_TK, L)
    nq = L // tq
    qb = P_B // LANES

    subs = SB_SUBS if L % (SB_SUBS * tq) == 0 else 1

    def body(q_ref, k_ref, v_ref, do_ref, of_ref, dq_ref, dk_ref, dv_ref):
        @pl.when(pl.program_id(1) == 0)
        def _():
            dk_ref[...] = jnp.zeros_like(dk_ref)
            dv_ref[...] = jnp.zeros_like(dv_ref)

        tails = [walk(pl.ds(s * tq, tq), pl.program_id(1) * subs + s, q_ref, k_ref, v_ref, do_ref, of_ref, dq_ref,
                      dk_ref, dv_ref) for s in range(subs)]
        for finish in tails:
            finish()

    def walk(rows, i, q_ref, k_ref, v_ref, do_ref, of_ref, dq_ref, dk_ref, dv_ref):
        first, qms, j0, diag_mask, strict = _sb_setup(q_ref.at[rows, :], i, tq, tk)
        do = do_ref[rows, :]
        doms = (jnp.where(first, do, jnp.zeros_like(do)), jnp.where(first, jnp.zeros_like(do), do))
        prod = _f(do) * of_ref[rows, :]
        totals = (jnp.sum(jnp.where(first, prod, 0.0), axis=1, keepdims=True),
                  jnp.sum(jnp.where(first, 0.0, prod), axis=1, keepdims=True))

        def tile(h, j, later, later_g, acc, mask=None, valid=None):
            off = pl.multiple_of(j * tk, tk)
            kb, vb = k_ref[pl.ds(off, tk), :], v_ref[pl.ds(off, tk), :]
            gate = later if valid is None else jnp.where(valid, later, -jnp.inf)
            z, lk, log_a = _sb_scores(qms[h], kb, gate, strict, mask)
            att = jnp.exp(log_a).astype(BF16)
            g = _f(att) * _dot_nt(doms[h], vb)
            before = totals[h] - later_g
            if valid is not None:
                before = jnp.where(valid, before, 0.0)
            dz = g - (before - _dot_split(g, strict)) * jnp.exp(z + lk)
            if mask is not None:
                dz = jnp.where(mask, dz, 0.0)
            dzb = dz.astype(BF16)
            rows = jnp.sum(lk, axis=1, keepdims=True)
            carry = (later + (rows if valid is None else jnp.where(valid, rows, 0.0)),
                     later_g + jnp.sum(g, axis=1, keepdims=True), acc + jnp.dot(dzb, kb, preferred_element_type=F32))
            return carry, _dot_tn(dzb, qms[h]), _dot_tn(att, doms[h])

        def tail(c):
            off = pl.multiple_of(c[0] * tk, tk)
            (c0, dk0, dv0), (c1, dk1, dv1) = [tile(h, c[0], c[1][h], c[2][h], c[3][h]) for h in range(2)]
            dk_ref[pl.ds(off, tk), :] += dk0 + dk1
            dv_ref[pl.ds(off, tk), :] += dv0 + dv1
            return (c[0] - 1,) + tuple(zip(c0, c1))

        zero, zacc = jnp.zeros((tq, 1), F32), jnp.zeros((tq, LANES), F32)
        blocks = [j0] + [jnp.maximum(j0 - n, 0) for n in range(1, SB_STRAIGHT)]
        carries, dks, dvs = [], [], []
        for h in range(2):
            carry, dk, dv = tile(h, j0, zero, zero, zacc, mask=diag_mask)
            dks.append([dk])
            dvs.append([dv])
            for n in range(1, SB_STRAIGHT):
                carry, dk, dv = tile(h, blocks[n], *carry, valid=j0 >= n)
                dks[h].append(dk)
                dvs[h].append(dv)
            carries.append(carry)
        for n, j in enumerate(blocks):
            off = pl.multiple_of(j * tk, tk)
            dk_ref[pl.ds(off, tk), :] += dks[0][n] + dks[1][n]
            dv_ref[pl.ds(off, tk), :] += dvs[0][n] + dvs[1][n]
        def finish():
            accs = lax.while_loop(_sb_continue, tail, (j0 - SB_STRAIGHT,) + tuple(zip(carries[0], carries[1])))[3]
            dq_ref[rows, :] = jnp.where(first, accs[0], accs[1]) * (SB_HEAD_DIM ** -0.5)

        return finish

    nq = nq // subs
    full = pl.BlockSpec((L, LANES), lambda p, i: (0, p))
    tile_spec = pl.BlockSpec((subs * tq, LANES), lambda p, i: (i, p))
    return pl.pallas_call(
        body, name="sb_bwd", grid=(2, nq),
        in_specs=[pl.BlockSpec((subs * tq, LANES), lambda p, i: (i, qb + p)),
                  pl.BlockSpec((L, LANES), lambda p, i: (0, qb + 2 + p), pipeline_mode=pl.Buffered(1)),
                  pl.BlockSpec((L, LANES), lambda p, i: (0, qb + 4 + p), pipeline_mode=pl.Buffered(1)), tile_spec, tile_spec],
        out_specs=[tile_spec, full, full],
        out_shape=[jax.ShapeDtypeStruct((L, 2 * LANES), F32)] * 3,
        compiler_params=_params(("arbitrary", "arbitrary")),
    )(P, P, P, dyb, yb32)


MOD_SHARD = N_MOD * D_MODEL // N_CHIPS


def _mod_fwd(c_all, mod_w, mod_b_sh):
    tn = 512

    def body(c_ref, w_ref, b_ref, o_ref):
        o_ref[0] = jnp.dot(_silu(c_ref[...]), w_ref[0], precision=HIGHEST, preferred_element_type=F32) + b_ref[0]

    return pl.pallas_call(
        body, name="mod_fwd", grid=(DEPTH, MOD_SHARD // tn),
        in_specs=[pl.BlockSpec((N_DEV, D_MODEL), lambda l, j: (0, 0)),
                  pl.BlockSpec((1, D_MODEL, tn), lambda l, j: (l, 0, j)),
                  pl.BlockSpec((1, 1, tn), lambda l, j: (l, 0, j))],
        out_specs=pl.BlockSpec((1, N_DEV, tn), lambda l, j: (l, 0, j)),
        out_shape=jax.ShapeDtypeStruct((DEPTH, N_DEV, MOD_SHARD), F32),
        compiler_params=_params(("arbitrary", "arbitrary")),
    )(c_all, mod_w, mod_b_sh)


def _mod_bwd(c_all, dmod_sh):
    tn = 512

    def body(c_ref, d_ref, o_ref):
        o_ref[0] = lax.dot_general(_silu(c_ref[...]), d_ref[0], (((0,), (0,)), ((), ())), precision=HIGHEST,
                                   preferred_element_type=F32)

    return pl.pallas_call(
        body, name="mod_bwd", grid=(DEPTH, MOD_SHARD // tn),
        in_specs=[pl.BlockSpec((N_DEV, D_MODEL), lambda l, j: (0, 0)),
                  pl.BlockSpec((1, N_DEV, tn), lambda l, j: (l, 0, j))],
        out_specs=pl.BlockSpec((1, D_MODEL, tn), lambda l, j: (l, 0, j)),
        out_shape=jax.ShapeDtypeStruct((DEPTH, D_MODEL, MOD_SHARD), F32),
        compiler_params=_params(("arbitrary", "arbitrary")),
    )(c_all, dmod_sh)


def _row_tile(rows, cap):
    if rows <= cap:
        return rows
    best = None
    for t in range(8, cap + 1, 8):
        if rows % t == 0:
            best = t
    assert best is not None, (rows, cap)
    return best


def _adamw(name, w, gs, m, v, tr=256):
    R, W = w.shape
    by_layer = any(isinstance(t, tuple) for t in gs)
    tr = _row_tile(R // 2 if by_layer else R, tr)
    per = (R // 2) // tr

    flat, specs = [], []
    for t in gs:
        if isinstance(t, tuple):
            flat += list(t)
            specs += [pl.BlockSpec((tr, W), lambda i: (jnp.minimum(i, per - 1), 0)),
                      pl.BlockSpec((tr, W), lambda i: (jnp.maximum(i - per, 0), 0))]
        else:
            flat.append(t)
            specs.append(pl.BlockSpec((tr, W), lambda i: (i, 0)))
    ng = len(flat)

    def body(*refs):
        w_ref, g_refs, (m_ref, v_ref) = refs[0], list(refs[1:1 + ng]), refs[1 + ng:3 + ng]
        g_out, d_out, m_out, v_out = refs[3 + ng:]
        g = None
        for t in gs:
            if isinstance(t, tuple):
                lo, hi = g_refs.pop(0), g_refs.pop(0)
                term = jnp.where(pl.program_id(0) < per, lo[...], hi[...])
            else:
                term = g_refs.pop(0)[...]
            g = term if g is None else g + term
        mm = ADAM_B1 * m_ref[...] + (1.0 - ADAM_B1) * g
        vv = ADAM_B2 * v_ref[...] + (1.0 - ADAM_B2) * (g * g)
        m_hat = mm / (1.0 - ADAM_B1 ** ADAM_STEP)
        v_hat = vv / (1.0 - ADAM_B2 ** ADAM_STEP)
        g_out[...] = g
        d_out[...] = -ADAM_LR * (m_hat / (jnp.sqrt(v_hat) + ADAM_EPS) + ADAM_WD * w_ref[...])
        m_out[...] = mm
        v_out[...] = vv

    spec = pl.BlockSpec((tr, W), lambda i: (i, 0))
    return pl.pallas_call(
        body, name=name, grid=(R // tr,), in_specs=[spec] + specs + [spec, spec], out_specs=[spec] * 4,
        out_shape=[jax.ShapeDtypeStruct((R, W), F32)] * 4, compiler_params=_params(("arbitrary",)),
    )(w, *flat, m, v)


def _sum_slots(name, a, tr=256):
    n, R, W = a.shape
    tr = _row_tile(R, tr)

    def body(a_ref, o_ref):
        acc = _f(a_ref[0])
        for j in range(1, n):
            acc = acc + _f(a_ref[j])
        o_ref[...] = acc

    return pl.pallas_call(
        body, name=name, grid=(R // tr,), in_specs=[pl.BlockSpec((n, tr, W), lambda i: (0, i, 0))],
        out_specs=pl.BlockSpec((tr, W), lambda i: (i, 0)), out_shape=jax.ShapeDtypeStruct((R, W), F32),
        compiler_params=_params(("arbitrary",)),
    )(a)


def _here():
    return lax.axis_index("x"), lax.axis_index("y"), lax.axis_index("c")


def _flip(v, d):
    return 1 - v if d else v


def _allgather_small(name, buf):
    R = buf.shape[0]
    rel = [(dx, dy, dc) for dx in (0, 1) for dy in (0, 1) for dc in (0, 1)][1:]

    def body(x_ref, o_ref, send, recv, lsem):
        x, y, c = _here()
        me = 4 * x + 2 * y + c
        mine = pltpu.make_async_copy(x_ref, o_ref.at[me], lsem)
        mine.start()

        def copy(k, slot):
            dx, dy, dc = rel[k]
            return pltpu.make_async_remote_copy(
                src_ref=x_ref, dst_ref=o_ref.at[slot], send_sem=send.at[k], recv_sem=recv.at[k],
                device_id=(_flip(x, dx), _flip(y, dy), _flip(c, dc)), device_id_type=MESH_ID)

        sent = [copy(k, me) for k in range(len(rel))]
        for cp in sent:
            cp.start()
        for k, (dx, dy, dc) in enumerate(rel):
            copy(k, 4 * _flip(x, dx) + 2 * _flip(y, dy) + _flip(c, dc)).wait_recv()
        for cp in sent:
            cp.wait_send()
        mine.wait()

    return pl.pallas_call(
        body, name=name, out_shape=jax.ShapeDtypeStruct((N_DEV, R, LANES), F32),
        in_specs=[pl.BlockSpec(memory_space=pltpu.VMEM)], out_specs=pl.BlockSpec(memory_space=pltpu.VMEM),
        scratch_shapes=[pltpu.SemaphoreType.DMA((7,)), pltpu.SemaphoreType.DMA((7,)), pltpu.SemaphoreType.DMA],
    )(buf)


CHIP_REL = [(1, 0), (0, 1), (1, 1)]


class _Side:
    def __init__(self, arrays, out_shapes, scratch, start, finish):
        self.arrays, self.out_shapes, self.scratch, self.start, self.finish = arrays, out_shapes, scratch, start, finish


def _chip_of(k):
    x, y, _ = _here()
    dx, dy = CHIP_REL[k]
    return _flip(x, dx), _flip(y, dy)


def _scatter_side(arrays):
    n = len(arrays)

    def parts(ins, outs, sems):
        send, recv, lsem = sems
        x, y, c = _here()
        s = 2 * x + y

        def copy(w, k, mine):
            px, py = _chip_of(k)
            return pltpu.make_async_remote_copy(
                src_ref=ins[w].at[2 * px + py], dst_ref=outs[w].at[s if mine else 2 * px + py],
                send_sem=send.at[3 * w + k], recv_sem=recv.at[3 * w + k], device_id=(px, py, c), device_id_type=MESH_ID)

        local = [pltpu.make_async_copy(ins[w].at[s], outs[w].at[s], lsem.at[w]) for w in range(n)]
        return copy, local

    def start(ins, outs, sems):
        copy, local = parts(ins, outs, sems)
        for cp in local:
            cp.start()
        for w in range(n):
            for k in range(3):
                copy(w, k, True).start()

    def finish(ins, outs, sems):
        copy, local = parts(ins, outs, sems)
        for w in range(n):
            for k in range(3):
                copy(w, k, False).wait_recv()
        for w in range(n):
            for k in range(3):
                copy(w, k, True).wait_send()
        for cp in local:
            cp.wait()

    scratch = [pltpu.SemaphoreType.DMA((3 * n,)), pltpu.SemaphoreType.DMA((3 * n,)), pltpu.SemaphoreType.DMA((n,))]
    return _Side(arrays, [jax.ShapeDtypeStruct(a.shape, a.dtype) for a in arrays], scratch, start, finish)


def _gather_side(shards):
    n = len(shards)

    def parts(ins, outs, sems):
        send, recv, fsend, frecv, lsem = sems
        x, y, c = _here()
        s = 2 * x + y

        def half(ref, w, which):
            rows = shards[w].shape[0] // 2
            return ref.at[pl.ds(pl.multiple_of(which * rows, 16), rows)]

        def over_ici(w, k, mine):
            px, py = _chip_of(k)
            return pltpu.make_async_remote_copy(
                src_ref=half(ins[w], w, c), dst_ref=half(outs[w].at[s if mine else 2 * px + py], w, c),
                send_sem=send.at[3 * w + k], recv_sem=recv.at[3 * w + k], device_id=(px, py, c), device_id_type=MESH_ID)

        def to_sibling(w, k, which):
            px, py = _chip_of(k)
            part = half(outs[w].at[2 * px + py], w, which)
            return pltpu.make_async_remote_copy(
                src_ref=part, dst_ref=part, send_sem=fsend.at[3 * w + k], recv_sem=frecv.at[3 * w + k],
                device_id=(x, y, 1 - c), device_id_type=MESH_ID)

        local = [pltpu.make_async_copy(ins[w], outs[w].at[s], lsem.at[w]) for w in range(n)]
        return c, over_ici, to_sibling, local

    def start(ins, outs, sems):
        _, over_ici, _, local = parts(ins, outs, sems)
        for cp in local:
            cp.start()
        for w in range(n):
            for k in range(3):
                over_ici(w, k, True).start()

    def finish(ins, outs, sems):
        c, over_ici, to_sibling, local = parts(ins, outs, sems)
        for w in range(n):
            for k in range(3):
                over_ici(w, k, False).wait_recv()
                to_sibling(w, k, c).start()
        for w in range(n):
            for k in range(3):
                to_sibling(w, k, 1 - c).wait_recv()
        for w in range(n):
            for k in range(3):
                over_ici(w, k, True).wait_send()
                to_sibling(w, k, c).wait_send()
        for cp in local:
            cp.wait()

    scratch = [pltpu.SemaphoreType.DMA((3 * n,))] * 4 + [pltpu.SemaphoreType.DMA((n,))]
    return _Side(shards, [jax.ShapeDtypeStruct((N_CHIPS,) + a.shape, a.dtype) for a in shards], scratch, start, finish)


def _sibling_exchange(name, arrays):
    n = len(arrays)

    def body(*refs):
        ins, outs = refs[:n], refs[n:2 * n]
        send, recv = refs[2 * n:]
        x, y, c = _here()
        cps = [pltpu.make_async_remote_copy(src_ref=ins[w], dst_ref=outs[w], send_sem=send.at[w], recv_sem=recv.at[w],
                                            device_id=(x, y, 1 - c), device_id_type=MESH_ID) for w in range(n)]
        for cp in cps:
            cp.start()
        for cp in cps:
            cp.wait()

    any_spec = pl.BlockSpec(memory_space=pl.ANY)
    return pl.pallas_call(
        body, name=name, out_shape=[jax.ShapeDtypeStruct(a.shape, a.dtype) for a in arrays],
        in_specs=[any_spec] * n, out_specs=[any_spec] * n,
        scratch_shapes=[pltpu.SemaphoreType.DMA((n,)), pltpu.SemaphoreType.DMA((n,))],
    )(*arrays)


def _pack(arrs):
    flat = jnp.concatenate([a.reshape(-1).astype(F32) for a in arrs])
    n = flat.shape[0]
    rows = -(-n // (8 * LANES)) * 8
    return jnp.pad(flat, (0, rows * LANES - n)).reshape(rows, LANES)


def _unpack(buf, shapes):
    lead = buf.shape[:-2]
    flat = buf.reshape(lead + (-1,))
    out, off = [], 0
    for s in shapes:
        n = 1
        for d in s:
            n *= d
        out.append(flat[..., off:off + n].reshape(lead + tuple(s)))
        off += n
    return out


def _pad_w_in(w):
    return jnp.concatenate([w[:, :2048], w[:, 2816:2824], jnp.zeros((w.shape[0], P_XBC - P_DT - 8), w.dtype),
                            w[:, 2048:2816], w[:, 2824:]], axis=1)


def _unpad_w_in(g):
    return jnp.concatenate([g[:, :P_DT], g[:, P_XBC:P_G], g[:, P_DT:P_DT + 8], g[:, P_G:]], axis=1)


FFN_HALF = FFN_HIDDEN // 2


def _ffn_in_cols(w):
    h = FFN_HALF
    return jnp.concatenate([w[:, :h], w[:, 2 * h:3 * h], w[:, h:2 * h], w[:, 3 * h:]], axis=1)


def _row(v):
    return v.reshape(1, -1)


BIG = (("w_in", 2), ("w_sc_out", 2), ("w_sb_out", 2), ("w_ssm_out", 2), ("w_o", 1), ("w_ffn_in", 2), ("w_ffn_out", 1))
SMALL = ("mod_b", "g_pre_mix", "g_post_mix", "g_pre_ffn", "g_post_ffn", "sc_conv_w", "ssm_conv_w", "ssm_conv_b",
         "ssm_dt_bias", "ssm_a_log", "ssm_d", "ssm_norm_w")
WEIGHT_ORDER = ("mod_w", "mod_b", "g_pre_mix", "g_post_mix", "g_pre_ffn", "g_post_ffn", "w_in", "sc_conv_w",
                "ssm_conv_w", "ssm_conv_b", "ssm_dt_bias", "ssm_a_log", "ssm_d", "ssm_norm_w", "w_sc_out", "w_sb_out",
                "w_ssm_out", "w_o", "w_ffn_in", "w_ffn_out")


def _mm_mid(name, a, w, x, vecs):
    return _mm_epi(name, a, w, "nn", D_MODEL, [(x, D_MODEL)] + [(v, None) for v in vecs],
                   lambda p, x, *v: (p,) + tuple(_mid(x, p, *v)), [(D_MODEL, BF16), (D_MODEL, F32), (D_MODEL, BF16)])


def _layer_fwd(l, x_in, h, W, V, sides, next_vecs):
    S = {"x_in": x_in, "h": h}
    side, handler = sides.get("in_proj", (None, None))
    P = _mm(f"in_proj{l}", h, W["w_in"], "nn", BF16, tm=2048, tn_cap=1024, side=side)
    if side:
        handler(P[1:])
        P = P[0]
    S["P"] = P
    S["ya"] = _shortconv_fwd(P, V["sc_w"])
    S["yb"], S["yb32"] = _sb_fwd(P)
    S["pre"] = _ssmconv_fwd(P, V["ssm_w"], V["ssm_b"])
    S["y_ssd"], S["states"] = _ssd_fwd(S["pre"], P, V["dtb"], V["alog"])
    S["yc"] = _rowwise(f"ssm_post{l}", lambda y, px, z, d, nw: _ssm_post(y, px, _f(z), d, nw),
                       [(S["y_ssd"], SSM_INNER, 0), (S["pre"], SSM_INNER, 0), (P, SSM_INNER, P_Z // SSM_INNER)],
                       [V["d_full"], V["norm_w"]], [(SSM_INNER, BF16)])[0]
    S["merged"] = _merge_fwd(f"merge{l}", P, [S["ya"], S["yb"], S["yc"]],
                             [W["w_sc_out"], W["w_sb_out"], W["w_ssm_out"]])
    S["mix"], S["x1"], S["h2"] = _mm_mid(f"w_o{l}", S["merged"], W["w_o"], x_in, V["mid_mix"])
    side, handler = sides.get("ffn_in", (None, None))
    res = _mm_epi(f"ffn_in{l}", S["h2"], W["w_ffn_in"], "nn", 2 * FFN_HALF, [],
                  lambda p: (p, _swiglu(p[:, :FFN_HALF], p[:, FFN_HALF:])),
                  [(2 * FFN_HALF, BF16), (FFN_HALF, BF16)], side=side)
    S["GU"], S["act"] = res[0], res[1]
    if side:
        handler(res[2:])
    if next_vecs is None:
        S["f"] = _mm(f"ffn_out{l}", S["act"], W["w_ffn_out"], "nn", BF16)
    else:
        S["f"], S["x_next"], S["h_next"] = _mm_mid(f"ffn_out{l}", S["act"], W["w_ffn_out"], S["x1"], next_vecs)
    return S


BRANCH_WIDTHS = (SC_WIDTH, 256, SSM_INNER)


def _branch_specs(tm):
    gb = P_G // D_MODEL
    gates = [pl.BlockSpec((tm, D_MODEL), functools.partial(lambda i, cb: (i, cb), cb=gb + k)) for k in range(3)]
    ys = [pl.BlockSpec((tm, w), lambda i: (i, 0)) for w in BRANCH_WIDTHS]
    ws = [pl.BlockSpec((w, D_MODEL), lambda i: (0, 0)) for w in BRANCH_WIDTHS]
    return gates, ys, ws


def _merge_fwd(name, P, ys, ws, tm=512):
    L = P.shape[0]
    tm = min(tm, L)
    gates, y_specs, w_specs = _branch_specs(tm)

    def body(ga, gb, gc, ya, yb, yc, wa, wb, wc, o_ref):
        acc = None
        for g_ref, y_ref, w_ref in ((ga, ya, wa), (gb, yb, wb), (gc, yc, wc)):
            t = jax.nn.sigmoid(_f(g_ref[...])) * jnp.dot(y_ref[...], w_ref[...], preferred_element_type=F32)
            acc = t if acc is None else acc + t
        o_ref[...] = acc.astype(o_ref.dtype)

    return pl.pallas_call(
        body, name=name, grid=(L // tm,), in_specs=gates + y_specs + w_specs,
        out_specs=pl.BlockSpec((tm, D_MODEL), lambda i: (i, 0)), out_shape=jax.ShapeDtypeStruct((L, D_MODEL), BF16),
        compiler_params=_params(("arbitrary",)),
    )(P, P, P, *ys, *ws)


def _merge_bwd(name, P, ys, ws, dmerged, tm=512):
    L = P.shape[0]
    tm = min(tm, L)
    gates, y_specs, w_specs = _branch_specs(tm)

    def body(ga, gb, gc, ya, yb, yc, wa, wb, wc, dm_ref, dg_ref, dya, dyb, dyc, gwa, gwb, gwc):
        @pl.when(pl.program_id(0) == 0)
        def _():
            for r in (gwa, gwb, gwc):
                r[...] = jnp.zeros_like(r)

        dm = _f(dm_ref[...])
        for k, (g_ref, y_ref, w_ref, dy_ref, gw_ref) in enumerate(
                ((ga, ya, wa, dya, gwa), (gb, yb, wb, dyb, gwb), (gc, yc, wc, dyc, gwc))):
            y, w = y_ref[...], w_ref[...]
            s = jax.nn.sigmoid(_f(g_ref[...]))
            proj = jnp.dot(y, w, preferred_element_type=F32)
            d_proj = (dm * s).astype(BF16)
            dg_ref[:, k * D_MODEL:(k + 1) * D_MODEL] = (dm * proj * s * (1.0 - s)).astype(dg_ref.dtype)
            dy_ref[...] = _dot_nt(d_proj, w).astype(dy_ref.dtype)
            gw_ref[...] += _dot_tn(y, d_proj)

    gate_cols = pl.BlockSpec((tm, P_WIDTH - P_G), lambda i: (i, P_G // (P_WIDTH - P_G)))
    return pl.pallas_call(
        body, name=name, grid=(L // tm,),
        in_specs=gates + y_specs + w_specs + [pl.BlockSpec((tm, D_MODEL), lambda i: (i, 0))],
        out_specs=[gate_cols] + y_specs + w_specs,
        out_shape=[jax.ShapeDtypeStruct((L, P_WIDTH), BF16)] + [jax.ShapeDtypeStruct((L, w), BF16) for w in BRANCH_WIDTHS]
        + [jax.ShapeDtypeStruct((w, D_MODEL), F32) for w in BRANCH_WIDTHS],
        compiler_params=_params(("arbitrary",)),
    )(P, P, P, *ys, *ws, dmerged)


def _assemble_dp(name, dP, parts, tl=512):
    L = dP.shape[0]
    tl = min(tl, L)
    n = len(parts)

    def body(*refs):
        o_ref = refs[n + 1]
        o_ref[...] = jnp.concatenate([r[...].astype(o_ref.dtype) for r in refs[:n]], axis=1)

    return pl.pallas_call(
        body, name=name, grid=(L // tl,),
        in_specs=[pl.BlockSpec((tl, a.shape[1]), lambda i: (i, 0)) for a in parts] + [pl.BlockSpec(memory_space=pl.ANY)],
        out_specs=pl.BlockSpec((tl, P_G), lambda i: (i, 0)), out_shape=jax.ShapeDtypeStruct(dP.shape, dP.dtype),
        input_output_aliases={n: 0}, compiler_params=_params(("arbitrary",)),
    )(*parts, dP)


def _layer_bwd(l, S, W, V, dx1, df, sides, landed):
    G = {}
    P = S["P"]

    def mm(key, *args, **kw):
        if key not in sides:
            return _mm(f"{key}{l}", *args, **kw)
        names, layer, make = sides[key]
        res = _mm(f"{key}{l}", *args, side=make(G), **kw)
        for n, a in zip(names, res[1:]):
            landed[(n, layer)] = a
        return res[0]

    G["w_ffn_out"] = _mm(f"gw_ffn_out{l}", S["act"], df, "tn", F32)

    def swiglu_bwd(d_act, gu):
        gt, up = _f(gu[:, :FFN_HALF]), _f(gu[:, FFN_HALF:])
        s = jax.nn.sigmoid(gt)
        gs = gt * s
        return (jnp.concatenate([d_act * up * (s + gs * (1.0 - s)), d_act * gs], axis=1),)

    names, layer, make = sides.get("d_gu", ((), None, None))
    res = _mm_epi(f"d_gu{l}", df, W["w_ffn_out"], "nt", FFN_HALF, [(S["GU"], 2 * FFN_HALF)], swiglu_bwd,
                  [(2 * FFN_HALF, BF16)], side=make(G) if make else None)
    dGU = res[0]
    for n, a in zip(names, res[1:]):
        landed[(n, layer)] = a
    dh2 = mm("d_h2", dGU, W["w_ffn_in"], "nt", BF16)
    G["w_ffn_in"] = _ffn_in_cols(mm("gw_ffn_in", S["h2"], dGU, "tn", F32))
    dx, dmix, G["gate1"], G["g_post_mix"], G["g_pre_ffn"], G["scale2"], G["shift2"] = _mid_bwd(
        f"mid_mix_bwd{l}", S["x_in"], S["mix"], dx1, dh2, V["mid_mix"])
    dmerged = _mm(f"d_merged{l}", dmix, W["w_o"], "nt", BF16)
    G["w_o"] = _mm(f"gw_o{l}", S["merged"], dmix, "tn", F32)

    dP, dya, dyb, dyc, G["w_sc_out"], G["w_sb_out"], G["w_ssm_out"] = _merge_bwd(
        f"merge_bwd{l}", P, [S["ya"], S["yb"], S["yc"]], [W["w_sc_out"], W["w_sb_out"], W["w_ssm_out"]], dmerged)

    def post_bwd(y_ssd, px, z, d, dfull, nw):
        z, d = _f(z), _f(d)
        sx, sz = jax.nn.sigmoid(px), jax.nn.sigmoid(z)
        xs, gz = px * sx, z * sz
        t = y_ssd + xs * dfull
        y = t * gz
        dn = d * nw
        half = SSM_INNER // 2
        ns, dys = [], []
        for g in range(2):
            yg, dng = y[:, g * half:(g + 1) * half], dn[:, g * half:(g + 1) * half]
            r = lax.rsqrt(jnp.mean(yg * yg, axis=-1, keepdims=True) + NORM_EPS)
            ns.append(yg * r)
            dys.append(r * (dng - ns[g] * jnp.mean(dng * ns[g], axis=-1, keepdims=True)))
        n, dy = jnp.concatenate(ns, axis=1), jnp.concatenate(dys, axis=1)
        dt = dy * gz
        return (dt, dt * dfull * (sx + xs * (1.0 - sx)), dy * t * (sz + gz * (1.0 - sz)),
                jnp.sum(dt * xs, axis=0, keepdims=True), jnp.sum(d * n, axis=0, keepdims=True))

    dy_ssd, dxs, dz, G["d_full"], G["ssm_norm_w"] = _rowwise(
        f"ssm_post_bwd{l}", post_bwd,
        [(S["y_ssd"], SSM_INNER, 0), (S["pre"], SSM_INNER, 0), (P, SSM_INNER, P_Z // SSM_INNER), (dyc, SSM_INNER, 0)],
        [V["d_full"], V["norm_w"]], [(SSM_INNER, F32), (SSM_INNER, F32), (SSM_INNER, BF16)], [(1, SSM_INNER)] * 2)
    dpre, ddt, G["dtb"], G["alog"] = _ssd_bwd(S["pre"], P, S["states"], dy_ssd, dxs, V["dtb"], V["alog"])
    dxbc, w0, w1, w2, w3, G["ssm_conv_b"] = _ssmconv_bwd(P, dpre, V["ssm_w"])
    G["ssm_conv_w"] = jnp.concatenate([w0, w1, w2, w3], axis=0)
    dq, dk, dv = _sb_bwd(P, dyb, S["yb32"])
    dA, s0, s1, s2 = _shortconv_bwd(P, dya, V["sc_w"])
    G["sc_conv_w"] = jnp.concatenate([s0, s1, s2], axis=0)
    dP = _assemble_dp(f"assemble_dp{l}", dP, [dA, dq, dk, dv, dz, ddt, dxbc])
    G["w_in"] = _mm(f"gw_in{l}", S["h"], dP, "tn", F32, tn_cap=1024)
    dh = mm("d_h", dP, W["w_in"], "nt", BF16, tk_cap=3072)
    return dx, dh, G


def kernel(x, c, mod_w, mod_b, g_pre_mix, g_post_mix, g_pre_ffn, g_post_ffn, w_in, sc_conv_w, ssm_conv_w, ssm_conv_b, ssm_dt_bias, ssm_a_log, ssm_d, ssm_norm_w, w_sc_out, w_sb_out, w_ssm_out, w_o, w_ffn_in, w_ffn_out, loss_target, m_mod_w, m_mod_b, m_g_pre_mix, m_g_post_mix, m_g_pre_ffn, m_g_post_ffn, m_w_in, m_sc_conv_w, m_ssm_conv_w, m_ssm_conv_b, m_ssm_dt_bias, m_ssm_a_log, m_ssm_d, m_ssm_norm_w, m_w_sc_out, m_w_sb_out, m_w_ssm_out, m_w_o, m_w_ffn_in, m_w_ffn_out, v_mod_w, v_mod_b, v_g_pre_mix, v_g_post_mix, v_g_pre_ffn, v_g_post_ffn, v_w_in, v_sc_conv_w, v_ssm_conv_w, v_ssm_conv_b, v_ssm_dt_bias, v_ssm_a_log, v_ssm_d, v_ssm_norm_w, v_w_sc_out, v_w_sb_out, v_w_ssm_out, v_w_o, v_w_ffn_in, v_w_ffn_out):
    wts = dict(mod_w=mod_w, mod_b=mod_b, g_pre_mix=g_pre_mix, g_post_mix=g_post_mix, g_pre_ffn=g_pre_ffn,
               g_post_ffn=g_post_ffn, w_in=w_in, sc_conv_w=sc_conv_w, ssm_conv_w=ssm_conv_w, ssm_conv_b=ssm_conv_b,
               ssm_dt_bias=ssm_dt_bias, ssm_a_log=ssm_a_log, ssm_d=ssm_d, ssm_norm_w=ssm_norm_w, w_sc_out=w_sc_out,
               w_sb_out=w_sb_out, w_ssm_out=w_ssm_out, w_o=w_o, w_ffn_in=w_ffn_in, w_ffn_out=w_ffn_out)
    mom = dict(mod_w=m_mod_w, mod_b=m_mod_b, g_pre_mix=m_g_pre_mix, g_post_mix=m_g_post_mix, g_pre_ffn=m_g_pre_ffn,
               g_post_ffn=m_g_post_ffn, w_in=m_w_in, sc_conv_w=m_sc_conv_w, ssm_conv_w=m_ssm_conv_w,
               ssm_conv_b=m_ssm_conv_b, ssm_dt_bias=m_ssm_dt_bias, ssm_a_log=m_ssm_a_log, ssm_d=m_ssm_d,
               ssm_norm_w=m_ssm_norm_w, w_sc_out=m_w_sc_out, w_sb_out=m_w_sb_out, w_ssm_out=m_w_ssm_out, w_o=m_w_o,
               w_ffn_in=m_w_ffn_in, w_ffn_out=m_w_ffn_out)
    var = dict(mod_w=v_mod_w, mod_b=v_mod_b, g_pre_mix=v_g_pre_mix, g_post_mix=v_g_post_mix, g_pre_ffn=v_g_pre_ffn,
               g_post_ffn=v_g_post_ffn, w_in=v_w_in, sc_conv_w=v_sc_conv_w, ssm_conv_w=v_ssm_conv_w,
               ssm_conv_b=v_ssm_conv_b, ssm_dt_bias=v_ssm_dt_bias, ssm_a_log=v_ssm_a_log, ssm_d=v_ssm_d,
               ssm_norm_w=v_ssm_norm_w, w_sc_out=v_w_sc_out, w_sb_out=v_w_sb_out, w_ssm_out=v_w_ssm_out, w_o=v_w_o,
               w_ffn_in=v_w_ffn_in, w_ffn_out=v_w_ffn_out)
    xi, yi, ci = _here()
    chip = 2 * xi + yi
    me = 4 * xi + 2 * yi + ci
    x0, target = x[0], loss_target[0]

    first_shapes = [(D_MODEL,), sc_conv_w.shape, ssm_conv_w.shape]
    g0 = _allgather_small("gather_cond", _pack([c, sc_conv_w, ssm_conv_w]))
    c_rows, sc_sh, ssm_sh = _unpack(g0, first_shapes)
    c_all = c_rows
    sc_w = jnp.concatenate([sc_sh[2 * j] for j in range(N_CHIPS)], axis=-1)
    ssm_w = jnp.concatenate([ssm_sh[2 * j] for j in range(N_CHIPS)], axis=-1)

    mod_b_sh = lax.dynamic_slice_in_dim(mod_b, chip * MOD_SHARD, MOD_SHARD, axis=1).reshape(DEPTH, 1, MOD_SHARD)
    modpart = _mod_fwd(c_all, mod_w, mod_b_sh)
    g1 = _allgather_small("gather_mod", modpart.reshape(-1, LANES)).reshape(N_DEV, DEPTH, N_DEV, MOD_SHARD)
    mod = jnp.concatenate([lax.dynamic_index_in_dim(g1[2 * j], me, axis=1, keepdims=False) for j in range(N_CHIPS)],
                          axis=-1)

    def layer_shards(l):
        return [wts[n][l].astype(BF16) for n, _ in BIG]

    def full_weights(which, gathered):
        W = {n: jnp.concatenate([g[j] for j in range(N_CHIPS)], axis=ax - 1) for (n, ax), g in zip(which, gathered)}
        if "w_in" in W:
            W["w_in"] = _pad_w_in(W["w_in"])
        if "w_ffn_in" in W:
            W["w_ffn_in"] = _ffn_in_cols(W["w_ffn_in"])
        return W

    Ws = [{}, {}]
    fwd_sides = [{"in_proj": (_gather_side(layer_shards(0)[1:]), lambda got: Ws[0].update(full_weights(BIG[1:], got))),
                  "ffn_in": (_gather_side(layer_shards(1)), lambda got: Ws[1].update(full_weights(BIG, got)))}, {}]
    Vs = []
    for l in range(DEPTH):
        sh1, sc1, gt1, sh2, sc2, gt2 = [_row(v) for v in jnp.split(mod[l], N_MOD)]
        Vs.append(dict(
            shift1=sh1, scale1=sc1, g_pre_mix=_row(g_pre_mix[l]),
            mid_mix=[gt1, _row(g_post_mix[l]), _row(g_pre_ffn[l]), sc2, sh2],
            gate2=gt2, g_post_ffn=_row(g_post_ffn[l]),
            sc_w=sc_w[l], ssm_w=ssm_w[l], ssm_b=_row(ssm_conv_b[l]),
            dtb=_row(jnp.pad(ssm_dt_bias[l], (0, LANES - SSM_HEADS))), alog=_row(jnp.pad(ssm_a_log[l], (0, LANES - SSM_HEADS))),
            d_full=_row(jnp.repeat(ssm_d[l], SSM_INNER // SSM_HEADS)), norm_w=_row(ssm_norm_w[l])))

    def mid_ffn_vecs(l):
        return [Vs[l]["gate2"], Vs[l]["g_post_ffn"], Vs[l + 1]["g_pre_mix"], Vs[l + 1]["scale1"], Vs[l + 1]["shift1"]]

    saved = []
    x_in = x0
    h, *got = _first_fwd(x0, [Vs[0]["g_pre_mix"], Vs[0]["scale1"], Vs[0]["shift1"]], _gather_side(layer_shards(0)[:1]))
    Ws[0].update(full_weights(BIG[:1], got))
    for l in range(DEPTH):
        S = _layer_fwd(l, x_in, h, Ws[l], Vs[l], fwd_sides[l], mid_ffn_vecs(l) if l + 1 < DEPTH else None)
        saved.append(S)
        if l + 1 < DEPTH:
            x_in, h = S["x_next"], S["h_next"]

    def pieces(G, names):
        out = []
        for n, ax in BIG:
            if n in names:
                g = _unpad_w_in(G[n]) if n == "w_in" else G[n]
                out.append(jnp.stack(jnp.split(g, N_CHIPS, axis=ax - 1)).astype(BF16))
        return out

    small_names = tuple(n for n, _ in BIG if n not in ("w_in", "w_ffn_in"))
    late_names = tuple(n for n, _ in BIG if n != "w_in")
    landed = {}

    GL = [None] * DEPTH
    S = saved[-1]
    dx1, df, g_gate2, g_gpf, loss_cols = _last_bwd(S["x1"], S["f"], target, [Vs[-1]["gate2"], Vs[-1]["g_post_ffn"]])
    for l in reversed(range(DEPTH)):
        sides = {}
        if l + 1 < DEPTH:
            for key, names in (("d_gu", small_names), ("d_h2", ("w_ffn_in",)), ("gw_ffn_in", ("w_in",))):
                sides[key] = (names, l + 1, lambda G, up=GL[l + 1], names=names: _scatter_side(pieces(up, names)))
        if l == 0:
            sides["d_h"] = (late_names, l, lambda G: _scatter_side(pieces(G, late_names)))
        dx, dh, G = _layer_bwd(l, saved[l], Ws[l], Vs[l], dx1, df, sides, landed)
        G["gate2"], G["g_post_ffn"] = g_gate2, g_gpf
        GL[l] = G
        if l > 0:
            Sp = saved[l - 1]
            dx1, df, g_gate2, g_gpf, G["g_pre_mix"], G["scale1"], G["shift1"] = _mid_bwd(
                f"mid_ffn_bwd{l - 1}", Sp["x1"], Sp["f"], dx, dh, mid_ffn_vecs(l - 1))
        else:
            grad_x, G["g_pre_mix"], G["scale1"], G["shift1"], landed[("w_in", 0)] = _first_bwd(
                x0, dx, dh, [Vs[0]["g_pre_mix"], Vs[0]["scale1"], Vs[0]["shift1"]], _scatter_side(pieces(G, ("w_in",))))
    loss = lax.psum(jnp.sum(loss_cols), ("x", "y", "c"))

    def both(key, shape=None):
        a = jnp.stack([GL[l][key] for l in range(DEPTH)])
        return a if shape is None else a.reshape(shape)

    dmod = jnp.concatenate([both(k, (DEPTH, D_MODEL)) for k in ("shift1", "scale1", "gate1", "shift2", "scale2", "gate2")],
                           axis=1)
    part_small = dict(
        mod_b=dmod, g_pre_mix=both("g_pre_mix", (DEPTH, D_MODEL)), g_post_mix=both("g_post_mix", (DEPTH, D_MODEL)),
        g_pre_ffn=both("g_pre_ffn", (DEPTH, D_MODEL)), g_post_ffn=both("g_post_ffn", (DEPTH, D_MODEL)),
        sc_conv_w=both("sc_conv_w"), ssm_conv_w=both("ssm_conv_w"), ssm_conv_b=both("ssm_conv_b", (DEPTH, SSM_CONV_DIM)),
        ssm_dt_bias=both("dtb", (DEPTH, LANES))[:, :SSM_HEADS], ssm_a_log=both("alog", (DEPTH, LANES))[:, :SSM_HEADS],
        ssm_d=both("d_full", (DEPTH, SSM_HEADS, SSM_INNER // SSM_HEADS)).sum(-1),
        ssm_norm_w=both("ssm_norm_w", (DEPTH, SSM_INNER)))
    small_shapes = [part_small[n].shape for n in SMALL]
    g2 = _allgather_small("gather_small_grads", _pack([part_small[n] for n in SMALL]))
    tot = dict(zip(SMALL, _unpack(_sum_slots("sum_small_grads", g2), small_shapes)))
    dmod_all = _unpack(g2, small_shapes)[0]
    dmod_sh = jnp.swapaxes(lax.dynamic_slice_in_dim(dmod_all, chip * MOD_SHARD, MOD_SHARD, axis=2), 0, 1)
    grads = {"mod_w": _mod_bwd(c_all, dmod_sh)}
    for n in SMALL:
        grads[n] = tot[n]
    grads["sc_conv_w"] = lax.dynamic_slice_in_dim(tot["sc_conv_w"], chip * 64, 64, axis=2)
    grads["ssm_conv_w"] = lax.dynamic_slice_in_dim(tot["ssm_conv_w"], chip * 192, 192, axis=2)

    keys = [(n, l) for n, _ in BIG for l in range(DEPTH)]
    mine = [_sum_slots(f"sum_{n}{l}", landed[(n, l)]) for n, l in keys]
    theirs = dict(zip(keys, _sibling_exchange("swap_core_sums", mine)))
    mine = dict(zip(keys, mine))

    out = {}

    def update(name, w2, gs, m2, v2, shape):
        g, d, nm, nv = _adamw(f"adamw_{name}", w2, gs, m2, v2)
        out[name] = tuple(a.reshape(shape) for a in (g, d, nm, nv))

    for n, _ in BIG:
        shp = wts[n].shape
        two = (-1, shp[-1])
        by_layer = [tuple(src[(n, l)] for l in range(DEPTH)) for src in (mine, theirs)]
        update(n, wts[n].reshape(two), by_layer, mom[n].reshape(two), var[n].reshape(two), shp)
    two = (-1, MOD_SHARD)
    update("mod_w", mod_w.reshape(two), [grads["mod_w"].reshape(two)], m_mod_w.reshape(two), v_mod_w.reshape(two), mod_w.shape)
    shapes = [wts[n].shape for n in SMALL]
    res = _adamw("adamw_small", _pack([wts[n] for n in SMALL]), [_pack([grads[n] for n in SMALL])],
                 _pack([mom[n] for n in SMALL]), _pack([var[n] for n in SMALL]))
    for n, g, d, nm, nv in zip(SMALL, *[_unpack(r, shapes) for r in res]):
        out[n] = (g, d, nm, nv)

    result = [loss, grad_x[None]]
    for k in range(4):
        result += [out[n][k] for n in WEIGHT_ORDER]
    return tuple(result)
```

```python
import functools

import jax
import jax.numpy as jnp
from jax import lax
from jax.experimental import pallas as pl
from jax.experimental.pallas import tpu as pltpu

F32 = jnp.float32
BF16 = jnp.bfloat16
HIGHEST = lax.Precision.HIGHEST
MESH_ID = pl.DeviceIdType.MESH

D_MODEL = 1024
DEPTH = 2
SC_WIDTH = 256
SB_HEAD_DIM = 64
SSM_INNER = 512
SSM_HEADS = 8
SSM_STATE = 64
SSM_CONV = 4
SSM_CHUNK = 256
SSM_CONV_DIM = 768
FFN_HIDDEN = 2816
NORM_EPS = 1e-6
N_MOD = 6
N_CHIPS = 4
N_DEV = 8

ADAM_LR = 0.001
ADAM_B1 = 0.9
ADAM_B2 = 0.999
ADAM_EPS = 1e-08
ADAM_WD = 0.01
ADAM_STEP = 10

P_WIDTH = 6144
P_A, P_B, P_Z, P_DT, P_XBC, P_G = 0, 768, 1536, 2048, 2304, 3072
DT_PAD = 256

VMEM_LIMIT_BYTES = 56 * 1024 * 1024
LANES = 128

SB_LOG_CUTOFF = -105.0
SB_TQ = 256
SB_TK = 256
SB_SUBS = 2
SB_STRAIGHT = 2


def _params(sem):
    return pltpu.CompilerParams(dimension_semantics=sem, vmem_limit_bytes=VMEM_LIMIT_BYTES)


def _pick(n, cap):
    if n <= cap:
        return n
    best = None
    for m in range(LANES, cap + 1, LANES):
        if n % m == 0:
            best = m
    assert best is not None, (n, cap)
    return best


def _rowwise(name, fn, rows, vecs, row_outs, acc_outs=(), tl=512, side=None):
    L = rows[0][0].shape[0]
    tl = min(tl, L)
    assert L % tl == 0
    n_in = len(rows) + len(vecs)
    n_ro, n_ao = len(row_outs), len(acc_outs)
    n_si = len(side.arrays) if side else 0
    n_so = len(side.out_shapes) if side else 0

    def body(*refs):
        ins, s_in = refs[:n_in], refs[n_in:n_in + n_si]
        outs = refs[n_in + n_si:]
        ro, ao, s_out, sems = outs[:n_ro], outs[n_ro:n_ro + n_ao], outs[n_ro + n_ao:n_ro + n_ao + n_so], outs[n_ro + n_ao + n_so:]
        if side:
            @pl.when(pl.program_id(0) == 0)
            def _():
                side.start(s_in, s_out, sems)

        _rows(ins, ro, ao)
        if side:
            @pl.when(pl.program_id(0) == L // tl - 1)
            def _():
                side.finish(s_in, s_out, sems)

    def _rows(ins, ro, ao):
        vals = fn(*[r[...] for r in ins])
        if not isinstance(vals, (tuple, list)):
            vals = (vals,)
        for o, v in zip(ro, vals[:n_ro]):
            o[...] = v.astype(o.dtype)
        if ao:
            @pl.when(pl.program_id(0) == 0)
            def _():
                for o in ao:
                    o[...] = jnp.zeros_like(o)
            for o, v in zip(ao, vals[n_ro:]):
                o[...] += v.astype(F32)

    in_specs = [pl.BlockSpec((tl, w), functools.partial(lambda i, cb: (i, cb), cb=cb)) for _, w, cb in rows]
    in_specs += [pl.BlockSpec(v.shape, lambda i: (0, 0)) for v in vecs]
    out_specs = [pl.BlockSpec((tl, w), lambda i: (i, 0)) for w, _ in row_outs]
    out_specs += [pl.BlockSpec(s, lambda i: (0, 0)) for s in acc_outs]
    out_shape = [jax.ShapeDtypeStruct((L, w), dt) for w, dt in row_outs]
    out_shape += [jax.ShapeDtypeStruct(s, F32) for s in acc_outs]
    any_spec = pl.BlockSpec(memory_space=pl.ANY)
    return pl.pallas_call(
        body, name=name, grid=(L // tl,), in_specs=in_specs + [any_spec] * n_si, out_specs=out_specs + [any_spec] * n_so,
        out_shape=out_shape + (side.out_shapes if side else []), scratch_shapes=side.scratch if side else [],
        compiler_params=_params(("arbitrary",)),
    )(*[a for a, _, _ in rows], *vecs, *(side.arrays if side else []))


def _mm(name, a, b, mode, out_dtype, tm=1024, tn_cap=1408, tk_cap=2816, side=None):
    if mode == "nn":
        (M, K), (_, N) = a.shape, b.shape
    elif mode == "nt":
        (M, K), (N, _) = a.shape, b.shape
    else:
        (K, M), (_, N) = a.shape, b.shape
        tm, tk_cap = 1408, 2048
    tm = _pick(M, tm)
    tn = _pick(N, tn_cap)
    tk = _pick(K, tk_cap)
    nk = K // tk
    grid = (M // tm, N // tn, nk)
    n_si = len(side.arrays) if side else 0
    n_so = len(side.out_shapes) if side else 0
    n_acc = 1 if nk > 1 else 0

    def body(a_ref, b_ref, *rest):
        s_in, o_ref, s_out = rest[:n_si], rest[n_si], rest[n_si + 1:n_si + 1 + n_so]
        scr = rest[n_si + 1 + n_so:]
        if side:
            at = [pl.program_id(d) for d in range(3)]
            is_first = jnp.logical_and(jnp.logical_and(at[0] == 0, at[1] == 0), at[2] == 0)
            is_last = jnp.logical_and(jnp.logical_and(at[0] == grid[0] - 1, at[1] == grid[1] - 1), at[2] == grid[2] - 1)

            @pl.when(is_first)
            def _():
                side.start(s_in, s_out, scr[n_acc:])

        _product(a_ref, b_ref, o_ref, scr)
        if side:
            @pl.when(is_last)
            def _():
                side.finish(s_in, s_out, scr[n_acc:])

    def _product(a_ref, b_ref, o_ref, scr):
        if mode == "nn":
            p = jnp.dot(a_ref[...], b_ref[...], preferred_element_type=F32)
        elif mode == "nt":
            p = lax.dot_general(a_ref[...], b_ref[...], (((1,), (1,)), ((), ())), preferred_element_type=F32)
        else:
            p = lax.dot_general(a_ref[...], b_ref[...], (((0,), (0,)), ((), ())), preferred_element_type=F32)
        if nk == 1:
            o_ref[...] = p.astype(o_ref.dtype)
        else:
            acc = scr[0]
            k = pl.program_id(2)

            @pl.when(k == 0)
            def _():
                acc[...] = p

            @pl.when(k > 0)
            def _():
                acc[...] += p

            @pl.when(k == nk - 1)
            def _():
                o_ref[...] = acc[...].astype(o_ref.dtype)

    if mode == "nn":
        a_spec = pl.BlockSpec((tm, tk), lambda i, j, k: (i, k))
        b_spec = pl.BlockSpec((tk, tn), lambda i, j, k: (k, j))
    elif mode == "nt":
        a_spec = pl.BlockSpec((tm, tk), lambda i, j, k: (i, k))
        b_spec = pl.BlockSpec((tn, tk), lambda i, j, k: (j, k))
    else:
        a_spec = pl.BlockSpec((tk, tm), lambda i, j, k: (k, i))
        b_spec = pl.BlockSpec((tk, tn), lambda i, j, k: (k, j))
    any_spec = pl.BlockSpec(memory_space=pl.ANY)
    res = pl.pallas_call(
        body, name=name, grid=grid, in_specs=[a_spec, b_spec] + [any_spec] * n_si,
        out_specs=[pl.BlockSpec((tm, tn), lambda i, j, k: (i, j))] + [any_spec] * n_so,
        out_shape=[jax.ShapeDtypeStruct((M, N), out_dtype)] + (side.out_shapes if side else []),
        scratch_shapes=([pltpu.VMEM((tm, tn), F32)] if nk > 1 else []) + (side.scratch if side else []),
        compiler_params=_params(("arbitrary", "arbitrary", "arbitrary")),
    )(a, b, *(side.arrays if side else []))
    return res if side else res[0]


def _mm_epi(name, a, b, mode, tn, extras, epi, outs, tm=1024, side=None):
    if mode == "nn":
        (M, K), (_, N) = a.shape, b.shape
    else:
        (M, K), (N, _) = a.shape, b.shape
    tm = _pick(M, tm)
    grid = (N // tn, M // tm)
    n_ex, n_out = len(extras), len(outs)
    n_si = len(side.arrays) if side else 0
    n_so = len(side.out_shapes) if side else 0

    def body(*refs):
        a_ref, b_ref, ex = refs[0], refs[1], refs[2:2 + n_ex]
        s_in = refs[2 + n_ex:2 + n_ex + n_si]
        o_refs = refs[2 + n_ex + n_si:2 + n_ex + n_si + n_out]
        s_out = refs[2 + n_ex + n_si + n_out:2 + n_ex + n_si + n_out + n_so]
        sems = refs[2 + n_ex + n_si + n_out + n_so:]
        if side:
            @pl.when(jnp.logical_and(pl.program_id(0) == 0, pl.program_id(1) == 0))
            def _():
                side.start(s_in, s_out, sems)

        if mode == "nn":
            p = jnp.dot(a_ref[...], b_ref[...], preferred_element_type=F32)
        else:
            p = lax.dot_general(a_ref[...], b_ref[...], (((1,), (1,)), ((), ())), preferred_element_type=F32)
        for o, v in zip(o_refs, epi(p, *[r[...] for r in ex])):
            o[...] = v.astype(o.dtype)
        if side:
            @pl.when(jnp.logical_and(pl.program_id(0) == grid[0] - 1, pl.program_id(1) == grid[1] - 1))
            def _():
                side.finish(s_in, s_out, sems)

    any_spec = pl.BlockSpec(memory_space=pl.ANY)
    a_spec = pl.BlockSpec((tm, K), lambda j, i: (i, 0))
    once = dict(pipeline_mode=pl.Buffered(1))
    b_spec = (pl.BlockSpec((K, tn), lambda j, i: (0, j), **once) if mode == "nn"
              else pl.BlockSpec((tn, K), lambda j, i: (j, 0), **once))
    return pl.pallas_call(
        body, name=name, grid=grid,
        in_specs=[a_spec, b_spec]
        + [pl.BlockSpec(e.shape, lambda j, i: (0, 0)) if w is None else pl.BlockSpec((tm, w), lambda j, i: (i, j))
           for e, w in extras] + [any_spec] * n_si,
        out_specs=[pl.BlockSpec((tm, w), lambda j, i: (i, j)) for w, _ in outs] + [any_spec] * n_so,
        out_shape=[jax.ShapeDtypeStruct((M, (N // tn) * w), dt) for w, dt in outs] + (side.out_shapes if side else []),
        scratch_shapes=side.scratch if side else [],
        compiler_params=_params(("arbitrary", "arbitrary")),
    )(a, b, *[e for e, _ in extras], *(side.arrays if side else []))


def _f(x):
    return x.astype(F32)


def _silu(x):
    return x * jax.nn.sigmoid(x)


def _softplus(x):
    return jnp.maximum(x, 0.0) + jnp.log1p(jnp.exp(-jnp.abs(x)))


def _rms(x, g):
    r = lax.rsqrt(jnp.mean(x * x, axis=-1, keepdims=True) + NORM_EPS)
    return x * r * g


def _adaln(x, g, scale, shift):
    return _rms(x, g) * (1.0 + scale) + shift


def _resid(x, y, gate, g):
    return x + gate * _rms(y, g)


def _mid(x, y, gate, g_post, g_pre, scale, shift):
    x_new = _resid(x, y, gate, g_post)
    return x_new, _adaln(x_new, g_pre, scale, shift)


def _swiglu(gt, up):
    return _silu(gt) * up


def _ssm_post(y_ssd, pre_xs, z, d_full, norm_w):
    y = (y_ssd + _silu(pre_xs) * d_full) * _silu(z)
    half = SSM_INNER // 2
    parts = []
    for g in range(2):
        yg = y[:, g * half:(g + 1) * half]
        parts.append(yg * lax.rsqrt(jnp.mean(yg * yg, axis=-1, keepdims=True) + NORM_EPS))
    return jnp.concatenate(parts, axis=1) * norm_w


def _first_fwd(x, vecs, side=None):
    return _rowwise("adaln_first", lambda x, g, sc, sh: _adaln(x, g, sc, sh),
                    [(x, D_MODEL, 0)], vecs, [(D_MODEL, BF16)], side=side)


def _mid_bwd(name, x, y, dx_new, dh, vecs):
    def fn(x, y, dxn, dh, gate, g_post, g_pre, scale, shift):
        y, dh = _f(y), _f(dh)
        r_y = lax.rsqrt(jnp.mean(y * y, axis=-1, keepdims=True) + NORM_EPS)
        n_y = y * r_y
        both = gate * g_post
        x_new = x + n_y * both
        r_x = lax.rsqrt(jnp.mean(x_new * x_new, axis=-1, keepdims=True) + NORM_EPS)
        u = x_new * r_x
        col_p = jnp.sum(dh * u, axis=0, keepdims=True)
        du = dh * (g_pre * (1.0 + scale))
        dxt = dxn + r_x * (du - u * jnp.mean(du * u, axis=-1, keepdims=True))
        col_q = jnp.sum(dxt * n_y, axis=0, keepdims=True)
        dn = dxt * both
        dy = r_y * (dn - n_y * jnp.mean(dn * n_y, axis=-1, keepdims=True))
        return (dxt, dy, col_q * g_post, col_q * gate, col_p * (1.0 + scale), col_p * g_pre,
                jnp.sum(dh, axis=0, keepdims=True))

    L = x.shape[0]
    tl = min(512, L)
    n = L // tl
    streams = (x, y, dx_new, dh)

    def body(*refs):
        hbm, vec_refs = refs[:4], refs[4:9]
        dx_o, dy_o, accs = refs[9], refs[10], refs[11:16]
        bufs, sems = refs[16:20], refs[20:24]
        s = pl.program_id(0)

        def copy(k, step):
            rows = pl.ds(pl.multiple_of(step * tl, tl), tl)
            slot = lax.rem(step, RING_SLOTS)
            return pltpu.make_async_copy(hbm[k].at[rows, :], bufs[k].at[slot], sems[k].at[slot])

        @pl.when(s == 0)
        def _():
            for a in accs:
                a[...] = jnp.zeros_like(a)
            for ahead in range(min(RING_SLOTS - 1, n)):
                for k in range(4):
                    copy(k, ahead).start()

        @pl.when(s + RING_SLOTS - 1 < n)
        def _():
            for k in range(4):
                copy(k, s + RING_SLOTS - 1).start()

        for k in range(4):
            copy(k, s).wait()
        slot = lax.rem(s, RING_SLOTS)
        vals = fn(*[b[slot] for b in bufs], *[v[...] for v in vec_refs])
        dx_o[...] = vals[0]
        dy_o[...] = vals[1].astype(dy_o.dtype)
        for a, v in zip(accs, vals[2:]):
            a[...] += v

    any_spec = pl.BlockSpec(memory_space=pl.ANY)
    vec_spec = pl.BlockSpec((1, D_MODEL), lambda i: (0, 0))
    row_spec = pl.BlockSpec((tl, D_MODEL), lambda i: (i, 0))
    return pl.pallas_call(
        body, name=name, grid=(n,), in_specs=[any_spec] * 4 + [vec_spec] * 5, out_specs=[row_spec, row_spec] + [vec_spec] * 5,
        out_shape=[jax.ShapeDtypeStruct((L, D_MODEL), F32), jax.ShapeDtypeStruct((L, D_MODEL), BF16)]
        + [jax.ShapeDtypeStruct((1, D_MODEL), F32)] * 5,
        scratch_shapes=[pltpu.VMEM((RING_SLOTS, tl, D_MODEL), a.dtype) for a in streams]
        + [pltpu.SemaphoreType.DMA((RING_SLOTS,))] * 4,
        compiler_params=_params(("arbitrary",)),
    )(*streams, *vecs)


RING_SLOTS = 3


def _first_bwd(x, dx_in, dh, vecs, side=None):
    def fn(x, dxi, dh, g, scale, shift):
        dh = _f(dh)
        r = lax.rsqrt(jnp.mean(x * x, axis=-1, keepdims=True) + NORM_EPS)
        u = x * r
        col_p = jnp.sum(dh * u, axis=0, keepdims=True)
        du = dh * (g * (1.0 + scale))
        dx = dxi + r * (du - u * jnp.mean(du * u, axis=-1, keepdims=True))
        return dx, col_p * (1.0 + scale), col_p * g, jnp.sum(dh, axis=0, keepdims=True)

    vec = (1, D_MODEL)
    return _rowwise("adaln_first_bwd", fn, [(x, D_MODEL, 0), (dx_in, D_MODEL, 0), (dh, D_MODEL, 0)], vecs,
                    [(D_MODEL, F32)], [vec] * 3, side=side)


def _last_bwd(x1, f, target, vecs):
    def fn(x1, f, t, gate, g):
        f = _f(f)
        r = lax.rsqrt(jnp.mean(f * f, axis=-1, keepdims=True) + NORM_EPS)
        n = f * r
        both = gate * g
        err = x1 + n * both - t
        d = err * (1.0 / D_MODEL)
        col_q = jnp.sum(d * n, axis=0, keepdims=True)
        dn = d * both
        df = r * (dn - n * jnp.mean(dn * n, axis=-1, keepdims=True))
        loss_cols = jnp.sum(err * err, axis=0, keepdims=True) * (0.5 / D_MODEL)
        return d, df, col_q * g, col_q * gate, loss_cols

    vec = (1, D_MODEL)
    return _rowwise("loss_last_bwd", fn, [(x1, D_MODEL, 0), (f, D_MODEL, 0), (target, D_MODEL, 0)], vecs,
                    [(D_MODEL, F32), (D_MODEL, BF16)], [vec] * 3)


HALO = 16


def _shift_down(u, prev, k):
    rows = lax.broadcasted_iota(jnp.int32, u.shape, 0)
    v = pltpu.roll(u, k, 0)
    for t in range(k):
        v = jnp.where(rows == t, prev[HALO - k + t:HALO - k + t + 1, :], v)
    return v


def _shift_up(u, nxt, k):
    n = u.shape[0]
    rows = lax.broadcasted_iota(jnp.int32, u.shape, 0)
    v = pltpu.roll(u, n - k, 0)
    for t in range(k):
        v = jnp.where(rows == n - k + t, nxt[t:t + 1, :], v)
    return v


def _conv_specs(L, tl, width, col_block):
    per = tl // HALO
    last = L // HALO - 1
    main = pl.BlockSpec((tl, width), lambda i: (i, col_block))
    before = pl.BlockSpec((HALO, width), lambda i: (jnp.maximum(i * per - 1, 0), col_block))
    after = pl.BlockSpec((HALO, width), lambda i: (jnp.minimum((i + 1) * per, last), col_block))
    return main, before, after


def _shortconv_fwd(P, w, tl=512):
    L = P.shape[0]
    tl = min(tl, L)
    C = SC_WIDTH
    main, before, _ = _conv_specs(L, tl, 3 * C, 0)

    def body(p_ref, h_ref, w_ref, o_ref):
        first = (pl.program_id(0) == 0)
        p, h = _f(p_ref[...]), _f(h_ref[...])
        b, u = p[:, :C], p[:, C:2 * C] * p[:, 2 * C:]
        uh = jnp.where(first, 0.0, h[:, C:2 * C] * h[:, 2 * C:])
        wv = w_ref[...]
        cv = wv[2:3] * u + wv[1:2] * _shift_down(u, uh, 1) + wv[0:1] * _shift_down(u, uh, 2)
        o_ref[...] = (b * cv).astype(o_ref.dtype)

    return pl.pallas_call(
        body, name="shortconv_fwd", grid=(L // tl,),
        in_specs=[main, before, pl.BlockSpec(w.shape, lambda i: (0, 0))],
        out_specs=pl.BlockSpec((tl, C), lambda i: (i, 0)),
        out_shape=jax.ShapeDtypeStruct((L, C), BF16), compiler_params=_params(("arbitrary",)),
    )(P, P, w)


def _shortconv_bwd(P, dya, w, tl=512):
    L = P.shape[0]
    tl = min(tl, L)
    C = SC_WIDTH
    main, before, after = _conv_specs(L, tl, 3 * C, 0)
    dmain, _, dafter = _conv_specs(L, tl, C, 0)
    n = L // tl

    def body(p_ref, h_ref, n_ref, d_ref, dn_ref, w_ref, o_ref, dw0, dw1, dw2):
        i = pl.program_id(0)
        p, h, nx = _f(p_ref[...]), _f(h_ref[...]), _f(n_ref[...])
        b, c, x = p[:, :C], p[:, C:2 * C], p[:, 2 * C:]
        u = c * x
        uh = jnp.where(i == 0, 0.0, h[:, C:2 * C] * h[:, 2 * C:])
        u1, u2 = _shift_down(u, uh, 1), _shift_down(u, uh, 2)
        wv = w_ref[...]
        cv = wv[2:3] * u + wv[1:2] * u1 + wv[0:1] * u2
        dy = _f(d_ref[...])
        dcv = dy * b
        dcv_n = jnp.where(i == n - 1, 0.0, _f(dn_ref[...]) * nx[:, :C])
        du = wv[2:3] * dcv + wv[1:2] * _shift_up(dcv, dcv_n, 1) + wv[0:1] * _shift_up(dcv, dcv_n, 2)
        o_ref[:, :C] = (dy * cv).astype(o_ref.dtype)
        o_ref[:, C:2 * C] = (du * x).astype(o_ref.dtype)
        o_ref[:, 2 * C:] = (du * c).astype(o_ref.dtype)

        @pl.when(i == 0)
        def _():
            for r in (dw0, dw1, dw2):
                r[...] = jnp.zeros_like(r)

        dw0[...] += jnp.sum(dcv * u2, axis=0, keepdims=True)
        dw1[...] += jnp.sum(dcv * u1, axis=0, keepdims=True)
        dw2[...] += jnp.sum(dcv * u, axis=0, keepdims=True)

    vec = pl.BlockSpec((1, C), lambda i: (0, 0))
    return pl.pallas_call(
        body, name="shortconv_bwd", grid=(n,),
        in_specs=[main, before, after, dmain, dafter, pl.BlockSpec(w.shape, lambda i: (0, 0))],
        out_specs=[pl.BlockSpec((tl, 3 * C), lambda i: (i, 0)), vec, vec, vec],
        out_shape=[jax.ShapeDtypeStruct((L, 3 * C), BF16)] + [jax.ShapeDtypeStruct((1, C), F32)] * 3,
        compiler_params=_params(("arbitrary",)),
    )(P, P, P, dya, dya, w)


def _ssmconv_fwd(P, w, bias, tl=512):
    L = P.shape[0]
    tl = min(tl, L)
    C = SSM_CONV_DIM
    main, before, _ = _conv_specs(L, tl, C, P_XBC // C)

    def body(p_ref, h_ref, w_ref, b_ref, o_ref):
        u = _f(p_ref[...])
        uh = jnp.where(pl.program_id(0) == 0, 0.0, _f(h_ref[...]))
        wv = w_ref[...]
        acc = wv[3:4] * u + b_ref[...]
        for k in range(1, SSM_CONV):
            acc = acc + wv[3 - k:4 - k] * _shift_down(u, uh, k)
        o_ref[...] = acc

    return pl.pallas_call(
        body, name="ssmconv_fwd", grid=(L // tl,),
        in_specs=[main, before, pl.BlockSpec(w.shape, lambda i: (0, 0)), pl.BlockSpec(bias.shape, lambda i: (0, 0))],
        out_specs=pl.BlockSpec((tl, C), lambda i: (i, 0)),
        out_shape=jax.ShapeDtypeStruct((L, C), F32), compiler_params=_params(("arbitrary",)),
    )(P, P, w, bias)


def _ssmconv_bwd(P, dpre, w, tl=512):
    L = P.shape[0]
    tl = min(tl, L)
    C = SSM_CONV_DIM
    main, before, _ = _conv_specs(L, tl, C, P_XBC // C)
    dmain, _, dafter = _conv_specs(L, tl, C, 0)
    n = L // tl

    def body(p_ref, h_ref, d_ref, dn_ref, w_ref, o_ref, dw0, dw1, dw2, dw3, db):
        i = pl.program_id(0)
        u = _f(p_ref[...])
        uh = jnp.where(i == 0, 0.0, _f(h_ref[...]))
        d = d_ref[...]
        dn = jnp.where(i == n - 1, 0.0, dn_ref[...])
        wv = w_ref[...]
        du = wv[3:4] * d
        for k in range(1, SSM_CONV):
            du = du + wv[3 - k:4 - k] * _shift_up(d, dn, k)
        o_ref[...] = du.astype(o_ref.dtype)

        @pl.when(i == 0)
        def _():
            for r in (dw0, dw1, dw2, dw3, db):
                r[...] = jnp.zeros_like(r)

        for k, r in ((3, dw0), (2, dw1), (1, dw2)):
            r[...] += jnp.sum(d * _shift_down(u, uh, k), axis=0, keepdims=True)
        dw3[...] += jnp.sum(d * u, axis=0, keepdims=True)
        db[...] += jnp.sum(d, axis=0, keepdims=True)

    vec = pl.BlockSpec((1, C), lambda i: (0, 0))
    return pl.pallas_call(
        body, name="ssmconv_bwd", grid=(n,),
        in_specs=[main, before, dmain, dafter, pl.BlockSpec(w.shape, lambda i: (0, 0))],
        out_specs=[pl.BlockSpec((tl, C), lambda i: (i, 0))] + [vec] * 5,
        out_shape=[jax.ShapeDtypeStruct((L, C), BF16)] + [jax.ShapeDtypeStruct((1, C), F32)] * 5,
        compiler_params=_params(("arbitrary",)),
    )(P, P, dpre, dpre, w)


def _dot_nt(a, b):
    return lax.dot_general(a, b, (((1,), (1,)), ((), ())), preferred_element_type=F32)


def _dot_tn(a, b):
    return lax.dot_general(a, b, (((0,), (0,)), ((), ())), preferred_element_type=F32)


def _split3(x):
    hi = x.astype(BF16)
    r = x - hi.astype(F32)
    mid = r.astype(BF16)
    return hi, mid, (r - mid.astype(F32)).astype(BF16)


@jax.custom_vjp
def _xm01(x, m):
    return sum(jnp.dot(t, m, preferred_element_type=F32) for t in _split3(x))


def _xm01_fwd(x, m):
    return _xm01(x, m), m


def _xm01_bwd(m, g):
    return sum(_dot_nt(t, m) for t in _split3(g)), jnp.zeros_like(m)


_xm01.defvjp(_xm01_fwd, _xm01_bwd)


@jax.custom_vjp
def _m01x(m, x):
    return sum(jnp.dot(m, t, preferred_element_type=F32) for t in _split3(x))


def _m01x_fwd(m, x):
    return _m01x(m, x), m


def _m01x_bwd(m, g):
    return jnp.zeros_like(m), sum(_dot_tn(m, t) for t in _split3(g))


_m01x.defvjp(_m01x_fwd, _m01x_bwd)


def _ssd_chunk(pre, dtr, s_prev, dtb, alog):
    T = pre.shape[0]
    act = _silu(pre)
    xs, bm, cm = act[:, :SSM_INNER], act[:, SSM_INNER:SSM_INNER + 128], act[:, SSM_INNER + 128:]
    lane = lax.broadcasted_iota(jnp.int32, (1, LANES), 1)
    dt = jnp.where(lane < SSM_HEADS, _softplus(dtr + dtb), 0.0)
    a = dt * (-jnp.exp(alog))
    ri = lax.broadcasted_iota(jnp.int32, (T, T), 0)
    ci = lax.broadcasted_iota(jnp.int32, (T, T), 1)
    causal = ci <= ri
    a_cs = _m01x(causal.astype(BF16), a)
    eh = lax.broadcasted_iota(jnp.int32, (LANES, SSM_INNER), 0)
    ej = lax.broadcasted_iota(jnp.int32, (LANES, SSM_INNER), 1)
    expand = (lax.shift_right_logical(ej, 6) == eh).astype(BF16)
    dt_full = _xm01(dt, expand)
    acs_full = _xm01(a_cs, expand)
    alast_full = acs_full[T - 1:T, :]
    xdt = xs * dt_full
    a_cs_t = a_cs.T
    ys, s_new = [], []
    for g in range(2):
        in_group = lax.shift_right_logical(lane, 6) == g
        cg = jnp.where(in_group, cm, 0.0).astype(BF16)
        bg = jnp.where(in_group, bm, 0.0).astype(BF16)
        scores = _dot_nt(cg, bg)
        for pp in range(2):
            hp = 2 * g + pp
            cols = slice(hp * LANES, (hp + 1) * LANES)
            xp, acsp = xdt[:, cols], acs_full[:, cols]
            per_head = []
            for hh in range(2):
                h = 2 * hp + hh
                decay = jnp.exp(jnp.where(causal, a_cs[:, h:h + 1] - a_cs_t[h:h + 1, :], -jnp.inf))
                per_head.append(jnp.dot((scores * decay).astype(BF16), xp.astype(BF16), preferred_element_type=F32))
            y_diag = jnp.where(lane < SSM_STATE, per_head[0], per_head[1])
            sp = s_prev[hp * LANES:(hp + 1) * LANES, :]
            y_off = jnp.dot(cg, sp.astype(BF16), preferred_element_type=F32) * jnp.exp(acsp)
            ys.append(y_diag + y_off)
            to_end = jnp.exp(alast_full[:, cols] - acsp)
            s_new.append(sp * jnp.exp(alast_full[:, cols]) + _dot_tn(bg, (xp * to_end).astype(BF16)))
    return jnp.concatenate(ys, axis=1), jnp.concatenate(s_new, axis=0)


def _ssd_fwd(pre, P, dtb, alog):
    L = pre.shape[0]
    T = min(SSM_CHUNK, L)
    nc = L // T

    def body(pre_ref, dt_ref, dtb_ref, al_ref, y_ref, st_ref, s_scr):
        @pl.when(pl.program_id(0) == 0)
        def _():
            s_scr[...] = jnp.zeros_like(s_scr)

        st_ref[0] = s_scr[...]
        y, s = _ssd_chunk(pre_ref[...], _f(dt_ref[...]), s_scr[...], dtb_ref[...], al_ref[...])
        y_ref[...] = y
        s_scr[...] = s

    vec = pl.BlockSpec((1, LANES), lambda i: (0, 0))
    return pl.pallas_call(
        body, name="ssd_fwd", grid=(nc,),
        in_specs=[pl.BlockSpec((T, SSM_CONV_DIM), lambda i: (i, 0)), pl.BlockSpec((T, LANES), lambda i: (i, P_DT // LANES)),
                  vec, vec],
        out_specs=[pl.BlockSpec((T, SSM_INNER), lambda i: (i, 0)), pl.BlockSpec((1, 512, LANES), lambda i: (i, 0, 0))],
        out_shape=[jax.ShapeDtypeStruct((L, SSM_INNER), F32), jax.ShapeDtypeStruct((nc, 512, LANES), F32)],
        scratch_shapes=[pltpu.VMEM((512, LANES), F32)], compiler_params=_params(("arbitrary",)),
    )(pre, P, dtb, alog)


def _ssd_bwd(pre, P, states, dy, dxs_extra, dtb, alog):
    L = pre.shape[0]
    T = min(SSM_CHUNK, L)
    nc = L // T

    def body(pre_ref, dt_ref, st_ref, dy_ref, dx_ref, dtb_ref, al_ref, dpre_ref, ddt_ref, ddtb_ref, dal_ref, ds_scr):
        @pl.when(pl.program_id(0) == 0)
        def _():
            ds_scr[...] = jnp.zeros_like(ds_scr)
            ddtb_ref[...] = jnp.zeros_like(ddtb_ref)
            dal_ref[...] = jnp.zeros_like(dal_ref)

        _, vjp = jax.vjp(_ssd_chunk, pre_ref[...], _f(dt_ref[...]), st_ref[0], dtb_ref[...], al_ref[...])
        dpre, ddt, ds, ddtb, dal = vjp((dy_ref[...], ds_scr[...]))
        dpre_ref[:, :SSM_INNER] = dpre[:, :SSM_INNER] + dx_ref[...]
        dpre_ref[:, SSM_INNER:] = dpre[:, SSM_INNER:]
        ddt_ref[:, :LANES] = ddt.astype(ddt_ref.dtype)
        ddt_ref[:, LANES:] = jnp.zeros((T, DT_PAD - LANES), ddt_ref.dtype)
        ds_scr[...] = ds
        ddtb_ref[...] += ddtb
        dal_ref[...] += dal

    vec = pl.BlockSpec((1, LANES), lambda i: (0, 0))
    rev = lambda i: (nc - 1 - i, 0)
    return pl.pallas_call(
        body, name="ssd_bwd", grid=(nc,),
        in_specs=[pl.BlockSpec((T, SSM_CONV_DIM), rev), pl.BlockSpec((T, LANES), lambda i: (nc - 1 - i, P_DT // LANES)),
                  pl.BlockSpec((1, 512, LANES), lambda i: (nc - 1 - i, 0, 0)),
                  pl.BlockSpec((T, SSM_INNER), rev), pl.BlockSpec((T, SSM_INNER), rev), vec, vec],
        out_specs=[pl.BlockSpec((T, SSM_CONV_DIM), rev), pl.BlockSpec((T, DT_PAD), rev), vec, vec],
        out_shape=[jax.ShapeDtypeStruct((L, SSM_CONV_DIM), F32), jax.ShapeDtypeStruct((L, DT_PAD), BF16),
                   jax.ShapeDtypeStruct((1, LANES), F32), jax.ShapeDtypeStruct((1, LANES), F32)],
        scratch_shapes=[pltpu.VMEM((512, LANES), F32)], compiler_params=_params(("arbitrary",)),
    )(pre, P, states, dy, dxs_extra, dtb, alog)


def _sb_scores(qm, kb, later, strict, mask):
    z = _dot_nt(qm, kb)
    lk = jnp.minimum(-z, 0.0) - jnp.log(1.0 + jnp.exp(-jnp.abs(z)))
    if mask is not None:
        lk = jnp.where(mask, lk, 0.0)
    log_a = z + lk + jnp.dot(lk.astype(BF16), strict, preferred_element_type=F32) + later
    if mask is not None:
        log_a = jnp.where(mask, log_a, -jnp.inf)
    return z, lk, log_a


def _dot_split(x, m):
    hi = x.astype(BF16)
    lo = (x - hi.astype(F32)).astype(BF16)
    return jnp.dot(hi, m, preferred_element_type=F32) + jnp.dot(lo, m, preferred_element_type=F32)


def _sb_setup(q_ref, i, tq, tk):
    lane = lax.broadcasted_iota(jnp.int32, (1, LANES), 1)
    first = lane < SB_HEAD_DIM
    q = q_ref[...] * (SB_HEAD_DIM ** -0.5)
    qms = (jnp.where(first, q, jnp.zeros_like(q)), jnp.where(first, jnp.zeros_like(q), q))
    j0 = lax.div(i * tq, tk)
    ri = lax.broadcasted_iota(jnp.int32, (tq, tk), 0)
    ci = lax.broadcasted_iota(jnp.int32, (tq, tk), 1)
    diag_mask = (ci + (j0 * tk - i * tq)) < ri
    kr = lax.broadcasted_iota(jnp.int32, (tk, tk), 0)
    kc = lax.broadcasted_iota(jnp.int32, (tk, tk), 1)
    strict = (kr > kc).astype(BF16)
    return first, qms, j0, diag_mask, strict


def _sb_continue(c):
    return jnp.logical_and(c[0] >= 0, jnp.maximum(jnp.max(c[1][0]), jnp.max(c[1][1])) > SB_LOG_CUTOFF)


def _sb_fwd(P):
    L = P.shape[0]
    tq, tk = min(SB_TQ, L), min(SB_TK, L)
    nq = L // tq
    qb = P_B // LANES

    subs = SB_SUBS if L % (SB_SUBS * tq) == 0 else 1

    def body(q_ref, k_ref, v_ref, o_ref, of_ref):
        zero, zacc = jnp.zeros((tq, 1), F32), jnp.zeros((tq, LANES), F32)
        walks = []
        for s in range(subs):
            rows = pl.ds(s * tq, tq)
            first, qms, j0, diag_mask, strict = _sb_setup(q_ref.at[rows, :], pl.program_id(1) * subs + s, tq, tk)

            def tile(h, j, later, acc, mask=None, valid=None, qms=qms, strict=strict):
                off = pl.multiple_of(j * tk, tk)
                gate = later if valid is None else jnp.where(valid, later, -jnp.inf)
                _, lk, log_a = _sb_scores(qms[h], k_ref[pl.ds(off, tk), :], gate, strict, mask)
                acc = acc + jnp.dot(jnp.exp(log_a).astype(BF16), v_ref[pl.ds(off, tk), :], preferred_element_type=F32)
                total = jnp.sum(lk, axis=1, keepdims=True)
                return later + (total if valid is None else jnp.where(valid, total, 0.0)), acc

            state = []
            for h in range(2):
                carry = tile(h, j0, zero, zacc, mask=diag_mask)
                for n in range(1, SB_STRAIGHT):
                    carry = tile(h, jnp.maximum(j0 - n, 0), *carry, valid=j0 >= n)
                state.append(carry)
            walks.append((rows, first, j0, tile, state))

        for rows, first, j0, tile, state in walks:
            def tail(c, tile=tile):
                res = [tile(h, c[0], c[1][h], c[2][h]) for h in range(2)]
                return c[0] - 1, (res[0][0], res[1][0]), (res[0][1], res[1][1])

            _, _, accs = lax.while_loop(
                _sb_continue, tail, (j0 - SB_STRAIGHT, (state[0][0], state[1][0]), (state[0][1], state[1][1])))
            out = jnp.where(first, accs[0], accs[1])
            o_ref[rows, :] = out.astype(o_ref.dtype)
            of_ref[rows, :] = out

    nq = nq // subs
    tile_spec = pl.BlockSpec((subs * tq, LANES), lambda p, i: (i, p))
    return pl.pallas_call(
        body, name="sb_fwd", grid=(2, nq),
        in_specs=[pl.BlockSpec((subs * tq, LANES), lambda p, i: (i, qb + p)),
                  pl.BlockSpec((L, LANES), lambda p, i: (0, qb + 2 + p)),
                  pl.BlockSpec((L, LANES), lambda p, i: (0, qb + 4 + p))],
        out_specs=[tile_spec, tile_spec],
        out_shape=[jax.ShapeDtypeStruct((L, 2 * LANES), BF16), jax.ShapeDtypeStruct((L, 2 * LANES), F32)],
        compiler_params=_params(("arbitrary", "arbitrary")),
    )(P, P, P)


def _sb_bwd(P, dyb, yb32):
    L = P.shape[0]
    tq, tk = min(SB_TQ, L), min(SB_TK, L)
    nq = L // tq
    qb = P_B // LANES

    subs = SB_SUBS if L % (SB_SUBS * tq) == 0 else 1

    def body(q_ref, k_ref, v_ref, do_ref, of_ref, dq_ref, dk_ref, dv_ref):
        @pl.when(pl.program_id(1) == 0)
        def _():
            dk_ref[...] = jnp.zeros_like(dk_ref)
            dv_ref[...] = jnp.zeros_like(dv_ref)

        tails = [walk(pl.ds(s * tq, tq), pl.program_id(1) * subs + s, q_ref, k_ref, v_ref, do_ref, of_ref, dq_ref,
                      dk_ref, dv_ref) for s in range(subs)]
        for finish in tails:
            finish()

    def walk(rows, i, q_ref, k_ref, v_ref, do_ref, of_ref, dq_ref, dk_ref, dv_ref):
        first, qms, j0, diag_mask, strict = _sb_setup(q_ref.at[rows, :], i, tq, tk)
        do = do_ref[rows, :]
        doms = (jnp.where(first, do, jnp.zeros_like(do)), jnp.where(first, jnp.zeros_like(do), do))
        prod = _f(do) * of_ref[rows, :]
        totals = (jnp.sum(jnp.where(first, prod, 0.0), axis=1, keepdims=True),
                  jnp.sum(jnp.where(first, 0.0, prod), axis=1, keepdims=True))

        def tile(h, j, later, later_g, acc, mask=None, valid=None):
            off = pl.multiple_of(j * tk, tk)
            kb, vb = k_ref[pl.ds(off, tk), :], v_ref[pl.ds(off, tk), :]
            gate = later if valid is None else jnp.where(valid, later, -jnp.inf)
            z, lk, log_a = _sb_scores(qms[h], kb, gate, strict, mask)
            att = jnp.exp(log_a).astype(BF16)
            g = _f(att) * _dot_nt(doms[h], vb)
            before = totals[h] - later_g
            if valid is not None:
                before = jnp.where(valid, before, 0.0)
            dz = g - (before - _dot_split(g, strict)) * jnp.exp(z + lk)
            if mask is not None:
                dz = jnp.where(mask, dz, 0.0)
            dzb = dz.astype(BF16)
            rows = jnp.sum(lk, axis=1, keepdims=True)
            carry = (later + (rows if valid is None else jnp.where(valid, rows, 0.0)),
                     later_g + jnp.sum(g, axis=1, keepdims=True), acc + jnp.dot(dzb, kb, preferred_element_type=F32))
            return carry, _dot_tn(dzb, qms[h]), _dot_tn(att, doms[h])

        def tail(c):
            off = pl.multiple_of(c[0] * tk, tk)
            (c0, dk0, dv0), (c1, dk1, dv1) = [tile(h, c[0], c[1][h], c[2][h], c[3][h]) for h in range(2)]
            dk_ref[pl.ds(off, tk), :] += dk0 + dk1
            dv_ref[pl.ds(off, tk), :] += dv0 + dv1
            return (c[0] - 1,) + tuple(zip(c0, c1))

        zero, zacc = jnp.zeros((tq, 1), F32), jnp.zeros((tq, LANES), F32)
        blocks = [j0] + [jnp.maximum(j0 - n, 0) for n in range(1, SB_STRAIGHT)]
        carries, dks, dvs = [], [], []
        for h in range(2):
            carry, dk, dv = tile(h, j0, zero, zero, zacc, mask=diag_mask)
            dks.append([dk])
            dvs.append([dv])
            for n in range(1, SB_STRAIGHT):
                carry, dk, dv = tile(h, blocks[n], *carry, valid=j0 >= n)
                dks[h].append(dk)
                dvs[h].append(dv)
            carries.append(carry)
        for n, j in enumerate(blocks):
            off = pl.multiple_of(j * tk, tk)
            dk_ref[pl.ds(off, tk), :] += dks[0][n] + dks[1][n]
            dv_ref[pl.ds(off, tk), :] += dvs[0][n] + dvs[1][n]
        def finish():
            accs = lax.while_loop(_sb_continue, tail, (j0 - SB_STRAIGHT,) + tuple(zip(carries[0], carries[1])))[3]
            dq_ref[rows, :] = jnp.where(first, accs[0], accs[1]) * (SB_HEAD_DIM ** -0.5)

        return finish

    nq = nq // subs
    full = pl.BlockSpec((L, LANES), lambda p, i: (0, p))
    tile_spec = pl.BlockSpec((subs * tq, LANES), lambda p, i: (i, p))
    return pl.pallas_call(
        body, name="sb_bwd", grid=(2, nq),
        in_specs=[pl.BlockSpec((subs * tq, LANES), lambda p, i: (i, qb + p)),
                  pl.BlockSpec((L, LANES), lambda p, i: (0, qb + 2 + p), pipeline_mode=pl.Buffered(1)),
                  pl.BlockSpec((L, LANES), lambda p, i: (0, qb + 4 + p), pipeline_mode=pl.Buffered(1)), tile_spec, tile_spec],
        out_specs=[tile_spec, full, full],
        out_shape=[jax.ShapeDtypeStruct((L, 2 * LANES), F32)] * 3,
        compiler_params=_params(("arbitrary", "arbitrary")),
    )(P, P, P, dyb, yb32)


MOD_SHARD = N_MOD * D_MODEL // N_CHIPS


def _mod_fwd(c_all, mod_w, mod_b_sh):
    tn = 512

    def body(c_ref, w_ref, b_ref, o_ref):
        o_ref[0] = jnp.dot(_silu(c_ref[...]), w_ref[0], precision=HIGHEST, preferred_element_type=F32) + b_ref[0]

    return pl.pallas_call(
        body, name="mod_fwd", grid=(DEPTH, MOD_SHARD // tn),
        in_specs=[pl.BlockSpec((N_DEV, D_MODEL), lambda l, j: (0, 0)),
                  pl.BlockSpec((1, D_MODEL, tn), lambda l, j: (l, 0, j)),
                  pl.BlockSpec((1, 1, tn), lambda l, j: (l, 0, j))],
        out_specs=pl.BlockSpec((1, N_DEV, tn), lambda l, j: (l, 0, j)),
        out_shape=jax.ShapeDtypeStruct((DEPTH, N_DEV, MOD_SHARD), F32),
        compiler_params=_params(("arbitrary", "arbitrary")),
    )(c_all, mod_w, mod_b_sh)


def _mod_bwd(c_all, dmod_sh):
    tn = 512

    def body(c_ref, d_ref, o_ref):
        o_ref[0] = lax.dot_general(_silu(c_ref[...]), d_ref[0], (((0,), (0,)), ((), ())), precision=HIGHEST,
                                   preferred_element_type=F32)

    return pl.pallas_call(
        body, name="mod_bwd", grid=(DEPTH, MOD_SHARD // tn),
        in_specs=[pl.BlockSpec((N_DEV, D_MODEL), lambda l, j: (0, 0)),
                  pl.BlockSpec((1, N_DEV, tn), lambda l, j: (l, 0, j))],
        out_specs=pl.BlockSpec((1, D_MODEL, tn), lambda l, j: (l, 0, j)),
        out_shape=jax.ShapeDtypeStruct((DEPTH, D_MODEL, MOD_SHARD), F32),
        compiler_params=_params(("arbitrary", "arbitrary")),
    )(c_all, dmod_sh)


def _row_tile(rows, cap):
    if rows <= cap:
        return rows
    best = None
    for t in range(8, cap + 1, 8):
        if rows % t == 0:
            best = t
    assert best is not None, (rows, cap)
    return best


def _adamw(name, w, gs, m, v, tr=256):
    R, W = w.shape
    by_layer = any(isinstance(t, tuple) for t in gs)
    tr = _row_tile(R // 2 if by_layer else R, tr)
    per = (R // 2) // tr

    flat, specs = [], []
    for t in gs:
        if isinstance(t, tuple):
            flat += list(t)
            specs += [pl.BlockSpec((tr, W), lambda i: (jnp.minimum(i, per - 1), 0)),
                      pl.BlockSpec((tr, W), lambda i: (jnp.maximum(i - per, 0), 0))]
        else:
            flat.append(t)
            specs.append(pl.BlockSpec((tr, W), lambda i: (i, 0)))
    ng = len(flat)

    def body(*refs):
        w_ref, g_refs, (m_ref, v_ref) = refs[0], list(refs[1:1 + ng]), refs[1 + ng:3 + ng]
        g_out, d_out, m_out, v_out = refs[3 + ng:]
        g = None
        for t in gs:
            if isinstance(t, tuple):
                lo, hi = g_refs.pop(0), g_refs.pop(0)
                term = jnp.where(pl.program_id(0) < per, lo[...], hi[...])
            else:
                term = g_refs.pop(0)[...]
            g = term if g is None else g + term
        mm = ADAM_B1 * m_ref[...] + (1.0 - ADAM_B1) * g
        vv = ADAM_B2 * v_ref[...] + (1.0 - ADAM_B2) * (g * g)
        m_hat = mm / (1.0 - ADAM_B1 ** ADAM_STEP)
        v_hat = vv / (1.0 - ADAM_B2 ** ADAM_STEP)
        g_out[...] = g
        d_out[...] = -ADAM_LR * (m_hat / (jnp.sqrt(v_hat) + ADAM_EPS) + ADAM_WD * w_ref[...])
        m_out[...] = mm
        v_out[...] = vv

    spec = pl.BlockSpec((tr, W), lambda i: (i, 0))
    return pl.pallas_call(
        body, name=name, grid=(R // tr,), in_specs=[spec] + specs + [spec, spec], out_specs=[spec] * 4,
        out_shape=[jax.ShapeDtypeStruct((R, W), F32)] * 4, compiler_params=_params(("arbitrary",)),
    )(w, *flat, m, v)


def _sum_slots(name, a, tr=256):
    n, R, W = a.shape
    tr = _row_tile(R, tr)

    def body(a_ref, o_ref):
        acc = _f(a_ref[0])
        for j in range(1, n):
            acc = acc + _f(a_ref[j])
        o_ref[...] = acc

    return pl.pallas_call(
        body, name=name, grid=(R // tr,), in_specs=[pl.BlockSpec((n, tr, W), lambda i: (0, i, 0))],
        out_specs=pl.BlockSpec((tr, W), lambda i: (i, 0)), out_shape=jax.ShapeDtypeStruct((R, W), F32),
        compiler_params=_params(("arbitrary",)),
    )(a)


def _here():
    return lax.axis_index("x"), lax.axis_index("y"), lax.axis_index("c")


def _flip(v, d):
    return 1 - v if d else v


def _allgather_small(name, buf):
    R = buf.shape[0]
    rel = [(dx, dy, dc) for dx in (0, 1) for dy in (0, 1) for dc in (0, 1)][1:]

    def body(x_ref, o_ref, send, recv, lsem):
        x, y, c = _here()
        me = 4 * x + 2 * y + c
        mine = pltpu.make_async_copy(x_ref, o_ref.at[me], lsem)
        mine.start()

        def copy(k, slot):
            dx, dy, dc = rel[k]
            return pltpu.make_async_remote_copy(
                src_ref=x_ref, dst_ref=o_ref.at[slot], send_sem=send.at[k], recv_sem=recv.at[k],
                device_id=(_flip(x, dx), _flip(y, dy), _flip(c, dc)), device_id_type=MESH_ID)

        sent = [copy(k, me) for k in range(len(rel))]
        for cp in sent:
            cp.start()
        for k, (dx, dy, dc) in enumerate(rel):
            copy(k, 4 * _flip(x, dx) + 2 * _flip(y, dy) + _flip(c, dc)).wait_recv()
        for cp in sent:
            cp.wait_send()
        mine.wait()

    return pl.pallas_call(
        body, name=name, out_shape=jax.ShapeDtypeStruct((N_DEV, R, LANES), F32),
        in_specs=[pl.BlockSpec(memory_space=pltpu.VMEM)], out_specs=pl.BlockSpec(memory_space=pltpu.VMEM),
        scratch_shapes=[pltpu.SemaphoreType.DMA((7,)), pltpu.SemaphoreType.DMA((7,)), pltpu.SemaphoreType.DMA],
    )(buf)


CHIP_REL = [(1, 0), (0, 1), (1, 1)]


class _Side:
    def __init__(self, arrays, out_shapes, scratch, start, finish):
        self.arrays, self.out_shapes, self.scratch, self.start, self.finish = arrays, out_shapes, scratch, start, finish


def _chip_of(k):
    x, y, _ = _here()
    dx, dy = CHIP_REL[k]
    return _flip(x, dx), _flip(y, dy)


def _scatter_side(arrays):
    n = len(arrays)

    def parts(ins, outs, sems):
        send, recv, lsem = sems
        x, y, c = _here()
        s = 2 * x + y

        def copy(w, k, mine):
            px, py = _chip_of(k)
            return pltpu.make_async_remote_copy(
                src_ref=ins[w].at[2 * px + py], dst_ref=outs[w].at[s if mine else 2 * px + py],
                send_sem=send.at[3 * w + k], recv_sem=recv.at[3 * w + k], device_id=(px, py, c), device_id_type=MESH_ID)

        local = [pltpu.make_async_copy(ins[w].at[s], outs[w].at[s], lsem.at[w]) for w in range(n)]
        return copy, local

    def start(ins, outs, sems):
        copy, local = parts(ins, outs, sems)
        for cp in local:
            cp.start()
        for w in range(n):
            for k in range(3):
                copy(w, k, True).start()

    def finish(ins, outs, sems):
        copy, local = parts(ins, outs, sems)
        for w in range(n):
            for k in range(3):
                copy(w, k, False).wait_recv()
        for w in range(n):
            for k in range(3):
                copy(w, k, True).wait_send()
        for cp in local:
            cp.wait()

    scratch = [pltpu.SemaphoreType.DMA((3 * n,)), pltpu.SemaphoreType.DMA((3 * n,)), pltpu.SemaphoreType.DMA((n,))]
    return _Side(arrays, [jax.ShapeDtypeStruct(a.shape, a.dtype) for a in arrays], scratch, start, finish)


def _gather_side(shards):
    n = len(shards)

    def parts(ins, outs, sems):
        send, recv, fsend, frecv, lsem = sems
        x, y, c = _here()
        s = 2 * x + y

        def half(ref, w, which):
            rows = shards[w].shape[0] // 2
            return ref.at[pl.ds(pl.multiple_of(which * rows, 16), rows)]

        def over_ici(w, k, mine):
            px, py = _chip_of(k)
            return pltpu.make_async_remote_copy(
                src_ref=half(ins[w], w, c), dst_ref=half(outs[w].at[s if mine else 2 * px + py], w, c),
                send_sem=send.at[3 * w + k], recv_sem=recv.at[3 * w + k], device_id=(px, py, c), device_id_type=MESH_ID)

        def to_sibling(w, k, which):
            px, py = _chip_of(k)
            part = half(outs[w].at[2 * px + py], w, which)
            return pltpu.make_async_remote_copy(
                src_ref=part, dst_ref=part, send_sem=fsend.at[3 * w + k], recv_sem=frecv.at[3 * w + k],
                device_id=(x, y, 1 - c), device_id_type=MESH_ID)

        local = [pltpu.make_async_copy(ins[w], outs[w].at[s], lsem.at[w]) for w in range(n)]
        return c, over_ici, to_sibling, local

    def start(ins, outs, sems):
        _, over_ici, _, local = parts(ins, outs, sems)
        for cp in local:
            cp.start()
        for w in range(n):
            for k in range(3):
                over_ici(w, k, True).start()

    def finish(ins, outs, sems):
        c, over_ici, to_sibling, local = parts(ins, outs, sems)
        for w in range(n):
            for k in range(3):
                over_ici(w, k, False).wait_recv()
                to_sibling(w, k, c).start()
        for w in range(n):
            for k in range(3):
                to_sibling(w, k, 1 - c).wait_recv()
        for w in range(n):
            for k in range(3):
                over_ici(w, k, True).wait_send()
                to_sibling(w, k, c).wait_send()
        for cp in local:
            cp.wait()

    scratch = [pltpu.SemaphoreType.DMA((3 * n,))] * 4 + [pltpu.SemaphoreType.DMA((n,))]
    return _Side(shards, [jax.ShapeDtypeStruct((N_CHIPS,) + a.shape, a.dtype) for a in shards], scratch, start, finish)


def _sibling_exchange(name, arrays):
    n = len(arrays)

    def body(*refs):
        ins, outs = refs[:n], refs[n:2 * n]
        send, recv = refs[2 * n:]
        x, y, c = _here()
        cps = [pltpu.make_async_remote_copy(src_ref=ins[w], dst_ref=outs[w], send_sem=send.at[w], recv_sem=recv.at[w],
                                            device_id=(x, y, 1 - c), device_id_type=MESH_ID) for w in range(n)]
        for cp in cps:
            cp.start()
        for cp in cps:
            cp.wait()

    any_spec = pl.BlockSpec(memory_space=pl.ANY)
    return pl.pallas_call(
        body, name=name, out_shape=[jax.ShapeDtypeStruct(a.shape, a.dtype) for a in arrays],
        in_specs=[any_spec] * n, out_specs=[any_spec] * n,
        scratch_shapes=[pltpu.SemaphoreType.DMA((n,)), pltpu.SemaphoreType.DMA((n,))],
    )(*arrays)


def _pack(arrs):
    flat = jnp.concatenate([a.reshape(-1).astype(F32) for a in arrs])
    n = flat.shape[0]
    rows = -(-n // (8 * LANES)) * 8
    return jnp.pad(flat, (0, rows * LANES - n)).reshape(rows, LANES)


def _unpack(buf, shapes):
    lead = buf.shape[:-2]
    flat = buf.reshape(lead + (-1,))
    out, off = [], 0
    for s in shapes:
        n = 1
        for d in s:
            n *= d
        out.append(flat[..., off:off + n].reshape(lead + tuple(s)))
        off += n
    return out


def _pad_w_in(w):
    return jnp.concatenate([w[:, :2048], w[:, 2816:2824], jnp.zeros((w.shape[0], P_XBC - P_DT - 8), w.dtype),
                            w[:, 2048:2816], w[:, 2824:]], axis=1)


def _unpad_w_in(g):
    return jnp.concatenate([g[:, :P_DT], g[:, P_XBC:P_G], g[:, P_DT:P_DT + 8], g[:, P_G:]], axis=1)


FFN_HALF = FFN_HIDDEN // 2


def _ffn_in_cols(w):
    h = FFN_HALF
    return jnp.concatenate([w[:, :h], w[:, 2 * h:3 * h], w[:, h:2 * h], w[:, 3 * h:]], axis=1)


def _row(v):
    return v.reshape(1, -1)


BIG = (("w_in", 2), ("w_sc_out", 2), ("w_sb_out", 2), ("w_ssm_out", 2), ("w_o", 1), ("w_ffn_in", 2), ("w_ffn_out", 1))
SMALL = ("mod_b", "g_pre_mix", "g_post_mix", "g_pre_ffn", "g_post_ffn", "sc_conv_w", "ssm_conv_w", "ssm_conv_b",
         "ssm_dt_bias", "ssm_a_log", "ssm_d", "ssm_norm_w")
WEIGHT_ORDER = ("mod_w", "mod_b", "g_pre_mix", "g_post_mix", "g_pre_ffn", "g_post_ffn", "w_in", "sc_conv_w",
                "ssm_conv_w", "ssm_conv_b", "ssm_dt_bias", "ssm_a_log", "ssm_d", "ssm_norm_w", "w_sc_out", "w_sb_out",
                "w_ssm_out", "w_o", "w_ffn_in", "w_ffn_out")


def _mm_mid(name, a, w, x, vecs):
    return _mm_epi(name, a, w, "nn", D_MODEL, [(x, D_MODEL)] + [(v, None) for v in vecs],
                   lambda p, x, *v: (p,) + tuple(_mid(x, p, *v)), [(D_MODEL, BF16), (D_MODEL, F32), (D_MODEL, BF16)])


def _layer_fwd(l, x_in, h, W, V, sides, next_vecs):
    S = {"x_in": x_in, "h": h}
    side, handler = sides.get("in_proj", (None, None))
    P = _mm(f"in_proj{l}", h, W["w_in"], "nn", BF16, tm=2048, tn_cap=1024, side=side)
    if side:
        handler(P[1:])
        P = P[0]
    S["P"] = P
    S["ya"] = _shortconv_fwd(P, V["sc_w"])
    S["yb"], S["yb32"] = _sb_fwd(P)
    S["pre"] = _ssmconv_fwd(P, V["ssm_w"], V["ssm_b"])
    S["y_ssd"], S["states"] = _ssd_fwd(S["pre"], P, V["dtb"], V["alog"])
    S["yc"] = _rowwise(f"ssm_post{l}", lambda y, px, z, d, nw: _ssm_post(y, px, _f(z), d, nw),
                       [(S["y_ssd"], SSM_INNER, 0), (S["pre"], SSM_INNER, 0), (P, SSM_INNER, P_Z // SSM_INNER)],
                       [V["d_full"], V["norm_w"]], [(SSM_INNER, BF16)])[0]
    S["merged"] = _merge_fwd(f"merge{l}", P, [S["ya"], S["yb"], S["yc"]],
                             [W["w_sc_out"], W["w_sb_out"], W["w_ssm_out"]])
    S["mix"], S["x1"], S["h2"] = _mm_mid(f"w_o{l}", S["merged"], W["w_o"], x_in, V["mid_mix"])
    side, handler = sides.get("ffn_in", (None, None))
    res = _mm_epi(f"ffn_in{l}", S["h2"], W["w_ffn_in"], "nn", 2 * FFN_HALF, [],
                  lambda p: (p, _swiglu(p[:, :FFN_HALF], p[:, FFN_HALF:])),
                  [(2 * FFN_HALF, BF16), (FFN_HALF, BF16)], side=side)
    S["GU"], S["act"] = res[0], res[1]
    if side:
        handler(res[2:])
    if next_vecs is None:
        S["f"] = _mm(f"ffn_out{l}", S["act"], W["w_ffn_out"], "nn", BF16)
    else:
        S["f"], S["x_next"], S["h_next"] = _mm_mid(f"ffn_out{l}", S["act"], W["w_ffn_out"], S["x1"], next_vecs)
    return S


BRANCH_WIDTHS = (SC_WIDTH, 256, SSM_INNER)


def _branch_specs(tm):
    gb = P_G // D_MODEL
    gates = [pl.BlockSpec((tm, D_MODEL), functools.partial(lambda i, cb: (i, cb), cb=gb + k)) for k in range(3)]
    ys = [pl.BlockSpec((tm, w), lambda i: (i, 0)) for w in BRANCH_WIDTHS]
    ws = [pl.BlockSpec((w, D_MODEL), lambda i: (0, 0)) for w in BRANCH_WIDTHS]
    return gates, ys, ws


def _merge_fwd(name, P, ys, ws, tm=512):
    L = P.shape[0]
    tm = min(tm, L)
    gates, y_specs, w_specs = _branch_specs(tm)

    def body(ga, gb, gc, ya, yb, yc, wa, wb, wc, o_ref):
        acc = None
        for g_ref, y_ref, w_ref in ((ga, ya, wa), (gb, yb, wb), (gc, yc, wc)):
            t = jax.nn.sigmoid(_f(g_ref[...])) * jnp.dot(y_ref[...], w_ref[...], preferred_element_type=F32)
            acc = t if acc is None else acc + t
        o_ref[...] = acc.astype(o_ref.dtype)

    return pl.pallas_call(
        body, name=name, grid=(L // tm,), in_specs=gates + y_specs + w_specs,
        out_specs=pl.BlockSpec((tm, D_MODEL), lambda i: (i, 0)), out_shape=jax.ShapeDtypeStruct((L, D_MODEL), BF16),
        compiler_params=_params(("arbitrary",)),
    )(P, P, P, *ys, *ws)


def _merge_bwd(name, P, ys, ws, dmerged, tm=512):
    L = P.shape[0]
    tm = min(tm, L)
    gates, y_specs, w_specs = _branch_specs(tm)

    def body(ga, gb, gc, ya, yb, yc, wa, wb, wc, dm_ref, dg_ref, dya, dyb, dyc, gwa, gwb, gwc):
        @pl.when(pl.program_id(0) == 0)
        def _():
            for r in (gwa, gwb, gwc):
                r[...] = jnp.zeros_like(r)

        dm = _f(dm_ref[...])
        for k, (g_ref, y_ref, w_ref, dy_ref, gw_ref) in enumerate(
                ((ga, ya, wa, dya, gwa), (gb, yb, wb, dyb, gwb), (gc, yc, wc, dyc, gwc))):
            y, w = y_ref[...], w_ref[...]
            s = jax.nn.sigmoid(_f(g_ref[...]))
            proj = jnp.dot(y, w, preferred_element_type=F32)
            d_proj = (dm * s).astype(BF16)
            dg_ref[:, k * D_MODEL:(k + 1) * D_MODEL] = (dm * proj * s * (1.0 - s)).astype(dg_ref.dtype)
            dy_ref[...] = _dot_nt(d_proj, w).astype(dy_ref.dtype)
            gw_ref[...] += _dot_tn(y, d_proj)

    gate_cols = pl.BlockSpec((tm, P_WIDTH - P_G), lambda i: (i, P_G // (P_WIDTH - P_G)))
    return pl.pallas_call(
        body, name=name, grid=(L // tm,),
        in_specs=gates + y_specs + w_specs + [pl.BlockSpec((tm, D_MODEL), lambda i: (i, 0))],
        out_specs=[gate_cols] + y_specs + w_specs,
        out_shape=[jax.ShapeDtypeStruct((L, P_WIDTH), BF16)] + [jax.ShapeDtypeStruct((L, w), BF16) for w in BRANCH_WIDTHS]
        + [jax.ShapeDtypeStruct((w, D_MODEL), F32) for w in BRANCH_WIDTHS],
        compiler_params=_params(("arbitrary",)),
    )(P, P, P, *ys, *ws, dmerged)


def _assemble_dp(name, dP, parts, tl=512):
    L = dP.shape[0]
    tl = min(tl, L)
    n = len(parts)

    def body(*refs):
        o_ref = refs[n + 1]
        o_ref[...] = jnp.concatenate([r[...].astype(o_ref.dtype) for r in refs[:n]], axis=1)

    return pl.pallas_call(
        body, name=name, grid=(L // tl,),
        in_specs=[pl.BlockSpec((tl, a.shape[1]), lambda i: (i, 0)) for a in parts] + [pl.BlockSpec(memory_space=pl.ANY)],
        out_specs=pl.BlockSpec((tl, P_G), lambda i: (i, 0)), out_shape=jax.ShapeDtypeStruct(dP.shape, dP.dtype),
        input_output_aliases={n: 0}, compiler_params=_params(("arbitrary",)),
    )(*parts, dP)


def _layer_bwd(l, S, W, V, dx1, df, sides, landed):
    G = {}
    P = S["P"]

    def mm(key, *args, **kw):
        if key not in sides:
            return _mm(f"{key}{l}", *args, **kw)
        names, layer, make = sides[key]
        res = _mm(f"{key}{l}", *args, side=make(G), **kw)
        for n, a in zip(names, res[1:]):
            landed[(n, layer)] = a
        return res[0]

    G["w_ffn_out"] = _mm(f"gw_ffn_out{l}", S["act"], df, "tn", F32)

    def swiglu_bwd(d_act, gu):
        gt, up = _f(gu[:, :FFN_HALF]), _f(gu[:, FFN_HALF:])
        s = jax.nn.sigmoid(gt)
        gs = gt * s
        return (jnp.concatenate([d_act * up * (s + gs * (1.0 - s)), d_act * gs], axis=1),)

    names, layer, make = sides.get("d_gu", ((), None, None))
    res = _mm_epi(f"d_gu{l}", df, W["w_ffn_out"], "nt", FFN_HALF, [(S["GU"], 2 * FFN_HALF)], swiglu_bwd,
                  [(2 * FFN_HALF, BF16)], side=make(G) if make else None)
    dGU = res[0]
    for n, a in zip(names, res[1:]):
        landed[(n, layer)] = a
    dh2 = mm("d_h2", dGU, W["w_ffn_in"], "nt", BF16)
    G["w_ffn_in"] = _ffn_in_cols(mm("gw_ffn_in", S["h2"], dGU, "tn", F32))
    dx, dmix, G["gate1"], G["g_post_mix"], G["g_pre_ffn"], G["scale2"], G["shift2"] = _mid_bwd(
        f"mid_mix_bwd{l}", S["x_in"], S["mix"], dx1, dh2, V["mid_mix"])
    dmerged = _mm(f"d_merged{l}", dmix, W["w_o"], "nt", BF16)
    G["w_o"] = _mm(f"gw_o{l}", S["merged"], dmix, "tn", F32)

    dP, dya, dyb, dyc, G["w_sc_out"], G["w_sb_out"], G["w_ssm_out"] = _merge_bwd(
        f"merge_bwd{l}", P, [S["ya"], S["yb"], S["yc"]], [W["w_sc_out"], W["w_sb_out"], W["w_ssm_out"]], dmerged)

    def post_bwd(y_ssd, px, z, d, dfull, nw):
        z, d = _f(z), _f(d)
        sx, sz = jax.nn.sigmoid(px), jax.nn.sigmoid(z)
        xs, gz = px * sx, z * sz
        t = y_ssd + xs * dfull
        y = t * gz
        dn = d * nw
        half = SSM_INNER // 2
        ns, dys = [], []
        for g in range(2):
            yg, dng = y[:, g * half:(g + 1) * half], dn[:, g * half:(g + 1) * half]
            r = lax.rsqrt(jnp.mean(yg * yg, axis=-1, keepdims=True) + NORM_EPS)
            ns.append(yg * r)
            dys.append(r * (dng - ns[g] * jnp.mean(dng * ns[g], axis=-1, keepdims=True)))
        n, dy = jnp.concatenate(ns, axis=1), jnp.concatenate(dys, axis=1)
        dt = dy * gz
        return (dt, dt * dfull * (sx + xs * (1.0 - sx)), dy * t * (sz + gz * (1.0 - sz)),
                jnp.sum(dt * xs, axis=0, keepdims=True), jnp.sum(d * n, axis=0, keepdims=True))

    dy_ssd, dxs, dz, G["d_full"], G["ssm_norm_w"] = _rowwise(
        f"ssm_post_bwd{l}", post_bwd,
        [(S["y_ssd"], SSM_INNER, 0), (S["pre"], SSM_INNER, 0), (P, SSM_INNER, P_Z // SSM_INNER), (dyc, SSM_INNER, 0)],
        [V["d_full"], V["norm_w"]], [(SSM_INNER, F32), (SSM_INNER, F32), (SSM_INNER, BF16)], [(1, SSM_INNER)] * 2)
    dpre, ddt, G["dtb"], G["alog"] = _ssd_bwd(S["pre"], P, S["states"], dy_ssd, dxs, V["dtb"], V["alog"])
    dxbc, w0, w1, w2, w3, G["ssm_conv_b"] = _ssmconv_bwd(P, dpre, V["ssm_w"])
    G["ssm_conv_w"] = jnp.concatenate([w0, w1, w2, w3], axis=0)
    dq, dk, dv = _sb_bwd(P, dyb, S["yb32"])
    dA, s0, s1, s2 = _shortconv_bwd(P, dya, V["sc_w"])
    G["sc_conv_w"] = jnp.concatenate([s0, s1, s2], axis=0)
    dP = _assemble_dp(f"assemble_dp{l}", dP, [dA, dq, dk, dv, dz, ddt, dxbc])
    G["w_in"] = _mm(f"gw_in{l}", S["h"], dP, "tn", F32, tn_cap=1024)
    dh = mm("d_h", dP, W["w_in"], "nt", BF16, tk_cap=3072)
    return dx, dh, G


def kernel(x, c, mod_w, mod_b, g_pre_mix, g_post_mix, g_pre_ffn, g_post_ffn, w_in, sc_conv_w, ssm_conv_w, ssm_conv_b, ssm_dt_bias, ssm_a_log, ssm_d, ssm_norm_w, w_sc_out, w_sb_out, w_ssm_out, w_o, w_ffn_in, w_ffn_out, loss_target, m_mod_w, m_mod_b, m_g_pre_mix, m_g_post_mix, m_g_pre_ffn, m_g_post_ffn, m_w_in, m_sc_conv_w, m_ssm_conv_w, m_ssm_conv_b, m_ssm_dt_bias, m_ssm_a_log, m_ssm_d, m_ssm_norm_w, m_w_sc_out, m_w_sb_out, m_w_ssm_out, m_w_o, m_w_ffn_in, m_w_ffn_out, v_mod_w, v_mod_b, v_g_pre_mix, v_g_post_mix, v_g_pre_ffn, v_g_post_ffn, v_w_in, v_sc_conv_w, v_ssm_conv_w, v_ssm_conv_b, v_ssm_dt_bias, v_ssm_a_log, v_ssm_d, v_ssm_norm_w, v_w_sc_out, v_w_sb_out, v_w_ssm_out, v_w_o, v_w_ffn_in, v_w_ffn_out):
    wts = dict(mod_w=mod_w, mod_b=mod_b, g_pre_mix=g_pre_mix, g_post_mix=g_post_mix, g_pre_ffn=g_pre_ffn,
               g_post_ffn=g_post_ffn, w_in=w_in, sc_conv_w=sc_conv_w, ssm_conv_w=ssm_conv_w, ssm_conv_b=ssm_conv_b,
               ssm_dt_bias=ssm_dt_bias, ssm_a_log=ssm_a_log, ssm_d=ssm_d, ssm_norm_w=ssm_norm_w, w_sc_out=w_sc_out,
               w_sb_out=w_sb_out, w_ssm_out=w_ssm_out, w_o=w_o, w_ffn_in=w_ffn_in, w_ffn_out=w_ffn_out)
    mom = dict(mod_w=m_mod_w, mod_b=m_mod_b, g_pre_mix=m_g_pre_mix, g_post_mix=m_g_post_mix, g_pre_ffn=m_g_pre_ffn,
               g_post_ffn=m_g_post_ffn, w_in=m_w_in, sc_conv_w=m_sc_conv_w, ssm_conv_w=m_ssm_conv_w,
               ssm_conv_b=m_ssm_conv_b, ssm_dt_bias=m_ssm_dt_bias, ssm_a_log=m_ssm_a_log, ssm_d=m_ssm_d,
               ssm_norm_w=m_ssm_norm_w, w_sc_out=m_w_sc_out, w_sb_out=m_w_sb_out, w_ssm_out=m_w_ssm_out, w_o=m_w_o,
               w_ffn_in=m_w_ffn_in, w_ffn_out=m_w_ffn_out)
    var = dict(mod_w=v_mod_w, mod_b=v_mod_b, g_pre_mix=v_g_pre_mix, g_post_mix=v_g_post_mix, g_pre_ffn=v_g_pre_ffn,
               g_post_ffn=v_g_post_ffn, w_in=v_w_in, sc_conv_w=v_sc_conv_w, ssm_conv_w=v_ssm_conv_w,
               ssm_conv_b=v_ssm_conv_b, ssm_dt_bias=v_ssm_dt_bias, ssm_a_log=v_ssm_a_log, ssm_d=v_ssm_d,
               ssm_norm_w=v_ssm_norm_w, w_sc_out=v_w_sc_out, w_sb_out=v_w_sb_out, w_ssm_out=v_w_ssm_out, w_o=v_w_o,
               w_ffn_in=v_w_ffn_in, w_ffn_out=v_w_ffn_out)
    xi, yi, ci = _here()
    chip = 2 * xi + yi
    me = 4 * xi + 2 * yi + ci
    x0, target = x[0], loss_target[0]

    first_shapes = [(D_MODEL,), sc_conv_w.shape, ssm_conv_w.shape]
    g0 = _allgather_small("gather_cond", _pack([c, sc_conv_w, ssm_conv_w]))
    c_rows, sc_sh, ssm_sh = _unpack(g0, first_shapes)
    c_all = c_rows
    sc_w = jnp.concatenate([sc_sh[2 * j] for j in range(N_CHIPS)], axis=-1)
    ssm_w = jnp.concatenate([ssm_sh[2 * j] for j in range(N_CHIPS)], axis=-1)

    mod_b_sh = lax.dynamic_slice_in_dim(mod_b, chip * MOD_SHARD, MOD_SHARD, axis=1).reshape(DEPTH, 1, MOD_SHARD)
    modpart = _mod_fwd(c_all, mod_w, mod_b_sh)
    g1 = _allgather_small("gather_mod", modpart.reshape(-1, LANES)).reshape(N_DEV, DEPTH, N_DEV, MOD_SHARD)
    mod = jnp.concatenate([lax.dynamic_index_in_dim(g1[2 * j], me, axis=1, keepdims=False) for j in range(N_CHIPS)],
                          axis=-1)

    def layer_shards(l):
        return [wts[n][l].astype(BF16) for n, _ in BIG]

    def full_weights(which, gathered):
        W = {n: jnp.concatenate([g[j] for j in range(N_CHIPS)], axis=ax - 1) for (n, ax), g in zip(which, gathered)}
        if "w_in" in W:
            W["w_in"] = _pad_w_in(W["w_in"])
        if "w_ffn_in" in W:
            W["w_ffn_in"] = _ffn_in_cols(W["w_ffn_in"])
        return W

    Ws = [{}, {}]
    fwd_sides = [{"in_proj": (_gather_side(layer_shards(0)[1:]), lambda got: Ws[0].update(full_weights(BIG[1:], got))),
                  "ffn_in": (_gather_side(layer_shards(1)), lambda got: Ws[1].update(full_weights(BIG, got)))}, {}]
    Vs = []
    for l in range(DEPTH):
        sh1, sc1, gt1, sh2, sc2, gt2 = [_row(v) for v in jnp.split(mod[l], N_MOD)]
        Vs.append(dict(
            shift1=sh1, scale1=sc1, g_pre_mix=_row(g_pre_mix[l]),
            mid_mix=[gt1, _row(g_post_mix[l]), _row(g_pre_ffn[l]), sc2, sh2],
            gate2=gt2, g_post_ffn=_row(g_post_ffn[l]),
            sc_w=sc_w[l], ssm_w=ssm_w[l], ssm_b=_row(ssm_conv_b[l]),
            dtb=_row(jnp.pad(ssm_dt_bias[l], (0, LANES - SSM_HEADS))), alog=_row(jnp.pad(ssm_a_log[l], (0, LANES - SSM_HEADS))),
            d_full=_row(jnp.repeat(ssm_d[l], SSM_INNER // SSM_HEADS)), norm_w=_row(ssm_norm_w[l])))

    def mid_ffn_vecs(l):
        return [Vs[l]["gate2"], Vs[l]["g_post_ffn"], Vs[l + 1]["g_pre_mix"], Vs[l + 1]["scale1"], Vs[l + 1]["shift1"]]

    saved = []
    x_in = x0
    h, *got = _first_fwd(x0, [Vs[0]["g_pre_mix"], Vs[0]["scale1"], Vs[0]["shift1"]], _gather_side(layer_shards(0)[:1]))
    Ws[0].update(full_weights(BIG[:1], got))
    for l in range(DEPTH):
        S = _layer_fwd(l, x_in, h, Ws[l], Vs[l], fwd_sides[l], mid_ffn_vecs(l) if l + 1 < DEPTH else None)
        saved.append(S)
        if l + 1 < DEPTH:
            x_in, h = S["x_next"], S["h_next"]

    def pieces(G, names):
        out = []
        for n, ax in BIG:
            if n in names:
                g = _unpad_w_in(G[n]) if n == "w_in" else G[n]
                out.append(jnp.stack(jnp.split(g, N_CHIPS, axis=ax - 1)).astype(BF16))
        return out

    small_names = tuple(n for n, _ in BIG if n not in ("w_in", "w_ffn_in"))
    late_names = tuple(n for n, _ in BIG if n != "w_in")
    landed = {}

    GL = [None] * DEPTH
    S = saved[-1]
    dx1, df, g_gate2, g_gpf, loss_cols = _last_bwd(S["x1"], S["f"], target, [Vs[-1]["gate2"], Vs[-1]["g_post_ffn"]])
    for l in reversed(range(DEPTH)):
        sides = {}
        if l + 1 < DEPTH:
            for key, names in (("d_gu", small_names), ("d_h2", ("w_ffn_in",)), ("gw_ffn_in", ("w_in",))):
                sides[key] = (names, l + 1, lambda G, up=GL[l + 1], names=names: _scatter_side(pieces(up, names)))
        if l == 0:
            sides["d_h"] = (late_names, l, lambda G: _scatter_side(pieces(G, late_names)))
        dx, dh, G = _layer_bwd(l, saved[l], Ws[l], Vs[l], dx1, df, sides, landed)
        G["gate2"], G["g_post_ffn"] = g_gate2, g_gpf
        GL[l] = G
        if l > 0:
            Sp = saved[l - 1]
            dx1, df, g_gate2, g_gpf, G["g_pre_mix"], G["scale1"], G["shift1"] = _mid_bwd(
                f"mid_ffn_bwd{l - 1}", Sp["x1"], Sp["f"], dx, dh, mid_ffn_vecs(l - 1))
        else:
            grad_x, G["g_pre_mix"], G["scale1"], G["shift1"], landed[("w_in", 0)] = _first_bwd(
                x0, dx, dh, [Vs[0]["g_pre_mix"], Vs[0]["scale1"], Vs[0]["shift1"]], _scatter_side(pieces(G, ("w_in",))))
    loss = lax.psum(jnp.sum(loss_cols), ("x", "y", "c"))

    def both(key, shape=None):
        a = jnp.stack([GL[l][key] for l in range(DEPTH)])
        return a if shape is None else a.reshape(shape)

    dmod = jnp.concatenate([both(k, (DEPTH, D_MODEL)) for k in ("shift1", "scale1", "gate1", "shift2", "scale2", "gate2")],
                           axis=1)
    part_small = dict(
        mod_b=dmod, g_pre_mix=both("g_pre_mix", (DEPTH, D_MODEL)), g_post_mix=both("g_post_mix", (DEPTH, D_MODEL)),
        g_pre_ffn=both("g_pre_ffn", (DEPTH, D_MODEL)), g_post_ffn=both("g_post_ffn", (DEPTH, D_MODEL)),
        sc_conv_w=both("sc_conv_w"), ssm_conv_w=both("ssm_conv_w"), ssm_conv_b=both("ssm_conv_b", (DEPTH, SSM_CONV_DIM)),
        ssm_dt_bias=both("dtb", (DEPTH, LANES))[:, :SSM_HEADS], ssm_a_log=both("alog", (DEPTH, LANES))[:, :SSM_HEADS],
        ssm_d=both("d_full", (DEPTH, SSM_HEADS, SSM_INNER // SSM_HEADS)).sum(-1),
        ssm_norm_w=both("ssm_norm_w", (DEPTH, SSM_INNER)))
    small_shapes = [part_small[n].shape for n in SMALL]
    g2 = _allgather_small("gather_small_grads", _pack([part_small[n] for n in SMALL]))
    tot = dict(zip(SMALL, _unpack(_sum_slots("sum_small_grads", g2), small_shapes)))
    dmod_all = _unpack(g2, small_shapes)[0]
    dmod_sh = jnp.swapaxes(lax.dynamic_slice_in_dim(dmod_all, chip * MOD_SHARD, MOD_SHARD, axis=2), 0, 1)
    grads = {"mod_w": _mod_bwd(c_all, dmod_sh)}
    for n in SMALL:
        grads[n] = tot[n]
    grads["sc_conv_w"] = lax.dynamic_slice_in_dim(tot["sc_conv_w"], chip * 64, 64, axis=2)
    grads["ssm_conv_w"] = lax.dynamic_slice_in_dim(tot["ssm_conv_w"], chip * 192, 192, axis=2)

    keys = [(n, l) for n, _ in BIG for l in range(DEPTH)]
    mine = [_sum_slots(f"sum_{n}{l}", landed[(n, l)]) for n, l in keys]
    theirs = dict(zip(keys, _sibling_exchange("swap_core_sums", mine)))
    mine = dict(zip(keys, mine))

    out = {}

    def update(name, w2, gs, m2, v2, shape):
        g, d, nm, nv = _adamw(f"adamw_{name}", w2, gs, m2, v2)
        out[name] = tuple(a.reshape(shape) for a in (g, d, nm, nv))

    for n, _ in BIG:
        shp = wts[n].shape
        two = (-1, shp[-1])
        by_layer = [tuple(src[(n, l)] for l in range(DEPTH)) for src in (mine, theirs)]
        update(n, wts[n].reshape(two), by_layer, mom[n].reshape(two), var[n].reshape(two), shp)
    two = (-1, MOD_SHARD)
    update("mod_w", mod_w.reshape(two), [grads["mod_w"].reshape(two)], m_mod_w.reshape(two), v_mod_w.reshape(two), mod_w.shape)
    shapes = [wts[n].shape for n in SMALL]
    res = _adamw("adamw_small", _pack([wts[n] for n in SMALL]), [_pack([grads[n] for n in SMALL])],
                 _pack([mom[n] for n in SMALL]), _pack([var[n] for n in SMALL]))
    for n, g, d, nm, nv in zip(SMALL, *[_unpack(r, shapes) for r in res]):
        out[n] = (g, d, nm, nv)

    result = [loss, grad_x[None]]
    for k in range(4):
        result += [out[n][k] for n in WEIGHT_ORDER]
    return tuple(result)
```
